```python
import math
import jax, jax.numpy as jnp
from jax import lax
import numpy as np

D_MODEL = 1024
BATCH = 8
SEQ = 8192
DEPTH = 1

MEM_LEN = 256
MLA_HEADS = 8
MLA_Q_RANK = 256
MLA_KV_RANK = 128
MLA_NOPE = 64
MLA_ROPE = 32
MLA_V = 64
ROPE_THETA = 10000.0
SB_HEADS = 8
SB_HEAD_DIM = 64
X_HEADS = 4
X_HEAD_DIM = 128
D_FF = -(-8 * D_MODEL // (3 * 256)) * 256
Q_BLOCK = 128
EPS = 1e-6
SB_WIDTH = SB_HEADS * SB_HEAD_DIM
IN_SPLITS = (MLA_Q_RANK, MLA_KV_RANK, MLA_ROPE, SB_WIDTH, SB_WIDTH, SB_WIDTH, D_MODEL, D_MODEL)
D_IN = MLA_Q_RANK + MLA_KV_RANK + MLA_ROPE + 3 * SB_WIDTH + 2 * D_MODEL

kernel_name = "hybrid_mla_stickbreaking_gated_block"


def rms_norm(x, g):
    xf = x.astype(jnp.float32)
    y = xf * lax.rsqrt(jnp.mean(xf * xf, axis=-1, keepdims=True) + EPS)
    return (y * g.astype(jnp.float32)).astype(x.dtype)


def rope_cos_sin(positions):
    inv_freq = ROPE_THETA ** (-jnp.arange(0, MLA_ROPE, 2, dtype=jnp.float32) / MLA_ROPE)
    ang = positions.astype(jnp.float32)[..., None] * inv_freq
    return jnp.cos(ang), jnp.sin(ang)


def apply_rope(x, cos, sin):
    half = x.shape[-1] // 2
    x1 = x[..., :half].astype(jnp.float32)
    x2 = x[..., half:].astype(jnp.float32)
    out = jnp.concatenate([x1 * cos - x2 * sin, x2 * cos + x1 * sin], axis=-1)
    return out.astype(x.dtype)


def mla_block(qb, blk, k, v):
    scale = 1.0 / math.sqrt(MLA_NOPE + MLA_ROPE)
    s = jnp.einsum('bqhd,bkhd->bhqk', qb, k).astype(jnp.float32) * scale
    q_pos = blk * Q_BLOCK + jnp.arange(Q_BLOCK)
    k_pos = jnp.arange(k.shape[1])
    mask = k_pos[None, :] <= q_pos[:, None]
    s = jnp.where(mask, s, jnp.finfo(jnp.float32).min)
    p = jax.nn.softmax(s, axis=-1).astype(v.dtype)
    return jnp.einsum('bhqk,bkhd->bqhd', p, v)


def stick_breaking_block(qb, blk, k, v):
    scale = 1.0 / math.sqrt(SB_HEAD_DIM)
    z = jnp.einsum('bqhd,bkhd->bhqk', qb, k).astype(jnp.float32) * scale
    q_pos = blk * Q_BLOCK + jnp.arange(Q_BLOCK)
    k_pos = jnp.arange(k.shape[1])
    mask = k_pos[None, :] < q_pos[:, None]
    log_beta = jax.nn.log_sigmoid(z)
    log_one_minus = jnp.where(mask, jax.nn.log_sigmoid(-z), 0.0)
    suffix = lax.cumsum(log_one_minus, axis=3, reverse=True) - log_one_minus
    a = jnp.where(mask, jnp.exp(log_beta + suffix), 0.0).astype(v.dtype)
    return jnp.einsum('bhqk,bkhd->bqhd', a, v)


def blocked_attend(block_fn, q, k, v):
    b, s, h, dq = q.shape
    dv = v.shape[-1]
    nb = s // Q_BLOCK
    qb = q.reshape(b, nb, Q_BLOCK, h, dq).transpose(1, 0, 2, 3, 4)
    out = lax.map(lambda a: block_fn(a[0], a[1], k, v), (qb, jnp.arange(nb)))
    return out.transpose(1, 0, 2, 3, 4).reshape(b, s, h * dv)


def _fwd_setup_inputs(seed: int = 0) -> dict:
    key = jax.random.key(seed)
    ks = jax.random.split(key, 24)

    def w(k, shape, fan_in):
        return jax.random.normal(k, shape, jnp.float32) * (fan_in ** -0.5)

    def gain(k, shape):
        return 1.0 + 0.01 * jax.random.normal(k, shape, jnp.float32)

    L = DEPTH
    return {
        "x": jax.random.normal(ks[0], (BATCH, SEQ, D_MODEL), jnp.float32),
        "mem": jax.random.normal(ks[1], (BATCH, MEM_LEN, D_MODEL), jnp.float32),
        "positions": jnp.broadcast_to(jnp.arange(SEQ, dtype=jnp.int32)[None, :], (BATCH, SEQ)),
        "g_mix": gain(ks[2], (L, D_MODEL)),
        "w_in": w(ks[3], (L, D_MODEL, D_IN), D_MODEL),
        "b_gate": 0.01 * jax.random.normal(ks[4], (L, 2, D_MODEL), jnp.float32),
        "g_q_lat": gain(ks[5], (L, MLA_Q_RANK)),
        "w_uq": w(ks[6], (L, MLA_Q_RANK, MLA_HEADS * (MLA_NOPE + MLA_ROPE)), MLA_Q_RANK),
        "g_kv_lat": gain(ks[7], (L, MLA_KV_RANK)),
        "w_ukv": w(ks[8], (L, MLA_KV_RANK, MLA_HEADS * (MLA_NOPE + MLA_V)), MLA_KV_RANK),
        "w_a_proj": w(ks[9], (L, MLA_HEADS * MLA_V, D_MODEL), MLA_HEADS * MLA_V),
        "w_b_proj": w(ks[10], (L, SB_WIDTH, D_MODEL), SB_WIDTH),
        "w_o": w(ks[11], (L, D_MODEL, D_MODEL), D_MODEL),
        "g_x": gain(ks[12], (L, D_MODEL)),
        "g_mem": gain(ks[13], (L, D_MODEL)),
        "w_xq": w(ks[14], (L, D_MODEL, X_HEADS * X_HEAD_DIM), D_MODEL),
        "w_xkv": w(ks[15], (L, D_MODEL, 2 * X_HEADS * X_HEAD_DIM), D_MODEL),
        "w_xo": w(ks[16], (L, X_HEADS * X_HEAD_DIM, D_MODEL), X_HEADS * X_HEAD_DIM),
        "g_ffn": gain(ks[17], (L, D_MODEL)),
        "w_gate": w(ks[18], (L, D_MODEL, D_FF), D_MODEL),
        "w_up": w(ks[19], (L, D_MODEL, D_FF), D_MODEL),
        "w_down": w(ks[20], (L, D_FF, D_MODEL), D_FF),
        "g_final": gain(ks[21], (D_MODEL,)),
    }


def _fwd_reference(x, mem, positions, g_mix, w_in, b_gate, g_q_lat, w_uq, g_kv_lat, w_ukv,
              w_a_proj, w_b_proj, w_o, g_x, g_mem, w_xq, w_xkv, w_xo,
              g_ffn, w_gate, w_up, w_down, g_final):
    b, s, _ = x.shape
    m_len = mem.shape[1]
    cos, sin = rope_cos_sin(positions)
    split_points = list(np.cumsum(IN_SPLITS)[:-1])

    for l in range(DEPTH):
        h = rms_norm(x, g_mix[l])
        proj = h @ w_in[l]
        c_q, c_kv, k_r, sb_q, sb_k, sb_v, gate_a, gate_b = jnp.split(proj, split_points, axis=-1)

        q = (rms_norm(c_q, g_q_lat[l]) @ w_uq[l]).reshape(b, s, MLA_HEADS, MLA_NOPE + MLA_ROPE)
        q_rope = apply_rope(q[..., MLA_NOPE:], cos[:, :, None, :], sin[:, :, None, :])
        q_a = jnp.concatenate([q[..., :MLA_NOPE], q_rope], axis=-1)
        kv = (rms_norm(c_kv, g_kv_lat[l]) @ w_ukv[l]).reshape(b, s, MLA_HEADS, MLA_NOPE + MLA_V)
        k_r = apply_rope(k_r, cos, sin)
        k_a = jnp.concatenate(
            [kv[..., :MLA_NOPE], jnp.broadcast_to(k_r[:, :, None, :], (b, s, MLA_HEADS, MLA_ROPE))], axis=-1)
        v_a = kv[..., MLA_NOPE:]
        o_a = blocked_attend(mla_block, q_a, k_a, v_a)

        o_b = blocked_attend(
            stick_breaking_block,
            sb_q.reshape(b, s, SB_HEADS, SB_HEAD_DIM),
            sb_k.reshape(b, s, SB_HEADS, SB_HEAD_DIM),
            sb_v.reshape(b, s, SB_HEADS, SB_HEAD_DIM))

        merged = (jax.nn.sigmoid(gate_a + b_gate[l, 0]) * (o_a @ w_a_proj[l])
                  + jax.nn.sigmoid(gate_b + b_gate[l, 1]) * (o_b @ w_b_proj[l]))
        x = x + merged @ w_o[l]

        hx = rms_norm(x, g_x[l])
        mn = rms_norm(mem, g_mem[l])
        xq = (hx @ w_xq[l]).reshape(b, s, X_HEADS, X_HEAD_DIM)
        xkv = (mn @ w_xkv[l]).reshape(b, m_len, 2, X_HEADS, X_HEAD_DIM)
        xs = jnp.einsum('bqhd,bkhd->bhqk', xq, xkv[:, :, 0]).astype(jnp.float32) / math.sqrt(X_HEAD_DIM)
        xp = jax.nn.softmax(xs, axis=-1).astype(x.dtype)
        xo = jnp.einsum('bhqk,bkhd->bqhd', xp, xkv[:, :, 1]).reshape(b, s, X_HEADS * X_HEAD_DIM)
        x = x + xo @ w_xo[l]

        hf = rms_norm(x, g_ffn[l])
        x = x + (jax.nn.silu(hf @ w_gate[l]) * (hf @ w_up[l])) @ w_down[l]

    return rms_norm(x, g_final)


import jax as _jax
import jax.numpy as _jnp

TWIN_FORMAT = 'train_step'
FWD_PARAMS = ['x', 'mem', 'positions', 'g_mix', 'w_in', 'b_gate', 'g_q_lat', 'w_uq', 'g_kv_lat', 'w_ukv', 'w_a_proj', 'w_b_proj', 'w_o', 'g_x', 'g_mem', 'w_xq', 'w_xkv', 'w_xo', 'g_ffn', 'w_gate', 'w_up', 'w_down', 'g_final']
TWIN_WEIGHTS = ['g_mix', 'w_in', 'b_gate', 'g_q_lat', 'w_uq', 'g_kv_lat', 'w_ukv', 'w_a_proj', 'w_b_proj', 'w_o', 'g_x', 'g_mem', 'w_xq', 'w_xkv', 'w_xo', 'g_ffn', 'w_gate', 'w_up', 'w_down', 'g_final']
TWIN_DIFF_INPUT = 'x'
TWIN_INPUTS = ['x', 'mem', 'positions', 'g_mix', 'w_in', 'b_gate', 'g_q_lat', 'w_uq', 'g_kv_lat', 'w_ukv', 'w_a_proj', 'w_b_proj', 'w_o', 'g_x', 'g_mem', 'w_xq', 'w_xkv', 'w_xo', 'g_ffn', 'w_gate', 'w_up', 'w_down', 'g_final', 'loss_target', 'm_g_mix', 'm_w_in', 'm_b_gate', 'm_g_q_lat', 'm_w_uq', 'm_g_kv_lat', 'm_w_ukv', 'm_w_a_proj', 'm_w_b_proj', 'm_w_o', 'm_g_x', 'm_g_mem', 'm_w_xq', 'm_w_xkv', 'm_w_xo', 'm_g_ffn', 'm_w_gate', 'm_w_up', 'm_w_down', 'm_g_final', 'v_g_mix', 'v_w_in', 'v_b_gate', 'v_g_q_lat', 'v_w_uq', 'v_g_kv_lat', 'v_w_ukv', 'v_w_a_proj', 'v_w_b_proj', 'v_w_o', 'v_g_x', 'v_g_mem', 'v_w_xq', 'v_w_xkv', 'v_w_xo', 'v_g_ffn', 'v_w_gate', 'v_w_up', 'v_w_down', 'v_g_final']
TWIN_OUTPUTS = ['loss', 'grad_x', 'grad_g_mix', 'grad_w_in', 'grad_b_gate', 'grad_g_q_lat', 'grad_w_uq', 'grad_g_kv_lat', 'grad_w_ukv', 'grad_w_a_proj', 'grad_w_b_proj', 'grad_w_o', 'grad_g_x', 'grad_g_mem', 'grad_w_xq', 'grad_w_xkv', 'grad_w_xo', 'grad_g_ffn', 'grad_w_gate', 'grad_w_up', 'grad_w_down', 'grad_g_final', 'delta_g_mix', 'delta_w_in', 'delta_b_gate', 'delta_g_q_lat', 'delta_w_uq', 'delta_g_kv_lat', 'delta_w_ukv', 'delta_w_a_proj', 'delta_w_b_proj', 'delta_w_o', 'delta_g_x', 'delta_g_mem', 'delta_w_xq', 'delta_w_xkv', 'delta_w_xo', 'delta_g_ffn', 'delta_w_gate', 'delta_w_up', 'delta_w_down', 'delta_g_final', 'new_m_g_mix', 'new_m_w_in', 'new_m_b_gate', 'new_m_g_q_lat', 'new_m_w_uq', 'new_m_g_kv_lat', 'new_m_w_ukv', 'new_m_w_a_proj', 'new_m_w_b_proj', 'new_m_w_o', 'new_m_g_x', 'new_m_g_mem', 'new_m_w_xq', 'new_m_w_xkv', 'new_m_w_xo', 'new_m_g_ffn', 'new_m_w_gate', 'new_m_w_up', 'new_m_w_down', 'new_m_g_final', 'new_v_g_mix', 'new_v_w_in', 'new_v_b_gate', 'new_v_g_q_lat', 'new_v_w_uq', 'new_v_g_kv_lat', 'new_v_w_ukv', 'new_v_w_a_proj', 'new_v_w_b_proj', 'new_v_w_o', 'new_v_g_x', 'new_v_g_mem', 'new_v_w_xq', 'new_v_w_xkv', 'new_v_w_xo', 'new_v_g_ffn', 'new_v_w_gate', 'new_v_w_up', 'new_v_w_down', 'new_v_g_final']
TWIN_LEAF_KINDS = {'loss': 'loss', 'grad_x': 'grad_x', 'grad_g_mix': 'grad_w', 'grad_w_in': 'grad_w', 'grad_b_gate': 'grad_w', 'grad_g_q_lat': 'grad_w', 'grad_w_uq': 'grad_w', 'grad_g_kv_lat': 'grad_w', 'grad_w_ukv': 'grad_w', 'grad_w_a_proj': 'grad_w', 'grad_w_b_proj': 'grad_w', 'grad_w_o': 'grad_w', 'grad_g_x': 'grad_w', 'grad_g_mem': 'grad_w', 'grad_w_xq': 'grad_w', 'grad_w_xkv': 'grad_w', 'grad_w_xo': 'grad_w', 'grad_g_ffn': 'grad_w', 'grad_w_gate': 'grad_w', 'grad_w_up': 'grad_w', 'grad_w_down': 'grad_w', 'grad_g_final': 'grad_w', 'delta_g_mix': 'delta_w', 'delta_w_in': 'delta_w', 'delta_b_gate': 'delta_w', 'delta_g_q_lat': 'delta_w', 'delta_w_uq': 'delta_w', 'delta_g_kv_lat': 'delta_w', 'delta_w_ukv': 'delta_w', 'delta_w_a_proj': 'delta_w', 'delta_w_b_proj': 'delta_w', 'delta_w_o': 'delta_w', 'delta_g_x': 'delta_w', 'delta_g_mem': 'delta_w', 'delta_w_xq': 'delta_w', 'delta_w_xkv': 'delta_w', 'delta_w_xo': 'delta_w', 'delta_g_ffn': 'delta_w', 'delta_w_gate': 'delta_w', 'delta_w_up': 'delta_w', 'delta_w_down': 'delta_w', 'delta_g_final': 'delta_w', 'new_m_g_mix': 'new_m', 'new_m_w_in': 'new_m', 'new_m_b_gate': 'new_m', 'new_m_g_q_lat': 'new_m', 'new_m_w_uq': 'new_m', 'new_m_g_kv_lat': 'new_m', 'new_m_w_ukv': 'new_m', 'new_m_w_a_proj': 'new_m', 'new_m_w_b_proj': 'new_m', 'new_m_w_o': 'new_m', 'new_m_g_x': 'new_m', 'new_m_g_mem': 'new_m', 'new_m_w_xq': 'new_m', 'new_m_w_xkv': 'new_m', 'new_m_w_xo': 'new_m', 'new_m_g_ffn': 'new_m', 'new_m_w_gate': 'new_m', 'new_m_w_up': 'new_m', 'new_m_w_down': 'new_m', 'new_m_g_final': 'new_m', 'new_v_g_mix': 'new_v', 'new_v_w_in': 'new_v', 'new_v_b_gate': 'new_v', 'new_v_g_q_lat': 'new_v', 'new_v_w_uq': 'new_v', 'new_v_g_kv_lat': 'new_v', 'new_v_w_ukv': 'new_v', 'new_v_w_a_proj': 'new_v', 'new_v_w_b_proj': 'new_v', 'new_v_w_o': 'new_v', 'new_v_g_x': 'new_v', 'new_v_g_mem': 'new_v', 'new_v_w_xq': 'new_v', 'new_v_w_xkv': 'new_v', 'new_v_w_xo': 'new_v', 'new_v_g_ffn': 'new_v', 'new_v_w_gate': 'new_v', 'new_v_w_up': 'new_v', 'new_v_w_down': 'new_v', 'new_v_g_final': 'new_v'}


def _forward(args):
    return _fwd_reference(*[args[k] for k in FWD_PARAMS])


def _output_shape():
    def fwd():
        inp = _fwd_setup_inputs(0)
        return _fwd_reference(*[inp[k] for k in FWD_PARAMS])
    out = _jax.eval_shape(fwd)
    return out.shape, out.dtype

N_MICROBATCH = 1
ADAM_LR = 0.001
ADAM_B1 = 0.9
ADAM_B2 = 0.999
ADAM_EPS = 1e-08
ADAM_WD = 0.01
ADAM_STEP = 10
PER_EXAMPLE_BATCH_AXIS = {'x': 0, 'mem': 0, 'positions': 0, 'loss_target': 0}
SHARED_INPUTS = []
_WEIGHT_DTYPES = {'g_mix': _jnp.float32, 'w_in': _jnp.float32, 'b_gate': _jnp.float32, 'g_q_lat': _jnp.float32, 'w_uq': _jnp.float32, 'g_kv_lat': _jnp.float32, 'w_ukv': _jnp.float32, 'w_a_proj': _jnp.float32, 'w_b_proj': _jnp.float32, 'w_o': _jnp.float32, 'g_x': _jnp.float32, 'g_mem': _jnp.float32, 'w_xq': _jnp.float32, 'w_xkv': _jnp.float32, 'w_xo': _jnp.float32, 'g_ffn': _jnp.float32, 'w_gate': _jnp.float32, 'w_up': _jnp.float32, 'w_down': _jnp.float32, 'g_final': _jnp.float32}
MOMENT_SCALE = {'g_mix': 1.266524e-01, 'w_in': 6.320238e-02, 'b_gate': 2.764520e-02, 'g_q_lat': 5.732957e-02, 'w_uq': 3.017265e-02, 'g_kv_lat': 1.143065e-01, 'w_ukv': 3.840449e-02, 'w_a_proj': 3.135399e-02, 'w_b_proj': 9.197563e-02, 'w_o': 9.682644e-02, 'g_x': 2.884557e-02, 'g_mem': 4.234175e-02, 'w_xq': 3.895634e-02, 'w_xkv': 3.918247e-02, 'w_xo': 2.806991e-02, 'g_ffn': 1.758651e-01, 'w_gate': 7.664710e-02, 'w_up': 7.427028e-02, 'w_down': 1.229619e-01, 'g_final': 6.393937e+01}


def _to_microbatches(a, axis):
    t = _jnp.moveaxis(a, axis, 0)
    t = t.reshape((N_MICROBATCH, t.shape[0] // N_MICROBATCH) + t.shape[1:])
    return _jnp.moveaxis(t, 1, axis + 1)


def setup_inputs(seed: int = 0) -> dict:
    inp = _fwd_setup_inputs(seed)
    key = _jax.random.fold_in(_jax.random.key(seed), 7919)
    shape, _ = _output_shape()
    out = dict(inp)
    out["loss_target"] = _jax.random.normal(_jax.random.fold_in(key, 0), shape, _jnp.float32)
    for i, name in enumerate(TWIN_WEIGHTS):
        w = inp[name].astype(_jnp.float32)
        if MOMENT_SCALE is None:
            s = _jnp.sqrt(_jnp.mean(_jnp.square(w)) + 1e-30)
        else:
            s = MOMENT_SCALE[name]
        km, kv = _jax.random.split(_jax.random.fold_in(key, i + 1))
        out[name] = w
        out["m_" + name] = s * _jax.random.normal(km, w.shape, _jnp.float32)
        out["v_" + name] = (s * s) * _jax.random.uniform(kv, w.shape, _jnp.float32, 0.5, 1.5)
    if N_MICROBATCH > 1:
        for name, axis in PER_EXAMPLE_BATCH_AXIS.items():
            out[name] = _to_microbatches(out[name], axis)
    return {'x': out['x'], 'mem': out['mem'], 'positions': out['positions'], 'g_mix': out['g_mix'], 'w_in': out['w_in'], 'b_gate': out['b_gate'], 'g_q_lat': out['g_q_lat'], 'w_uq': out['w_uq'], 'g_kv_lat': out['g_kv_lat'], 'w_ukv': out['w_ukv'], 'w_a_proj': out['w_a_proj'], 'w_b_proj': out['w_b_proj'], 'w_o': out['w_o'], 'g_x': out['g_x'], 'g_mem': out['g_mem'], 'w_xq': out['w_xq'], 'w_xkv': out['w_xkv'], 'w_xo': out['w_xo'], 'g_ffn': out['g_ffn'], 'w_gate': out['w_gate'], 'w_up': out['w_up'], 'w_down': out['w_down'], 'g_final': out['g_final'], 'loss_target': out['loss_target'], 'm_g_mix': out['m_g_mix'], 'm_w_in': out['m_w_in'], 'm_b_gate': out['m_b_gate'], 'm_g_q_lat': out['m_g_q_lat'], 'm_w_uq': out['m_w_uq'], 'm_g_kv_lat': out['m_g_kv_lat'], 'm_w_ukv': out['m_w_ukv'], 'm_w_a_proj': out['m_w_a_proj'], 'm_w_b_proj': out['m_w_b_proj'], 'm_w_o': out['m_w_o'], 'm_g_x': out['m_g_x'], 'm_g_mem': out['m_g_mem'], 'm_w_xq': out['m_w_xq'], 'm_w_xkv': out['m_w_xkv'], 'm_w_xo': out['m_w_xo'], 'm_g_ffn': out['m_g_ffn'], 'm_w_gate': out['m_w_gate'], 'm_w_up': out['m_w_up'], 'm_w_down': out['m_w_down'], 'm_g_final': out['m_g_final'], 'v_g_mix': out['v_g_mix'], 'v_w_in': out['v_w_in'], 'v_b_gate': out['v_b_gate'], 'v_g_q_lat': out['v_g_q_lat'], 'v_w_uq': out['v_w_uq'], 'v_g_kv_lat': out['v_g_kv_lat'], 'v_w_ukv': out['v_w_ukv'], 'v_w_a_proj': out['v_w_a_proj'], 'v_w_b_proj': out['v_w_b_proj'], 'v_w_o': out['v_w_o'], 'v_g_x': out['v_g_x'], 'v_g_mem': out['v_g_mem'], 'v_w_xq': out['v_w_xq'], 'v_w_xkv': out['v_w_xkv'], 'v_w_xo': out['v_w_xo'], 'v_g_ffn': out['v_g_ffn'], 'v_w_gate': out['v_w_gate'], 'v_w_up': out['v_w_up'], 'v_w_down': out['v_w_down'], 'v_g_final': out['v_g_final']}


def _loss(weights, diff, rest, loss_target):
    with _jax.named_scope("forward"):
        args = {**rest, TWIN_DIFF_INPUT: diff, **{k: w.astype(_WEIGHT_DTYPES[k]) for k, w in weights.items()}}
        y = _forward(args)
    with _jax.named_scope("loss_head"):
        err = _jnp.square(y.astype(_jnp.float32) - loss_target)
        return 0.5 * _jnp.sum(_jnp.mean(err, axis=-1)) if err.ndim else 0.5 * err


def _adamw(w, g, m, v):
    m = ADAM_B1 * m + (1.0 - ADAM_B1) * g
    v = ADAM_B2 * v + (1.0 - ADAM_B2) * _jnp.square(g)
    m_hat = m / (1.0 - ADAM_B1 ** ADAM_STEP)
    v_hat = v / (1.0 - ADAM_B2 ** ADAM_STEP)
    delta = -ADAM_LR * (m_hat / (_jnp.sqrt(v_hat) + ADAM_EPS) + ADAM_WD * w)
    return delta, m, v


def reference(x, mem, positions, g_mix, w_in, b_gate, g_q_lat, w_uq, g_kv_lat, w_ukv, w_a_proj, w_b_proj, w_o, g_x, g_mem, w_xq, w_xkv, w_xo, g_ffn, w_gate, w_up, w_down, g_final, loss_target, m_g_mix, m_w_in, m_b_gate, m_g_q_lat, m_w_uq, m_g_kv_lat, m_w_ukv, m_w_a_proj, m_w_b_proj, m_w_o, m_g_x, m_g_mem, m_w_xq, m_w_xkv, m_w_xo, m_g_ffn, m_w_gate, m_w_up, m_w_down, m_g_final, v_g_mix, v_w_in, v_b_gate, v_g_q_lat, v_w_uq, v_g_kv_lat, v_w_ukv, v_w_a_proj, v_w_b_proj, v_w_o, v_g_x, v_g_mem, v_w_xq, v_w_xkv, v_w_xo, v_g_ffn, v_w_gate, v_w_up, v_w_down, v_g_final):
    given = dict(x=x, mem=mem, positions=positions, g_mix=g_mix, w_in=w_in, b_gate=b_gate, g_q_lat=g_q_lat, w_uq=w_uq, g_kv_lat=g_kv_lat, w_ukv=w_ukv, w_a_proj=w_a_proj, w_b_proj=w_b_proj, w_o=w_o, g_x=g_x, g_mem=g_mem, w_xq=w_xq, w_xkv=w_xkv, w_xo=w_xo, g_ffn=g_ffn, w_gate=w_gate, w_up=w_up, w_down=w_down, g_final=g_final, loss_target=loss_target, m_g_mix=m_g_mix, m_w_in=m_w_in, m_b_gate=m_b_gate, m_g_q_lat=m_g_q_lat, m_w_uq=m_w_uq, m_g_kv_lat=m_g_kv_lat, m_w_ukv=m_w_ukv, m_w_a_proj=m_w_a_proj, m_w_b_proj=m_w_b_proj, m_w_o=m_w_o, m_g_x=m_g_x, m_g_mem=m_g_mem, m_w_xq=m_w_xq, m_w_xkv=m_w_xkv, m_w_xo=m_w_xo, m_g_ffn=m_g_ffn, m_w_gate=m_w_gate, m_w_up=m_w_up, m_w_down=m_w_down, m_g_final=m_g_final, v_g_mix=v_g_mix, v_w_in=v_w_in, v_b_gate=v_b_gate, v_g_q_lat=v_g_q_lat, v_w_uq=v_w_uq, v_g_kv_lat=v_g_kv_lat, v_w_ukv=v_w_ukv, v_w_a_proj=v_w_a_proj, v_w_b_proj=v_w_b_proj, v_w_o=v_w_o, v_g_x=v_g_x, v_g_mem=v_g_mem, v_w_xq=v_w_xq, v_w_xkv=v_w_xkv, v_w_xo=v_w_xo, v_g_ffn=v_g_ffn, v_w_gate=v_w_gate, v_w_up=v_w_up, v_w_down=v_w_down, v_g_final=v_g_final)
    weights = {n: given[n] for n in TWIN_WEIGHTS}
    shared = {n: given[n] for n in SHARED_INPUTS}
    per_example = {n: given[n] for n in ['x', 'mem', 'positions']}
    grad_fn = _jax.value_and_grad(_loss, argnums=(0, 1))

    def one_microbatch(ex, loss_target):
        ex = dict(ex)
        diff = ex.pop(TWIN_DIFF_INPUT)
        return grad_fn(weights, diff, {**shared, **ex}, loss_target)

    if N_MICROBATCH == 1:
        loss, (grad_w, grad_x) = one_microbatch(per_example, given["loss_target"])
    else:
        def body(carry, xs):
            loss_sum, grad_sum = carry
            l_k, (gw_k, gx_k) = one_microbatch(xs[0], xs[1])
            with _jax.named_scope("update"):
                return (loss_sum + l_k, _jax.tree.map(_jnp.add, grad_sum, gw_k)), gx_k

        init = (_jnp.zeros((), _jnp.float32), _jax.tree.map(_jnp.zeros_like, weights))
        (loss, grad_w), grad_x = _jax.lax.scan(body, init, (per_example, given["loss_target"]))
    with _jax.named_scope("update"):
        delta_w, new_m, new_v = {}, {}, {}
        for n in TWIN_WEIGHTS:
            delta_w[n], new_m[n], new_v[n] = _adamw(weights[n], grad_w[n], given["m_" + n], given["v_" + n])
    return (loss, grad_x, *[grad_w[n] for n in TWIN_WEIGHTS], *[delta_w[n] for n in TWIN_WEIGHTS],
            *[new_m[n] for n in TWIN_WEIGHTS], *[new_v[n] for n in TWIN_WEIGHTS])
```

```python
import functools
import math

import jax
import jax.numpy as jnp
from jax import lax
from jax.experimental import pallas as pl
from jax.experimental.pallas import tpu as pltpu

F32 = jnp.float32
BF16 = jnp.bfloat16

D_MODEL = 1024
MLA_HEADS = 8
MLA_Q_RANK = 256
MLA_KV_RANK = 128
MLA_NOPE = 64
MLA_ROPE = 32
MLA_V = 64
ROPE_THETA = 10000.0
SB_WIDTH = 512
X_HEADS = 4
X_HEAD_DIM = 128
D_FF = 2816
EPS = 1e-6
D_IN = 4000
D_IN_PAD = 4096
K_R_OFF = 384
LANES = 128
HEAD_PAD = 128
MLA_SCALE = 1.0 / math.sqrt(MLA_NOPE + MLA_ROPE)
SB_SCALE = 0.125
X_SCALE = 1.0 / math.sqrt(X_HEAD_DIM)
NEG_BIG = -1e30

ADAM_LR = 0.001
ADAM_B1 = 0.9
ADAM_B2 = 0.999
ADAM_EPS = 1e-08
ADAM_WD = 0.01
ADAM_STEP = 10

N_DEV = 8
PACK_ALIGN = 2048
ADAM_BLOCK_ROWS = 1024
MIB = 1024 * 1024

SHARDED = (
    ("w_in", (D_MODEL, D_IN), 1),
    ("b_gate", (2, D_MODEL), 1),
    ("w_uq", (MLA_Q_RANK, 768), 1),
    ("w_ukv", (MLA_KV_RANK, 1024), 1),
    ("w_a_proj", (512, D_MODEL), 1),
    ("w_b_proj", (512, D_MODEL), 1),
    ("w_o", (D_MODEL, D_MODEL), 0),
    ("w_xq", (D_MODEL, 512), 0),
    ("w_xkv", (D_MODEL, 1024), 0),
    ("w_xo", (512, D_MODEL), 1),
    ("w_gate", (D_MODEL, D_FF), 1),
    ("w_up", (D_MODEL, D_FF), 1),
    ("w_down", (D_FF, D_MODEL), 0),
)
REPLICATED = (
    ("g_mix", 1024), ("g_q_lat", 256), ("g_kv_lat", 128), ("g_x", 1024),
    ("g_mem", 1024), ("g_ffn", 1024), ("g_final", 1024),
)
WEIGHT_ORDER = ("g_mix", "w_in", "b_gate", "g_q_lat", "w_uq", "g_kv_lat", "w_ukv", "w_a_proj",
                "w_b_proj", "w_o", "g_x", "g_mem", "w_xq", "w_xkv", "w_xo", "g_ffn", "w_gate",
                "w_up", "w_down", "g_final")


def _round_up(n, m):
    return -(-n // m) * m


def _cparams(vmem_mib=None, **kw):
    if vmem_mib is not None:
        kw["vmem_limit_bytes"] = vmem_mib * MIB
    return pltpu.CompilerParams(**kw)


def _dot(a, b):
    return jnp.dot(a, b, preferred_element_type=F32)


def _dot_nt(a, b):
    return lax.dot_general(a, b, (((1,), (1,)), ((), ())), preferred_element_type=F32)


def _dot_tn(a, b):
    return lax.dot_general(a, b, (((0,), (0,)), ((), ())), preferred_element_type=F32)


def _rms(x, g):
    r = lax.rsqrt(jnp.mean(x * x, axis=-1, keepdims=True) + EPS)
    xh = x * r
    return xh * g, xh, r


def _rms_bwd(dy, xh, r, g):
    u = dy * g
    dx = r * (u - xh * jnp.mean(u * xh, axis=-1, keepdims=True))
    return dx, dy * xh


def _sigmoid(z):
    return 1.0 / (1.0 + jnp.exp(-z))


def _acc_rows(ref, val, first):
    s = jnp.sum(val, axis=0, keepdims=True)

    @pl.when(first)
    def _():
        ref[...] = s

    @pl.when(jnp.logical_not(first))
    def _():
        ref[...] += s


def _acc(ref, val, first):
    @pl.when(first)
    def _():
        ref[...] = val

    @pl.when(jnp.logical_not(first))
    def _():
        ref[...] += val


def _peer(k):
    x, y, c = lax.axis_index("x"), lax.axis_index("y"), lax.axis_index("c")
    px = 1 - x if (k >> 2) & 1 else x
    py = 1 - y if (k >> 1) & 1 else y
    pc = 1 - c if k & 1 else c
    return (px, py, pc), 4 * px + 2 * py + pc


def _all_gather(shard):
    rows = shard.shape[0]

    def body(src, out, send_sems, recv_sems, local_sem):
        _, me = _peer(0)
        mine = pltpu.make_async_copy(src, out.at[me], local_sem)
        mine.start()
        sends = []
        for k in range(1, N_DEV):
            peer, _ = _peer(k)
            cp = pltpu.make_async_remote_copy(
                src_ref=src, dst_ref=out.at[me], send_sem=send_sems.at[k - 1],
                recv_sem=recv_sems.at[k - 1], device_id=peer, device_id_type=pl.DeviceIdType.MESH)
            cp.start()
            sends.append(cp)
        for k in range(1, N_DEV):
            peer, pid = _peer(k)
            pltpu.make_async_remote_copy(
                src_ref=src, dst_ref=out.at[pid], send_sem=send_sems.at[k - 1],
                recv_sem=recv_sems.at[k - 1], device_id=peer,
                device_id_type=pl.DeviceIdType.MESH).wait_recv()
        for cp in sends:
            cp.wait_send()
        mine.wait()

    return pl.pallas_call(
        body, name="weights_all_gather",
        out_shape=jax.ShapeDtypeStruct((N_DEV, rows, LANES), shard.dtype),
        in_specs=[pl.BlockSpec(memory_space=pl.ANY)],
        out_specs=pl.BlockSpec(memory_space=pl.ANY),
        scratch_shapes=[pltpu.SemaphoreType.DMA((N_DEV - 1,)), pltpu.SemaphoreType.DMA((N_DEV - 1,)),
                        pltpu.SemaphoreType.DMA(())],
    )(shard)


def _all_to_all(pieces):
    rows = pieces.shape[1]

    def body(src, out, send_sems, recv_sems, local_sem):
        _, me = _peer(0)
        mine = pltpu.make_async_copy(src.at[me], out.at[me], local_sem)
        mine.start()
        sends = []
        for k in range(1, N_DEV):
            peer, pid = _peer(k)
            cp = pltpu.make_async_remote_copy(
                src_ref=src.at[pid], dst_ref=out.at[me], send_sem=send_sems.at[k - 1],
                recv_sem=recv_sems.at[k - 1], device_id=peer, device_id_type=pl.DeviceIdType.MESH)
            cp.start()
            sends.append(cp)
        for k in range(1, N_DEV):
            peer, pid = _peer(k)
            pltpu.make_async_remote_copy(
                src_ref=src.at[pid], dst_ref=out.at[pid], send_sem=send_sems.at[k - 1],
                recv_sem=recv_sems.at[k - 1], device_id=peer,
                device_id_type=pl.DeviceIdType.MESH).wait_recv()
        for cp in sends:
            cp.wait_send()
        mine.wait()

    return pl.pallas_call(
        body, name="grads_all_to_all",
        out_shape=jax.ShapeDtypeStruct((N_DEV, rows, LANES), pieces.dtype),
        in_specs=[pl.BlockSpec(memory_space=pl.ANY)],
        out_specs=pl.BlockSpec(memory_space=pl.ANY),
        scratch_shapes=[pltpu.SemaphoreType.DMA((N_DEV - 1,)), pltpu.SemaphoreType.DMA((N_DEV - 1,)),
                        pltpu.SemaphoreType.DMA(())],
    )(pieces)


def _tn_matmul(a, b, name, tka=512, tn=1024, ts=512):
    s_len, ka = a.shape
    n = b.shape[1]
    tka, tn, ts = min(tka, ka), min(tn, n), min(ts, s_len)
    assert ka % tka == 0 and n % tn == 0 and s_len % ts == 0

    def body(a_ref, b_ref, o_ref):
        _acc(o_ref, _dot_tn(a_ref[...], b_ref[...]), pl.program_id(2) == 0)

    return pl.pallas_call(
        body, name=name, grid=(ka // tka, n // tn, s_len // ts),
        in_specs=[pl.BlockSpec((ts, tka), lambda i, j, s: (s, i)),
                  pl.BlockSpec((ts, tn), lambda i, j, s: (s, j))],
        out_specs=pl.BlockSpec((tka, tn), lambda i, j, s: (i, j)),
        out_shape=jax.ShapeDtypeStruct((ka, n), F32),
        compiler_params=_cparams(dimension_semantics=("parallel", "parallel", "arbitrary")),
    )(a, b)


def _row_block(s_len):
    return min(s_len, 512)


def _in_proj(x, g, w):
    s_len = x.shape[0]
    tm, tn = _row_block(s_len), 512

    def body(x_ref, g_ref, w_ref, h_ref, lat_ref, sb_ref, gate_ref, h_scr):
        j = pl.program_id(1)

        @pl.when(j == 0)
        def _():
            h, _, _ = _rms(x_ref[...], g_ref[...])
            hb = h.astype(BF16)
            h_scr[...] = hb
            h_ref[...] = hb

        p = _dot(h_scr[...], w_ref[...])

        @pl.when(j == 0)
        def _():
            lat_ref[...] = p

        @pl.when(j == 1)
        def _():
            sb_ref[...] = (p * SB_SCALE).astype(BF16)

        @pl.when((j == 2) | (j == 3))
        def _():
            sb_ref[...] = p.astype(BF16)

        @pl.when(j >= 4)
        def _():
            gate_ref[...] = p

    return pl.pallas_call(
        body, name="in_proj", grid=(s_len // tm, D_IN_PAD // tn),
        in_specs=[pl.BlockSpec((tm, D_MODEL), lambda i, j: (i, 0)),
                  pl.BlockSpec((1, D_MODEL), lambda i, j: (0, 0)),
                  pl.BlockSpec((D_MODEL, tn), lambda i, j: (0, j))],
        out_specs=[pl.BlockSpec((tm, D_MODEL), lambda i, j: (i, 0)),
                   pl.BlockSpec((tm, tn), lambda i, j: (i, 0)),
                   pl.BlockSpec((tm, tn), lambda i, j: (i, jnp.clip(j - 1, 0, 2))),
                   pl.BlockSpec((tm, tn), lambda i, j: (i, jnp.clip(j - 4, 0, 3)))],
        out_shape=[jax.ShapeDtypeStruct((s_len, D_MODEL), BF16),
                   jax.ShapeDtypeStruct((s_len, 512), F32),
                   jax.ShapeDtypeStruct((s_len, 3 * SB_WIDTH), BF16),
                   jax.ShapeDtypeStruct((s_len, 2 * D_MODEL), F32)],
        scratch_shapes=[pltpu.VMEM((tm, D_MODEL), BF16)],
        compiler_params=_cparams(dimension_semantics=("parallel", "arbitrary")),
    )(x, g, w)


def _rope_rot(blk, lane):
    return jnp.where(lane < 80, -pltpu.roll(blk, 112, 1), pltpu.roll(blk, 16, 1))


def _rope_rot_t(blk, lane):
    return jnp.where(lane < 80, pltpu.roll(blk, 112, 1), -pltpu.roll(blk, 16, 1))


def _mla_prep(lat, g_q, g_kv, w_uq, w_uk, w_uv, cosf, sinf):
    s_len = lat.shape[0]
    tm = _row_block(s_len)

    def body(lat_ref, gq_ref, gkv_ref, wuq_ref, wuk_ref, wuv_ref, cos_ref, sin_ref,
             q_ref, k_ref, v_ref, ql_ref, kvl_ref):
        lane = lax.broadcasted_iota(jnp.int32, (tm, LANES), 1)
        cosv, sinv = cos_ref[...], sin_ref[...]
        ql, _, _ = _rms(lat_ref[:, 0:256], gq_ref[...])
        kvl, _, _ = _rms(lat_ref[:, 256:384], gkv_ref[...])
        qlb, kvlb = ql.astype(BF16), kvl.astype(BF16)
        ql_ref[...] = qlb
        kvl_ref[...] = kvlb
        q = _dot(qlb, wuq_ref[...])
        kn = _dot(kvlb, wuk_ref[...])
        v_ref[...] = _dot(kvlb, wuv_ref[...]).astype(BF16)
        kr = pltpu.roll(lat_ref[:, K_R_OFF:K_R_OFF + LANES], 64, 1)
        kr = kr * cosv + _rope_rot(kr, lane) * sinv
        for h in range(MLA_HEADS):
            sl = slice(h * HEAD_PAD, (h + 1) * HEAD_PAD)
            blk = q[:, sl]
            q_ref[:, sl] = (blk * cosv + _rope_rot(blk, lane) * sinv).astype(BF16)
            k_ref[:, sl] = (kn[:, sl] + kr).astype(BF16)

    full = lambda shape: pl.BlockSpec(shape, lambda i: (0, 0))
    rowb = lambda n: pl.BlockSpec((tm, n), lambda i: (i, 0))
    return pl.pallas_call(
        body, name="mla_prep", grid=(s_len // tm,),
        in_specs=[rowb(512), full((1, 256)), full((1, 128)), full((256, 1024)), full((128, 1024)),
                  full((128, 512)), rowb(128), rowb(128)],
        out_specs=[rowb(1024), rowb(1024), rowb(512), rowb(256), rowb(128)],
        out_shape=[jax.ShapeDtypeStruct((s_len, 1024), BF16), jax.ShapeDtypeStruct((s_len, 1024), BF16),
                   jax.ShapeDtypeStruct((s_len, 512), BF16), jax.ShapeDtypeStruct((s_len, 256), BF16),
                   jax.ShapeDtypeStruct((s_len, 128), BF16)],
        compiler_params=_cparams(dimension_semantics=("parallel",)),
    )(lat, g_q, g_kv, w_uq, w_uk, w_uv, cosf, sinf)


def _attn_block(s_len):
    return min(s_len, 256)


def _mla_fwd(q, k, v):
    s_len = q.shape[0]
    tq = tk = _attn_block(s_len)

    def body(q_ref, k_ref, v_ref, o_ref, lse_ref):
        i = pl.program_id(1)
        lane = lax.broadcasted_iota(jnp.int32, (tq, LANES), 1)
        causal = (lax.broadcasted_iota(jnp.int32, (tq, tk), 1)
                  <= lax.broadcasted_iota(jnp.int32, (tq, tk), 0))
        outs, lses = [], []
        for hh in range(2):
            hs = slice(hh * HEAD_PAD, (hh + 1) * HEAD_PAD)
            qh = q_ref[:, hs]

            def step(kb, carry, masked, hs=hs, qh=qh):
                m, l, acc = carry
                rows = pl.ds(pl.multiple_of(kb * tk, tk), tk)
                s = _dot_nt(qh, k_ref[rows, hs]) * MLA_SCALE
                if masked:
                    s = jnp.where(causal, s, NEG_BIG)
                m_new = jnp.maximum(m, jnp.max(s, axis=-1, keepdims=True))
                alpha = jnp.exp(m - m_new)
                p = jnp.exp(s - m_new)
                l = alpha * l + jnp.sum(p, axis=-1, keepdims=True)
                acc = alpha * acc + _dot(p.astype(BF16), v_ref[rows, :])
                return m_new, l, acc

            carry = (jnp.full((tq, 1), NEG_BIG, F32), jnp.zeros((tq, 1), F32), jnp.zeros((tq, LANES), F32))
            carry = lax.fori_loop(0, i, functools.partial(step, masked=False), carry)
            m, l, acc = step(i, carry, True)
            outs.append(acc / l)
            lses.append(jnp.broadcast_to(m + jnp.log(l), (tq, LANES)))
        o_ref[...] = jnp.where(lane < 64, outs[0], outs[1]).astype(BF16)
        lse_ref[...] = jnp.where(lane < 64, lses[0], lses[1])

    return pl.pallas_call(
        body, name="mla_fwd", grid=(4, s_len // tq),
        in_specs=[pl.BlockSpec((tq, 2 * HEAD_PAD), lambda p, i: (i, p)),
                  pl.BlockSpec((s_len, 2 * HEAD_PAD), lambda p, i: (0, p)),
                  pl.BlockSpec((s_len, LANES), lambda p, i: (0, p))],
        out_specs=[pl.BlockSpec((tq, LANES), lambda p, i: (i, p)),
                   pl.BlockSpec((None, tq, LANES), lambda p, i: (p, i, 0))],
        out_shape=[jax.ShapeDtypeStruct((s_len, 512), BF16), jax.ShapeDtypeStruct((4, s_len, LANES), F32)],
        compiler_params=_cparams(40, dimension_semantics=("parallel", "arbitrary")),
    )(q, k, v)


def _mla_bwd(q, k, v, o, do, lse):
    s_len = q.shape[0]
    tq = tk = _attn_block(s_len)

    def body(q_ref, k_ref, v_ref, o_ref, do_ref, lse_ref, dq_ref, dk_ref, dv_ref):
        h, i = pl.program_id(0), pl.program_id(1)
        lane = lax.broadcasted_iota(jnp.int32, (tq, LANES), 1)
        causal = (lax.broadcasted_iota(jnp.int32, (tq, tk), 1)
                  <= lax.broadcasted_iota(jnp.int32, (tq, tk), 0))
        hmask = (lane // 64) == (h % 2)

        @pl.when(i == 0)
        def _():
            dk_ref[...] = jnp.zeros_like(dk_ref)

        @pl.when((i == 0) & (h % 2 == 0))
        def _():
            dv_ref[...] = jnp.zeros_like(dv_ref)

        qh = q_ref[...]
        doh = jnp.where(hmask, do_ref[...], jnp.zeros((), BF16))
        delta = jnp.sum(doh.astype(F32) * o_ref[...].astype(F32), axis=-1, keepdims=True)
        lse_h = jnp.sum(jnp.where(lane == 64 * (h % 2), lse_ref[...], 0.0), axis=-1, keepdims=True)

        def step(kb, dq, masked):
            rows = pl.ds(pl.multiple_of(kb * tk, tk), tk)
            kblk = k_ref[rows, :]
            s = _dot_nt(qh, kblk) * MLA_SCALE
            if masked:
                s = jnp.where(causal, s, NEG_BIG)
            p = jnp.exp(s - lse_h)
            dp = _dot_nt(doh, v_ref[rows, :])
            ds = (p * (dp - delta) * MLA_SCALE).astype(BF16)
            dv_ref[rows, :] += _dot_tn(p.astype(BF16), doh)
            dk_ref[rows, :] += _dot_tn(ds, qh)
            return dq + _dot(ds, kblk)

        dq = lax.fori_loop(0, i, functools.partial(step, masked=False), jnp.zeros((tq, LANES), F32))
        dq_ref[...] = step(i, dq, True)

    return pl.pallas_call(
        body, name="mla_bwd", grid=(MLA_HEADS, s_len // tq),
        in_specs=[pl.BlockSpec((tq, HEAD_PAD), lambda h, i: (i, h)),
                  pl.BlockSpec((s_len, HEAD_PAD), lambda h, i: (0, h)),
                  pl.BlockSpec((s_len, LANES), lambda h, i: (0, h // 2)),
                  pl.BlockSpec((tq, LANES), lambda h, i: (i, h // 2)),
                  pl.BlockSpec((tq, LANES), lambda h, i: (i, h // 2)),
                  pl.BlockSpec((None, tq, LANES), lambda h, i: (h // 2, i, 0))],
        out_specs=[pl.BlockSpec((tq, HEAD_PAD), lambda h, i: (i, h)),
                   pl.BlockSpec((s_len, HEAD_PAD), lambda h, i: (0, h)),
                   pl.BlockSpec((s_len, LANES), lambda h, i: (0, h // 2))],
        out_shape=[jax.ShapeDtypeStruct((s_len, 1024), F32), jax.ShapeDtypeStruct((s_len, 1024), F32),
                   jax.ShapeDtypeStruct((s_len, 512), F32)],
        compiler_params=_cparams(48, dimension_semantics=("arbitrary", "arbitrary")),
    )(q, k, v, o, do, lse)


def _log_sigmoids(z):
    sp = jnp.log(1.0 + jnp.exp(-jnp.abs(z)))
    return jnp.minimum(z, 0.0) - sp, jnp.minimum(-z, 0.0) - sp


def _split_dot(x, w, parts, nt=False):
    dot = _dot_nt if nt else _dot
    out = None
    for _ in range(parts):
        xb = x.astype(BF16)
        t = dot(xb, w)
        out = t if out is None else out + t
        x = x - xb.astype(F32)
    return out


def _sb_fwd(sb):
    s_len = sb.shape[0]
    tq = tk = _attn_block(s_len)
    assert s_len // tk <= 64

    def body(q_ref, k_ref, v_ref, o_ref, r_ref):
        i = pl.program_id(1)
        lane = lax.broadcasted_iota(jnp.int32, (tq, LANES), 1)
        strict = (lax.broadcasted_iota(jnp.int32, (tq, tk), 1)
                  < lax.broadcasted_iota(jnp.int32, (tq, tk), 0))
        upper = (lax.broadcasted_iota(jnp.int32, (tk, tk), 0)
                 > lax.broadcasted_iota(jnp.int32, (tk, tk), 1)).astype(BF16)
        qp = q_ref[...]
        r_acc = jnp.zeros((tq, LANES), F32)
        outs = []
        for hh in range(2):
            qh = jnp.where((lane // 64) == hh, qp, jnp.zeros((), BF16))

            def step(kb, carry, masked, hh=hh, qh=qh):
                c, acc, r = carry
                rows = pl.ds(pl.multiple_of(kb * tk, tk), tk)
                z = _dot_nt(qh, k_ref[rows, :])
                lb, lom = _log_sigmoids(z)
                if masked:
                    lom = jnp.where(strict, lom, 0.0)
                suf = _split_dot(lom, upper, 2)
                a = jnp.exp(lb + suf + c)
                if masked:
                    a = jnp.where(strict, a, 0.0)
                acc = acc + _dot(a.astype(BF16), v_ref[rows, :])
                rs = suf[:, 0:1] + lom[:, 0:1]
                r = jnp.where(lane == 64 * hh + kb, rs, r)
                return c + rs, acc, r

            carry = step(i, (jnp.zeros((tq, 1), F32), jnp.zeros((tq, LANES), F32), r_acc), True)
            carry = lax.fori_loop(0, i, lambda t, cy, step=step: step(i - 1 - t, cy, False), carry)
            _, acc, r_acc = carry
            outs.append(acc)
        o_ref[...] = jnp.where(lane < 64, outs[0], outs[1]).astype(BF16)
        r_ref[...] = r_acc

    return pl.pallas_call(
        body, name="sb_fwd", grid=(4, s_len // tq),
        in_specs=[pl.BlockSpec((tq, LANES), lambda p, i: (i, p)),
                  pl.BlockSpec((s_len, LANES), lambda p, i: (0, 4 + p)),
                  pl.BlockSpec((s_len, LANES), lambda p, i: (0, 8 + p))],
        out_specs=[pl.BlockSpec((tq, LANES), lambda p, i: (i, p)),
                   pl.BlockSpec((None, tq, LANES), lambda p, i: (p, i, 0))],
        out_shape=[jax.ShapeDtypeStruct((s_len, 512), BF16), jax.ShapeDtypeStruct((4, s_len, LANES), F32)],
        compiler_params=_cparams(40, dimension_semantics=("parallel", "arbitrary")),
    )(sb, sb, sb)


def _sb_bwd(sb, do, r):
    s_len = sb.shape[0]
    tq = tk = _attn_block(s_len)

    def body(q_ref, k_ref, v_ref, do_ref, r_ref, dq_ref, dk_ref, dv_ref):
        i = pl.program_id(1)
        lane = lax.broadcasted_iota(jnp.int32, (tq, LANES), 1)
        strict = (lax.broadcasted_iota(jnp.int32, (tq, tk), 1)
                  < lax.broadcasted_iota(jnp.int32, (tq, tk), 0))
        rowi = lax.broadcasted_iota(jnp.int32, (tk, tk), 0)
        coli = lax.broadcasted_iota(jnp.int32, (tk, tk), 1)
        upper = (rowi > coli).astype(BF16)
        tri = (lax.broadcasted_iota(jnp.int32, (LANES, LANES), 0)
               > lax.broadcasted_iota(jnp.int32, (LANES, LANES), 1)).astype(BF16)

        @pl.when(i == 0)
        def _():
            dk_ref[...] = jnp.zeros_like(dk_ref)
            dv_ref[...] = jnp.zeros_like(dv_ref)

        qp, dop, rp = q_ref[...], do_ref[...], r_ref[...]
        dqs = []
        for hh in range(2):
            hmask = (lane // 64) == hh
            qh = jnp.where(hmask, qp, jnp.zeros((), BF16))
            doh = jnp.where(hmask, dop, jnp.zeros((), BF16))
            right = _split_dot(jnp.where(hmask, rp, 0.0), tri, 3)

            def step(kb, carry, masked, hh=hh, qh=qh, doh=doh, right=right):
                pre, dq = carry
                rows = pl.ds(pl.multiple_of(kb * tk, tk), tk)
                kblk, vblk = k_ref[rows, :], v_ref[rows, :]
                c = jnp.sum(jnp.where(lane == 64 * hh + kb, right, 0.0), axis=-1, keepdims=True)
                z = _dot_nt(qh, kblk)
                lb, lom = _log_sigmoids(z)
                if masked:
                    lom = jnp.where(strict, lom, 0.0)
                suf = _split_dot(lom, upper, 2)
                a = jnp.exp(lb + suf + c)
                if masked:
                    a = jnp.where(strict, a, 0.0)
                g = a * _dot_nt(doh, vblk)
                dv_ref[rows, :] += _dot_tn(a.astype(BF16), doh)
                left = _split_dot(g, upper, 1, nt=True) + pre
                sig = jnp.exp(lb)
                dz = g * (1.0 - sig) - sig * left
                if masked:
                    dz = jnp.where(strict, dz, 0.0)
                dzb = dz.astype(BF16)
                dk_ref[rows, :] += _dot_tn(dzb, qh)
                pre = left[:, tk - 1:tk] + g[:, tk - 1:tk]
                return pre, dq + _dot(dzb, kblk)

            carry = (jnp.zeros((tq, 1), F32), jnp.zeros((tq, LANES), F32))
            carry = lax.fori_loop(0, i, functools.partial(step, masked=False), carry)
            _, dq = step(i, carry, True)
            dqs.append(dq)
        dq_ref[...] = jnp.where(lane < 64, dqs[0], dqs[1]) * SB_SCALE

    return pl.pallas_call(
        body, name="sb_bwd", grid=(4, s_len // tq),
        in_specs=[pl.BlockSpec((tq, LANES), lambda p, i: (i, p)),
                  pl.BlockSpec((s_len, LANES), lambda p, i: (0, 4 + p)),
                  pl.BlockSpec((s_len, LANES), lambda p, i: (0, 8 + p)),
                  pl.BlockSpec((tq, LANES), lambda p, i: (i, p)),
                  pl.BlockSpec((None, tq, LANES), lambda p, i: (p, i, 0))],
        out_specs=[pl.BlockSpec((tq, LANES), lambda p, i: (i, p)),
                   pl.BlockSpec((s_len, LANES), lambda p, i: (0, p)),
                   pl.BlockSpec((s_len, LANES), lambda p, i: (0, p))],
        out_shape=[jax.ShapeDtypeStruct((s_len, 512), F32)] * 3,
        compiler_params=_cparams(48, dimension_semantics=("arbitrary", "arbitrary")),
    )(sb, sb, sb, do, r)


def _merge_fwd(x, oa, ob, gates, bg, wa, wb, wo):
    s_len = x.shape[0]
    tm = _row_block(s_len)

    def body(x_ref, oa_ref, ob_ref, g_ref, bg_ref, wa_ref, wb_ref, wo_ref, y_ref):
        pa = _dot(oa_ref[...], wa_ref[...])
        pb = _dot(ob_ref[...], wb_ref[...])
        merged = (_sigmoid(g_ref[:, 0:D_MODEL] + bg_ref[0:1, :]) * pa
                  + _sigmoid(g_ref[:, D_MODEL:2 * D_MODEL] + bg_ref[1:2, :]) * pb)
        y_ref[...] = x_ref[...] + _dot(merged.astype(BF16), wo_ref[...])

    full = lambda shape: pl.BlockSpec(shape, lambda i: (0, 0))
    rowb = lambda n: pl.BlockSpec((tm, n), lambda i: (i, 0))
    return pl.pallas_call(
        body, name="merge_fwd", grid=(s_len // tm,),
        in_specs=[rowb(1024), rowb(512), rowb(512), rowb(2048), full((2, 1024)), full((512, 1024)),
                  full((512, 1024)), full((1024, 1024))],
        out_specs=rowb(1024),
        out_shape=jax.ShapeDtypeStruct((s_len, D_MODEL), F32),
        compiler_params=_cparams(48, dimension_semantics=("parallel",)),
    )(x, oa, ob, gates, bg, wa, wb, wo)


def _merge_bwd(dx1, oa, ob, gates, bg, wa, wb, wo):
    s_len = dx1.shape[0]
    tm = _row_block(s_len)

    def body(dx_ref, oa_ref, ob_ref, g_ref, bg_ref, wa_ref, wb_ref, wo_ref,
             doa_ref, dob_ref, dgate_ref, dpa_ref, dpb_ref, merged_ref, dxb_ref, dbg_ref):
        first = pl.program_id(0) == 0
        dxb = dx_ref[...].astype(BF16)
        dxb_ref[...] = dxb
        pa = _dot(oa_ref[...], wa_ref[...])
        pb = _dot(ob_ref[...], wb_ref[...])
        sa = _sigmoid(g_ref[:, 0:D_MODEL] + bg_ref[0:1, :])
        sbg = _sigmoid(g_ref[:, D_MODEL:2 * D_MODEL] + bg_ref[1:2, :])
        merged_ref[...] = (sa * pa + sbg * pb).astype(BF16)
        dm = _dot_nt(dxb, wo_ref[...])
        dpa = (dm * sa).astype(BF16)
        dpb = (dm * sbg).astype(BF16)
        dpa_ref[...] = dpa
        dpb_ref[...] = dpb
        dga = dm * pa * sa * (1.0 - sa)
        dgb = dm * pb * sbg * (1.0 - sbg)
        dgate_ref[:, 0:D_MODEL] = dga.astype(BF16)
        dgate_ref[:, D_MODEL:2 * D_MODEL] = dgb.astype(BF16)
        _acc_rows(dbg_ref.at[0:1, :], dga, first)
        _acc_rows(dbg_ref.at[1:2, :], dgb, first)
        doa_ref[...] = _dot_nt(dpa, wa_ref[...]).astype(BF16)
        dob_ref[...] = _dot_nt(dpb, wb_ref[...]).astype(BF16)

    full = lambda shape: pl.BlockSpec(shape, lambda i: (0, 0))
    rowb = lambda n: pl.BlockSpec((tm, n), lambda i: (i, 0))
    sds = lambda n, dt: jax.ShapeDtypeStruct((s_len, n), dt)
    return pl.pallas_call(
        body, name="merge_bwd", grid=(s_len // tm,),
        in_specs=[rowb(1024), rowb(512), rowb(512), rowb(2048), full((2, 1024)), full((512, 1024)),
                  full((512, 1024)), full((1024, 1024))],
        out_specs=[rowb(512), rowb(512), rowb(2048), rowb(1024), rowb(1024), rowb(1024), rowb(1024),
                   full((2, 1024))],
        out_shape=[sds(512, BF16), sds(512, BF16), sds(2048, BF16), sds(1024, BF16), sds(1024, BF16),
                   sds(1024, BF16), sds(1024, BF16), jax.ShapeDtypeStruct((2, 1024), F32)],
        compiler_params=_cparams(48, dimension_semantics=("arbitrary",)),
    )(dx1, oa, ob, gates, bg, wa, wb, wo)


def _mem_kv(mem, g, w):
    m_len = mem.shape[0]

    def body(mem_ref, g_ref, w_ref, mn_ref, kv_ref):
        mn, _, _ = _rms(mem_ref[...], g_ref[...])
        mnb = mn.astype(BF16)
        mn_ref[...] = mnb
        kv_ref[...] = _dot(mnb, w_ref[...]).astype(BF16)

    return pl.pallas_call(
        body, name="mem_kv",
        out_shape=[jax.ShapeDtypeStruct((m_len, D_MODEL), BF16), jax.ShapeDtypeStruct((m_len, 1024), BF16)],
    )(mem, g, w)


def _mem_bwd(mem, g, w, mn, dkv):
    def body(mem_ref, g_ref, w_ref, mn_ref, dkv_ref, dw_ref, dg_ref):
        dkvb = dkv_ref[...].astype(BF16)
        dw_ref[...] = _dot_tn(mn_ref[...], dkvb)
        dmn = _dot_nt(dkvb, w_ref[...])
        _, xh, _ = _rms(mem_ref[...], g_ref[...])
        dg_ref[...] = jnp.sum(dmn * xh, axis=0, keepdims=True)

    return pl.pallas_call(
        body, name="mem_bwd",
        out_shape=[jax.ShapeDtypeStruct((D_MODEL, 1024), F32), jax.ShapeDtypeStruct((1, D_MODEL), F32)],
    )(mem, g, w, mn, dkv)


def _xattn_heads(xqb, kv_ref, m_len):
    ps = []
    for h in range(X_HEADS):
        hs = slice(h * X_HEAD_DIM, (h + 1) * X_HEAD_DIM)
        s = _dot_nt(xqb[:, hs], kv_ref[:, hs]) * X_SCALE
        e = jnp.exp(s - jnp.max(s, axis=-1, keepdims=True))
        ps.append(e / jnp.sum(e, axis=-1, keepdims=True))
    return ps


def _xattn_fwd(x1, g, wxq, kv, wxo):
    s_len, m_len = x1.shape[0], kv.shape[0]
    tm = _row_block(s_len)

    def body(x_ref, g_ref, wq_ref, kv_ref, wo_ref, y_ref):
        hx, _, _ = _rms(x_ref[...], g_ref[...])
        xqb = _dot(hx.astype(BF16), wq_ref[...]).astype(BF16)
        ps = _xattn_heads(xqb, kv_ref, m_len)
        xo = jnp.concatenate(
            [_dot(ps[h].astype(BF16), kv_ref[:, 512 + h * X_HEAD_DIM:512 + (h + 1) * X_HEAD_DIM])
             for h in range(X_HEADS)], axis=-1)
        y_ref[...] = x_ref[...] + _dot(xo.astype(BF16), wo_ref[...])

    full = lambda shape: pl.BlockSpec(shape, lambda i: (0, 0))
    rowb = lambda n: pl.BlockSpec((tm, n), lambda i: (i, 0))
    return pl.pallas_call(
        body, name="xattn_fwd", grid=(s_len // tm,),
        in_specs=[rowb(1024), full((1, 1024)), full((1024, 512)), full((m_len, 1024)), full((512, 1024))],
        out_specs=rowb(1024),
        out_shape=jax.ShapeDtypeStruct((s_len, D_MODEL), F32),
        compiler_params=_cparams(48, dimension_semantics=("parallel",)),
    )(x1, g, wxq, kv, wxo)


def _xattn_bwd(x1, dx2, g, wxq, kv, wxo):
    s_len, m_len = x1.shape[0], kv.shape[0]
    tm = _row_block(s_len)

    def body(x_ref, dy_ref, g_ref, wq_ref, kv_ref, wo_ref, dx_ref, dwq_ref, dwo_ref, dkv_ref, dg_ref):
        first = pl.program_id(0) == 0
        gv = g_ref[...]
        hx, xh, r = _rms(x_ref[...], gv)
        hxb = hx.astype(BF16)
        xqb = _dot(hxb, wq_ref[...]).astype(BF16)
        ps = _xattn_heads(xqb, kv_ref, m_len)
        dy = dy_ref[...]
        dyb = dy.astype(BF16)
        dxo = _dot_nt(dyb, wo_ref[...])
        xos, dqs, dks, dvs = [], [], [], []
        for h in range(X_HEADS):
            hs = slice(h * X_HEAD_DIM, (h + 1) * X_HEAD_DIM)
            vs = slice(512 + h * X_HEAD_DIM, 512 + (h + 1) * X_HEAD_DIM)
            p = ps[h]
            pb = p.astype(BF16)
            dxoh = dxo[:, hs].astype(BF16)
            xos.append(_dot(pb, kv_ref[:, vs]))
            dp = _dot_nt(dxoh, kv_ref[:, vs])
            ds = (p * (dp - jnp.sum(dp * p, axis=-1, keepdims=True)) * X_SCALE).astype(BF16)
            dvs.append(_dot_tn(pb, dxoh))
            dks.append(_dot_tn(ds, xqb[:, hs]))
            dqs.append(_dot(ds, kv_ref[:, hs]))
        xob = jnp.concatenate(xos, axis=-1).astype(BF16)
        dxqb = jnp.concatenate(dqs, axis=-1).astype(BF16)
        _acc(dwo_ref, _dot_tn(xob, dyb), first)
        _acc(dwq_ref, _dot_tn(hxb, dxqb), first)
        _acc(dkv_ref, jnp.concatenate(dks + dvs, axis=-1), first)
        dhx = _dot_nt(dxqb, wq_ref[...])
        dx, dgr = _rms_bwd(dhx, xh, r, gv)
        dx_ref[...] = dy + dx
        _acc_rows(dg_ref, dgr, first)

    full = lambda shape: pl.BlockSpec(shape, lambda i: (0, 0))
    rowb = lambda n: pl.BlockSpec((tm, n), lambda i: (i, 0))
    return pl.pallas_call(
        body, name="xattn_bwd", grid=(s_len // tm,),
        in_specs=[rowb(1024), rowb(1024), full((1, 1024)), full((1024, 512)), full((m_len, 1024)),
                  full((512, 1024))],
        out_specs=[rowb(1024), full((1024, 512)), full((512, 1024)), full((m_len, 1024)), full((1, 1024))],
        out_shape=[jax.ShapeDtypeStruct((s_len, D_MODEL), F32), jax.ShapeDtypeStruct((1024, 512), F32),
                   jax.ShapeDtypeStruct((512, 1024), F32), jax.ShapeDtypeStruct((m_len, 1024), F32),
                   jax.ShapeDtypeStruct((1, D_MODEL), F32)],
        compiler_params=_cparams(48, dimension_semantics=("arbitrary",)),
    )(x1, dx2, g, wxq, kv, wxo)


FF_TILE = 1408
FF_TILE_BWD = 256


def _ffn_fwd(x2, g, wg, wu, wd):
    s_len = x2.shape[0]
    tm, tf = _row_block(s_len), FF_TILE

    def body(x_ref, g_ref, wg_ref, wu_ref, wd_ref, y_ref, h_scr):
        j = pl.program_id(1)

        @pl.when(j == 0)
        def _():
            hf, _, _ = _rms(x_ref[...], g_ref[...])
            h_scr[...] = hf.astype(BF16)
            y_ref[...] = x_ref[...]

        hb = h_scr[...]
        gt = _dot(hb, wg_ref[...])
        up = _dot(hb, wu_ref[...])
        act = gt * _sigmoid(gt) * up
        y_ref[...] += _dot(act.astype(BF16), wd_ref[...])

    return pl.pallas_call(
        body, name="ffn_fwd", grid=(s_len // tm, D_FF // tf),
        in_specs=[pl.BlockSpec((tm, D_MODEL), lambda i, j: (i, 0)),
                  pl.BlockSpec((1, D_MODEL), lambda i, j: (0, 0)),
                  pl.BlockSpec((D_MODEL, tf), lambda i, j: (0, j)),
                  pl.BlockSpec((D_MODEL, tf), lambda i, j: (0, j)),
                  pl.BlockSpec((tf, D_MODEL), lambda i, j: (j, 0))],
        out_specs=pl.BlockSpec((tm, D_MODEL), lambda i, j: (i, 0)),
        out_shape=jax.ShapeDtypeStruct((s_len, D_MODEL), F32),
        scratch_shapes=[pltpu.VMEM((tm, D_MODEL), BF16)],
        compiler_params=_cparams(48, dimension_semantics=("parallel", "arbitrary")),
    )(x2, g, wg, wu, wd)


def _ffn_bwd(x2, dx3, g, wg, wu, wd):
    s_len = x2.shape[0]
    tm, tf = _row_block(s_len), FF_TILE_BWD
    nf = D_FF // tf

    def body(x_ref, dy_ref, g_ref, wg_ref, wu_ref, wd_ref,
             dx_ref, h_ref, dgt_ref, dup_ref, act_ref, dg_ref, h_scr, dyb_scr, dh_scr):
        i, j = pl.program_id(0), pl.program_id(1)

        @pl.when(j == 0)
        def _():
            hf, _, _ = _rms(x_ref[...], g_ref[...])
            hb = hf.astype(BF16)
            h_scr[...] = hb
            h_ref[...] = hb
            dyb_scr[...] = dy_ref[...].astype(BF16)
            dh_scr[...] = jnp.zeros_like(dh_scr)

        hb = h_scr[...]
        gt = _dot(hb, wg_ref[...])
        up = _dot(hb, wu_ref[...])
        sg = _sigmoid(gt)
        silu = gt * sg
        dact = _dot_nt(dyb_scr[...], wd_ref[...])
        dgt = (dact * up * (sg * (1.0 + gt * (1.0 - sg)))).astype(BF16)
        dup = (dact * silu).astype(BF16)
        dgt_ref[...] = dgt
        dup_ref[...] = dup
        act_ref[...] = (silu * up).astype(BF16)
        dh_scr[...] += _dot_nt(dgt, wg_ref[...]) + _dot_nt(dup, wu_ref[...])

        @pl.when(j == nf - 1)
        def _():
            gv = g_ref[...]
            _, xh, r = _rms(x_ref[...], gv)
            dx, dgr = _rms_bwd(dh_scr[...], xh, r, gv)
            dx_ref[...] = dy_ref[...] + dx
            _acc_rows(dg_ref, dgr, i == 0)

    rowb = pl.BlockSpec((tm, D_MODEL), lambda i, j: (i, 0))
    ffb = pl.BlockSpec((tm, tf), lambda i, j: (i, j))
    return pl.pallas_call(
        body, name="ffn_bwd", grid=(s_len // tm, nf),
        in_specs=[rowb, rowb, pl.BlockSpec((1, D_MODEL), lambda i, j: (0, 0)),
                  pl.BlockSpec((D_MODEL, tf), lambda i, j: (0, j)),
                  pl.BlockSpec((D_MODEL, tf), lambda i, j: (0, j)),
                  pl.BlockSpec((tf, D_MODEL), lambda i, j: (j, 0))],
        out_specs=[rowb, rowb, ffb, ffb, ffb, pl.BlockSpec((1, D_MODEL), lambda i, j: (0, 0))],
        out_shape=[jax.ShapeDtypeStruct((s_len, D_MODEL), F32), jax.ShapeDtypeStruct((s_len, D_MODEL), BF16),
                   jax.ShapeDtypeStruct((s_len, D_FF), BF16), jax.ShapeDtypeStruct((s_len, D_FF), BF16),
                   jax.ShapeDtypeStruct((s_len, D_FF), BF16), jax.ShapeDtypeStruct((1, D_MODEL), F32)],
        scratch_shapes=[pltpu.VMEM((tm, D_MODEL), BF16), pltpu.VMEM((tm, D_MODEL), BF16),
                        pltpu.VMEM((tm, D_MODEL), F32)],
        compiler_params=_cparams(56, dimension_semantics=("arbitrary", "arbitrary")),
    )(x2, dx3, g, wg, wu, wd)


def _loss_head(x3, g, target):
    s_len = x3.shape[0]
    tm = _row_block(s_len)

    def body(x_ref, g_ref, t_ref, sse_ref, dx_ref, dxb_ref, dg_ref):
        first = pl.program_id(0) == 0
        gv = g_ref[...]
        y, xh, r = _rms(x_ref[...], gv)
        err = y - t_ref[...]
        _acc(sse_ref, jnp.broadcast_to(jnp.sum(err * err), (8, LANES)), first)
        dx, dgr = _rms_bwd(err * (1.0 / D_MODEL), xh, r, gv)
        dx_ref[...] = dx
        dxb_ref[...] = dx.astype(BF16)
        _acc_rows(dg_ref, dgr, first)

    rowb = pl.BlockSpec((tm, D_MODEL), lambda i: (i, 0))
    return pl.pallas_call(
        body, name="loss_head", grid=(s_len // tm,),
        in_specs=[rowb, pl.BlockSpec((1, D_MODEL), lambda i: (0, 0)), rowb],
        out_specs=[pl.BlockSpec((8, LANES), lambda i: (0, 0)), rowb, rowb,
                   pl.BlockSpec((1, D_MODEL), lambda i: (0, 0))],
        out_shape=[jax.ShapeDtypeStruct((8, LANES), F32), jax.ShapeDtypeStruct((s_len, D_MODEL), F32),
                   jax.ShapeDtypeStruct((s_len, D_MODEL), BF16), jax.ShapeDtypeStruct((1, D_MODEL), F32)],
        compiler_params=_cparams(dimension_semantics=("arbitrary",)),
    )(x3, g, target)


def _mla_prep_bwd(lat, g_q, g_kv, w_uq, w_uk, w_uv, cosf, sinf, dq, dk, dv):
    s_len = lat.shape[0]
    tm = _row_block(s_len)

    def body(lat_ref, gq_ref, gkv_ref, wuq_ref, wuk_ref, wuv_ref, cos_ref, sin_ref, dq_ref, dk_ref, dv_ref,
             dlat_ref, dqb_ref, dkb_ref, dvb_ref, dgq_ref, dgkv_ref):
        first = pl.program_id(0) == 0
        lane = lax.broadcasted_iota(jnp.int32, (tm, LANES), 1)
        cosv, sinv = cos_ref[...], sin_ref[...]
        gq, gkv = gq_ref[...], gkv_ref[...]
        _, qxh, qr = _rms(lat_ref[:, 0:256], gq)
        _, kxh, kr_ = _rms(lat_ref[:, 256:384], gkv)
        dkr = jnp.zeros((tm, LANES), F32)
        for h in range(MLA_HEADS):
            sl = slice(h * HEAD_PAD, (h + 1) * HEAD_PAD)
            blk = dq_ref[:, sl]
            dqb_ref[:, sl] = (blk * cosv + _rope_rot_t(blk, lane) * sinv).astype(BF16)
            kblk = dk_ref[:, sl]
            dkb_ref[:, sl] = kblk.astype(BF16)
            dkr = dkr + kblk
        dvb = dv_ref[...].astype(BF16)
        dvb_ref[...] = dvb
        dkr = jnp.where((lane >= 64) & (lane < 96), dkr, 0.0)
        dkr = dkr * cosv + _rope_rot_t(dkr, lane) * sinv
        dql = _dot_nt(dqb_ref[...], wuq_ref[...])
        dkvl = _dot_nt(dkb_ref[...], wuk_ref[...]) + _dot_nt(dvb, wuv_ref[...])
        dcq, dgqr = _rms_bwd(dql, qxh, qr, gq)
        dckv, dgkvr = _rms_bwd(dkvl, kxh, kr_, gkv)
        dlat_ref[:, 0:256] = dcq
        dlat_ref[:, 256:384] = dckv
        dlat_ref[:, K_R_OFF:K_R_OFF + LANES] = pltpu.roll(dkr, 64, 1)
        _acc_rows(dgq_ref, dgqr, first)
        _acc_rows(dgkv_ref, dgkvr, first)

    full = lambda shape: pl.BlockSpec(shape, lambda i: (0, 0))
    rowb = lambda n: pl.BlockSpec((tm, n), lambda i: (i, 0))
    sds = lambda n, dt: jax.ShapeDtypeStruct((s_len, n), dt)
    return pl.pallas_call(
        body, name="mla_prep_bwd", grid=(s_len // tm,),
        in_specs=[rowb(512), full((1, 256)), full((1, 128)), full((256, 1024)), full((128, 1024)),
                  full((128, 512)), rowb(128), rowb(128), rowb(1024), rowb(1024), rowb(512)],
        out_specs=[rowb(512), rowb(1024), rowb(1024), rowb(512), full((1, 256)), full((1, 128))],
        out_shape=[sds(512, F32), sds(1024, BF16), sds(1024, BF16), sds(512, BF16),
                   jax.ShapeDtypeStruct((1, 256), F32), jax.ShapeDtypeStruct((1, 128), F32)],
        compiler_params=_cparams(48, dimension_semantics=("arbitrary",)),
    )(lat, g_q, g_kv, w_uq, w_uk, w_uv, cosf, sinf, dq, dk, dv)


def _in_proj_bwd(x, g, w, dx1, dlat, dsbq, dsbk, dsbv, dgates):
    s_len = x.shape[0]
    tm, tn = _row_block(s_len), 512
    nj = D_IN_PAD // tn

    def body(x_ref, g_ref, w_ref, dx1_ref, dlat_ref, dq_ref, dk_ref, dv_ref, dgate_ref,
             gx_ref, dproj_ref, dg_ref, dh_scr):
        i, j = pl.program_id(0), pl.program_id(1)

        @pl.when(j == 0)
        def _():
            dh_scr[...] = jnp.zeros_like(dh_scr)

        def chunk(val):
            vb = val.astype(BF16)
            dproj_ref[...] = vb
            dh_scr[...] += _dot_nt(vb, w_ref[...])

        for jj, ref in ((0, dlat_ref), (1, dq_ref), (2, dk_ref), (3, dv_ref)):
            pl.when(j == jj)(functools.partial(lambda ref: chunk(ref[...]), ref))
        pl.when(j >= 4)(lambda: chunk(dgate_ref[...]))

        @pl.when(j == nj - 1)
        def _():
            gv = g_ref[...]
            _, xh, r = _rms(x_ref[...], gv)
            dx, dgr = _rms_bwd(dh_scr[...], xh, r, gv)
            gx_ref[...] = dx1_ref[...] + dx
            _acc_rows(dg_ref, dgr, i == 0)

    rowb = pl.BlockSpec((tm, D_MODEL), lambda i, j: (i, 0))
    colb = lambda f: pl.BlockSpec((tm, tn), f)
    return pl.pallas_call(
        body, name="in_proj_bwd", grid=(s_len // tm, nj),
        in_specs=[rowb, pl.BlockSpec((1, D_MODEL), lambda i, j: (0, 0)),
                  pl.BlockSpec((D_MODEL, tn), lambda i, j: (0, j)), rowb,
                  colb(lambda i, j: (i, 0)), colb(lambda i, j: (i, 0)), colb(lambda i, j: (i, 0)),
                  colb(lambda i, j: (i, 0)), colb(lambda i, j: (i, jnp.clip(j - 4, 0, 3)))],
        out_specs=[rowb, colb(lambda i, j: (i, j)), pl.BlockSpec((1, D_MODEL), lambda i, j: (0, 0))],
        out_shape=[jax.ShapeDtypeStruct((s_len, D_MODEL), F32), jax.ShapeDtypeStruct((s_len, D_IN_PAD), BF16),
                   jax.ShapeDtypeStruct((1, D_MODEL), F32)],
        scratch_shapes=[pltpu.VMEM((tm, D_MODEL), F32)],
        compiler_params=_cparams(48, dimension_semantics=("arbitrary", "arbitrary")),
    )(x, g, w, dx1, dlat, dsbq, dsbk, dsbv, dgates)


def _adamw(landed, w, m, v):
    rows = w.shape[0]
    tb = min(rows, ADAM_BLOCK_ROWS)
    c1 = 1.0 - ADAM_B1 ** ADAM_STEP
    c2 = 1.0 - ADAM_B2 ** ADAM_STEP

    def body(l_ref, w_ref, m_ref, v_ref, g_ref, d_ref, nm_ref, nv_ref):
        g = l_ref[0]
        for k in range(1, N_DEV):
            g = g + l_ref[k]
        nm = ADAM_B1 * m_ref[...] + (1.0 - ADAM_B1) * g
        nv = ADAM_B2 * v_ref[...] + (1.0 - ADAM_B2) * (g * g)
        g_ref[...] = g
        nm_ref[...] = nm
        nv_ref[...] = nv
        d_ref[...] = -ADAM_LR * ((nm / c1) / (jnp.sqrt(nv / c2) + ADAM_EPS) + ADAM_WD * w_ref[...])

    blk = pl.BlockSpec((tb, LANES), lambda i: (i, 0))
    return pl.pallas_call(
        body, name="reduce_adamw", grid=(rows // tb,),
        in_specs=[pl.BlockSpec((N_DEV, tb, LANES), lambda i: (0, i, 0)), blk, blk, blk],
        out_specs=[blk, blk, blk, blk],
        out_shape=[jax.ShapeDtypeStruct((rows, LANES), F32)] * 4,
        compiler_params=_cparams(dimension_semantics=("parallel",)),
    )(landed, w, m, v)


def _padded(n):
    return _round_up(n, PACK_ALIGN)


def _shard_shape(shape, axis):
    return tuple(d // N_DEV if a == axis else d for a, d in enumerate(shape))


def _pack_flat(parts, total):
    flat = []
    used = 0
    for p in parts:
        p = p.reshape(-1)
        n = _padded(p.shape[0])
        flat.append(jnp.pad(p, (0, n - p.shape[0])))
        used += n
    if total > used:
        flat.append(jnp.zeros((total - used,), parts[0].dtype))
    return jnp.concatenate(flat)


def _split_pieces(full, axis):
    r, c = full.shape
    if axis == 0:
        return full.reshape(N_DEV, (r // N_DEV) * c)
    return full.reshape(r, N_DEV, c // N_DEV).transpose(1, 0, 2).reshape(N_DEV, r * (c // N_DEV))


def _join_shards(gathered, shape, axis):
    r, c = shape
    if axis == 0:
        return gathered.reshape(r, c)
    return gathered.reshape(N_DEV, r, c // N_DEV).transpose(1, 0, 2).reshape(r, c)


def kernel(x, mem, positions, g_mix, w_in, b_gate, g_q_lat, w_uq, g_kv_lat, w_ukv, w_a_proj, w_b_proj, w_o, g_x, g_mem, w_xq, w_xkv, w_xo, g_ffn, w_gate, w_up, w_down, g_final, loss_target, m_g_mix, m_w_in, m_b_gate, m_g_q_lat, m_w_uq, m_g_kv_lat, m_w_ukv, m_w_a_proj, m_w_b_proj, m_w_o, m_g_x, m_g_mem, m_w_xq, m_w_xkv, m_w_xo, m_g_ffn, m_w_gate, m_w_up, m_w_down, m_g_final, v_g_mix, v_w_in, v_b_gate, v_g_q_lat, v_w_uq, v_g_kv_lat, v_w_ukv, v_w_a_proj, v_w_b_proj, v_w_o, v_g_x, v_g_mem, v_w_xq, v_w_xkv, v_w_xo, v_g_ffn, v_w_gate, v_w_up, v_w_down, v_g_final):
    given = dict(locals())
    s_len = x.shape[1]
    x2d = x.reshape(s_len, D_MODEL)
    mem2d = mem.reshape(-1, D_MODEL)
    target = loss_target.reshape(s_len, D_MODEL)

    parts = []
    for name, shape, axis in SHARDED:
        a = given[name].reshape(_shard_shape(shape, axis))
        if name == "b_gate":
            parts.append(lax.bitcast_convert_type(a, BF16))
        else:
            parts.append(a.astype(BF16))
    n_bf = sum(_padded(math.prod(p.shape)) for p in parts)
    gathered = _all_gather(_pack_flat(parts, n_bf).reshape(n_bf // LANES, LANES)).reshape(N_DEV, n_bf)
    wts = {}
    off = 0
    for (name, shape, axis), p in zip(SHARDED, parts):
        n = math.prod(p.shape)
        piece = gathered[:, off:off + n]
        off += _padded(n)
        if name == "b_gate":
            piece = lax.bitcast_convert_type(piece.reshape(N_DEV, n // 2, 2), F32)
        wts[name] = _join_shards(piece, shape, axis)

    w_in_p = jnp.concatenate([wts["w_in"][:, :416], jnp.zeros((D_MODEL, 96), BF16), wts["w_in"][:, 416:]], axis=1)
    w_uq_p = jnp.pad(wts["w_uq"].reshape(256, MLA_HEADS, 96), ((0, 0), (0, 0), (0, 32))).reshape(256, 1024)
    ukv = wts["w_ukv"].reshape(128, MLA_HEADS, 128)
    w_uk_p = jnp.pad(ukv[:, :, :64], ((0, 0), (0, 0), (0, 64))).reshape(128, 1024)
    w_uv = ukv[:, :, 64:].reshape(128, 512)
    bg = wts["b_gate"]

    inv_freq = ROPE_THETA ** (-jnp.arange(0, MLA_ROPE, 2, dtype=F32) / MLA_ROPE)
    ang = positions.reshape(s_len).astype(F32)[:, None] * inv_freq
    cos16, sin16 = jnp.cos(ang), jnp.sin(ang)
    cosf = jnp.concatenate([jnp.ones((s_len, 64), F32), cos16, cos16, jnp.ones((s_len, 32), F32)], axis=1)
    sinf = jnp.concatenate([jnp.zeros((s_len, 64), F32), sin16, sin16, jnp.zeros((s_len, 32), F32)], axis=1)

    h1, lat, sb, gates = _in_proj(x2d, g_mix, w_in_p)
    qa, ka, va, q_lat, kv_lat = _mla_prep(lat, g_q_lat, g_kv_lat, w_uq_p, w_uk_p, w_uv, cosf, sinf)
    oa, lse = _mla_fwd(qa, ka, va)
    ob, sb_r = _sb_fwd(sb)
    x1 = _merge_fwd(x2d, oa, ob, gates, bg, wts["w_a_proj"], wts["w_b_proj"], wts["w_o"])
    mn, xkv = _mem_kv(mem2d, g_mem, wts["w_xkv"])
    x2 = _xattn_fwd(x1, g_x, wts["w_xq"], xkv, wts["w_xo"])
    x3 = _ffn_fwd(x2, g_ffn, wts["w_gate"], wts["w_up"], wts["w_down"])
    g_final2d = g_final.reshape(1, D_MODEL)
    sse, dx3, dx3b, dg_final = _loss_head(x3, g_final2d, target)
    loss = lax.psum(sse[0, 0] * (0.5 / D_MODEL), ("x", "y", "c"))

    dx2, hf, dgt, dup, act, dg_ffn = _ffn_bwd(x2, dx3, g_ffn, wts["w_gate"], wts["w_up"], wts["w_down"])
    dx1, dw_xq, dw_xo, dxkv, dg_x = _xattn_bwd(x1, dx2, g_x, wts["w_xq"], xkv, wts["w_xo"])
    dw_xkv, dg_mem = _mem_bwd(mem2d, g_mem, wts["w_xkv"], mn, dxkv)
    doa, dob, dgates, dpa, dpb, merged, dx1b, dbg = _merge_bwd(
        dx1, oa, ob, gates, bg, wts["w_a_proj"], wts["w_b_proj"], wts["w_o"])
    dsbq, dsbk, dsbv = _sb_bwd(sb, dob, sb_r)
    dqa, dka, dva = _mla_bwd(qa, ka, va, oa, doa, lse)
    dlat, dqb, dkb, dvb, dg_q, dg_kv = _mla_prep_bwd(
        lat, g_q_lat, g_kv_lat, w_uq_p, w_uk_p, w_uv, cosf, sinf, dqa, dka, dva)
    grad_x, dproj, dg_mix = _in_proj_bwd(x2d, g_mix, w_in_p, dx1, dlat, dsbq, dsbk, dsbv, dgates)

    dw_in_p = _tn_matmul(h1, dproj, "dw_in")
    dw_uq_p = _tn_matmul(q_lat, dqb, "dw_uq")
    dw_uk_p = _tn_matmul(kv_lat, dkb, "dw_uk")
    dw_uv = _tn_matmul(kv_lat, dvb, "dw_uv")
    full_grads = {
        "w_in": jnp.concatenate([dw_in_p[:, :416], dw_in_p[:, 512:]], axis=1),
        "b_gate": dbg,
        "w_uq": dw_uq_p.reshape(256, MLA_HEADS, 128)[:, :, :96].reshape(256, 768),
        "w_ukv": jnp.concatenate([dw_uk_p.reshape(128, MLA_HEADS, 128)[:, :, :64],
                                  dw_uv.reshape(128, MLA_HEADS, 64)], axis=2).reshape(128, 1024),
        "w_a_proj": _tn_matmul(oa, dpa, "dw_a"),
        "w_b_proj": _tn_matmul(ob, dpb, "dw_b"),
        "w_o": _tn_matmul(merged, dx1b, "dw_o"),
        "w_xq": dw_xq,
        "w_xkv": dw_xkv,
        "w_xo": dw_xo,
        "w_gate": _tn_matmul(hf, dgt, "dw_gate", tn=FF_TILE),
        "w_up": _tn_matmul(hf, dup, "dw_up", tn=FF_TILE),
        "w_down": _tn_matmul(act, dx3b, "dw_down", tka=FF_TILE),
    }
    rep_grads = {"g_mix": dg_mix, "g_q_lat": dg_q, "g_kv_lat": dg_kv, "g_x": dg_x, "g_mem": dg_mem,
                 "g_ffn": dg_ffn, "g_final": dg_final}

    n_sh = sum(_padded(math.prod(_shard_shape(shape, axis))) for _, shape, axis in SHARDED)
    n_rep = _padded(sum(n for _, n in REPLICATED))
    n_all = _round_up(n_sh + n_rep, ADAM_BLOCK_ROWS * LANES)
    rows_all = n_all // LANES
    cols = []
    for name, shape, axis in SHARDED:
        pc = _split_pieces(full_grads[name], axis)
        cols.append(jnp.pad(pc, ((0, 0), (0, _padded(pc.shape[1]) - pc.shape[1]))))
    rep = jnp.concatenate([rep_grads[name].reshape(-1) for name, _ in REPLICATED])
    cols.append(jnp.broadcast_to(jnp.pad(rep, (0, n_all - n_sh - rep.shape[0])), (N_DEV, n_all - n_sh)))
    landed = _all_to_all(jnp.concatenate(cols, axis=1).reshape(N_DEV, rows_all, LANES))

    def pack_local(prefix):
        ps = [given[prefix + name] for name, _, _ in SHARDED] + [given[prefix + name] for name, _ in REPLICATED]
        flat = []
        for p in ps[:len(SHARDED)]:
            p = p.reshape(-1)
            flat.append(jnp.pad(p, (0, _padded(p.shape[0]) - p.shape[0])))
        flat.append(jnp.concatenate([p.reshape(-1) for p in ps[len(SHARDED):]]))
        flat = jnp.concatenate(flat)
        return jnp.pad(flat, (0, n_all - flat.shape[0])).reshape(rows_all, LANES)

    outs = _adamw(landed, pack_local(""), pack_local("m_"), pack_local("v_"))
    outs = [o.reshape(n_all) for o in outs]

    def unpack(flat):
        res = {}
        off = 0
        for name, shape, axis in SHARDED:
            n = math.prod(_shard_shape(shape, axis))
            res[name] = flat[off:off + n].reshape(given[name].shape)
            off += _padded(n)
        for name, n in REPLICATED:
            res[name] = flat[off:off + n].reshape(given[name].shape)
            off += n
        return res

    groups = [unpack(o) for o in outs]
    result = [loss, grad_x.reshape(x.shape)]
    for grp in groups:
        result.extend(grp[name] for name in WEIGHT_ORDER)
    return tuple(result)
```

```python
import functools
import math

import jax
import jax.numpy as jnp
from jax import lax
from jax.experimental import pallas as pl
from jax.experimental.pallas import tpu as pltpu

F32 = jnp.float32
BF16 = jnp.bfloat16

D_MODEL = 1024
MLA_HEADS = 8
MLA_Q_RANK = 256
MLA_KV_RANK = 128
MLA_NOPE = 64
MLA_ROPE = 32
MLA_V = 64
ROPE_THETA = 10000.0
SB_WIDTH = 512
X_HEADS = 4
X_HEAD_DIM = 128
D_FF = 2816
EPS = 1e-6
D_IN = 4000
D_IN_PAD = 4096
K_R_OFF = 384
LANES = 128
HEAD_PAD = 128
MLA_SCALE = 1.0 / math.sqrt(MLA_NOPE + MLA_ROPE)
SB_SCALE = 0.125
X_SCALE = 1.0 / math.sqrt(X_HEAD_DIM)
NEG_BIG = -1e30

ADAM_LR = 0.001
ADAM_B1 = 0.9
ADAM_B2 = 0.999
ADAM_EPS = 1e-08
ADAM_WD = 0.01
ADAM_STEP = 10

N_DEV = 8
PACK_ALIGN = 2048
ADAM_BLOCK_ROWS = 1024
MIB = 1024 * 1024

SHARDED = (
    ("w_in", (D_MODEL, D_IN), 1),
    ("b_gate", (2, D_MODEL), 1),
    ("w_uq", (MLA_Q_RANK, 768), 1),
    ("w_ukv", (MLA_KV_RANK, 1024), 1),
    ("w_a_proj", (512, D_MODEL), 1),
    ("w_b_proj", (512, D_MODEL), 1),
    ("w_o", (D_MODEL, D_MODEL), 0),
    ("w_xq", (D_MODEL, 512), 0),
    ("w_xkv", (D_MODEL, 1024), 0),
    ("w_xo", (512, D_MODEL), 1),
    ("w_gate", (D_MODEL, D_FF), 1),
    ("w_up", (D_MODEL, D_FF), 1),
    ("w_down", (D_FF, D_MODEL), 0),
)
REPLICATED = (
    ("g_mix", 1024), ("g_q_lat", 256), ("g_kv_lat", 128), ("g_x", 1024),
    ("g_mem", 1024), ("g_ffn", 1024), ("g_final", 1024),
)
WEIGHT_ORDER = ("g_mix", "w_in", "b_gate", "g_q_lat", "w_uq", "g_kv_lat", "w_ukv", "w_a_proj",
                "w_b_proj", "w_o", "g_x", "g_mem", "w_xq", "w_xkv", "w_xo", "g_ffn", "w_gate",
                "w_up", "w_down", "g_final")


def _round_up(n, m):
    return -(-n // m) * m


def _cparams(vmem_mib=None, **kw):
    if vmem_mib is not None:
        kw["vmem_limit_bytes"] = vmem_mib * MIB
    return pltpu.CompilerParams(**kw)


def _dot(a, b):
    return jnp.dot(a, b, preferred_element_type=F32)


def _dot_nt(a, b):
    return lax.dot_general(a, b, (((1,), (1,)), ((), ())), preferred_element_type=F32)


def _dot_tn(a, b):
    return lax.dot_general(a, b, (((0,), (0,)), ((), ())), preferred_element_type=F32)


def _rms(x, g):
    r = lax.rsqrt(jnp.mean(x * x, axis=-1, keepdims=True) + EPS)
    xh = x * r
    return xh * g, xh, r


def _rms_bwd(dy, xh, r, g):
    u = dy * g
    dx = r * (u - xh * jnp.mean(u * xh, axis=-1, keepdims=True))
    return dx, dy * xh


def _sigmoid(z):
    return 1.0 / (1.0 + jnp.exp(-z))


def _acc_rows(ref, val, first):
    s = jnp.sum(val, axis=0, keepdims=True)

    @pl.when(first)
    def _():
        ref[...] = s

    @pl.when(jnp.logical_not(first))
    def _():
        ref[...] += s


def _acc(ref, val, first):
    @pl.when(first)
    def _():
        ref[...] = val

    @pl.when(jnp.logical_not(first))
    def _():
        ref[...] += val


def _peer(k):
    x, y, c = lax.axis_index("x"), lax.axis_index("y"), lax.axis_index("c")
    px = 1 - x if (k >> 2) & 1 else x
    py = 1 - y if (k >> 1) & 1 else y
    pc = 1 - c if k & 1 else c
    return (px, py, pc), 4 * px + 2 * py + pc


def _all_gather(shard):
    rows = shard.shape[0]

    def body(src, out, send_sems, recv_sems, local_sem):
        _, me = _peer(0)
        mine = pltpu.make_async_copy(src, out.at[me], local_sem)
        mine.start()
        sends = []
        for k in range(1, N_DEV):
            peer, _ = _peer(k)
            cp = pltpu.make_async_remote_copy(
                src_ref=src, dst_ref=out.at[me], send_sem=send_sems.at[k - 1],
                recv_sem=recv_sems.at[k - 1], device_id=peer, device_id_type=pl.DeviceIdType.MESH)
            cp.start()
            sends.append(cp)
        for k in range(1, N_DEV):
            peer, pid = _peer(k)
            pltpu.make_async_remote_copy(
                src_ref=src, dst_ref=out.at[pid], send_sem=send_sems.at[k - 1],
                recv_sem=recv_sems.at[k - 1], device_id=peer,
                device_id_type=pl.DeviceIdType.MESH).wait_recv()
        for cp in sends:
            cp.wait_send()
        mine.wait()

    return pl.pallas_call(
        body, name="weights_all_gather",
        out_shape=jax.ShapeDtypeStruct((N_DEV, rows, LANES), shard.dtype),
        in_specs=[pl.BlockSpec(memory_space=pl.ANY)],
        out_specs=pl.BlockSpec(memory_space=pl.ANY),
        scratch_shapes=[pltpu.SemaphoreType.DMA((N_DEV - 1,)), pltpu.SemaphoreType.DMA((N_DEV - 1,)),
                        pltpu.SemaphoreType.DMA(())],
    )(shard)


def _all_to_all(pieces):
    rows = pieces.shape[1]

    def body(src, out, send_sems, recv_sems, local_sem):
        _, me = _peer(0)
        mine = pltpu.make_async_copy(src.at[me], out.at[me], local_sem)
        mine.start()
        sends = []
        for k in range(1, N_DEV):
            peer, pid = _peer(k)
            cp = pltpu.make_async_remote_copy(
                src_ref=src.at[pid], dst_ref=out.at[me], send_sem=send_sems.at[k - 1],
                recv_sem=recv_sems.at[k - 1], device_id=peer, device_id_type=pl.DeviceIdType.MESH)
            cp.start()
            sends.append(cp)
        for k in range(1, N_DEV):
            peer, pid = _peer(k)
            pltpu.make_async_remote_copy(
                src_ref=src.at[pid], dst_ref=out.at[pid], send_sem=send_sems.at[k - 1],
                recv_sem=recv_sems.at[k - 1], device_id=peer,
                device_id_type=pl.DeviceIdType.MESH).wait_recv()
        for cp in sends:
            cp.wait_send()
        mine.wait()

    return pl.pallas_call(
        body, name="grads_all_to_all",
        out_shape=jax.ShapeDtypeStruct((N_DEV, rows, LANES), pieces.dtype),
        in_specs=[pl.BlockSpec(memory_space=pl.ANY)],
        out_specs=pl.BlockSpec(memory_space=pl.ANY),
        scratch_shapes=[pltpu.SemaphoreType.DMA((N_DEV - 1,)), pltpu.SemaphoreType.DMA((N_DEV - 1,)),
                        pltpu.SemaphoreType.DMA(())],
    )(pieces)


def _tn_matmul(a, b, name, tka=512, tn=1024, ts=512):
    s_len, ka = a.shape
    n = b.shape[1]
    tka, tn, ts = min(tka, ka), min(tn, n), min(ts, s_len)
    assert ka % tka == 0 and n % tn == 0 and s_len % ts == 0

    def body(a_ref, b_ref, o_ref):
        _acc(o_ref, _dot_tn(a_ref[...], b_ref[...]), pl.program_id(2) == 0)

    return pl.pallas_call(
        body, name=name, grid=(ka // tka, n // tn, s_len // ts),
        in_specs=[pl.BlockSpec((ts, tka), lambda i, j, s: (s, i)),
                  pl.BlockSpec((ts, tn), lambda i, j, s: (s, j))],
        out_specs=pl.BlockSpec((tka, tn), lambda i, j, s: (i, j)),
        out_shape=jax.ShapeDtypeStruct((ka, n), F32),
        compiler_params=_cparams(dimension_semantics=("parallel", "parallel", "arbitrary")),
    )(a, b)


def _row_block(s_len):
    return min(s_len, 512)


def _in_proj(x, g, w):
    s_len = x.shape[0]
    tm, tn = _row_block(s_len), 512

    def body(x_ref, g_ref, w_ref, h_ref, lat_ref, sb_ref, gate_ref, h_scr):
        j = pl.program_id(1)

        @pl.when(j == 0)
        def _():
            h, _, _ = _rms(x_ref[...], g_ref[...])
            hb = h.astype(BF16)
            h_scr[...] = hb
            h_ref[...] = hb

        p = _dot(h_scr[...], w_ref[...])

        @pl.when(j == 0)
        def _():
            lat_ref[...] = p

        @pl.when(j == 1)
        def _():
            sb_ref[...] = (p * SB_SCALE).astype(BF16)

        @pl.when((j == 2) | (j == 3))
        def _():
            sb_ref[...] = p.astype(BF16)

        @pl.when(j >= 4)
        def _():
            gate_ref[...] = p

    return pl.pallas_call(
        body, name="in_proj", grid=(s_len // tm, D_IN_PAD // tn),
        in_specs=[pl.BlockSpec((tm, D_MODEL), lambda i, j: (i, 0)),
                  pl.BlockSpec((1, D_MODEL), lambda i, j: (0, 0)),
                  pl.BlockSpec((D_MODEL, tn), lambda i, j: (0, j))],
        out_specs=[pl.BlockSpec((tm, D_MODEL), lambda i, j: (i, 0)),
                   pl.BlockSpec((tm, tn), lambda i, j: (i, 0)),
                   pl.BlockSpec((tm, tn), lambda i, j: (i, jnp.clip(j - 1, 0, 2))),
                   pl.BlockSpec((tm, tn), lambda i, j: (i, jnp.clip(j - 4, 0, 3)))],
        out_shape=[jax.ShapeDtypeStruct((s_len, D_MODEL), BF16),
                   jax.ShapeDtypeStruct((s_len, 512), F32),
                   jax.ShapeDtypeStruct((s_len, 3 * SB_WIDTH), BF16),
                   jax.ShapeDtypeStruct((s_len, 2 * D_MODEL), F32)],
        scratch_shapes=[pltpu.VMEM((tm, D_MODEL), BF16)],
        compiler_params=_cparams(dimension_semantics=("parallel", "arbitrary")),
    )(x, g, w)


def _rope_rot(blk, lane):
    return jnp.where(lane < 80, -pltpu.roll(blk, 112, 1), pltpu.roll(blk, 16, 1))


def _rope_rot_t(blk, lane):
    return jnp.where(lane < 80, pltpu.roll(blk, 112, 1), -pltpu.roll(blk, 16, 1))


def _mla_prep(lat, g_q, g_kv, w_uq, w_uk, w_uv, cosf, sinf):
    s_len = lat.shape[0]
    tm = _row_block(s_len)

    def body(lat_ref, gq_ref, gkv_ref, wuq_ref, wuk_ref, wuv_ref, cos_ref, sin_ref,
             q_ref, k_ref, v_ref, ql_ref, kvl_ref):
        lane = lax.broadcasted_iota(jnp.int32, (tm, LANES), 1)
        cosv, sinv = cos_ref[...], sin_ref[...]
        ql, _, _ = _rms(lat_ref[:, 0:256], gq_ref[...])
        kvl, _, _ = _rms(lat_ref[:, 256:384], gkv_ref[...])
        qlb, kvlb = ql.astype(BF16), kvl.astype(BF16)
        ql_ref[...] = qlb
        kvl_ref[...] = kvlb
        q = _dot(qlb, wuq_ref[...])
        kn = _dot(kvlb, wuk_ref[...])
        v_ref[...] = _dot(kvlb, wuv_ref[...]).astype(BF16)
        kr = pltpu.roll(lat_ref[:, K_R_OFF:K_R_OFF + LANES], 64, 1)
        kr = kr * cosv + _rope_rot(kr, lane) * sinv
        for h in range(MLA_HEADS):
            sl = slice(h * HEAD_PAD, (h + 1) * HEAD_PAD)
            blk = q[:, sl]
            q_ref[:, sl] = (blk * cosv + _rope_rot(blk, lane) * sinv).astype(BF16)
            k_ref[:, sl] = (kn[:, sl] + kr).astype(BF16)

    full = lambda shape: pl.BlockSpec(shape, lambda i: (0, 0))
    rowb = lambda n: pl.BlockSpec((tm, n), lambda i: (i, 0))
    return pl.pallas_call(
        body, name="mla_prep", grid=(s_len // tm,),
        in_specs=[rowb(512), full((1, 256)), full((1, 128)), full((256, 1024)), full((128, 1024)),
                  full((128, 512)), rowb(128), rowb(128)],
        out_specs=[rowb(1024), rowb(1024), rowb(512), rowb(256), rowb(128)],
        out_shape=[jax.ShapeDtypeStruct((s_len, 1024), BF16), jax.ShapeDtypeStruct((s_len, 1024), BF16),
                   jax.ShapeDtypeStruct((s_len, 512), BF16), jax.ShapeDtypeStruct((s_len, 256), BF16),
                   jax.ShapeDtypeStruct((s_len, 128), BF16)],
        compiler_params=_cparams(dimension_semantics=("parallel",)),
    )(lat, g_q, g_kv, w_uq, w_uk, w_uv, cosf, sinf)


ATTN_TQ = 512
ATTN_TK = 256


def _attn_blocks(s_len):
    tq, tk = min(s_len, ATTN_TQ), min(s_len, ATTN_TK)
    return tq, tk, tq // tk


def _causal_mask(tq, tk, sub, strict):
    row = lax.broadcasted_iota(jnp.int32, (tq, tk), 0)
    col = lax.broadcasted_iota(jnp.int32, (tq, tk), 1) + sub * tk
    return col < row if strict else col <= row


def _mla_fwd(q, k, v):
    s_len = q.shape[0]
    tq, tk, nsub = _attn_blocks(s_len)

    def body(q_ref, k_ref, v_ref, o_ref, lse_ref):
        i = pl.program_id(1)
        lane = lax.broadcasted_iota(jnp.int32, (tq, LANES), 1)
        hsl = [slice(hh * HEAD_PAD, (hh + 1) * HEAD_PAD) for hh in range(2)]
        qs = [q_ref[:, hs] for hs in hsl]

        def step(kb, carry, sub):
            rows = pl.ds(pl.multiple_of(kb * tk, tk), tk)
            vblk = v_ref[rows, :]
            new = []
            for hh in range(2):
                m, l, acc = carry[hh]
                s = _dot_nt(qs[hh], k_ref[rows, hsl[hh]]) * MLA_SCALE
                if sub is not None:
                    s = jnp.where(_causal_mask(tq, tk, sub, strict=False), s, NEG_BIG)
                m_new = jnp.maximum(m, jnp.max(s, axis=-1, keepdims=True))
                alpha = jnp.exp(m - m_new)
                p = jnp.exp(s - m_new)
                l = alpha * l + jnp.sum(p, axis=-1, keepdims=True)
                acc = alpha * acc + _dot(p.astype(BF16), vblk)
                new.append((m_new, l, acc))
            return tuple(new)

        init = (jnp.full((tq, 1), NEG_BIG, F32), jnp.zeros((tq, 1), F32), jnp.zeros((tq, LANES), F32))
        carry = lax.fori_loop(0, i * nsub, lambda kb, cy: step(kb, cy, None), (init, init))
        for sub in range(nsub):
            carry = step(i * nsub + sub, carry, sub)
        outs = [acc / l for _, l, acc in carry]
        lses = [jnp.broadcast_to(m + jnp.log(l), (tq, LANES)) for m, l, _ in carry]
        o_ref[...] = jnp.where(lane < 64, outs[0], outs[1]).astype(BF16)
        lse_ref[...] = jnp.where(lane < 64, lses[0], lses[1])

    return pl.pallas_call(
        body, name="mla_fwd", grid=(4, s_len // tq),
        in_specs=[pl.BlockSpec((tq, 2 * HEAD_PAD), lambda p, i: (i, p)),
                  pl.BlockSpec((s_len, 2 * HEAD_PAD), lambda p, i: (0, p)),
                  pl.BlockSpec((s_len, LANES), lambda p, i: (0, p))],
        out_specs=[pl.BlockSpec((tq, LANES), lambda p, i: (i, p)),
                   pl.BlockSpec((None, tq, LANES), lambda p, i: (p, i, 0))],
        out_shape=[jax.ShapeDtypeStruct((s_len, 512), BF16), jax.ShapeDtypeStruct((4, s_len, LANES), F32)],
        compiler_params=_cparams(40, dimension_semantics=("parallel", "arbitrary")),
    )(q, k, v)


def _mla_bwd(q, k, v, o, do, lse):
    s_len = q.shape[0]
    tq, tk, nsub = _attn_blocks(s_len)

    def body(q_ref, k_ref, v_ref, o_ref, do_ref, lse_ref, dq_ref, dk_ref, dv_ref):
        i = pl.program_id(1)
        lane = lax.broadcasted_iota(jnp.int32, (tq, LANES), 1)

        @pl.when(i == 0)
        def _():
            dk_ref[...] = jnp.zeros_like(dk_ref)
            dv_ref[...] = jnp.zeros_like(dv_ref)

        hsl = [slice(hh * HEAD_PAD, (hh + 1) * HEAD_PAD) for hh in range(2)]
        qs = [q_ref[:, hs] for hs in hsl]
        dop, op, lsep = do_ref[...], o_ref[...], lse_ref[...]
        dohs = [jnp.where((lane // 64) == hh, dop, jnp.zeros((), BF16)) for hh in range(2)]
        deltas = [jnp.sum(d.astype(F32) * op.astype(F32), axis=-1, keepdims=True) for d in dohs]
        lses = [lsep[:, 64 * hh:64 * hh + 1] for hh in range(2)]

        def step(kb, dqs, sub):
            rows = pl.ds(pl.multiple_of(kb * tk, tk), tk)
            vblk = v_ref[rows, :]
            new = []
            for hh in range(2):
                kblk = k_ref[rows, hsl[hh]]
                s = _dot_nt(qs[hh], kblk) * MLA_SCALE
                if sub is not None:
                    s = jnp.where(_causal_mask(tq, tk, sub, strict=False), s, NEG_BIG)
                p = jnp.exp(s - lses[hh])
                dp = _dot_nt(dohs[hh], vblk)
                ds = (p * (dp - deltas[hh]) * MLA_SCALE).astype(BF16)
                dv_ref[rows, :] += _dot_tn(p.astype(BF16), dohs[hh])
                dk_ref[rows, hsl[hh]] += _dot_tn(ds, qs[hh])
                new.append(dqs[hh] + _dot(ds, kblk))
            return tuple(new)

        zero = jnp.zeros((tq, LANES), F32)
        dqs = lax.fori_loop(0, i * nsub, lambda kb, cy: step(kb, cy, None), (zero, zero))
        for sub in range(nsub):
            dqs = step(i * nsub + sub, dqs, sub)
        dq_ref[:, hsl[0]] = dqs[0]
        dq_ref[:, hsl[1]] = dqs[1]

    return pl.pallas_call(
        body, name="mla_bwd", grid=(4, s_len // tq),
        in_specs=[pl.BlockSpec((tq, 2 * HEAD_PAD), lambda p, i: (i, p)),
                  pl.BlockSpec((s_len, 2 * HEAD_PAD), lambda p, i: (0, p)),
                  pl.BlockSpec((s_len, LANES), lambda p, i: (0, p)),
                  pl.BlockSpec((tq, LANES), lambda p, i: (i, p)),
                  pl.BlockSpec((tq, LANES), lambda p, i: (i, p)),
                  pl.BlockSpec((None, tq, LANES), lambda p, i: (p, i, 0))],
        out_specs=[pl.BlockSpec((tq, 2 * HEAD_PAD), lambda p, i: (i, p)),
                   pl.BlockSpec((s_len, 2 * HEAD_PAD), lambda p, i: (0, p)),
                   pl.BlockSpec((s_len, LANES), lambda p, i: (0, p))],
        out_shape=[jax.ShapeDtypeStruct((s_len, 1024), F32), jax.ShapeDtypeStruct((s_len, 1024), F32),
                   jax.ShapeDtypeStruct((s_len, 512), F32)],
        compiler_params=_cparams(56, dimension_semantics=("arbitrary", "arbitrary")),
    )(q, k, v, o, do, lse)


def _log_sigmoids(z):
    sp = jnp.log(1.0 + jnp.exp(-jnp.abs(z)))
    return jnp.minimum(z, 0.0) - sp, jnp.minimum(-z, 0.0) - sp


def _split_dot(x, w, parts, nt=False):
    dot = _dot_nt if nt else _dot
    out = None
    for _ in range(parts):
        xb = x.astype(BF16)
        t = dot(xb, w)
        out = t if out is None else out + t
        x = x - xb.astype(F32)
    return out


def _sb_fwd(sb):
    s_len = sb.shape[0]
    tq, tk, nsub = _attn_blocks(s_len)
    assert s_len // tk <= 64

    def body(q_ref, k_ref, v_ref, o_ref, r_ref):
        i = pl.program_id(1)
        lane = lax.broadcasted_iota(jnp.int32, (tq, LANES), 1)
        upper = (lax.broadcasted_iota(jnp.int32, (tk, tk), 0)
                 > lax.broadcasted_iota(jnp.int32, (tk, tk), 1)).astype(BF16)
        qp = q_ref[...]
        qs = [jnp.where((lane // 64) == hh, qp, jnp.zeros((), BF16)) for hh in range(2)]

        def step(kb, carry, sub):
            rows = pl.ds(pl.multiple_of(kb * tk, tk), tk)
            kblk, vblk = k_ref[rows, :], v_ref[rows, :]
            r = carry[2]
            new = []
            for hh in range(2):
                c, acc = carry[hh]
                z = _dot_nt(qs[hh], kblk)
                lb, lom = _log_sigmoids(z)
                if sub is not None:
                    strict = _causal_mask(tq, tk, sub, strict=True)
                    lom = jnp.where(strict, lom, 0.0)
                suf = _split_dot(lom, upper, 2)
                a = jnp.exp(lb + suf + c)
                if sub is not None:
                    a = jnp.where(strict, a, 0.0)
                acc = acc + _dot(a.astype(BF16), vblk)
                rs = suf[:, 0:1] + lom[:, 0:1]
                r = jnp.where(lane == 64 * hh + kb, rs, r)
                new.append((c + rs, acc))
            return new[0], new[1], r

        init = (jnp.zeros((tq, 1), F32), jnp.zeros((tq, LANES), F32))
        carry = (init, init, jnp.zeros((tq, LANES), F32))
        for sub in reversed(range(nsub)):
            carry = step(i * nsub + sub, carry, sub)
        carry = lax.fori_loop(0, i * nsub, lambda t, cy: step(i * nsub - 1 - t, cy, None), carry)
        o_ref[...] = jnp.where(lane < 64, carry[0][1], carry[1][1]).astype(BF16)
        r_ref[...] = carry[2]

    return pl.pallas_call(
        body, name="sb_fwd", grid=(4, s_len // tq),
        in_specs=[pl.BlockSpec((tq, LANES), lambda p, i: (i, p)),
                  pl.BlockSpec((s_len, LANES), lambda p, i: (0, 4 + p)),
                  pl.BlockSpec((s_len, LANES), lambda p, i: (0, 8 + p))],
        out_specs=[pl.BlockSpec((tq, LANES), lambda p, i: (i, p)),
                   pl.BlockSpec((None, tq, LANES), lambda p, i: (p, i, 0))],
        out_shape=[jax.ShapeDtypeStruct((s_len, 512), BF16), jax.ShapeDtypeStruct((4, s_len, LANES), F32)],
        compiler_params=_cparams(40, dimension_semantics=("parallel", "arbitrary")),
    )(sb, sb, sb)


def _sb_bwd(sb, do, r):
    s_len = sb.shape[0]
    tq, tk, nsub = _attn_blocks(s_len)

    def body(q_ref, k_ref, v_ref, do_ref, r_ref, dq_ref, dk_ref, dv_ref):
        i = pl.program_id(1)
        lane = lax.broadcasted_iota(jnp.int32, (tq, LANES), 1)
        upper = (lax.broadcasted_iota(jnp.int32, (tk, tk), 0)
                 > lax.broadcasted_iota(jnp.int32, (tk, tk), 1)).astype(BF16)
        tri = (lax.broadcasted_iota(jnp.int32, (LANES, LANES), 0)
               > lax.broadcasted_iota(jnp.int32, (LANES, LANES), 1)).astype(BF16)

        @pl.when(i == 0)
        def _():
            dk_ref[...] = jnp.zeros_like(dk_ref)
            dv_ref[...] = jnp.zeros_like(dv_ref)

        qp, dop, rp = q_ref[...], do_ref[...], r_ref[...]
        hmasks = [(lane // 64) == hh for hh in range(2)]
        qs = [jnp.where(hm, qp, jnp.zeros((), BF16)) for hm in hmasks]
        dohs = [jnp.where(hm, dop, jnp.zeros((), BF16)) for hm in hmasks]
        rights = [_split_dot(jnp.where(hm, rp, 0.0), tri, 3) for hm in hmasks]

        def step(kb, carry, sub):
            rows = pl.ds(pl.multiple_of(kb * tk, tk), tk)
            kblk, vblk = k_ref[rows, :], v_ref[rows, :]
            new = []
            for hh in range(2):
                pre, dq = carry[hh]
                c = jnp.sum(jnp.where(lane == 64 * hh + kb, rights[hh], 0.0), axis=-1, keepdims=True)
                z = _dot_nt(qs[hh], kblk)
                lb, lom = _log_sigmoids(z)
                if sub is not None:
                    strict = _causal_mask(tq, tk, sub, strict=True)
                    lom = jnp.where(strict, lom, 0.0)
                suf = _split_dot(lom, upper, 2)
                a = jnp.exp(lb + suf + c)
                if sub is not None:
                    a = jnp.where(strict, a, 0.0)
                g = a * _dot_nt(dohs[hh], vblk)
                dv_ref[rows, :] += _dot_tn(a.astype(BF16), dohs[hh])
                left = _split_dot(g, upper, 1, nt=True) + pre
                sig = jnp.exp(lb)
                dz = g * (1.0 - sig) - sig * left
                if sub is not None:
                    dz = jnp.where(strict, dz, 0.0)
                dzb = dz.astype(BF16)
                dk_ref[rows, :] += _dot_tn(dzb, qs[hh])
                pre = left[:, tk - 1:tk] + g[:, tk - 1:tk]
                new.append((pre, dq + _dot(dzb, kblk)))
            return tuple(new)

        init = (jnp.zeros((tq, 1), F32), jnp.zeros((tq, LANES), F32))
        carry = lax.fori_loop(0, i * nsub, lambda kb, cy: step(kb, cy, None), (init, init))
        for sub in range(nsub):
            carry = step(i * nsub + sub, carry, sub)
        dq_ref[...] = jnp.where(lane < 64, carry[0][1], carry[1][1]) * SB_SCALE

    return pl.pallas_call(
        body, name="sb_bwd", grid=(4, s_len // tq),
        in_specs=[pl.BlockSpec((tq, LANES), lambda p, i: (i, p)),
                  pl.BlockSpec((s_len, LANES), lambda p, i: (0, 4 + p)),
                  pl.BlockSpec((s_len, LANES), lambda p, i: (0, 8 + p)),
                  pl.BlockSpec((tq, LANES), lambda p, i: (i, p)),
                  pl.BlockSpec((None, tq, LANES), lambda p, i: (p, i, 0))],
        out_specs=[pl.BlockSpec((tq, LANES), lambda p, i: (i, p)),
                   pl.BlockSpec((s_len, LANES), lambda p, i: (0, p)),
                   pl.BlockSpec((s_len, LANES), lambda p, i: (0, p))],
        out_shape=[jax.ShapeDtypeStruct((s_len, 512), F32)] * 3,
        compiler_params=_cparams(48, dimension_semantics=("arbitrary", "arbitrary")),
    )(sb, sb, sb, do, r)


def _merge_fwd(x, oa, ob, gates, bg, wa, wb, wo):
    s_len = x.shape[0]
    tm = _row_block(s_len)

    def body(x_ref, oa_ref, ob_ref, g_ref, bg_ref, wa_ref, wb_ref, wo_ref, y_ref):
        pa = _dot(oa_ref[...], wa_ref[...])
        pb = _dot(ob_ref[...], wb_ref[...])
        merged = (_sigmoid(g_ref[:, 0:D_MODEL] + bg_ref[0:1, :]) * pa
                  + _sigmoid(g_ref[:, D_MODEL:2 * D_MODEL] + bg_ref[1:2, :]) * pb)
        y_ref[...] = x_ref[...] + _dot(merged.astype(BF16), wo_ref[...])

    full = lambda shape: pl.BlockSpec(shape, lambda i: (0, 0))
    rowb = lambda n: pl.BlockSpec((tm, n), lambda i: (i, 0))
    return pl.pallas_call(
        body, name="merge_fwd", grid=(s_len // tm,),
        in_specs=[rowb(1024), rowb(512), rowb(512), rowb(2048), full((2, 1024)), full((512, 1024)),
                  full((512, 1024)), full((1024, 1024))],
        out_specs=rowb(1024),
        out_shape=jax.ShapeDtypeStruct((s_len, D_MODEL), F32),
        compiler_params=_cparams(48, dimension_semantics=("parallel",)),
    )(x, oa, ob, gates, bg, wa, wb, wo)


def _merge_bwd(dx1, oa, ob, gates, bg, wa, wb, wo):
    s_len = dx1.shape[0]
    tm = _row_block(s_len)

    def body(dx_ref, oa_ref, ob_ref, g_ref, bg_ref, wa_ref, wb_ref, wo_ref,
             doa_ref, dob_ref, dgate_ref, dpa_ref, dpb_ref, merged_ref, dxb_ref, dbg_ref):
        first = pl.program_id(0) == 0
        dxb = dx_ref[...].astype(BF16)
        dxb_ref[...] = dxb
        pa = _dot(oa_ref[...], wa_ref[...])
        pb = _dot(ob_ref[...], wb_ref[...])
        sa = _sigmoid(g_ref[:, 0:D_MODEL] + bg_ref[0:1, :])
        sbg = _sigmoid(g_ref[:, D_MODEL:2 * D_MODEL] + bg_ref[1:2, :])
        merged_ref[...] = (sa * pa + sbg * pb).astype(BF16)
        dm = _dot_nt(dxb, wo_ref[...])
        dpa = (dm * sa).astype(BF16)
        dpb = (dm * sbg).astype(BF16)
        dpa_ref[...] = dpa
        dpb_ref[...] = dpb
        dga = dm * pa * sa * (1.0 - sa)
        dgb = dm * pb * sbg * (1.0 - sbg)
        dgate_ref[:, 0:D_MODEL] = dga.astype(BF16)
        dgate_ref[:, D_MODEL:2 * D_MODEL] = dgb.astype(BF16)
        _acc_rows(dbg_ref.at[0:1, :], dga, first)
        _acc_rows(dbg_ref.at[1:2, :], dgb, first)
        doa_ref[...] = _dot_nt(dpa, wa_ref[...]).astype(BF16)
        dob_ref[...] = _dot_nt(dpb, wb_ref[...]).astype(BF16)

    full = lambda shape: pl.BlockSpec(shape, lambda i: (0, 0))
    rowb = lambda n: pl.BlockSpec((tm, n), lambda i: (i, 0))
    sds = lambda n, dt: jax.ShapeDtypeStruct((s_len, n), dt)
    return pl.pallas_call(
        body, name="merge_bwd", grid=(s_len // tm,),
        in_specs=[rowb(1024), rowb(512), rowb(512), rowb(2048), full((2, 1024)), full((512, 1024)),
                  full((512, 1024)), full((1024, 1024))],
        out_specs=[rowb(512), rowb(512), rowb(2048), rowb(1024), rowb(1024), rowb(1024), rowb(1024),
                   full((2, 1024))],
        out_shape=[sds(512, BF16), sds(512, BF16), sds(2048, BF16), sds(1024, BF16), sds(1024, BF16),
                   sds(1024, BF16), sds(1024, BF16), jax.ShapeDtypeStruct((2, 1024), F32)],
        compiler_params=_cparams(48, dimension_semantics=("arbitrary",)),
    )(dx1, oa, ob, gates, bg, wa, wb, wo)


def _mem_kv(mem, g, w):
    m_len = mem.shape[0]

    def body(mem_ref, g_ref, w_ref, mn_ref, kv_ref):
        mn, _, _ = _rms(mem_ref[...], g_ref[...])
        mnb = mn.astype(BF16)
        mn_ref[...] = mnb
        kv_ref[...] = _dot(mnb, w_ref[...]).astype(BF16)

    return pl.pallas_call(
        body, name="mem_kv",
        out_shape=[jax.ShapeDtypeStruct((m_len, D_MODEL), BF16), jax.ShapeDtypeStruct((m_len, 1024), BF16)],
    )(mem, g, w)


def _mem_bwd(mem, g, w, mn, dkv):
    def body(mem_ref, g_ref, w_ref, mn_ref, dkv_ref, dw_ref, dg_ref):
        dkvb = dkv_ref[...].astype(BF16)
        dw_ref[...] = _dot_tn(mn_ref[...], dkvb)
        dmn = _dot_nt(dkvb, w_ref[...])
        _, xh, _ = _rms(mem_ref[...], g_ref[...])
        dg_ref[...] = jnp.sum(dmn * xh, axis=0, keepdims=True)

    return pl.pallas_call(
        body, name="mem_bwd",
        out_shape=[jax.ShapeDtypeStruct((D_MODEL, 1024), F32), jax.ShapeDtypeStruct((1, D_MODEL), F32)],
    )(mem, g, w, mn, dkv)


def _xattn_heads(xqb, kv_ref, m_len):
    ps = []
    for h in range(X_HEADS):
        hs = slice(h * X_HEAD_DIM, (h + 1) * X_HEAD_DIM)
        s = _dot_nt(xqb[:, hs], kv_ref[:, hs]) * X_SCALE
        e = jnp.exp(s - jnp.max(s, axis=-1, keepdims=True))
        ps.append(e / jnp.sum(e, axis=-1, keepdims=True))
    return ps


def _xattn_fwd(x1, g, wxq, kv, wxo):
    s_len, m_len = x1.shape[0], kv.shape[0]
    tm = _row_block(s_len)

    def body(x_ref, g_ref, wq_ref, kv_ref, wo_ref, y_ref):
        hx, _, _ = _rms(x_ref[...], g_ref[...])
        xqb = _dot(hx.astype(BF16), wq_ref[...]).astype(BF16)
        ps = _xattn_heads(xqb, kv_ref, m_len)
        xo = jnp.concatenate(
            [_dot(ps[h].astype(BF16), kv_ref[:, 512 + h * X_HEAD_DIM:512 + (h + 1) * X_HEAD_DIM])
             for h in range(X_HEADS)], axis=-1)
        y_ref[...] = x_ref[...] + _dot(xo.astype(BF16), wo_ref[...])

    full = lambda shape: pl.BlockSpec(shape, lambda i: (0, 0))
    rowb = lambda n: pl.BlockSpec((tm, n), lambda i: (i, 0))
    return pl.pallas_call(
        body, name="xattn_fwd", grid=(s_len // tm,),
        in_specs=[rowb(1024), full((1, 1024)), full((1024, 512)), full((m_len, 1024)), full((512, 1024))],
        out_specs=rowb(1024),
        out_shape=jax.ShapeDtypeStruct((s_len, D_MODEL), F32),
        compiler_params=_cparams(48, dimension_semantics=("parallel",)),
    )(x1, g, wxq, kv, wxo)


def _xattn_bwd(x1, dx2, g, wxq, kv, wxo):
    s_len, m_len = x1.shape[0], kv.shape[0]
    tm = _row_block(s_len)

    def body(x_ref, dy_ref, g_ref, wq_ref, kv_ref, wo_ref, dx_ref, dwq_ref, dwo_ref, dkv_ref, dg_ref):
        first = pl.program_id(0) == 0
        gv = g_ref[...]
        hx, xh, r = _rms(x_ref[...], gv)
        hxb = hx.astype(BF16)
        xqb = _dot(hxb, wq_ref[...]).astype(BF16)
        ps = _xattn_heads(xqb, kv_ref, m_len)
        dy = dy_ref[...]
        dyb = dy.astype(BF16)
        dxo = _dot_nt(dyb, wo_ref[...])
        xos, dqs, dks, dvs = [], [], [], []
        for h in range(X_HEADS):
            hs = slice(h * X_HEAD_DIM, (h + 1) * X_HEAD_DIM)
            vs = slice(512 + h * X_HEAD_DIM, 512 + (h + 1) * X_HEAD_DIM)
            p = ps[h]
            pb = p.astype(BF16)
            dxoh = dxo[:, hs].astype(BF16)
            xos.append(_dot(pb, kv_ref[:, vs]))
            dp = _dot_nt(dxoh, kv_ref[:, vs])
            ds = (p * (dp - jnp.sum(dp * p, axis=-1, keepdims=True)) * X_SCALE).astype(BF16)
            dvs.append(_dot_tn(pb, dxoh))
            dks.append(_dot_tn(ds, xqb[:, hs]))
            dqs.append(_dot(ds, kv_ref[:, hs]))
        xob = jnp.concatenate(xos, axis=-1).astype(BF16)
        dxqb = jnp.concatenate(dqs, axis=-1).astype(BF16)
        _acc(dwo_ref, _dot_tn(xob, dyb), first)
        _acc(dwq_ref, _dot_tn(hxb, dxqb), first)
        _acc(dkv_ref, jnp.concatenate(dks + dvs, axis=-1), first)
        dhx = _dot_nt(dxqb, wq_ref[...])
        dx, dgr = _rms_bwd(dhx, xh, r, gv)
        dx_ref[...] = dy + dx
        _acc_rows(dg_ref, dgr, first)

    full = lambda shape: pl.BlockSpec(shape, lambda i: (0, 0))
    rowb = lambda n: pl.BlockSpec((tm, n), lambda i: (i, 0))
    return pl.pallas_call(
        body, name="xattn_bwd", grid=(s_len // tm,),
        in_specs=[rowb(1024), rowb(1024), full((1, 1024)), full((1024, 512)), full((m_len, 1024)),
                  full((512, 1024))],
        out_specs=[rowb(1024), full((1024, 512)), full((512, 1024)), full((m_len, 1024)), full((1, 1024))],
        out_shape=[jax.ShapeDtypeStruct((s_len, D_MODEL), F32), jax.ShapeDtypeStruct((1024, 512), F32),
                   jax.ShapeDtypeStruct((512, 1024), F32), jax.ShapeDtypeStruct((m_len, 1024), F32),
                   jax.ShapeDtypeStruct((1, D_MODEL), F32)],
        compiler_params=_cparams(48, dimension_semantics=("arbitrary",)),
    )(x1, dx2, g, wxq, kv, wxo)


FF_TILE = 1408
FF_TILE_BWD = 256


def _ffn_fwd(x2, g, wg, wu, wd):
    s_len = x2.shape[0]
    tm, tf = _row_block(s_len), FF_TILE

    def body(x_ref, g_ref, wg_ref, wu_ref, wd_ref, y_ref, h_scr):
        j = pl.program_id(1)

        @pl.when(j == 0)
        def _():
            hf, _, _ = _rms(x_ref[...], g_ref[...])
            h_scr[...] = hf.astype(BF16)
            y_ref[...] = x_ref[...]

        hb = h_scr[...]
        gt = _dot(hb, wg_ref[...])
        up = _dot(hb, wu_ref[...])
        act = gt * _sigmoid(gt) * up
        y_ref[...] += _dot(act.astype(BF16), wd_ref[...])

    return pl.pallas_call(
        body, name="ffn_fwd", grid=(s_len // tm, D_FF // tf),
        in_specs=[pl.BlockSpec((tm, D_MODEL), lambda i, j: (i, 0)),
                  pl.BlockSpec((1, D_MODEL), lambda i, j: (0, 0)),
                  pl.BlockSpec((D_MODEL, tf), lambda i, j: (0, j)),
                  pl.BlockSpec((D_MODEL, tf), lambda i, j: (0, j)),
                  pl.BlockSpec((tf, D_MODEL), lambda i, j: (j, 0))],
        out_specs=pl.BlockSpec((tm, D_MODEL), lambda i, j: (i, 0)),
        out_shape=jax.ShapeDtypeStruct((s_len, D_MODEL), F32),
        scratch_shapes=[pltpu.VMEM((tm, D_MODEL), BF16)],
        compiler_params=_cparams(48, dimension_semantics=("parallel", "arbitrary")),
    )(x2, g, wg, wu, wd)


def _ffn_bwd(x2, dx3, g, wg, wu, wd):
    s_len = x2.shape[0]
    tm, tf = _row_block(s_len), FF_TILE_BWD
    nf = D_FF // tf

    def body(x_ref, dy_ref, g_ref, wg_ref, wu_ref, wd_ref,
             dx_ref, h_ref, dgt_ref, dup_ref, act_ref, dg_ref, h_scr, dyb_scr, dh_scr):
        i, j = pl.program_id(0), pl.program_id(1)

        @pl.when(j == 0)
        def _():
            hf, _, _ = _rms(x_ref[...], g_ref[...])
            hb = hf.astype(BF16)
            h_scr[...] = hb
            h_ref[...] = hb
            dyb_scr[...] = dy_ref[...].astype(BF16)
            dh_scr[...] = jnp.zeros_like(dh_scr)

        hb = h_scr[...]
        gt = _dot(hb, wg_ref[...])
        up = _dot(hb, wu_ref[...])
        sg = _sigmoid(gt)
        silu = gt * sg
        dact = _dot_nt(dyb_scr[...], wd_ref[...])
        dgt = (dact * up * (sg * (1.0 + gt * (1.0 - sg)))).astype(BF16)
        dup = (dact * silu).astype(BF16)
        dgt_ref[...] = dgt
        dup_ref[...] = dup
        act_ref[...] = (silu * up).astype(BF16)
        dh_scr[...] += _dot_nt(dgt, wg_ref[...]) + _dot_nt(dup, wu_ref[...])

        @pl.when(j == nf - 1)
        def _():
            gv = g_ref[...]
            _, xh, r = _rms(x_ref[...], gv)
            dx, dgr = _rms_bwd(dh_scr[...], xh, r, gv)
            dx_ref[...] = dy_ref[...] + dx
            _acc_rows(dg_ref, dgr, i == 0)

    rowb = pl.BlockSpec((tm, D_MODEL), lambda i, j: (i, 0))
    ffb = pl.BlockSpec((tm, tf), lambda i, j: (i, j))
    return pl.pallas_call(
        body, name="ffn_bwd", grid=(s_len // tm, nf),
        in_specs=[rowb, rowb, pl.BlockSpec((1, D_MODEL), lambda i, j: (0, 0)),
                  pl.BlockSpec((D_MODEL, tf), lambda i, j: (0, j)),
                  pl.BlockSpec((D_MODEL, tf), lambda i, j: (0, j)),
                  pl.BlockSpec((tf, D_MODEL), lambda i, j: (j, 0))],
        out_specs=[rowb, rowb, ffb, ffb, ffb, pl.BlockSpec((1, D_MODEL), lambda i, j: (0, 0))],
        out_shape=[jax.ShapeDtypeStruct((s_len, D_MODEL), F32), jax.ShapeDtypeStruct((s_len, D_MODEL), BF16),
                   jax.ShapeDtypeStruct((s_len, D_FF), BF16), jax.ShapeDtypeStruct((s_len, D_FF), BF16),
                   jax.ShapeDtypeStruct((s_len, D_FF), BF16), jax.ShapeDtypeStruct((1, D_MODEL), F32)],
        scratch_shapes=[pltpu.VMEM((tm, D_MODEL), BF16), pltpu.VMEM((tm, D_MODEL), BF16),
                        pltpu.VMEM((tm, D_MODEL), F32)],
        compiler_params=_cparams(56, dimension_semantics=("arbitrary", "arbitrary")),
    )(x2, dx3, g, wg, wu, wd)


def _loss_head(x3, g, target):
    s_len = x3.shape[0]
    tm = _row_block(s_len)

    def body(x_ref, g_ref, t_ref, sse_ref, dx_ref, dxb_ref, dg_ref):
        first = pl.program_id(0) == 0
        gv = g_ref[...]
        y, xh, r = _rms(x_ref[...], gv)
        err = y - t_ref[...]
        _acc(sse_ref, jnp.broadcast_to(jnp.sum(err * err), (8, LANES)), first)
        dx, dgr = _rms_bwd(err * (1.0 / D_MODEL), xh, r, gv)
        dx_ref[...] = dx
        dxb_ref[...] = dx.astype(BF16)
        _acc_rows(dg_ref, dgr, first)

    rowb = pl.BlockSpec((tm, D_MODEL), lambda i: (i, 0))
    return pl.pallas_call(
        body, name="loss_head", grid=(s_len // tm,),
        in_specs=[rowb, pl.BlockSpec((1, D_MODEL), lambda i: (0, 0)), rowb],
        out_specs=[pl.BlockSpec((8, LANES), lambda i: (0, 0)), rowb, rowb,
                   pl.BlockSpec((1, D_MODEL), lambda i: (0, 0))],
        out_shape=[jax.ShapeDtypeStruct((8, LANES), F32), jax.ShapeDtypeStruct((s_len, D_MODEL), F32),
                   jax.ShapeDtypeStruct((s_len, D_MODEL), BF16), jax.ShapeDtypeStruct((1, D_MODEL), F32)],
        compiler_params=_cparams(dimension_semantics=("arbitrary",)),
    )(x3, g, target)


def _mla_prep_bwd(lat, g_q, g_kv, w_uq, w_uk, w_uv, cosf, sinf, dq, dk, dv):
    s_len = lat.shape[0]
    tm = _row_block(s_len)

    def body(lat_ref, gq_ref, gkv_ref, wuq_ref, wuk_ref, wuv_ref, cos_ref, sin_ref, dq_ref, dk_ref, dv_ref,
             dlat_ref, dqb_ref, dkb_ref, dvb_ref, dgq_ref, dgkv_ref):
        first = pl.program_id(0) == 0
        lane = lax.broadcasted_iota(jnp.int32, (tm, LANES), 1)
        cosv, sinv = cos_ref[...], sin_ref[...]
        gq, gkv = gq_ref[...], gkv_ref[...]
        _, qxh, qr = _rms(lat_ref[:, 0:256], gq)
        _, kxh, kr_ = _rms(lat_ref[:, 256:384], gkv)
        dkr = jnp.zeros((tm, LANES), F32)
        for h in range(MLA_HEADS):
            sl = slice(h * HEAD_PAD, (h + 1) * HEAD_PAD)
            blk = dq_ref[:, sl]
            dqb_ref[:, sl] = (blk * cosv + _rope_rot_t(blk, lane) * sinv).astype(BF16)
            kblk = dk_ref[:, sl]
            dkb_ref[:, sl] = kblk.astype(BF16)
            dkr = dkr + kblk
        dvb = dv_ref[...].astype(BF16)
        dvb_ref[...] = dvb
        dkr = jnp.where((lane >= 64) & (lane < 96), dkr, 0.0)
        dkr = dkr * cosv + _rope_rot_t(dkr, lane) * sinv
        dql = _dot_nt(dqb_ref[...], wuq_ref[...])
        dkvl = _dot_nt(dkb_ref[...], wuk_ref[...]) + _dot_nt(dvb, wuv_ref[...])
        dcq, dgqr = _rms_bwd(dql, qxh, qr, gq)
        dckv, dgkvr = _rms_bwd(dkvl, kxh, kr_, gkv)
        dlat_ref[:, 0:256] = dcq
        dlat_ref[:, 256:384] = dckv
        dlat_ref[:, K_R_OFF:K_R_OFF + LANES] = pltpu.roll(dkr, 64, 1)
        _acc_rows(dgq_ref, dgqr, first)
        _acc_rows(dgkv_ref, dgkvr, first)

    full = lambda shape: pl.BlockSpec(shape, lambda i: (0, 0))
    rowb = lambda n: pl.BlockSpec((tm, n), lambda i: (i, 0))
    sds = lambda n, dt: jax.ShapeDtypeStruct((s_len, n), dt)
    return pl.pallas_call(
        body, name="mla_prep_bwd", grid=(s_len // tm,),
        in_specs=[rowb(512), full((1, 256)), full((1, 128)), full((256, 1024)), full((128, 1024)),
                  full((128, 512)), rowb(128), rowb(128), rowb(1024), rowb(1024), rowb(512)],
        out_specs=[rowb(512), rowb(1024), rowb(1024), rowb(512), full((1, 256)), full((1, 128))],
        out_shape=[sds(512, F32), sds(1024, BF16), sds(1024, BF16), sds(512, BF16),
                   jax.ShapeDtypeStruct((1, 256), F32), jax.ShapeDtypeStruct((1, 128), F32)],
        compiler_params=_cparams(48, dimension_semantics=("arbitrary",)),
    )(lat, g_q, g_kv, w_uq, w_uk, w_uv, cosf, sinf, dq, dk, dv)


def _in_proj_bwd(x, g, w, dx1, dlat, dsbq, dsbk, dsbv, dgates):
    s_len = x.shape[0]
    tm, tn = _row_block(s_len), 512
    nj = D_IN_PAD // tn

    def body(x_ref, g_ref, w_ref, dx1_ref, dlat_ref, dq_ref, dk_ref, dv_ref, dgate_ref,
             gx_ref, dproj_ref, dg_ref, dh_scr):
        i, j = pl.program_id(0), pl.program_id(1)

        @pl.when(j == 0)
        def _():
            dh_scr[...] = jnp.zeros_like(dh_scr)

        def chunk(val):
            vb = val.astype(BF16)
            dproj_ref[...] = vb
            dh_scr[...] += _dot_nt(vb, w_ref[...])

        for jj, ref in ((0, dlat_ref), (1, dq_ref), (2, dk_ref), (3, dv_ref)):
            pl.when(j == jj)(functools.partial(lambda ref: chunk(ref[...]), ref))
        pl.when(j >= 4)(lambda: chunk(dgate_ref[...]))

        @pl.when(j == nj - 1)
        def _():
            gv = g_ref[...]
            _, xh, r = _rms(x_ref[...], gv)
            dx, dgr = _rms_bwd(dh_scr[...], xh, r, gv)
            gx_ref[...] = dx1_ref[...] + dx
            _acc_rows(dg_ref, dgr, i == 0)

    rowb = pl.BlockSpec((tm, D_MODEL), lambda i, j: (i, 0))
    colb = lambda f: pl.BlockSpec((tm, tn), f)
    return pl.pallas_call(
        body, name="in_proj_bwd", grid=(s_len // tm, nj),
        in_specs=[rowb, pl.BlockSpec((1, D_MODEL), lambda i, j: (0, 0)),
                  pl.BlockSpec((D_MODEL, tn), lambda i, j: (0, j)), rowb,
                  colb(lambda i, j: (i, 0)), colb(lambda i, j: (i, 0)), colb(lambda i, j: (i, 0)),
                  colb(lambda i, j: (i, 0)), colb(lambda i, j: (i, jnp.clip(j - 4, 0, 3)))],
        out_specs=[rowb, colb(lambda i, j: (i, j)), pl.BlockSpec((1, D_MODEL), lambda i, j: (0, 0))],
        out_shape=[jax.ShapeDtypeStruct((s_len, D_MODEL), F32), jax.ShapeDtypeStruct((s_len, D_IN_PAD), BF16),
                   jax.ShapeDtypeStruct((1, D_MODEL), F32)],
        scratch_shapes=[pltpu.VMEM((tm, D_MODEL), F32)],
        compiler_params=_cparams(48, dimension_semantics=("arbitrary", "arbitrary")),
    )(x, g, w, dx1, dlat, dsbq, dsbk, dsbv, dgates)


def _adamw(landed, w, m, v):
    rows = w.shape[0]
    tb = min(rows, ADAM_BLOCK_ROWS)
    c1 = 1.0 - ADAM_B1 ** ADAM_STEP
    c2 = 1.0 - ADAM_B2 ** ADAM_STEP

    def body(l_ref, w_ref, m_ref, v_ref, g_ref, d_ref, nm_ref, nv_ref):
        g = l_ref[0]
        for k in range(1, N_DEV):
            g = g + l_ref[k]
        nm = ADAM_B1 * m_ref[...] + (1.0 - ADAM_B1) * g
        nv = ADAM_B2 * v_ref[...] + (1.0 - ADAM_B2) * (g * g)
        g_ref[...] = g
        nm_ref[...] = nm
        nv_ref[...] = nv
        d_ref[...] = -ADAM_LR * ((nm / c1) / (jnp.sqrt(nv / c2) + ADAM_EPS) + ADAM_WD * w_ref[...])

    blk = pl.BlockSpec((tb, LANES), lambda i: (i, 0))
    return pl.pallas_call(
        body, name="reduce_adamw", grid=(rows // tb,),
        in_specs=[pl.BlockSpec((N_DEV, tb, LANES), lambda i: (0, i, 0)), blk, blk, blk],
        out_specs=[blk, blk, blk, blk],
        out_shape=[jax.ShapeDtypeStruct((rows, LANES), F32)] * 4,
        compiler_params=_cparams(dimension_semantics=("parallel",)),
    )(landed, w, m, v)


def _padded(n):
    return _round_up(n, PACK_ALIGN)


def _shard_shape(shape, axis):
    return tuple(d // N_DEV if a == axis else d for a, d in enumerate(shape))


def _pack_flat(parts, total):
    flat = []
    used = 0
    for p in parts:
        p = p.reshape(-1)
        n = _padded(p.shape[0])
        flat.append(jnp.pad(p, (0, n - p.shape[0])))
        used += n
    if total > used:
        flat.append(jnp.zeros((total - used,), parts[0].dtype))
    return jnp.concatenate(flat)


def _split_pieces(full, axis):
    r, c = full.shape
    if axis == 0:
        return full.reshape(N_DEV, (r // N_DEV) * c)
    return full.reshape(r, N_DEV, c // N_DEV).transpose(1, 0, 2).reshape(N_DEV, r * (c // N_DEV))


def _join_shards(gathered, shape, axis):
    r, c = shape
    if axis == 0:
        return gathered.reshape(r, c)
    return gathered.reshape(N_DEV, r, c // N_DEV).transpose(1, 0, 2).reshape(r, c)


def kernel(x, mem, positions, g_mix, w_in, b_gate, g_q_lat, w_uq, g_kv_lat, w_ukv, w_a_proj, w_b_proj, w_o, g_x, g_mem, w_xq, w_xkv, w_xo, g_ffn, w_gate, w_up, w_down, g_final, loss_target, m_g_mix, m_w_in, m_b_gate, m_g_q_lat, m_w_uq, m_g_kv_lat, m_w_ukv, m_w_a_proj, m_w_b_proj, m_w_o, m_g_x, m_g_mem, m_w_xq, m_w_xkv, m_w_xo, m_g_ffn, m_w_gate, m_w_up, m_w_down, m_g_final, v_g_mix, v_w_in, v_b_gate, v_g_q_lat, v_w_uq, v_g_kv_lat, v_w_ukv, v_w_a_proj, v_w_b_proj, v_w_o, v_g_x, v_g_mem, v_w_xq, v_w_xkv, v_w_xo, v_g_ffn, v_w_gate, v_w_up, v_w_down, v_g_final):
    given = dict(locals())
    s_len = x.shape[1]
    x2d = x.reshape(s_len, D_MODEL)
    mem2d = mem.reshape(-1, D_MODEL)
    target = loss_target.reshape(s_len, D_MODEL)

    parts = []
    for name, shape, axis in SHARDED:
        a = given[name].reshape(_shard_shape(shape, axis))
        if name == "b_gate":
            parts.append(lax.bitcast_convert_type(a, BF16))
        else:
            parts.append(a.astype(BF16))
    n_bf = sum(_padded(math.prod(p.shape)) for p in parts)
    gathered = _all_gather(_pack_flat(parts, n_bf).reshape(n_bf // LANES, LANES)).reshape(N_DEV, n_bf)
    wts = {}
    off = 0
    for (name, shape, axis), p in zip(SHARDED, parts):
        n = math.prod(p.shape)
        piece = gathered[:, off:off + n]
        off += _padded(n)
        if name == "b_gate":
            piece = lax.bitcast_convert_type(piece.reshape(N_DEV, n // 2, 2), F32)
        wts[name] = _join_shards(piece, shape, axis)

    w_in_p = jnp.concatenate([wts["w_in"][:, :416], jnp.zeros((D_MODEL, 96), BF16), wts["w_in"][:, 416:]], axis=1)
    w_uq_p = jnp.pad(wts["w_uq"].reshape(256, MLA_HEADS, 96), ((0, 0), (0, 0), (0, 32))).reshape(256, 1024)
    ukv = wts["w_ukv"].reshape(128, MLA_HEADS, 128)
    w_uk_p = jnp.pad(ukv[:, :, :64], ((0, 0), (0, 0), (0, 64))).reshape(128, 1024)
    w_uv = ukv[:, :, 64:].reshape(128, 512)
    bg = wts["b_gate"]

    inv_freq = ROPE_THETA ** (-jnp.arange(0, MLA_ROPE, 2, dtype=F32) / MLA_ROPE)
    ang = positions.reshape(s_len).astype(F32)[:, None] * inv_freq
    cos16, sin16 = jnp.cos(ang), jnp.sin(ang)
    cosf = jnp.concatenate([jnp.ones((s_len, 64), F32), cos16, cos16, jnp.ones((s_len, 32), F32)], axis=1)
    sinf = jnp.concatenate([jnp.zeros((s_len, 64), F32), sin16, sin16, jnp.zeros((s_len, 32), F32)], axis=1)

    h1, lat, sb, gates = _in_proj(x2d, g_mix, w_in_p)
    qa, ka, va, q_lat, kv_lat = _mla_prep(lat, g_q_lat, g_kv_lat, w_uq_p, w_uk_p, w_uv, cosf, sinf)
    oa, lse = _mla_fwd(qa, ka, va)
    ob, sb_r = _sb_fwd(sb)
    x1 = _merge_fwd(x2d, oa, ob, gates, bg, wts["w_a_proj"], wts["w_b_proj"], wts["w_o"])
    mn, xkv = _mem_kv(mem2d, g_mem, wts["w_xkv"])
    x2 = _xattn_fwd(x1, g_x, wts["w_xq"], xkv, wts["w_xo"])
    x3 = _ffn_fwd(x2, g_ffn, wts["w_gate"], wts["w_up"], wts["w_down"])
    g_final2d = g_final.reshape(1, D_MODEL)
    sse, dx3, dx3b, dg_final = _loss_head(x3, g_final2d, target)
    loss = lax.psum(sse[0, 0] * (0.5 / D_MODEL), ("x", "y", "c"))

    dx2, hf, dgt, dup, act, dg_ffn = _ffn_bwd(x2, dx3, g_ffn, wts["w_gate"], wts["w_up"], wts["w_down"])
    dx1, dw_xq, dw_xo, dxkv, dg_x = _xattn_bwd(x1, dx2, g_x, wts["w_xq"], xkv, wts["w_xo"])
    dw_xkv, dg_mem = _mem_bwd(mem2d, g_mem, wts["w_xkv"], mn, dxkv)
    doa, dob, dgates, dpa, dpb, merged, dx1b, dbg = _merge_bwd(
        dx1, oa, ob, gates, bg, wts["w_a_proj"], wts["w_b_proj"], wts["w_o"])
    dsbq, dsbk, dsbv = _sb_bwd(sb, dob, sb_r)
    dqa, dka, dva = _mla_bwd(qa, ka, va, oa, doa, lse)
    dlat, dqb, dkb, dvb, dg_q, dg_kv = _mla_prep_bwd(
        lat, g_q_lat, g_kv_lat, w_uq_p, w_uk_p, w_uv, cosf, sinf, dqa, dka, dva)
    grad_x, dproj, dg_mix = _in_proj_bwd(x2d, g_mix, w_in_p, dx1, dlat, dsbq, dsbk, dsbv, dgates)

    dw_in_p = _tn_matmul(h1, dproj, "dw_in")
    dw_uq_p = _tn_matmul(q_lat, dqb, "dw_uq")
    dw_uk_p = _tn_matmul(kv_lat, dkb, "dw_uk")
    dw_uv = _tn_matmul(kv_lat, dvb, "dw_uv")
    full_grads = {
        "w_in": jnp.concatenate([dw_in_p[:, :416], dw_in_p[:, 512:]], axis=1),
        "b_gate": dbg,
        "w_uq": dw_uq_p.reshape(256, MLA_HEADS, 128)[:, :, :96].reshape(256, 768),
        "w_ukv": jnp.concatenate([dw_uk_p.reshape(128, MLA_HEADS, 128)[:, :, :64],
                                  dw_uv.reshape(128, MLA_HEADS, 64)], axis=2).reshape(128, 1024),
        "w_a_proj": _tn_matmul(oa, dpa, "dw_a"),
        "w_b_proj": _tn_matmul(ob, dpb, "dw_b"),
        "w_o": _tn_matmul(merged, dx1b, "dw_o"),
        "w_xq": dw_xq,
        "w_xkv": dw_xkv,
        "w_xo": dw_xo,
        "w_gate": _tn_matmul(hf, dgt, "dw_gate", tn=FF_TILE),
        "w_up": _tn_matmul(hf, dup, "dw_up", tn=FF_TILE),
        "w_down": _tn_matmul(act, dx3b, "dw_down", tka=FF_TILE),
    }
    rep_grads = {"g_mix": dg_mix, "g_q_lat": dg_q, "g_kv_lat": dg_kv, "g_x": dg_x, "g_mem": dg_mem,
                 "g_ffn": dg_ffn, "g_final": dg_final}

    n_sh = sum(_padded(math.prod(_shard_shape(shape, axis))) for _, shape, axis in SHARDED)
    n_rep = _padded(sum(n for _, n in REPLICATED))
    n_all = _round_up(n_sh + n_rep, ADAM_BLOCK_ROWS * LANES)
    rows_all = n_all // LANES
    cols = []
    for name, shape, axis in SHARDED:
        pc = _split_pieces(full_grads[name], axis)
        cols.append(jnp.pad(pc, ((0, 0), (0, _padded(pc.shape[1]) - pc.shape[1]))))
    rep = jnp.concatenate([rep_grads[name].reshape(-1) for name, _ in REPLICATED])
    cols.append(jnp.broadcast_to(jnp.pad(rep, (0, n_all - n_sh - rep.shape[0])), (N_DEV, n_all - n_sh)))
    landed = _all_to_all(jnp.concatenate(cols, axis=1).reshape(N_DEV, rows_all, LANES))

    def pack_local(prefix):
        ps = [given[prefix + name] for name, _, _ in SHARDED] + [given[prefix + name] for name, _ in REPLICATED]
        flat = []
        for p in ps[:len(SHARDED)]:
            p = p.reshape(-1)
            flat.append(jnp.pad(p, (0, _padded(p.shape[0]) - p.shape[0])))
        flat.append(jnp.concatenate([p.reshape(-1) for p in ps[len(SHARDED):]]))
        flat = jnp.concatenate(flat)
        return jnp.pad(flat, (0, n_all - flat.shape[0])).reshape(rows_all, LANES)

    outs = _adamw(landed, pack_local(""), pack_local("m_"), pack_local("v_"))
    outs = [o.reshape(n_all) for o in outs]

    def unpack(flat):
        res = {}
        off = 0
        for name, shape, axis in SHARDED:
            n = math.prod(_shard_shape(shape, axis))
            res[name] = flat[off:off + n].reshape(given[name].shape)
            off += _padded(n)
        for name, n in REPLICATED:
            res[name] = flat[off:off + n].reshape(given[name].shape)
            off += n
        return res

    groups = [unpack(o) for o in outs]
    result = [loss, grad_x.reshape(x.shape)]
    for grp in groups:
        result.extend(grp[name] for name in WEIGHT_ORDER)
    return tuple(result)
```

```python
import functools
import math

import jax
import jax.numpy as jnp
from jax import lax
from jax.experimental import pallas as pl
from jax.experimental.pallas import tpu as pltpu

F32 = jnp.float32
BF16 = jnp.bfloat16

D_MODEL = 1024
MLA_HEADS = 8
MLA_Q_RANK = 256
MLA_KV_RANK = 128
MLA_NOPE = 64
MLA_ROPE = 32
MLA_V = 64
ROPE_THETA = 10000.0
SB_WIDTH = 512
X_HEADS = 4
X_HEAD_DIM = 128
D_FF = 2816
EPS = 1e-6
D_IN = 4000
D_IN_PAD = 4096
K_R_OFF = 384
LANES = 128
HEAD_PAD = 128
MLA_SCALE = 1.0 / math.sqrt(MLA_NOPE + MLA_ROPE)
SB_SCALE = 0.125
X_SCALE = 1.0 / math.sqrt(X_HEAD_DIM)
NEG_BIG = -1e30

ADAM_LR = 0.001
ADAM_B1 = 0.9
ADAM_B2 = 0.999
ADAM_EPS = 1e-08
ADAM_WD = 0.01
ADAM_STEP = 10

N_DEV = 8
PACK_ALIGN = 2048
ADAM_BLOCK_ROWS = 1024
MIB = 1024 * 1024

SHARDED = (
    ("w_in", (D_MODEL, D_IN), 1),
    ("b_gate", (2, D_MODEL), 1),
    ("w_uq", (MLA_Q_RANK, 768), 1),
    ("w_ukv", (MLA_KV_RANK, 1024), 1),
    ("w_a_proj", (512, D_MODEL), 1),
    ("w_b_proj", (512, D_MODEL), 1),
    ("w_o", (D_MODEL, D_MODEL), 0),
    ("w_xq", (D_MODEL, 512), 0),
    ("w_xkv", (D_MODEL, 1024), 0),
    ("w_xo", (512, D_MODEL), 1),
    ("w_gate", (D_MODEL, D_FF), 1),
    ("w_up", (D_MODEL, D_FF), 1),
    ("w_down", (D_FF, D_MODEL), 0),
)
REPLICATED = (
    ("g_mix", 1024), ("g_q_lat", 256), ("g_kv_lat", 128), ("g_x", 1024),
    ("g_mem", 1024), ("g_ffn", 1024), ("g_final", 1024),
)
WEIGHT_ORDER = ("g_mix", "w_in", "b_gate", "g_q_lat", "w_uq", "g_kv_lat", "w_ukv", "w_a_proj",
                "w_b_proj", "w_o", "g_x", "g_mem", "w_xq", "w_xkv", "w_xo", "g_ffn", "w_gate",
                "w_up", "w_down", "g_final")


def _round_up(n, m):
    return -(-n // m) * m


def _cparams(vmem_mib=None, **kw):
    if vmem_mib is not None:
        kw["vmem_limit_bytes"] = vmem_mib * MIB
    return pltpu.CompilerParams(**kw)


def _dot(a, b):
    return jnp.dot(a, b, preferred_element_type=F32)


def _dot_nt(a, b):
    return lax.dot_general(a, b, (((1,), (1,)), ((), ())), preferred_element_type=F32)


def _dot_tn(a, b):
    return lax.dot_general(a, b, (((0,), (0,)), ((), ())), preferred_element_type=F32)


def _rms(x, g):
    r = lax.rsqrt(jnp.mean(x * x, axis=-1, keepdims=True) + EPS)
    xh = x * r
    return xh * g, xh, r


def _rms_bwd(dy, xh, r, g):
    u = dy * g
    dx = r * (u - xh * jnp.mean(u * xh, axis=-1, keepdims=True))
    return dx, dy * xh


def _sigmoid(z):
    return 1.0 / (1.0 + jnp.exp(-z))


def _acc_rows(ref, val, first):
    s = jnp.sum(val, axis=0, keepdims=True)

    @pl.when(first)
    def _():
        ref[...] = s

    @pl.when(jnp.logical_not(first))
    def _():
        ref[...] += s


def _acc(ref, val, first):
    @pl.when(first)
    def _():
        ref[...] = val

    @pl.when(jnp.logical_not(first))
    def _():
        ref[...] += val


def _peer(k):
    x, y, c = lax.axis_index("x"), lax.axis_index("y"), lax.axis_index("c")
    px = 1 - x if (k >> 2) & 1 else x
    py = 1 - y if (k >> 1) & 1 else y
    pc = 1 - c if k & 1 else c
    return (px, py, pc), 4 * px + 2 * py + pc


def _all_gather(shard):
    rows = shard.shape[0]

    def body(src, out, send_sems, recv_sems, local_sem):
        _, me = _peer(0)
        mine = pltpu.make_async_copy(src, out.at[me], local_sem)
        mine.start()
        sends = []
        for k in range(1, N_DEV):
            peer, _ = _peer(k)
            cp = pltpu.make_async_remote_copy(
                src_ref=src, dst_ref=out.at[me], send_sem=send_sems.at[k - 1],
                recv_sem=recv_sems.at[k - 1], device_id=peer, device_id_type=pl.DeviceIdType.MESH)
            cp.start()
            sends.append(cp)
        for k in range(1, N_DEV):
            peer, pid = _peer(k)
            pltpu.make_async_remote_copy(
                src_ref=src, dst_ref=out.at[pid], send_sem=send_sems.at[k - 1],
                recv_sem=recv_sems.at[k - 1], device_id=peer,
                device_id_type=pl.DeviceIdType.MESH).wait_recv()
        for cp in sends:
            cp.wait_send()
        mine.wait()

    return pl.pallas_call(
        body, name="weights_all_gather",
        out_shape=jax.ShapeDtypeStruct((N_DEV, rows, LANES), shard.dtype),
        in_specs=[pl.BlockSpec(memory_space=pl.ANY)],
        out_specs=pl.BlockSpec(memory_space=pl.ANY),
        scratch_shapes=[pltpu.SemaphoreType.DMA((N_DEV - 1,)), pltpu.SemaphoreType.DMA((N_DEV - 1,)),
                        pltpu.SemaphoreType.DMA(())],
    )(shard)


def _all_to_all(pieces):
    rows = pieces.shape[1]

    def body(src, out, send_sems, recv_sems, local_sem):
        _, me = _peer(0)
        mine = pltpu.make_async_copy(src.at[me], out.at[me], local_sem)
        mine.start()
        sends = []
        for k in range(1, N_DEV):
            peer, pid = _peer(k)
            cp = pltpu.make_async_remote_copy(
                src_ref=src.at[pid], dst_ref=out.at[me], send_sem=send_sems.at[k - 1],
                recv_sem=recv_sems.at[k - 1], device_id=peer, device_id_type=pl.DeviceIdType.MESH)
            cp.start()
            sends.append(cp)
        for k in range(1, N_DEV):
            peer, pid = _peer(k)
            pltpu.make_async_remote_copy(
                src_ref=src.at[pid], dst_ref=out.at[pid], send_sem=send_sems.at[k - 1],
                recv_sem=recv_sems.at[k - 1], device_id=peer,
                device_id_type=pl.DeviceIdType.MESH).wait_recv()
        for cp in sends:
            cp.wait_send()
        mine.wait()

    return pl.pallas_call(
        body, name="grads_all_to_all",
        out_shape=jax.ShapeDtypeStruct((N_DEV, rows, LANES), pieces.dtype),
        in_specs=[pl.BlockSpec(memory_space=pl.ANY)],
        out_specs=pl.BlockSpec(memory_space=pl.ANY),
        scratch_shapes=[pltpu.SemaphoreType.DMA((N_DEV - 1,)), pltpu.SemaphoreType.DMA((N_DEV - 1,)),
                        pltpu.SemaphoreType.DMA(())],
    )(pieces)


def _tn_matmul(a, b, name, tka=512, tn=1024, ts=512):
    s_len, ka = a.shape
    n = b.shape[1]
    tka, tn, ts = min(tka, ka), min(tn, n), min(ts, s_len)
    assert ka % tka == 0 and n % tn == 0 and s_len % ts == 0

    def body(a_ref, b_ref, o_ref):
        _acc(o_ref, _dot_tn(a_ref[...], b_ref[...]), pl.program_id(2) == 0)

    return pl.pallas_call(
        body, name=name, grid=(ka // tka, n // tn, s_len // ts),
        in_specs=[pl.BlockSpec((ts, tka), lambda i, j, s: (s, i)),
                  pl.BlockSpec((ts, tn), lambda i, j, s: (s, j))],
        out_specs=pl.BlockSpec((tka, tn), lambda i, j, s: (i, j)),
        out_shape=jax.ShapeDtypeStruct((ka, n), F32),
        compiler_params=_cparams(dimension_semantics=("parallel", "parallel", "arbitrary")),
    )(a, b)


def _row_block(s_len):
    return min(s_len, 512)


def _in_proj(x, g, w):
    s_len = x.shape[0]
    tm, tn = _row_block(s_len), 512

    def body(x_ref, g_ref, w_ref, h_ref, lat_ref, sb_ref, gate_ref, h_scr):
        j = pl.program_id(1)

        @pl.when(j == 0)
        def _():
            h, _, _ = _rms(x_ref[...], g_ref[...])
            hb = h.astype(BF16)
            h_scr[...] = hb
            h_ref[...] = hb

        p = _dot(h_scr[...], w_ref[...])

        @pl.when(j == 0)
        def _():
            lat_ref[...] = p

        @pl.when(j == 1)
        def _():
            sb_ref[...] = (p * SB_SCALE).astype(BF16)

        @pl.when((j == 2) | (j == 3))
        def _():
            sb_ref[...] = p.astype(BF16)

        @pl.when(j >= 4)
        def _():
            gate_ref[...] = p

    return pl.pallas_call(
        body, name="in_proj", grid=(s_len // tm, D_IN_PAD // tn),
        in_specs=[pl.BlockSpec((tm, D_MODEL), lambda i, j: (i, 0)),
                  pl.BlockSpec((1, D_MODEL), lambda i, j: (0, 0)),
                  pl.BlockSpec((D_MODEL, tn), lambda i, j: (0, j))],
        out_specs=[pl.BlockSpec((tm, D_MODEL), lambda i, j: (i, 0)),
                   pl.BlockSpec((tm, tn), lambda i, j: (i, 0)),
                   pl.BlockSpec((tm, tn), lambda i, j: (i, jnp.clip(j - 1, 0, 2))),
                   pl.BlockSpec((tm, tn), lambda i, j: (i, jnp.clip(j - 4, 0, 3)))],
        out_shape=[jax.ShapeDtypeStruct((s_len, D_MODEL), BF16),
                   jax.ShapeDtypeStruct((s_len, 512), F32),
                   jax.ShapeDtypeStruct((s_len, 3 * SB_WIDTH), BF16),
                   jax.ShapeDtypeStruct((s_len, 2 * D_MODEL), F32)],
        scratch_shapes=[pltpu.VMEM((tm, D_MODEL), BF16)],
        compiler_params=_cparams(dimension_semantics=("parallel", "arbitrary")),
    )(x, g, w)


def _rope_rot(blk, lane):
    return jnp.where(lane < 80, -pltpu.roll(blk, 112, 1), pltpu.roll(blk, 16, 1))


def _rope_rot_t(blk, lane):
    return jnp.where(lane < 80, pltpu.roll(blk, 112, 1), -pltpu.roll(blk, 16, 1))


def _mla_prep(lat, g_q, g_kv, w_uq, w_uk, w_uv, cosf, sinf):
    s_len = lat.shape[0]
    tm = _row_block(s_len)

    def body(lat_ref, gq_ref, gkv_ref, wuq_ref, wuk_ref, wuv_ref, cos_ref, sin_ref,
             q_ref, k_ref, v_ref, ql_ref, kvl_ref):
        lane = lax.broadcasted_iota(jnp.int32, (tm, LANES), 1)
        cosv, sinv = cos_ref[...], sin_ref[...]
        ql, _, _ = _rms(lat_ref[:, 0:256], gq_ref[...])
        kvl, _, _ = _rms(lat_ref[:, 256:384], gkv_ref[...])
        qlb, kvlb = ql.astype(BF16), kvl.astype(BF16)
        ql_ref[...] = qlb
        kvl_ref[...] = kvlb
        q = _dot(qlb, wuq_ref[...])
        kn = _dot(kvlb, wuk_ref[...])
        v_ref[...] = _dot(kvlb, wuv_ref[...]).astype(BF16)
        kr = pltpu.roll(lat_ref[:, K_R_OFF:K_R_OFF + LANES], 64, 1)
        kr = kr * cosv + _rope_rot(kr, lane) * sinv
        for h in range(MLA_HEADS):
            sl = slice(h * HEAD_PAD, (h + 1) * HEAD_PAD)
            blk = q[:, sl]
            q_ref[:, sl] = (blk * cosv + _rope_rot(blk, lane) * sinv).astype(BF16)
            k_ref[:, sl] = (kn[:, sl] + kr).astype(BF16)

    full = lambda shape: pl.BlockSpec(shape, lambda i: (0, 0))
    rowb = lambda n: pl.BlockSpec((tm, n), lambda i: (i, 0))
    return pl.pallas_call(
        body, name="mla_prep", grid=(s_len // tm,),
        in_specs=[rowb(512), full((1, 256)), full((1, 128)), full((256, 1024)), full((128, 1024)),
                  full((128, 512)), rowb(128), rowb(128)],
        out_specs=[rowb(1024), rowb(1024), rowb(512), rowb(256), rowb(128)],
        out_shape=[jax.ShapeDtypeStruct((s_len, 1024), BF16), jax.ShapeDtypeStruct((s_len, 1024), BF16),
                   jax.ShapeDtypeStruct((s_len, 512), BF16), jax.ShapeDtypeStruct((s_len, 256), BF16),
                   jax.ShapeDtypeStruct((s_len, 128), BF16)],
        compiler_params=_cparams(dimension_semantics=("parallel",)),
    )(lat, g_q, g_kv, w_uq, w_uk, w_uv, cosf, sinf)


ATTN_TQ = 1024
ATTN_TH = 512
ATTN_TK = 256


def _attn_blocks(s_len):
    tq, th, tk = min(s_len, ATTN_TQ), min(s_len, ATTN_TH), min(s_len, ATTN_TK)
    return tq, th, tk, tq // tk


def _chains(tq, th):
    return [(hh, r0) for hh in range(2) for r0 in range(0, tq, th)]


def _diag_mask(th, tk, r0, sub, strict):
    lo, hi = sub * tk, (sub + 1) * tk - 1
    last, first = r0 + th - 1, r0
    if (lo >= last) if strict else (lo > last):
        return "none"
    if (hi < first) if strict else (hi <= first):
        return "all"
    row = lax.broadcasted_iota(jnp.int32, (th, tk), 0) + r0
    col = lax.broadcasted_iota(jnp.int32, (th, tk), 1) + lo
    return col < row if strict else col <= row


def _mla_fwd(q, k, v):
    s_len = q.shape[0]
    tq, th, tk, nsub = _attn_blocks(s_len)
    chains = _chains(tq, th)
    nh = tq // th

    def body(q_ref, k_ref, v_ref, o_ref, lse_ref):
        i = pl.program_id(1)
        lane = lax.broadcasted_iota(jnp.int32, (th, LANES), 1)
        hsl = [slice(hh * HEAD_PAD, (hh + 1) * HEAD_PAD) for hh in range(2)]

        def step(kb, carry, sub):
            rows = pl.ds(pl.multiple_of(kb * tk, tk), tk)
            vblk = v_ref[rows, :]
            new = []
            for (hh, r0), (m, l, acc) in zip(chains, carry):
                mask = "all" if sub is None else _diag_mask(th, tk, r0, sub, strict=False)
                if isinstance(mask, str) and mask == "none":
                    new.append((m, l, acc))
                    continue
                s = _dot_nt(q_ref[r0:r0 + th, hsl[hh]], k_ref[rows, hsl[hh]]) * MLA_SCALE
                if not isinstance(mask, str):
                    s = jnp.where(mask, s, NEG_BIG)
                m_new = jnp.maximum(m, jnp.max(s, axis=-1, keepdims=True))
                alpha = jnp.exp(m - m_new)
                p = jnp.exp(s - m_new)
                l = alpha * l + jnp.sum(p, axis=-1, keepdims=True)
                acc = alpha * acc + _dot(p.astype(BF16), vblk)
                new.append((m_new, l, acc))
            return tuple(new)

        init = (jnp.full((th, 1), NEG_BIG, F32), jnp.zeros((th, 1), F32), jnp.zeros((th, LANES), F32))
        carry = lax.fori_loop(0, i * nsub, lambda kb, cy: step(kb, cy, None), (init,) * len(chains))
        for sub in range(nsub):
            carry = step(i * nsub + sub, carry, sub)
        for c in range(nh):
            (m0, l0, a0), (m1, l1, a1) = carry[c], carry[nh + c]
            rs = slice(c * th, (c + 1) * th)
            o_ref[rs, :] = jnp.where(lane < 64, a0 / l0, a1 / l1).astype(BF16)
            lse_ref[rs, :] = jnp.where(lane < 64, m0 + jnp.log(l0), m1 + jnp.log(l1))

    return pl.pallas_call(
        body, name="mla_fwd", grid=(4, s_len // tq),
        in_specs=[pl.BlockSpec((tq, 2 * HEAD_PAD), lambda p, i: (i, p)),
                  pl.BlockSpec((s_len, 2 * HEAD_PAD), lambda p, i: (0, p)),
                  pl.BlockSpec((s_len, LANES), lambda p, i: (0, p))],
        out_specs=[pl.BlockSpec((tq, LANES), lambda p, i: (i, p)),
                   pl.BlockSpec((None, tq, LANES), lambda p, i: (p, i, 0))],
        out_shape=[jax.ShapeDtypeStruct((s_len, 512), BF16), jax.ShapeDtypeStruct((4, s_len, LANES), F32)],
        compiler_params=_cparams(40, dimension_semantics=("parallel", "arbitrary")),
    )(q, k, v)


def _mla_bwd(q, k, v, o, do, lse):
    s_len = q.shape[0]
    tq, th, tk, nsub = _attn_blocks(s_len)
    chains = _chains(tq, th)

    def body(q_ref, k_ref, v_ref, o_ref, do_ref, lse_ref, dq_ref, dk_ref, dv_ref):
        i = pl.program_id(1)
        lane = lax.broadcasted_iota(jnp.int32, (th, LANES), 1)

        @pl.when(i == 0)
        def _():
            dk_ref[...] = jnp.zeros_like(dk_ref)
            dv_ref[...] = jnp.zeros_like(dv_ref)

        hsl = [slice(hh * HEAD_PAD, (hh + 1) * HEAD_PAD) for hh in range(2)]
        qs, dos, deltas, lses = [], [], [], []
        for hh, r0 in chains:
            rs = slice(r0, r0 + th)
            qs.append(q_ref[rs, hsl[hh]])
            doh = jnp.where((lane // 64) == hh, do_ref[rs, :], jnp.zeros((), BF16))
            dos.append(doh)
            deltas.append(jnp.sum(doh.astype(F32) * o_ref[rs, :].astype(F32), axis=-1, keepdims=True))
            lses.append(lse_ref[rs, 64 * hh:64 * hh + 1])

        def step(kb, dqs, sub):
            rows = pl.ds(pl.multiple_of(kb * tk, tk), tk)
            vblk = v_ref[rows, :]
            new, p_all, do_all = [], [], []
            ds_h, q_h = [[], []], [[], []]
            for c, (hh, r0) in enumerate(chains):
                mask = "all" if sub is None else _diag_mask(th, tk, r0, sub, strict=False)
                if isinstance(mask, str) and mask == "none":
                    new.append(dqs[c])
                    continue
                kblk = k_ref[rows, hsl[hh]]
                s = _dot_nt(qs[c], kblk) * MLA_SCALE
                if not isinstance(mask, str):
                    s = jnp.where(mask, s, NEG_BIG)
                p = jnp.exp(s - lses[c])
                dp = _dot_nt(dos[c], vblk)
                ds = (p * (dp - deltas[c]) * MLA_SCALE).astype(BF16)
                p_all.append(p.astype(BF16))
                do_all.append(dos[c])
                ds_h[hh].append(ds)
                q_h[hh].append(qs[c])
                new.append(dqs[c] + _dot(ds, kblk))
            dv_ref[rows, :] += _dot_tn(jnp.concatenate(p_all, axis=0), jnp.concatenate(do_all, axis=0))
            for hh in range(2):
                dk_ref[rows, hsl[hh]] += _dot_tn(jnp.concatenate(ds_h[hh], axis=0),
                                                 jnp.concatenate(q_h[hh], axis=0))
            return tuple(new)

        zero = jnp.zeros((th, LANES), F32)
        dqs = lax.fori_loop(0, i * nsub, lambda kb, cy: step(kb, cy, None), (zero,) * len(chains))
        for sub in range(nsub):
            dqs = step(i * nsub + sub, dqs, sub)
        for c, (hh, r0) in enumerate(chains):
            dq_ref[r0:r0 + th, hsl[hh]] = dqs[c]

    return pl.pallas_call(
        body, name="mla_bwd", grid=(4, s_len // tq),
        in_specs=[pl.BlockSpec((tq, 2 * HEAD_PAD), lambda p, i: (i, p)),
                  pl.BlockSpec((s_len, 2 * HEAD_PAD), lambda p, i: (0, p)),
                  pl.BlockSpec((s_len, LANES), lambda p, i: (0, p)),
                  pl.BlockSpec((tq, LANES), lambda p, i: (i, p)),
                  pl.BlockSpec((tq, LANES), lambda p, i: (i, p)),
                  pl.BlockSpec((None, tq, LANES), lambda p, i: (p, i, 0))],
        out_specs=[pl.BlockSpec((tq, 2 * HEAD_PAD), lambda p, i: (i, p)),
                   pl.BlockSpec((s_len, 2 * HEAD_PAD), lambda p, i: (0, p)),
                   pl.BlockSpec((s_len, LANES), lambda p, i: (0, p))],
        out_shape=[jax.ShapeDtypeStruct((s_len, 1024), F32), jax.ShapeDtypeStruct((s_len, 1024), F32),
                   jax.ShapeDtypeStruct((s_len, 512), F32)],
        compiler_params=_cparams(56, dimension_semantics=("arbitrary", "arbitrary")),
    )(q, k, v, o, do, lse)


def _log_sigmoids(z):
    sp = jnp.log(1.0 + jnp.exp(-jnp.abs(z)))
    return jnp.minimum(z, 0.0) - sp, jnp.minimum(-z, 0.0) - sp


def _split_dot(x, w, parts, nt=False):
    dot = _dot_nt if nt else _dot
    out = None
    for _ in range(parts):
        xb = x.astype(BF16)
        t = dot(xb, w)
        out = t if out is None else out + t
        x = x - xb.astype(F32)
    return out


def _sb_fwd(sb):
    s_len = sb.shape[0]
    tq, th, tk, nsub = _attn_blocks(s_len)
    chains = _chains(tq, th)
    nh = tq // th
    assert s_len // tk <= 64

    def body(q_ref, k_ref, v_ref, o_ref, r_ref):
        i = pl.program_id(1)
        lane = lax.broadcasted_iota(jnp.int32, (th, LANES), 1)
        upper = (lax.broadcasted_iota(jnp.int32, (tk, tk), 0)
                 > lax.broadcasted_iota(jnp.int32, (tk, tk), 1)).astype(BF16)
        qs = [jnp.where((lane // 64) == hh, q_ref[r0:r0 + th, :], jnp.zeros((), BF16)) for hh, r0 in chains]

        def step(kb, carry, sub):
            rows = pl.ds(pl.multiple_of(kb * tk, tk), tk)
            kblk, vblk = k_ref[rows, :], v_ref[rows, :]
            masks = ["all" if sub is None else _diag_mask(th, tk, r0, sub, strict=True) for _, r0 in chains]
            live = [n for n, m in enumerate(masks) if not (isinstance(m, str) and m == "none")]
            masked = {n: not isinstance(masks[n], str) for n in live}
            z = {n: _dot_nt(qs[n], kblk) for n in live}
            lb, lom = {}, {}
            for n in live:
                lb[n], lom[n] = _log_sigmoids(z[n])
                if masked[n]:
                    lom[n] = jnp.where(masks[n], lom[n], 0.0)
            suf = {n: _split_dot(lom[n], upper, 2) for n in live}
            a = {}
            for n in live:
                a[n] = jnp.exp(lb[n] + suf[n] + carry[n][0])
                if masked[n]:
                    a[n] = jnp.where(masks[n], a[n], 0.0)
            pv = {n: _dot(a[n].astype(BF16), vblk) for n in live}
            new = list(carry)
            for n in live:
                c, acc, r = carry[n]
                rs = suf[n][:, 0:1] + lom[n][:, 0:1]
                new[n] = (c + rs, acc + pv[n], jnp.where(lane == 64 * chains[n][0] + kb, rs, r))
            return tuple(new)

        init = (jnp.zeros((th, 1), F32), jnp.zeros((th, LANES), F32), jnp.zeros((th, LANES), F32))
        carry = (init,) * len(chains)
        for sub in reversed(range(nsub)):
            carry = step(i * nsub + sub, carry, sub)
        carry = lax.fori_loop(0, i * nsub, lambda t, cy: step(i * nsub - 1 - t, cy, None), carry)
        for n in range(nh):
            rs = slice(n * th, (n + 1) * th)
            o_ref[rs, :] = jnp.where(lane < 64, carry[n][1], carry[nh + n][1]).astype(BF16)
            r_ref[rs, :] = jnp.where(lane < 64, carry[n][2], carry[nh + n][2])

    return pl.pallas_call(
        body, name="sb_fwd", grid=(4, s_len // tq),
        in_specs=[pl.BlockSpec((tq, LANES), lambda p, i: (i, p)),
                  pl.BlockSpec((s_len, LANES), lambda p, i: (0, 4 + p)),
                  pl.BlockSpec((s_len, LANES), lambda p, i: (0, 8 + p))],
        out_specs=[pl.BlockSpec((tq, LANES), lambda p, i: (i, p)),
                   pl.BlockSpec((None, tq, LANES), lambda p, i: (p, i, 0))],
        out_shape=[jax.ShapeDtypeStruct((s_len, 512), BF16), jax.ShapeDtypeStruct((4, s_len, LANES), F32)],
        compiler_params=_cparams(40, dimension_semantics=("parallel", "arbitrary")),
    )(sb, sb, sb)


def _sb_bwd(sb, do, r):
    s_len = sb.shape[0]
    tq, th, tk, nsub = _attn_blocks(s_len)
    chains = _chains(tq, th)
    nh = tq // th

    def body(q_ref, k_ref, v_ref, do_ref, r_ref, dq_ref, dk_ref, dv_ref):
        i = pl.program_id(1)
        lane = lax.broadcasted_iota(jnp.int32, (th, LANES), 1)
        upper = (lax.broadcasted_iota(jnp.int32, (tk, tk), 0)
                 > lax.broadcasted_iota(jnp.int32, (tk, tk), 1)).astype(BF16)
        tri = (lax.broadcasted_iota(jnp.int32, (LANES, LANES), 0)
               > lax.broadcasted_iota(jnp.int32, (LANES, LANES), 1)).astype(BF16)

        @pl.when(i == 0)
        def _():
            dk_ref[...] = jnp.zeros_like(dk_ref)
            dv_ref[...] = jnp.zeros_like(dv_ref)

        qs, dos, rights = [], [], []
        for hh, r0 in chains:
            rs = slice(r0, r0 + th)
            hm = (lane // 64) == hh
            qs.append(jnp.where(hm, q_ref[rs, :], jnp.zeros((), BF16)))
            dos.append(jnp.where(hm, do_ref[rs, :], jnp.zeros((), BF16)))
            rights.append(_split_dot(jnp.where(hm, r_ref[rs, :], 0.0), tri, 3))

        def step(kb, carry, sub):
            rows = pl.ds(pl.multiple_of(kb * tk, tk), tk)
            kblk, vblk = k_ref[rows, :], v_ref[rows, :]
            new, a_all, do_all, dz_all, q_all = [], [], [], [], []
            for n, ((hh, r0), (pre, dq)) in enumerate(zip(chains, carry)):
                mask = "all" if sub is None else _diag_mask(th, tk, r0, sub, strict=True)
                if isinstance(mask, str) and mask == "none":
                    new.append((pre, dq))
                    continue
                c = jnp.sum(jnp.where(lane == 64 * hh + kb, rights[n], 0.0), axis=-1, keepdims=True)
                z = _dot_nt(qs[n], kblk)
                lb, lom = _log_sigmoids(z)
                if not isinstance(mask, str):
                    lom = jnp.where(mask, lom, 0.0)
                suf = _split_dot(lom, upper, 2)
                a = jnp.exp(lb + suf + c)
                if not isinstance(mask, str):
                    a = jnp.where(mask, a, 0.0)
                g = a * _dot_nt(dos[n], vblk)
                left = _split_dot(g, upper, 1, nt=True) + pre
                sig = jnp.exp(lb)
                dz = g * (1.0 - sig) - sig * left
                if not isinstance(mask, str):
                    dz = jnp.where(mask, dz, 0.0)
                dzb = dz.astype(BF16)
                a_all.append(a.astype(BF16))
                do_all.append(dos[n])
                dz_all.append(dzb)
                q_all.append(qs[n])
                new.append((left[:, tk - 1:tk] + g[:, tk - 1:tk], dq + _dot(dzb, kblk)))
            dv_ref[rows, :] += _dot_tn(jnp.concatenate(a_all, axis=0), jnp.concatenate(do_all, axis=0))
            dk_ref[rows, :] += _dot_tn(jnp.concatenate(dz_all, axis=0), jnp.concatenate(q_all, axis=0))
            return tuple(new)

        init = (jnp.zeros((th, 1), F32), jnp.zeros((th, LANES), F32))
        carry = lax.fori_loop(0, i * nsub, lambda kb, cy: step(kb, cy, None), (init,) * len(chains))
        for sub in range(nsub):
            carry = step(i * nsub + sub, carry, sub)
        for n in range(nh):
            dq_ref[n * th:(n + 1) * th, :] = jnp.where(lane < 64, carry[n][1], carry[nh + n][1]) * SB_SCALE

    return pl.pallas_call(
        body, name="sb_bwd", grid=(4, s_len // tq),
        in_specs=[pl.BlockSpec((tq, LANES), lambda p, i: (i, p)),
                  pl.BlockSpec((s_len, LANES), lambda p, i: (0, 4 + p)),
                  pl.BlockSpec((s_len, LANES), lambda p, i: (0, 8 + p)),
                  pl.BlockSpec((tq, LANES), lambda p, i: (i, p)),
                  pl.BlockSpec((None, tq, LANES), lambda p, i: (p, i, 0))],
        out_specs=[pl.BlockSpec((tq, LANES), lambda p, i: (i, p)),
                   pl.BlockSpec((s_len, LANES), lambda p, i: (0, p)),
                   pl.BlockSpec((s_len, LANES), lambda p, i: (0, p))],
        out_shape=[jax.ShapeDtypeStruct((s_len, 512), F32)] * 3,
        compiler_params=_cparams(48, dimension_semantics=("arbitrary", "arbitrary")),
    )(sb, sb, sb, do, r)


def _merge_fwd(x, oa, ob, gates, bg, wa, wb, wo):
    s_len = x.shape[0]
    tm = _row_block(s_len)

    def body(x_ref, oa_ref, ob_ref, g_ref, bg_ref, wa_ref, wb_ref, wo_ref, y_ref):
        pa = _dot(oa_ref[...], wa_ref[...])
        pb = _dot(ob_ref[...], wb_ref[...])
        merged = (_sigmoid(g_ref[:, 0:D_MODEL] + bg_ref[0:1, :]) * pa
                  + _sigmoid(g_ref[:, D_MODEL:2 * D_MODEL] + bg_ref[1:2, :]) * pb)
        y_ref[...] = x_ref[...] + _dot(merged.astype(BF16), wo_ref[...])

    full = lambda shape: pl.BlockSpec(shape, lambda i: (0, 0))
    rowb = lambda n: pl.BlockSpec((tm, n), lambda i: (i, 0))
    return pl.pallas_call(
        body, name="merge_fwd", grid=(s_len // tm,),
        in_specs=[rowb(1024), rowb(512), rowb(512), rowb(2048), full((2, 1024)), full((512, 1024)),
                  full((512, 1024)), full((1024, 1024))],
        out_specs=rowb(1024),
        out_shape=jax.ShapeDtypeStruct((s_len, D_MODEL), F32),
        compiler_params=_cparams(48, dimension_semantics=("parallel",)),
    )(x, oa, ob, gates, bg, wa, wb, wo)


def _merge_bwd(dx1, oa, ob, gates, bg, wa, wb, wo):
    s_len = dx1.shape[0]
    tm = _row_block(s_len)

    def body(dx_ref, oa_ref, ob_ref, g_ref, bg_ref, wa_ref, wb_ref, wo_ref,
             doa_ref, dob_ref, dgate_ref, dpa_ref, dpb_ref, merged_ref, dxb_ref, dbg_ref):
        first = pl.program_id(0) == 0
        dxb = dx_ref[...].astype(BF16)
        dxb_ref[...] = dxb
        pa = _dot(oa_ref[...], wa_ref[...])
        pb = _dot(ob_ref[...], wb_ref[...])
        sa = _sigmoid(g_ref[:, 0:D_MODEL] + bg_ref[0:1, :])
        sbg = _sigmoid(g_ref[:, D_MODEL:2 * D_MODEL] + bg_ref[1:2, :])
        merged_ref[...] = (sa * pa + sbg * pb).astype(BF16)
        dm = _dot_nt(dxb, wo_ref[...])
        dpa = (dm * sa).astype(BF16)
        dpb = (dm * sbg).astype(BF16)
        dpa_ref[...] = dpa
        dpb_ref[...] = dpb
        dga = dm * pa * sa * (1.0 - sa)
        dgb = dm * pb * sbg * (1.0 - sbg)
        dgate_ref[:, 0:D_MODEL] = dga.astype(BF16)
        dgate_ref[:, D_MODEL:2 * D_MODEL] = dgb.astype(BF16)
        _acc_rows(dbg_ref.at[0:1, :], dga, first)
        _acc_rows(dbg_ref.at[1:2, :], dgb, first)
        doa_ref[...] = _dot_nt(dpa, wa_ref[...]).astype(BF16)
        dob_ref[...] = _dot_nt(dpb, wb_ref[...]).astype(BF16)

    full = lambda shape: pl.BlockSpec(shape, lambda i: (0, 0))
    rowb = lambda n: pl.BlockSpec((tm, n), lambda i: (i, 0))
    sds = lambda n, dt: jax.ShapeDtypeStruct((s_len, n), dt)
    return pl.pallas_call(
        body, name="merge_bwd", grid=(s_len // tm,),
        in_specs=[rowb(1024), rowb(512), rowb(512), rowb(2048), full((2, 1024)), full((512, 1024)),
                  full((512, 1024)), full((1024, 1024))],
        out_specs=[rowb(512), rowb(512), rowb(2048), rowb(1024), rowb(1024), rowb(1024), rowb(1024),
                   full((2, 1024))],
        out_shape=[sds(512, BF16), sds(512, BF16), sds(2048, BF16), sds(1024, BF16), sds(1024, BF16),
                   sds(1024, BF16), sds(1024, BF16), jax.ShapeDtypeStruct((2, 1024), F32)],
        compiler_params=_cparams(48, dimension_semantics=("arbitrary",)),
    )(dx1, oa, ob, gates, bg, wa, wb, wo)


def _mem_kv(mem, g, w):
    m_len = mem.shape[0]

    def body(mem_ref, g_ref, w_ref, mn_ref, kv_ref):
        mn, _, _ = _rms(mem_ref[...], g_ref[...])
        mnb = mn.astype(BF16)
        mn_ref[...] = mnb
        kv_ref[...] = _dot(mnb, w_ref[...]).astype(BF16)

    return pl.pallas_call(
        body, name="mem_kv",
        out_shape=[jax.ShapeDtypeStruct((m_len, D_MODEL), BF16), jax.ShapeDtypeStruct((m_len, 1024), BF16)],
    )(mem, g, w)


def _mem_bwd(mem, g, w, mn, dkv):
    def body(mem_ref, g_ref, w_ref, mn_ref, dkv_ref, dw_ref, dg_ref):
        dkvb = dkv_ref[...].astype(BF16)
        dw_ref[...] = _dot_tn(mn_ref[...], dkvb)
        dmn = _dot_nt(dkvb, w_ref[...])
        _, xh, _ = _rms(mem_ref[...], g_ref[...])
        dg_ref[...] = jnp.sum(dmn * xh, axis=0, keepdims=True)

    return pl.pallas_call(
        body, name="mem_bwd",
        out_shape=[jax.ShapeDtypeStruct((D_MODEL, 1024), F32), jax.ShapeDtypeStruct((1, D_MODEL), F32)],
    )(mem, g, w, mn, dkv)


def _xattn_heads(xqb, kv_ref, m_len):
    ps = []
    for h in range(X_HEADS):
        hs = slice(h * X_HEAD_DIM, (h + 1) * X_HEAD_DIM)
        s = _dot_nt(xqb[:, hs], kv_ref[:, hs]) * X_SCALE
        e = jnp.exp(s - jnp.max(s, axis=-1, keepdims=True))
        ps.append(e / jnp.sum(e, axis=-1, keepdims=True))
    return ps


def _xattn_fwd(x1, g, wxq, kv, wxo):
    s_len, m_len = x1.shape[0], kv.shape[0]
    tm = _row_block(s_len)

    def body(x_ref, g_ref, wq_ref, kv_ref, wo_ref, y_ref):
        hx, _, _ = _rms(x_ref[...], g_ref[...])
        xqb = _dot(hx.astype(BF16), wq_ref[...]).astype(BF16)
        ps = _xattn_heads(xqb, kv_ref, m_len)
        xo = jnp.concatenate(
            [_dot(ps[h].astype(BF16), kv_ref[:, 512 + h * X_HEAD_DIM:512 + (h + 1) * X_HEAD_DIM])
             for h in range(X_HEADS)], axis=-1)
        y_ref[...] = x_ref[...] + _dot(xo.astype(BF16), wo_ref[...])

    full = lambda shape: pl.BlockSpec(shape, lambda i: (0, 0))
    rowb = lambda n: pl.BlockSpec((tm, n), lambda i: (i, 0))
    return pl.pallas_call(
        body, name="xattn_fwd", grid=(s_len // tm,),
        in_specs=[rowb(1024), full((1, 1024)), full((1024, 512)), full((m_len, 1024)), full((512, 1024))],
        out_specs=rowb(1024),
        out_shape=jax.ShapeDtypeStruct((s_len, D_MODEL), F32),
        compiler_params=_cparams(48, dimension_semantics=("parallel",)),
    )(x1, g, wxq, kv, wxo)


def _xattn_bwd(x1, dx2, g, wxq, kv, wxo):
    s_len, m_len = x1.shape[0], kv.shape[0]
    tm = _row_block(s_len)

    def body(x_ref, dy_ref, g_ref, wq_ref, kv_ref, wo_ref, dx_ref, dwq_ref, dwo_ref, dkv_ref, dg_ref):
        first = pl.program_id(0) == 0
        gv = g_ref[...]
        hx, xh, r = _rms(x_ref[...], gv)
        hxb = hx.astype(BF16)
        xqb = _dot(hxb, wq_ref[...]).astype(BF16)
        ps = _xattn_heads(xqb, kv_ref, m_len)
        dy = dy_ref[...]
        dyb = dy.astype(BF16)
        dxo = _dot_nt(dyb, wo_ref[...])
        xos, dqs, dks, dvs = [], [], [], []
        for h in range(X_HEADS):
            hs = slice(h * X_HEAD_DIM, (h + 1) * X_HEAD_DIM)
            vs = slice(512 + h * X_HEAD_DIM, 512 + (h + 1) * X_HEAD_DIM)
            p = ps[h]
            pb = p.astype(BF16)
            dxoh = dxo[:, hs].astype(BF16)
            xos.append(_dot(pb, kv_ref[:, vs]))
            dp = _dot_nt(dxoh, kv_ref[:, vs])
            ds = (p * (dp - jnp.sum(dp * p, axis=-1, keepdims=True)) * X_SCALE).astype(BF16)
            dvs.append(_dot_tn(pb, dxoh))
            dks.append(_dot_tn(ds, xqb[:, hs]))
            dqs.append(_dot(ds, kv_ref[:, hs]))
        xob = jnp.concatenate(xos, axis=-1).astype(BF16)
        dxqb = jnp.concatenate(dqs, axis=-1).astype(BF16)
        _acc(dwo_ref, _dot_tn(xob, dyb), first)
        _acc(dwq_ref, _dot_tn(hxb, dxqb), first)
        _acc(dkv_ref, jnp.concatenate(dks + dvs, axis=-1), first)
        dhx = _dot_nt(dxqb, wq_ref[...])
        dx, dgr = _rms_bwd(dhx, xh, r, gv)
        dx_ref[...] = dy + dx
        _acc_rows(dg_ref, dgr, first)

    full = lambda shape: pl.BlockSpec(shape, lambda i: (0, 0))
    rowb = lambda n: pl.BlockSpec((tm, n), lambda i: (i, 0))
    return pl.pallas_call(
        body, name="xattn_bwd", grid=(s_len // tm,),
        in_specs=[rowb(1024), rowb(1024), full((1, 1024)), full((1024, 512)), full((m_len, 1024)),
                  full((512, 1024))],
        out_specs=[rowb(1024), full((1024, 512)), full((512, 1024)), full((m_len, 1024)), full((1, 1024))],
        out_shape=[jax.ShapeDtypeStruct((s_len, D_MODEL), F32), jax.ShapeDtypeStruct((1024, 512), F32),
                   jax.ShapeDtypeStruct((512, 1024), F32), jax.ShapeDtypeStruct((m_len, 1024), F32),
                   jax.ShapeDtypeStruct((1, D_MODEL), F32)],
        compiler_params=_cparams(48, dimension_semantics=("arbitrary",)),
    )(x1, dx2, g, wxq, kv, wxo)


FF_TILE = 1408
FF_TILE_BWD = 256


def _ffn_fwd(x2, g, wg, wu, wd):
    s_len = x2.shape[0]
    tm, tf = _row_block(s_len), FF_TILE

    def body(x_ref, g_ref, wg_ref, wu_ref, wd_ref, y_ref, h_scr):
        j = pl.program_id(1)

        @pl.when(j == 0)
        def _():
            hf, _, _ = _rms(x_ref[...], g_ref[...])
            h_scr[...] = hf.astype(BF16)
            y_ref[...] = x_ref[...]

        hb = h_scr[...]
        gt = _dot(hb, wg_ref[...])
        up = _dot(hb, wu_ref[...])
        act = gt * _sigmoid(gt) * up
        y_ref[...] += _dot(act.astype(BF16), wd_ref[...])

    return pl.pallas_call(
        body, name="ffn_fwd", grid=(s_len // tm, D_FF // tf),
        in_specs=[pl.BlockSpec((tm, D_MODEL), lambda i, j: (i, 0)),
                  pl.BlockSpec((1, D_MODEL), lambda i, j: (0, 0)),
                  pl.BlockSpec((D_MODEL, tf), lambda i, j: (0, j)),
                  pl.BlockSpec((D_MODEL, tf), lambda i, j: (0, j)),
                  pl.BlockSpec((tf, D_MODEL), lambda i, j: (j, 0))],
        out_specs=pl.BlockSpec((tm, D_MODEL), lambda i, j: (i, 0)),
        out_shape=jax.ShapeDtypeStruct((s_len, D_MODEL), F32),
        scratch_shapes=[pltpu.VMEM((tm, D_MODEL), BF16)],
        compiler_params=_cparams(48, dimension_semantics=("parallel", "arbitrary")),
    )(x2, g, wg, wu, wd)


def _ffn_bwd(x2, dx3, g, wg, wu, wd):
    s_len = x2.shape[0]
    tm, tf = _row_block(s_len), FF_TILE_BWD
    nf = D_FF // tf

    def body(x_ref, dy_ref, g_ref, wg_ref, wu_ref, wd_ref,
             dx_ref, h_ref, dgt_ref, dup_ref, act_ref, dg_ref, h_scr, dyb_scr, dh_scr):
        i, j = pl.program_id(0), pl.program_id(1)

        @pl.when(j == 0)
        def _():
            hf, _, _ = _rms(x_ref[...], g_ref[...])
            hb = hf.astype(BF16)
            h_scr[...] = hb
            h_ref[...] = hb
            dyb_scr[...] = dy_ref[...].astype(BF16)
            dh_scr[...] = jnp.zeros_like(dh_scr)

        hb = h_scr[...]
        gt = _dot(hb, wg_ref[...])
        up = _dot(hb, wu_ref[...])
        sg = _sigmoid(gt)
        silu = gt * sg
        dact = _dot_nt(dyb_scr[...], wd_ref[...])
        dgt = (dact * up * (sg * (1.0 + gt * (1.0 - sg)))).astype(BF16)
        dup = (dact * silu).astype(BF16)
        dgt_ref[...] = dgt
        dup_ref[...] = dup
        act_ref[...] = (silu * up).astype(BF16)
        dh_scr[...] += _dot_nt(dgt, wg_ref[...]) + _dot_nt(dup, wu_ref[...])

        @pl.when(j == nf - 1)
        def _():
            gv = g_ref[...]
            _, xh, r = _rms(x_ref[...], gv)
            dx, dgr = _rms_bwd(dh_scr[...], xh, r, gv)
            dx_ref[...] = dy_ref[...] + dx
            _acc_rows(dg_ref, dgr, i == 0)

    rowb = pl.BlockSpec((tm, D_MODEL), lambda i, j: (i, 0))
    ffb = pl.BlockSpec((tm, tf), lambda i, j: (i, j))
    return pl.pallas_call(
        body, name="ffn_bwd", grid=(s_len // tm, nf),
        in_specs=[rowb, rowb, pl.BlockSpec((1, D_MODEL), lambda i, j: (0, 0)),
                  pl.BlockSpec((D_MODEL, tf), lambda i, j: (0, j)),
                  pl.BlockSpec((D_MODEL, tf), lambda i, j: (0, j)),
                  pl.BlockSpec((tf, D_MODEL), lambda i, j: (j, 0))],
        out_specs=[rowb, rowb, ffb, ffb, ffb, pl.BlockSpec((1, D_MODEL), lambda i, j: (0, 0))],
        out_shape=[jax.ShapeDtypeStruct((s_len, D_MODEL), F32), jax.ShapeDtypeStruct((s_len, D_MODEL), BF16),
                   jax.ShapeDtypeStruct((s_len, D_FF), BF16), jax.ShapeDtypeStruct((s_len, D_FF), BF16),
                   jax.ShapeDtypeStruct((s_len, D_FF), BF16), jax.ShapeDtypeStruct((1, D_MODEL), F32)],
        scratch_shapes=[pltpu.VMEM((tm, D_MODEL), BF16), pltpu.VMEM((tm, D_MODEL), BF16),
                        pltpu.VMEM((tm, D_MODEL), F32)],
        compiler_params=_cparams(56, dimension_semantics=("arbitrary", "arbitrary")),
    )(x2, dx3, g, wg, wu, wd)


def _loss_head(x3, g, target):
    s_len = x3.shape[0]
    tm = _row_block(s_len)

    def body(x_ref, g_ref, t_ref, sse_ref, dx_ref, dxb_ref, dg_ref):
        first = pl.program_id(0) == 0
        gv = g_ref[...]
        y, xh, r = _rms(x_ref[...], gv)
        err = y - t_ref[...]
        _acc(sse_ref, jnp.broadcast_to(jnp.sum(err * err), (8, LANES)), first)
        dx, dgr = _rms_bwd(err * (1.0 / D_MODEL), xh, r, gv)
        dx_ref[...] = dx
        dxb_ref[...] = dx.astype(BF16)
        _acc_rows(dg_ref, dgr, first)

    rowb = pl.BlockSpec((tm, D_MODEL), lambda i: (i, 0))
    return pl.pallas_call(
        body, name="loss_head", grid=(s_len // tm,),
        in_specs=[rowb, pl.BlockSpec((1, D_MODEL), lambda i: (0, 0)), rowb],
        out_specs=[pl.BlockSpec((8, LANES), lambda i: (0, 0)), rowb, rowb,
                   pl.BlockSpec((1, D_MODEL), lambda i: (0, 0))],
        out_shape=[jax.ShapeDtypeStruct((8, LANES), F32), jax.ShapeDtypeStruct((s_len, D_MODEL), F32),
                   jax.ShapeDtypeStruct((s_len, D_MODEL), BF16), jax.ShapeDtypeStruct((1, D_MODEL), F32)],
        compiler_params=_cparams(dimension_semantics=("arbitrary",)),
    )(x3, g, target)


def _mla_prep_bwd(lat, g_q, g_kv, w_uq, w_uk, w_uv, cosf, sinf, dq, dk, dv):
    s_len = lat.shape[0]
    tm = _row_block(s_len)

    def body(lat_ref, gq_ref, gkv_ref, wuq_ref, wuk_ref, wuv_ref, cos_ref, sin_ref, dq_ref, dk_ref, dv_ref,
             dlat_ref, dqb_ref, dkb_ref, dvb_ref, dgq_ref, dgkv_ref):
        first = pl.program_id(0) == 0
        lane = lax.broadcasted_iota(jnp.int32, (tm, LANES), 1)
        cosv, sinv = cos_ref[...], sin_ref[...]
        gq, gkv = gq_ref[...], gkv_ref[...]
        _, qxh, qr = _rms(lat_ref[:, 0:256], gq)
        _, kxh, kr_ = _rms(lat_ref[:, 256:384], gkv)
        dkr = jnp.zeros((tm, LANES), F32)
        for h in range(MLA_HEADS):
            sl = slice(h * HEAD_PAD, (h + 1) * HEAD_PAD)
            blk = dq_ref[:, sl]
            dqb_ref[:, sl] = (blk * cosv + _rope_rot_t(blk, lane) * sinv).astype(BF16)
            kblk = dk_ref[:, sl]
            dkb_ref[:, sl] = kblk.astype(BF16)
            dkr = dkr + kblk
        dvb = dv_ref[...].astype(BF16)
        dvb_ref[...] = dvb
        dkr = jnp.where((lane >= 64) & (lane < 96), dkr, 0.0)
        dkr = dkr * cosv + _rope_rot_t(dkr, lane) * sinv
        dql = _dot_nt(dqb_ref[...], wuq_ref[...])
        dkvl = _dot_nt(dkb_ref[...], wuk_ref[...]) + _dot_nt(dvb, wuv_ref[...])
        dcq, dgqr = _rms_bwd(dql, qxh, qr, gq)
        dckv, dgkvr = _rms_bwd(dkvl, kxh, kr_, gkv)
        dlat_ref[:, 0:256] = dcq
        dlat_ref[:, 256:384] = dckv
        dlat_ref[:, K_R_OFF:K_R_OFF + LANES] = pltpu.roll(dkr, 64, 1)
        _acc_rows(dgq_ref, dgqr, first)
        _acc_rows(dgkv_ref, dgkvr, first)

    full = lambda shape: pl.BlockSpec(shape, lambda i: (0, 0))
    rowb = lambda n: pl.BlockSpec((tm, n), lambda i: (i, 0))
    sds = lambda n, dt: jax.ShapeDtypeStruct((s_len, n), dt)
    return pl.pallas_call(
        body, name="mla_prep_bwd", grid=(s_len // tm,),
        in_specs=[rowb(512), full((1, 256)), full((1, 128)), full((256, 1024)), full((128, 1024)),
                  full((128, 512)), rowb(128), rowb(128), rowb(1024), rowb(1024), rowb(512)],
        out_specs=[rowb(512), rowb(1024), rowb(1024), rowb(512), full((1, 256)), full((1, 128))],
        out_shape=[sds(512, F32), sds(1024, BF16), sds(1024, BF16), sds(512, BF16),
                   jax.ShapeDtypeStruct((1, 256), F32), jax.ShapeDtypeStruct((1, 128), F32)],
        compiler_params=_cparams(48, dimension_semantics=("arbitrary",)),
    )(lat, g_q, g_kv, w_uq, w_uk, w_uv, cosf, sinf, dq, dk, dv)


def _in_proj_bwd(x, g, w, dx1, dlat, dsbq, dsbk, dsbv, dgates):
    s_len = x.shape[0]
    tm, tn = _row_block(s_len), 512
    nj = D_IN_PAD // tn

    def body(x_ref, g_ref, w_ref, dx1_ref, dlat_ref, dq_ref, dk_ref, dv_ref, dgate_ref,
             gx_ref, dproj_ref, dg_ref, dh_scr):
        i, j = pl.program_id(0), pl.program_id(1)

        @pl.when(j == 0)
        def _():
            dh_scr[...] = jnp.zeros_like(dh_scr)

        def chunk(val):
            vb = val.astype(BF16)
            dproj_ref[...] = vb
            dh_scr[...] += _dot_nt(vb, w_ref[...])

        for jj, ref in ((0, dlat_ref), (1, dq_ref), (2, dk_ref), (3, dv_ref)):
            pl.when(j == jj)(functools.partial(lambda ref: chunk(ref[...]), ref))
        pl.when(j >= 4)(lambda: chunk(dgate_ref[...]))

        @pl.when(j == nj - 1)
        def _():
            gv = g_ref[...]
            _, xh, r = _rms(x_ref[...], gv)
            dx, dgr = _rms_bwd(dh_scr[...], xh, r, gv)
            gx_ref[...] = dx1_ref[...] + dx
            _acc_rows(dg_ref, dgr, i == 0)

    rowb = pl.BlockSpec((tm, D_MODEL), lambda i, j: (i, 0))
    colb = lambda f: pl.BlockSpec((tm, tn), f)
    return pl.pallas_call(
        body, name="in_proj_bwd", grid=(s_len // tm, nj),
        in_specs=[rowb, pl.BlockSpec((1, D_MODEL), lambda i, j: (0, 0)),
                  pl.BlockSpec((D_MODEL, tn), lambda i, j: (0, j)), rowb,
                  colb(lambda i, j: (i, 0)), colb(lambda i, j: (i, 0)), colb(lambda i, j: (i, 0)),
                  colb(lambda i, j: (i, 0)), colb(lambda i, j: (i, jnp.clip(j - 4, 0, 3)))],
        out_specs=[rowb, colb(lambda i, j: (i, j)), pl.BlockSpec((1, D_MODEL), lambda i, j: (0, 0))],
        out_shape=[jax.ShapeDtypeStruct((s_len, D_MODEL), F32), jax.ShapeDtypeStruct((s_len, D_IN_PAD), BF16),
                   jax.ShapeDtypeStruct((1, D_MODEL), F32)],
        scratch_shapes=[pltpu.VMEM((tm, D_MODEL), F32)],
        compiler_params=_cparams(48, dimension_semantics=("arbitrary", "arbitrary")),
    )(x, g, w, dx1, dlat, dsbq, dsbk, dsbv, dgates)


def _adamw(landed, w, m, v):
    rows = w.shape[0]
    tb = min(rows, ADAM_BLOCK_ROWS)
    c1 = 1.0 - ADAM_B1 ** ADAM_STEP
    c2 = 1.0 - ADAM_B2 ** ADAM_STEP

    def body(l_ref, w_ref, m_ref, v_ref, g_ref, d_ref, nm_ref, nv_ref):
        g = l_ref[0]
        for k in range(1, N_DEV):
            g = g + l_ref[k]
        nm = ADAM_B1 * m_ref[...] + (1.0 - ADAM_B1) * g
        nv = ADAM_B2 * v_ref[...] + (1.0 - ADAM_B2) * (g * g)
        g_ref[...] = g
        nm_ref[...] = nm
        nv_ref[...] = nv
        d_ref[...] = -ADAM_LR * ((nm / c1) / (jnp.sqrt(nv / c2) + ADAM_EPS) + ADAM_WD * w_ref[...])

    blk = pl.BlockSpec((tb, LANES), lambda i: (i, 0))
    return pl.pallas_call(
        body, name="reduce_adamw", grid=(rows // tb,),
        in_specs=[pl.BlockSpec((N_DEV, tb, LANES), lambda i: (0, i, 0)), blk, blk, blk],
        out_specs=[blk, blk, blk, blk],
        out_shape=[jax.ShapeDtypeStruct((rows, LANES), F32)] * 4,
        compiler_params=_cparams(dimension_semantics=("parallel",)),
    )(landed, w, m, v)


def _padded(n):
    return _round_up(n, PACK_ALIGN)


def _shard_shape(shape, axis):
    return tuple(d // N_DEV if a == axis else d for a, d in enumerate(shape))


def _pack_flat(parts, total):
    flat = []
    used = 0
    for p in parts:
        p = p.reshape(-1)
        n = _padded(p.shape[0])
        flat.append(jnp.pad(p, (0, n - p.shape[0])))
        used += n
    if total > used:
        flat.append(jnp.zeros((total - used,), parts[0].dtype))
    return jnp.concatenate(flat)


def _split_pieces(full, axis):
    r, c = full.shape
    if axis == 0:
        return full.reshape(N_DEV, (r // N_DEV) * c)
    return full.reshape(r, N_DEV, c // N_DEV).transpose(1, 0, 2).reshape(N_DEV, r * (c // N_DEV))


def _join_shards(gathered, shape, axis):
    r, c = shape
    if axis == 0:
        return gathered.reshape(r, c)
    return gathered.reshape(N_DEV, r, c // N_DEV).transpose(1, 0, 2).reshape(r, c)


def kernel(x, mem, positions, g_mix, w_in, b_gate, g_q_lat, w_uq, g_kv_lat, w_ukv, w_a_proj, w_b_proj, w_o, g_x, g_mem, w_xq, w_xkv, w_xo, g_ffn, w_gate, w_up, w_down, g_final, loss_target, m_g_mix, m_w_in, m_b_gate, m_g_q_lat, m_w_uq, m_g_kv_lat, m_w_ukv, m_w_a_proj, m_w_b_proj, m_w_o, m_g_x, m_g_mem, m_w_xq, m_w_xkv, m_w_xo, m_g_ffn, m_w_gate, m_w_up, m_w_down, m_g_final, v_g_mix, v_w_in, v_b_gate, v_g_q_lat, v_w_uq, v_g_kv_lat, v_w_ukv, v_w_a_proj, v_w_b_proj, v_w_o, v_g_x, v_g_mem, v_w_xq, v_w_xkv, v_w_xo, v_g_ffn, v_w_gate, v_w_up, v_w_down, v_g_final):
    given = dict(locals())
    s_len = x.shape[1]
    x2d = x.reshape(s_len, D_MODEL)
    mem2d = mem.reshape(-1, D_MODEL)
    target = loss_target.reshape(s_len, D_MODEL)

    parts = []
    for name, shape, axis in SHARDED:
        a = given[name].reshape(_shard_shape(shape, axis))
        if name == "b_gate":
            parts.append(lax.bitcast_convert_type(a, BF16))
        else:
            parts.append(a.astype(BF16))
    n_bf = sum(_padded(math.prod(p.shape)) for p in parts)
    gathered = _all_gather(_pack_flat(parts, n_bf).reshape(n_bf // LANES, LANES)).reshape(N_DEV, n_bf)
    wts = {}
    off = 0
    for (name, shape, axis), p in zip(SHARDED, parts):
        n = math.prod(p.shape)
        piece = gathered[:, off:off + n]
        off += _padded(n)
        if name == "b_gate":
            piece = lax.bitcast_convert_type(piece.reshape(N_DEV, n // 2, 2), F32)
        wts[name] = _join_shards(piece, shape, axis)

    w_in_p = jnp.concatenate([wts["w_in"][:, :416], jnp.zeros((D_MODEL, 96), BF16), wts["w_in"][:, 416:]], axis=1)
    w_uq_p = jnp.pad(wts["w_uq"].reshape(256, MLA_HEADS, 96), ((0, 0), (0, 0), (0, 32))).reshape(256, 1024)
    ukv = wts["w_ukv"].reshape(128, MLA_HEADS, 128)
    w_uk_p = jnp.pad(ukv[:, :, :64], ((0, 0), (0, 0), (0, 64))).reshape(128, 1024)
    w_uv = ukv[:, :, 64:].reshape(128, 512)
    bg = wts["b_gate"]

    inv_freq = ROPE_THETA ** (-jnp.arange(0, MLA_ROPE, 2, dtype=F32) / MLA_ROPE)
    ang = positions.reshape(s_len).astype(F32)[:, None] * inv_freq
    cos16, sin16 = jnp.cos(ang), jnp.sin(ang)
    cosf = jnp.concatenate([jnp.ones((s_len, 64), F32), cos16, cos16, jnp.ones((s_len, 32), F32)], axis=1)
    sinf = jnp.concatenate([jnp.zeros((s_len, 64), F32), sin16, sin16, jnp.zeros((s_len, 32), F32)], axis=1)

    h1, lat, sb, gates = _in_proj(x2d, g_mix, w_in_p)
    qa, ka, va, q_lat, kv_lat = _mla_prep(lat, g_q_lat, g_kv_lat, w_uq_p, w_uk_p, w_uv, cosf, sinf)
    oa, lse = _mla_fwd(qa, ka, va)
    ob, sb_r = _sb_fwd(sb)
    x1 = _merge_fwd(x2d, oa, ob, gates, bg, wts["w_a_proj"], wts["w_b_proj"], wts["w_o"])
    mn, xkv = _mem_kv(mem2d, g_mem, wts["w_xkv"])
    x2 = _xattn_fwd(x1, g_x, wts["w_xq"], xkv, wts["w_xo"])
    x3 = _ffn_fwd(x2, g_ffn, wts["w_gate"], wts["w_up"], wts["w_down"])
    g_final2d = g_final.reshape(1, D_MODEL)
    sse, dx3, dx3b, dg_final = _loss_head(x3, g_final2d, target)
    loss = lax.psum(sse[0, 0] * (0.5 / D_MODEL), ("x", "y", "c"))

    dx2, hf, dgt, dup, act, dg_ffn = _ffn_bwd(x2, dx3, g_ffn, wts["w_gate"], wts["w_up"], wts["w_down"])
    dx1, dw_xq, dw_xo, dxkv, dg_x = _xattn_bwd(x1, dx2, g_x, wts["w_xq"], xkv, wts["w_xo"])
    dw_xkv, dg_mem = _mem_bwd(mem2d, g_mem, wts["w_xkv"], mn, dxkv)
    doa, dob, dgates, dpa, dpb, merged, dx1b, dbg = _merge_bwd(
        dx1, oa, ob, gates, bg, wts["w_a_proj"], wts["w_b_proj"], wts["w_o"])
    dsbq, dsbk, dsbv = _sb_bwd(sb, dob, sb_r)
    dqa, dka, dva = _mla_bwd(qa, ka, va, oa, doa, lse)
    dlat, dqb, dkb, dvb, dg_q, dg_kv = _mla_prep_bwd(
        lat, g_q_lat, g_kv_lat, w_uq_p, w_uk_p, w_uv, cosf, sinf, dqa, dka, dva)
    grad_x, dproj, dg_mix = _in_proj_bwd(x2d, g_mix, w_in_p, dx1, dlat, dsbq, dsbk, dsbv, dgates)

    dw_in_p = _tn_matmul(h1, dproj, "dw_in")
    dw_uq_p = _tn_matmul(q_lat, dqb, "dw_uq")
    dw_uk_p = _tn_matmul(kv_lat, dkb, "dw_uk")
    dw_uv = _tn_matmul(kv_lat, dvb, "dw_uv")
    full_grads = {
        "w_in": jnp.concatenate([dw_in_p[:, :416], dw_in_p[:, 512:]], axis=1),
        "b_gate": dbg,
        "w_uq": dw_uq_p.reshape(256, MLA_HEADS, 128)[:, :, :96].reshape(256, 768),
        "w_ukv": jnp.concatenate([dw_uk_p.reshape(128, MLA_HEADS, 128)[:, :, :64],
                                  dw_uv.reshape(128, MLA_HEADS, 64)], axis=2).reshape(128, 1024),
        "w_a_proj": _tn_matmul(oa, dpa, "dw_a"),
        "w_b_proj": _tn_matmul(ob, dpb, "dw_b"),
        "w_o": _tn_matmul(merged, dx1b, "dw_o"),
        "w_xq": dw_xq,
        "w_xkv": dw_xkv,
        "w_xo": dw_xo,
        "w_gate": _tn_matmul(hf, dgt, "dw_gate", tn=FF_TILE),
        "w_up": _tn_matmul(hf, dup, "dw_up", tn=FF_TILE),
        "w_down": _tn_matmul(act, dx3b, "dw_down", tka=FF_TILE),
    }
    rep_grads = {"g_mix": dg_mix, "g_q_lat": dg_q, "g_kv_lat": dg_kv, "g_x": dg_x, "g_mem": dg_mem,
                 "g_ffn": dg_ffn, "g_final": dg_final}

    n_sh = sum(_padded(math.prod(_shard_shape(shape, axis))) for _, shape, axis in SHARDED)
    n_rep = _padded(sum(n for _, n in REPLICATED))
    n_all = _round_up(n_sh + n_rep, ADAM_BLOCK_ROWS * LANES)
    rows_all = n_all // LANES
    cols = []
    for name, shape, axis in SHARDED:
        pc = _split_pieces(full_grads[name], axis)
        cols.append(jnp.pad(pc, ((0, 0), (0, _padded(pc.shape[1]) - pc.shape[1]))))
    rep = jnp.concatenate([rep_grads[name].reshape(-1) for name, _ in REPLICATED])
    cols.append(jnp.broadcast_to(jnp.pad(rep, (0, n_all - n_sh - rep.shape[0])), (N_DEV, n_all - n_sh)))
    landed = _all_to_all(jnp.concatenate(cols, axis=1).reshape(N_DEV, rows_all, LANES))

    def pack_local(prefix):
        ps = [given[prefix + name] for name, _, _ in SHARDED] + [given[prefix + name] for name, _ in REPLICATED]
        flat = []
        for p in ps[:len(SHARDED)]:
            p = p.reshape(-1)
            flat.append(jnp.pad(p, (0, _padded(p.shape[0]) - p.shape[0])))
        flat.append(jnp.concatenate([p.reshape(-1) for p in ps[len(SHARDED):]]))
        flat = jnp.concatenate(flat)
        return jnp.pad(flat, (0, n_all - flat.shape[0])).reshape(rows_all, LANES)

    outs = _adamw(landed, pack_local(""), pack_local("m_"), pack_local("v_"))
    outs = [o.reshape(n_all) for o in outs]

    def unpack(flat):
        res = {}
        off = 0
        for name, shape, axis in SHARDED:
            n = math.prod(_shard_shape(shape, axis))
            res[name] = flat[off:off + n].reshape(given[name].shape)
            off += _padded(n)
        for name, n in REPLICATED:
            res[name] = flat[off:off + n].reshape(given[name].shape)
            off += n
        return res

    groups = [unpack(o) for o in outs]
    result = [loss, grad_x.reshape(x.shape)]
    for grp in groups:
        result.extend(grp[name] for name in WEIGHT_ORDER)
    return tuple(result)
```

```python
import functools
import math

import jax
import jax.numpy as jnp
from jax import lax
from jax.experimental import pallas as pl
from jax.experimental.pallas import tpu as pltpu

F32 = jnp.float32
BF16 = jnp.bfloat16

D_MODEL = 1024
MLA_HEADS = 8
MLA_Q_RANK = 256
MLA_KV_RANK = 128
MLA_NOPE = 64
MLA_ROPE = 32
MLA_V = 64
ROPE_THETA = 10000.0
SB_WIDTH = 512
X_HEADS = 4
X_HEAD_DIM = 128
D_FF = 2816
EPS = 1e-6
D_IN = 4000
D_IN_PAD = 4096
K_R_OFF = 384
LANES = 128
HEAD_PAD = 128
MLA_SCALE = 1.0 / math.sqrt(MLA_NOPE + MLA_ROPE)
SB_SCALE = 0.125
X_SCALE = 1.0 / math.sqrt(X_HEAD_DIM)
NEG_BIG = -1e30

ADAM_LR = 0.001
ADAM_B1 = 0.9
ADAM_B2 = 0.999
ADAM_EPS = 1e-08
ADAM_WD = 0.01
ADAM_STEP = 10

N_DEV = 8
MIB = 1024 * 1024
ADAM_BLOCK_BYTES = 4 * MIB

SHARDED = (
    ("w_in", (D_MODEL, D_IN), 1),
    ("b_gate", (2, D_MODEL), 1),
    ("w_uq", (MLA_Q_RANK, 768), 1),
    ("w_ukv", (MLA_KV_RANK, 1024), 1),
    ("w_a_proj", (512, D_MODEL), 1),
    ("w_b_proj", (512, D_MODEL), 1),
    ("w_o", (D_MODEL, D_MODEL), 0),
    ("w_xq", (D_MODEL, 512), 0),
    ("w_xkv", (D_MODEL, 1024), 0),
    ("w_xo", (512, D_MODEL), 1),
    ("w_gate", (D_MODEL, D_FF), 1),
    ("w_up", (D_MODEL, D_FF), 1),
    ("w_down", (D_FF, D_MODEL), 0),
)
GATHER_FIRST = ("w_in", "b_gate", "w_uq", "w_ukv")
GATHER_REST = ("w_a_proj", "w_b_proj", "w_o", "w_xq", "w_xkv", "w_xo", "w_gate", "w_up", "w_down")
SCATTER_FFN = ("w_gate", "w_up", "w_down")
SCATTER_MID = ("w_xq", "w_xkv", "w_xo", "w_a_proj", "w_b_proj", "w_o")
SCATTER_LAST = ("w_in", "b_gate", "w_uq", "w_ukv")
REPLICATED = (
    ("g_mix", 1024), ("g_q_lat", 256), ("g_kv_lat", 128), ("g_x", 1024),
    ("g_mem", 1024), ("g_ffn", 1024), ("g_final", 1024),
)
WEIGHT_ORDER = ("g_mix", "w_in", "b_gate", "g_q_lat", "w_uq", "g_kv_lat", "w_ukv", "w_a_proj",
                "w_b_proj", "w_o", "g_x", "g_mem", "w_xq", "w_xkv", "w_xo", "g_ffn", "w_gate",
                "w_up", "w_down", "g_final")


def _round_up(n, m):
    return -(-n // m) * m


def _cparams(vmem_mib=None, **kw):
    if vmem_mib is not None:
        kw["vmem_limit_bytes"] = vmem_mib * MIB
    return pltpu.CompilerParams(**kw)


def _dot(a, b):
    return jnp.dot(a, b, preferred_element_type=F32)


def _dot_nt(a, b):
    return lax.dot_general(a, b, (((1,), (1,)), ((), ())), preferred_element_type=F32)


def _dot_tn(a, b):
    return lax.dot_general(a, b, (((0,), (0,)), ((), ())), preferred_element_type=F32)


def _rms(x, g):
    r = lax.rsqrt(jnp.mean(x * x, axis=-1, keepdims=True) + EPS)
    xh = x * r
    return xh * g, xh, r


def _rms_bwd(dy, xh, r, g):
    u = dy * g
    dx = r * (u - xh * jnp.mean(u * xh, axis=-1, keepdims=True))
    return dx, dy * xh


def _sigmoid(z):
    return 1.0 / (1.0 + jnp.exp(-z))


def _acc_rows(ref, val, first):
    s = jnp.sum(val, axis=0, keepdims=True)

    @pl.when(first)
    def _():
        ref[...] = s

    @pl.when(jnp.logical_not(first))
    def _():
        ref[...] += s


def _acc(ref, val, first):
    @pl.when(first)
    def _():
        ref[...] = val

    @pl.when(jnp.logical_not(first))
    def _():
        ref[...] += val


def _peer(k):
    x, y, c = lax.axis_index("x"), lax.axis_index("y"), lax.axis_index("c")
    px = 1 - x if (k >> 2) & 1 else x
    py = 1 - y if (k >> 1) & 1 else y
    pc = 1 - c if k & 1 else c
    return (px, py, pc), 4 * px + 2 * py + pc


N_PEERS = N_DEV - 1
HBM_SPEC = pl.BlockSpec(memory_space=pltpu.HBM)
SEM_SPEC = pl.BlockSpec(memory_space=pltpu.SEMAPHORE)


def _land_shape(gather, src):
    return (N_DEV,) + src.shape if gather else src.shape


def _remote_copy(gather, srcs, lands, a, k, send_sems, recv_sems, arrival):
    peer, pid = _peer(k)
    _, me = _peer(0)
    return pltpu.make_async_remote_copy(
        src_ref=srcs[a] if gather else srcs[a].at[pid],
        dst_ref=lands[a].at[pid if arrival else me],
        send_sem=send_sems.at[a * N_PEERS + k - 1], recv_sem=recv_sems.at[a * N_PEERS + k - 1],
        device_id=peer, device_id_type=pl.DeviceIdType.MESH)


def _exchange(gather, srcs, name):
    n = len(srcs)

    def body(*refs):
        src, land = refs[:n], refs[n:2 * n]
        send_sems, recv_sems, local_sems = refs[2 * n:]
        _, me = _peer(0)
        mine = [pltpu.make_async_copy(src[a] if gather else src[a].at[me], land[a].at[me], local_sems.at[a])
                for a in range(n)]
        for cp in mine:
            cp.start()
        for a in range(n):
            for k in range(1, N_DEV):
                _remote_copy(gather, src, land, a, k, send_sems, recv_sems, False).start()
        for a in range(n):
            for k in range(1, N_DEV):
                _remote_copy(gather, src, land, a, k, send_sems, recv_sems, True).wait_recv()
        for a in range(n):
            for k in range(1, N_DEV):
                _remote_copy(gather, src, land, a, k, send_sems, recv_sems, False).wait_send()
        for cp in mine:
            cp.wait()

    return pl.pallas_call(
        body, name=name,
        out_shape=[jax.ShapeDtypeStruct(_land_shape(gather, s), s.dtype) for s in srcs],
        in_specs=[pl.BlockSpec(memory_space=pl.ANY)] * n,
        out_specs=[pl.BlockSpec(memory_space=pl.ANY)] * n,
        scratch_shapes=[pltpu.SemaphoreType.DMA((n * N_PEERS,)), pltpu.SemaphoreType.DMA((n * N_PEERS,)),
                        pltpu.SemaphoreType.DMA((n,))],
    )(*srcs)


def _exchange_start(gather, srcs, after, name):
    n = len(srcs)

    def body(*refs):
        src, land = refs[:n], refs[n:2 * n]
        send_sems, recv_sems = refs[2 * n + 1], refs[2 * n + 2]
        token = refs[-1]
        for a in range(n):
            for k in range(1, N_DEV):
                _remote_copy(gather, src, land, a, k, send_sems, recv_sems, False).start()
        token[...] = jnp.zeros_like(token)

    lands = [lax.empty(_land_shape(gather, s), s.dtype) for s in srcs]
    hbm = lambda v: pltpu.with_memory_space_constraint(v, pltpu.HBM)
    outs = pl.pallas_call(
        body, name=name,
        out_shape=(pltpu.SemaphoreType.DMA((n * N_PEERS,)), pltpu.SemaphoreType.DMA((n * N_PEERS,)),
                   *[pltpu.HBM(v.shape, v.dtype) for v in srcs + lands],
                   jax.ShapeDtypeStruct((8, LANES), F32)),
        in_specs=[HBM_SPEC] * (2 * n) + [pl.BlockSpec(memory_space=pl.ANY)],
        out_specs=(SEM_SPEC, SEM_SPEC, *[HBM_SPEC] * (2 * n), pl.BlockSpec(memory_space=pltpu.VMEM)),
        input_output_aliases={i: 2 + i for i in range(2 * n)},
        compiler_params=pltpu.CompilerParams(has_side_effects=pltpu.SideEffectType.DATAFLOW_SIDE_EFFECTING),
    )(*[hbm(v) for v in srcs + lands], after)
    return (outs[0], outs[1], list(outs[2:2 + n]), list(outs[2 + n:2 + 2 * n])), outs[-1]


def _exchange_wait(gather, handle, after, name):
    send_sems, recv_sems, srcs, lands = handle
    n = len(srcs)

    def body(*refs):
        src, land = refs[:n], refs[n:2 * n]
        ssem, rsem = refs[2 * n], refs[2 * n + 1]
        for a in range(n):
            for k in range(1, N_DEV):
                cp = _remote_copy(gather, src, land, a, k, ssem, rsem, True)
                cp.wait_send()
                cp.wait_recv()

    outs = pl.pallas_call(
        body, name=name,
        out_shape=tuple(pltpu.HBM(v.shape, v.dtype) for v in srcs + lands),
        in_specs=[HBM_SPEC] * (2 * n) + [SEM_SPEC, SEM_SPEC, pl.BlockSpec(memory_space=pl.ANY)],
        out_specs=tuple([HBM_SPEC] * (2 * n)),
        input_output_aliases={i: i for i in range(2 * n)},
        compiler_params=pltpu.CompilerParams(has_side_effects=pltpu.SideEffectType.DATAFLOW_SIDE_EFFECTING),
    )(*srcs, *lands, send_sems, recv_sems, after)
    return list(outs[n:])


def _tn_matmul(a, b, name, tka=512, tn=1024, ts=512):
    s_len, ka = a.shape
    n = b.shape[1]
    tka, tn, ts = min(tka, ka), min(tn, n), min(ts, s_len)
    assert ka % tka == 0 and n % tn == 0 and s_len % ts == 0

    def body(a_ref, b_ref, o_ref):
        _acc(o_ref, _dot_tn(a_ref[...], b_ref[...]), pl.program_id(2) == 0)

    return pl.pallas_call(
        body, name=name, grid=(ka // tka, n // tn, s_len // ts),
        in_specs=[pl.BlockSpec((ts, tka), lambda i, j, s: (s, i)),
                  pl.BlockSpec((ts, tn), lambda i, j, s: (s, j))],
        out_specs=pl.BlockSpec((tka, tn), lambda i, j, s: (i, j)),
        out_shape=jax.ShapeDtypeStruct((ka, n), F32),
        compiler_params=_cparams(dimension_semantics=("parallel", "parallel", "arbitrary")),
    )(a, b)


def _row_block(s_len):
    return min(s_len, 512)


def _in_proj(x, g, w):
    s_len = x.shape[0]
    tm, tn = _row_block(s_len), 512

    def body(x_ref, g_ref, w_ref, h_ref, lat_ref, sb_ref, gate_ref, h_scr):
        j = pl.program_id(1)

        @pl.when(j == 0)
        def _():
            h, _, _ = _rms(x_ref[...], g_ref[...])
            hb = h.astype(BF16)
            h_scr[...] = hb
            h_ref[...] = hb

        p = _dot(h_scr[...], w_ref[...])

        @pl.when(j == 0)
        def _():
            lat_ref[...] = p

        @pl.when(j == 1)
        def _():
            sb_ref[...] = (p * SB_SCALE).astype(BF16)

        @pl.when((j == 2) | (j == 3))
        def _():
            sb_ref[...] = p.astype(BF16)

        @pl.when(j >= 4)
        def _():
            gate_ref[...] = p

    return pl.pallas_call(
        body, name="in_proj", grid=(s_len // tm, D_IN_PAD // tn),
        in_specs=[pl.BlockSpec((tm, D_MODEL), lambda i, j: (i, 0)),
                  pl.BlockSpec((1, D_MODEL), lambda i, j: (0, 0)),
                  pl.BlockSpec((D_MODEL, tn), lambda i, j: (0, j))],
        out_specs=[pl.BlockSpec((tm, D_MODEL), lambda i, j: (i, 0)),
                   pl.BlockSpec((tm, tn), lambda i, j: (i, 0)),
                   pl.BlockSpec((tm, tn), lambda i, j: (i, jnp.clip(j - 1, 0, 2))),
                   pl.BlockSpec((tm, tn), lambda i, j: (i, jnp.clip(j - 4, 0, 3)))],
        out_shape=[jax.ShapeDtypeStruct((s_len, D_MODEL), BF16),
                   jax.ShapeDtypeStruct((s_len, 512), F32),
                   jax.ShapeDtypeStruct((s_len, 3 * SB_WIDTH), BF16),
                   jax.ShapeDtypeStruct((s_len, 2 * D_MODEL), F32)],
        scratch_shapes=[pltpu.VMEM((tm, D_MODEL), BF16)],
        compiler_params=_cparams(dimension_semantics=("parallel", "arbitrary")),
    )(x, g, w)


def _rope_rot(blk, lane):
    return jnp.where(lane < 80, -pltpu.roll(blk, 112, 1), pltpu.roll(blk, 16, 1))


def _rope_rot_t(blk, lane):
    return jnp.where(lane < 80, pltpu.roll(blk, 112, 1), -pltpu.roll(blk, 16, 1))


def _mla_prep(lat, g_q, g_kv, w_uq, w_uk, w_uv, cosf, sinf):
    s_len = lat.shape[0]
    tm = _row_block(s_len)

    def body(lat_ref, gq_ref, gkv_ref, wuq_ref, wuk_ref, wuv_ref, cos_ref, sin_ref,
             q_ref, k_ref, v_ref, ql_ref, kvl_ref):
        lane = lax.broadcasted_iota(jnp.int32, (tm, LANES), 1)
        cosv, sinv = cos_ref[...], sin_ref[...]
        ql, _, _ = _rms(lat_ref[:, 0:256], gq_ref[...])
        kvl, _, _ = _rms(lat_ref[:, 256:384], gkv_ref[...])
        qlb, kvlb = ql.astype(BF16), kvl.astype(BF16)
        ql_ref[...] = qlb
        kvl_ref[...] = kvlb
        q = _dot(qlb, wuq_ref[...])
        kn = _dot(kvlb, wuk_ref[...])
        v_ref[...] = _dot(kvlb, wuv_ref[...]).astype(BF16)
        kr = pltpu.roll(lat_ref[:, K_R_OFF:K_R_OFF + LANES], 64, 1)
        kr = kr * cosv + _rope_rot(kr, lane) * sinv
        for h in range(MLA_HEADS):
            sl = slice(h * HEAD_PAD, (h + 1) * HEAD_PAD)
            blk = q[:, sl]
            q_ref[:, sl] = (blk * cosv + _rope_rot(blk, lane) * sinv).astype(BF16)
            k_ref[:, sl] = (kn[:, sl] + kr).astype(BF16)

    full = lambda shape: pl.BlockSpec(shape, lambda i: (0, 0))
    rowb = lambda n: pl.BlockSpec((tm, n), lambda i: (i, 0))
    return pl.pallas_call(
        body, name="mla_prep", grid=(s_len // tm,),
        in_specs=[rowb(512), full((1, 256)), full((1, 128)), full((256, 1024)), full((128, 1024)),
                  full((128, 512)), rowb(128), rowb(128)],
        out_specs=[rowb(1024), rowb(1024), rowb(512), rowb(256), rowb(128)],
        out_shape=[jax.ShapeDtypeStruct((s_len, 1024), BF16), jax.ShapeDtypeStruct((s_len, 1024), BF16),
                   jax.ShapeDtypeStruct((s_len, 512), BF16), jax.ShapeDtypeStruct((s_len, 256), BF16),
                   jax.ShapeDtypeStruct((s_len, 128), BF16)],
        compiler_params=_cparams(dimension_semantics=("parallel",)),
    )(lat, g_q, g_kv, w_uq, w_uk, w_uv, cosf, sinf)


ATTN_TQ = 1024
ATTN_TH = 512
ATTN_TK = 256


def _attn_blocks(s_len):
    tq, th, tk = min(s_len, ATTN_TQ), min(s_len, ATTN_TH), min(s_len, ATTN_TK)
    return tq, th, tk, tq // tk


def _chains(tq, th):
    return [(hh, r0) for hh in range(2) for r0 in range(0, tq, th)]


def _diag_mask(th, tk, r0, sub, strict):
    lo, hi = sub * tk, (sub + 1) * tk - 1
    last, first = r0 + th - 1, r0
    if (lo >= last) if strict else (lo > last):
        return "none"
    if (hi < first) if strict else (hi <= first):
        return "all"
    row = lax.broadcasted_iota(jnp.int32, (th, tk), 0) + r0
    col = lax.broadcasted_iota(jnp.int32, (th, tk), 1) + lo
    return col < row if strict else col <= row


def _mla_fwd(q, k, v):
    s_len = q.shape[0]
    tq, th, tk, nsub = _attn_blocks(s_len)
    chains = _chains(tq, th)
    nh = tq // th

    def body(q_ref, k_ref, v_ref, o_ref, lse_ref):
        i = pl.program_id(1)
        lane = lax.broadcasted_iota(jnp.int32, (th, LANES), 1)
        hsl = [slice(hh * HEAD_PAD, (hh + 1) * HEAD_PAD) for hh in range(2)]

        def step(kb, carry, sub):
            rows = pl.ds(pl.multiple_of(kb * tk, tk), tk)
            vblk = v_ref[rows, :]
            masks = ["all" if sub is None else _diag_mask(th, tk, r0, sub, strict=False) for _, r0 in chains]
            live = [n for n, m in enumerate(masks) if not (isinstance(m, str) and m == "none")]
            s = {n: _dot_nt(q_ref[chains[n][1]:chains[n][1] + th, hsl[chains[n][0]]], k_ref[rows, hsl[chains[n][0]]])
                 for n in live}
            new = list(carry)
            pb, alpha = {}, {}
            for n in live:
                m, l, _ = carry[n]
                sn = s[n] * MLA_SCALE
                if not isinstance(masks[n], str):
                    sn = jnp.where(masks[n], sn, NEG_BIG)
                m_new = jnp.maximum(m, jnp.max(sn, axis=-1, keepdims=True))
                alpha[n] = jnp.exp(m - m_new)
                p = jnp.exp(sn - m_new)
                pb[n] = p.astype(BF16)
                new[n] = (m_new, alpha[n] * l + jnp.sum(p, axis=-1, keepdims=True), None)
            pv = {n: _dot(pb[n], vblk) for n in live}
            for n in live:
                new[n] = (new[n][0], new[n][1], alpha[n] * carry[n][2] + pv[n])
            return tuple(new)

        init = (jnp.full((th, 1), NEG_BIG, F32), jnp.zeros((th, 1), F32), jnp.zeros((th, LANES), F32))
        carry = lax.fori_loop(0, i * nsub, lambda kb, cy: step(kb, cy, None), (init,) * len(chains))
        for sub in range(nsub):
            carry = step(i * nsub + sub, carry, sub)
        for c in range(nh):
            (m0, l0, a0), (m1, l1, a1) = carry[c], carry[nh + c]
            rs = slice(c * th, (c + 1) * th)
            o_ref[rs, :] = jnp.where(lane < 64, a0 / l0, a1 / l1).astype(BF16)
            lse_ref[rs, :] = jnp.where(lane < 64, m0 + jnp.log(l0), m1 + jnp.log(l1))

    return pl.pallas_call(
        body, name="mla_fwd", grid=(4, s_len // tq),
        in_specs=[pl.BlockSpec((tq, 2 * HEAD_PAD), lambda p, i: (i, p)),
                  pl.BlockSpec((s_len, 2 * HEAD_PAD), lambda p, i: (0, p)),
                  pl.BlockSpec((s_len, LANES), lambda p, i: (0, p))],
        out_specs=[pl.BlockSpec((tq, LANES), lambda p, i: (i, p)),
                   pl.BlockSpec((None, tq, LANES), lambda p, i: (p, i, 0))],
        out_shape=[jax.ShapeDtypeStruct((s_len, 512), BF16), jax.ShapeDtypeStruct((4, s_len, LANES), F32)],
        compiler_params=_cparams(40, dimension_semantics=("parallel", "arbitrary")),
    )(q, k, v)


def _mla_bwd(q, k, v, o, do, lse):
    s_len = q.shape[0]
    tq, th, tk, nsub = _attn_blocks(s_len)
    chains = _chains(tq, th)

    def body(q_ref, k_ref, v_ref, o_ref, do_ref, lse_ref, dq_ref, dk_ref, dv_ref):
        i = pl.program_id(1)
        lane = lax.broadcasted_iota(jnp.int32, (th, LANES), 1)

        @pl.when(i == 0)
        def _():
            dk_ref[...] = jnp.zeros_like(dk_ref)
            dv_ref[...] = jnp.zeros_like(dv_ref)

        hsl = [slice(hh * HEAD_PAD, (hh + 1) * HEAD_PAD) for hh in range(2)]
        qs, dos, deltas, lses = [], [], [], []
        for hh, r0 in chains:
            rs = slice(r0, r0 + th)
            qs.append(q_ref[rs, hsl[hh]])
            doh = jnp.where((lane // 64) == hh, do_ref[rs, :], jnp.zeros((), BF16))
            dos.append(doh)
            deltas.append(jnp.sum(doh.astype(F32) * o_ref[rs, :].astype(F32), axis=-1, keepdims=True))
            lses.append(lse_ref[rs, 64 * hh:64 * hh + 1])

        def step(kb, dqs, sub):
            rows = pl.ds(pl.multiple_of(kb * tk, tk), tk)
            vblk = v_ref[rows, :]
            new, p_all, do_all = [], [], []
            ds_h, q_h = [[], []], [[], []]
            for c, (hh, r0) in enumerate(chains):
                mask = "all" if sub is None else _diag_mask(th, tk, r0, sub, strict=False)
                if isinstance(mask, str) and mask == "none":
                    new.append(dqs[c])
                    continue
                kblk = k_ref[rows, hsl[hh]]
                s = _dot_nt(qs[c], kblk) * MLA_SCALE
                if not isinstance(mask, str):
                    s = jnp.where(mask, s, NEG_BIG)
                p = jnp.exp(s - lses[c])
                dp = _dot_nt(dos[c], vblk)
                ds = (p * (dp - deltas[c]) * MLA_SCALE).astype(BF16)
                p_all.append(p.astype(BF16))
                do_all.append(dos[c])
                ds_h[hh].append(ds)
                q_h[hh].append(qs[c])
                new.append(dqs[c] + _dot(ds, kblk))
            dv_ref[rows, :] += _dot_tn(jnp.concatenate(p_all, axis=0), jnp.concatenate(do_all, axis=0))
            for hh in range(2):
                dk_ref[rows, hsl[hh]] += _dot_tn(jnp.concatenate(ds_h[hh], axis=0),
                                                 jnp.concatenate(q_h[hh], axis=0))
            return tuple(new)

        zero = jnp.zeros((th, LANES), F32)
        dqs = lax.fori_loop(0, i * nsub, lambda kb, cy: step(kb, cy, None), (zero,) * len(chains))
        for sub in range(nsub):
            dqs = step(i * nsub + sub, dqs, sub)
        for c, (hh, r0) in enumerate(chains):
            dq_ref[r0:r0 + th, hsl[hh]] = dqs[c]

    return pl.pallas_call(
        body, name="mla_bwd", grid=(4, s_len // tq),
        in_specs=[pl.BlockSpec((tq, 2 * HEAD_PAD), lambda p, i: (i, p)),
                  pl.BlockSpec((s_len, 2 * HEAD_PAD), lambda p, i: (0, p)),
                  pl.BlockSpec((s_len, LANES), lambda p, i: (0, p)),
                  pl.BlockSpec((tq, LANES), lambda p, i: (i, p)),
                  pl.BlockSpec((tq, LANES), lambda p, i: (i, p)),
                  pl.BlockSpec((None, tq, LANES), lambda p, i: (p, i, 0))],
        out_specs=[pl.BlockSpec((tq, 2 * HEAD_PAD), lambda p, i: (i, p)),
                   pl.BlockSpec((s_len, 2 * HEAD_PAD), lambda p, i: (0, p)),
                   pl.BlockSpec((s_len, LANES), lambda p, i: (0, p))],
        out_shape=[jax.ShapeDtypeStruct((s_len, 1024), F32), jax.ShapeDtypeStruct((s_len, 1024), F32),
                   jax.ShapeDtypeStruct((s_len, 512), F32)],
        compiler_params=_cparams(56, dimension_semantics=("arbitrary", "arbitrary")),
    )(q, k, v, o, do, lse)


def _log_sigmoids(z):
    sp = jnp.log(1.0 + jnp.exp(-jnp.abs(z)))
    return jnp.minimum(z, 0.0) - sp, jnp.minimum(-z, 0.0) - sp


def _split_dot(x, w, parts, nt=False):
    dot = _dot_nt if nt else _dot
    out = None
    for _ in range(parts):
        xb = x.astype(BF16)
        t = dot(xb, w)
        out = t if out is None else out + t
        x = x - xb.astype(F32)
    return out


def _sb_fwd(sb):
    s_len = sb.shape[0]
    tq, th, tk, nsub = _attn_blocks(s_len)
    chains = _chains(tq, th)
    nh = tq // th
    assert s_len // tk <= 64

    def body(q_ref, k_ref, v_ref, o_ref, r_ref):
        i = pl.program_id(1)
        lane = lax.broadcasted_iota(jnp.int32, (th, LANES), 1)
        upper = (lax.broadcasted_iota(jnp.int32, (tk, tk), 0)
                 > lax.broadcasted_iota(jnp.int32, (tk, tk), 1)).astype(BF16)
        qs = [jnp.where((lane // 64) == hh, q_ref[r0:r0 + th, :], jnp.zeros((), BF16)) for hh, r0 in chains]

        def step(kb, carry, sub):
            rows = pl.ds(pl.multiple_of(kb * tk, tk), tk)
            kblk, vblk = k_ref[rows, :], v_ref[rows, :]
            masks = ["all" if sub is None else _diag_mask(th, tk, r0, sub, strict=True) for _, r0 in chains]
            live = [n for n, m in enumerate(masks) if not (isinstance(m, str) and m == "none")]
            masked = {n: not isinstance(masks[n], str) for n in live}
            z = {n: _dot_nt(qs[n], kblk) for n in live}
            lb, lom = {}, {}
            for n in live:
                lb[n], lom[n] = _log_sigmoids(z[n])
                if masked[n]:
                    lom[n] = jnp.where(masks[n], lom[n], 0.0)
            suf = {n: _split_dot(lom[n], upper, 2) for n in live}
            a = {}
            for n in live:
                a[n] = jnp.exp(lb[n] + suf[n] + carry[n][0])
                if masked[n]:
                    a[n] = jnp.where(masks[n], a[n], 0.0)
            pv = {n: _dot(a[n].astype(BF16), vblk) for n in live}
            new = list(carry)
            for n in live:
                c, acc, r = carry[n]
                rs = suf[n][:, 0:1] + lom[n][:, 0:1]
                new[n] = (c + rs, acc + pv[n], jnp.where(lane == 64 * chains[n][0] + kb, rs, r))
            return tuple(new)

        init = (jnp.zeros((th, 1), F32), jnp.zeros((th, LANES), F32), jnp.zeros((th, LANES), F32))
        carry = (init,) * len(chains)
        for sub in reversed(range(nsub)):
            carry = step(i * nsub + sub, carry, sub)
        carry = lax.fori_loop(0, i * nsub, lambda t, cy: step(i * nsub - 1 - t, cy, None), carry)
        for n in range(nh):
            rs = slice(n * th, (n + 1) * th)
            o_ref[rs, :] = jnp.where(lane < 64, carry[n][1], carry[nh + n][1]).astype(BF16)
            r_ref[rs, :] = jnp.where(lane < 64, carry[n][2], carry[nh + n][2])

    return pl.pallas_call(
        body, name="sb_fwd", grid=(4, s_len // tq),
        in_specs=[pl.BlockSpec((tq, LANES), lambda p, i: (i, p)),
                  pl.BlockSpec((s_len, LANES), lambda p, i: (0, 4 + p)),
                  pl.BlockSpec((s_len, LANES), lambda p, i: (0, 8 + p))],
        out_specs=[pl.BlockSpec((tq, LANES), lambda p, i: (i, p)),
                   pl.BlockSpec((None, tq, LANES), lambda p, i: (p, i, 0))],
        out_shape=[jax.ShapeDtypeStruct((s_len, 512), BF16), jax.ShapeDtypeStruct((4, s_len, LANES), F32)],
        compiler_params=_cparams(40, dimension_semantics=("parallel", "arbitrary")),
    )(sb, sb, sb)


def _sb_bwd(sb, do, r):
    s_len = sb.shape[0]
    tq, th, tk, nsub = _attn_blocks(s_len)
    chains = _chains(tq, th)
    nh = tq // th

    def body(q_ref, k_ref, v_ref, do_ref, r_ref, dq_ref, dk_ref, dv_ref):
        i = pl.program_id(1)
        lane = lax.broadcasted_iota(jnp.int32, (th, LANES), 1)
        upper = (lax.broadcasted_iota(jnp.int32, (tk, tk), 0)
                 > lax.broadcasted_iota(jnp.int32, (tk, tk), 1)).astype(BF16)
        tri = (lax.broadcasted_iota(jnp.int32, (LANES, LANES), 0)
               > lax.broadcasted_iota(jnp.int32, (LANES, LANES), 1)).astype(BF16)

        @pl.when(i == 0)
        def _():
            dk_ref[...] = jnp.zeros_like(dk_ref)
            dv_ref[...] = jnp.zeros_like(dv_ref)

        qs, dos, rights = [], [], []
        for hh, r0 in chains:
            rs = slice(r0, r0 + th)
            hm = (lane // 64) == hh
            qs.append(jnp.where(hm, q_ref[rs, :], jnp.zeros((), BF16)))
            dos.append(jnp.where(hm, do_ref[rs, :], jnp.zeros((), BF16)))
            rights.append(_split_dot(jnp.where(hm, r_ref[rs, :], 0.0), tri, 3))

        def step(kb, carry, sub):
            rows = pl.ds(pl.multiple_of(kb * tk, tk), tk)
            kblk, vblk = k_ref[rows, :], v_ref[rows, :]
            new, a_all, do_all, dz_all, q_all = [], [], [], [], []
            for n, ((hh, r0), (pre, dq)) in enumerate(zip(chains, carry)):
                mask = "all" if sub is None else _diag_mask(th, tk, r0, sub, strict=True)
                if isinstance(mask, str) and mask == "none":
                    new.append((pre, dq))
                    continue
                c = jnp.sum(jnp.where(lane == 64 * hh + kb, rights[n], 0.0), axis=-1, keepdims=True)
                z = _dot_nt(qs[n], kblk)
                lb, lom = _log_sigmoids(z)
                if not isinstance(mask, str):
                    lom = jnp.where(mask, lom, 0.0)
                suf = _split_dot(lom, upper, 2)
                a = jnp.exp(lb + suf + c)
                if not isinstance(mask, str):
                    a = jnp.where(mask, a, 0.0)
                g = a * _dot_nt(dos[n], vblk)
                left = _split_dot(g, upper, 1, nt=True) + pre
                sig = jnp.exp(lb)
                dz = g * (1.0 - sig) - sig * left
                if not isinstance(mask, str):
                    dz = jnp.where(mask, dz, 0.0)
                dzb = dz.astype(BF16)
                a_all.append(a.astype(BF16))
                do_all.append(dos[n])
                dz_all.append(dzb)
                q_all.append(qs[n])
                new.append((left[:, tk - 1:tk] + g[:, tk - 1:tk], dq + _dot(dzb, kblk)))
            dv_ref[rows, :] += _dot_tn(jnp.concatenate(a_all, axis=0), jnp.concatenate(do_all, axis=0))
            dk_ref[rows, :] += _dot_tn(jnp.concatenate(dz_all, axis=0), jnp.concatenate(q_all, axis=0))
            return tuple(new)

        init = (jnp.zeros((th, 1), F32), jnp.zeros((th, LANES), F32))
        carry = lax.fori_loop(0, i * nsub, lambda kb, cy: step(kb, cy, None), (init,) * len(chains))
        for sub in range(nsub):
            carry = step(i * nsub + sub, carry, sub)
        for n in range(nh):
            dq_ref[n * th:(n + 1) * th, :] = jnp.where(lane < 64, carry[n][1], carry[nh + n][1]) * SB_SCALE

    return pl.pallas_call(
        body, name="sb_bwd", grid=(4, s_len // tq),
        in_specs=[pl.BlockSpec((tq, LANES), lambda p, i: (i, p)),
                  pl.BlockSpec((s_len, LANES), lambda p, i: (0, 4 + p)),
                  pl.BlockSpec((s_len, LANES), lambda p, i: (0, 8 + p)),
                  pl.BlockSpec((tq, LANES), lambda p, i: (i, p)),
                  pl.BlockSpec((None, tq, LANES), lambda p, i: (p, i, 0))],
        out_specs=[pl.BlockSpec((tq, LANES), lambda p, i: (i, p)),
                   pl.BlockSpec((s_len, LANES), lambda p, i: (0, p)),
                   pl.BlockSpec((s_len, LANES), lambda p, i: (0, p))],
        out_shape=[jax.ShapeDtypeStruct((s_len, 512), F32)] * 3,
        compiler_params=_cparams(48, dimension_semantics=("arbitrary", "arbitrary")),
    )(sb, sb, sb, do, r)


def _merge_fwd(x, oa, ob, gates, bg, wa, wb, wo):
    s_len = x.shape[0]
    tm = _row_block(s_len)

    def body(x_ref, oa_ref, ob_ref, g_ref, bg_ref, wa_ref, wb_ref, wo_ref, y_ref):
        pa = _dot(oa_ref[...], wa_ref[...])
        pb = _dot(ob_ref[...], wb_ref[...])
        merged = (_sigmoid(g_ref[:, 0:D_MODEL] + bg_ref[0:1, :]) * pa
                  + _sigmoid(g_ref[:, D_MODEL:2 * D_MODEL] + bg_ref[1:2, :]) * pb)
        y_ref[...] = x_ref[...] + _dot(merged.astype(BF16), wo_ref[...])

    full = lambda shape: pl.BlockSpec(shape, lambda i: (0, 0))
    rowb = lambda n: pl.BlockSpec((tm, n), lambda i: (i, 0))
    return pl.pallas_call(
        body, name="merge_fwd", grid=(s_len // tm,),
        in_specs=[rowb(1024), rowb(512), rowb(512), rowb(2048), full((2, 1024)), full((512, 1024)),
                  full((512, 1024)), full((1024, 1024))],
        out_specs=rowb(1024),
        out_shape=jax.ShapeDtypeStruct((s_len, D_MODEL), F32),
        compiler_params=_cparams(48, dimension_semantics=("parallel",)),
    )(x, oa, ob, gates, bg, wa, wb, wo)


def _merge_bwd(dx1, oa, ob, gates, bg, wa, wb, wo):
    s_len = dx1.shape[0]
    tm = _row_block(s_len)

    def body(dx_ref, oa_ref, ob_ref, g_ref, bg_ref, wa_ref, wb_ref, wo_ref,
             doa_ref, dob_ref, dgate_ref, dpa_ref, dpb_ref, merged_ref, dxb_ref, dbg_ref):
        first = pl.program_id(0) == 0
        dxb = dx_ref[...].astype(BF16)
        dxb_ref[...] = dxb
        pa = _dot(oa_ref[...], wa_ref[...])
        pb = _dot(ob_ref[...], wb_ref[...])
        sa = _sigmoid(g_ref[:, 0:D_MODEL] + bg_ref[0:1, :])
        sbg = _sigmoid(g_ref[:, D_MODEL:2 * D_MODEL] + bg_ref[1:2, :])
        merged_ref[...] = (sa * pa + sbg * pb).astype(BF16)
        dm = _dot_nt(dxb, wo_ref[...])
        dpa = (dm * sa).astype(BF16)
        dpb = (dm * sbg).astype(BF16)
        dpa_ref[...] = dpa
        dpb_ref[...] = dpb
        dga = dm * pa * sa * (1.0 - sa)
        dgb = dm * pb * sbg * (1.0 - sbg)
        dgate_ref[:, 0:D_MODEL] = dga.astype(BF16)
        dgate_ref[:, D_MODEL:2 * D_MODEL] = dgb.astype(BF16)
        _acc_rows(dbg_ref.at[0:1, :], dga, first)
        _acc_rows(dbg_ref.at[1:2, :], dgb, first)
        doa_ref[...] = _dot_nt(dpa, wa_ref[...]).astype(BF16)
        dob_ref[...] = _dot_nt(dpb, wb_ref[...]).astype(BF16)

    full = lambda shape: pl.BlockSpec(shape, lambda i: (0, 0))
    rowb = lambda n: pl.BlockSpec((tm, n), lambda i: (i, 0))
    sds = lambda n, dt: jax.ShapeDtypeStruct((s_len, n), dt)
    return pl.pallas_call(
        body, name="merge_bwd", grid=(s_len // tm,),
        in_specs=[rowb(1024), rowb(512), rowb(512), rowb(2048), full((2, 1024)), full((512, 1024)),
                  full((512, 1024)), full((1024, 1024))],
        out_specs=[rowb(512), rowb(512), rowb(2048), rowb(1024), rowb(1024), rowb(1024), rowb(1024),
                   full((2, 1024))],
        out_shape=[sds(512, BF16), sds(512, BF16), sds(2048, BF16), sds(1024, BF16), sds(1024, BF16),
                   sds(1024, BF16), sds(1024, BF16), jax.ShapeDtypeStruct((2, 1024), F32)],
        compiler_params=_cparams(48, dimension_semantics=("arbitrary",)),
    )(dx1, oa, ob, gates, bg, wa, wb, wo)


def _mem_kv(mem, g, w):
    m_len = mem.shape[0]

    def body(mem_ref, g_ref, w_ref, mn_ref, kv_ref):
        mn, _, _ = _rms(mem_ref[...], g_ref[...])
        mnb = mn.astype(BF16)
        mn_ref[...] = mnb
        kv_ref[...] = _dot(mnb, w_ref[...]).astype(BF16)

    return pl.pallas_call(
        body, name="mem_kv",
        out_shape=[jax.ShapeDtypeStruct((m_len, D_MODEL), BF16), jax.ShapeDtypeStruct((m_len, 1024), BF16)],
    )(mem, g, w)


def _mem_bwd(mem, g, w, mn, dkv):
    def body(mem_ref, g_ref, w_ref, mn_ref, dkv_ref, dw_ref, dg_ref):
        dkvb = dkv_ref[...].astype(BF16)
        dw_ref[...] = _dot_tn(mn_ref[...], dkvb)
        dmn = _dot_nt(dkvb, w_ref[...])
        _, xh, _ = _rms(mem_ref[...], g_ref[...])
        dg_ref[...] = jnp.sum(dmn * xh, axis=0, keepdims=True)

    return pl.pallas_call(
        body, name="mem_bwd",
        out_shape=[jax.ShapeDtypeStruct((D_MODEL, 1024), F32), jax.ShapeDtypeStruct((1, D_MODEL), F32)],
    )(mem, g, w, mn, dkv)


def _xattn_heads(xqb, kv_ref, m_len):
    ps = []
    for h in range(X_HEADS):
        hs = slice(h * X_HEAD_DIM, (h + 1) * X_HEAD_DIM)
        s = _dot_nt(xqb[:, hs], kv_ref[:, hs]) * X_SCALE
        e = jnp.exp(s - jnp.max(s, axis=-1, keepdims=True))
        ps.append(e / jnp.sum(e, axis=-1, keepdims=True))
    return ps


def _xattn_fwd(x1, g, wxq, kv, wxo):
    s_len, m_len = x1.shape[0], kv.shape[0]
    tm = _row_block(s_len)

    def body(x_ref, g_ref, wq_ref, kv_ref, wo_ref, y_ref):
        hx, _, _ = _rms(x_ref[...], g_ref[...])
        xqb = _dot(hx.astype(BF16), wq_ref[...]).astype(BF16)
        ps = _xattn_heads(xqb, kv_ref, m_len)
        xo = jnp.concatenate(
            [_dot(ps[h].astype(BF16), kv_ref[:, 512 + h * X_HEAD_DIM:512 + (h + 1) * X_HEAD_DIM])
             for h in range(X_HEADS)], axis=-1)
        y_ref[...] = x_ref[...] + _dot(xo.astype(BF16), wo_ref[...])

    full = lambda shape: pl.BlockSpec(shape, lambda i: (0, 0))
    rowb = lambda n: pl.BlockSpec((tm, n), lambda i: (i, 0))
    return pl.pallas_call(
        body, name="xattn_fwd", grid=(s_len // tm,),
        in_specs=[rowb(1024), full((1, 1024)), full((1024, 512)), full((m_len, 1024)), full((512, 1024))],
        out_specs=rowb(1024),
        out_shape=jax.ShapeDtypeStruct((s_len, D_MODEL), F32),
        compiler_params=_cparams(48, dimension_semantics=("parallel",)),
    )(x1, g, wxq, kv, wxo)


def _xattn_bwd(x1, dx2, g, wxq, kv, wxo):
    s_len, m_len = x1.shape[0], kv.shape[0]
    tm = _row_block(s_len)

    def body(x_ref, dy_ref, g_ref, wq_ref, kv_ref, wo_ref, dx_ref, dwq_ref, dwo_ref, dkv_ref, dg_ref):
        first = pl.program_id(0) == 0
        gv = g_ref[...]
        hx, xh, r = _rms(x_ref[...], gv)
        hxb = hx.astype(BF16)
        xqb = _dot(hxb, wq_ref[...]).astype(BF16)
        ps = _xattn_heads(xqb, kv_ref, m_len)
        dy = dy_ref[...]
        dyb = dy.astype(BF16)
        dxo = _dot_nt(dyb, wo_ref[...])
        xos, dqs, dks, dvs = [], [], [], []
        for h in range(X_HEADS):
            hs = slice(h * X_HEAD_DIM, (h + 1) * X_HEAD_DIM)
            vs = slice(512 + h * X_HEAD_DIM, 512 + (h + 1) * X_HEAD_DIM)
            p = ps[h]
            pb = p.astype(BF16)
            dxoh = dxo[:, hs].astype(BF16)
            xos.append(_dot(pb, kv_ref[:, vs]))
            dp = _dot_nt(dxoh, kv_ref[:, vs])
            ds = (p * (dp - jnp.sum(dp * p, axis=-1, keepdims=True)) * X_SCALE).astype(BF16)
            dvs.append(_dot_tn(pb, dxoh))
            dks.append(_dot_tn(ds, xqb[:, hs]))
            dqs.append(_dot(ds, kv_ref[:, hs]))
        xob = jnp.concatenate(xos, axis=-1).astype(BF16)
        dxqb = jnp.concatenate(dqs, axis=-1).astype(BF16)
        _acc(dwo_ref, _dot_tn(xob, dyb), first)
        _acc(dwq_ref, _dot_tn(hxb, dxqb), first)
        _acc(dkv_ref, jnp.concatenate(dks + dvs, axis=-1), first)
        dhx = _dot_nt(dxqb, wq_ref[...])
        dx, dgr = _rms_bwd(dhx, xh, r, gv)
        dx_ref[...] = dy + dx
        _acc_rows(dg_ref, dgr, first)

    full = lambda shape: pl.BlockSpec(shape, lambda i: (0, 0))
    rowb = lambda n: pl.BlockSpec((tm, n), lambda i: (i, 0))
    return pl.pallas_call(
        body, name="xattn_bwd", grid=(s_len // tm,),
        in_specs=[rowb(1024), rowb(1024), full((1, 1024)), full((1024, 512)), full((m_len, 1024)),
                  full((512, 1024))],
        out_specs=[rowb(1024), full((1024, 512)), full((512, 1024)), full((m_len, 1024)), full((1, 1024))],
        out_shape=[jax.ShapeDtypeStruct((s_len, D_MODEL), F32), jax.ShapeDtypeStruct((1024, 512), F32),
                   jax.ShapeDtypeStruct((512, 1024), F32), jax.ShapeDtypeStruct((m_len, 1024), F32),
                   jax.ShapeDtypeStruct((1, D_MODEL), F32)],
        compiler_params=_cparams(48, dimension_semantics=("arbitrary",)),
    )(x1, dx2, g, wxq, kv, wxo)


FF_TILE = 1408
FF_TILE_BWD = 256


def _ffn_fwd(x2, g, wg, wu, wd):
    s_len = x2.shape[0]
    tm, tf = _row_block(s_len), FF_TILE

    def body(x_ref, g_ref, wg_ref, wu_ref, wd_ref, y_ref, h_scr):
        j = pl.program_id(1)

        @pl.when(j == 0)
        def _():
            hf, _, _ = _rms(x_ref[...], g_ref[...])
            h_scr[...] = hf.astype(BF16)
            y_ref[...] = x_ref[...]

        hb = h_scr[...]
        gt = _dot(hb, wg_ref[...])
        up = _dot(hb, wu_ref[...])
        act = gt * _sigmoid(gt) * up
        y_ref[...] += _dot(act.astype(BF16), wd_ref[...])

    return pl.pallas_call(
        body, name="ffn_fwd", grid=(s_len // tm, D_FF // tf),
        in_specs=[pl.BlockSpec((tm, D_MODEL), lambda i, j: (i, 0)),
                  pl.BlockSpec((1, D_MODEL), lambda i, j: (0, 0)),
                  pl.BlockSpec((D_MODEL, tf), lambda i, j: (0, j)),
                  pl.BlockSpec((D_MODEL, tf), lambda i, j: (0, j)),
                  pl.BlockSpec((tf, D_MODEL), lambda i, j: (j, 0))],
        out_specs=pl.BlockSpec((tm, D_MODEL), lambda i, j: (i, 0)),
        out_shape=jax.ShapeDtypeStruct((s_len, D_MODEL), F32),
        scratch_shapes=[pltpu.VMEM((tm, D_MODEL), BF16)],
        compiler_params=_cparams(48, dimension_semantics=("parallel", "arbitrary")),
    )(x2, g, wg, wu, wd)


def _ffn_bwd(x2, dx3, g, wg, wu, wd):
    s_len = x2.shape[0]
    tm, tf = _row_block(s_len), FF_TILE_BWD
    nf = D_FF // tf

    def body(x_ref, dy_ref, g_ref, wg_ref, wu_ref, wd_ref,
             dx_ref, h_ref, dgt_ref, dup_ref, act_ref, dg_ref, h_scr, dyb_scr, dh_scr):
        i, j = pl.program_id(0), pl.program_id(1)

        @pl.when(j == 0)
        def _():
            hf, _, _ = _rms(x_ref[...], g_ref[...])
            hb = hf.astype(BF16)
            h_scr[...] = hb
            h_ref[...] = hb
            dyb_scr[...] = dy_ref[...].astype(BF16)
            dh_scr[...] = jnp.zeros_like(dh_scr)

        hb = h_scr[...]
        gt = _dot(hb, wg_ref[...])
        up = _dot(hb, wu_ref[...])
        sg = _sigmoid(gt)
        silu = gt * sg
        dact = _dot_nt(dyb_scr[...], wd_ref[...])
        dgt = (dact * up * (sg * (1.0 + gt * (1.0 - sg)))).astype(BF16)
        dup = (dact * silu).astype(BF16)
        dgt_ref[...] = dgt
        dup_ref[...] = dup
        act_ref[...] = (silu * up).astype(BF16)
        dh_scr[...] += _dot_nt(dgt, wg_ref[...]) + _dot_nt(dup, wu_ref[...])

        @pl.when(j == nf - 1)
        def _():
            gv = g_ref[...]
            _, xh, r = _rms(x_ref[...], gv)
            dx, dgr = _rms_bwd(dh_scr[...], xh, r, gv)
            dx_ref[...] = dy_ref[...] + dx
            _acc_rows(dg_ref, dgr, i == 0)

    rowb = pl.BlockSpec((tm, D_MODEL), lambda i, j: (i, 0))
    ffb = pl.BlockSpec((tm, tf), lambda i, j: (i, j))
    return pl.pallas_call(
        body, name="ffn_bwd", grid=(s_len // tm, nf),
        in_specs=[rowb, rowb, pl.BlockSpec((1, D_MODEL), lambda i, j: (0, 0)),
                  pl.BlockSpec((D_MODEL, tf), lambda i, j: (0, j)),
                  pl.BlockSpec((D_MODEL, tf), lambda i, j: (0, j)),
                  pl.BlockSpec((tf, D_MODEL), lambda i, j: (j, 0))],
        out_specs=[rowb, rowb, ffb, ffb, ffb, pl.BlockSpec((1, D_MODEL), lambda i, j: (0, 0))],
        out_shape=[jax.ShapeDtypeStruct((s_len, D_MODEL), F32), jax.ShapeDtypeStruct((s_len, D_MODEL), BF16),
                   jax.ShapeDtypeStruct((s_len, D_FF), BF16), jax.ShapeDtypeStruct((s_len, D_FF), BF16),
                   jax.ShapeDtypeStruct((s_len, D_FF), BF16), jax.ShapeDtypeStruct((1, D_MODEL), F32)],
        scratch_shapes=[pltpu.VMEM((tm, D_MODEL), BF16), pltpu.VMEM((tm, D_MODEL), BF16),
                        pltpu.VMEM((tm, D_MODEL), F32)],
        compiler_params=_cparams(56, dimension_semantics=("arbitrary", "arbitrary")),
    )(x2, dx3, g, wg, wu, wd)


def _loss_head(x3, g, target):
    s_len = x3.shape[0]
    tm = _row_block(s_len)

    def body(x_ref, g_ref, t_ref, sse_ref, dx_ref, dxb_ref, dg_ref):
        first = pl.program_id(0) == 0
        gv = g_ref[...]
        y, xh, r = _rms(x_ref[...], gv)
        err = y - t_ref[...]
        _acc(sse_ref, jnp.broadcast_to(jnp.sum(err * err), (8, LANES)), first)
        dx, dgr = _rms_bwd(err * (1.0 / D_MODEL), xh, r, gv)
        dx_ref[...] = dx
        dxb_ref[...] = dx.astype(BF16)
        _acc_rows(dg_ref, dgr, first)

    rowb = pl.BlockSpec((tm, D_MODEL), lambda i: (i, 0))
    return pl.pallas_call(
        body, name="loss_head", grid=(s_len // tm,),
        in_specs=[rowb, pl.BlockSpec((1, D_MODEL), lambda i: (0, 0)), rowb],
        out_specs=[pl.BlockSpec((8, LANES), lambda i: (0, 0)), rowb, rowb,
                   pl.BlockSpec((1, D_MODEL), lambda i: (0, 0))],
        out_shape=[jax.ShapeDtypeStruct((8, LANES), F32), jax.ShapeDtypeStruct((s_len, D_MODEL), F32),
                   jax.ShapeDtypeStruct((s_len, D_MODEL), BF16), jax.ShapeDtypeStruct((1, D_MODEL), F32)],
        compiler_params=_cparams(dimension_semantics=("arbitrary",)),
    )(x3, g, target)


def _mla_prep_bwd(lat, g_q, g_kv, w_uq, w_uk, w_uv, cosf, sinf, dq, dk, dv):
    s_len = lat.shape[0]
    tm = _row_block(s_len)

    def body(lat_ref, gq_ref, gkv_ref, wuq_ref, wuk_ref, wuv_ref, cos_ref, sin_ref, dq_ref, dk_ref, dv_ref,
             dlat_ref, dqb_ref, dkb_ref, dvb_ref, dgq_ref, dgkv_ref):
        first = pl.program_id(0) == 0
        lane = lax.broadcasted_iota(jnp.int32, (tm, LANES), 1)
        cosv, sinv = cos_ref[...], sin_ref[...]
        gq, gkv = gq_ref[...], gkv_ref[...]
        _, qxh, qr = _rms(lat_ref[:, 0:256], gq)
        _, kxh, kr_ = _rms(lat_ref[:, 256:384], gkv)
        dkr = jnp.zeros((tm, LANES), F32)
        for h in range(MLA_HEADS):
            sl = slice(h * HEAD_PAD, (h + 1) * HEAD_PAD)
            blk = dq_ref[:, sl]
            dqb_ref[:, sl] = (blk * cosv + _rope_rot_t(blk, lane) * sinv).astype(BF16)
            kblk = dk_ref[:, sl]
            dkb_ref[:, sl] = kblk.astype(BF16)
            dkr = dkr + kblk
        dvb = dv_ref[...].astype(BF16)
        dvb_ref[...] = dvb
        dkr = jnp.where((lane >= 64) & (lane < 96), dkr, 0.0)
        dkr = dkr * cosv + _rope_rot_t(dkr, lane) * sinv
        dql = _dot_nt(dqb_ref[...], wuq_ref[...])
        dkvl = _dot_nt(dkb_ref[...], wuk_ref[...]) + _dot_nt(dvb, wuv_ref[...])
        dcq, dgqr = _rms_bwd(dql, qxh, qr, gq)
        dckv, dgkvr = _rms_bwd(dkvl, kxh, kr_, gkv)
        dlat_ref[:, 0:256] = dcq
        dlat_ref[:, 256:384] = dckv
        dlat_ref[:, K_R_OFF:K_R_OFF + LANES] = pltpu.roll(dkr, 64, 1)
        _acc_rows(dgq_ref, dgqr, first)
        _acc_rows(dgkv_ref, dgkvr, first)

    full = lambda shape: pl.BlockSpec(shape, lambda i: (0, 0))
    rowb = lambda n: pl.BlockSpec((tm, n), lambda i: (i, 0))
    sds = lambda n, dt: jax.ShapeDtypeStruct((s_len, n), dt)
    return pl.pallas_call(
        body, name="mla_prep_bwd", grid=(s_len // tm,),
        in_specs=[rowb(512), full((1, 256)), full((1, 128)), full((256, 1024)), full((128, 1024)),
                  full((128, 512)), rowb(128), rowb(128), rowb(1024), rowb(1024), rowb(512)],
        out_specs=[rowb(512), rowb(1024), rowb(1024), rowb(512), full((1, 256)), full((1, 128))],
        out_shape=[sds(512, F32), sds(1024, BF16), sds(1024, BF16), sds(512, BF16),
                   jax.ShapeDtypeStruct((1, 256), F32), jax.ShapeDtypeStruct((1, 128), F32)],
        compiler_params=_cparams(48, dimension_semantics=("arbitrary",)),
    )(lat, g_q, g_kv, w_uq, w_uk, w_uv, cosf, sinf, dq, dk, dv)


def _in_proj_bwd(x, g, w, dx1, dlat, dsbq, dsbk, dsbv, dgates):
    s_len = x.shape[0]
    tm, tn = _row_block(s_len), 512
    nj = D_IN_PAD // tn

    def body(x_ref, g_ref, w_ref, dx1_ref, dlat_ref, dq_ref, dk_ref, dv_ref, dgate_ref,
             gx_ref, dproj_ref, dg_ref, dh_scr):
        i, j = pl.program_id(0), pl.program_id(1)

        @pl.when(j == 0)
        def _():
            dh_scr[...] = jnp.zeros_like(dh_scr)

        def chunk(val):
            vb = val.astype(BF16)
            dproj_ref[...] = vb
            dh_scr[...] += _dot_nt(vb, w_ref[...])

        for jj, ref in ((0, dlat_ref), (1, dq_ref), (2, dk_ref), (3, dv_ref)):
            pl.when(j == jj)(functools.partial(lambda ref: chunk(ref[...]), ref))
        pl.when(j >= 4)(lambda: chunk(dgate_ref[...]))

        @pl.when(j == nj - 1)
        def _():
            gv = g_ref[...]
            _, xh, r = _rms(x_ref[...], gv)
            dx, dgr = _rms_bwd(dh_scr[...], xh, r, gv)
            gx_ref[...] = dx1_ref[...] + dx
            _acc_rows(dg_ref, dgr, i == 0)

    rowb = pl.BlockSpec((tm, D_MODEL), lambda i, j: (i, 0))
    colb = lambda f: pl.BlockSpec((tm, tn), f)
    return pl.pallas_call(
        body, name="in_proj_bwd", grid=(s_len // tm, nj),
        in_specs=[rowb, pl.BlockSpec((1, D_MODEL), lambda i, j: (0, 0)),
                  pl.BlockSpec((D_MODEL, tn), lambda i, j: (0, j)), rowb,
                  colb(lambda i, j: (i, 0)), colb(lambda i, j: (i, 0)), colb(lambda i, j: (i, 0)),
                  colb(lambda i, j: (i, 0)), colb(lambda i, j: (i, jnp.clip(j - 4, 0, 3)))],
        out_specs=[rowb, colb(lambda i, j: (i, j)), pl.BlockSpec((1, D_MODEL), lambda i, j: (0, 0))],
        out_shape=[jax.ShapeDtypeStruct((s_len, D_MODEL), F32), jax.ShapeDtypeStruct((s_len, D_IN_PAD), BF16),
                   jax.ShapeDtypeStruct((1, D_MODEL), F32)],
        scratch_shapes=[pltpu.VMEM((tm, D_MODEL), F32)],
        compiler_params=_cparams(48, dimension_semantics=("arbitrary", "arbitrary")),
    )(x, g, w, dx1, dlat, dsbq, dsbk, dsbv, dgates)


def _adamw(landed, w, m, v, name):
    r, c = w.shape
    lanes = _round_up(c, LANES)
    tb = r
    for cand in range(r, 0, -1):
        if r % cand == 0 and (cand % 8 == 0 or cand == r) and N_DEV * cand * lanes * 4 <= ADAM_BLOCK_BYTES:
            tb = cand
            break
    c1 = 1.0 - ADAM_B1 ** ADAM_STEP
    c2 = 1.0 - ADAM_B2 ** ADAM_STEP

    def body(l_ref, w_ref, m_ref, v_ref, g_ref, d_ref, nm_ref, nv_ref):
        g = l_ref[0]
        for k in range(1, N_DEV):
            g = g + l_ref[k]
        nm = ADAM_B1 * m_ref[...] + (1.0 - ADAM_B1) * g
        nv = ADAM_B2 * v_ref[...] + (1.0 - ADAM_B2) * (g * g)
        g_ref[...] = g
        nm_ref[...] = nm
        nv_ref[...] = nv
        d_ref[...] = -ADAM_LR * ((nm / c1) / (jnp.sqrt(nv / c2) + ADAM_EPS) + ADAM_WD * w_ref[...])

    blk = pl.BlockSpec((tb, c), lambda i: (i, 0))
    return pl.pallas_call(
        body, name=name, grid=(r // tb,),
        in_specs=[pl.BlockSpec((N_DEV, tb, c), lambda i: (0, i, 0)), blk, blk, blk],
        out_specs=[blk, blk, blk, blk],
        out_shape=[jax.ShapeDtypeStruct((r, c), F32)] * 4,
        compiler_params=_cparams(dimension_semantics=("parallel",)),
    )(landed, w, m, v)


def _shard_shape(shape, axis):
    return tuple(d // N_DEV if a == axis else d for a, d in enumerate(shape))


def _split_pieces(full, axis):
    r, c = full.shape
    if axis == 0:
        return full.reshape(N_DEV, r // N_DEV, c)
    return full.reshape(r, N_DEV, c // N_DEV).transpose(1, 0, 2)


def _join_shards(gathered, axis):
    _, r, c = gathered.shape
    if axis == 0:
        return gathered.reshape(N_DEV * r, c)
    return gathered.transpose(1, 0, 2).reshape(r, N_DEV * c)


def _own_row(landed, own, me):
    return lax.dynamic_update_slice(landed, own[None], (me, 0, 0))


def kernel(x, mem, positions, g_mix, w_in, b_gate, g_q_lat, w_uq, g_kv_lat, w_ukv, w_a_proj, w_b_proj, w_o, g_x, g_mem, w_xq, w_xkv, w_xo, g_ffn, w_gate, w_up, w_down, g_final, loss_target, m_g_mix, m_w_in, m_b_gate, m_g_q_lat, m_w_uq, m_g_kv_lat, m_w_ukv, m_w_a_proj, m_w_b_proj, m_w_o, m_g_x, m_g_mem, m_w_xq, m_w_xkv, m_w_xo, m_g_ffn, m_w_gate, m_w_up, m_w_down, m_g_final, v_g_mix, v_w_in, v_b_gate, v_g_q_lat, v_w_uq, v_g_kv_lat, v_w_ukv, v_w_a_proj, v_w_b_proj, v_w_o, v_g_x, v_g_mem, v_w_xq, v_w_xkv, v_w_xo, v_g_ffn, v_w_gate, v_w_up, v_w_down, v_g_final):
    given = dict(locals())
    s_len = x.shape[1]
    x2d = x.reshape(s_len, D_MODEL)
    mem2d = mem.reshape(-1, D_MODEL)
    target = loss_target.reshape(s_len, D_MODEL)

    me = 4 * lax.axis_index("x") + 2 * lax.axis_index("y") + lax.axis_index("c")
    axis_of = {name: axis for name, _, axis in SHARDED}
    shard2d = lambda name, prefix="": given[prefix + name].reshape(
        _shard_shape(dict((n, s) for n, s, _ in SHARDED)[name], axis_of[name]))

    wire = lambda name: shard2d(name) if name == "b_gate" else shard2d(name).astype(BF16)
    first = _exchange(True, [wire(n) for n in GATHER_FIRST], "weights_gather_first")
    rest_src = [wire(n) for n in GATHER_REST]
    rest_handle, rest_token = _exchange_start(True, rest_src, first[0], "weights_gather_rest_start")
    wts = {n: _join_shards(g, axis_of[n]) for n, g in zip(GATHER_FIRST, first)}
    g_mix_t = g_mix + rest_token[0:1, 0:1]

    w_in_p = jnp.concatenate([wts["w_in"][:, :416], jnp.zeros((D_MODEL, 96), BF16), wts["w_in"][:, 416:]], axis=1)
    w_uq_p = jnp.pad(wts["w_uq"].reshape(256, MLA_HEADS, 96), ((0, 0), (0, 0), (0, 32))).reshape(256, 1024)
    ukv = wts["w_ukv"].reshape(128, MLA_HEADS, 128)
    w_uk_p = jnp.pad(ukv[:, :, :64], ((0, 0), (0, 0), (0, 64))).reshape(128, 1024)
    w_uv = ukv[:, :, 64:].reshape(128, 512)
    bg = wts["b_gate"]

    inv_freq = ROPE_THETA ** (-jnp.arange(0, MLA_ROPE, 2, dtype=F32) / MLA_ROPE)
    ang = positions.reshape(s_len).astype(F32)[:, None] * inv_freq
    cos16, sin16 = jnp.cos(ang), jnp.sin(ang)
    cosf = jnp.concatenate([jnp.ones((s_len, 64), F32), cos16, cos16, jnp.ones((s_len, 32), F32)], axis=1)
    sinf = jnp.concatenate([jnp.zeros((s_len, 64), F32), sin16, sin16, jnp.zeros((s_len, 32), F32)], axis=1)

    h1, lat, sb, gates = _in_proj(x2d, g_mix_t, w_in_p)
    qa, ka, va, q_lat, kv_lat = _mla_prep(lat, g_q_lat, g_kv_lat, w_uq_p, w_uk_p, w_uv, cosf, sinf)
    oa, lse = _mla_fwd(qa, ka, va)
    ob, sb_r = _sb_fwd(sb)
    rest = _exchange_wait(True, rest_handle, ob, "weights_gather_rest_wait")
    for n, g, own in zip(GATHER_REST, rest, rest_src):
        wts[n] = _join_shards(_own_row(g, own, me), axis_of[n])
    x1 = _merge_fwd(x2d, oa, ob, gates, bg, wts["w_a_proj"], wts["w_b_proj"], wts["w_o"])
    mn, xkv = _mem_kv(mem2d, g_mem, wts["w_xkv"])
    x2 = _xattn_fwd(x1, g_x, wts["w_xq"], xkv, wts["w_xo"])
    x3 = _ffn_fwd(x2, g_ffn, wts["w_gate"], wts["w_up"], wts["w_down"])
    g_final2d = g_final.reshape(1, D_MODEL)
    sse, dx3, dx3b, dg_final = _loss_head(x3, g_final2d, target)
    loss = lax.psum(sse[0, 0] * (0.5 / D_MODEL), ("x", "y", "c"))

    pieces = {}

    def start_group(names, grads, tag):
        for n in names:
            pieces[n] = _split_pieces(grads[n], axis_of[n])
        return _exchange_start(False, [pieces[n] for n in names], pieces[names[0]], "grads_" + tag + "_start")

    dx2, hf, dgt, dup, act, dg_ffn = _ffn_bwd(x2, dx3, g_ffn, wts["w_gate"], wts["w_up"], wts["w_down"])
    ffn_handle, ffn_token = start_group(SCATTER_FFN, {
        "w_gate": _tn_matmul(hf, dgt, "dw_gate", tn=FF_TILE),
        "w_up": _tn_matmul(hf, dup, "dw_up", tn=FF_TILE),
        "w_down": _tn_matmul(act, dx3b, "dw_down", tka=FF_TILE)}, "ffn")
    dx1, dw_xq, dw_xo, dxkv, dg_x = _xattn_bwd(x1, dx2, g_x + ffn_token[0:1, 0:1], wts["w_xq"], xkv, wts["w_xo"])
    dw_xkv, dg_mem = _mem_bwd(mem2d, g_mem, wts["w_xkv"], mn, dxkv)
    doa, dob, dgates, dpa, dpb, merged, dx1b, dbg = _merge_bwd(
        dx1, oa, ob, gates, bg, wts["w_a_proj"], wts["w_b_proj"], wts["w_o"])
    mid_handle, mid_token = start_group(SCATTER_MID, {
        "w_xq": dw_xq, "w_xkv": dw_xkv, "w_xo": dw_xo,
        "w_a_proj": _tn_matmul(oa, dpa, "dw_a"),
        "w_b_proj": _tn_matmul(ob, dpb, "dw_b"),
        "w_o": _tn_matmul(merged, dx1b, "dw_o")}, "mid")
    dsbq, dsbk, dsbv = _sb_bwd(sb, dob + mid_token[0:1, 0:1].astype(BF16), sb_r)
    dqa, dka, dva = _mla_bwd(qa, ka, va, oa, doa, lse)
    dlat, dqb, dkb, dvb, dg_q, dg_kv = _mla_prep_bwd(
        lat, g_q_lat, g_kv_lat, w_uq_p, w_uk_p, w_uv, cosf, sinf, dqa, dka, dva)
    grad_x, dproj, dg_mix = _in_proj_bwd(x2d, g_mix, w_in_p, dx1, dlat, dsbq, dsbk, dsbv, dgates)
    dw_in_p = _tn_matmul(h1, dproj, "dw_in")
    dw_uq_p = _tn_matmul(q_lat, dqb, "dw_uq")
    dw_uk_p = _tn_matmul(kv_lat, dkb, "dw_uk")
    dw_uv = _tn_matmul(kv_lat, dvb, "dw_uv")
    last_grads = {
        "w_in": jnp.concatenate([dw_in_p[:, :416], dw_in_p[:, 512:]], axis=1),
        "b_gate": dbg,
        "w_uq": dw_uq_p.reshape(256, MLA_HEADS, 128)[:, :, :96].reshape(256, 768),
        "w_ukv": jnp.concatenate([dw_uk_p.reshape(128, MLA_HEADS, 128)[:, :, :64],
                                  dw_uv.reshape(128, MLA_HEADS, 64)], axis=2).reshape(128, 1024),
    }
    rep_grads = {"g_mix": dg_mix, "g_q_lat": dg_q, "g_kv_lat": dg_kv, "g_x": dg_x, "g_mem": dg_mem,
                 "g_ffn": dg_ffn, "g_final": dg_final}
    rep_cat = lambda prefix, src: jnp.concatenate(
        [src[prefix + n].reshape(-1) for n, _ in REPLICATED]).reshape(-1, LANES)
    rep_src = jnp.broadcast_to(rep_cat("", rep_grads), (N_DEV,) + rep_cat("", rep_grads).shape)
    last = _exchange(False, [_split_pieces(last_grads[n], axis_of[n]) for n in SCATTER_LAST] + [rep_src],
                     "grads_last")
    landed = dict(zip(SCATTER_LAST, last[:-1]))
    rep_landed = last[-1]
    for names, handle, tag in ((SCATTER_FFN, ffn_handle, "ffn"), (SCATTER_MID, mid_handle, "mid")):
        got = _exchange_wait(False, handle, rep_landed, "grads_" + tag + "_wait")
        for n, g in zip(names, got):
            landed[n] = _own_row(g, lax.dynamic_index_in_dim(pieces[n], me, 0, keepdims=False), me)

    res = {}
    for name, _, _ in SHARDED:
        outs = _adamw(landed[name], shard2d(name), shard2d(name, "m_"), shard2d(name, "v_"), "adamw_" + name)
        res[name] = [o.reshape(given[name].shape) for o in outs]
    rep_outs = _adamw(rep_landed, rep_cat("", given), rep_cat("m_", given), rep_cat("v_", given), "adamw_gains")
    off = 0
    for name, n in REPLICATED:
        res[name] = [o.reshape(-1)[off:off + n].reshape(given[name].shape) for o in rep_outs]
        off += n
    result = [loss, grad_x.reshape(x.shape)]
    for k in range(4):
        result.extend(res[name][k] for name in WEIGHT_ORDER)
    return tuple(result)
```

```python
import functools
import math

import jax
import jax.numpy as jnp
from jax import lax
from jax.experimental import pallas as pl
from jax.experimental.pallas import tpu as pltpu

F32 = jnp.float32
BF16 = jnp.bfloat16

D_MODEL = 1024
MLA_HEADS = 8
MLA_Q_RANK = 256
MLA_KV_RANK = 128
MLA_NOPE = 64
MLA_ROPE = 32
MLA_V = 64
ROPE_THETA = 10000.0
SB_WIDTH = 512
X_HEADS = 4
X_HEAD_DIM = 128
D_FF = 2816
EPS = 1e-6
D_IN = 4000
D_IN_PAD = 4096
K_R_OFF = 384
LANES = 128
HEAD_PAD = 128
MLA_SCALE = 1.0 / math.sqrt(MLA_NOPE + MLA_ROPE)
SB_SCALE = 0.125
LOG2E = math.log2(math.e)
LN2 = math.log(2.0)
MLA_Q_FOLD = MLA_SCALE * LOG2E
SB_Q_FOLD = SB_SCALE * LOG2E
X_SCALE = 1.0 / math.sqrt(X_HEAD_DIM)
NEG_BIG = -1e30

ADAM_LR = 0.001
ADAM_B1 = 0.9
ADAM_B2 = 0.999
ADAM_EPS = 1e-08
ADAM_WD = 0.01
ADAM_STEP = 10

N_DEV = 8
MIB = 1024 * 1024
ADAM_BLOCK_BYTES = 4 * MIB

SHARDED = (
    ("w_in", (D_MODEL, D_IN), 1),
    ("b_gate", (2, D_MODEL), 1),
    ("w_uq", (MLA_Q_RANK, 768), 1),
    ("w_ukv", (MLA_KV_RANK, 1024), 1),
    ("w_a_proj", (512, D_MODEL), 1),
    ("w_b_proj", (512, D_MODEL), 1),
    ("w_o", (D_MODEL, D_MODEL), 0),
    ("w_xq", (D_MODEL, 512), 0),
    ("w_xkv", (D_MODEL, 1024), 0),
    ("w_xo", (512, D_MODEL), 1),
    ("w_gate", (D_MODEL, D_FF), 1),
    ("w_up", (D_MODEL, D_FF), 1),
    ("w_down", (D_FF, D_MODEL), 0),
)
GATHER_FIRST = ("w_in", "b_gate", "w_uq", "w_ukv")
GATHER_REST = ("w_a_proj", "w_b_proj", "w_o", "w_xq", "w_xkv", "w_xo", "w_gate", "w_up", "w_down")
SCATTER_FFN = ("w_gate", "w_up", "w_down")
SCATTER_MID = ("w_xq", "w_xkv", "w_xo", "w_a_proj", "w_b_proj", "w_o")
SCATTER_LAST = ("w_in", "b_gate", "w_uq", "w_ukv")
REPLICATED = (
    ("g_mix", 1024), ("g_q_lat", 256), ("g_kv_lat", 128), ("g_x", 1024),
    ("g_mem", 1024), ("g_ffn", 1024), ("g_final", 1024),
)
WEIGHT_ORDER = ("g_mix", "w_in", "b_gate", "g_q_lat", "w_uq", "g_kv_lat", "w_ukv", "w_a_proj",
                "w_b_proj", "w_o", "g_x", "g_mem", "w_xq", "w_xkv", "w_xo", "g_ffn", "w_gate",
                "w_up", "w_down", "g_final")


def _round_up(n, m):
    return -(-n // m) * m


def _cparams(vmem_mib=None, **kw):
    if vmem_mib is not None:
        kw["vmem_limit_bytes"] = vmem_mib * MIB
    return pltpu.CompilerParams(**kw)


def _dot(a, b):
    return jnp.dot(a, b, preferred_element_type=F32)


def _dot_nt(a, b):
    return lax.dot_general(a, b, (((1,), (1,)), ((), ())), preferred_element_type=F32)


def _dot_tn(a, b):
    return lax.dot_general(a, b, (((0,), (0,)), ((), ())), preferred_element_type=F32)


def _rms(x, g):
    r = lax.rsqrt(jnp.mean(x * x, axis=-1, keepdims=True) + EPS)
    xh = x * r
    return xh * g, xh, r


def _rms_bwd(dy, xh, r, g):
    u = dy * g
    dx = r * (u - xh * jnp.mean(u * xh, axis=-1, keepdims=True))
    return dx, dy * xh


def _sigmoid(z):
    return 1.0 / (1.0 + jnp.exp(-z))


def _acc_rows(ref, val, first):
    s = jnp.sum(val, axis=0, keepdims=True)

    @pl.when(first)
    def _():
        ref[...] = s

    @pl.when(jnp.logical_not(first))
    def _():
        ref[...] += s


def _acc(ref, val, first):
    @pl.when(first)
    def _():
        ref[...] = val

    @pl.when(jnp.logical_not(first))
    def _():
        ref[...] += val


def _peer(k):
    x, y, c = lax.axis_index("x"), lax.axis_index("y"), lax.axis_index("c")
    px = 1 - x if (k >> 2) & 1 else x
    py = 1 - y if (k >> 1) & 1 else y
    pc = 1 - c if k & 1 else c
    return (px, py, pc), 4 * px + 2 * py + pc


N_PEERS = N_DEV - 1
HBM_SPEC = pl.BlockSpec(memory_space=pltpu.HBM)
SEM_SPEC = pl.BlockSpec(memory_space=pltpu.SEMAPHORE)


def _land_shape(gather, src):
    return (N_DEV,) + src.shape if gather else src.shape


def _remote_copy(gather, srcs, lands, a, k, send_sems, recv_sems, arrival):
    peer, pid = _peer(k)
    _, me = _peer(0)
    return pltpu.make_async_remote_copy(
        src_ref=srcs[a] if gather else srcs[a].at[pid],
        dst_ref=lands[a].at[pid if arrival else me],
        send_sem=send_sems.at[a * N_PEERS + k - 1], recv_sem=recv_sems.at[a * N_PEERS + k - 1],
        device_id=peer, device_id_type=pl.DeviceIdType.MESH)


def _exchange(gather, srcs, name):
    n = len(srcs)

    def body(*refs):
        src, land = refs[:n], refs[n:2 * n]
        send_sems, recv_sems, local_sems = refs[2 * n:]
        _, me = _peer(0)
        mine = [pltpu.make_async_copy(src[a] if gather else src[a].at[me], land[a].at[me], local_sems.at[a])
                for a in range(n)]
        for cp in mine:
            cp.start()
        for a in range(n):
            for k in range(1, N_DEV):
                _remote_copy(gather, src, land, a, k, send_sems, recv_sems, False).start()
        for a in range(n):
            for k in range(1, N_DEV):
                _remote_copy(gather, src, land, a, k, send_sems, recv_sems, True).wait_recv()
        for a in range(n):
            for k in range(1, N_DEV):
                _remote_copy(gather, src, land, a, k, send_sems, recv_sems, False).wait_send()
        for cp in mine:
            cp.wait()

    return pl.pallas_call(
        body, name=name,
        out_shape=[jax.ShapeDtypeStruct(_land_shape(gather, s), s.dtype) for s in srcs],
        in_specs=[pl.BlockSpec(memory_space=pl.ANY)] * n,
        out_specs=[pl.BlockSpec(memory_space=pl.ANY)] * n,
        scratch_shapes=[pltpu.SemaphoreType.DMA((n * N_PEERS,)), pltpu.SemaphoreType.DMA((n * N_PEERS,)),
                        pltpu.SemaphoreType.DMA((n,))],
    )(*srcs)


def _exchange_start(gather, srcs, after, name):
    n = len(srcs)

    def body(*refs):
        src, land = refs[:n], refs[n:2 * n]
        send_sems, recv_sems = refs[2 * n + 1], refs[2 * n + 2]
        token = refs[-1]
        for a in range(n):
            for k in range(1, N_DEV):
                _remote_copy(gather, src, land, a, k, send_sems, recv_sems, False).start()
        token[...] = jnp.zeros_like(token)

    lands = [lax.empty(_land_shape(gather, s), s.dtype) for s in srcs]
    hbm = lambda v: pltpu.with_memory_space_constraint(v, pltpu.HBM)
    outs = pl.pallas_call(
        body, name=name,
        out_shape=(pltpu.SemaphoreType.DMA((n * N_PEERS,)), pltpu.SemaphoreType.DMA((n * N_PEERS,)),
                   *[pltpu.HBM(v.shape, v.dtype) for v in srcs + lands],
                   jax.ShapeDtypeStruct((8, LANES), F32)),
        in_specs=[HBM_SPEC] * (2 * n) + [pl.BlockSpec(memory_space=pl.ANY)],
        out_specs=(SEM_SPEC, SEM_SPEC, *[HBM_SPEC] * (2 * n), pl.BlockSpec(memory_space=pltpu.VMEM)),
        input_output_aliases={i: 2 + i for i in range(2 * n)},
        compiler_params=pltpu.CompilerParams(has_side_effects=pltpu.SideEffectType.DATAFLOW_SIDE_EFFECTING),
    )(*[hbm(v) for v in srcs + lands], after)
    return (outs[0], outs[1], list(outs[2:2 + n]), list(outs[2 + n:2 + 2 * n])), outs[-1]


def _exchange_wait(gather, handle, after, name):
    send_sems, recv_sems, srcs, lands = handle
    n = len(srcs)

    def body(*refs):
        src, land = refs[:n], refs[n:2 * n]
        ssem, rsem = refs[2 * n], refs[2 * n + 1]
        for a in range(n):
            for k in range(1, N_DEV):
                cp = _remote_copy(gather, src, land, a, k, ssem, rsem, True)
                cp.wait_send()
                cp.wait_recv()

    outs = pl.pallas_call(
        body, name=name,
        out_shape=tuple(pltpu.HBM(v.shape, v.dtype) for v in srcs + lands),
        in_specs=[HBM_SPEC] * (2 * n) + [SEM_SPEC, SEM_SPEC, pl.BlockSpec(memory_space=pl.ANY)],
        out_specs=tuple([HBM_SPEC] * (2 * n)),
        input_output_aliases={i: i for i in range(2 * n)},
        compiler_params=pltpu.CompilerParams(has_side_effects=pltpu.SideEffectType.DATAFLOW_SIDE_EFFECTING),
    )(*srcs, *lands, send_sems, recv_sems, after)
    return list(outs[n:])


def _tn_matmul(a, b, name, tka=512, tn=1024, ts=512):
    s_len, ka = a.shape
    n = b.shape[1]
    tka, tn, ts = min(tka, ka), min(tn, n), min(ts, s_len)
    assert ka % tka == 0 and n % tn == 0 and s_len % ts == 0

    def body(a_ref, b_ref, o_ref):
        _acc(o_ref, _dot_tn(a_ref[...], b_ref[...]), pl.program_id(2) == 0)

    return pl.pallas_call(
        body, name=name, grid=(ka // tka, n // tn, s_len // ts),
        in_specs=[pl.BlockSpec((ts, tka), lambda i, j, s: (s, i)),
                  pl.BlockSpec((ts, tn), lambda i, j, s: (s, j))],
        out_specs=pl.BlockSpec((tka, tn), lambda i, j, s: (i, j)),
        out_shape=jax.ShapeDtypeStruct((ka, n), F32),
        compiler_params=_cparams(dimension_semantics=("parallel", "parallel", "arbitrary")),
    )(a, b)


def _row_block(s_len):
    return min(s_len, 512)


def _in_proj(x, g, w):
    s_len = x.shape[0]
    tm, tn = _row_block(s_len), 512

    def body(x_ref, g_ref, w_ref, h_ref, lat_ref, sb_ref, gate_ref, h_scr):
        j = pl.program_id(1)

        @pl.when(j == 0)
        def _():
            h, _, _ = _rms(x_ref[...], g_ref[...])
            hb = h.astype(BF16)
            h_scr[...] = hb
            h_ref[...] = hb

        p = _dot(h_scr[...], w_ref[...])

        @pl.when(j == 0)
        def _():
            lat_ref[...] = p

        @pl.when(j == 1)
        def _():
            sb_ref[...] = (p * SB_Q_FOLD).astype(BF16)

        @pl.when((j == 2) | (j == 3))
        def _():
            sb_ref[...] = p.astype(BF16)

        @pl.when(j >= 4)
        def _():
            gate_ref[...] = p

    return pl.pallas_call(
        body, name="in_proj", grid=(s_len // tm, D_IN_PAD // tn),
        in_specs=[pl.BlockSpec((tm, D_MODEL), lambda i, j: (i, 0)),
                  pl.BlockSpec((1, D_MODEL), lambda i, j: (0, 0)),
                  pl.BlockSpec((D_MODEL, tn), lambda i, j: (0, j))],
        out_specs=[pl.BlockSpec((tm, D_MODEL), lambda i, j: (i, 0)),
                   pl.BlockSpec((tm, tn), lambda i, j: (i, 0)),
                   pl.BlockSpec((tm, tn), lambda i, j: (i, jnp.clip(j - 1, 0, 2))),
                   pl.BlockSpec((tm, tn), lambda i, j: (i, jnp.clip(j - 4, 0, 3)))],
        out_shape=[jax.ShapeDtypeStruct((s_len, D_MODEL), BF16),
                   jax.ShapeDtypeStruct((s_len, 512), F32),
                   jax.ShapeDtypeStruct((s_len, 3 * SB_WIDTH), BF16),
                   jax.ShapeDtypeStruct((s_len, 2 * D_MODEL), F32)],
        scratch_shapes=[pltpu.VMEM((tm, D_MODEL), BF16)],
        compiler_params=_cparams(dimension_semantics=("parallel", "arbitrary")),
    )(x, g, w)


def _rope_rot(blk, lane):
    return jnp.where(lane < 80, -pltpu.roll(blk, 112, 1), pltpu.roll(blk, 16, 1))


def _rope_rot_t(blk, lane):
    return jnp.where(lane < 80, pltpu.roll(blk, 112, 1), -pltpu.roll(blk, 16, 1))


def _mla_prep(lat, g_q, g_kv, w_uq, w_uk, w_uv, cosf, sinf):
    s_len = lat.shape[0]
    tm = _row_block(s_len)

    def body(lat_ref, gq_ref, gkv_ref, wuq_ref, wuk_ref, wuv_ref, cos_ref, sin_ref,
             q_ref, k_ref, v_ref, ql_ref, kvl_ref):
        lane = lax.broadcasted_iota(jnp.int32, (tm, LANES), 1)
        cosv, sinv = cos_ref[...], sin_ref[...]
        ql, _, _ = _rms(lat_ref[:, 0:256], gq_ref[...])
        kvl, _, _ = _rms(lat_ref[:, 256:384], gkv_ref[...])
        qlb, kvlb = ql.astype(BF16), kvl.astype(BF16)
        ql_ref[...] = qlb
        kvl_ref[...] = kvlb
        q = _dot(qlb, wuq_ref[...])
        kn = _dot(kvlb, wuk_ref[...])
        v_ref[...] = _dot(kvlb, wuv_ref[...]).astype(BF16)
        kr = pltpu.roll(lat_ref[:, K_R_OFF:K_R_OFF + LANES], 64, 1)
        kr = kr * cosv + _rope_rot(kr, lane) * sinv
        for h in range(MLA_HEADS):
            sl = slice(h * HEAD_PAD, (h + 1) * HEAD_PAD)
            blk = q[:, sl]
            q_ref[:, sl] = ((blk * cosv + _rope_rot(blk, lane) * sinv) * MLA_Q_FOLD).astype(BF16)
            k_ref[:, sl] = (kn[:, sl] + kr).astype(BF16)

    full = lambda shape: pl.BlockSpec(shape, lambda i: (0, 0))
    rowb = lambda n: pl.BlockSpec((tm, n), lambda i: (i, 0))
    return pl.pallas_call(
        body, name="mla_prep", grid=(s_len // tm,),
        in_specs=[rowb(512), full((1, 256)), full((1, 128)), full((256, 1024)), full((128, 1024)),
                  full((128, 512)), rowb(128), rowb(128)],
        out_specs=[rowb(1024), rowb(1024), rowb(512), rowb(256), rowb(128)],
        out_shape=[jax.ShapeDtypeStruct((s_len, 1024), BF16), jax.ShapeDtypeStruct((s_len, 1024), BF16),
                   jax.ShapeDtypeStruct((s_len, 512), BF16), jax.ShapeDtypeStruct((s_len, 256), BF16),
                   jax.ShapeDtypeStruct((s_len, 128), BF16)],
        compiler_params=_cparams(dimension_semantics=("parallel",)),
    )(lat, g_q, g_kv, w_uq, w_uk, w_uv, cosf, sinf)


ATTN_TQ = 1024
ATTN_TH = 512
ATTN_TK = 256
MLA_TK = 512


def _attn_blocks(s_len, tk=ATTN_TK):
    tq, th, tk = min(s_len, ATTN_TQ), min(s_len, ATTN_TH), min(s_len, tk)
    return tq, th, tk, tq // tk


def _chains(tq, th):
    return [(hh, r0) for hh in range(2) for r0 in range(0, tq, th)]


def _diag_mask(th, tk, r0, sub, strict):
    lo, hi = sub * tk, (sub + 1) * tk - 1
    last, first = r0 + th - 1, r0
    if (lo >= last) if strict else (lo > last):
        return "none"
    if (hi < first) if strict else (hi <= first):
        return "all"
    row = lax.broadcasted_iota(jnp.int32, (th, tk), 0) + r0
    col = lax.broadcasted_iota(jnp.int32, (th, tk), 1) + lo
    return col < row if strict else col <= row


def _mla_fwd(q, k, v):
    s_len = q.shape[0]
    tq, th, tk, nsub = _attn_blocks(s_len, MLA_TK)
    chains = _chains(tq, th)
    nh = tq // th

    def body(q_ref, k_ref, v_ref, o_ref, lse_ref):
        i = pl.program_id(1)
        lane = lax.broadcasted_iota(jnp.int32, (th, LANES), 1)
        hsl = [slice(hh * HEAD_PAD, (hh + 1) * HEAD_PAD) for hh in range(2)]

        def step(kb, carry, sub):
            rows = pl.ds(pl.multiple_of(kb * tk, tk), tk)
            vblk = v_ref[rows, :]
            masks = ["all" if sub is None else _diag_mask(th, tk, r0, sub, strict=False) for _, r0 in chains]
            live = [n for n, m in enumerate(masks) if not (isinstance(m, str) and m == "none")]
            s = {n: _dot_nt(q_ref[chains[n][1]:chains[n][1] + th, hsl[chains[n][0]]], k_ref[rows, hsl[chains[n][0]]])
                 for n in live}
            new = list(carry)
            pb, alpha = {}, {}
            for n in live:
                m, l, _ = carry[n]
                sn = s[n]
                if not isinstance(masks[n], str):
                    sn = jnp.where(masks[n], sn, NEG_BIG)
                m_new = jnp.maximum(m, jnp.max(sn, axis=-1, keepdims=True))
                alpha[n] = jnp.exp2(m - m_new)
                p = jnp.exp2(sn - m_new)
                pb[n] = p.astype(BF16)
                new[n] = (m_new, alpha[n] * l + jnp.sum(p, axis=-1, keepdims=True), None)
            pv = {n: _dot(pb[n], vblk) for n in live}
            for n in live:
                new[n] = (new[n][0], new[n][1], alpha[n] * carry[n][2] + pv[n])
            return tuple(new)

        init = (jnp.full((th, 1), NEG_BIG, F32), jnp.zeros((th, 1), F32), jnp.zeros((th, LANES), F32))
        carry = lax.fori_loop(0, i * nsub, lambda kb, cy: step(kb, cy, None), (init,) * len(chains))
        for sub in range(nsub):
            carry = step(i * nsub + sub, carry, sub)
        for c in range(nh):
            (m0, l0, a0), (m1, l1, a1) = carry[c], carry[nh + c]
            rs = slice(c * th, (c + 1) * th)
            o_ref[rs, :] = jnp.where(lane < 64, a0 / l0, a1 / l1).astype(BF16)
            lse_ref[rs, :] = jnp.where(lane < 64, m0 + jnp.log2(l0), m1 + jnp.log2(l1))

    return pl.pallas_call(
        body, name="mla_fwd", grid=(4, s_len // tq),
        in_specs=[pl.BlockSpec((tq, 2 * HEAD_PAD), lambda p, i: (i, p)),
                  pl.BlockSpec((s_len, 2 * HEAD_PAD), lambda p, i: (0, p)),
                  pl.BlockSpec((s_len, LANES), lambda p, i: (0, p))],
        out_specs=[pl.BlockSpec((tq, LANES), lambda p, i: (i, p)),
                   pl.BlockSpec((None, tq, LANES), lambda p, i: (p, i, 0))],
        out_shape=[jax.ShapeDtypeStruct((s_len, 512), BF16), jax.ShapeDtypeStruct((4, s_len, LANES), F32)],
        compiler_params=_cparams(40, dimension_semantics=("parallel", "arbitrary")),
    )(q, k, v)


def _mla_bwd(q, k, v, o, do, lse):
    s_len = q.shape[0]
    tq, th, tk, nsub = _attn_blocks(s_len, MLA_TK)
    chains = _chains(tq, th)

    def body(q_ref, k_ref, v_ref, o_ref, do_ref, lse_ref, dq_ref, dk_ref, dv_ref):
        i = pl.program_id(1)
        lane = lax.broadcasted_iota(jnp.int32, (th, LANES), 1)

        @pl.when(i == 0)
        def _():
            dk_ref[...] = jnp.zeros_like(dk_ref)
            dv_ref[...] = jnp.zeros_like(dv_ref)

        hsl = [slice(hh * HEAD_PAD, (hh + 1) * HEAD_PAD) for hh in range(2)]
        qs, dos, deltas, lses = [], [], [], []
        for hh, r0 in chains:
            rs = slice(r0, r0 + th)
            qs.append(q_ref[rs, hsl[hh]])
            doh = jnp.where((lane // 64) == hh, do_ref[rs, :], jnp.zeros((), BF16))
            dos.append(doh)
            deltas.append(jnp.sum(doh.astype(F32) * o_ref[rs, :].astype(F32), axis=-1, keepdims=True))
            lses.append(lse_ref[rs, 64 * hh:64 * hh + 1])

        def step(kb, dqs, sub):
            rows = pl.ds(pl.multiple_of(kb * tk, tk), tk)
            vblk = v_ref[rows, :]
            new, p_all, do_all = [], [], []
            ds_h, q_h = [[], []], [[], []]
            for c, (hh, r0) in enumerate(chains):
                mask = "all" if sub is None else _diag_mask(th, tk, r0, sub, strict=False)
                if isinstance(mask, str) and mask == "none":
                    new.append(dqs[c])
                    continue
                kblk = k_ref[rows, hsl[hh]]
                s = _dot_nt(qs[c], kblk)
                if not isinstance(mask, str):
                    s = jnp.where(mask, s, NEG_BIG)
                p = jnp.exp2(s - lses[c])
                dp = _dot_nt(dos[c], vblk)
                ds = (p * (dp - deltas[c]) * MLA_SCALE).astype(BF16)
                p_all.append(p.astype(BF16))
                do_all.append(dos[c])
                ds_h[hh].append(ds)
                q_h[hh].append(qs[c])
                new.append(dqs[c] + _dot(ds, kblk))
            dv_ref[rows, :] += _dot_tn(jnp.concatenate(p_all, axis=0), jnp.concatenate(do_all, axis=0))
            for hh in range(2):
                dk_ref[rows, hsl[hh]] += _dot_tn(jnp.concatenate(ds_h[hh], axis=0),
                                                 jnp.concatenate(q_h[hh], axis=0))
            return tuple(new)

        zero = jnp.zeros((th, LANES), F32)
        dqs = lax.fori_loop(0, i * nsub, lambda kb, cy: step(kb, cy, None), (zero,) * len(chains))
        for sub in range(nsub):
            dqs = step(i * nsub + sub, dqs, sub)
        for c, (hh, r0) in enumerate(chains):
            dq_ref[r0:r0 + th, hsl[hh]] = dqs[c]

    return pl.pallas_call(
        body, name="mla_bwd", grid=(4, s_len // tq),
        in_specs=[pl.BlockSpec((tq, 2 * HEAD_PAD), lambda p, i: (i, p)),
                  pl.BlockSpec((s_len, 2 * HEAD_PAD), lambda p, i: (0, p)),
                  pl.BlockSpec((s_len, LANES), lambda p, i: (0, p)),
                  pl.BlockSpec((tq, LANES), lambda p, i: (i, p)),
                  pl.BlockSpec((tq, LANES), lambda p, i: (i, p)),
                  pl.BlockSpec((None, tq, LANES), lambda p, i: (p, i, 0))],
        out_specs=[pl.BlockSpec((tq, 2 * HEAD_PAD), lambda p, i: (i, p)),
                   pl.BlockSpec((s_len, 2 * HEAD_PAD), lambda p, i: (0, p)),
                   pl.BlockSpec((s_len, LANES), lambda p, i: (0, p))],
        out_shape=[jax.ShapeDtypeStruct((s_len, 1024), F32), jax.ShapeDtypeStruct((s_len, 1024), F32),
                   jax.ShapeDtypeStruct((s_len, 512), F32)],
        compiler_params=_cparams(56, dimension_semantics=("arbitrary", "arbitrary")),
    )(q, k, v, o, do, lse)


def _log_sigmoids(z2):
    sp = jnp.log2(1.0 + jnp.exp2(-jnp.abs(z2)))
    lb = jnp.minimum(z2, 0.0) - sp
    return lb, lb - z2


def _split_dot(x, w, parts, nt=False):
    dot = _dot_nt if nt else _dot
    out = None
    for _ in range(parts):
        xb = x.astype(BF16)
        t = dot(xb, w)
        out = t if out is None else out + t
        x = x - xb.astype(F32)
    return out


def _sb_fwd(sb):
    s_len = sb.shape[0]
    tq, th, tk, nsub = _attn_blocks(s_len)
    chains = _chains(tq, th)
    nh = tq // th
    assert s_len // tk <= 64

    def body(q_ref, k_ref, v_ref, o_ref, r_ref):
        i = pl.program_id(1)
        lane = lax.broadcasted_iota(jnp.int32, (th, LANES), 1)
        upper = (lax.broadcasted_iota(jnp.int32, (tk, tk), 0)
                 > lax.broadcasted_iota(jnp.int32, (tk, tk), 1)).astype(BF16)
        qs = [jnp.where((lane // 64) == hh, q_ref[r0:r0 + th, :], jnp.zeros((), BF16)) for hh, r0 in chains]

        def step(kb, carry, sub):
            rows = pl.ds(pl.multiple_of(kb * tk, tk), tk)
            kblk, vblk = k_ref[rows, :], v_ref[rows, :]
            masks = ["all" if sub is None else _diag_mask(th, tk, r0, sub, strict=True) for _, r0 in chains]
            live = [n for n, m in enumerate(masks) if not (isinstance(m, str) and m == "none")]
            masked = {n: not isinstance(masks[n], str) for n in live}
            z = {n: _dot_nt(qs[n], kblk) for n in live}
            lb, lom = {}, {}
            for n in live:
                lb[n], lom[n] = _log_sigmoids(z[n])
                if masked[n]:
                    lom[n] = jnp.where(masks[n], lom[n], 0.0)
            suf = {n: _split_dot(lom[n], upper, 2) for n in live}
            a = {}
            for n in live:
                a[n] = jnp.exp2(lb[n] + suf[n] + carry[n][0])
                if masked[n]:
                    a[n] = jnp.where(masks[n], a[n], 0.0)
            pv = {n: _dot(a[n].astype(BF16), vblk) for n in live}
            new = list(carry)
            for n in live:
                c, acc, r = carry[n]
                rs = suf[n][:, 0:1] + lom[n][:, 0:1]
                new[n] = (c + rs, acc + pv[n], jnp.where(lane == 64 * chains[n][0] + kb, rs, r))
            return tuple(new)

        init = (jnp.zeros((th, 1), F32), jnp.zeros((th, LANES), F32), jnp.zeros((th, LANES), F32))
        carry = (init,) * len(chains)
        for sub in reversed(range(nsub)):
            carry = step(i * nsub + sub, carry, sub)
        carry = lax.fori_loop(0, i * nsub, lambda t, cy: step(i * nsub - 1 - t, cy, None), carry)
        for n in range(nh):
            rs = slice(n * th, (n + 1) * th)
            o_ref[rs, :] = jnp.where(lane < 64, carry[n][1], carry[nh + n][1]).astype(BF16)
            r_ref[rs, :] = jnp.where(lane < 64, carry[n][2], carry[nh + n][2])

    return pl.pallas_call(
        body, name="sb_fwd", grid=(4, s_len // tq),
        in_specs=[pl.BlockSpec((tq, LANES), lambda p, i: (i, p)),
                  pl.BlockSpec((s_len, LANES), lambda p, i: (0, 4 + p)),
                  pl.BlockSpec((s_len, LANES), lambda p, i: (0, 8 + p))],
        out_specs=[pl.BlockSpec((tq, LANES), lambda p, i: (i, p)),
                   pl.BlockSpec((None, tq, LANES), lambda p, i: (p, i, 0))],
        out_shape=[jax.ShapeDtypeStruct((s_len, 512), BF16), jax.ShapeDtypeStruct((4, s_len, LANES), F32)],
        compiler_params=_cparams(40, dimension_semantics=("parallel", "arbitrary")),
    )(sb, sb, sb)


def _sb_bwd(sb, do, r):
    s_len = sb.shape[0]
    tq, th, tk, nsub = _attn_blocks(s_len)
    chains = _chains(tq, th)
    nh = tq // th

    def body(q_ref, k_ref, v_ref, do_ref, r_ref, dq_ref, dk_ref, dv_ref):
        i = pl.program_id(1)
        lane = lax.broadcasted_iota(jnp.int32, (th, LANES), 1)
        upper = (lax.broadcasted_iota(jnp.int32, (tk, tk), 0)
                 > lax.broadcasted_iota(jnp.int32, (tk, tk), 1)).astype(BF16)
        tri = (lax.broadcasted_iota(jnp.int32, (LANES, LANES), 0)
               > lax.broadcasted_iota(jnp.int32, (LANES, LANES), 1)).astype(BF16)

        @pl.when(i == 0)
        def _():
            dk_ref[...] = jnp.zeros_like(dk_ref)
            dv_ref[...] = jnp.zeros_like(dv_ref)

        qs, dos, rights = [], [], []
        for hh, r0 in chains:
            rs = slice(r0, r0 + th)
            hm = (lane // 64) == hh
            qs.append(jnp.where(hm, q_ref[rs, :], jnp.zeros((), BF16)))
            dos.append(jnp.where(hm, do_ref[rs, :], jnp.zeros((), BF16)))
            rights.append(_split_dot(jnp.where(hm, r_ref[rs, :], 0.0), tri, 3))

        def step(kb, carry, sub):
            rows = pl.ds(pl.multiple_of(kb * tk, tk), tk)
            kblk, vblk = k_ref[rows, :], v_ref[rows, :]
            new, a_all, do_all, dz_all, q_all = [], [], [], [], []
            for n, ((hh, r0), (pre, dq)) in enumerate(zip(chains, carry)):
                mask = "all" if sub is None else _diag_mask(th, tk, r0, sub, strict=True)
                if isinstance(mask, str) and mask == "none":
                    new.append((pre, dq))
                    continue
                c = jnp.sum(jnp.where(lane == 64 * hh + kb, rights[n], 0.0), axis=-1, keepdims=True)
                z = _dot_nt(qs[n], kblk)
                lb, lom = _log_sigmoids(z)
                if not isinstance(mask, str):
                    lom = jnp.where(mask, lom, 0.0)
                suf = _split_dot(lom, upper, 2)
                a = jnp.exp2(lb + suf + c)
                if not isinstance(mask, str):
                    a = jnp.where(mask, a, 0.0)
                g = a * _dot_nt(dos[n], vblk)
                left = _split_dot(g, upper, 1, nt=True) + pre
                sig = jnp.exp2(lb)
                dz = g * (1.0 - sig) - sig * left
                if not isinstance(mask, str):
                    dz = jnp.where(mask, dz, 0.0)
                dzb = dz.astype(BF16)
                a_all.append(a.astype(BF16))
                do_all.append(dos[n])
                dz_all.append(dzb)
                q_all.append(qs[n])
                new.append((left[:, tk - 1:tk] + g[:, tk - 1:tk], dq + _dot(dzb, kblk)))
            dv_ref[rows, :] += _dot_tn(jnp.concatenate(a_all, axis=0), jnp.concatenate(do_all, axis=0))
            dk_ref[rows, :] += _dot_tn(jnp.concatenate(dz_all, axis=0), jnp.concatenate(q_all, axis=0))
            return tuple(new)

        init = (jnp.zeros((th, 1), F32), jnp.zeros((th, LANES), F32))
        carry = lax.fori_loop(0, i * nsub, lambda kb, cy: step(kb, cy, None), (init,) * len(chains))
        for sub in range(nsub):
            carry = step(i * nsub + sub, carry, sub)
        for n in range(nh):
            dq_ref[n * th:(n + 1) * th, :] = jnp.where(lane < 64, carry[n][1], carry[nh + n][1]) * SB_SCALE

    return pl.pallas_call(
        body, name="sb_bwd", grid=(4, s_len // tq),
        in_specs=[pl.BlockSpec((tq, LANES), lambda p, i: (i, p)),
                  pl.BlockSpec((s_len, LANES), lambda p, i: (0, 4 + p)),
                  pl.BlockSpec((s_len, LANES), lambda p, i: (0, 8 + p)),
                  pl.BlockSpec((tq, LANES), lambda p, i: (i, p)),
                  pl.BlockSpec((None, tq, LANES), lambda p, i: (p, i, 0))],
        out_specs=[pl.BlockSpec((tq, LANES), lambda p, i: (i, p)),
                   pl.BlockSpec((s_len, LANES), lambda p, i: (0, p)),
                   pl.BlockSpec((s_len, LANES), lambda p, i: (0, p))],
        out_shape=[jax.ShapeDtypeStruct((s_len, 512), F32)] * 3,
        compiler_params=_cparams(48, dimension_semantics=("arbitrary", "arbitrary")),
    )(sb, sb, sb, do, r)


def _merge_fwd(x, oa, ob, gates, bg, wa, wb, wo):
    s_len = x.shape[0]
    tm = _row_block(s_len)

    def body(x_ref, oa_ref, ob_ref, g_ref, bg_ref, wa_ref, wb_ref, wo_ref, y_ref):
        pa = _dot(oa_ref[...], wa_ref[...])
        pb = _dot(ob_ref[...], wb_ref[...])
        merged = (_sigmoid(g_ref[:, 0:D_MODEL] + bg_ref[0:1, :]) * pa
                  + _sigmoid(g_ref[:, D_MODEL:2 * D_MODEL] + bg_ref[1:2, :]) * pb)
        y_ref[...] = x_ref[...] + _dot(merged.astype(BF16), wo_ref[...])

    full = lambda shape: pl.BlockSpec(shape, lambda i: (0, 0))
    rowb = lambda n: pl.BlockSpec((tm, n), lambda i: (i, 0))
    return pl.pallas_call(
        body, name="merge_fwd", grid=(s_len // tm,),
        in_specs=[rowb(1024), rowb(512), rowb(512), rowb(2048), full((2, 1024)), full((512, 1024)),
                  full((512, 1024)), full((1024, 1024))],
        out_specs=rowb(1024),
        out_shape=jax.ShapeDtypeStruct((s_len, D_MODEL), F32),
        compiler_params=_cparams(48, dimension_semantics=("parallel",)),
    )(x, oa, ob, gates, bg, wa, wb, wo)


def _merge_bwd(dx1, oa, ob, gates, bg, wa, wb, wo):
    s_len = dx1.shape[0]
    tm = _row_block(s_len)

    def body(dx_ref, oa_ref, ob_ref, g_ref, bg_ref, wa_ref, wb_ref, wo_ref,
             doa_ref, dob_ref, dgate_ref, dpa_ref, dpb_ref, merged_ref, dxb_ref, dbg_ref):
        first = pl.program_id(0) == 0
        dxb = dx_ref[...].astype(BF16)
        dxb_ref[...] = dxb
        pa = _dot(oa_ref[...], wa_ref[...])
        pb = _dot(ob_ref[...], wb_ref[...])
        sa = _sigmoid(g_ref[:, 0:D_MODEL] + bg_ref[0:1, :])
        sbg = _sigmoid(g_ref[:, D_MODEL:2 * D_MODEL] + bg_ref[1:2, :])
        merged_ref[...] = (sa * pa + sbg * pb).astype(BF16)
        dm = _dot_nt(dxb, wo_ref[...])
        dpa = (dm * sa).astype(BF16)
        dpb = (dm * sbg).astype(BF16)
        dpa_ref[...] = dpa
        dpb_ref[...] = dpb
        dga = dm * pa * sa * (1.0 - sa)
        dgb = dm * pb * sbg * (1.0 - sbg)
        dgate_ref[:, 0:D_MODEL] = dga.astype(BF16)
        dgate_ref[:, D_MODEL:2 * D_MODEL] = dgb.astype(BF16)
        _acc_rows(dbg_ref.at[0:1, :], dga, first)
        _acc_rows(dbg_ref.at[1:2, :], dgb, first)
        doa_ref[...] = _dot_nt(dpa, wa_ref[...]).astype(BF16)
        dob_ref[...] = _dot_nt(dpb, wb_ref[...]).astype(BF16)

    full = lambda shape: pl.BlockSpec(shape, lambda i: (0, 0))
    rowb = lambda n: pl.BlockSpec((tm, n), lambda i: (i, 0))
    sds = lambda n, dt: jax.ShapeDtypeStruct((s_len, n), dt)
    return pl.pallas_call(
        body, name="merge_bwd", grid=(s_len // tm,),
        in_specs=[rowb(1024), rowb(512), rowb(512), rowb(2048), full((2, 1024)), full((512, 1024)),
                  full((512, 1024)), full((1024, 1024))],
        out_specs=[rowb(512), rowb(512), rowb(2048), rowb(1024), rowb(1024), rowb(1024), rowb(1024),
                   full((2, 1024))],
        out_shape=[sds(512, BF16), sds(512, BF16), sds(2048, BF16), sds(1024, BF16), sds(1024, BF16),
                   sds(1024, BF16), sds(1024, BF16), jax.ShapeDtypeStruct((2, 1024), F32)],
        compiler_params=_cparams(48, dimension_semantics=("arbitrary",)),
    )(dx1, oa, ob, gates, bg, wa, wb, wo)


def _mem_kv(mem, g, w):
    m_len = mem.shape[0]

    def body(mem_ref, g_ref, w_ref, mn_ref, kv_ref):
        mn, _, _ = _rms(mem_ref[...], g_ref[...])
        mnb = mn.astype(BF16)
        mn_ref[...] = mnb
        kv_ref[...] = _dot(mnb, w_ref[...]).astype(BF16)

    return pl.pallas_call(
        body, name="mem_kv",
        out_shape=[jax.ShapeDtypeStruct((m_len, D_MODEL), BF16), jax.ShapeDtypeStruct((m_len, 1024), BF16)],
    )(mem, g, w)


def _mem_bwd(mem, g, w, mn, dkv):
    def body(mem_ref, g_ref, w_ref, mn_ref, dkv_ref, dw_ref, dg_ref):
        dkvb = dkv_ref[...].astype(BF16)
        dw_ref[...] = _dot_tn(mn_ref[...], dkvb)
        dmn = _dot_nt(dkvb, w_ref[...])
        _, xh, _ = _rms(mem_ref[...], g_ref[...])
        dg_ref[...] = jnp.sum(dmn * xh, axis=0, keepdims=True)

    return pl.pallas_call(
        body, name="mem_bwd",
        out_shape=[jax.ShapeDtypeStruct((D_MODEL, 1024), F32), jax.ShapeDtypeStruct((1, D_MODEL), F32)],
    )(mem, g, w, mn, dkv)


def _xattn_heads(xqb, kv_ref, m_len):
    ps = []
    for h in range(X_HEADS):
        hs = slice(h * X_HEAD_DIM, (h + 1) * X_HEAD_DIM)
        s = _dot_nt(xqb[:, hs], kv_ref[:, hs]) * X_SCALE
        e = jnp.exp(s - jnp.max(s, axis=-1, keepdims=True))
        ps.append(e / jnp.sum(e, axis=-1, keepdims=True))
    return ps


def _xattn_fwd(x1, g, wxq, kv, wxo):
    s_len, m_len = x1.shape[0], kv.shape[0]
    tm = _row_block(s_len)

    def body(x_ref, g_ref, wq_ref, kv_ref, wo_ref, y_ref):
        hx, _, _ = _rms(x_ref[...], g_ref[...])
        xqb = _dot(hx.astype(BF16), wq_ref[...]).astype(BF16)
        ps = _xattn_heads(xqb, kv_ref, m_len)
        xo = jnp.concatenate(
            [_dot(ps[h].astype(BF16), kv_ref[:, 512 + h * X_HEAD_DIM:512 + (h + 1) * X_HEAD_DIM])
             for h in range(X_HEADS)], axis=-1)
        y_ref[...] = x_ref[...] + _dot(xo.astype(BF16), wo_ref[...])

    full = lambda shape: pl.BlockSpec(shape, lambda i: (0, 0))
    rowb = lambda n: pl.BlockSpec((tm, n), lambda i: (i, 0))
    return pl.pallas_call(
        body, name="xattn_fwd", grid=(s_len // tm,),
        in_specs=[rowb(1024), full((1, 1024)), full((1024, 512)), full((m_len, 1024)), full((512, 1024))],
        out_specs=rowb(1024),
        out_shape=jax.ShapeDtypeStruct((s_len, D_MODEL), F32),
        compiler_params=_cparams(48, dimension_semantics=("parallel",)),
    )(x1, g, wxq, kv, wxo)


def _xattn_bwd(x1, dx2, g, wxq, kv, wxo):
    s_len, m_len = x1.shape[0], kv.shape[0]
    tm = _row_block(s_len)

    def body(x_ref, dy_ref, g_ref, wq_ref, kv_ref, wo_ref, dx_ref, dwq_ref, dwo_ref, dkv_ref, dg_ref):
        first = pl.program_id(0) == 0
        gv = g_ref[...]
        hx, xh, r = _rms(x_ref[...], gv)
        hxb = hx.astype(BF16)
        xqb = _dot(hxb, wq_ref[...]).astype(BF16)
        ps = _xattn_heads(xqb, kv_ref, m_len)
        dy = dy_ref[...]
        dyb = dy.astype(BF16)
        dxo = _dot_nt(dyb, wo_ref[...])
        xos, dqs, dks, dvs = [], [], [], []
        for h in range(X_HEADS):
            hs = slice(h * X_HEAD_DIM, (h + 1) * X_HEAD_DIM)
            vs = slice(512 + h * X_HEAD_DIM, 512 + (h + 1) * X_HEAD_DIM)
            p = ps[h]
            pb = p.astype(BF16)
            dxoh = dxo[:, hs].astype(BF16)
            xos.append(_dot(pb, kv_ref[:, vs]))
            dp = _dot_nt(dxoh, kv_ref[:, vs])
            ds = (p * (dp - jnp.sum(dp * p, axis=-1, keepdims=True)) * X_SCALE).astype(BF16)
            dvs.append(_dot_tn(pb, dxoh))
            dks.append(_dot_tn(ds, xqb[:, hs]))
            dqs.append(_dot(ds, kv_ref[:, hs]))
        xob = jnp.concatenate(xos, axis=-1).astype(BF16)
        dxqb = jnp.concatenate(dqs, axis=-1).astype(BF16)
        _acc(dwo_ref, _dot_tn(xob, dyb), first)
        _acc(dwq_ref, _dot_tn(hxb, dxqb), first)
        _acc(dkv_ref, jnp.concatenate(dks + dvs, axis=-1), first)
        dhx = _dot_nt(dxqb, wq_ref[...])
        dx, dgr = _rms_bwd(dhx, xh, r, gv)
        dx_ref[...] = dy + dx
        _acc_rows(dg_ref, dgr, first)

    full = lambda shape: pl.BlockSpec(shape, lambda i: (0, 0))
    rowb = lambda n: pl.BlockSpec((tm, n), lambda i: (i, 0))
    return pl.pallas_call(
        body, name="xattn_bwd", grid=(s_len // tm,),
        in_specs=[rowb(1024), rowb(1024), full((1, 1024)), full((1024, 512)), full((m_len, 1024)),
                  full((512, 1024))],
        out_specs=[rowb(1024), full((1024, 512)), full((512, 1024)), full((m_len, 1024)), full((1, 1024))],
        out_shape=[jax.ShapeDtypeStruct((s_len, D_MODEL), F32), jax.ShapeDtypeStruct((1024, 512), F32),
                   jax.ShapeDtypeStruct((512, 1024), F32), jax.ShapeDtypeStruct((m_len, 1024), F32),
                   jax.ShapeDtypeStruct((1, D_MODEL), F32)],
        compiler_params=_cparams(48, dimension_semantics=("arbitrary",)),
    )(x1, dx2, g, wxq, kv, wxo)


FF_TILE = 1408
FF_TILE_BWD = 256


def _ffn_fwd(x2, g, wg, wu, wd):
    s_len = x2.shape[0]
    tm, tf = _row_block(s_len), FF_TILE

    def body(x_ref, g_ref, wg_ref, wu_ref, wd_ref, y_ref, h_scr):
        j = pl.program_id(1)

        @pl.when(j == 0)
        def _():
            hf, _, _ = _rms(x_ref[...], g_ref[...])
            h_scr[...] = hf.astype(BF16)
            y_ref[...] = x_ref[...]

        hb = h_scr[...]
        gt = _dot(hb, wg_ref[...])
        up = _dot(hb, wu_ref[...])
        act = gt * _sigmoid(gt) * up
        y_ref[...] += _dot(act.astype(BF16), wd_ref[...])

    return pl.pallas_call(
        body, name="ffn_fwd", grid=(s_len // tm, D_FF // tf),
        in_specs=[pl.BlockSpec((tm, D_MODEL), lambda i, j: (i, 0)),
                  pl.BlockSpec((1, D_MODEL), lambda i, j: (0, 0)),
                  pl.BlockSpec((D_MODEL, tf), lambda i, j: (0, j)),
                  pl.BlockSpec((D_MODEL, tf), lambda i, j: (0, j)),
                  pl.BlockSpec((tf, D_MODEL), lambda i, j: (j, 0))],
        out_specs=pl.BlockSpec((tm, D_MODEL), lambda i, j: (i, 0)),
        out_shape=jax.ShapeDtypeStruct((s_len, D_MODEL), F32),
        scratch_shapes=[pltpu.VMEM((tm, D_MODEL), BF16)],
        compiler_params=_cparams(48, dimension_semantics=("parallel", "arbitrary")),
    )(x2, g, wg, wu, wd)


def _ffn_bwd(x2, dx3, g, wg, wu, wd):
    s_len = x2.shape[0]
    tm, tf = _row_block(s_len), FF_TILE_BWD
    nf = D_FF // tf

    def body(x_ref, dy_ref, g_ref, wg_ref, wu_ref, wd_ref,
             dx_ref, h_ref, dgt_ref, dup_ref, act_ref, dg_ref, h_scr, dyb_scr, dh_scr):
        i, j = pl.program_id(0), pl.program_id(1)

        @pl.when(j == 0)
        def _():
            hf, _, _ = _rms(x_ref[...], g_ref[...])
            hb = hf.astype(BF16)
            h_scr[...] = hb
            h_ref[...] = hb
            dyb_scr[...] = dy_ref[...].astype(BF16)
            dh_scr[...] = jnp.zeros_like(dh_scr)

        hb = h_scr[...]
        gt = _dot(hb, wg_ref[...])
        up = _dot(hb, wu_ref[...])
        sg = _sigmoid(gt)
        silu = gt * sg
        dact = _dot_nt(dyb_scr[...], wd_ref[...])
        dgt = (dact * up * (sg * (1.0 + gt * (1.0 - sg)))).astype(BF16)
        dup = (dact * silu).astype(BF16)
        dgt_ref[...] = dgt
        dup_ref[...] = dup
        act_ref[...] = (silu * up).astype(BF16)
        dh_scr[...] += _dot_nt(dgt, wg_ref[...]) + _dot_nt(dup, wu_ref[...])

        @pl.when(j == nf - 1)
        def _():
            gv = g_ref[...]
            _, xh, r = _rms(x_ref[...], gv)
            dx, dgr = _rms_bwd(dh_scr[...], xh, r, gv)
            dx_ref[...] = dy_ref[...] + dx
            _acc_rows(dg_ref, dgr, i == 0)

    rowb = pl.BlockSpec((tm, D_MODEL), lambda i, j: (i, 0))
    ffb = pl.BlockSpec((tm, tf), lambda i, j: (i, j))
    return pl.pallas_call(
        body, name="ffn_bwd", grid=(s_len // tm, nf),
        in_specs=[rowb, rowb, pl.BlockSpec((1, D_MODEL), lambda i, j: (0, 0)),
                  pl.BlockSpec((D_MODEL, tf), lambda i, j: (0, j)),
                  pl.BlockSpec((D_MODEL, tf), lambda i, j: (0, j)),
                  pl.BlockSpec((tf, D_MODEL), lambda i, j: (j, 0))],
        out_specs=[rowb, rowb, ffb, ffb, ffb, pl.BlockSpec((1, D_MODEL), lambda i, j: (0, 0))],
        out_shape=[jax.ShapeDtypeStruct((s_len, D_MODEL), F32), jax.ShapeDtypeStruct((s_len, D_MODEL), BF16),
                   jax.ShapeDtypeStruct((s_len, D_FF), BF16), jax.ShapeDtypeStruct((s_len, D_FF), BF16),
                   jax.ShapeDtypeStruct((s_len, D_FF), BF16), jax.ShapeDtypeStruct((1, D_MODEL), F32)],
        scratch_shapes=[pltpu.VMEM((tm, D_MODEL), BF16), pltpu.VMEM((tm, D_MODEL), BF16),
                        pltpu.VMEM((tm, D_MODEL), F32)],
        compiler_params=_cparams(56, dimension_semantics=("arbitrary", "arbitrary")),
    )(x2, dx3, g, wg, wu, wd)


def _loss_head(x3, g, target):
    s_len = x3.shape[0]
    tm = _row_block(s_len)

    def body(x_ref, g_ref, t_ref, sse_ref, dx_ref, dxb_ref, dg_ref):
        first = pl.program_id(0) == 0
        gv = g_ref[...]
        y, xh, r = _rms(x_ref[...], gv)
        err = y - t_ref[...]
        _acc(sse_ref, jnp.broadcast_to(jnp.sum(err * err), (8, LANES)), first)
        dx, dgr = _rms_bwd(err * (1.0 / D_MODEL), xh, r, gv)
        dx_ref[...] = dx
        dxb_ref[...] = dx.astype(BF16)
        _acc_rows(dg_ref, dgr, first)

    rowb = pl.BlockSpec((tm, D_MODEL), lambda i: (i, 0))
    return pl.pallas_call(
        body, name="loss_head", grid=(s_len // tm,),
        in_specs=[rowb, pl.BlockSpec((1, D_MODEL), lambda i: (0, 0)), rowb],
        out_specs=[pl.BlockSpec((8, LANES), lambda i: (0, 0)), rowb, rowb,
                   pl.BlockSpec((1, D_MODEL), lambda i: (0, 0))],
        out_shape=[jax.ShapeDtypeStruct((8, LANES), F32), jax.ShapeDtypeStruct((s_len, D_MODEL), F32),
                   jax.ShapeDtypeStruct((s_len, D_MODEL), BF16), jax.ShapeDtypeStruct((1, D_MODEL), F32)],
        compiler_params=_cparams(dimension_semantics=("arbitrary",)),
    )(x3, g, target)


def _mla_prep_bwd(lat, g_q, g_kv, w_uq, w_uk, w_uv, cosf, sinf, dq, dk, dv):
    s_len = lat.shape[0]
    tm = _row_block(s_len)

    def body(lat_ref, gq_ref, gkv_ref, wuq_ref, wuk_ref, wuv_ref, cos_ref, sin_ref, dq_ref, dk_ref, dv_ref,
             dlat_ref, dqb_ref, dkb_ref, dvb_ref, dgq_ref, dgkv_ref):
        first = pl.program_id(0) == 0
        lane = lax.broadcasted_iota(jnp.int32, (tm, LANES), 1)
        cosv, sinv = cos_ref[...], sin_ref[...]
        gq, gkv = gq_ref[...], gkv_ref[...]
        _, qxh, qr = _rms(lat_ref[:, 0:256], gq)
        _, kxh, kr_ = _rms(lat_ref[:, 256:384], gkv)
        dkr = jnp.zeros((tm, LANES), F32)
        for h in range(MLA_HEADS):
            sl = slice(h * HEAD_PAD, (h + 1) * HEAD_PAD)
            blk = dq_ref[:, sl]
            dqb_ref[:, sl] = (blk * cosv + _rope_rot_t(blk, lane) * sinv).astype(BF16)
            kblk = dk_ref[:, sl] * (1.0 / MLA_Q_FOLD)
            dkb_ref[:, sl] = kblk.astype(BF16)
            dkr = dkr + kblk
        dvb = dv_ref[...].astype(BF16)
        dvb_ref[...] = dvb
        dkr = jnp.where((lane >= 64) & (lane < 96), dkr, 0.0)
        dkr = dkr * cosv + _rope_rot_t(dkr, lane) * sinv
        dql = _dot_nt(dqb_ref[...], wuq_ref[...])
        dkvl = _dot_nt(dkb_ref[...], wuk_ref[...]) + _dot_nt(dvb, wuv_ref[...])
        dcq, dgqr = _rms_bwd(dql, qxh, qr, gq)
        dckv, dgkvr = _rms_bwd(dkvl, kxh, kr_, gkv)
        dlat_ref[:, 0:256] = dcq
        dlat_ref[:, 256:384] = dckv
        dlat_ref[:, K_R_OFF:K_R_OFF + LANES] = pltpu.roll(dkr, 64, 1)
        _acc_rows(dgq_ref, dgqr, first)
        _acc_rows(dgkv_ref, dgkvr, first)

    full = lambda shape: pl.BlockSpec(shape, lambda i: (0, 0))
    rowb = lambda n: pl.BlockSpec((tm, n), lambda i: (i, 0))
    sds = lambda n, dt: jax.ShapeDtypeStruct((s_len, n), dt)
    return pl.pallas_call(
        body, name="mla_prep_bwd", grid=(s_len // tm,),
        in_specs=[rowb(512), full((1, 256)), full((1, 128)), full((256, 1024)), full((128, 1024)),
                  full((128, 512)), rowb(128), rowb(128), rowb(1024), rowb(1024), rowb(512)],
        out_specs=[rowb(512), rowb(1024), rowb(1024), rowb(512), full((1, 256)), full((1, 128))],
        out_shape=[sds(512, F32), sds(1024, BF16), sds(1024, BF16), sds(512, BF16),
                   jax.ShapeDtypeStruct((1, 256), F32), jax.ShapeDtypeStruct((1, 128), F32)],
        compiler_params=_cparams(48, dimension_semantics=("arbitrary",)),
    )(lat, g_q, g_kv, w_uq, w_uk, w_uv, cosf, sinf, dq, dk, dv)


def _in_proj_bwd(x, g, w, dx1, dlat, dsbq, dsbk, dsbv, dgates):
    s_len = x.shape[0]
    tm, tn = _row_block(s_len), 512
    nj = D_IN_PAD // tn

    def body(x_ref, g_ref, w_ref, dx1_ref, dlat_ref, dq_ref, dk_ref, dv_ref, dgate_ref,
             gx_ref, dproj_ref, dg_ref, dh_scr):
        i, j = pl.program_id(0), pl.program_id(1)

        @pl.when(j == 0)
        def _():
            dh_scr[...] = jnp.zeros_like(dh_scr)

        def chunk(val):
            vb = val.astype(BF16)
            dproj_ref[...] = vb
            dh_scr[...] += _dot_nt(vb, w_ref[...])

        for jj, ref in ((0, dlat_ref), (1, dq_ref), (3, dv_ref)):
            pl.when(j == jj)(functools.partial(lambda ref: chunk(ref[...]), ref))
        pl.when(j == 2)(lambda: chunk(dk_ref[...] * LN2))
        pl.when(j >= 4)(lambda: chunk(dgate_ref[...]))

        @pl.when(j == nj - 1)
        def _():
            gv = g_ref[...]
            _, xh, r = _rms(x_ref[...], gv)
            dx, dgr = _rms_bwd(dh_scr[...], xh, r, gv)
            gx_ref[...] = dx1_ref[...] + dx
            _acc_rows(dg_ref, dgr, i == 0)

    rowb = pl.BlockSpec((tm, D_MODEL), lambda i, j: (i, 0))
    colb = lambda f: pl.BlockSpec((tm, tn), f)
    return pl.pallas_call(
        body, name="in_proj_bwd", grid=(s_len // tm, nj),
        in_specs=[rowb, pl.BlockSpec((1, D_MODEL), lambda i, j: (0, 0)),
                  pl.BlockSpec((D_MODEL, tn), lambda i, j: (0, j)), rowb,
                  colb(lambda i, j: (i, 0)), colb(lambda i, j: (i, 0)), colb(lambda i, j: (i, 0)),
                  colb(lambda i, j: (i, 0)), colb(lambda i, j: (i, jnp.clip(j - 4, 0, 3)))],
        out_specs=[rowb, colb(lambda i, j: (i, j)), pl.BlockSpec((1, D_MODEL), lambda i, j: (0, 0))],
        out_shape=[jax.ShapeDtypeStruct((s_len, D_MODEL), F32), jax.ShapeDtypeStruct((s_len, D_IN_PAD), BF16),
                   jax.ShapeDtypeStruct((1, D_MODEL), F32)],
        scratch_shapes=[pltpu.VMEM((tm, D_MODEL), F32)],
        compiler_params=_cparams(48, dimension_semantics=("arbitrary", "arbitrary")),
    )(x, g, w, dx1, dlat, dsbq, dsbk, dsbv, dgates)


def _adamw(landed, w, m, v, name):
    r, c = w.shape
    lanes = _round_up(c, LANES)
    tb = r
    for cand in range(r, 0, -1):
        if r % cand == 0 and (cand % 8 == 0 or cand == r) and N_DEV * cand * lanes * 4 <= ADAM_BLOCK_BYTES:
            tb = cand
            break
    c1 = 1.0 - ADAM_B1 ** ADAM_STEP
    c2 = 1.0 - ADAM_B2 ** ADAM_STEP

    def body(l_ref, w_ref, m_ref, v_ref, g_ref, d_ref, nm_ref, nv_ref):
        g = l_ref[0]
        for k in range(1, N_DEV):
            g = g + l_ref[k]
        nm = ADAM_B1 * m_ref[...] + (1.0 - ADAM_B1) * g
        nv = ADAM_B2 * v_ref[...] + (1.0 - ADAM_B2) * (g * g)
        g_ref[...] = g
        nm_ref[...] = nm
        nv_ref[...] = nv
        d_ref[...] = -ADAM_LR * ((nm / c1) / (jnp.sqrt(nv / c2) + ADAM_EPS) + ADAM_WD * w_ref[...])

    blk = pl.BlockSpec((tb, c), lambda i: (i, 0))
    return pl.pallas_call(
        body, name=name, grid=(r // tb,),
        in_specs=[pl.BlockSpec((N_DEV, tb, c), lambda i: (0, i, 0)), blk, blk, blk],
        out_specs=[blk, blk, blk, blk],
        out_shape=[jax.ShapeDtypeStruct((r, c), F32)] * 4,
        compiler_params=_cparams(dimension_semantics=("parallel",)),
    )(landed, w, m, v)


def _shard_shape(shape, axis):
    return tuple(d // N_DEV if a == axis else d for a, d in enumerate(shape))


def _split_pieces(full, axis):
    r, c = full.shape
    if axis == 0:
        return full.reshape(N_DEV, r // N_DEV, c)
    return full.reshape(r, N_DEV, c // N_DEV).transpose(1, 0, 2)


def _join_shards(gathered, axis):
    _, r, c = gathered.shape
    if axis == 0:
        return gathered.reshape(N_DEV * r, c)
    return gathered.transpose(1, 0, 2).reshape(r, N_DEV * c)


def _own_row(landed, own, me):
    return lax.dynamic_update_slice(landed, own[None], (me, 0, 0))


def kernel(x, mem, positions, g_mix, w_in, b_gate, g_q_lat, w_uq, g_kv_lat, w_ukv, w_a_proj, w_b_proj, w_o, g_x, g_mem, w_xq, w_xkv, w_xo, g_ffn, w_gate, w_up, w_down, g_final, loss_target, m_g_mix, m_w_in, m_b_gate, m_g_q_lat, m_w_uq, m_g_kv_lat, m_w_ukv, m_w_a_proj, m_w_b_proj, m_w_o, m_g_x, m_g_mem, m_w_xq, m_w_xkv, m_w_xo, m_g_ffn, m_w_gate, m_w_up, m_w_down, m_g_final, v_g_mix, v_w_in, v_b_gate, v_g_q_lat, v_w_uq, v_g_kv_lat, v_w_ukv, v_w_a_proj, v_w_b_proj, v_w_o, v_g_x, v_g_mem, v_w_xq, v_w_xkv, v_w_xo, v_g_ffn, v_w_gate, v_w_up, v_w_down, v_g_final):
    given = dict(locals())
    s_len = x.shape[1]
    x2d = x.reshape(s_len, D_MODEL)
    mem2d = mem.reshape(-1, D_MODEL)
    target = loss_target.reshape(s_len, D_MODEL)

    me = 4 * lax.axis_index("x") + 2 * lax.axis_index("y") + lax.axis_index("c")
    axis_of = {name: axis for name, _, axis in SHARDED}
    shard2d = lambda name, prefix="": given[prefix + name].reshape(
        _shard_shape(dict((n, s) for n, s, _ in SHARDED)[name], axis_of[name]))

    wire = lambda name: shard2d(name) if name == "b_gate" else shard2d(name).astype(BF16)
    first = _exchange(True, [wire(n) for n in GATHER_FIRST], "weights_gather_first")
    rest_src = [wire(n) for n in GATHER_REST]
    rest_handle, rest_token = _exchange_start(True, rest_src, first[0], "weights_gather_rest_start")
    wts = {n: _join_shards(g, axis_of[n]) for n, g in zip(GATHER_FIRST, first)}
    g_mix_t = g_mix + rest_token[0:1, 0:1]

    w_in_p = jnp.concatenate([wts["w_in"][:, :416], jnp.zeros((D_MODEL, 96), BF16), wts["w_in"][:, 416:]], axis=1)
    w_uq_p = jnp.pad(wts["w_uq"].reshape(256, MLA_HEADS, 96), ((0, 0), (0, 0), (0, 32))).reshape(256, 1024)
    ukv = wts["w_ukv"].reshape(128, MLA_HEADS, 128)
    w_uk_p = jnp.pad(ukv[:, :, :64], ((0, 0), (0, 0), (0, 64))).reshape(128, 1024)
    w_uv = ukv[:, :, 64:].reshape(128, 512)
    bg = wts["b_gate"]

    inv_freq = ROPE_THETA ** (-jnp.arange(0, MLA_ROPE, 2, dtype=F32) / MLA_ROPE)
    ang = positions.reshape(s_len).astype(F32)[:, None] * inv_freq
    cos16, sin16 = jnp.cos(ang), jnp.sin(ang)
    cosf = jnp.concatenate([jnp.ones((s_len, 64), F32), cos16, cos16, jnp.ones((s_len, 32), F32)], axis=1)
    sinf = jnp.concatenate([jnp.zeros((s_len, 64), F32), sin16, sin16, jnp.zeros((s_len, 32), F32)], axis=1)

    h1, lat, sb, gates = _in_proj(x2d, g_mix_t, w_in_p)
    qa, ka, va, q_lat, kv_lat = _mla_prep(lat, g_q_lat, g_kv_lat, w_uq_p, w_uk_p, w_uv, cosf, sinf)
    oa, lse = _mla_fwd(qa, ka, va)
    ob, sb_r = _sb_fwd(sb)
    rest = _exchange_wait(True, rest_handle, ob, "weights_gather_rest_wait")
    for n, g, own in zip(GATHER_REST, rest, rest_src):
        wts[n] = _join_shards(_own_row(g, own, me), axis_of[n])
    x1 = _merge_fwd(x2d, oa, ob, gates, bg, wts["w_a_proj"], wts["w_b_proj"], wts["w_o"])
    mn, xkv = _mem_kv(mem2d, g_mem, wts["w_xkv"])
    x2 = _xattn_fwd(x1, g_x, wts["w_xq"], xkv, wts["w_xo"])
    x3 = _ffn_fwd(x2, g_ffn, wts["w_gate"], wts["w_up"], wts["w_down"])
    g_final2d = g_final.reshape(1, D_MODEL)
    sse, dx3, dx3b, dg_final = _loss_head(x3, g_final2d, target)
    loss = lax.psum(sse[0, 0] * (0.5 / D_MODEL), ("x", "y", "c"))

    pieces = {}

    def start_group(names, grads, tag):
        for n in names:
            pieces[n] = _split_pieces(grads[n], axis_of[n])
        return _exchange_start(False, [pieces[n] for n in names], pieces[names[0]], "grads_" + tag + "_start")

    dx2, hf, dgt, dup, act, dg_ffn = _ffn_bwd(x2, dx3, g_ffn, wts["w_gate"], wts["w_up"], wts["w_down"])
    ffn_handle, ffn_token = start_group(SCATTER_FFN, {
        "w_gate": _tn_matmul(hf, dgt, "dw_gate", tn=FF_TILE),
        "w_up": _tn_matmul(hf, dup, "dw_up", tn=FF_TILE),
        "w_down": _tn_matmul(act, dx3b, "dw_down", tka=FF_TILE)}, "ffn")
    dx1, dw_xq, dw_xo, dxkv, dg_x = _xattn_bwd(x1, dx2, g_x + ffn_token[0:1, 0:1], wts["w_xq"], xkv, wts["w_xo"])
    dw_xkv, dg_mem = _mem_bwd(mem2d, g_mem, wts["w_xkv"], mn, dxkv)
    doa, dob, dgates, dpa, dpb, merged, dx1b, dbg = _merge_bwd(
        dx1, oa, ob, gates, bg, wts["w_a_proj"], wts["w_b_proj"], wts["w_o"])
    mid_handle, mid_token = start_group(SCATTER_MID, {
        "w_xq": dw_xq, "w_xkv": dw_xkv, "w_xo": dw_xo,
        "w_a_proj": _tn_matmul(oa, dpa, "dw_a"),
        "w_b_proj": _tn_matmul(ob, dpb, "dw_b"),
        "w_o": _tn_matmul(merged, dx1b, "dw_o")}, "mid")
    dsbq, dsbk, dsbv = _sb_bwd(sb, dob + mid_token[0:1, 0:1].astype(BF16), sb_r)
    dqa, dka, dva = _mla_bwd(qa, ka, va, oa, doa, lse)
    dlat, dqb, dkb, dvb, dg_q, dg_kv = _mla_prep_bwd(
        lat, g_q_lat, g_kv_lat, w_uq_p, w_uk_p, w_uv, cosf, sinf, dqa, dka, dva)
    grad_x, dproj, dg_mix = _in_proj_bwd(x2d, g_mix, w_in_p, dx1, dlat, dsbq, dsbk, dsbv, dgates)
    dw_in_p = _tn_matmul(h1, dproj, "dw_in")
    dw_uq_p = _tn_matmul(q_lat, dqb, "dw_uq")
    dw_uk_p = _tn_matmul(kv_lat, dkb, "dw_uk")
    dw_uv = _tn_matmul(kv_lat, dvb, "dw_uv")
    last_grads = {
        "w_in": jnp.concatenate([dw_in_p[:, :416], dw_in_p[:, 512:]], axis=1),
        "b_gate": dbg,
        "w_uq": dw_uq_p.reshape(256, MLA_HEADS, 128)[:, :, :96].reshape(256, 768),
        "w_ukv": jnp.concatenate([dw_uk_p.reshape(128, MLA_HEADS, 128)[:, :, :64],
                                  dw_uv.reshape(128, MLA_HEADS, 64)], axis=2).reshape(128, 1024),
    }
    rep_grads = {"g_mix": dg_mix, "g_q_lat": dg_q, "g_kv_lat": dg_kv, "g_x": dg_x, "g_mem": dg_mem,
                 "g_ffn": dg_ffn, "g_final": dg_final}
    rep_cat = lambda prefix, src: jnp.concatenate(
        [src[prefix + n].reshape(-1) for n, _ in REPLICATED]).reshape(-1, LANES)
    rep_src = jnp.broadcast_to(rep_cat("", rep_grads), (N_DEV,) + rep_cat("", rep_grads).shape)
    last = _exchange(False, [_split_pieces(last_grads[n], axis_of[n]) for n in SCATTER_LAST] + [rep_src],
                     "grads_last")
    landed = dict(zip(SCATTER_LAST, last[:-1]))
    rep_landed = last[-1]
    for names, handle, tag in ((SCATTER_FFN, ffn_handle, "ffn"), (SCATTER_MID, mid_handle, "mid")):
        got = _exchange_wait(False, handle, rep_landed, "grads_" + tag + "_wait")
        for n, g in zip(names, got):
            landed[n] = _own_row(g, lax.dynamic_index_in_dim(pieces[n], me, 0, keepdims=False), me)

    res = {}
    for name, _, _ in SHARDED:
        outs = _adamw(landed[name], shard2d(name), shard2d(name, "m_"), shard2d(name, "v_"), "adamw_" + name)
        res[name] = [o.reshape(given[name].shape) for o in outs]
    rep_outs = _adamw(rep_landed, rep_cat("", given), rep_cat("m_", given), rep_cat("v_", given), "adamw_gains")
    off = 0
    for name, n in REPLICATED:
        res[name] = [o.reshape(-1)[off:off + n].reshape(given[name].shape) for o in rep_outs]
        off += n
    result = [loss, grad_x.reshape(x.shape)]
    for k in range(4):
        result.extend(res[name][k] for name in WEIGHT_ORDER)
    return tuple(result)
```

```python
import functools
import math

import jax
import jax.numpy as jnp
from jax import lax
from jax.experimental import pallas as pl
from jax.experimental.pallas import tpu as pltpu

F32 = jnp.float32
BF16 = jnp.bfloat16

D_MODEL = 1024
MLA_HEADS = 8
MLA_Q_RANK = 256
MLA_KV_RANK = 128
MLA_NOPE = 64
MLA_ROPE = 32
MLA_V = 64
ROPE_THETA = 10000.0
SB_WIDTH = 512
X_HEADS = 4
X_HEAD_DIM = 128
D_FF = 2816
EPS = 1e-6
D_IN = 4000
D_IN_PAD = 4096
K_R_OFF = 384
LANES = 128
HEAD_PAD = 128
MLA_SCALE = 1.0 / math.sqrt(MLA_NOPE + MLA_ROPE)
SB_SCALE = 0.125
LOG2E = math.log2(math.e)
LN2 = math.log(2.0)
MLA_Q_FOLD = MLA_SCALE * LOG2E
SB_Q_FOLD = SB_SCALE * LOG2E
SB_CUT = -160.0
X_SCALE = 1.0 / math.sqrt(X_HEAD_DIM)
NEG_BIG = -1e30

ADAM_LR = 0.001
ADAM_B1 = 0.9
ADAM_B2 = 0.999
ADAM_EPS = 1e-08
ADAM_WD = 0.01
ADAM_STEP = 10

N_DEV = 8
MIB = 1024 * 1024
ADAM_BLOCK_BYTES = 4 * MIB

SHARDED = (
    ("w_in", (D_MODEL, D_IN), 1),
    ("b_gate", (2, D_MODEL), 1),
    ("w_uq", (MLA_Q_RANK, 768), 1),
    ("w_ukv", (MLA_KV_RANK, 1024), 1),
    ("w_a_proj", (512, D_MODEL), 1),
    ("w_b_proj", (512, D_MODEL), 1),
    ("w_o", (D_MODEL, D_MODEL), 0),
    ("w_xq", (D_MODEL, 512), 0),
    ("w_xkv", (D_MODEL, 1024), 0),
    ("w_xo", (512, D_MODEL), 1),
    ("w_gate", (D_MODEL, D_FF), 1),
    ("w_up", (D_MODEL, D_FF), 1),
    ("w_down", (D_FF, D_MODEL), 0),
)
REPLICATED = (
    ("g_mix", 1024), ("g_q_lat", 256), ("g_kv_lat", 128), ("g_x", 1024),
    ("g_mem", 1024), ("g_ffn", 1024), ("g_final", 1024),
)
WEIGHT_ORDER = ("g_mix", "w_in", "b_gate", "g_q_lat", "w_uq", "g_kv_lat", "w_ukv", "w_a_proj",
                "w_b_proj", "w_o", "g_x", "g_mem", "w_xq", "w_xkv", "w_xo", "g_ffn", "w_gate",
                "w_up", "w_down", "g_final")


def _round_up(n, m):
    return -(-n // m) * m


def _cparams(vmem_mib=None, **kw):
    if vmem_mib is not None:
        kw["vmem_limit_bytes"] = vmem_mib * MIB
    return pltpu.CompilerParams(**kw)


def _dot(a, b):
    return jnp.dot(a, b, preferred_element_type=F32)


def _dot_nt(a, b):
    return lax.dot_general(a, b, (((1,), (1,)), ((), ())), preferred_element_type=F32)


def _dot_tn(a, b):
    return lax.dot_general(a, b, (((0,), (0,)), ((), ())), preferred_element_type=F32)


def _rms(x, g):
    r = lax.rsqrt(jnp.mean(x * x, axis=-1, keepdims=True) + EPS)
    xh = x * r
    return xh * g, xh, r


def _rms_bwd(dy, xh, r, g):
    u = dy * g
    dx = r * (u - xh * jnp.mean(u * xh, axis=-1, keepdims=True))
    return dx, dy * xh


def _sigmoid(z):
    return 1.0 / (1.0 + jnp.exp(-z))


def _acc_rows(ref, val, first):
    s = jnp.sum(val, axis=0, keepdims=True)

    @pl.when(first)
    def _():
        ref[...] = s

    @pl.when(jnp.logical_not(first))
    def _():
        ref[...] += s


def _acc(ref, val, first):
    @pl.when(first)
    def _():
        ref[...] = val

    @pl.when(jnp.logical_not(first))
    def _():
        ref[...] += val


def _peer(k):
    x, y, c = lax.axis_index("x"), lax.axis_index("y"), lax.axis_index("c")
    px = 1 - x if (k >> 2) & 1 else x
    py = 1 - y if (k >> 1) & 1 else y
    pc = 1 - c if k & 1 else c
    return (px, py, pc), 4 * px + 2 * py + pc


N_PEERS = N_DEV - 1
OTHER_CHIPS = (2, 4, 6)


def _land_shape(gather, src):
    return (N_DEV,) + src.shape if gather else src.shape


def _exchange(gather, srcs, name):
    n = len(srcs)

    def body(*refs):
        src, land = refs[:n], refs[n:2 * n]
        send_sems, recv_sems, local_sems = refs[2 * n:]
        _, me = _peer(0)

        def copy(a, k, source, to, target=1):
            return pltpu.make_async_remote_copy(
                src_ref=source, dst_ref=to,
                send_sem=send_sems.at[a * N_PEERS + k - 1], recv_sem=recv_sems.at[a * N_PEERS + k - 1],
                device_id=_peer(target)[0], device_id_type=pl.DeviceIdType.MESH)

        row = lambda a, k: land[a].at[_peer(k)[1]]
        mine = [pltpu.make_async_copy(src[a] if gather else src[a].at[me], land[a].at[me], local_sems.at[a])
                for a in range(n)]
        for cp in mine:
            cp.start()
        if gather:
            for a in range(n):
                for k in (1,) + OTHER_CHIPS:
                    copy(a, k, src[a], land[a].at[me], target=k).start()
            for a in range(n):
                for k in OTHER_CHIPS:
                    copy(a, k, src[a], row(a, k)).wait_recv()
                    copy(a, k + 1, row(a, k), row(a, k), target=1).start()
            for a in range(n):
                for k in (1, 3, 5, 7):
                    copy(a, k, src[a], row(a, k)).wait_recv()
            for a in range(n):
                for k in range(1, N_DEV):
                    copy(a, k, src[a], land[a].at[me]).wait_send()
        else:
            for a in range(n):
                for k in range(1, N_DEV):
                    copy(a, k, src[a].at[_peer(k)[1]], land[a].at[me], target=k).start()
            for a in range(n):
                for k in range(1, N_DEV):
                    copy(a, k, src[a].at[me], row(a, k)).wait_recv()
            for a in range(n):
                for k in range(1, N_DEV):
                    copy(a, k, src[a].at[me], land[a].at[me]).wait_send()
        for cp in mine:
            cp.wait()

    return pl.pallas_call(
        body, name=name,
        out_shape=[jax.ShapeDtypeStruct(_land_shape(gather, s), s.dtype) for s in srcs],
        in_specs=[pl.BlockSpec(memory_space=pl.ANY)] * n,
        out_specs=[pl.BlockSpec(memory_space=pl.ANY)] * n,
        scratch_shapes=[pltpu.SemaphoreType.DMA((n * N_PEERS,)), pltpu.SemaphoreType.DMA((n * N_PEERS,)),
                        pltpu.SemaphoreType.DMA((n,))],
    )(*srcs)


def _tn_matmul(a, b, name, tka=512, tn=1024, ts=512):
    s_len, ka = a.shape
    n = b.shape[1]
    tka, tn, ts = min(tka, ka), min(tn, n), min(ts, s_len)
    assert ka % tka == 0 and n % tn == 0 and s_len % ts == 0

    def body(a_ref, b_ref, o_ref):
        _acc(o_ref, _dot_tn(a_ref[...], b_ref[...]), pl.program_id(2) == 0)

    return pl.pallas_call(
        body, name=name, grid=(ka // tka, n // tn, s_len // ts),
        in_specs=[pl.BlockSpec((ts, tka), lambda i, j, s: (s, i)),
                  pl.BlockSpec((ts, tn), lambda i, j, s: (s, j))],
        out_specs=pl.BlockSpec((tka, tn), lambda i, j, s: (i, j)),
        out_shape=jax.ShapeDtypeStruct((ka, n), F32),
        compiler_params=_cparams(dimension_semantics=("parallel", "parallel", "arbitrary")),
    )(a, b)


def _row_block(s_len):
    return min(s_len, 512)


def _in_proj(x, g, w):
    s_len = x.shape[0]
    tm, tn = _row_block(s_len), 512

    def body(x_ref, g_ref, w_ref, h_ref, lat_ref, sb_ref, gate_ref, h_scr):
        j = pl.program_id(1)

        @pl.when(j == 0)
        def _():
            h, _, _ = _rms(x_ref[...], g_ref[...])
            hb = h.astype(BF16)
            h_scr[...] = hb
            h_ref[...] = hb

        p = _dot(h_scr[...], w_ref[...])

        @pl.when(j == 0)
        def _():
            lat_ref[...] = p

        @pl.when(j == 1)
        def _():
            sb_ref[...] = (p * SB_Q_FOLD).astype(BF16)

        @pl.when((j == 2) | (j == 3))
        def _():
            sb_ref[...] = p.astype(BF16)

        @pl.when(j >= 4)
        def _():
            gate_ref[...] = p

    return pl.pallas_call(
        body, name="in_proj", grid=(s_len // tm, D_IN_PAD // tn),
        in_specs=[pl.BlockSpec((tm, D_MODEL), lambda i, j: (i, 0)),
                  pl.BlockSpec((1, D_MODEL), lambda i, j: (0, 0)),
                  pl.BlockSpec((D_MODEL, tn), lambda i, j: (0, j))],
        out_specs=[pl.BlockSpec((tm, D_MODEL), lambda i, j: (i, 0)),
                   pl.BlockSpec((tm, tn), lambda i, j: (i, 0)),
                   pl.BlockSpec((tm, tn), lambda i, j: (i, jnp.clip(j - 1, 0, 2))),
                   pl.BlockSpec((tm, tn), lambda i, j: (i, jnp.clip(j - 4, 0, 3)))],
        out_shape=[jax.ShapeDtypeStruct((s_len, D_MODEL), BF16),
                   jax.ShapeDtypeStruct((s_len, 512), F32),
                   jax.ShapeDtypeStruct((s_len, 3 * SB_WIDTH), BF16),
                   jax.ShapeDtypeStruct((s_len, 2 * D_MODEL), F32)],
        scratch_shapes=[pltpu.VMEM((tm, D_MODEL), BF16)],
        compiler_params=_cparams(dimension_semantics=("parallel", "arbitrary")),
    )(x, g, w)


def _rope_rot(blk, lane):
    return jnp.where(lane < 80, -pltpu.roll(blk, 112, 1), pltpu.roll(blk, 16, 1))


def _rope_rot_t(blk, lane):
    return jnp.where(lane < 80, pltpu.roll(blk, 112, 1), -pltpu.roll(blk, 16, 1))


def _mla_prep(lat, g_q, g_kv, w_uq, w_uk, w_uv, cosf, sinf):
    s_len = lat.shape[0]
    tm = _row_block(s_len)

    def body(lat_ref, gq_ref, gkv_ref, wuq_ref, wuk_ref, wuv_ref, cos_ref, sin_ref,
             q_ref, k_ref, v_ref, ql_ref, kvl_ref):
        lane = lax.broadcasted_iota(jnp.int32, (tm, LANES), 1)
        cosv, sinv = cos_ref[...], sin_ref[...]
        ql, _, _ = _rms(lat_ref[:, 0:256], gq_ref[...])
        kvl, _, _ = _rms(lat_ref[:, 256:384], gkv_ref[...])
        qlb, kvlb = ql.astype(BF16), kvl.astype(BF16)
        ql_ref[...] = qlb
        kvl_ref[...] = kvlb
        q = _dot(qlb, wuq_ref[...])
        kn = _dot(kvlb, wuk_ref[...])
        v_ref[...] = _dot(kvlb, wuv_ref[...]).astype(BF16)
        kr = pltpu.roll(lat_ref[:, K_R_OFF:K_R_OFF + LANES], 64, 1)
        kr = kr * cosv + _rope_rot(kr, lane) * sinv
        for h in range(MLA_HEADS):
            sl = slice(h * HEAD_PAD, (h + 1) * HEAD_PAD)
            blk = q[:, sl]
            q_ref[:, sl] = ((blk * cosv + _rope_rot(blk, lane) * sinv) * MLA_Q_FOLD).astype(BF16)
            k_ref[:, sl] = (kn[:, sl] + kr).astype(BF16)

    full = lambda shape: pl.BlockSpec(shape, lambda i: (0, 0))
    rowb = lambda n: pl.BlockSpec((tm, n), lambda i: (i, 0))
    return pl.pallas_call(
        body, name="mla_prep", grid=(s_len // tm,),
        in_specs=[rowb(512), full((1, 256)), full((1, 128)), full((256, 1024)), full((128, 1024)),
                  full((128, 512)), rowb(128), rowb(128)],
        out_specs=[rowb(1024), rowb(1024), rowb(512), rowb(256), rowb(128)],
        out_shape=[jax.ShapeDtypeStruct((s_len, 1024), BF16), jax.ShapeDtypeStruct((s_len, 1024), BF16),
                   jax.ShapeDtypeStruct((s_len, 512), BF16), jax.ShapeDtypeStruct((s_len, 256), BF16),
                   jax.ShapeDtypeStruct((s_len, 128), BF16)],
        compiler_params=_cparams(dimension_semantics=("parallel",)),
    )(lat, g_q, g_kv, w_uq, w_uk, w_uv, cosf, sinf)


ATTN_TQ = 1024
ATTN_TH = 512
ATTN_TK = 256
MLA_TK = 512


def _attn_blocks(s_len, tk=ATTN_TK):
    tq, th, tk = min(s_len, ATTN_TQ), min(s_len, ATTN_TH), min(s_len, tk)
    return tq, th, tk, tq // tk


def _chains(tq, th):
    return [(hh, r0) for hh in range(2) for r0 in range(0, tq, th)]


def _diag_mask(th, tk, r0, sub, strict):
    lo, hi = sub * tk, (sub + 1) * tk - 1
    last, first = r0 + th - 1, r0
    if (lo >= last) if strict else (lo > last):
        return "none"
    if (hi < first) if strict else (hi <= first):
        return "all"
    row = lax.broadcasted_iota(jnp.int32, (th, tk), 0) + r0
    col = lax.broadcasted_iota(jnp.int32, (th, tk), 1) + lo
    return col < row if strict else col <= row


def _mla_fwd(q, k, v):
    s_len = q.shape[0]
    tq, th, tk, nsub = _attn_blocks(s_len, MLA_TK)
    chains = _chains(tq, th)
    nh = tq // th

    def body(q_ref, k_ref, v_ref, o_ref, lse_ref):
        i = pl.program_id(1)
        lane = lax.broadcasted_iota(jnp.int32, (th, LANES), 1)
        hsl = [slice(hh * HEAD_PAD, (hh + 1) * HEAD_PAD) for hh in range(2)]

        def step(kb, carry, sub):
            rows = pl.ds(pl.multiple_of(kb * tk, tk), tk)
            vblk = v_ref[rows, :]
            masks = ["all" if sub is None else _diag_mask(th, tk, r0, sub, strict=False) for _, r0 in chains]
            live = [n for n, m in enumerate(masks) if not (isinstance(m, str) and m == "none")]
            s = {n: _dot_nt(q_ref[chains[n][1]:chains[n][1] + th, hsl[chains[n][0]]], k_ref[rows, hsl[chains[n][0]]])
                 for n in live}
            new = list(carry)
            pb, alpha = {}, {}
            for n in live:
                m, l, _ = carry[n]
                sn = s[n]
                if not isinstance(masks[n], str):
                    sn = jnp.where(masks[n], sn, NEG_BIG)
                m_new = jnp.maximum(m, jnp.max(sn, axis=-1, keepdims=True))
                alpha[n] = jnp.exp2(m - m_new)
                p = jnp.exp2(sn - m_new)
                pb[n] = p.astype(BF16)
                new[n] = (m_new, alpha[n] * l + jnp.sum(p, axis=-1, keepdims=True), None)
            pv = {n: _dot(pb[n], vblk) for n in live}
            for n in live:
                new[n] = (new[n][0], new[n][1], alpha[n] * carry[n][2] + pv[n])
            return tuple(new)

        init = (jnp.full((th, 1), NEG_BIG, F32), jnp.zeros((th, 1), F32), jnp.zeros((th, LANES), F32))
        carry = lax.fori_loop(0, i * nsub, lambda kb, cy: step(kb, cy, None), (init,) * len(chains))
        for sub in range(nsub):
            carry = step(i * nsub + sub, carry, sub)
        for c in range(nh):
            (m0, l0, a0), (m1, l1, a1) = carry[c], carry[nh + c]
            rs = slice(c * th, (c + 1) * th)
            o_ref[rs, :] = jnp.where(lane < 64, a0 / l0, a1 / l1).astype(BF16)
            lse_ref[rs, :] = jnp.where(lane < 64, m0 + jnp.log2(l0), m1 + jnp.log2(l1))

    return pl.pallas_call(
        body, name="mla_fwd", grid=(4, s_len // tq),
        in_specs=[pl.BlockSpec((tq, 2 * HEAD_PAD), lambda p, i: (i, p)),
                  pl.BlockSpec((s_len, 2 * HEAD_PAD), lambda p, i: (0, p)),
                  pl.BlockSpec((s_len, LANES), lambda p, i: (0, p))],
        out_specs=[pl.BlockSpec((tq, LANES), lambda p, i: (i, p)),
                   pl.BlockSpec((None, tq, LANES), lambda p, i: (p, i, 0))],
        out_shape=[jax.ShapeDtypeStruct((s_len, 512), BF16), jax.ShapeDtypeStruct((4, s_len, LANES), F32)],
        compiler_params=_cparams(40, dimension_semantics=("parallel", "arbitrary")),
    )(q, k, v)


def _mla_bwd(q, k, v, o, do, lse):
    s_len = q.shape[0]
    tq, th, tk, nsub = _attn_blocks(s_len, MLA_TK)
    chains = _chains(tq, th)

    def body(q_ref, k_ref, v_ref, o_ref, do_ref, lse_ref, dq_ref, dk_ref, dv_ref):
        i = pl.program_id(1)
        lane = lax.broadcasted_iota(jnp.int32, (th, LANES), 1)

        @pl.when(i == 0)
        def _():
            dk_ref[...] = jnp.zeros_like(dk_ref)
            dv_ref[...] = jnp.zeros_like(dv_ref)

        hsl = [slice(hh * HEAD_PAD, (hh + 1) * HEAD_PAD) for hh in range(2)]
        qs, dos, deltas, lses = [], [], [], []
        for hh, r0 in chains:
            rs = slice(r0, r0 + th)
            qs.append(q_ref[rs, hsl[hh]])
            doh = jnp.where((lane // 64) == hh, do_ref[rs, :], jnp.zeros((), BF16))
            dos.append(doh)
            deltas.append(jnp.sum(doh.astype(F32) * o_ref[rs, :].astype(F32), axis=-1, keepdims=True))
            lses.append(lse_ref[rs, 64 * hh:64 * hh + 1])

        def step(kb, dqs, sub):
            rows = pl.ds(pl.multiple_of(kb * tk, tk), tk)
            vblk = v_ref[rows, :]
            new, p_all, do_all = [], [], []
            ds_h, q_h = [[], []], [[], []]
            for c, (hh, r0) in enumerate(chains):
                mask = "all" if sub is None else _diag_mask(th, tk, r0, sub, strict=False)
                if isinstance(mask, str) and mask == "none":
                    new.append(dqs[c])
                    continue
                kblk = k_ref[rows, hsl[hh]]
                s = _dot_nt(qs[c], kblk)
                if not isinstance(mask, str):
                    s = jnp.where(mask, s, NEG_BIG)
                p = jnp.exp2(s - lses[c])
                dp = _dot_nt(dos[c], vblk)
                ds = (p * (dp - deltas[c]) * MLA_SCALE).astype(BF16)
                p_all.append(p.astype(BF16))
                do_all.append(dos[c])
                ds_h[hh].append(ds)
                q_h[hh].append(qs[c])
                new.append(dqs[c] + _dot(ds, kblk))
            dv_ref[rows, :] += _dot_tn(jnp.concatenate(p_all, axis=0), jnp.concatenate(do_all, axis=0))
            for hh in range(2):
                dk_ref[rows, hsl[hh]] += _dot_tn(jnp.concatenate(ds_h[hh], axis=0),
                                                 jnp.concatenate(q_h[hh], axis=0))
            return tuple(new)

        zero = jnp.zeros((th, LANES), F32)
        dqs = lax.fori_loop(0, i * nsub, lambda kb, cy: step(kb, cy, None), (zero,) * len(chains))
        for sub in range(nsub):
            dqs = step(i * nsub + sub, dqs, sub)
        for c, (hh, r0) in enumerate(chains):
            dq_ref[r0:r0 + th, hsl[hh]] = dqs[c]

    return pl.pallas_call(
        body, name="mla_bwd", grid=(4, s_len // tq),
        in_specs=[pl.BlockSpec((tq, 2 * HEAD_PAD), lambda p, i: (i, p)),
                  pl.BlockSpec((s_len, 2 * HEAD_PAD), lambda p, i: (0, p)),
                  pl.BlockSpec((s_len, LANES), lambda p, i: (0, p)),
                  pl.BlockSpec((tq, LANES), lambda p, i: (i, p)),
                  pl.BlockSpec((tq, LANES), lambda p, i: (i, p)),
                  pl.BlockSpec((None, tq, LANES), lambda p, i: (p, i, 0))],
        out_specs=[pl.BlockSpec((tq, 2 * HEAD_PAD), lambda p, i: (i, p)),
                   pl.BlockSpec((s_len, 2 * HEAD_PAD), lambda p, i: (0, p)),
                   pl.BlockSpec((s_len, LANES), lambda p, i: (0, p))],
        out_shape=[jax.ShapeDtypeStruct((s_len, 1024), F32), jax.ShapeDtypeStruct((s_len, 1024), F32),
                   jax.ShapeDtypeStruct((s_len, 512), F32)],
        compiler_params=_cparams(56, dimension_semantics=("arbitrary", "arbitrary")),
    )(q, k, v, o, do, lse)


def _log_sigmoids(z2):
    sp = jnp.log2(1.0 + jnp.exp2(-jnp.abs(z2)))
    lb = jnp.minimum(z2, 0.0) - sp
    return lb, lb - z2


def _split_dot(x, w, parts, nt=False):
    dot = _dot_nt if nt else _dot
    out = None
    for _ in range(parts):
        xb = x.astype(BF16)
        t = dot(xb, w)
        out = t if out is None else out + t
        x = x - xb.astype(F32)
    return out


def _sb_fwd(sb):
    s_len = sb.shape[0]
    tq, th, tk, nsub = _attn_blocks(s_len)
    chains = _chains(tq, th)
    nh = tq // th
    assert s_len // tk <= 64

    def body(q_ref, k_ref, v_ref, o_ref, r_ref):
        i = pl.program_id(1)
        lane = lax.broadcasted_iota(jnp.int32, (th, LANES), 1)
        upper = (lax.broadcasted_iota(jnp.int32, (tk, tk), 0)
                 > lax.broadcasted_iota(jnp.int32, (tk, tk), 1)).astype(BF16)
        qs = [jnp.where((lane // 64) == hh, q_ref[r0:r0 + th, :], jnp.zeros((), BF16)) for hh, r0 in chains]

        def step(kb, carry, sub):
            rows = pl.ds(pl.multiple_of(kb * tk, tk), tk)
            kblk, vblk = k_ref[rows, :], v_ref[rows, :]
            masks = ["all" if sub is None else _diag_mask(th, tk, r0, sub, strict=True) for _, r0 in chains]
            live = [n for n, m in enumerate(masks) if not (isinstance(m, str) and m == "none")]
            masked = {n: not isinstance(masks[n], str) for n in live}
            z = {n: _dot_nt(qs[n], kblk) for n in live}
            lb, lom = {}, {}
            for n in live:
                lb[n], lom[n] = _log_sigmoids(z[n])
                if masked[n]:
                    lom[n] = jnp.where(masks[n], lom[n], 0.0)
            suf = {n: _split_dot(lom[n], upper, 2) for n in live}
            a = {}
            for n in live:
                a[n] = jnp.exp2(lb[n] + suf[n] + carry[n][0])
                if masked[n]:
                    a[n] = jnp.where(masks[n], a[n], 0.0)
            pv = {n: _dot(a[n].astype(BF16), vblk) for n in live}
            new = list(carry)
            for n in live:
                c, acc, r = carry[n]
                rs = suf[n][:, 0:1] + lom[n][:, 0:1]
                new[n] = (c + rs, acc + pv[n], jnp.where(lane == 64 * chains[n][0] + kb, rs, r))
            return tuple(new)

        init = (jnp.zeros((th, 1), F32), jnp.zeros((th, LANES), F32), jnp.zeros((th, LANES), F32))
        carry = (init,) * len(chains)
        for sub in reversed(range(nsub)):
            carry = step(i * nsub + sub, carry, sub)

        def spent(cy):
            top = functools.reduce(jnp.maximum, [jnp.max(c) for c, _, _ in cy])
            return (top < SB_CUT).astype(jnp.int32)

        def walk(state):
            t, _, cy = state
            cy = step(i * nsub - 1 - t, cy, None)
            return t + 1, spent(cy), cy

        _, _, carry = lax.while_loop(lambda st: (st[0] < i * nsub) & (st[1] == 0), walk,
                                     (jnp.int32(0), spent(carry), carry))
        for n in range(nh):
            rs = slice(n * th, (n + 1) * th)
            o_ref[rs, :] = jnp.where(lane < 64, carry[n][1], carry[nh + n][1]).astype(BF16)
            r_ref[rs, :] = jnp.where(lane < 64, carry[n][2], carry[nh + n][2])

    return pl.pallas_call(
        body, name="sb_fwd", grid=(4, s_len // tq),
        in_specs=[pl.BlockSpec((tq, LANES), lambda p, i: (i, p)),
                  pl.BlockSpec((s_len, LANES), lambda p, i: (0, 4 + p)),
                  pl.BlockSpec((s_len, LANES), lambda p, i: (0, 8 + p))],
        out_specs=[pl.BlockSpec((tq, LANES), lambda p, i: (i, p)),
                   pl.BlockSpec((None, tq, LANES), lambda p, i: (p, i, 0))],
        out_shape=[jax.ShapeDtypeStruct((s_len, 512), BF16), jax.ShapeDtypeStruct((4, s_len, LANES), F32)],
        compiler_params=_cparams(40, dimension_semantics=("parallel", "arbitrary")),
    )(sb, sb, sb)


def _sb_bwd(sb, do, r):
    s_len = sb.shape[0]
    tq, th, tk, nsub = _attn_blocks(s_len)
    chains = _chains(tq, th)
    nh = tq // th

    def body(q_ref, k_ref, v_ref, do_ref, r_ref, dq_ref, dk_ref, dv_ref):
        i = pl.program_id(1)
        lane = lax.broadcasted_iota(jnp.int32, (th, LANES), 1)
        upper = (lax.broadcasted_iota(jnp.int32, (tk, tk), 0)
                 > lax.broadcasted_iota(jnp.int32, (tk, tk), 1)).astype(BF16)
        tri = (lax.broadcasted_iota(jnp.int32, (LANES, LANES), 0)
               > lax.broadcasted_iota(jnp.int32, (LANES, LANES), 1)).astype(BF16)

        @pl.when(i == 0)
        def _():
            dk_ref[...] = jnp.zeros_like(dk_ref)
            dv_ref[...] = jnp.zeros_like(dv_ref)

        qs, dos, rights = [], [], []
        for hh, r0 in chains:
            rs = slice(r0, r0 + th)
            hm = (lane // 64) == hh
            qs.append(jnp.where(hm, q_ref[rs, :], jnp.zeros((), BF16)))
            dos.append(jnp.where(hm, do_ref[rs, :], jnp.zeros((), BF16)))
            rights.append(_split_dot(jnp.where(hm, r_ref[rs, :], 0.0), tri, 3))

        def step(kb, carry, sub):
            rows = pl.ds(pl.multiple_of(kb * tk, tk), tk)
            kblk, vblk = k_ref[rows, :], v_ref[rows, :]
            new, a_all, do_all, dz_all, q_all = [], [], [], [], []
            for n, ((hh, r0), (pre, dq)) in enumerate(zip(chains, carry)):
                mask = "all" if sub is None else _diag_mask(th, tk, r0, sub, strict=True)
                if isinstance(mask, str) and mask == "none":
                    new.append((pre, dq))
                    continue
                c = jnp.sum(jnp.where(lane == 64 * hh + kb, rights[n], 0.0), axis=-1, keepdims=True)
                z = _dot_nt(qs[n], kblk)
                lb, lom = _log_sigmoids(z)
                if not isinstance(mask, str):
                    lom = jnp.where(mask, lom, 0.0)
                suf = _split_dot(lom, upper, 2)
                a = jnp.exp2(lb + suf + c)
                if not isinstance(mask, str):
                    a = jnp.where(mask, a, 0.0)
                g = a * _dot_nt(dos[n], vblk)
                left = _split_dot(g, upper, 1, nt=True) + pre
                sig = jnp.exp2(lb)
                dz = g * (1.0 - sig) - sig * left
                if not isinstance(mask, str):
                    dz = jnp.where(mask, dz, 0.0)
                dzb = dz.astype(BF16)
                a_all.append(a.astype(BF16))
                do_all.append(dos[n])
                dz_all.append(dzb)
                q_all.append(qs[n])
                new.append((left[:, tk - 1:tk] + g[:, tk - 1:tk], dq + _dot(dzb, kblk)))
            dv_ref[rows, :] += _dot_tn(jnp.concatenate(a_all, axis=0), jnp.concatenate(do_all, axis=0))
            dk_ref[rows, :] += _dot_tn(jnp.concatenate(dz_all, axis=0), jnp.concatenate(q_all, axis=0))
            return tuple(new)

        lane1 = lax.broadcasted_iota(jnp.int32, (1, LANES), 1)
        first = i * nsub
        for n, (hh, _) in enumerate(chains):
            top = jnp.max(rights[n], axis=0, keepdims=True)
            kb_of = lane1 - 64 * hh
            live = (kb_of >= 0) & (kb_of < i * nsub) & (top >= SB_CUT)
            first = jnp.minimum(first, jnp.min(jnp.where(live, kb_of, i * nsub)))

        init = (jnp.zeros((th, 1), F32), jnp.zeros((th, LANES), F32))
        carry = lax.fori_loop(first, i * nsub, lambda kb, cy: step(kb, cy, None), (init,) * len(chains))
        for sub in range(nsub):
            carry = step(i * nsub + sub, carry, sub)
        for n in range(nh):
            dq_ref[n * th:(n + 1) * th, :] = jnp.where(lane < 64, carry[n][1], carry[nh + n][1]) * SB_SCALE

    return pl.pallas_call(
        body, name="sb_bwd", grid=(4, s_len // tq),
        in_specs=[pl.BlockSpec((tq, LANES), lambda p, i: (i, p)),
                  pl.BlockSpec((s_len, LANES), lambda p, i: (0, 4 + p)),
                  pl.BlockSpec((s_len, LANES), lambda p, i: (0, 8 + p)),
                  pl.BlockSpec((tq, LANES), lambda p, i: (i, p)),
                  pl.BlockSpec((None, tq, LANES), lambda p, i: (p, i, 0))],
        out_specs=[pl.BlockSpec((tq, LANES), lambda p, i: (i, p)),
                   pl.BlockSpec((s_len, LANES), lambda p, i: (0, p)),
                   pl.BlockSpec((s_len, LANES), lambda p, i: (0, p))],
        out_shape=[jax.ShapeDtypeStruct((s_len, 512), F32)] * 3,
        compiler_params=_cparams(48, dimension_semantics=("arbitrary", "arbitrary")),
    )(sb, sb, sb, do, r)


def _merge_fwd(x, oa, ob, gates, bg, wa, wb, wo):
    s_len = x.shape[0]
    tm = _row_block(s_len)

    def body(x_ref, oa_ref, ob_ref, g_ref, bg_ref, wa_ref, wb_ref, wo_ref, y_ref):
        pa = _dot(oa_ref[...], wa_ref[...])
        pb = _dot(ob_ref[...], wb_ref[...])
        merged = (_sigmoid(g_ref[:, 0:D_MODEL] + bg_ref[0:1, :]) * pa
                  + _sigmoid(g_ref[:, D_MODEL:2 * D_MODEL] + bg_ref[1:2, :]) * pb)
        y_ref[...] = x_ref[...] + _dot(merged.astype(BF16), wo_ref[...])

    full = lambda shape: pl.BlockSpec(shape, lambda i: (0, 0))
    rowb = lambda n: pl.BlockSpec((tm, n), lambda i: (i, 0))
    return pl.pallas_call(
        body, name="merge_fwd", grid=(s_len // tm,),
        in_specs=[rowb(1024), rowb(512), rowb(512), rowb(2048), full((2, 1024)), full((512, 1024)),
                  full((512, 1024)), full((1024, 1024))],
        out_specs=rowb(1024),
        out_shape=jax.ShapeDtypeStruct((s_len, D_MODEL), F32),
        compiler_params=_cparams(48, dimension_semantics=("parallel",)),
    )(x, oa, ob, gates, bg, wa, wb, wo)


def _merge_bwd(dx1, oa, ob, gates, bg, wa, wb, wo):
    s_len = dx1.shape[0]
    tm = _row_block(s_len)

    def body(dx_ref, oa_ref, ob_ref, g_ref, bg_ref, wa_ref, wb_ref, wo_ref,
             doa_ref, dob_ref, dgate_ref, dpa_ref, dpb_ref, merged_ref, dxb_ref, dbg_ref):
        first = pl.program_id(0) == 0
        dxb = dx_ref[...].astype(BF16)
        dxb_ref[...] = dxb
        pa = _dot(oa_ref[...], wa_ref[...])
        pb = _dot(ob_ref[...], wb_ref[...])
        sa = _sigmoid(g_ref[:, 0:D_MODEL] + bg_ref[0:1, :])
        sbg = _sigmoid(g_ref[:, D_MODEL:2 * D_MODEL] + bg_ref[1:2, :])
        merged_ref[...] = (sa * pa + sbg * pb).astype(BF16)
        dm = _dot_nt(dxb, wo_ref[...])
        dpa = (dm * sa).astype(BF16)
        dpb = (dm * sbg).astype(BF16)
        dpa_ref[...] = dpa
        dpb_ref[...] = dpb
        dga = dm * pa * sa * (1.0 - sa)
        dgb = dm * pb * sbg * (1.0 - sbg)
        dgate_ref[:, 0:D_MODEL] = dga.astype(BF16)
        dgate_ref[:, D_MODEL:2 * D_MODEL] = dgb.astype(BF16)
        _acc_rows(dbg_ref.at[0:1, :], dga, first)
        _acc_rows(dbg_ref.at[1:2, :], dgb, first)
        doa_ref[...] = _dot_nt(dpa, wa_ref[...]).astype(BF16)
        dob_ref[...] = _dot_nt(dpb, wb_ref[...]).astype(BF16)

    full = lambda shape: pl.BlockSpec(shape, lambda i: (0, 0))
    rowb = lambda n: pl.BlockSpec((tm, n), lambda i: (i, 0))
    sds = lambda n, dt: jax.ShapeDtypeStruct((s_len, n), dt)
    return pl.pallas_call(
        body, name="merge_bwd", grid=(s_len // tm,),
        in_specs=[rowb(1024), rowb(512), rowb(512), rowb(2048), full((2, 1024)), full((512, 1024)),
                  full((512, 1024)), full((1024, 1024))],
        out_specs=[rowb(512), rowb(512), rowb(2048), rowb(1024), rowb(1024), rowb(1024), rowb(1024),
                   full((2, 1024))],
        out_shape=[sds(512, BF16), sds(512, BF16), sds(2048, BF16), sds(1024, BF16), sds(1024, BF16),
                   sds(1024, BF16), sds(1024, BF16), jax.ShapeDtypeStruct((2, 1024), F32)],
        compiler_params=_cparams(48, dimension_semantics=("arbitrary",)),
    )(dx1, oa, ob, gates, bg, wa, wb, wo)


def _mem_kv(mem, g, w):
    m_len = mem.shape[0]

    def body(mem_ref, g_ref, w_ref, mn_ref, kv_ref):
        mn, _, _ = _rms(mem_ref[...], g_ref[...])
        mnb = mn.astype(BF16)
        mn_ref[...] = mnb
        kv_ref[...] = _dot(mnb, w_ref[...]).astype(BF16)

    return pl.pallas_call(
        body, name="mem_kv",
        out_shape=[jax.ShapeDtypeStruct((m_len, D_MODEL), BF16), jax.ShapeDtypeStruct((m_len, 1024), BF16)],
    )(mem, g, w)


def _mem_bwd(mem, g, w, mn, dkv):
    def body(mem_ref, g_ref, w_ref, mn_ref, dkv_ref, dw_ref, dg_ref):
        dkvb = dkv_ref[...].astype(BF16)
        dw_ref[...] = _dot_tn(mn_ref[...], dkvb)
        dmn = _dot_nt(dkvb, w_ref[...])
        _, xh, _ = _rms(mem_ref[...], g_ref[...])
        dg_ref[...] = jnp.sum(dmn * xh, axis=0, keepdims=True)

    return pl.pallas_call(
        body, name="mem_bwd",
        out_shape=[jax.ShapeDtypeStruct((D_MODEL, 1024), F32), jax.ShapeDtypeStruct((1, D_MODEL), F32)],
    )(mem, g, w, mn, dkv)


def _xattn_heads(xqb, kv_ref, m_len):
    ps = []
    for h in range(X_HEADS):
        hs = slice(h * X_HEAD_DIM, (h + 1) * X_HEAD_DIM)
        s = _dot_nt(xqb[:, hs], kv_ref[:, hs]) * X_SCALE
        e = jnp.exp(s - jnp.max(s, axis=-1, keepdims=True))
        ps.append(e / jnp.sum(e, axis=-1, keepdims=True))
    return ps


def _xattn_fwd(x1, g, wxq, kv, wxo):
    s_len, m_len = x1.shape[0], kv.shape[0]
    tm = _row_block(s_len)

    def body(x_ref, g_ref, wq_ref, kv_ref, wo_ref, y_ref):
        hx, _, _ = _rms(x_ref[...], g_ref[...])
        xqb = _dot(hx.astype(BF16), wq_ref[...]).astype(BF16)
        ps = _xattn_heads(xqb, kv_ref, m_len)
        xo = jnp.concatenate(
            [_dot(ps[h].astype(BF16), kv_ref[:, 512 + h * X_HEAD_DIM:512 + (h + 1) * X_HEAD_DIM])
             for h in range(X_HEADS)], axis=-1)
        y_ref[...] = x_ref[...] + _dot(xo.astype(BF16), wo_ref[...])

    full = lambda shape: pl.BlockSpec(shape, lambda i: (0, 0))
    rowb = lambda n: pl.BlockSpec((tm, n), lambda i: (i, 0))
    return pl.pallas_call(
        body, name="xattn_fwd", grid=(s_len // tm,),
        in_specs=[rowb(1024), full((1, 1024)), full((1024, 512)), full((m_len, 1024)), full((512, 1024))],
        out_specs=rowb(1024),
        out_shape=jax.ShapeDtypeStruct((s_len, D_MODEL), F32),
        compiler_params=_cparams(48, dimension_semantics=("parallel",)),
    )(x1, g, wxq, kv, wxo)


def _xattn_bwd(x1, dx2, g, wxq, kv, wxo):
    s_len, m_len = x1.shape[0], kv.shape[0]
    tm = _row_block(s_len)

    def body(x_ref, dy_ref, g_ref, wq_ref, kv_ref, wo_ref, dx_ref, dwq_ref, dwo_ref, dkv_ref, dg_ref):
        first = pl.program_id(0) == 0
        gv = g_ref[...]
        hx, xh, r = _rms(x_ref[...], gv)
        hxb = hx.astype(BF16)
        xqb = _dot(hxb, wq_ref[...]).astype(BF16)
        ps = _xattn_heads(xqb, kv_ref, m_len)
        dy = dy_ref[...]
        dyb = dy.astype(BF16)
        dxo = _dot_nt(dyb, wo_ref[...])
        xos, dqs, dks, dvs = [], [], [], []
        for h in range(X_HEADS):
            hs = slice(h * X_HEAD_DIM, (h + 1) * X_HEAD_DIM)
            vs = slice(512 + h * X_HEAD_DIM, 512 + (h + 1) * X_HEAD_DIM)
            p = ps[h]
            pb = p.astype(BF16)
            dxoh = dxo[:, hs].astype(BF16)
            xos.append(_dot(pb, kv_ref[:, vs]))
            dp = _dot_nt(dxoh, kv_ref[:, vs])
            ds = (p * (dp - jnp.sum(dp * p, axis=-1, keepdims=True)) * X_SCALE).astype(BF16)
            dvs.append(_dot_tn(pb, dxoh))
            dks.append(_dot_tn(ds, xqb[:, hs]))
            dqs.append(_dot(ds, kv_ref[:, hs]))
        xob = jnp.concatenate(xos, axis=-1).astype(BF16)
        dxqb = jnp.concatenate(dqs, axis=-1).astype(BF16)
        _acc(dwo_ref, _dot_tn(xob, dyb), first)
        _acc(dwq_ref, _dot_tn(hxb, dxqb), first)
        _acc(dkv_ref, jnp.concatenate(dks + dvs, axis=-1), first)
        dhx = _dot_nt(dxqb, wq_ref[...])
        dx, dgr = _rms_bwd(dhx, xh, r, gv)
        dx_ref[...] = dy + dx
        _acc_rows(dg_ref, dgr, first)

    full = lambda shape: pl.BlockSpec(shape, lambda i: (0, 0))
    rowb = lambda n: pl.BlockSpec((tm, n), lambda i: (i, 0))
    return pl.pallas_call(
        body, name="xattn_bwd", grid=(s_len // tm,),
        in_specs=[rowb(1024), rowb(1024), full((1, 1024)), full((1024, 512)), full((m_len, 1024)),
                  full((512, 1024))],
        out_specs=[rowb(1024), full((1024, 512)), full((512, 1024)), full((m_len, 1024)), full((1, 1024))],
        out_shape=[jax.ShapeDtypeStruct((s_len, D_MODEL), F32), jax.ShapeDtypeStruct((1024, 512), F32),
                   jax.ShapeDtypeStruct((512, 1024), F32), jax.ShapeDtypeStruct((m_len, 1024), F32),
                   jax.ShapeDtypeStruct((1, D_MODEL), F32)],
        compiler_params=_cparams(48, dimension_semantics=("arbitrary",)),
    )(x1, dx2, g, wxq, kv, wxo)


FF_TILE = 1408
FF_TILE_BWD = 256


def _ffn_fwd(x2, g, wg, wu, wd):
    s_len = x2.shape[0]
    tm, tf = _row_block(s_len), FF_TILE

    def body(x_ref, g_ref, wg_ref, wu_ref, wd_ref, y_ref, h_scr):
        j = pl.program_id(1)

        @pl.when(j == 0)
        def _():
            hf, _, _ = _rms(x_ref[...], g_ref[...])
            h_scr[...] = hf.astype(BF16)
            y_ref[...] = x_ref[...]

        hb = h_scr[...]
        gt = _dot(hb, wg_ref[...])
        up = _dot(hb, wu_ref[...])
        act = gt * _sigmoid(gt) * up
        y_ref[...] += _dot(act.astype(BF16), wd_ref[...])

    return pl.pallas_call(
        body, name="ffn_fwd", grid=(s_len // tm, D_FF // tf),
        in_specs=[pl.BlockSpec((tm, D_MODEL), lambda i, j: (i, 0)),
                  pl.BlockSpec((1, D_MODEL), lambda i, j: (0, 0)),
                  pl.BlockSpec((D_MODEL, tf), lambda i, j: (0, j)),
                  pl.BlockSpec((D_MODEL, tf), lambda i, j: (0, j)),
                  pl.BlockSpec((tf, D_MODEL), lambda i, j: (j, 0))],
        out_specs=pl.BlockSpec((tm, D_MODEL), lambda i, j: (i, 0)),
        out_shape=jax.ShapeDtypeStruct((s_len, D_MODEL), F32),
        scratch_shapes=[pltpu.VMEM((tm, D_MODEL), BF16)],
        compiler_params=_cparams(48, dimension_semantics=("parallel", "arbitrary")),
    )(x2, g, wg, wu, wd)


def _ffn_bwd(x2, dx3, g, wg, wu, wd):
    s_len = x2.shape[0]
    tm, tf = _row_block(s_len), FF_TILE_BWD
    nf = D_FF // tf

    def body(x_ref, dy_ref, g_ref, wg_ref, wu_ref, wd_ref,
             dx_ref, h_ref, dgt_ref, dup_ref, act_ref, dg_ref, h_scr, dyb_scr, dh_scr):
        i, j = pl.program_id(0), pl.program_id(1)

        @pl.when(j == 0)
        def _():
            hf, _, _ = _rms(x_ref[...], g_ref[...])
            hb = hf.astype(BF16)
            h_scr[...] = hb
            h_ref[...] = hb
            dyb_scr[...] = dy_ref[...].astype(BF16)
            dh_scr[...] = jnp.zeros_like(dh_scr)

        hb = h_scr[...]
        gt = _dot(hb, wg_ref[...])
        up = _dot(hb, wu_ref[...])
        sg = _sigmoid(gt)
        silu = gt * sg
        dact = _dot_nt(dyb_scr[...], wd_ref[...])
        dgt = (dact * up * (sg * (1.0 + gt * (1.0 - sg)))).astype(BF16)
        dup = (dact * silu).astype(BF16)
        dgt_ref[...] = dgt
        dup_ref[...] = dup
        act_ref[...] = (silu * up).astype(BF16)
        dh_scr[...] += _dot_nt(dgt, wg_ref[...]) + _dot_nt(dup, wu_ref[...])

        @pl.when(j == nf - 1)
        def _():
            gv = g_ref[...]
            _, xh, r = _rms(x_ref[...], gv)
            dx, dgr = _rms_bwd(dh_scr[...], xh, r, gv)
            dx_ref[...] = dy_ref[...] + dx
            _acc_rows(dg_ref, dgr, i == 0)

    rowb = pl.BlockSpec((tm, D_MODEL), lambda i, j: (i, 0))
    ffb = pl.BlockSpec((tm, tf), lambda i, j: (i, j))
    return pl.pallas_call(
        body, name="ffn_bwd", grid=(s_len // tm, nf),
        in_specs=[rowb, rowb, pl.BlockSpec((1, D_MODEL), lambda i, j: (0, 0)),
                  pl.BlockSpec((D_MODEL, tf), lambda i, j: (0, j)),
                  pl.BlockSpec((D_MODEL, tf), lambda i, j: (0, j)),
                  pl.BlockSpec((tf, D_MODEL), lambda i, j: (j, 0))],
        out_specs=[rowb, rowb, ffb, ffb, ffb, pl.BlockSpec((1, D_MODEL), lambda i, j: (0, 0))],
        out_shape=[jax.ShapeDtypeStruct((s_len, D_MODEL), F32), jax.ShapeDtypeStruct((s_len, D_MODEL), BF16),
                   jax.ShapeDtypeStruct((s_len, D_FF), BF16), jax.ShapeDtypeStruct((s_len, D_FF), BF16),
                   jax.ShapeDtypeStruct((s_len, D_FF), BF16), jax.ShapeDtypeStruct((1, D_MODEL), F32)],
        scratch_shapes=[pltpu.VMEM((tm, D_MODEL), BF16), pltpu.VMEM((tm, D_MODEL), BF16),
                        pltpu.VMEM((tm, D_MODEL), F32)],
        compiler_params=_cparams(56, dimension_semantics=("arbitrary", "arbitrary")),
    )(x2, dx3, g, wg, wu, wd)


def _loss_head(x3, g, target):
    s_len = x3.shape[0]
    tm = _row_block(s_len)

    def body(x_ref, g_ref, t_ref, sse_ref, dx_ref, dxb_ref, dg_ref):
        first = pl.program_id(0) == 0
        gv = g_ref[...]
        y, xh, r = _rms(x_ref[...], gv)
        err = y - t_ref[...]
        _acc(sse_ref, jnp.broadcast_to(jnp.sum(err * err), (8, LANES)), first)
        dx, dgr = _rms_bwd(err * (1.0 / D_MODEL), xh, r, gv)
        dx_ref[...] = dx
        dxb_ref[...] = dx.astype(BF16)
        _acc_rows(dg_ref, dgr, first)

    rowb = pl.BlockSpec((tm, D_MODEL), lambda i: (i, 0))
    return pl.pallas_call(
        body, name="loss_head", grid=(s_len // tm,),
        in_specs=[rowb, pl.BlockSpec((1, D_MODEL), lambda i: (0, 0)), rowb],
        out_specs=[pl.BlockSpec((8, LANES), lambda i: (0, 0)), rowb, rowb,
                   pl.BlockSpec((1, D_MODEL), lambda i: (0, 0))],
        out_shape=[jax.ShapeDtypeStruct((8, LANES), F32), jax.ShapeDtypeStruct((s_len, D_MODEL), F32),
                   jax.ShapeDtypeStruct((s_len, D_MODEL), BF16), jax.ShapeDtypeStruct((1, D_MODEL), F32)],
        compiler_params=_cparams(dimension_semantics=("arbitrary",)),
    )(x3, g, target)


def _mla_prep_bwd(lat, g_q, g_kv, w_uq, w_uk, w_uv, cosf, sinf, dq, dk, dv):
    s_len = lat.shape[0]
    tm = _row_block(s_len)

    def body(lat_ref, gq_ref, gkv_ref, wuq_ref, wuk_ref, wuv_ref, cos_ref, sin_ref, dq_ref, dk_ref, dv_ref,
             dlat_ref, dqb_ref, dkb_ref, dvb_ref, dgq_ref, dgkv_ref):
        first = pl.program_id(0) == 0
        lane = lax.broadcasted_iota(jnp.int32, (tm, LANES), 1)
        cosv, sinv = cos_ref[...], sin_ref[...]
        gq, gkv = gq_ref[...], gkv_ref[...]
        _, qxh, qr = _rms(lat_ref[:, 0:256], gq)
        _, kxh, kr_ = _rms(lat_ref[:, 256:384], gkv)
        dkr = jnp.zeros((tm, LANES), F32)
        for h in range(MLA_HEADS):
            sl = slice(h * HEAD_PAD, (h + 1) * HEAD_PAD)
            blk = dq_ref[:, sl]
            dqb_ref[:, sl] = (blk * cosv + _rope_rot_t(blk, lane) * sinv).astype(BF16)
            kblk = dk_ref[:, sl] * (1.0 / MLA_Q_FOLD)
            dkb_ref[:, sl] = kblk.astype(BF16)
            dkr = dkr + kblk
        dvb = dv_ref[...].astype(BF16)
        dvb_ref[...] = dvb
        dkr = jnp.where((lane >= 64) & (lane < 96), dkr, 0.0)
        dkr = dkr * cosv + _rope_rot_t(dkr, lane) * sinv
        dql = _dot_nt(dqb_ref[...], wuq_ref[...])
        dkvl = _dot_nt(dkb_ref[...], wuk_ref[...]) + _dot_nt(dvb, wuv_ref[...])
        dcq, dgqr = _rms_bwd(dql, qxh, qr, gq)
        dckv, dgkvr = _rms_bwd(dkvl, kxh, kr_, gkv)
        dlat_ref[:, 0:256] = dcq
        dlat_ref[:, 256:384] = dckv
        dlat_ref[:, K_R_OFF:K_R_OFF + LANES] = pltpu.roll(dkr, 64, 1)
        _acc_rows(dgq_ref, dgqr, first)
        _acc_rows(dgkv_ref, dgkvr, first)

    full = lambda shape: pl.BlockSpec(shape, lambda i: (0, 0))
    rowb = lambda n: pl.BlockSpec((tm, n), lambda i: (i, 0))
    sds = lambda n, dt: jax.ShapeDtypeStruct((s_len, n), dt)
    return pl.pallas_call(
        body, name="mla_prep_bwd", grid=(s_len // tm,),
        in_specs=[rowb(512), full((1, 256)), full((1, 128)), full((256, 1024)), full((128, 1024)),
                  full((128, 512)), rowb(128), rowb(128), rowb(1024), rowb(1024), rowb(512)],
        out_specs=[rowb(512), rowb(1024), rowb(1024), rowb(512), full((1, 256)), full((1, 128))],
        out_shape=[sds(512, F32), sds(1024, BF16), sds(1024, BF16), sds(512, BF16),
                   jax.ShapeDtypeStruct((1, 256), F32), jax.ShapeDtypeStruct((1, 128), F32)],
        compiler_params=_cparams(48, dimension_semantics=("arbitrary",)),
    )(lat, g_q, g_kv, w_uq, w_uk, w_uv, cosf, sinf, dq, dk, dv)


def _in_proj_bwd(x, g, w, dx1, dlat, dsbq, dsbk, dsbv, dgates):
    s_len = x.shape[0]
    tm, tn = _row_block(s_len), 512
    nj = D_IN_PAD // tn

    def body(x_ref, g_ref, w_ref, dx1_ref, dlat_ref, dq_ref, dk_ref, dv_ref, dgate_ref,
             gx_ref, dproj_ref, dg_ref, dh_scr):
        i, j = pl.program_id(0), pl.program_id(1)

        @pl.when(j == 0)
        def _():
            dh_scr[...] = jnp.zeros_like(dh_scr)

        def chunk(val):
            vb = val.astype(BF16)
            dproj_ref[...] = vb
            dh_scr[...] += _dot_nt(vb, w_ref[...])

        for jj, ref in ((0, dlat_ref), (1, dq_ref), (3, dv_ref)):
            pl.when(j == jj)(functools.partial(lambda ref: chunk(ref[...]), ref))
        pl.when(j == 2)(lambda: chunk(dk_ref[...] * LN2))
        pl.when(j >= 4)(lambda: chunk(dgate_ref[...]))

        @pl.when(j == nj - 1)
        def _():
            gv = g_ref[...]
            _, xh, r = _rms(x_ref[...], gv)
            dx, dgr = _rms_bwd(dh_scr[...], xh, r, gv)
            gx_ref[...] = dx1_ref[...] + dx
            _acc_rows(dg_ref, dgr, i == 0)

    rowb = pl.BlockSpec((tm, D_MODEL), lambda i, j: (i, 0))
    colb = lambda f: pl.BlockSpec((tm, tn), f)
    return pl.pallas_call(
        body, name="in_proj_bwd", grid=(s_len // tm, nj),
        in_specs=[rowb, pl.BlockSpec((1, D_MODEL), lambda i, j: (0, 0)),
                  pl.BlockSpec((D_MODEL, tn), lambda i, j: (0, j)), rowb,
                  colb(lambda i, j: (i, 0)), colb(lambda i, j: (i, 0)), colb(lambda i, j: (i, 0)),
                  colb(lambda i, j: (i, 0)), colb(lambda i, j: (i, jnp.clip(j - 4, 0, 3)))],
        out_specs=[rowb, colb(lambda i, j: (i, j)), pl.BlockSpec((1, D_MODEL), lambda i, j: (0, 0))],
        out_shape=[jax.ShapeDtypeStruct((s_len, D_MODEL), F32), jax.ShapeDtypeStruct((s_len, D_IN_PAD), BF16),
                   jax.ShapeDtypeStruct((1, D_MODEL), F32)],
        scratch_shapes=[pltpu.VMEM((tm, D_MODEL), F32)],
        compiler_params=_cparams(48, dimension_semantics=("arbitrary", "arbitrary")),
    )(x, g, w, dx1, dlat, dsbq, dsbk, dsbv, dgates)


def _adamw(landed, w, m, v, name):
    r, c = w.shape
    lanes = _round_up(c, LANES)
    tb = r
    for cand in range(r, 0, -1):
        if r % cand == 0 and (cand % 8 == 0 or cand == r) and N_DEV * cand * lanes * 4 <= ADAM_BLOCK_BYTES:
            tb = cand
            break
    c1 = 1.0 - ADAM_B1 ** ADAM_STEP
    c2 = 1.0 - ADAM_B2 ** ADAM_STEP

    def body(l_ref, w_ref, m_ref, v_ref, g_ref, d_ref, nm_ref, nv_ref):
        g = l_ref[0]
        for k in range(1, N_DEV):
            g = g + l_ref[k]
        nm = ADAM_B1 * m_ref[...] + (1.0 - ADAM_B1) * g
        nv = ADAM_B2 * v_ref[...] + (1.0 - ADAM_B2) * (g * g)
        g_ref[...] = g
        nm_ref[...] = nm
        nv_ref[...] = nv
        d_ref[...] = -ADAM_LR * ((nm / c1) / (jnp.sqrt(nv / c2) + ADAM_EPS) + ADAM_WD * w_ref[...])

    blk = pl.BlockSpec((tb, c), lambda i: (i, 0))
    return pl.pallas_call(
        body, name=name, grid=(r // tb,),
        in_specs=[pl.BlockSpec((N_DEV, tb, c), lambda i: (0, i, 0)), blk, blk, blk],
        out_specs=[blk, blk, blk, blk],
        out_shape=[jax.ShapeDtypeStruct((r, c), F32)] * 4,
        compiler_params=_cparams(dimension_semantics=("parallel",)),
    )(landed, w, m, v)


def _shard_shape(shape, axis):
    return tuple(d // N_DEV if a == axis else d for a, d in enumerate(shape))


def _split_pieces(full, axis):
    r, c = full.shape
    if axis == 0:
        return full.reshape(N_DEV, r // N_DEV, c)
    return full.reshape(r, N_DEV, c // N_DEV).transpose(1, 0, 2)


def _join_shards(gathered, axis):
    _, r, c = gathered.shape
    if axis == 0:
        return gathered.reshape(N_DEV * r, c)
    return gathered.transpose(1, 0, 2).reshape(r, N_DEV * c)


def kernel(x, mem, positions, g_mix, w_in, b_gate, g_q_lat, w_uq, g_kv_lat, w_ukv, w_a_proj, w_b_proj, w_o, g_x, g_mem, w_xq, w_xkv, w_xo, g_ffn, w_gate, w_up, w_down, g_final, loss_target, m_g_mix, m_w_in, m_b_gate, m_g_q_lat, m_w_uq, m_g_kv_lat, m_w_ukv, m_w_a_proj, m_w_b_proj, m_w_o, m_g_x, m_g_mem, m_w_xq, m_w_xkv, m_w_xo, m_g_ffn, m_w_gate, m_w_up, m_w_down, m_g_final, v_g_mix, v_w_in, v_b_gate, v_g_q_lat, v_w_uq, v_g_kv_lat, v_w_ukv, v_w_a_proj, v_w_b_proj, v_w_o, v_g_x, v_g_mem, v_w_xq, v_w_xkv, v_w_xo, v_g_ffn, v_w_gate, v_w_up, v_w_down, v_g_final):
    given = dict(locals())
    s_len = x.shape[1]
    x2d = x.reshape(s_len, D_MODEL)
    mem2d = mem.reshape(-1, D_MODEL)
    target = loss_target.reshape(s_len, D_MODEL)

    names = [name for name, _, _ in SHARDED]
    axis_of = {name: axis for name, _, axis in SHARDED}
    shard2d = lambda name, prefix="": given[prefix + name].reshape(
        _shard_shape(dict((n, s) for n, s, _ in SHARDED)[name], axis_of[name]))

    wire = lambda name: shard2d(name) if name == "b_gate" else shard2d(name).astype(BF16)
    gathered = _exchange(True, [wire(n) for n in names], "weights_all_gather")
    wts = {n: _join_shards(g, axis_of[n]) for n, g in zip(names, gathered)}

    w_in_p = jnp.concatenate([wts["w_in"][:, :416], jnp.zeros((D_MODEL, 96), BF16), wts["w_in"][:, 416:]], axis=1)
    w_uq_p = jnp.pad(wts["w_uq"].reshape(256, MLA_HEADS, 96), ((0, 0), (0, 0), (0, 32))).reshape(256, 1024)
    ukv = wts["w_ukv"].reshape(128, MLA_HEADS, 128)
    w_uk_p = jnp.pad(ukv[:, :, :64], ((0, 0), (0, 0), (0, 64))).reshape(128, 1024)
    w_uv = ukv[:, :, 64:].reshape(128, 512)
    bg = wts["b_gate"]

    inv_freq = ROPE_THETA ** (-jnp.arange(0, MLA_ROPE, 2, dtype=F32) / MLA_ROPE)
    ang = positions.reshape(s_len).astype(F32)[:, None] * inv_freq
    cos16, sin16 = jnp.cos(ang), jnp.sin(ang)
    cosf = jnp.concatenate([jnp.ones((s_len, 64), F32), cos16, cos16, jnp.ones((s_len, 32), F32)], axis=1)
    sinf = jnp.concatenate([jnp.zeros((s_len, 64), F32), sin16, sin16, jnp.zeros((s_len, 32), F32)], axis=1)

    h1, lat, sb, gates = _in_proj(x2d, g_mix, w_in_p)
    qa, ka, va, q_lat, kv_lat = _mla_prep(lat, g_q_lat, g_kv_lat, w_uq_p, w_uk_p, w_uv, cosf, sinf)
    oa, lse = _mla_fwd(qa, ka, va)
    ob, sb_r = _sb_fwd(sb)
    x1 = _merge_fwd(x2d, oa, ob, gates, bg, wts["w_a_proj"], wts["w_b_proj"], wts["w_o"])
    mn, xkv = _mem_kv(mem2d, g_mem, wts["w_xkv"])
    x2 = _xattn_fwd(x1, g_x, wts["w_xq"], xkv, wts["w_xo"])
    x3 = _ffn_fwd(x2, g_ffn, wts["w_gate"], wts["w_up"], wts["w_down"])
    g_final2d = g_final.reshape(1, D_MODEL)
    sse, dx3, dx3b, dg_final = _loss_head(x3, g_final2d, target)
    loss = lax.psum(sse[0, 0] * (0.5 / D_MODEL), ("x", "y", "c"))

    dx2, hf, dgt, dup, act, dg_ffn = _ffn_bwd(x2, dx3, g_ffn, wts["w_gate"], wts["w_up"], wts["w_down"])
    dx1, dw_xq, dw_xo, dxkv, dg_x = _xattn_bwd(x1, dx2, g_x, wts["w_xq"], xkv, wts["w_xo"])
    dw_xkv, dg_mem = _mem_bwd(mem2d, g_mem, wts["w_xkv"], mn, dxkv)
    doa, dob, dgates, dpa, dpb, merged, dx1b, dbg = _merge_bwd(
        dx1, oa, ob, gates, bg, wts["w_a_proj"], wts["w_b_proj"], wts["w_o"])
    dsbq, dsbk, dsbv = _sb_bwd(sb, dob, sb_r)
    dqa, dka, dva = _mla_bwd(qa, ka, va, oa, doa, lse)
    dlat, dqb, dkb, dvb, dg_q, dg_kv = _mla_prep_bwd(
        lat, g_q_lat, g_kv_lat, w_uq_p, w_uk_p, w_uv, cosf, sinf, dqa, dka, dva)
    grad_x, dproj, dg_mix = _in_proj_bwd(x2d, g_mix, w_in_p, dx1, dlat, dsbq, dsbk, dsbv, dgates)
    dw_in_p = _tn_matmul(h1, dproj, "dw_in")
    dw_uq_p = _tn_matmul(q_lat, dqb, "dw_uq")
    dw_uk_p = _tn_matmul(kv_lat, dkb, "dw_uk")
    dw_uv = _tn_matmul(kv_lat, dvb, "dw_uv")
    full_grads = {
        "w_in": jnp.concatenate([dw_in_p[:, :416], dw_in_p[:, 512:]], axis=1),
        "b_gate": dbg,
        "w_uq": dw_uq_p.reshape(256, MLA_HEADS, 128)[:, :, :96].reshape(256, 768),
        "w_ukv": jnp.concatenate([dw_uk_p.reshape(128, MLA_HEADS, 128)[:, :, :64],
                                  dw_uv.reshape(128, MLA_HEADS, 64)], axis=2).reshape(128, 1024),
        "w_a_proj": _tn_matmul(oa, dpa, "dw_a"),
        "w_b_proj": _tn_matmul(ob, dpb, "dw_b"),
        "w_o": _tn_matmul(merged, dx1b, "dw_o"),
        "w_xq": dw_xq,
        "w_xkv": dw_xkv,
        "w_xo": dw_xo,
        "w_gate": _tn_matmul(hf, dgt, "dw_gate", tn=FF_TILE),
        "w_up": _tn_matmul(hf, dup, "dw_up", tn=FF_TILE),
        "w_down": _tn_matmul(act, dx3b, "dw_down", tka=FF_TILE),
    }
    rep_grads = {"g_mix": dg_mix, "g_q_lat": dg_q, "g_kv_lat": dg_kv, "g_x": dg_x, "g_mem": dg_mem,
                 "g_ffn": dg_ffn, "g_final": dg_final}
    rep_cat = lambda prefix, src: jnp.concatenate(
        [src[prefix + n].reshape(-1) for n, _ in REPLICATED]).reshape(-1, LANES)
    rep_src = jnp.broadcast_to(rep_cat("", rep_grads), (N_DEV,) + rep_cat("", rep_grads).shape)

    got = _exchange(False, [_split_pieces(full_grads[n], axis_of[n]) for n in names] + [rep_src],
                    "grads_all_to_all")
    landed = dict(zip(names, got[:-1]))
    rep_landed = got[-1]

    res = {}
    for name, _, _ in SHARDED:
        outs = _adamw(landed[name], shard2d(name), shard2d(name, "m_"), shard2d(name, "v_"), "adamw_" + name)
        res[name] = [o.reshape(given[name].shape) for o in outs]
    rep_outs = _adamw(rep_landed, rep_cat("", given), rep_cat("m_", given), rep_cat("v_", given), "adamw_gains")
    off = 0
    for name, n in REPLICATED:
        res[name] = [o.reshape(-1)[off:off + n].reshape(given[name].shape) for o in rep_outs]
        off += n
    result = [loss, grad_x.reshape(x.shape)]
    for k in range(4):
        result.extend(res[name][k] for name in WEIGHT_ORDER)
    return tuple(result)
```

```python
import functools
import math

import jax
import jax.numpy as jnp
from jax import lax
from jax.experimental import pallas as pl
from jax.experimental.pallas import tpu as pltpu

F32 = jnp.float32
BF16 = jnp.bfloat16

D_MODEL = 1024
MLA_HEADS = 8
MLA_Q_RANK = 256
MLA_KV_RANK = 128
MLA_NOPE = 64
MLA_ROPE = 32
MLA_V = 64
ROPE_THETA = 10000.0
SB_WIDTH = 512
X_HEADS = 4
X_HEAD_DIM = 128
D_FF = 2816
EPS = 1e-6
D_IN = 4000
D_IN_PAD = 4096
K_R_OFF = 384
LANES = 128
HEAD_PAD = 128
MLA_SCALE = 1.0 / math.sqrt(MLA_NOPE + MLA_ROPE)
SB_SCALE = 0.125
LOG2E = math.log2(math.e)
LN2 = math.log(2.0)
MLA_Q_FOLD = MLA_SCALE * LOG2E
SB_Q_FOLD = SB_SCALE * LOG2E
SB_CUT = -160.0
X_SCALE = 1.0 / math.sqrt(X_HEAD_DIM)
NEG_BIG = -1e30

ADAM_LR = 0.001
ADAM_B1 = 0.9
ADAM_B2 = 0.999
ADAM_EPS = 1e-08
ADAM_WD = 0.01
ADAM_STEP = 10

N_DEV = 8
MIB = 1024 * 1024
ADAM_BLOCK_BYTES = 4 * MIB

SHARDED = (
    ("w_in", (D_MODEL, D_IN), 1),
    ("b_gate", (2, D_MODEL), 1),
    ("w_uq", (MLA_Q_RANK, 768), 1),
    ("w_ukv", (MLA_KV_RANK, 1024), 1),
    ("w_a_proj", (512, D_MODEL), 1),
    ("w_b_proj", (512, D_MODEL), 1),
    ("w_o", (D_MODEL, D_MODEL), 0),
    ("w_xq", (D_MODEL, 512), 0),
    ("w_xkv", (D_MODEL, 1024), 0),
    ("w_xo", (512, D_MODEL), 1),
    ("w_gate", (D_MODEL, D_FF), 1),
    ("w_up", (D_MODEL, D_FF), 1),
    ("w_down", (D_FF, D_MODEL), 0),
)
NEEDED_FIRST = ("w_in", "b_gate", "w_uq", "w_ukv")
REPLICATED = (
    ("g_mix", 1024), ("g_q_lat", 256), ("g_kv_lat", 128), ("g_x", 1024),
    ("g_mem", 1024), ("g_ffn", 1024), ("g_final", 1024),
)
WEIGHT_ORDER = ("g_mix", "w_in", "b_gate", "g_q_lat", "w_uq", "g_kv_lat", "w_ukv", "w_a_proj",
                "w_b_proj", "w_o", "g_x", "g_mem", "w_xq", "w_xkv", "w_xo", "g_ffn", "w_gate",
                "w_up", "w_down", "g_final")


def _round_up(n, m):
    return -(-n // m) * m


def _cparams(vmem_mib=None, **kw):
    if vmem_mib is not None:
        kw["vmem_limit_bytes"] = vmem_mib * MIB
    return pltpu.CompilerParams(**kw)


def _dot(a, b):
    return jnp.dot(a, b, preferred_element_type=F32)


def _dot_nt(a, b):
    return lax.dot_general(a, b, (((1,), (1,)), ((), ())), preferred_element_type=F32)


def _dot_tn(a, b):
    return lax.dot_general(a, b, (((0,), (0,)), ((), ())), preferred_element_type=F32)


def _rms(x, g):
    r = lax.rsqrt(jnp.mean(x * x, axis=-1, keepdims=True) + EPS)
    xh = x * r
    return xh * g, xh, r


def _rms_bwd(dy, xh, r, g):
    u = dy * g
    dx = r * (u - xh * jnp.mean(u * xh, axis=-1, keepdims=True))
    return dx, dy * xh


def _sigmoid(z):
    return 1.0 / (1.0 + jnp.exp(-z))


def _acc_rows(ref, val, first):
    s = jnp.sum(val, axis=0, keepdims=True)

    @pl.when(first)
    def _():
        ref[...] = s

    @pl.when(jnp.logical_not(first))
    def _():
        ref[...] += s


def _acc(ref, val, first):
    @pl.when(first)
    def _():
        ref[...] = val

    @pl.when(jnp.logical_not(first))
    def _():
        ref[...] += val


def _peer(k):
    x, y, c = lax.axis_index("x"), lax.axis_index("y"), lax.axis_index("c")
    px = 1 - x if (k >> 2) & 1 else x
    py = 1 - y if (k >> 1) & 1 else y
    pc = 1 - c if k & 1 else c
    return (px, py, pc), 4 * px + 2 * py + pc


N_PEERS = N_DEV - 1
OTHER_CHIPS = (2, 4, 6)


def _land_shape(gather, src):
    return (N_DEV,) + src.shape if gather else src.shape


class _Exchange:
    def __init__(self, gather, srcs):
        self.gather, self.n, self.srcs = gather, len(srcs), list(srcs)
        self.out_shape = [jax.ShapeDtypeStruct(_land_shape(gather, s), s.dtype) for s in srcs]
        self.specs = [pl.BlockSpec(memory_space=pl.ANY)] * self.n
        self.scratch = [pltpu.SemaphoreType.DMA((self.n * N_PEERS,)), pltpu.SemaphoreType.DMA((self.n * N_PEERS,)),
                        pltpu.SemaphoreType.DMA((self.n,))]

    def bind(self, src, land, sems):
        self.src, self.land = src, land
        self.send_sems, self.recv_sems, self.local_sems = sems

    def _copy(self, a, k, source, to, target=1):
        return pltpu.make_async_remote_copy(
            src_ref=source, dst_ref=to,
            send_sem=self.send_sems.at[a * N_PEERS + k - 1], recv_sem=self.recv_sems.at[a * N_PEERS + k - 1],
            device_id=_peer(target)[0], device_id_type=pl.DeviceIdType.MESH)

    def _row(self, a, k):
        return self.land[a].at[_peer(k)[1]]

    def _mine(self, a):
        me = _peer(0)[1]
        return pltpu.make_async_copy(self.src[a] if self.gather else self.src[a].at[me], self.land[a].at[me],
                                     self.local_sems.at[a])

    def issue(self):
        me = _peer(0)[1]
        for a in range(self.n):
            self._mine(a).start()
            for k in ((1,) + OTHER_CHIPS if self.gather else range(1, N_DEV)):
                source = self.src[a] if self.gather else self.src[a].at[_peer(k)[1]]
                self._copy(a, k, source, self.land[a].at[me], target=k).start()

    def finish(self):
        me = _peer(0)[1]
        part = lambda a: self.src[a] if self.gather else self.src[a].at[me]
        if self.gather:
            for a in range(self.n):
                for k in OTHER_CHIPS:
                    self._copy(a, k, part(a), self._row(a, k)).wait_recv()
                    self._copy(a, k + 1, self._row(a, k), self._row(a, k), target=1).start()
        for a in range(self.n):
            for k in ((1, 3, 5, 7) if self.gather else range(1, N_DEV)):
                self._copy(a, k, part(a), self._row(a, k)).wait_recv()
        for a in range(self.n):
            for k in range(1, N_DEV):
                self._copy(a, k, part(a), self.land[a].at[me]).wait_send()
            self._mine(a).wait()


def _exchange(gather, srcs, name):
    ex = _Exchange(gather, srcs)

    def body(*refs):
        ex.bind(refs[:ex.n], refs[ex.n:2 * ex.n], refs[2 * ex.n:])
        ex.issue()
        ex.finish()

    return pl.pallas_call(body, name=name, out_shape=ex.out_shape, in_specs=ex.specs, out_specs=ex.specs,
                          scratch_shapes=ex.scratch)(*ex.srcs)


def _tn_matmul(a, b, name, tka=512, tn=1024, ts=512):
    s_len, ka = a.shape
    n = b.shape[1]
    tka, tn, ts = min(tka, ka), min(tn, n), min(ts, s_len)
    assert ka % tka == 0 and n % tn == 0 and s_len % ts == 0

    def body(a_ref, b_ref, o_ref):
        _acc(o_ref, _dot_tn(a_ref[...], b_ref[...]), pl.program_id(2) == 0)

    return pl.pallas_call(
        body, name=name, grid=(ka // tka, n // tn, s_len // ts),
        in_specs=[pl.BlockSpec((ts, tka), lambda i, j, s: (s, i)),
                  pl.BlockSpec((ts, tn), lambda i, j, s: (s, j))],
        out_specs=pl.BlockSpec((tka, tn), lambda i, j, s: (i, j)),
        out_shape=jax.ShapeDtypeStruct((ka, n), F32),
        compiler_params=_cparams(dimension_semantics=("parallel", "parallel", "arbitrary")),
    )(a, b)


def _row_block(s_len):
    return min(s_len, 512)


def _in_proj(x, g, w):
    s_len = x.shape[0]
    tm, tn = _row_block(s_len), 512

    def body(x_ref, g_ref, w_ref, h_ref, lat_ref, sb_ref, gate_ref, h_scr):
        j = pl.program_id(1)

        @pl.when(j == 0)
        def _():
            h, _, _ = _rms(x_ref[...], g_ref[...])
            hb = h.astype(BF16)
            h_scr[...] = hb
            h_ref[...] = hb

        p = _dot(h_scr[...], w_ref[...])

        @pl.when(j == 0)
        def _():
            lat_ref[...] = p

        @pl.when(j == 1)
        def _():
            sb_ref[...] = (p * SB_Q_FOLD).astype(BF16)

        @pl.when((j == 2) | (j == 3))
        def _():
            sb_ref[...] = p.astype(BF16)

        @pl.when(j >= 4)
        def _():
            gate_ref[...] = p

    return pl.pallas_call(
        body, name="in_proj", grid=(s_len // tm, D_IN_PAD // tn),
        in_specs=[pl.BlockSpec((tm, D_MODEL), lambda i, j: (i, 0)),
                  pl.BlockSpec((1, D_MODEL), lambda i, j: (0, 0)),
                  pl.BlockSpec((D_MODEL, tn), lambda i, j: (0, j))],
        out_specs=[pl.BlockSpec((tm, D_MODEL), lambda i, j: (i, 0)),
                   pl.BlockSpec((tm, tn), lambda i, j: (i, 0)),
                   pl.BlockSpec((tm, tn), lambda i, j: (i, jnp.clip(j - 1, 0, 2))),
                   pl.BlockSpec((tm, tn), lambda i, j: (i, jnp.clip(j - 4, 0, 3)))],
        out_shape=[jax.ShapeDtypeStruct((s_len, D_MODEL), BF16),
                   jax.ShapeDtypeStruct((s_len, 512), F32),
                   jax.ShapeDtypeStruct((s_len, 3 * SB_WIDTH), BF16),
                   jax.ShapeDtypeStruct((s_len, 2 * D_MODEL), F32)],
        scratch_shapes=[pltpu.VMEM((tm, D_MODEL), BF16)],
        compiler_params=_cparams(dimension_semantics=("parallel", "arbitrary")),
    )(x, g, w)


def _rope_rot(blk, lane):
    return jnp.where(lane < 80, -pltpu.roll(blk, 112, 1), pltpu.roll(blk, 16, 1))


def _rope_rot_t(blk, lane):
    return jnp.where(lane < 80, pltpu.roll(blk, 112, 1), -pltpu.roll(blk, 16, 1))


def _mla_prep(lat, g_q, g_kv, w_uq, w_uk, w_uv, cosf, sinf):
    s_len = lat.shape[0]
    tm = _row_block(s_len)

    def body(lat_ref, gq_ref, gkv_ref, wuq_ref, wuk_ref, wuv_ref, cos_ref, sin_ref,
             q_ref, k_ref, v_ref, ql_ref, kvl_ref):
        lane = lax.broadcasted_iota(jnp.int32, (tm, LANES), 1)
        cosv, sinv = cos_ref[...], sin_ref[...]
        ql, _, _ = _rms(lat_ref[:, 0:256], gq_ref[...])
        kvl, _, _ = _rms(lat_ref[:, 256:384], gkv_ref[...])
        qlb, kvlb = ql.astype(BF16), kvl.astype(BF16)
        ql_ref[...] = qlb
        kvl_ref[...] = kvlb
        q = _dot(qlb, wuq_ref[...])
        kn = _dot(kvlb, wuk_ref[...])
        v_ref[...] = _dot(kvlb, wuv_ref[...]).astype(BF16)
        kr = pltpu.roll(lat_ref[:, K_R_OFF:K_R_OFF + LANES], 64, 1)
        kr = kr * cosv + _rope_rot(kr, lane) * sinv
        for h in range(MLA_HEADS):
            sl = slice(h * HEAD_PAD, (h + 1) * HEAD_PAD)
            blk = q[:, sl]
            q_ref[:, sl] = ((blk * cosv + _rope_rot(blk, lane) * sinv) * MLA_Q_FOLD).astype(BF16)
            k_ref[:, sl] = (kn[:, sl] + kr).astype(BF16)

    full = lambda shape: pl.BlockSpec(shape, lambda i: (0, 0))
    rowb = lambda n: pl.BlockSpec((tm, n), lambda i: (i, 0))
    return pl.pallas_call(
        body, name="mla_prep", grid=(s_len // tm,),
        in_specs=[rowb(512), full((1, 256)), full((1, 128)), full((256, 1024)), full((128, 1024)),
                  full((128, 512)), rowb(128), rowb(128)],
        out_specs=[rowb(1024), rowb(1024), rowb(512), rowb(256), rowb(128)],
        out_shape=[jax.ShapeDtypeStruct((s_len, 1024), BF16), jax.ShapeDtypeStruct((s_len, 1024), BF16),
                   jax.ShapeDtypeStruct((s_len, 512), BF16), jax.ShapeDtypeStruct((s_len, 256), BF16),
                   jax.ShapeDtypeStruct((s_len, 128), BF16)],
        compiler_params=_cparams(dimension_semantics=("parallel",)),
    )(lat, g_q, g_kv, w_uq, w_uk, w_uv, cosf, sinf)


ATTN_TQ = 1024
ATTN_TH = 512
ATTN_TK = 256
MLA_TK = 512


def _attn_blocks(s_len, tk=ATTN_TK):
    tq, th, tk = min(s_len, ATTN_TQ), min(s_len, ATTN_TH), min(s_len, tk)
    return tq, th, tk, tq // tk


def _chains(tq, th):
    return [(hh, r0) for hh in range(2) for r0 in range(0, tq, th)]


def _diag_mask(th, tk, r0, sub, strict):
    lo, hi = sub * tk, (sub + 1) * tk - 1
    last, first = r0 + th - 1, r0
    if (lo >= last) if strict else (lo > last):
        return "none"
    if (hi < first) if strict else (hi <= first):
        return "all"
    row = lax.broadcasted_iota(jnp.int32, (th, tk), 0) + r0
    col = lax.broadcasted_iota(jnp.int32, (th, tk), 1) + lo
    return col < row if strict else col <= row


def _mla_fwd(q, k, v, ride):
    s_len = q.shape[0]
    tq, th, tk, nsub = _attn_blocks(s_len, MLA_TK)
    chains = _chains(tq, th)
    nh = tq // th
    nq = s_len // tq

    def body(q_ref, k_ref, v_ref, *rest):
        o_ref, lse_ref = rest[ride.n:ride.n + 2]
        ride.bind(rest[:ride.n], rest[ride.n + 2:2 * ride.n + 2], rest[2 * ride.n + 2:])
        pl.when((pl.program_id(0) == 0) & (pl.program_id(1) == 0))(ride.issue)
        i = pl.program_id(1)
        lane = lax.broadcasted_iota(jnp.int32, (th, LANES), 1)
        hsl = [slice(hh * HEAD_PAD, (hh + 1) * HEAD_PAD) for hh in range(2)]

        def step(kb, carry, sub):
            rows = pl.ds(pl.multiple_of(kb * tk, tk), tk)
            vblk = v_ref[rows, :]
            masks = ["all" if sub is None else _diag_mask(th, tk, r0, sub, strict=False) for _, r0 in chains]
            live = [n for n, m in enumerate(masks) if not (isinstance(m, str) and m == "none")]
            s = {n: _dot_nt(q_ref[chains[n][1]:chains[n][1] + th, hsl[chains[n][0]]], k_ref[rows, hsl[chains[n][0]]])
                 for n in live}
            new = list(carry)
            pb, alpha = {}, {}
            for n in live:
                m, l, _ = carry[n]
                sn = s[n]
                if not isinstance(masks[n], str):
                    sn = jnp.where(masks[n], sn, NEG_BIG)
                m_new = jnp.maximum(m, jnp.max(sn, axis=-1, keepdims=True))
                alpha[n] = jnp.exp2(m - m_new)
                p = jnp.exp2(sn - m_new)
                pb[n] = p.astype(BF16)
                new[n] = (m_new, alpha[n] * l + jnp.sum(p, axis=-1, keepdims=True), None)
            pv = {n: _dot(pb[n], vblk) for n in live}
            for n in live:
                new[n] = (new[n][0], new[n][1], alpha[n] * carry[n][2] + pv[n])
            return tuple(new)

        init = (jnp.full((th, 1), NEG_BIG, F32), jnp.zeros((th, 1), F32), jnp.zeros((th, LANES), F32))
        carry = lax.fori_loop(0, i * nsub, lambda kb, cy: step(kb, cy, None), (init,) * len(chains))
        for sub in range(nsub):
            carry = step(i * nsub + sub, carry, sub)
        for c in range(nh):
            (m0, l0, a0), (m1, l1, a1) = carry[c], carry[nh + c]
            rs = slice(c * th, (c + 1) * th)
            o_ref[rs, :] = jnp.where(lane < 64, a0 / l0, a1 / l1).astype(BF16)
            lse_ref[rs, :] = jnp.where(lane < 64, m0 + jnp.log2(l0), m1 + jnp.log2(l1))
        pl.when((pl.program_id(0) == 3) & (i == nq - 1))(ride.finish)

    outs = pl.pallas_call(
        body, name="mla_fwd", grid=(4, nq),
        in_specs=[pl.BlockSpec((tq, 2 * HEAD_PAD), lambda p, i: (i, p)),
                  pl.BlockSpec((s_len, 2 * HEAD_PAD), lambda p, i: (0, p)),
                  pl.BlockSpec((s_len, LANES), lambda p, i: (0, p))] + ride.specs,
        out_specs=[pl.BlockSpec((tq, LANES), lambda p, i: (i, p)),
                   pl.BlockSpec((None, tq, LANES), lambda p, i: (p, i, 0))] + ride.specs,
        out_shape=[jax.ShapeDtypeStruct((s_len, 512), BF16),
                   jax.ShapeDtypeStruct((4, s_len, LANES), F32)] + ride.out_shape,
        scratch_shapes=ride.scratch,
        compiler_params=_cparams(40, dimension_semantics=("arbitrary", "arbitrary")),
    )(q, k, v, *ride.srcs)
    return outs[0], outs[1], outs[2:]


def _mla_bwd(q, k, v, o, do, lse, ride):
    s_len = q.shape[0]
    tq, th, tk, nsub = _attn_blocks(s_len, MLA_TK)
    chains = _chains(tq, th)
    nq = s_len // tq

    def body(q_ref, k_ref, v_ref, o_ref, do_ref, lse_ref, *rest):
        dq_ref, dk_ref, dv_ref = rest[ride.n:ride.n + 3]
        ride.bind(rest[:ride.n], rest[ride.n + 3:2 * ride.n + 3], rest[2 * ride.n + 3:])
        pl.when((pl.program_id(0) == 0) & (pl.program_id(1) == 0))(ride.issue)
        i = pl.program_id(1)
        lane = lax.broadcasted_iota(jnp.int32, (th, LANES), 1)

        @pl.when(i == 0)
        def _():
            dk_ref[...] = jnp.zeros_like(dk_ref)
            dv_ref[...] = jnp.zeros_like(dv_ref)

        hsl = [slice(hh * HEAD_PAD, (hh + 1) * HEAD_PAD) for hh in range(2)]
        qs, dos, deltas, lses = [], [], [], []
        for hh, r0 in chains:
            rs = slice(r0, r0 + th)
            qs.append(q_ref[rs, hsl[hh]])
            doh = jnp.where((lane // 64) == hh, do_ref[rs, :], jnp.zeros((), BF16))
            dos.append(doh)
            deltas.append(jnp.sum(doh.astype(F32) * o_ref[rs, :].astype(F32), axis=-1, keepdims=True))
            lses.append(lse_ref[rs, 64 * hh:64 * hh + 1])

        def step(kb, dqs, sub):
            rows = pl.ds(pl.multiple_of(kb * tk, tk), tk)
            vblk = v_ref[rows, :]
            new, p_all, do_all = [], [], []
            ds_h, q_h = [[], []], [[], []]
            for c, (hh, r0) in enumerate(chains):
                mask = "all" if sub is None else _diag_mask(th, tk, r0, sub, strict=False)
                if isinstance(mask, str) and mask == "none":
                    new.append(dqs[c])
                    continue
                kblk = k_ref[rows, hsl[hh]]
                s = _dot_nt(qs[c], kblk)
                if not isinstance(mask, str):
                    s = jnp.where(mask, s, NEG_BIG)
                p = jnp.exp2(s - lses[c])
                dp = _dot_nt(dos[c], vblk)
                ds = (p * (dp - deltas[c]) * MLA_SCALE).astype(BF16)
                p_all.append(p.astype(BF16))
                do_all.append(dos[c])
                ds_h[hh].append(ds)
                q_h[hh].append(qs[c])
                new.append(dqs[c] + _dot(ds, kblk))
            dv_ref[rows, :] += _dot_tn(jnp.concatenate(p_all, axis=0), jnp.concatenate(do_all, axis=0))
            for hh in range(2):
                dk_ref[rows, hsl[hh]] += _dot_tn(jnp.concatenate(ds_h[hh], axis=0),
                                                 jnp.concatenate(q_h[hh], axis=0))
            return tuple(new)

        zero = jnp.zeros((th, LANES), F32)
        dqs = lax.fori_loop(0, i * nsub, lambda kb, cy: step(kb, cy, None), (zero,) * len(chains))
        for sub in range(nsub):
            dqs = step(i * nsub + sub, dqs, sub)
        for c, (hh, r0) in enumerate(chains):
            dq_ref[r0:r0 + th, hsl[hh]] = dqs[c]
        pl.when((pl.program_id(0) == 3) & (i == nq - 1))(ride.finish)

    outs = pl.pallas_call(
        body, name="mla_bwd", grid=(4, nq),
        in_specs=[pl.BlockSpec((tq, 2 * HEAD_PAD), lambda p, i: (i, p)),
                  pl.BlockSpec((s_len, 2 * HEAD_PAD), lambda p, i: (0, p)),
                  pl.BlockSpec((s_len, LANES), lambda p, i: (0, p)),
                  pl.BlockSpec((tq, LANES), lambda p, i: (i, p)),
                  pl.BlockSpec((tq, LANES), lambda p, i: (i, p)),
                  pl.BlockSpec((None, tq, LANES), lambda p, i: (p, i, 0))] + ride.specs,
        out_specs=[pl.BlockSpec((tq, 2 * HEAD_PAD), lambda p, i: (i, p)),
                   pl.BlockSpec((s_len, 2 * HEAD_PAD), lambda p, i: (0, p)),
                   pl.BlockSpec((s_len, LANES), lambda p, i: (0, p))] + ride.specs,
        out_shape=[jax.ShapeDtypeStruct((s_len, 1024), F32), jax.ShapeDtypeStruct((s_len, 1024), F32),
                   jax.ShapeDtypeStruct((s_len, 512), F32)] + ride.out_shape,
        scratch_shapes=ride.scratch,
        compiler_params=_cparams(56, dimension_semantics=("arbitrary", "arbitrary")),
    )(q, k, v, o, do, lse, *ride.srcs)
    return outs[0], outs[1], outs[2], outs[3:]


def _log_sigmoids(z2):
    sp = jnp.log2(1.0 + jnp.exp2(-jnp.abs(z2)))
    lb = jnp.minimum(z2, 0.0) - sp
    return lb, lb - z2


def _split_dot(x, w, parts, nt=False):
    dot = _dot_nt if nt else _dot
    out = None
    for _ in range(parts):
        xb = x.astype(BF16)
        t = dot(xb, w)
        out = t if out is None else out + t
        x = x - xb.astype(F32)
    return out


def _sb_fwd(sb):
    s_len = sb.shape[0]
    tq, th, tk, nsub = _attn_blocks(s_len)
    chains = _chains(tq, th)
    nh = tq // th
    assert s_len // tk <= 64

    def body(q_ref, k_ref, v_ref, o_ref, r_ref):
        i = pl.program_id(1)
        lane = lax.broadcasted_iota(jnp.int32, (th, LANES), 1)
        upper = (lax.broadcasted_iota(jnp.int32, (tk, tk), 0)
                 > lax.broadcasted_iota(jnp.int32, (tk, tk), 1)).astype(BF16)
        qs = [jnp.where((lane // 64) == hh, q_ref[r0:r0 + th, :], jnp.zeros((), BF16)) for hh, r0 in chains]

        def step(kb, carry, sub):
            rows = pl.ds(pl.multiple_of(kb * tk, tk), tk)
            kblk, vblk = k_ref[rows, :], v_ref[rows, :]
            masks = ["all" if sub is None else _diag_mask(th, tk, r0, sub, strict=True) for _, r0 in chains]
            live = [n for n, m in enumerate(masks) if not (isinstance(m, str) and m == "none")]
            masked = {n: not isinstance(masks[n], str) for n in live}
            z = {n: _dot_nt(qs[n], kblk) for n in live}
            lb, lom = {}, {}
            for n in live:
                lb[n], lom[n] = _log_sigmoids(z[n])
                if masked[n]:
                    lom[n] = jnp.where(masks[n], lom[n], 0.0)
            suf = {n: _split_dot(lom[n], upper, 2) for n in live}
            a = {}
            for n in live:
                a[n] = jnp.exp2(lb[n] + suf[n] + carry[n][0])
                if masked[n]:
                    a[n] = jnp.where(masks[n], a[n], 0.0)
            pv = {n: _dot(a[n].astype(BF16), vblk) for n in live}
            new = list(carry)
            for n in live:
                c, acc, r = carry[n]
                rs = suf[n][:, 0:1] + lom[n][:, 0:1]
                new[n] = (c + rs, acc + pv[n], jnp.where(lane == 64 * chains[n][0] + kb, rs, r))
            return tuple(new)

        init = (jnp.zeros((th, 1), F32), jnp.zeros((th, LANES), F32), jnp.zeros((th, LANES), F32))
        carry = (init,) * len(chains)
        for sub in reversed(range(nsub)):
            carry = step(i * nsub + sub, carry, sub)

        def spent(cy):
            top = functools.reduce(jnp.maximum, [jnp.max(c) for c, _, _ in cy])
            return (top < SB_CUT).astype(jnp.int32)

        def walk(state):
            t, _, cy = state
            cy = step(i * nsub - 1 - t, cy, None)
            return t + 1, spent(cy), cy

        _, _, carry = lax.while_loop(lambda st: (st[0] < i * nsub) & (st[1] == 0), walk,
                                     (jnp.int32(0), spent(carry), carry))
        for n in range(nh):
            rs = slice(n * th, (n + 1) * th)
            o_ref[rs, :] = jnp.where(lane < 64, carry[n][1], carry[nh + n][1]).astype(BF16)
            r_ref[rs, :] = jnp.where(lane < 64, carry[n][2], carry[nh + n][2])

    return pl.pallas_call(
        body, name="sb_fwd", grid=(4, s_len // tq),
        in_specs=[pl.BlockSpec((tq, LANES), lambda p, i: (i, p)),
                  pl.BlockSpec((s_len, LANES), lambda p, i: (0, 4 + p)),
                  pl.BlockSpec((s_len, LANES), lambda p, i: (0, 8 + p))],
        out_specs=[pl.BlockSpec((tq, LANES), lambda p, i: (i, p)),
                   pl.BlockSpec((None, tq, LANES), lambda p, i: (p, i, 0))],
        out_shape=[jax.ShapeDtypeStruct((s_len, 512), BF16), jax.ShapeDtypeStruct((4, s_len, LANES), F32)],
        compiler_params=_cparams(40, dimension_semantics=("parallel", "arbitrary")),
    )(sb, sb, sb)


def _sb_bwd(sb, do, r):
    s_len = sb.shape[0]
    tq, th, tk, nsub = _attn_blocks(s_len)
    chains = _chains(tq, th)
    nh = tq // th

    def body(q_ref, k_ref, v_ref, do_ref, r_ref, dq_ref, dk_ref, dv_ref):
        i = pl.program_id(1)
        lane = lax.broadcasted_iota(jnp.int32, (th, LANES), 1)
        upper = (lax.broadcasted_iota(jnp.int32, (tk, tk), 0)
                 > lax.broadcasted_iota(jnp.int32, (tk, tk), 1)).astype(BF16)
        tri = (lax.broadcasted_iota(jnp.int32, (LANES, LANES), 0)
               > lax.broadcasted_iota(jnp.int32, (LANES, LANES), 1)).astype(BF16)

        @pl.when(i == 0)
        def _():
            dk_ref[...] = jnp.zeros_like(dk_ref)
            dv_ref[...] = jnp.zeros_like(dv_ref)

        qs, dos, rights = [], [], []
        for hh, r0 in chains:
            rs = slice(r0, r0 + th)
            hm = (lane // 64) == hh
            qs.append(jnp.where(hm, q_ref[rs, :], jnp.zeros((), BF16)))
            dos.append(jnp.where(hm, do_ref[rs, :], jnp.zeros((), BF16)))
            rights.append(_split_dot(jnp.where(hm, r_ref[rs, :], 0.0), tri, 3))

        def step(kb, carry, sub):
            rows = pl.ds(pl.multiple_of(kb * tk, tk), tk)
            kblk, vblk = k_ref[rows, :], v_ref[rows, :]
            new, a_all, do_all, dz_all, q_all = [], [], [], [], []
            for n, ((hh, r0), (pre, dq)) in enumerate(zip(chains, carry)):
                mask = "all" if sub is None else _diag_mask(th, tk, r0, sub, strict=True)
                if isinstance(mask, str) and mask == "none":
                    new.append((pre, dq))
                    continue
                c = jnp.sum(jnp.where(lane == 64 * hh + kb, rights[n], 0.0), axis=-1, keepdims=True)
                z = _dot_nt(qs[n], kblk)
                lb, lom = _log_sigmoids(z)
                if not isinstance(mask, str):
                    lom = jnp.where(mask, lom, 0.0)
                suf = _split_dot(lom, upper, 2)
                a = jnp.exp2(lb + suf + c)
                if not isinstance(mask, str):
                    a = jnp.where(mask, a, 0.0)
                g = a * _dot_nt(dos[n], vblk)
                left = _split_dot(g, upper, 1, nt=True) + pre
                sig = jnp.exp2(lb)
                dz = g * (1.0 - sig) - sig * left
                if not isinstance(mask, str):
                    dz = jnp.where(mask, dz, 0.0)
                dzb = dz.astype(BF16)
                a_all.append(a.astype(BF16))
                do_all.append(dos[n])
                dz_all.append(dzb)
                q_all.append(qs[n])
                new.append((left[:, tk - 1:tk] + g[:, tk - 1:tk], dq + _dot(dzb, kblk)))
            dv_ref[rows, :] += _dot_tn(jnp.concatenate(a_all, axis=0), jnp.concatenate(do_all, axis=0))
            dk_ref[rows, :] += _dot_tn(jnp.concatenate(dz_all, axis=0), jnp.concatenate(q_all, axis=0))
            return tuple(new)

        lane1 = lax.broadcasted_iota(jnp.int32, (1, LANES), 1)
        first = i * nsub
        for n, (hh, _) in enumerate(chains):
            top = jnp.max(rights[n], axis=0, keepdims=True)
            kb_of = lane1 - 64 * hh
            live = (kb_of >= 0) & (kb_of < i * nsub) & (top >= SB_CUT)
            first = jnp.minimum(first, jnp.min(jnp.where(live, kb_of, i * nsub)))

        init = (jnp.zeros((th, 1), F32), jnp.zeros((th, LANES), F32))
        carry = lax.fori_loop(first, i * nsub, lambda kb, cy: step(kb, cy, None), (init,) * len(chains))
        for sub in range(nsub):
            carry = step(i * nsub + sub, carry, sub)
        for n in range(nh):
            dq_ref[n * th:(n + 1) * th, :] = jnp.where(lane < 64, carry[n][1], carry[nh + n][1]) * SB_SCALE

    return pl.pallas_call(
        body, name="sb_bwd", grid=(4, s_len // tq),
        in_specs=[pl.BlockSpec((tq, LANES), lambda p, i: (i, p)),
                  pl.BlockSpec((s_len, LANES), lambda p, i: (0, 4 + p)),
                  pl.BlockSpec((s_len, LANES), lambda p, i: (0, 8 + p)),
                  pl.BlockSpec((tq, LANES), lambda p, i: (i, p)),
                  pl.BlockSpec((None, tq, LANES), lambda p, i: (p, i, 0))],
        out_specs=[pl.BlockSpec((tq, LANES), lambda p, i: (i, p)),
                   pl.BlockSpec((s_len, LANES), lambda p, i: (0, p)),
                   pl.BlockSpec((s_len, LANES), lambda p, i: (0, p))],
        out_shape=[jax.ShapeDtypeStruct((s_len, 512), F32)] * 3,
        compiler_params=_cparams(48, dimension_semantics=("arbitrary", "arbitrary")),
    )(sb, sb, sb, do, r)


def _merge_fwd(x, oa, ob, gates, bg, wa, wb, wo):
    s_len = x.shape[0]
    tm = _row_block(s_len)

    def body(x_ref, oa_ref, ob_ref, g_ref, bg_ref, wa_ref, wb_ref, wo_ref, y_ref):
        pa = _dot(oa_ref[...], wa_ref[...])
        pb = _dot(ob_ref[...], wb_ref[...])
        merged = (_sigmoid(g_ref[:, 0:D_MODEL] + bg_ref[0:1, :]) * pa
                  + _sigmoid(g_ref[:, D_MODEL:2 * D_MODEL] + bg_ref[1:2, :]) * pb)
        y_ref[...] = x_ref[...] + _dot(merged.astype(BF16), wo_ref[...])

    full = lambda shape: pl.BlockSpec(shape, lambda i: (0, 0))
    rowb = lambda n: pl.BlockSpec((tm, n), lambda i: (i, 0))
    return pl.pallas_call(
        body, name="merge_fwd", grid=(s_len // tm,),
        in_specs=[rowb(1024), rowb(512), rowb(512), rowb(2048), full((2, 1024)), full((512, 1024)),
                  full((512, 1024)), full((1024, 1024))],
        out_specs=rowb(1024),
        out_shape=jax.ShapeDtypeStruct((s_len, D_MODEL), F32),
        compiler_params=_cparams(48, dimension_semantics=("parallel",)),
    )(x, oa, ob, gates, bg, wa, wb, wo)


def _merge_bwd(dx1, oa, ob, gates, bg, wa, wb, wo):
    s_len = dx1.shape[0]
    tm = _row_block(s_len)

    def body(dx_ref, oa_ref, ob_ref, g_ref, bg_ref, wa_ref, wb_ref, wo_ref,
             doa_ref, dob_ref, dgate_ref, dpa_ref, dpb_ref, merged_ref, dxb_ref, dbg_ref):
        first = pl.program_id(0) == 0
        dxb = dx_ref[...].astype(BF16)
        dxb_ref[...] = dxb
        pa = _dot(oa_ref[...], wa_ref[...])
        pb = _dot(ob_ref[...], wb_ref[...])
        sa = _sigmoid(g_ref[:, 0:D_MODEL] + bg_ref[0:1, :])
        sbg = _sigmoid(g_ref[:, D_MODEL:2 * D_MODEL] + bg_ref[1:2, :])
        merged_ref[...] = (sa * pa + sbg * pb).astype(BF16)
        dm = _dot_nt(dxb, wo_ref[...])
        dpa = (dm * sa).astype(BF16)
        dpb = (dm * sbg).astype(BF16)
        dpa_ref[...] = dpa
        dpb_ref[...] = dpb
        dga = dm * pa * sa * (1.0 - sa)
        dgb = dm * pb * sbg * (1.0 - sbg)
        dgate_ref[:, 0:D_MODEL] = dga.astype(BF16)
        dgate_ref[:, D_MODEL:2 * D_MODEL] = dgb.astype(BF16)
        _acc_rows(dbg_ref.at[0:1, :], dga, first)
        _acc_rows(dbg_ref.at[1:2, :], dgb, first)
        doa_ref[...] = _dot_nt(dpa, wa_ref[...]).astype(BF16)
        dob_ref[...] = _dot_nt(dpb, wb_ref[...]).astype(BF16)

    full = lambda shape: pl.BlockSpec(shape, lambda i: (0, 0))
    rowb = lambda n: pl.BlockSpec((tm, n), lambda i: (i, 0))
    sds = lambda n, dt: jax.ShapeDtypeStruct((s_len, n), dt)
    return pl.pallas_call(
        body, name="merge_bwd", grid=(s_len // tm,),
        in_specs=[rowb(1024), rowb(512), rowb(512), rowb(2048), full((2, 1024)), full((512, 1024)),
                  full((512, 1024)), full((1024, 1024))],
        out_specs=[rowb(512), rowb(512), rowb(2048), rowb(1024), rowb(1024), rowb(1024), rowb(1024),
                   full((2, 1024))],
        out_shape=[sds(512, BF16), sds(512, BF16), sds(2048, BF16), sds(1024, BF16), sds(1024, BF16),
                   sds(1024, BF16), sds(1024, BF16), jax.ShapeDtypeStruct((2, 1024), F32)],
        compiler_params=_cparams(48, dimension_semantics=("arbitrary",)),
    )(dx1, oa, ob, gates, bg, wa, wb, wo)


def _mem_kv(mem, g, w):
    m_len = mem.shape[0]

    def body(mem_ref, g_ref, w_ref, mn_ref, kv_ref):
        mn, _, _ = _rms(mem_ref[...], g_ref[...])
        mnb = mn.astype(BF16)
        mn_ref[...] = mnb
        kv_ref[...] = _dot(mnb, w_ref[...]).astype(BF16)

    return pl.pallas_call(
        body, name="mem_kv",
        out_shape=[jax.ShapeDtypeStruct((m_len, D_MODEL), BF16), jax.ShapeDtypeStruct((m_len, 1024), BF16)],
    )(mem, g, w)


def _mem_bwd(mem, g, w, mn, dkv):
    def body(mem_ref, g_ref, w_ref, mn_ref, dkv_ref, dw_ref, dg_ref):
        dkvb = dkv_ref[...].astype(BF16)
        dw_ref[...] = _dot_tn(mn_ref[...], dkvb)
        dmn = _dot_nt(dkvb, w_ref[...])
        _, xh, _ = _rms(mem_ref[...], g_ref[...])
        dg_ref[...] = jnp.sum(dmn * xh, axis=0, keepdims=True)

    return pl.pallas_call(
        body, name="mem_bwd",
        out_shape=[jax.ShapeDtypeStruct((D_MODEL, 1024), F32), jax.ShapeDtypeStruct((1, D_MODEL), F32)],
    )(mem, g, w, mn, dkv)


def _xattn_heads(xqb, kv_ref, m_len):
    ps = []
    for h in range(X_HEADS):
        hs = slice(h * X_HEAD_DIM, (h + 1) * X_HEAD_DIM)
        s = _dot_nt(xqb[:, hs], kv_ref[:, hs]) * X_SCALE
        e = jnp.exp(s - jnp.max(s, axis=-1, keepdims=True))
        ps.append(e / jnp.sum(e, axis=-1, keepdims=True))
    return ps


def _xattn_fwd(x1, g, wxq, kv, wxo):
    s_len, m_len = x1.shape[0], kv.shape[0]
    tm = _row_block(s_len)

    def body(x_ref, g_ref, wq_ref, kv_ref, wo_ref, y_ref):
        hx, _, _ = _rms(x_ref[...], g_ref[...])
        xqb = _dot(hx.astype(BF16), wq_ref[...]).astype(BF16)
        ps = _xattn_heads(xqb, kv_ref, m_len)
        xo = jnp.concatenate(
            [_dot(ps[h].astype(BF16), kv_ref[:, 512 + h * X_HEAD_DIM:512 + (h + 1) * X_HEAD_DIM])
             for h in range(X_HEADS)], axis=-1)
        y_ref[...] = x_ref[...] + _dot(xo.astype(BF16), wo_ref[...])

    full = lambda shape: pl.BlockSpec(shape, lambda i: (0, 0))
    rowb = lambda n: pl.BlockSpec((tm, n), lambda i: (i, 0))
    return pl.pallas_call(
        body, name="xattn_fwd", grid=(s_len // tm,),
        in_specs=[rowb(1024), full((1, 1024)), full((1024, 512)), full((m_len, 1024)), full((512, 1024))],
        out_specs=rowb(1024),
        out_shape=jax.ShapeDtypeStruct((s_len, D_MODEL), F32),
        compiler_params=_cparams(48, dimension_semantics=("parallel",)),
    )(x1, g, wxq, kv, wxo)


def _xattn_bwd(x1, dx2, g, wxq, kv, wxo):
    s_len, m_len = x1.shape[0], kv.shape[0]
    tm = _row_block(s_len)

    def body(x_ref, dy_ref, g_ref, wq_ref, kv_ref, wo_ref, dx_ref, dwq_ref, dwo_ref, dkv_ref, dg_ref):
        first = pl.program_id(0) == 0
        gv = g_ref[...]
        hx, xh, r = _rms(x_ref[...], gv)
        hxb = hx.astype(BF16)
        xqb = _dot(hxb, wq_ref[...]).astype(BF16)
        ps = _xattn_heads(xqb, kv_ref, m_len)
        dy = dy_ref[...]
        dyb = dy.astype(BF16)
        dxo = _dot_nt(dyb, wo_ref[...])
        xos, dqs, dks, dvs = [], [], [], []
        for h in range(X_HEADS):
            hs = slice(h * X_HEAD_DIM, (h + 1) * X_HEAD_DIM)
            vs = slice(512 + h * X_HEAD_DIM, 512 + (h + 1) * X_HEAD_DIM)
            p = ps[h]
            pb = p.astype(BF16)
            dxoh = dxo[:, hs].astype(BF16)
            xos.append(_dot(pb, kv_ref[:, vs]))
            dp = _dot_nt(dxoh, kv_ref[:, vs])
            ds = (p * (dp - jnp.sum(dp * p, axis=-1, keepdims=True)) * X_SCALE).astype(BF16)
            dvs.append(_dot_tn(pb, dxoh))
            dks.append(_dot_tn(ds, xqb[:, hs]))
            dqs.append(_dot(ds, kv_ref[:, hs]))
        xob = jnp.concatenate(xos, axis=-1).astype(BF16)
        dxqb = jnp.concatenate(dqs, axis=-1).astype(BF16)
        _acc(dwo_ref, _dot_tn(xob, dyb), first)
        _acc(dwq_ref, _dot_tn(hxb, dxqb), first)
        _acc(dkv_ref, jnp.concatenate(dks + dvs, axis=-1), first)
        dhx = _dot_nt(dxqb, wq_ref[...])
        dx, dgr = _rms_bwd(dhx, xh, r, gv)
        dx_ref[...] = dy + dx
        _acc_rows(dg_ref, dgr, first)

    full = lambda shape: pl.BlockSpec(shape, lambda i: (0, 0))
    rowb = lambda n: pl.BlockSpec((tm, n), lambda i: (i, 0))
    return pl.pallas_call(
        body, name="xattn_bwd", grid=(s_len // tm,),
        in_specs=[rowb(1024), rowb(1024), full((1, 1024)), full((1024, 512)), full((m_len, 1024)),
                  full((512, 1024))],
        out_specs=[rowb(1024), full((1024, 512)), full((512, 1024)), full((m_len, 1024)), full((1, 1024))],
        out_shape=[jax.ShapeDtypeStruct((s_len, D_MODEL), F32), jax.ShapeDtypeStruct((1024, 512), F32),
                   jax.ShapeDtypeStruct((512, 1024), F32), jax.ShapeDtypeStruct((m_len, 1024), F32),
                   jax.ShapeDtypeStruct((1, D_MODEL), F32)],
        compiler_params=_cparams(48, dimension_semantics=("arbitrary",)),
    )(x1, dx2, g, wxq, kv, wxo)


FF_TILE = 1408
FF_TILE_BWD = 256


def _ffn_fwd(x2, g, wg, wu, wd):
    s_len = x2.shape[0]
    tm, tf = _row_block(s_len), FF_TILE

    def body(x_ref, g_ref, wg_ref, wu_ref, wd_ref, y_ref, h_scr):
        j = pl.program_id(1)

        @pl.when(j == 0)
        def _():
            hf, _, _ = _rms(x_ref[...], g_ref[...])
            h_scr[...] = hf.astype(BF16)
            y_ref[...] = x_ref[...]

        hb = h_scr[...]
        gt = _dot(hb, wg_ref[...])
        up = _dot(hb, wu_ref[...])
        act = gt * _sigmoid(gt) * up
        y_ref[...] += _dot(act.astype(BF16), wd_ref[...])

    return pl.pallas_call(
        body, name="ffn_fwd", grid=(s_len // tm, D_FF // tf),
        in_specs=[pl.BlockSpec((tm, D_MODEL), lambda i, j: (i, 0)),
                  pl.BlockSpec((1, D_MODEL), lambda i, j: (0, 0)),
                  pl.BlockSpec((D_MODEL, tf), lambda i, j: (0, j)),
                  pl.BlockSpec((D_MODEL, tf), lambda i, j: (0, j)),
                  pl.BlockSpec((tf, D_MODEL), lambda i, j: (j, 0))],
        out_specs=pl.BlockSpec((tm, D_MODEL), lambda i, j: (i, 0)),
        out_shape=jax.ShapeDtypeStruct((s_len, D_MODEL), F32),
        scratch_shapes=[pltpu.VMEM((tm, D_MODEL), BF16)],
        compiler_params=_cparams(48, dimension_semantics=("parallel", "arbitrary")),
    )(x2, g, wg, wu, wd)


def _ffn_bwd(x2, dx3, g, wg, wu, wd):
    s_len = x2.shape[0]
    tm, tf = _row_block(s_len), FF_TILE_BWD
    nf = D_FF // tf

    def body(x_ref, dy_ref, g_ref, wg_ref, wu_ref, wd_ref,
             dx_ref, h_ref, dgt_ref, dup_ref, act_ref, dg_ref, h_scr, dyb_scr, dh_scr):
        i, j = pl.program_id(0), pl.program_id(1)

        @pl.when(j == 0)
        def _():
            hf, _, _ = _rms(x_ref[...], g_ref[...])
            hb = hf.astype(BF16)
            h_scr[...] = hb
            h_ref[...] = hb
            dyb_scr[...] = dy_ref[...].astype(BF16)
            dh_scr[...] = jnp.zeros_like(dh_scr)

        hb = h_scr[...]
        gt = _dot(hb, wg_ref[...])
        up = _dot(hb, wu_ref[...])
        sg = _sigmoid(gt)
        silu = gt * sg
        dact = _dot_nt(dyb_scr[...], wd_ref[...])
        dgt = (dact * up * (sg * (1.0 + gt * (1.0 - sg)))).astype(BF16)
        dup = (dact * silu).astype(BF16)
        dgt_ref[...] = dgt
        dup_ref[...] = dup
        act_ref[...] = (silu * up).astype(BF16)
        dh_scr[...] += _dot_nt(dgt, wg_ref[...]) + _dot_nt(dup, wu_ref[...])

        @pl.when(j == nf - 1)
        def _():
            gv = g_ref[...]
            _, xh, r = _rms(x_ref[...], gv)
            dx, dgr = _rms_bwd(dh_scr[...], xh, r, gv)
            dx_ref[...] = dy_ref[...] + dx
            _acc_rows(dg_ref, dgr, i == 0)

    rowb = pl.BlockSpec((tm, D_MODEL), lambda i, j: (i, 0))
    ffb = pl.BlockSpec((tm, tf), lambda i, j: (i, j))
    return pl.pallas_call(
        body, name="ffn_bwd", grid=(s_len // tm, nf),
        in_specs=[rowb, rowb, pl.BlockSpec((1, D_MODEL), lambda i, j: (0, 0)),
                  pl.BlockSpec((D_MODEL, tf), lambda i, j: (0, j)),
                  pl.BlockSpec((D_MODEL, tf), lambda i, j: (0, j)),
                  pl.BlockSpec((tf, D_MODEL), lambda i, j: (j, 0))],
        out_specs=[rowb, rowb, ffb, ffb, ffb, pl.BlockSpec((1, D_MODEL), lambda i, j: (0, 0))],
        out_shape=[jax.ShapeDtypeStruct((s_len, D_MODEL), F32), jax.ShapeDtypeStruct((s_len, D_MODEL), BF16),
                   jax.ShapeDtypeStruct((s_len, D_FF), BF16), jax.ShapeDtypeStruct((s_len, D_FF), BF16),
                   jax.ShapeDtypeStruct((s_len, D_FF), BF16), jax.ShapeDtypeStruct((1, D_MODEL), F32)],
        scratch_shapes=[pltpu.VMEM((tm, D_MODEL), BF16), pltpu.VMEM((tm, D_MODEL), BF16),
                        pltpu.VMEM((tm, D_MODEL), F32)],
        compiler_params=_cparams(56, dimension_semantics=("arbitrary", "arbitrary")),
    )(x2, dx3, g, wg, wu, wd)


def _loss_head(x3, g, target):
    s_len = x3.shape[0]
    tm = _row_block(s_len)

    def body(x_ref, g_ref, t_ref, sse_ref, dx_ref, dxb_ref, dg_ref):
        first = pl.program_id(0) == 0
        gv = g_ref[...]
        y, xh, r = _rms(x_ref[...], gv)
        err = y - t_ref[...]
        _acc(sse_ref, jnp.broadcast_to(jnp.sum(err * err), (8, LANES)), first)
        dx, dgr = _rms_bwd(err * (1.0 / D_MODEL), xh, r, gv)
        dx_ref[...] = dx
        dxb_ref[...] = dx.astype(BF16)
        _acc_rows(dg_ref, dgr, first)

    rowb = pl.BlockSpec((tm, D_MODEL), lambda i: (i, 0))
    return pl.pallas_call(
        body, name="loss_head", grid=(s_len // tm,),
        in_specs=[rowb, pl.BlockSpec((1, D_MODEL), lambda i: (0, 0)), rowb],
        out_specs=[pl.BlockSpec((8, LANES), lambda i: (0, 0)), rowb, rowb,
                   pl.BlockSpec((1, D_MODEL), lambda i: (0, 0))],
        out_shape=[jax.ShapeDtypeStruct((8, LANES), F32), jax.ShapeDtypeStruct((s_len, D_MODEL), F32),
                   jax.ShapeDtypeStruct((s_len, D_MODEL), BF16), jax.ShapeDtypeStruct((1, D_MODEL), F32)],
        compiler_params=_cparams(dimension_semantics=("arbitrary",)),
    )(x3, g, target)


def _mla_prep_bwd(lat, g_q, g_kv, w_uq, w_uk, w_uv, cosf, sinf, dq, dk, dv):
    s_len = lat.shape[0]
    tm = _row_block(s_len)

    def body(lat_ref, gq_ref, gkv_ref, wuq_ref, wuk_ref, wuv_ref, cos_ref, sin_ref, dq_ref, dk_ref, dv_ref,
             dlat_ref, dqb_ref, dkb_ref, dvb_ref, dgq_ref, dgkv_ref):
        first = pl.program_id(0) == 0
        lane = lax.broadcasted_iota(jnp.int32, (tm, LANES), 1)
        cosv, sinv = cos_ref[...], sin_ref[...]
        gq, gkv = gq_ref[...], gkv_ref[...]
        _, qxh, qr = _rms(lat_ref[:, 0:256], gq)
        _, kxh, kr_ = _rms(lat_ref[:, 256:384], gkv)
        dkr = jnp.zeros((tm, LANES), F32)
        for h in range(MLA_HEADS):
            sl = slice(h * HEAD_PAD, (h + 1) * HEAD_PAD)
            blk = dq_ref[:, sl]
            dqb_ref[:, sl] = (blk * cosv + _rope_rot_t(blk, lane) * sinv).astype(BF16)
            kblk = dk_ref[:, sl] * (1.0 / MLA_Q_FOLD)
            dkb_ref[:, sl] = kblk.astype(BF16)
            dkr = dkr + kblk
        dvb = dv_ref[...].astype(BF16)
        dvb_ref[...] = dvb
        dkr = jnp.where((lane >= 64) & (lane < 96), dkr, 0.0)
        dkr = dkr * cosv + _rope_rot_t(dkr, lane) * sinv
        dql = _dot_nt(dqb_ref[...], wuq_ref[...])
        dkvl = _dot_nt(dkb_ref[...], wuk_ref[...]) + _dot_nt(dvb, wuv_ref[...])
        dcq, dgqr = _rms_bwd(dql, qxh, qr, gq)
        dckv, dgkvr = _rms_bwd(dkvl, kxh, kr_, gkv)
        dlat_ref[:, 0:256] = dcq
        dlat_ref[:, 256:384] = dckv
        dlat_ref[:, K_R_OFF:K_R_OFF + LANES] = pltpu.roll(dkr, 64, 1)
        _acc_rows(dgq_ref, dgqr, first)
        _acc_rows(dgkv_ref, dgkvr, first)

    full = lambda shape: pl.BlockSpec(shape, lambda i: (0, 0))
    rowb = lambda n: pl.BlockSpec((tm, n), lambda i: (i, 0))
    sds = lambda n, dt: jax.ShapeDtypeStruct((s_len, n), dt)
    return pl.pallas_call(
        body, name="mla_prep_bwd", grid=(s_len // tm,),
        in_specs=[rowb(512), full((1, 256)), full((1, 128)), full((256, 1024)), full((128, 1024)),
                  full((128, 512)), rowb(128), rowb(128), rowb(1024), rowb(1024), rowb(512)],
        out_specs=[rowb(512), rowb(1024), rowb(1024), rowb(512), full((1, 256)), full((1, 128))],
        out_shape=[sds(512, F32), sds(1024, BF16), sds(1024, BF16), sds(512, BF16),
                   jax.ShapeDtypeStruct((1, 256), F32), jax.ShapeDtypeStruct((1, 128), F32)],
        compiler_params=_cparams(48, dimension_semantics=("arbitrary",)),
    )(lat, g_q, g_kv, w_uq, w_uk, w_uv, cosf, sinf, dq, dk, dv)


def _in_proj_bwd(x, g, w, dx1, dlat, dsbq, dsbk, dsbv, dgates):
    s_len = x.shape[0]
    tm, tn = _row_block(s_len), 512
    nj = D_IN_PAD // tn

    def body(x_ref, g_ref, w_ref, dx1_ref, dlat_ref, dq_ref, dk_ref, dv_ref, dgate_ref,
             gx_ref, dproj_ref, dg_ref, dh_scr):
        i, j = pl.program_id(0), pl.program_id(1)

        @pl.when(j == 0)
        def _():
            dh_scr[...] = jnp.zeros_like(dh_scr)

        def chunk(val):
            vb = val.astype(BF16)
            dproj_ref[...] = vb
            dh_scr[...] += _dot_nt(vb, w_ref[...])

        for jj, ref in ((0, dlat_ref), (1, dq_ref), (3, dv_ref)):
            pl.when(j == jj)(functools.partial(lambda ref: chunk(ref[...]), ref))
        pl.when(j == 2)(lambda: chunk(dk_ref[...] * LN2))
        pl.when(j >= 4)(lambda: chunk(dgate_ref[...]))

        @pl.when(j == nj - 1)
        def _():
            gv = g_ref[...]
            _, xh, r = _rms(x_ref[...], gv)
            dx, dgr = _rms_bwd(dh_scr[...], xh, r, gv)
            gx_ref[...] = dx1_ref[...] + dx
            _acc_rows(dg_ref, dgr, i == 0)

    rowb = pl.BlockSpec((tm, D_MODEL), lambda i, j: (i, 0))
    colb = lambda f: pl.BlockSpec((tm, tn), f)
    return pl.pallas_call(
        body, name="in_proj_bwd", grid=(s_len // tm, nj),
        in_specs=[rowb, pl.BlockSpec((1, D_MODEL), lambda i, j: (0, 0)),
                  pl.BlockSpec((D_MODEL, tn), lambda i, j: (0, j)), rowb,
                  colb(lambda i, j: (i, 0)), colb(lambda i, j: (i, 0)), colb(lambda i, j: (i, 0)),
                  colb(lambda i, j: (i, 0)), colb(lambda i, j: (i, jnp.clip(j - 4, 0, 3)))],
        out_specs=[rowb, colb(lambda i, j: (i, j)), pl.BlockSpec((1, D_MODEL), lambda i, j: (0, 0))],
        out_shape=[jax.ShapeDtypeStruct((s_len, D_MODEL), F32), jax.ShapeDtypeStruct((s_len, D_IN_PAD), BF16),
                   jax.ShapeDtypeStruct((1, D_MODEL), F32)],
        scratch_shapes=[pltpu.VMEM((tm, D_MODEL), F32)],
        compiler_params=_cparams(48, dimension_semantics=("arbitrary", "arbitrary")),
    )(x, g, w, dx1, dlat, dsbq, dsbk, dsbv, dgates)


def _adamw(landed, w, m, v, name):
    r, c = w.shape
    lanes = _round_up(c, LANES)
    tb = r
    for cand in range(r, 0, -1):
        if r % cand == 0 and (cand % 8 == 0 or cand == r) and N_DEV * cand * lanes * 4 <= ADAM_BLOCK_BYTES:
            tb = cand
            break
    c1 = 1.0 - ADAM_B1 ** ADAM_STEP
    c2 = 1.0 - ADAM_B2 ** ADAM_STEP

    def body(l_ref, w_ref, m_ref, v_ref, g_ref, d_ref, nm_ref, nv_ref):
        g = l_ref[0]
        for k in range(1, N_DEV):
            g = g + l_ref[k]
        nm = ADAM_B1 * m_ref[...] + (1.0 - ADAM_B1) * g
        nv = ADAM_B2 * v_ref[...] + (1.0 - ADAM_B2) * (g * g)
        g_ref[...] = g
        nm_ref[...] = nm
        nv_ref[...] = nv
        d_ref[...] = -ADAM_LR * ((nm / c1) / (jnp.sqrt(nv / c2) + ADAM_EPS) + ADAM_WD * w_ref[...])

    blk = pl.BlockSpec((tb, c), lambda i: (i, 0))
    return pl.pallas_call(
        body, name=name, grid=(r // tb,),
        in_specs=[pl.BlockSpec((N_DEV, tb, c), lambda i: (0, i, 0)), blk, blk, blk],
        out_specs=[blk, blk, blk, blk],
        out_shape=[jax.ShapeDtypeStruct((r, c), F32)] * 4,
        compiler_params=_cparams(dimension_semantics=("parallel",)),
    )(landed, w, m, v)


def _shard_shape(shape, axis):
    return tuple(d // N_DEV if a == axis else d for a, d in enumerate(shape))


def _split_pieces(full, axis):
    r, c = full.shape
    if axis == 0:
        return full.reshape(N_DEV, r // N_DEV, c)
    return full.reshape(r, N_DEV, c // N_DEV).transpose(1, 0, 2)


def _join_shards(gathered, axis):
    _, r, c = gathered.shape
    if axis == 0:
        return gathered.reshape(N_DEV * r, c)
    return gathered.transpose(1, 0, 2).reshape(r, N_DEV * c)


def kernel(x, mem, positions, g_mix, w_in, b_gate, g_q_lat, w_uq, g_kv_lat, w_ukv, w_a_proj, w_b_proj, w_o, g_x, g_mem, w_xq, w_xkv, w_xo, g_ffn, w_gate, w_up, w_down, g_final, loss_target, m_g_mix, m_w_in, m_b_gate, m_g_q_lat, m_w_uq, m_g_kv_lat, m_w_ukv, m_w_a_proj, m_w_b_proj, m_w_o, m_g_x, m_g_mem, m_w_xq, m_w_xkv, m_w_xo, m_g_ffn, m_w_gate, m_w_up, m_w_down, m_g_final, v_g_mix, v_w_in, v_b_gate, v_g_q_lat, v_w_uq, v_g_kv_lat, v_w_ukv, v_w_a_proj, v_w_b_proj, v_w_o, v_g_x, v_g_mem, v_w_xq, v_w_xkv, v_w_xo, v_g_ffn, v_w_gate, v_w_up, v_w_down, v_g_final):
    given = dict(locals())
    s_len = x.shape[1]
    x2d = x.reshape(s_len, D_MODEL)
    mem2d = mem.reshape(-1, D_MODEL)
    target = loss_target.reshape(s_len, D_MODEL)

    names = [name for name, _, _ in SHARDED]
    axis_of = {name: axis for name, _, axis in SHARDED}
    shard2d = lambda name, prefix="": given[prefix + name].reshape(
        _shard_shape(dict((n, s) for n, s, _ in SHARDED)[name], axis_of[name]))

    wire = lambda name: shard2d(name) if name == "b_gate" else shard2d(name).astype(BF16)
    early = [n for n in names if n in NEEDED_FIRST]
    late = [n for n in names if n not in NEEDED_FIRST]
    gathered = _exchange(True, [wire(n) for n in early], "weights_gather_first")
    wts = {n: _join_shards(g, axis_of[n]) for n, g in zip(early, gathered)}

    w_in_p = jnp.concatenate([wts["w_in"][:, :416], jnp.zeros((D_MODEL, 96), BF16), wts["w_in"][:, 416:]], axis=1)
    w_uq_p = jnp.pad(wts["w_uq"].reshape(256, MLA_HEADS, 96), ((0, 0), (0, 0), (0, 32))).reshape(256, 1024)
    ukv = wts["w_ukv"].reshape(128, MLA_HEADS, 128)
    w_uk_p = jnp.pad(ukv[:, :, :64], ((0, 0), (0, 0), (0, 64))).reshape(128, 1024)
    w_uv = ukv[:, :, 64:].reshape(128, 512)
    bg = wts["b_gate"]

    inv_freq = ROPE_THETA ** (-jnp.arange(0, MLA_ROPE, 2, dtype=F32) / MLA_ROPE)
    ang = positions.reshape(s_len).astype(F32)[:, None] * inv_freq
    cos16, sin16 = jnp.cos(ang), jnp.sin(ang)
    cosf = jnp.concatenate([jnp.ones((s_len, 64), F32), cos16, cos16, jnp.ones((s_len, 32), F32)], axis=1)
    sinf = jnp.concatenate([jnp.zeros((s_len, 64), F32), sin16, sin16, jnp.zeros((s_len, 32), F32)], axis=1)

    h1, lat, sb, gates = _in_proj(x2d, g_mix, w_in_p)
    qa, ka, va, q_lat, kv_lat = _mla_prep(lat, g_q_lat, g_kv_lat, w_uq_p, w_uk_p, w_uv, cosf, sinf)
    oa, lse, gathered = _mla_fwd(qa, ka, va, _Exchange(True, [wire(n) for n in late]))
    wts.update({n: _join_shards(g, axis_of[n]) for n, g in zip(late, gathered)})
    ob, sb_r = _sb_fwd(sb)
    x1 = _merge_fwd(x2d, oa, ob, gates, bg, wts["w_a_proj"], wts["w_b_proj"], wts["w_o"])
    mn, xkv = _mem_kv(mem2d, g_mem, wts["w_xkv"])
    x2 = _xattn_fwd(x1, g_x, wts["w_xq"], xkv, wts["w_xo"])
    x3 = _ffn_fwd(x2, g_ffn, wts["w_gate"], wts["w_up"], wts["w_down"])
    g_final2d = g_final.reshape(1, D_MODEL)
    sse, dx3, dx3b, dg_final = _loss_head(x3, g_final2d, target)
    loss = lax.psum(sse[0, 0] * (0.5 / D_MODEL), ("x", "y", "c"))

    dx2, hf, dgt, dup, act, dg_ffn = _ffn_bwd(x2, dx3, g_ffn, wts["w_gate"], wts["w_up"], wts["w_down"])
    dx1, dw_xq, dw_xo, dxkv, dg_x = _xattn_bwd(x1, dx2, g_x, wts["w_xq"], xkv, wts["w_xo"])
    dw_xkv, dg_mem = _mem_bwd(mem2d, g_mem, wts["w_xkv"], mn, dxkv)
    doa, dob, dgates, dpa, dpb, merged, dx1b, dbg = _merge_bwd(
        dx1, oa, ob, gates, bg, wts["w_a_proj"], wts["w_b_proj"], wts["w_o"])
    dsbq, dsbk, dsbv = _sb_bwd(sb, dob, sb_r)
    full_grads = {
        "w_a_proj": _tn_matmul(oa, dpa, "dw_a"),
        "w_b_proj": _tn_matmul(ob, dpb, "dw_b"),
        "w_o": _tn_matmul(merged, dx1b, "dw_o"),
        "w_xq": dw_xq,
        "w_xkv": dw_xkv,
        "w_xo": dw_xo,
        "w_gate": _tn_matmul(hf, dgt, "dw_gate", tn=FF_TILE),
        "w_up": _tn_matmul(hf, dup, "dw_up", tn=FF_TILE),
        "w_down": _tn_matmul(act, dx3b, "dw_down", tka=FF_TILE),
    }
    dqa, dka, dva, got = _mla_bwd(
        qa, ka, va, oa, doa, lse, _Exchange(False, [_split_pieces(full_grads[n], axis_of[n]) for n in late]))
    landed = dict(zip(late, got))
    dlat, dqb, dkb, dvb, dg_q, dg_kv = _mla_prep_bwd(
        lat, g_q_lat, g_kv_lat, w_uq_p, w_uk_p, w_uv, cosf, sinf, dqa, dka, dva)
    grad_x, dproj, dg_mix = _in_proj_bwd(x2d, g_mix, w_in_p, dx1, dlat, dsbq, dsbk, dsbv, dgates)
    dw_in_p = _tn_matmul(h1, dproj, "dw_in")
    dw_uq_p = _tn_matmul(q_lat, dqb, "dw_uq")
    dw_uk_p = _tn_matmul(kv_lat, dkb, "dw_uk")
    dw_uv = _tn_matmul(kv_lat, dvb, "dw_uv")
    full_grads.update({
        "w_in": jnp.concatenate([dw_in_p[:, :416], dw_in_p[:, 512:]], axis=1),
        "b_gate": dbg,
        "w_uq": dw_uq_p.reshape(256, MLA_HEADS, 128)[:, :, :96].reshape(256, 768),
        "w_ukv": jnp.concatenate([dw_uk_p.reshape(128, MLA_HEADS, 128)[:, :, :64],
                                  dw_uv.reshape(128, MLA_HEADS, 64)], axis=2).reshape(128, 1024),
    })
    rep_grads = {"g_mix": dg_mix, "g_q_lat": dg_q, "g_kv_lat": dg_kv, "g_x": dg_x, "g_mem": dg_mem,
                 "g_ffn": dg_ffn, "g_final": dg_final}
    rep_cat = lambda prefix, src: jnp.concatenate(
        [src[prefix + n].reshape(-1) for n, _ in REPLICATED]).reshape(-1, LANES)
    rep_src = jnp.broadcast_to(rep_cat("", rep_grads), (N_DEV,) + rep_cat("", rep_grads).shape)
    got = _exchange(False, [_split_pieces(full_grads[n], axis_of[n]) for n in early] + [rep_src], "grads_last")
    landed.update(zip(early, got[:-1]))
    rep_landed = got[-1]

    res = {}
    for name, _, _ in SHARDED:
        outs = _adamw(landed[name], shard2d(name), shard2d(name, "m_"), shard2d(name, "v_"), "adamw_" + name)
        res[name] = [o.reshape(given[name].shape) for o in outs]
    rep_outs = _adamw(rep_landed, rep_cat("", given), rep_cat("m_", given), rep_cat("v_", given), "adamw_gains")
    off = 0
    for name, n in REPLICATED:
        res[name] = [o.reshape(-1)[off:off + n].reshape(given[name].shape) for o in rep_outs]
        off += n
    result = [loss, grad_x.reshape(x.shape)]
    for k in range(4):
        result.extend(res[name][k] for name in WEIGHT_ORDER)
    return tuple(result)
```

```python
import functools
import math

import jax
import jax.numpy as jnp
from jax import lax
from jax.experimental import pallas as pl
from jax.experimental.pallas import tpu as pltpu

F32 = jnp.float32
BF16 = jnp.bfloat16

D_MODEL = 1024
MLA_HEADS = 8
MLA_Q_RANK = 256
MLA_KV_RANK = 128
MLA_NOPE = 64
MLA_ROPE = 32
MLA_V = 64
ROPE_THETA = 10000.0
SB_WIDTH = 512
X_HEADS = 4
X_HEAD_DIM = 128
D_FF = 2816
EPS = 1e-6
D_IN = 4000
D_IN_PAD = 4096
K_R_OFF = 384
LANES = 128
HEAD_PAD = 128
MLA_SCALE = 1.0 / math.sqrt(MLA_NOPE + MLA_ROPE)
SB_SCALE = 0.125
LOG2E = math.log2(math.e)
LN2 = math.log(2.0)
MLA_Q_FOLD = MLA_SCALE * LOG2E
SB_Q_FOLD = SB_SCALE * LOG2E
SB_CUT = -160.0
X_SCALE = 1.0 / math.sqrt(X_HEAD_DIM)
NEG_BIG = -1e30

ADAM_LR = 0.001
ADAM_B1 = 0.9
ADAM_B2 = 0.999
ADAM_EPS = 1e-08
ADAM_WD = 0.01
ADAM_STEP = 10

N_DEV = 8
MIB = 1024 * 1024
ADAM_BLOCK_BYTES = 4 * MIB

SHARDED = (
    ("w_in", (D_MODEL, D_IN), 1),
    ("b_gate", (2, D_MODEL), 1),
    ("w_uq", (MLA_Q_RANK, 768), 1),
    ("w_ukv", (MLA_KV_RANK, 1024), 1),
    ("w_a_proj", (512, D_MODEL), 1),
    ("w_b_proj", (512, D_MODEL), 1),
    ("w_o", (D_MODEL, D_MODEL), 0),
    ("w_xq", (D_MODEL, 512), 0),
    ("w_xkv", (D_MODEL, 1024), 0),
    ("w_xo", (512, D_MODEL), 1),
    ("w_gate", (D_MODEL, D_FF), 1),
    ("w_up", (D_MODEL, D_FF), 1),
    ("w_down", (D_FF, D_MODEL), 0),
)
NEEDED_FIRST = ("w_in", "b_gate", "w_uq", "w_ukv")
REPLICATED = (
    ("g_mix", 1024), ("g_q_lat", 256), ("g_kv_lat", 128), ("g_x", 1024),
    ("g_mem", 1024), ("g_ffn", 1024), ("g_final", 1024),
)
WEIGHT_ORDER = ("g_mix", "w_in", "b_gate", "g_q_lat", "w_uq", "g_kv_lat", "w_ukv", "w_a_proj",
                "w_b_proj", "w_o", "g_x", "g_mem", "w_xq", "w_xkv", "w_xo", "g_ffn", "w_gate",
                "w_up", "w_down", "g_final")


def _round_up(n, m):
    return -(-n // m) * m


def _cparams(vmem_mib=None, **kw):
    if vmem_mib is not None:
        kw["vmem_limit_bytes"] = vmem_mib * MIB
    return pltpu.CompilerParams(**kw)


def _dot(a, b):
    return jnp.dot(a, b, preferred_element_type=F32)


def _dot_nt(a, b):
    return lax.dot_general(a, b, (((1,), (1,)), ((), ())), preferred_element_type=F32)


def _dot_tn(a, b):
    return lax.dot_general(a, b, (((0,), (0,)), ((), ())), preferred_element_type=F32)


def _rms(x, g):
    r = lax.rsqrt(jnp.mean(x * x, axis=-1, keepdims=True) + EPS)
    xh = x * r
    return xh * g, xh, r


def _rms_bwd(dy, xh, r, g):
    u = dy * g
    dx = r * (u - xh * jnp.mean(u * xh, axis=-1, keepdims=True))
    return dx, dy * xh


def _sigmoid(z):
    return 1.0 / (1.0 + jnp.exp(-z))


def _acc_rows(ref, val, first):
    s = jnp.sum(val, axis=0, keepdims=True)

    @pl.when(first)
    def _():
        ref[...] = s

    @pl.when(jnp.logical_not(first))
    def _():
        ref[...] += s


def _acc(ref, val, first):
    @pl.when(first)
    def _():
        ref[...] = val

    @pl.when(jnp.logical_not(first))
    def _():
        ref[...] += val


def _peer(k):
    x, y, c = lax.axis_index("x"), lax.axis_index("y"), lax.axis_index("c")
    px = 1 - x if (k >> 2) & 1 else x
    py = 1 - y if (k >> 1) & 1 else y
    pc = 1 - c if k & 1 else c
    return (px, py, pc), 4 * px + 2 * py + pc


N_PEERS = N_DEV - 1
OTHER_CHIPS = (2, 4, 6)


def _land_shape(gather, src):
    return (N_DEV,) + src.shape if gather else src.shape


class _Exchange:
    def __init__(self, gather, srcs):
        self.gather, self.n, self.srcs = gather, len(srcs), list(srcs)
        self.out_shape = [jax.ShapeDtypeStruct(_land_shape(gather, s), s.dtype) for s in srcs]
        self.specs = [pl.BlockSpec(memory_space=pl.ANY)] * self.n
        self.scratch = [pltpu.SemaphoreType.DMA((self.n * N_PEERS,)), pltpu.SemaphoreType.DMA((self.n * N_PEERS,)),
                        pltpu.SemaphoreType.DMA((self.n,))]

    def bind(self, src, land, sems):
        self.src, self.land = src, land
        self.send_sems, self.recv_sems, self.local_sems = sems

    def _copy(self, a, k, source, to, target=1):
        return pltpu.make_async_remote_copy(
            src_ref=source, dst_ref=to,
            send_sem=self.send_sems.at[a * N_PEERS + k - 1], recv_sem=self.recv_sems.at[a * N_PEERS + k - 1],
            device_id=_peer(target)[0], device_id_type=pl.DeviceIdType.MESH)

    def _row(self, a, k):
        return self.land[a].at[_peer(k)[1]]

    def _mine(self, a):
        me = _peer(0)[1]
        return pltpu.make_async_copy(self.src[a] if self.gather else self.src[a].at[me], self.land[a].at[me],
                                     self.local_sems.at[a])

    def issue(self):
        me = _peer(0)[1]
        for a in range(self.n):
            self._mine(a).start()
            for k in ((1,) + OTHER_CHIPS if self.gather else range(1, N_DEV)):
                source = self.src[a] if self.gather else self.src[a].at[_peer(k)[1]]
                self._copy(a, k, source, self.land[a].at[me], target=k).start()

    def finish(self):
        me = _peer(0)[1]
        part = lambda a: self.src[a] if self.gather else self.src[a].at[me]
        if self.gather:
            for a in range(self.n):
                for k in OTHER_CHIPS:
                    self._copy(a, k, part(a), self._row(a, k)).wait_recv()
                    self._copy(a, k + 1, self._row(a, k), self._row(a, k), target=1).start()
        for a in range(self.n):
            for k in ((1, 3, 5, 7) if self.gather else range(1, N_DEV)):
                self._copy(a, k, part(a), self._row(a, k)).wait_recv()
        for a in range(self.n):
            for k in range(1, N_DEV):
                self._copy(a, k, part(a), self.land[a].at[me]).wait_send()
            self._mine(a).wait()


def _exchange(gather, srcs, name):
    ex = _Exchange(gather, srcs)

    def body(*refs):
        ex.bind(refs[:ex.n], refs[ex.n:2 * ex.n], refs[2 * ex.n:])
        ex.issue()
        ex.finish()

    return pl.pallas_call(body, name=name, out_shape=ex.out_shape, in_specs=ex.specs, out_specs=ex.specs,
                          scratch_shapes=ex.scratch)(*ex.srcs)


def _tn_matmul(a, b, name, tka=512, tn=1024, ts=2048):
    s_len, ka = a.shape
    n = b.shape[1]
    tka, tn, ts = min(tka, ka), min(tn, n), min(ts, s_len)
    assert ka % tka == 0 and n % tn == 0 and s_len % ts == 0

    def body(a_ref, b_ref, o_ref):
        _acc(o_ref, _dot_tn(a_ref[...], b_ref[...]), pl.program_id(2) == 0)

    return pl.pallas_call(
        body, name=name, grid=(ka // tka, n // tn, s_len // ts),
        in_specs=[pl.BlockSpec((ts, tka), lambda i, j, s: (s, i)),
                  pl.BlockSpec((ts, tn), lambda i, j, s: (s, j))],
        out_specs=pl.BlockSpec((tka, tn), lambda i, j, s: (i, j)),
        out_shape=jax.ShapeDtypeStruct((ka, n), F32),
        compiler_params=_cparams(dimension_semantics=("parallel", "parallel", "arbitrary")),
    )(a, b)


def _row_block(s_len):
    return min(s_len, 512)


def _in_proj(x, g, w):
    s_len = x.shape[0]
    tm = _row_block(s_len)

    def body(x_ref, g_ref, w_ref, h_ref, lat_ref, sb_ref, gate_ref):
        h, _, _ = _rms(x_ref[...], g_ref[...])
        hb = h.astype(BF16)
        h_ref[...] = hb
        p = _dot(hb, w_ref[:, 0:1024])
        lat_ref[...] = p[:, 0:512]
        sb_ref[:, 0:512] = (p[:, 512:1024] * SB_Q_FOLD).astype(BF16)
        sb_ref[:, 512:1536] = _dot(hb, w_ref[:, 1024:2048]).astype(BF16)
        gate_ref[:, 0:1024] = _dot(hb, w_ref[:, 2048:3072])
        gate_ref[:, 1024:2048] = _dot(hb, w_ref[:, 3072:4096])

    rowb = lambda n: pl.BlockSpec((tm, n), lambda i: (i, 0))
    return pl.pallas_call(
        body, name="in_proj", grid=(s_len // tm,),
        in_specs=[rowb(D_MODEL), pl.BlockSpec((1, D_MODEL), lambda i: (0, 0)),
                  pl.BlockSpec((D_MODEL, D_IN_PAD), lambda i: (0, 0))],
        out_specs=[rowb(D_MODEL), rowb(512), rowb(3 * SB_WIDTH), rowb(2 * D_MODEL)],
        out_shape=[jax.ShapeDtypeStruct((s_len, D_MODEL), BF16),
                   jax.ShapeDtypeStruct((s_len, 512), F32),
                   jax.ShapeDtypeStruct((s_len, 3 * SB_WIDTH), BF16),
                   jax.ShapeDtypeStruct((s_len, 2 * D_MODEL), F32)],
        compiler_params=_cparams(48, dimension_semantics=("parallel",)),
    )(x, g, w)


def _rope_rot(blk, lane):
    return jnp.where(lane < 80, -pltpu.roll(blk, 112, 1), pltpu.roll(blk, 16, 1))


def _rope_rot_t(blk, lane):
    return jnp.where(lane < 80, pltpu.roll(blk, 112, 1), -pltpu.roll(blk, 16, 1))


def _mla_prep(lat, g_q, g_kv, w_uq, w_uk, w_uv, cosf, sinf):
    s_len = lat.shape[0]
    tm = _row_block(s_len)

    def body(lat_ref, gq_ref, gkv_ref, wuq_ref, wuk_ref, wuv_ref, cos_ref, sin_ref,
             q_ref, k_ref, v_ref, ql_ref, kvl_ref):
        lane = lax.broadcasted_iota(jnp.int32, (tm, LANES), 1)
        cosv, sinv = cos_ref[...], sin_ref[...]
        ql, _, _ = _rms(lat_ref[:, 0:256], gq_ref[...])
        kvl, _, _ = _rms(lat_ref[:, 256:384], gkv_ref[...])
        qlb, kvlb = ql.astype(BF16), kvl.astype(BF16)
        ql_ref[...] = qlb
        kvl_ref[...] = kvlb
        q = _dot(qlb, wuq_ref[...])
        kn = _dot(kvlb, wuk_ref[...])
        v_ref[...] = _dot(kvlb, wuv_ref[...]).astype(BF16)
        kr = pltpu.roll(lat_ref[:, K_R_OFF:K_R_OFF + LANES], 64, 1)
        kr = kr * cosv + _rope_rot(kr, lane) * sinv
        for h in range(MLA_HEADS):
            sl = slice(h * HEAD_PAD, (h + 1) * HEAD_PAD)
            blk = q[:, sl]
            q_ref[:, sl] = ((blk * cosv + _rope_rot(blk, lane) * sinv) * MLA_Q_FOLD).astype(BF16)
            k_ref[:, sl] = (kn[:, sl] + kr).astype(BF16)

    full = lambda shape: pl.BlockSpec(shape, lambda i: (0, 0))
    rowb = lambda n: pl.BlockSpec((tm, n), lambda i: (i, 0))
    return pl.pallas_call(
        body, name="mla_prep", grid=(s_len // tm,),
        in_specs=[rowb(512), full((1, 256)), full((1, 128)), full((256, 1024)), full((128, 1024)),
                  full((128, 512)), rowb(128), rowb(128)],
        out_specs=[rowb(1024), rowb(1024), rowb(512), rowb(256), rowb(128)],
        out_shape=[jax.ShapeDtypeStruct((s_len, 1024), BF16), jax.ShapeDtypeStruct((s_len, 1024), BF16),
                   jax.ShapeDtypeStruct((s_len, 512), BF16), jax.ShapeDtypeStruct((s_len, 256), BF16),
                   jax.ShapeDtypeStruct((s_len, 128), BF16)],
        compiler_params=_cparams(dimension_semantics=("parallel",)),
    )(lat, g_q, g_kv, w_uq, w_uk, w_uv, cosf, sinf)


ATTN_TQ = 1024
ATTN_TH = 512
ATTN_TK = 256
MLA_TK = 512


def _attn_blocks(s_len, tk=ATTN_TK):
    tq, th, tk = min(s_len, ATTN_TQ), min(s_len, ATTN_TH), min(s_len, tk)
    return tq, th, tk, tq // tk


def _chains(tq, th):
    return [(hh, r0) for hh in range(2) for r0 in range(0, tq, th)]


def _diag_mask(th, tk, r0, sub, strict):
    lo, hi = sub * tk, (sub + 1) * tk - 1
    last, first = r0 + th - 1, r0
    if (lo >= last) if strict else (lo > last):
        return "none"
    if (hi < first) if strict else (hi <= first):
        return "all"
    row = lax.broadcasted_iota(jnp.int32, (th, tk), 0) + r0
    col = lax.broadcasted_iota(jnp.int32, (th, tk), 1) + lo
    return col < row if strict else col <= row


def _mla_fwd(q, k, v, ride):
    s_len = q.shape[0]
    tq, th, tk, nsub = _attn_blocks(s_len, MLA_TK)
    chains = _chains(tq, th)
    nh = tq // th
    nq = s_len // tq

    def body(q_ref, k_ref, v_ref, *rest):
        o_ref, lse_ref = rest[ride.n:ride.n + 2]
        ride.bind(rest[:ride.n], rest[ride.n + 2:2 * ride.n + 2], rest[2 * ride.n + 2:])
        pl.when((pl.program_id(0) == 0) & (pl.program_id(1) == 0))(ride.issue)
        i = pl.program_id(1)
        lane = lax.broadcasted_iota(jnp.int32, (th, LANES), 1)
        hsl = [slice(hh * HEAD_PAD, (hh + 1) * HEAD_PAD) for hh in range(2)]

        def step(kb, carry, sub):
            rows = pl.ds(pl.multiple_of(kb * tk, tk), tk)
            vblk = v_ref[rows, :]
            masks = ["all" if sub is None else _diag_mask(th, tk, r0, sub, strict=False) for _, r0 in chains]
            live = [n for n, m in enumerate(masks) if not (isinstance(m, str) and m == "none")]
            s = {n: _dot_nt(q_ref[chains[n][1]:chains[n][1] + th, hsl[chains[n][0]]], k_ref[rows, hsl[chains[n][0]]])
                 for n in live}
            new = list(carry)
            pb, alpha = {}, {}
            for n in live:
                m, l, _ = carry[n]
                sn = s[n]
                if not isinstance(masks[n], str):
                    sn = jnp.where(masks[n], sn, NEG_BIG)
                m_new = jnp.maximum(m, jnp.max(sn, axis=-1, keepdims=True))
                alpha[n] = jnp.exp2(m - m_new)
                p = jnp.exp2(sn - m_new)
                pb[n] = p.astype(BF16)
                new[n] = (m_new, alpha[n] * l + jnp.sum(p, axis=-1, keepdims=True), None)
            pv = {n: _dot(pb[n], vblk) for n in live}
            for n in live:
                new[n] = (new[n][0], new[n][1], alpha[n] * carry[n][2] + pv[n])
            return tuple(new)

        init = (jnp.full((th, 1), NEG_BIG, F32), jnp.zeros((th, 1), F32), jnp.zeros((th, LANES), F32))
        carry = lax.fori_loop(0, i * nsub, lambda kb, cy: step(kb, cy, None), (init,) * len(chains))
        for sub in range(nsub):
            carry = step(i * nsub + sub, carry, sub)
        for c in range(nh):
            (m0, l0, a0), (m1, l1, a1) = carry[c], carry[nh + c]
            rs = slice(c * th, (c + 1) * th)
            o_ref[rs, :] = jnp.where(lane < 64, a0 / l0, a1 / l1).astype(BF16)
            lse_ref[rs, :] = jnp.where(lane < 64, m0 + jnp.log2(l0), m1 + jnp.log2(l1))
        pl.when((pl.program_id(0) == 3) & (i == nq - 1))(ride.finish)

    outs = pl.pallas_call(
        body, name="mla_fwd", grid=(4, nq),
        in_specs=[pl.BlockSpec((tq, 2 * HEAD_PAD), lambda p, i: (i, p)),
                  pl.BlockSpec((s_len, 2 * HEAD_PAD), lambda p, i: (0, p)),
                  pl.BlockSpec((s_len, LANES), lambda p, i: (0, p))] + ride.specs,
        out_specs=[pl.BlockSpec((tq, LANES), lambda p, i: (i, p)),
                   pl.BlockSpec((None, tq, LANES), lambda p, i: (p, i, 0))] + ride.specs,
        out_shape=[jax.ShapeDtypeStruct((s_len, 512), BF16),
                   jax.ShapeDtypeStruct((4, s_len, LANES), F32)] + ride.out_shape,
        scratch_shapes=ride.scratch,
        compiler_params=_cparams(40, dimension_semantics=("arbitrary", "arbitrary")),
    )(q, k, v, *ride.srcs)
    return outs[0], outs[1], outs[2:]


def _mla_bwd(q, k, v, o, do, lse, ride):
    s_len = q.shape[0]
    tq, th, tk, nsub = _attn_blocks(s_len, MLA_TK)
    chains = _chains(tq, th)
    nq = s_len // tq

    def body(q_ref, k_ref, v_ref, o_ref, do_ref, lse_ref, *rest):
        dq_ref, dk_ref, dv_ref = rest[ride.n:ride.n + 3]
        ride.bind(rest[:ride.n], rest[ride.n + 3:2 * ride.n + 3], rest[2 * ride.n + 3:])
        pl.when((pl.program_id(0) == 0) & (pl.program_id(1) == 0))(ride.issue)
        i = pl.program_id(1)
        lane = lax.broadcasted_iota(jnp.int32, (th, LANES), 1)

        @pl.when(i == 0)
        def _():
            dk_ref[...] = jnp.zeros_like(dk_ref)
            dv_ref[...] = jnp.zeros_like(dv_ref)

        hsl = [slice(hh * HEAD_PAD, (hh + 1) * HEAD_PAD) for hh in range(2)]
        qs, dos, deltas, lses = [], [], [], []
        for hh, r0 in chains:
            rs = slice(r0, r0 + th)
            qs.append(q_ref[rs, hsl[hh]])
            doh = jnp.where((lane // 64) == hh, do_ref[rs, :], jnp.zeros((), BF16))
            dos.append(doh)
            deltas.append(jnp.sum(doh.astype(F32) * o_ref[rs, :].astype(F32), axis=-1, keepdims=True))
            lses.append(lse_ref[rs, 64 * hh:64 * hh + 1])

        def step(kb, dqs, sub):
            rows = pl.ds(pl.multiple_of(kb * tk, tk), tk)
            vblk = v_ref[rows, :]
            new, p_all, do_all = [], [], []
            ds_h, q_h = [[], []], [[], []]
            for c, (hh, r0) in enumerate(chains):
                mask = "all" if sub is None else _diag_mask(th, tk, r0, sub, strict=False)
                if isinstance(mask, str) and mask == "none":
                    new.append(dqs[c])
                    continue
                kblk = k_ref[rows, hsl[hh]]
                s = _dot_nt(qs[c], kblk)
                if not isinstance(mask, str):
                    s = jnp.where(mask, s, NEG_BIG)
                p = jnp.exp2(s - lses[c])
                dp = _dot_nt(dos[c], vblk)
                ds = (p * (dp - deltas[c]) * MLA_SCALE).astype(BF16)
                p_all.append(p.astype(BF16))
                do_all.append(dos[c])
                ds_h[hh].append(ds)
                q_h[hh].append(qs[c])
                new.append(dqs[c] + _dot(ds, kblk))
            dv_ref[rows, :] += _dot_tn(jnp.concatenate(p_all, axis=0), jnp.concatenate(do_all, axis=0))
            for hh in range(2):
                dk_ref[rows, hsl[hh]] += _dot_tn(jnp.concatenate(ds_h[hh], axis=0),
                                                 jnp.concatenate(q_h[hh], axis=0))
            return tuple(new)

        zero = jnp.zeros((th, LANES), F32)
        dqs = lax.fori_loop(0, i * nsub, lambda kb, cy: step(kb, cy, None), (zero,) * len(chains))
        for sub in range(nsub):
            dqs = step(i * nsub + sub, dqs, sub)
        for c, (hh, r0) in enumerate(chains):
            dq_ref[r0:r0 + th, hsl[hh]] = dqs[c]
        pl.when((pl.program_id(0) == 3) & (i == nq - 1))(ride.finish)

    outs = pl.pallas_call(
        body, name="mla_bwd", grid=(4, nq),
        in_specs=[pl.BlockSpec((tq, 2 * HEAD_PAD), lambda p, i: (i, p)),
                  pl.BlockSpec((s_len, 2 * HEAD_PAD), lambda p, i: (0, p)),
                  pl.BlockSpec((s_len, LANES), lambda p, i: (0, p)),
                  pl.BlockSpec((tq, LANES), lambda p, i: (i, p)),
                  pl.BlockSpec((tq, LANES), lambda p, i: (i, p)),
                  pl.BlockSpec((None, tq, LANES), lambda p, i: (p, i, 0))] + ride.specs,
        out_specs=[pl.BlockSpec((tq, 2 * HEAD_PAD), lambda p, i: (i, p)),
                   pl.BlockSpec((s_len, 2 * HEAD_PAD), lambda p, i: (0, p)),
                   pl.BlockSpec((s_len, LANES), lambda p, i: (0, p))] + ride.specs,
        out_shape=[jax.ShapeDtypeStruct((s_len, 1024), F32), jax.ShapeDtypeStruct((s_len, 1024), F32),
                   jax.ShapeDtypeStruct((s_len, 512), F32)] + ride.out_shape,
        scratch_shapes=ride.scratch,
        compiler_params=_cparams(56, dimension_semantics=("arbitrary", "arbitrary")),
    )(q, k, v, o, do, lse, *ride.srcs)
    return outs[0], outs[1], outs[2], outs[3:]


def _log_sigmoids(z2):
    sp = jnp.log2(1.0 + jnp.exp2(-jnp.abs(z2)))
    lb = jnp.minimum(z2, 0.0) - sp
    return lb, lb - z2


def _split_dot(x, w, parts, nt=False):
    dot = _dot_nt if nt else _dot
    out = None
    for _ in range(parts):
        xb = x.astype(BF16)
        t = dot(xb, w)
        out = t if out is None else out + t
        x = x - xb.astype(F32)
    return out


def _sb_fwd(sb):
    s_len = sb.shape[0]
    tq, th, tk, nsub = _attn_blocks(s_len)
    chains = _chains(tq, th)
    nh = tq // th
    assert s_len // tk <= 64

    def body(q_ref, k_ref, v_ref, o_ref, r_ref):
        i = pl.program_id(1)
        lane = lax.broadcasted_iota(jnp.int32, (th, LANES), 1)
        upper = (lax.broadcasted_iota(jnp.int32, (tk, tk), 0)
                 > lax.broadcasted_iota(jnp.int32, (tk, tk), 1)).astype(BF16)
        qs = [jnp.where((lane // 64) == hh, q_ref[r0:r0 + th, :], jnp.zeros((), BF16)) for hh, r0 in chains]

        def step(kb, carry, sub):
            rows = pl.ds(pl.multiple_of(kb * tk, tk), tk)
            kblk, vblk = k_ref[rows, :], v_ref[rows, :]
            masks = ["all" if sub is None else _diag_mask(th, tk, r0, sub, strict=True) for _, r0 in chains]
            live = [n for n, m in enumerate(masks) if not (isinstance(m, str) and m == "none")]
            masked = {n: not isinstance(masks[n], str) for n in live}
            z = {n: _dot_nt(qs[n], kblk) for n in live}
            lb, lom = {}, {}
            for n in live:
                lb[n], lom[n] = _log_sigmoids(z[n])
                if masked[n]:
                    lom[n] = jnp.where(masks[n], lom[n], 0.0)
            suf = {n: _split_dot(lom[n], upper, 2) for n in live}
            a = {}
            for n in live:
                a[n] = jnp.exp2(lb[n] + suf[n] + carry[n][0])
                if masked[n]:
                    a[n] = jnp.where(masks[n], a[n], 0.0)
            pv = {n: _dot(a[n].astype(BF16), vblk) for n in live}
            new = list(carry)
            for n in live:
                c, acc, r = carry[n]
                rs = suf[n][:, 0:1] + lom[n][:, 0:1]
                new[n] = (c + rs, acc + pv[n], jnp.where(lane == 64 * chains[n][0] + kb, rs, r))
            return tuple(new)

        init = (jnp.zeros((th, 1), F32), jnp.zeros((th, LANES), F32), jnp.zeros((th, LANES), F32))
        carry = (init,) * len(chains)
        for sub in reversed(range(nsub)):
            carry = step(i * nsub + sub, carry, sub)

        def spent(cy):
            top = functools.reduce(jnp.maximum, [jnp.max(c) for c, _, _ in cy])
            return (top < SB_CUT).astype(jnp.int32)

        def walk(state):
            t, _, cy = state
            cy = step(i * nsub - 1 - t, cy, None)
            return t + 1, spent(cy), cy

        _, _, carry = lax.while_loop(lambda st: (st[0] < i * nsub) & (st[1] == 0), walk,
                                     (jnp.int32(0), spent(carry), carry))
        for n in range(nh):
            rs = slice(n * th, (n + 1) * th)
            o_ref[rs, :] = jnp.where(lane < 64, carry[n][1], carry[nh + n][1]).astype(BF16)
            r_ref[rs, :] = jnp.where(lane < 64, carry[n][2], carry[nh + n][2])

    return pl.pallas_call(
        body, name="sb_fwd", grid=(4, s_len // tq),
        in_specs=[pl.BlockSpec((tq, LANES), lambda p, i: (i, p)),
                  pl.BlockSpec((s_len, LANES), lambda p, i: (0, 4 + p)),
                  pl.BlockSpec((s_len, LANES), lambda p, i: (0, 8 + p))],
        out_specs=[pl.BlockSpec((tq, LANES), lambda p, i: (i, p)),
                   pl.BlockSpec((None, tq, LANES), lambda p, i: (p, i, 0))],
        out_shape=[jax.ShapeDtypeStruct((s_len, 512), BF16), jax.ShapeDtypeStruct((4, s_len, LANES), F32)],
        compiler_params=_cparams(40, dimension_semantics=("parallel", "arbitrary")),
    )(sb, sb, sb)


def _sb_bwd(sb, do, r):
    s_len = sb.shape[0]
    tq, th, tk, nsub = _attn_blocks(s_len)
    chains = _chains(tq, th)
    nh = tq // th

    def body(q_ref, k_ref, v_ref, do_ref, r_ref, dq_ref, dk_ref, dv_ref):
        i = pl.program_id(1)
        lane = lax.broadcasted_iota(jnp.int32, (th, LANES), 1)
        upper = (lax.broadcasted_iota(jnp.int32, (tk, tk), 0)
                 > lax.broadcasted_iota(jnp.int32, (tk, tk), 1)).astype(BF16)
        tri = (lax.broadcasted_iota(jnp.int32, (LANES, LANES), 0)
               > lax.broadcasted_iota(jnp.int32, (LANES, LANES), 1)).astype(BF16)

        @pl.when(i == 0)
        def _():
            dk_ref[...] = jnp.zeros_like(dk_ref)
            dv_ref[...] = jnp.zeros_like(dv_ref)

        qs, dos, rights = [], [], []
        for hh, r0 in chains:
            rs = slice(r0, r0 + th)
            hm = (lane // 64) == hh
            qs.append(jnp.where(hm, q_ref[rs, :], jnp.zeros((), BF16)))
            dos.append(jnp.where(hm, do_ref[rs, :], jnp.zeros((), BF16)))
            rights.append(_split_dot(jnp.where(hm, r_ref[rs, :], 0.0), tri, 3))

        def step(kb, carry, sub):
            rows = pl.ds(pl.multiple_of(kb * tk, tk), tk)
            kblk, vblk = k_ref[rows, :], v_ref[rows, :]
            new, a_all, do_all, dz_all, q_all = [], [], [], [], []
            for n, ((hh, r0), (pre, dq)) in enumerate(zip(chains, carry)):
                mask = "all" if sub is None else _diag_mask(th, tk, r0, sub, strict=True)
                if isinstance(mask, str) and mask == "none":
                    new.append((pre, dq))
                    continue
                c = jnp.sum(jnp.where(lane == 64 * hh + kb, rights[n], 0.0), axis=-1, keepdims=True)
                z = _dot_nt(qs[n], kblk)
                lb, lom = _log_sigmoids(z)
                if not isinstance(mask, str):
                    lom = jnp.where(mask, lom, 0.0)
                suf = _split_dot(lom, upper, 2)
                a = jnp.exp2(lb + suf + c)
                if not isinstance(mask, str):
                    a = jnp.where(mask, a, 0.0)
                g = a * _dot_nt(dos[n], vblk)
                left = _split_dot(g, upper, 1, nt=True) + pre
                sig = jnp.exp2(lb)
                dz = g * (1.0 - sig) - sig * left
                if not isinstance(mask, str):
                    dz = jnp.where(mask, dz, 0.0)
                dzb = dz.astype(BF16)
                a_all.append(a.astype(BF16))
                do_all.append(dos[n])
                dz_all.append(dzb)
                q_all.append(qs[n])
                new.append((left[:, tk - 1:tk] + g[:, tk - 1:tk], dq + _dot(dzb, kblk)))
            dv_ref[rows, :] += _dot_tn(jnp.concatenate(a_all, axis=0), jnp.concatenate(do_all, axis=0))
            dk_ref[rows, :] += _dot_tn(jnp.concatenate(dz_all, axis=0), jnp.concatenate(q_all, axis=0))
            return tuple(new)

        lane1 = lax.broadcasted_iota(jnp.int32, (1, LANES), 1)
        first = i * nsub
        for n, (hh, _) in enumerate(chains):
            top = jnp.max(rights[n], axis=0, keepdims=True)
            kb_of = lane1 - 64 * hh
            live = (kb_of >= 0) & (kb_of < i * nsub) & (top >= SB_CUT)
            first = jnp.minimum(first, jnp.min(jnp.where(live, kb_of, i * nsub)))

        init = (jnp.zeros((th, 1), F32), jnp.zeros((th, LANES), F32))
        carry = lax.fori_loop(first, i * nsub, lambda kb, cy: step(kb, cy, None), (init,) * len(chains))
        for sub in range(nsub):
            carry = step(i * nsub + sub, carry, sub)
        for n in range(nh):
            dq_ref[n * th:(n + 1) * th, :] = jnp.where(lane < 64, carry[n][1], carry[nh + n][1]) * SB_SCALE

    return pl.pallas_call(
        body, name="sb_bwd", grid=(4, s_len // tq),
        in_specs=[pl.BlockSpec((tq, LANES), lambda p, i: (i, p)),
                  pl.BlockSpec((s_len, LANES), lambda p, i: (0, 4 + p)),
                  pl.BlockSpec((s_len, LANES), lambda p, i: (0, 8 + p)),
                  pl.BlockSpec((tq, LANES), lambda p, i: (i, p)),
                  pl.BlockSpec((None, tq, LANES), lambda p, i: (p, i, 0))],
        out_specs=[pl.BlockSpec((tq, LANES), lambda p, i: (i, p)),
                   pl.BlockSpec((s_len, LANES), lambda p, i: (0, p)),
                   pl.BlockSpec((s_len, LANES), lambda p, i: (0, p))],
        out_shape=[jax.ShapeDtypeStruct((s_len, 512), F32)] * 3,
        compiler_params=_cparams(48, dimension_semantics=("arbitrary", "arbitrary")),
    )(sb, sb, sb, do, r)


def _merge_fwd(x, oa, ob, gates, bg, wa, wb, wo):
    s_len = x.shape[0]
    tm = _row_block(s_len)

    def body(x_ref, oa_ref, ob_ref, g_ref, bg_ref, wa_ref, wb_ref, wo_ref, y_ref):
        pa = _dot(oa_ref[...], wa_ref[...])
        pb = _dot(ob_ref[...], wb_ref[...])
        merged = (_sigmoid(g_ref[:, 0:D_MODEL] + bg_ref[0:1, :]) * pa
                  + _sigmoid(g_ref[:, D_MODEL:2 * D_MODEL] + bg_ref[1:2, :]) * pb)
        y_ref[...] = x_ref[...] + _dot(merged.astype(BF16), wo_ref[...])

    full = lambda shape: pl.BlockSpec(shape, lambda i: (0, 0))
    rowb = lambda n: pl.BlockSpec((tm, n), lambda i: (i, 0))
    return pl.pallas_call(
        body, name="merge_fwd", grid=(s_len // tm,),
        in_specs=[rowb(1024), rowb(512), rowb(512), rowb(2048), full((2, 1024)), full((512, 1024)),
                  full((512, 1024)), full((1024, 1024))],
        out_specs=rowb(1024),
        out_shape=jax.ShapeDtypeStruct((s_len, D_MODEL), F32),
        compiler_params=_cparams(48, dimension_semantics=("parallel",)),
    )(x, oa, ob, gates, bg, wa, wb, wo)


def _merge_bwd(dx1, oa, ob, gates, bg, wa, wb, wo):
    s_len = dx1.shape[0]
    tm = _row_block(s_len)

    def body(dx_ref, oa_ref, ob_ref, g_ref, bg_ref, wa_ref, wb_ref, wo_ref,
             doa_ref, dob_ref, dgate_ref, dpa_ref, dpb_ref, merged_ref, dxb_ref, dbg_ref):
        first = pl.program_id(0) == 0
        dxb = dx_ref[...].astype(BF16)
        dxb_ref[...] = dxb
        pa = _dot(oa_ref[...], wa_ref[...])
        pb = _dot(ob_ref[...], wb_ref[...])
        sa = _sigmoid(g_ref[:, 0:D_MODEL] + bg_ref[0:1, :])
        sbg = _sigmoid(g_ref[:, D_MODEL:2 * D_MODEL] + bg_ref[1:2, :])
        merged_ref[...] = (sa * pa + sbg * pb).astype(BF16)
        dm = _dot_nt(dxb, wo_ref[...])
        dpa = (dm * sa).astype(BF16)
        dpb = (dm * sbg).astype(BF16)
        dpa_ref[...] = dpa
        dpb_ref[...] = dpb
        dga = dm * pa * sa * (1.0 - sa)
        dgb = dm * pb * sbg * (1.0 - sbg)
        dgate_ref[:, 0:D_MODEL] = dga.astype(BF16)
        dgate_ref[:, D_MODEL:2 * D_MODEL] = dgb.astype(BF16)
        _acc_rows(dbg_ref.at[0:1, :], dga, first)
        _acc_rows(dbg_ref.at[1:2, :], dgb, first)
        doa_ref[...] = _dot_nt(dpa, wa_ref[...]).astype(BF16)
        dob_ref[...] = _dot_nt(dpb, wb_ref[...]).astype(BF16)

    full = lambda shape: pl.BlockSpec(shape, lambda i: (0, 0))
    rowb = lambda n: pl.BlockSpec((tm, n), lambda i: (i, 0))
    sds = lambda n, dt: jax.ShapeDtypeStruct((s_len, n), dt)
    return pl.pallas_call(
        body, name="merge_bwd", grid=(s_len // tm,),
        in_specs=[rowb(1024), rowb(512), rowb(512), rowb(2048), full((2, 1024)), full((512, 1024)),
                  full((512, 1024)), full((1024, 1024))],
        out_specs=[rowb(512), rowb(512), rowb(2048), rowb(1024), rowb(1024), rowb(1024), rowb(1024),
                   full((2, 1024))],
        out_shape=[sds(512, BF16), sds(512, BF16), sds(2048, BF16), sds(1024, BF16), sds(1024, BF16),
                   sds(1024, BF16), sds(1024, BF16), jax.ShapeDtypeStruct((2, 1024), F32)],
        compiler_params=_cparams(48, dimension_semantics=("arbitrary",)),
    )(dx1, oa, ob, gates, bg, wa, wb, wo)


def _mem_kv(mem, g, w):
    m_len = mem.shape[0]

    def body(mem_ref, g_ref, w_ref, mn_ref, kv_ref):
        mn, _, _ = _rms(mem_ref[...], g_ref[...])
        mnb = mn.astype(BF16)
        mn_ref[...] = mnb
        kv_ref[...] = _dot(mnb, w_ref[...]).astype(BF16)

    return pl.pallas_call(
        body, name="mem_kv",
        out_shape=[jax.ShapeDtypeStruct((m_len, D_MODEL), BF16), jax.ShapeDtypeStruct((m_len, 1024), BF16)],
    )(mem, g, w)


def _mem_bwd(mem, g, w, mn, dkv):
    def body(mem_ref, g_ref, w_ref, mn_ref, dkv_ref, dw_ref, dg_ref):
        dkvb = dkv_ref[...].astype(BF16)
        dw_ref[...] = _dot_tn(mn_ref[...], dkvb)
        dmn = _dot_nt(dkvb, w_ref[...])
        _, xh, _ = _rms(mem_ref[...], g_ref[...])
        dg_ref[...] = jnp.sum(dmn * xh, axis=0, keepdims=True)

    return pl.pallas_call(
        body, name="mem_bwd",
        out_shape=[jax.ShapeDtypeStruct((D_MODEL, 1024), F32), jax.ShapeDtypeStruct((1, D_MODEL), F32)],
    )(mem, g, w, mn, dkv)


def _xattn_heads(xqb, kv_ref, m_len):
    ps = []
    for h in range(X_HEADS):
        hs = slice(h * X_HEAD_DIM, (h + 1) * X_HEAD_DIM)
        s = _dot_nt(xqb[:, hs], kv_ref[:, hs]) * X_SCALE
        e = jnp.exp(s - jnp.max(s, axis=-1, keepdims=True))
        ps.append(e / jnp.sum(e, axis=-1, keepdims=True))
    return ps


def _xattn_fwd(x1, g, wxq, kv, wxo):
    s_len, m_len = x1.shape[0], kv.shape[0]
    tm = _row_block(s_len)

    def body(x_ref, g_ref, wq_ref, kv_ref, wo_ref, y_ref):
        hx, _, _ = _rms(x_ref[...], g_ref[...])
        xqb = _dot(hx.astype(BF16), wq_ref[...]).astype(BF16)
        ps = _xattn_heads(xqb, kv_ref, m_len)
        xo = jnp.concatenate(
            [_dot(ps[h].astype(BF16), kv_ref[:, 512 + h * X_HEAD_DIM:512 + (h + 1) * X_HEAD_DIM])
             for h in range(X_HEADS)], axis=-1)
        y_ref[...] = x_ref[...] + _dot(xo.astype(BF16), wo_ref[...])

    full = lambda shape: pl.BlockSpec(shape, lambda i: (0, 0))
    rowb = lambda n: pl.BlockSpec((tm, n), lambda i: (i, 0))
    return pl.pallas_call(
        body, name="xattn_fwd", grid=(s_len // tm,),
        in_specs=[rowb(1024), full((1, 1024)), full((1024, 512)), full((m_len, 1024)), full((512, 1024))],
        out_specs=rowb(1024),
        out_shape=jax.ShapeDtypeStruct((s_len, D_MODEL), F32),
        compiler_params=_cparams(48, dimension_semantics=("parallel",)),
    )(x1, g, wxq, kv, wxo)


def _xattn_bwd(x1, dx2, g, wxq, kv, wxo):
    s_len, m_len = x1.shape[0], kv.shape[0]
    tm = _row_block(s_len)

    def body(x_ref, dy_ref, g_ref, wq_ref, kv_ref, wo_ref, dx_ref, dwq_ref, dwo_ref, dkv_ref, dg_ref):
        first = pl.program_id(0) == 0
        gv = g_ref[...]
        hx, xh, r = _rms(x_ref[...], gv)
        hxb = hx.astype(BF16)
        xqb = _dot(hxb, wq_ref[...]).astype(BF16)
        ps = _xattn_heads(xqb, kv_ref, m_len)
        dy = dy_ref[...]
        dyb = dy.astype(BF16)
        dxo = _dot_nt(dyb, wo_ref[...])
        xos, dqs, dks, dvs = [], [], [], []
        for h in range(X_HEADS):
            hs = slice(h * X_HEAD_DIM, (h + 1) * X_HEAD_DIM)
            vs = slice(512 + h * X_HEAD_DIM, 512 + (h + 1) * X_HEAD_DIM)
            p = ps[h]
            pb = p.astype(BF16)
            dxoh = dxo[:, hs].astype(BF16)
            xos.append(_dot(pb, kv_ref[:, vs]))
            dp = _dot_nt(dxoh, kv_ref[:, vs])
            ds = (p * (dp - jnp.sum(dp * p, axis=-1, keepdims=True)) * X_SCALE).astype(BF16)
            dvs.append(_dot_tn(pb, dxoh))
            dks.append(_dot_tn(ds, xqb[:, hs]))
            dqs.append(_dot(ds, kv_ref[:, hs]))
        xob = jnp.concatenate(xos, axis=-1).astype(BF16)
        dxqb = jnp.concatenate(dqs, axis=-1).astype(BF16)
        _acc(dwo_ref, _dot_tn(xob, dyb), first)
        _acc(dwq_ref, _dot_tn(hxb, dxqb), first)
        _acc(dkv_ref, jnp.concatenate(dks + dvs, axis=-1), first)
        dhx = _dot_nt(dxqb, wq_ref[...])
        dx, dgr = _rms_bwd(dhx, xh, r, gv)
        dx_ref[...] = dy + dx
        _acc_rows(dg_ref, dgr, first)

    full = lambda shape: pl.BlockSpec(shape, lambda i: (0, 0))
    rowb = lambda n: pl.BlockSpec((tm, n), lambda i: (i, 0))
    return pl.pallas_call(
        body, name="xattn_bwd", grid=(s_len // tm,),
        in_specs=[rowb(1024), rowb(1024), full((1, 1024)), full((1024, 512)), full((m_len, 1024)),
                  full((512, 1024))],
        out_specs=[rowb(1024), full((1024, 512)), full((512, 1024)), full((m_len, 1024)), full((1, 1024))],
        out_shape=[jax.ShapeDtypeStruct((s_len, D_MODEL), F32), jax.ShapeDtypeStruct((1024, 512), F32),
                   jax.ShapeDtypeStruct((512, 1024), F32), jax.ShapeDtypeStruct((m_len, 1024), F32),
                   jax.ShapeDtypeStruct((1, D_MODEL), F32)],
        compiler_params=_cparams(48, dimension_semantics=("arbitrary",)),
    )(x1, dx2, g, wxq, kv, wxo)


FF_TILE = 1408
FF_TILE_BWD = 256


def _ffn_fwd(x2, g, wg, wu, wd):
    s_len = x2.shape[0]
    tm, tf = _row_block(s_len), FF_TILE

    def body(x_ref, g_ref, wg_ref, wu_ref, wd_ref, y_ref, h_scr):
        j = pl.program_id(1)

        @pl.when(j == 0)
        def _():
            hf, _, _ = _rms(x_ref[...], g_ref[...])
            h_scr[...] = hf.astype(BF16)
            y_ref[...] = x_ref[...]

        hb = h_scr[...]
        gt = _dot(hb, wg_ref[...])
        up = _dot(hb, wu_ref[...])
        act = gt * _sigmoid(gt) * up
        y_ref[...] += _dot(act.astype(BF16), wd_ref[...])

    return pl.pallas_call(
        body, name="ffn_fwd", grid=(s_len // tm, D_FF // tf),
        in_specs=[pl.BlockSpec((tm, D_MODEL), lambda i, j: (i, 0)),
                  pl.BlockSpec((1, D_MODEL), lambda i, j: (0, 0)),
                  pl.BlockSpec((D_MODEL, tf), lambda i, j: (0, j)),
                  pl.BlockSpec((D_MODEL, tf), lambda i, j: (0, j)),
                  pl.BlockSpec((tf, D_MODEL), lambda i, j: (j, 0))],
        out_specs=pl.BlockSpec((tm, D_MODEL), lambda i, j: (i, 0)),
        out_shape=jax.ShapeDtypeStruct((s_len, D_MODEL), F32),
        scratch_shapes=[pltpu.VMEM((tm, D_MODEL), BF16)],
        compiler_params=_cparams(48, dimension_semantics=("parallel", "arbitrary")),
    )(x2, g, wg, wu, wd)


def _ffn_bwd(x2, dx3, g, wg, wu, wd):
    s_len = x2.shape[0]
    tm, tf = min(s_len, 1024), FF_TILE_BWD
    nf = D_FF // tf

    def body(x_ref, dy_ref, g_ref, wg_ref, wu_ref, wd_ref,
             dx_ref, h_ref, dgt_ref, dup_ref, act_ref, dg_ref, h_scr, dyb_scr, dh_scr):
        i, j = pl.program_id(0), pl.program_id(1)

        @pl.when(j == 0)
        def _():
            hf, _, _ = _rms(x_ref[...], g_ref[...])
            hb = hf.astype(BF16)
            h_scr[...] = hb
            h_ref[...] = hb
            dyb_scr[...] = dy_ref[...].astype(BF16)
            dh_scr[...] = jnp.zeros_like(dh_scr)

        hb = h_scr[...]
        gt = _dot(hb, wg_ref[...])
        up = _dot(hb, wu_ref[...])
        sg = _sigmoid(gt)
        silu = gt * sg
        dact = _dot_nt(dyb_scr[...], wd_ref[...])
        dgt = (dact * up * (sg * (1.0 + gt * (1.0 - sg)))).astype(BF16)
        dup = (dact * silu).astype(BF16)
        dgt_ref[...] = dgt
        dup_ref[...] = dup
        act_ref[...] = (silu * up).astype(BF16)
        dh_scr[...] += _dot_nt(dgt, wg_ref[...]) + _dot_nt(dup, wu_ref[...])

        @pl.when(j == nf - 1)
        def _():
            gv = g_ref[...]
            _, xh, r = _rms(x_ref[...], gv)
            dx, dgr = _rms_bwd(dh_scr[...], xh, r, gv)
            dx_ref[...] = dy_ref[...] + dx
            _acc_rows(dg_ref, dgr, i == 0)

    rowb = pl.BlockSpec((tm, D_MODEL), lambda i, j: (i, 0))
    ffb = pl.BlockSpec((tm, tf), lambda i, j: (i, j))
    return pl.pallas_call(
        body, name="ffn_bwd", grid=(s_len // tm, nf),
        in_specs=[rowb, rowb, pl.BlockSpec((1, D_MODEL), lambda i, j: (0, 0)),
                  pl.BlockSpec((D_MODEL, tf), lambda i, j: (0, j)),
                  pl.BlockSpec((D_MODEL, tf), lambda i, j: (0, j)),
                  pl.BlockSpec((tf, D_MODEL), lambda i, j: (j, 0))],
        out_specs=[rowb, rowb, ffb, ffb, ffb, pl.BlockSpec((1, D_MODEL), lambda i, j: (0, 0))],
        out_shape=[jax.ShapeDtypeStruct((s_len, D_MODEL), F32), jax.ShapeDtypeStruct((s_len, D_MODEL), BF16),
                   jax.ShapeDtypeStruct((s_len, D_FF), BF16), jax.ShapeDtypeStruct((s_len, D_FF), BF16),
                   jax.ShapeDtypeStruct((s_len, D_FF), BF16), jax.ShapeDtypeStruct((1, D_MODEL), F32)],
        scratch_shapes=[pltpu.VMEM((tm, D_MODEL), BF16), pltpu.VMEM((tm, D_MODEL), BF16),
                        pltpu.VMEM((tm, D_MODEL), F32)],
        compiler_params=_cparams(56, dimension_semantics=("arbitrary", "arbitrary")),
    )(x2, dx3, g, wg, wu, wd)


def _loss_head(x3, g, target):
    s_len = x3.shape[0]
    tm = _row_block(s_len)

    def body(x_ref, g_ref, t_ref, sse_ref, dx_ref, dxb_ref, dg_ref):
        first = pl.program_id(0) == 0
        gv = g_ref[...]
        y, xh, r = _rms(x_ref[...], gv)
        err = y - t_ref[...]
        _acc(sse_ref, jnp.broadcast_to(jnp.sum(err * err), (8, LANES)), first)
        dx, dgr = _rms_bwd(err * (1.0 / D_MODEL), xh, r, gv)
        dx_ref[...] = dx
        dxb_ref[...] = dx.astype(BF16)
        _acc_rows(dg_ref, dgr, first)

    rowb = pl.BlockSpec((tm, D_MODEL), lambda i: (i, 0))
    return pl.pallas_call(
        body, name="loss_head", grid=(s_len // tm,),
        in_specs=[rowb, pl.BlockSpec((1, D_MODEL), lambda i: (0, 0)), rowb],
        out_specs=[pl.BlockSpec((8, LANES), lambda i: (0, 0)), rowb, rowb,
                   pl.BlockSpec((1, D_MODEL), lambda i: (0, 0))],
        out_shape=[jax.ShapeDtypeStruct((8, LANES), F32), jax.ShapeDtypeStruct((s_len, D_MODEL), F32),
                   jax.ShapeDtypeStruct((s_len, D_MODEL), BF16), jax.ShapeDtypeStruct((1, D_MODEL), F32)],
        compiler_params=_cparams(dimension_semantics=("arbitrary",)),
    )(x3, g, target)


def _mla_prep_bwd(lat, g_q, g_kv, w_uq, w_uk, w_uv, cosf, sinf, dq, dk, dv):
    s_len = lat.shape[0]
    tm = _row_block(s_len)

    def body(lat_ref, gq_ref, gkv_ref, wuq_ref, wuk_ref, wuv_ref, cos_ref, sin_ref, dq_ref, dk_ref, dv_ref,
             dlat_ref, dqb_ref, dkb_ref, dvb_ref, dgq_ref, dgkv_ref):
        first = pl.program_id(0) == 0
        lane = lax.broadcasted_iota(jnp.int32, (tm, LANES), 1)
        cosv, sinv = cos_ref[...], sin_ref[...]
        gq, gkv = gq_ref[...], gkv_ref[...]
        _, qxh, qr = _rms(lat_ref[:, 0:256], gq)
        _, kxh, kr_ = _rms(lat_ref[:, 256:384], gkv)
        dkr = jnp.zeros((tm, LANES), F32)
        for h in range(MLA_HEADS):
            sl = slice(h * HEAD_PAD, (h + 1) * HEAD_PAD)
            blk = dq_ref[:, sl]
            dqb_ref[:, sl] = (blk * cosv + _rope_rot_t(blk, lane) * sinv).astype(BF16)
            kblk = dk_ref[:, sl] * (1.0 / MLA_Q_FOLD)
            dkb_ref[:, sl] = kblk.astype(BF16)
            dkr = dkr + kblk
        dvb = dv_ref[...].astype(BF16)
        dvb_ref[...] = dvb
        dkr = jnp.where((lane >= 64) & (lane < 96), dkr, 0.0)
        dkr = dkr * cosv + _rope_rot_t(dkr, lane) * sinv
        dql = _dot_nt(dqb_ref[...], wuq_ref[...])
        dkvl = _dot_nt(dkb_ref[...], wuk_ref[...]) + _dot_nt(dvb, wuv_ref[...])
        dcq, dgqr = _rms_bwd(dql, qxh, qr, gq)
        dckv, dgkvr = _rms_bwd(dkvl, kxh, kr_, gkv)
        dlat_ref[:, 0:256] = dcq
        dlat_ref[:, 256:384] = dckv
        dlat_ref[:, K_R_OFF:K_R_OFF + LANES] = pltpu.roll(dkr, 64, 1)
        _acc_rows(dgq_ref, dgqr, first)
        _acc_rows(dgkv_ref, dgkvr, first)

    full = lambda shape: pl.BlockSpec(shape, lambda i: (0, 0))
    rowb = lambda n: pl.BlockSpec((tm, n), lambda i: (i, 0))
    sds = lambda n, dt: jax.ShapeDtypeStruct((s_len, n), dt)
    return pl.pallas_call(
        body, name="mla_prep_bwd", grid=(s_len // tm,),
        in_specs=[rowb(512), full((1, 256)), full((1, 128)), full((256, 1024)), full((128, 1024)),
                  full((128, 512)), rowb(128), rowb(128), rowb(1024), rowb(1024), rowb(512)],
        out_specs=[rowb(512), rowb(1024), rowb(1024), rowb(512), full((1, 256)), full((1, 128))],
        out_shape=[sds(512, F32), sds(1024, BF16), sds(1024, BF16), sds(512, BF16),
                   jax.ShapeDtypeStruct((1, 256), F32), jax.ShapeDtypeStruct((1, 128), F32)],
        compiler_params=_cparams(48, dimension_semantics=("arbitrary",)),
    )(lat, g_q, g_kv, w_uq, w_uk, w_uv, cosf, sinf, dq, dk, dv)


def _in_proj_bwd(x, g, w, dx1, dlat, dsbq, dsbk, dsbv, dgates):
    s_len = x.shape[0]
    tm = min(s_len, 256)

    def body(x_ref, g_ref, w_ref, dx1_ref, dlat_ref, dq_ref, dk_ref, dv_ref, dgate_ref,
             gx_ref, dproj_ref, dg_ref):
        dproj_ref[:, 0:512] = dlat_ref[...].astype(BF16)
        dproj_ref[:, 512:1024] = dq_ref[...].astype(BF16)
        dproj_ref[:, 1024:1536] = (dk_ref[...] * LN2).astype(BF16)
        dproj_ref[:, 1536:2048] = dv_ref[...].astype(BF16)
        dproj_ref[:, 2048:4096] = dgate_ref[...]
        dh = _dot_nt(dproj_ref[...], w_ref[...])
        gv = g_ref[...]
        _, xh, r = _rms(x_ref[...], gv)
        dx, dgr = _rms_bwd(dh, xh, r, gv)
        gx_ref[...] = dx1_ref[...] + dx
        _acc_rows(dg_ref, dgr, pl.program_id(0) == 0)

    rowb = lambda n: pl.BlockSpec((tm, n), lambda i: (i, 0))
    full = lambda shape: pl.BlockSpec(shape, lambda i: (0, 0))
    return pl.pallas_call(
        body, name="in_proj_bwd", grid=(s_len // tm,),
        in_specs=[rowb(D_MODEL), full((1, D_MODEL)), full((D_MODEL, D_IN_PAD)), rowb(D_MODEL),
                  rowb(512), rowb(512), rowb(512), rowb(512), rowb(2 * D_MODEL)],
        out_specs=[rowb(D_MODEL), rowb(D_IN_PAD), full((1, D_MODEL))],
        out_shape=[jax.ShapeDtypeStruct((s_len, D_MODEL), F32), jax.ShapeDtypeStruct((s_len, D_IN_PAD), BF16),
                   jax.ShapeDtypeStruct((1, D_MODEL), F32)],
        compiler_params=_cparams(48, dimension_semantics=("arbitrary",)),
    )(x, g, w, dx1, dlat, dsbq, dsbk, dsbv, dgates)


def _adamw(landed, w, m, v, name):
    r, c = w.shape
    lanes = _round_up(c, LANES)
    tb = r
    for cand in range(r, 0, -1):
        if r % cand == 0 and (cand % 8 == 0 or cand == r) and N_DEV * cand * lanes * 4 <= ADAM_BLOCK_BYTES:
            tb = cand
            break
    c1 = 1.0 - ADAM_B1 ** ADAM_STEP
    c2 = 1.0 - ADAM_B2 ** ADAM_STEP

    def body(l_ref, w_ref, m_ref, v_ref, g_ref, d_ref, nm_ref, nv_ref):
        g = l_ref[0]
        for k in range(1, N_DEV):
            g = g + l_ref[k]
        nm = ADAM_B1 * m_ref[...] + (1.0 - ADAM_B1) * g
        nv = ADAM_B2 * v_ref[...] + (1.0 - ADAM_B2) * (g * g)
        g_ref[...] = g
        nm_ref[...] = nm
        nv_ref[...] = nv
        d_ref[...] = -ADAM_LR * ((nm / c1) / (jnp.sqrt(nv / c2) + ADAM_EPS) + ADAM_WD * w_ref[...])

    blk = pl.BlockSpec((tb, c), lambda i: (i, 0))
    return pl.pallas_call(
        body, name=name, grid=(r // tb,),
        in_specs=[pl.BlockSpec((N_DEV, tb, c), lambda i: (0, i, 0)), blk, blk, blk],
        out_specs=[blk, blk, blk, blk],
        out_shape=[jax.ShapeDtypeStruct((r, c), F32)] * 4,
        compiler_params=_cparams(dimension_semantics=("parallel",)),
    )(landed, w, m, v)


def _shard_shape(shape, axis):
    return tuple(d // N_DEV if a == axis else d for a, d in enumerate(shape))


def _split_pieces(full, axis):
    r, c = full.shape
    if axis == 0:
        return full.reshape(N_DEV, r // N_DEV, c)
    return full.reshape(r, N_DEV, c // N_DEV).transpose(1, 0, 2)


def _join_shards(gathered, axis):
    _, r, c = gathered.shape
    if axis == 0:
        return gathered.reshape(N_DEV * r, c)
    return gathered.transpose(1, 0, 2).reshape(r, N_DEV * c)


def kernel(x, mem, positions, g_mix, w_in, b_gate, g_q_lat, w_uq, g_kv_lat, w_ukv, w_a_proj, w_b_proj, w_o, g_x, g_mem, w_xq, w_xkv, w_xo, g_ffn, w_gate, w_up, w_down, g_final, loss_target, m_g_mix, m_w_in, m_b_gate, m_g_q_lat, m_w_uq, m_g_kv_lat, m_w_ukv, m_w_a_proj, m_w_b_proj, m_w_o, m_g_x, m_g_mem, m_w_xq, m_w_xkv, m_w_xo, m_g_ffn, m_w_gate, m_w_up, m_w_down, m_g_final, v_g_mix, v_w_in, v_b_gate, v_g_q_lat, v_w_uq, v_g_kv_lat, v_w_ukv, v_w_a_proj, v_w_b_proj, v_w_o, v_g_x, v_g_mem, v_w_xq, v_w_xkv, v_w_xo, v_g_ffn, v_w_gate, v_w_up, v_w_down, v_g_final):
    given = dict(locals())
    s_len = x.shape[1]
    x2d = x.reshape(s_len, D_MODEL)
    mem2d = mem.reshape(-1, D_MODEL)
    target = loss_target.reshape(s_len, D_MODEL)

    names = [name for name, _, _ in SHARDED]
    axis_of = {name: axis for name, _, axis in SHARDED}
    shard2d = lambda name, prefix="": given[prefix + name].reshape(
        _shard_shape(dict((n, s) for n, s, _ in SHARDED)[name], axis_of[name]))

    wire = lambda name: shard2d(name) if name == "b_gate" else shard2d(name).astype(BF16)
    early = [n for n in names if n in NEEDED_FIRST]
    late = [n for n in names if n not in NEEDED_FIRST]
    gathered = _exchange(True, [wire(n) for n in early], "weights_gather_first")
    wts = {n: _join_shards(g, axis_of[n]) for n, g in zip(early, gathered)}

    w_in_p = jnp.concatenate([wts["w_in"][:, :416], jnp.zeros((D_MODEL, 96), BF16), wts["w_in"][:, 416:]], axis=1)
    w_uq_p = jnp.pad(wts["w_uq"].reshape(256, MLA_HEADS, 96), ((0, 0), (0, 0), (0, 32))).reshape(256, 1024)
    ukv = wts["w_ukv"].reshape(128, MLA_HEADS, 128)
    w_uk_p = jnp.pad(ukv[:, :, :64], ((0, 0), (0, 0), (0, 64))).reshape(128, 1024)
    w_uv = ukv[:, :, 64:].reshape(128, 512)
    bg = wts["b_gate"]

    inv_freq = ROPE_THETA ** (-jnp.arange(0, MLA_ROPE, 2, dtype=F32) / MLA_ROPE)
    ang = positions.reshape(s_len).astype(F32)[:, None] * inv_freq
    cos16, sin16 = jnp.cos(ang), jnp.sin(ang)
    cosf = jnp.concatenate([jnp.ones((s_len, 64), F32), cos16, cos16, jnp.ones((s_len, 32), F32)], axis=1)
    sinf = jnp.concatenate([jnp.zeros((s_len, 64), F32), sin16, sin16, jnp.zeros((s_len, 32), F32)], axis=1)

    h1, lat, sb, gates = _in_proj(x2d, g_mix, w_in_p)
    qa, ka, va, q_lat, kv_lat = _mla_prep(lat, g_q_lat, g_kv_lat, w_uq_p, w_uk_p, w_uv, cosf, sinf)
    oa, lse, gathered = _mla_fwd(qa, ka, va, _Exchange(True, [wire(n) for n in late]))
    wts.update({n: _join_shards(g, axis_of[n]) for n, g in zip(late, gathered)})
    ob, sb_r = _sb_fwd(sb)
    x1 = _merge_fwd(x2d, oa, ob, gates, bg, wts["w_a_proj"], wts["w_b_proj"], wts["w_o"])
    mn, xkv = _mem_kv(mem2d, g_mem, wts["w_xkv"])
    x2 = _xattn_fwd(x1, g_x, wts["w_xq"], xkv, wts["w_xo"])
    x3 = _ffn_fwd(x2, g_ffn, wts["w_gate"], wts["w_up"], wts["w_down"])
    g_final2d = g_final.reshape(1, D_MODEL)
    sse, dx3, dx3b, dg_final = _loss_head(x3, g_final2d, target)
    loss = lax.psum(sse[0, 0] * (0.5 / D_MODEL), ("x", "y", "c"))

    dx2, hf, dgt, dup, act, dg_ffn = _ffn_bwd(x2, dx3, g_ffn, wts["w_gate"], wts["w_up"], wts["w_down"])
    dx1, dw_xq, dw_xo, dxkv, dg_x = _xattn_bwd(x1, dx2, g_x, wts["w_xq"], xkv, wts["w_xo"])
    dw_xkv, dg_mem = _mem_bwd(mem2d, g_mem, wts["w_xkv"], mn, dxkv)
    doa, dob, dgates, dpa, dpb, merged, dx1b, dbg = _merge_bwd(
        dx1, oa, ob, gates, bg, wts["w_a_proj"], wts["w_b_proj"], wts["w_o"])
    dsbq, dsbk, dsbv = _sb_bwd(sb, dob, sb_r)
    full_grads = {
        "w_a_proj": _tn_matmul(oa, dpa, "dw_a"),
        "w_b_proj": _tn_matmul(ob, dpb, "dw_b"),
        "w_o": _tn_matmul(merged, dx1b, "dw_o"),
        "w_xq": dw_xq,
        "w_xkv": dw_xkv,
        "w_xo": dw_xo,
        "w_gate": _tn_matmul(hf, dgt, "dw_gate", tn=FF_TILE),
        "w_up": _tn_matmul(hf, dup, "dw_up", tn=FF_TILE),
        "w_down": _tn_matmul(act, dx3b, "dw_down", tka=FF_TILE),
    }
    dqa, dka, dva, got = _mla_bwd(
        qa, ka, va, oa, doa, lse, _Exchange(False, [_split_pieces(full_grads[n], axis_of[n]) for n in late]))
    landed = dict(zip(late, got))
    dlat, dqb, dkb, dvb, dg_q, dg_kv = _mla_prep_bwd(
        lat, g_q_lat, g_kv_lat, w_uq_p, w_uk_p, w_uv, cosf, sinf, dqa, dka, dva)
    grad_x, dproj, dg_mix = _in_proj_bwd(x2d, g_mix, w_in_p, dx1, dlat, dsbq, dsbk, dsbv, dgates)
    dw_in_p = _tn_matmul(h1, dproj, "dw_in")
    dw_uq_p = _tn_matmul(q_lat, dqb, "dw_uq")
    dw_uk_p = _tn_matmul(kv_lat, dkb, "dw_uk")
    dw_uv = _tn_matmul(kv_lat, dvb, "dw_uv")
    full_grads.update({
        "w_in": jnp.concatenate([dw_in_p[:, :416], dw_in_p[:, 512:]], axis=1),
        "b_gate": dbg,
        "w_uq": dw_uq_p.reshape(256, MLA_HEADS, 128)[:, :, :96].reshape(256, 768),
        "w_ukv": jnp.concatenate([dw_uk_p.reshape(128, MLA_HEADS, 128)[:, :, :64],
                                  dw_uv.reshape(128, MLA_HEADS, 64)], axis=2).reshape(128, 1024),
    })
    rep_grads = {"g_mix": dg_mix, "g_q_lat": dg_q, "g_kv_lat": dg_kv, "g_x": dg_x, "g_mem": dg_mem,
                 "g_ffn": dg_ffn, "g_final": dg_final}
    rep_cat = lambda prefix, src: jnp.concatenate(
        [src[prefix + n].reshape(-1) for n, _ in REPLICATED]).reshape(-1, LANES)
    rep_src = jnp.broadcast_to(rep_cat("", rep_grads), (N_DEV,) + rep_cat("", rep_grads).shape)
    got = _exchange(False, [_split_pieces(full_grads[n], axis_of[n]) for n in early] + [rep_src], "grads_last")
    landed.update(zip(early, got[:-1]))
    rep_landed = got[-1]

    res = {}
    for name, _, _ in SHARDED:
        outs = _adamw(landed[name], shard2d(name), shard2d(name, "m_"), shard2d(name, "v_"), "adamw_" + name)
        res[name] = [o.reshape(given[name].shape) for o in outs]
    rep_outs = _adamw(rep_landed, rep_cat("", given), rep_cat("m_", given), rep_cat("v_", given), "adamw_gains")
    off = 0
    for name, n in REPLICATED:
        res[name] = [o.reshape(-1)[off:off + n].reshape(given[name].shape) for o in rep_outs]
        off += n
    result = [loss, grad_x.reshape(x.shape)]
    for k in range(4):
        result.extend(res[name][k] for name in WEIGHT_ORDER)
    return tuple(result)
```

```python
import functools
import math

import jax
import jax.numpy as jnp
from jax import lax
from jax.experimental import pallas as pl
from jax.experimental.pallas import tpu as pltpu

F32 = jnp.float32
BF16 = jnp.bfloat16

D_MODEL = 1024
MLA_HEADS = 8
MLA_Q_RANK = 256
MLA_KV_RANK = 128
MLA_NOPE = 64
MLA_ROPE = 32
MLA_V = 64
ROPE_THETA = 10000.0
SB_WIDTH = 512
X_HEADS = 4
X_HEAD_DIM = 128
D_FF = 2816
EPS = 1e-6
D_IN = 4000
D_IN_PAD = 4096
K_R_OFF = 384
LANES = 128
HEAD_PAD = 128
MLA_SCALE = 1.0 / math.sqrt(MLA_NOPE + MLA_ROPE)
SB_SCALE = 0.125
LOG2E = math.log2(math.e)
LN2 = math.log(2.0)
MLA_Q_FOLD = MLA_SCALE * LOG2E
SB_Q_FOLD = SB_SCALE * LOG2E
SB_CUT = -160.0
X_SCALE = 1.0 / math.sqrt(X_HEAD_DIM)
NEG_BIG = -1e30

ADAM_LR = 0.001
ADAM_B1 = 0.9
ADAM_B2 = 0.999
ADAM_EPS = 1e-08
ADAM_WD = 0.01
ADAM_STEP = 10

N_DEV = 8
MIB = 1024 * 1024
ADAM_BLOCK_BYTES = 4 * MIB

SHARDED = (
    ("w_in", (D_MODEL, D_IN), 1),
    ("b_gate", (2, D_MODEL), 1),
    ("w_uq", (MLA_Q_RANK, 768), 1),
    ("w_ukv", (MLA_KV_RANK, 1024), 1),
    ("w_a_proj", (512, D_MODEL), 1),
    ("w_b_proj", (512, D_MODEL), 1),
    ("w_o", (D_MODEL, D_MODEL), 0),
    ("w_xq", (D_MODEL, 512), 0),
    ("w_xkv", (D_MODEL, 1024), 0),
    ("w_xo", (512, D_MODEL), 1),
    ("w_gate", (D_MODEL, D_FF), 1),
    ("w_up", (D_MODEL, D_FF), 1),
    ("w_down", (D_FF, D_MODEL), 0),
)
NEEDED_FIRST = ("w_in", "b_gate", "w_uq", "w_ukv")
REPLICATED = (
    ("g_mix", 1024), ("g_q_lat", 256), ("g_kv_lat", 128), ("g_x", 1024),
    ("g_mem", 1024), ("g_ffn", 1024), ("g_final", 1024),
)
WEIGHT_ORDER = ("g_mix", "w_in", "b_gate", "g_q_lat", "w_uq", "g_kv_lat", "w_ukv", "w_a_proj",
                "w_b_proj", "w_o", "g_x", "g_mem", "w_xq", "w_xkv", "w_xo", "g_ffn", "w_gate",
                "w_up", "w_down", "g_final")


def _round_up(n, m):
    return -(-n // m) * m


def _cparams(vmem_mib=None, **kw):
    if vmem_mib is not None:
        kw["vmem_limit_bytes"] = vmem_mib * MIB
    return pltpu.CompilerParams(**kw)


def _dot(a, b):
    return jnp.dot(a, b, preferred_element_type=F32)


def _dot_nt(a, b):
    return lax.dot_general(a, b, (((1,), (1,)), ((), ())), preferred_element_type=F32)


def _dot_tn(a, b):
    return lax.dot_general(a, b, (((0,), (0,)), ((), ())), preferred_element_type=F32)


def _rms(x, g):
    r = lax.rsqrt(jnp.mean(x * x, axis=-1, keepdims=True) + EPS)
    xh = x * r
    return xh * g, xh, r


def _rms_bwd(dy, xh, r, g):
    u = dy * g
    dx = r * (u - xh * jnp.mean(u * xh, axis=-1, keepdims=True))
    return dx, dy * xh


def _sigmoid(z):
    return 1.0 / (1.0 + jnp.exp(-z))


def _acc_rows(ref, val, first):
    s = jnp.sum(val, axis=0, keepdims=True)

    @pl.when(first)
    def _():
        ref[...] = s

    @pl.when(jnp.logical_not(first))
    def _():
        ref[...] += s


def _acc(ref, val, first):
    @pl.when(first)
    def _():
        ref[...] = val

    @pl.when(jnp.logical_not(first))
    def _():
        ref[...] += val


def _peer(k):
    x, y, c = lax.axis_index("x"), lax.axis_index("y"), lax.axis_index("c")
    px = 1 - x if (k >> 2) & 1 else x
    py = 1 - y if (k >> 1) & 1 else y
    pc = 1 - c if k & 1 else c
    return (px, py, pc), 4 * px + 2 * py + pc


N_PEERS = N_DEV - 1
OTHER_CHIPS = (2, 4, 6)


def _land_shape(gather, src):
    return (N_DEV,) + src.shape if gather else src.shape


class _Exchange:
    def __init__(self, gather, srcs):
        self.gather, self.n, self.srcs = gather, len(srcs), list(srcs)
        self.out_shape = [jax.ShapeDtypeStruct(_land_shape(gather, s), s.dtype) for s in srcs]
        self.specs = [pl.BlockSpec(memory_space=pl.ANY)] * self.n
        self.scratch = [pltpu.SemaphoreType.DMA((self.n * N_PEERS,)), pltpu.SemaphoreType.DMA((self.n * N_PEERS,)),
                        pltpu.SemaphoreType.DMA((self.n,))]

    def bind(self, src, land, sems):
        self.src, self.land = src, land
        self.send_sems, self.recv_sems, self.local_sems = sems

    def _copy(self, a, k, source, to, target=1):
        return pltpu.make_async_remote_copy(
            src_ref=source, dst_ref=to,
            send_sem=self.send_sems.at[a * N_PEERS + k - 1], recv_sem=self.recv_sems.at[a * N_PEERS + k - 1],
            device_id=_peer(target)[0], device_id_type=pl.DeviceIdType.MESH)

    def _row(self, a, k):
        return self.land[a].at[_peer(k)[1]]

    def _mine(self, a):
        me = _peer(0)[1]
        return pltpu.make_async_copy(self.src[a] if self.gather else self.src[a].at[me], self.land[a].at[me],
                                     self.local_sems.at[a])

    def issue(self):
        me = _peer(0)[1]
        for a in range(self.n):
            self._mine(a).start()
            for k in ((1,) + OTHER_CHIPS if self.gather else range(1, N_DEV)):
                source = self.src[a] if self.gather else self.src[a].at[_peer(k)[1]]
                self._copy(a, k, source, self.land[a].at[me], target=k).start()

    def finish(self):
        me = _peer(0)[1]
        part = lambda a: self.src[a] if self.gather else self.src[a].at[me]
        if self.gather:
            for a in range(self.n):
                for k in OTHER_CHIPS:
                    self._copy(a, k, part(a), self._row(a, k)).wait_recv()
                    self._copy(a, k + 1, self._row(a, k), self._row(a, k), target=1).start()
        for a in range(self.n):
            for k in ((1, 3, 5, 7) if self.gather else range(1, N_DEV)):
                self._copy(a, k, part(a), self._row(a, k)).wait_recv()
        for a in range(self.n):
            for k in range(1, N_DEV):
                self._copy(a, k, part(a), self.land[a].at[me]).wait_send()
            self._mine(a).wait()


def _exchange(gather, srcs, name):
    ex = _Exchange(gather, srcs)

    def body(*refs):
        ex.bind(refs[:ex.n], refs[ex.n:2 * ex.n], refs[2 * ex.n:])
        ex.issue()
        ex.finish()

    return pl.pallas_call(body, name=name, out_shape=ex.out_shape, in_specs=ex.specs, out_specs=ex.specs,
                          scratch_shapes=ex.scratch)(*ex.srcs)


def _tn_matmul(a, b, name, tka=512, tn=1024, ts=2048):
    s_len, ka = a.shape
    n = b.shape[1]
    tka, tn, ts = min(tka, ka), min(tn, n), min(ts, s_len)
    assert ka % tka == 0 and n % tn == 0 and s_len % ts == 0

    def body(a_ref, b_ref, o_ref):
        _acc(o_ref, _dot_tn(a_ref[...], b_ref[...]), pl.program_id(2) == 0)

    return pl.pallas_call(
        body, name=name, grid=(ka // tka, n // tn, s_len // ts),
        in_specs=[pl.BlockSpec((ts, tka), lambda i, j, s: (s, i)),
                  pl.BlockSpec((ts, tn), lambda i, j, s: (s, j))],
        out_specs=pl.BlockSpec((tka, tn), lambda i, j, s: (i, j)),
        out_shape=jax.ShapeDtypeStruct((ka, n), F32),
        compiler_params=_cparams(dimension_semantics=("parallel", "parallel", "arbitrary")),
    )(a, b)


def _row_block(s_len):
    return min(s_len, 512)


def _in_proj(x, g, w):
    s_len = x.shape[0]
    tm = _row_block(s_len)

    def body(x_ref, g_ref, w_ref, h_ref, lat_ref, sb_ref, gate_ref):
        h, _, _ = _rms(x_ref[...], g_ref[...])
        hb = h.astype(BF16)
        h_ref[...] = hb
        p = _dot(hb, w_ref[:, 0:1024])
        lat_ref[...] = p[:, 0:512]
        sb_ref[:, 0:512] = (p[:, 512:1024] * SB_Q_FOLD).astype(BF16)
        sb_ref[:, 512:1536] = _dot(hb, w_ref[:, 1024:2048]).astype(BF16)
        gate_ref[:, 0:1024] = _dot(hb, w_ref[:, 2048:3072])
        gate_ref[:, 1024:2048] = _dot(hb, w_ref[:, 3072:4096])

    rowb = lambda n: pl.BlockSpec((tm, n), lambda i: (i, 0))
    return pl.pallas_call(
        body, name="in_proj", grid=(s_len // tm,),
        in_specs=[rowb(D_MODEL), pl.BlockSpec((1, D_MODEL), lambda i: (0, 0)),
                  pl.BlockSpec((D_MODEL, D_IN_PAD), lambda i: (0, 0))],
        out_specs=[rowb(D_MODEL), rowb(512), rowb(3 * SB_WIDTH), rowb(2 * D_MODEL)],
        out_shape=[jax.ShapeDtypeStruct((s_len, D_MODEL), BF16),
                   jax.ShapeDtypeStruct((s_len, 512), F32),
                   jax.ShapeDtypeStruct((s_len, 3 * SB_WIDTH), BF16),
                   jax.ShapeDtypeStruct((s_len, 2 * D_MODEL), F32)],
        compiler_params=_cparams(48, dimension_semantics=("parallel",)),
    )(x, g, w)


def _rope_rot(blk, lane):
    return jnp.where(lane < 80, -pltpu.roll(blk, 112, 1), pltpu.roll(blk, 16, 1))


def _rope_rot_t(blk, lane):
    return jnp.where(lane < 80, pltpu.roll(blk, 112, 1), -pltpu.roll(blk, 16, 1))


def _mla_prep(lat, g_q, g_kv, w_uq, w_uk, w_uv, cosf, sinf):
    s_len = lat.shape[0]
    tm = _row_block(s_len)

    def body(lat_ref, gq_ref, gkv_ref, wuq_ref, wuk_ref, wuv_ref, cos_ref, sin_ref,
             q_ref, k_ref, v_ref, ql_ref, kvl_ref):
        lane = lax.broadcasted_iota(jnp.int32, (tm, LANES), 1)
        cosv, sinv = cos_ref[...], sin_ref[...]
        ql, _, _ = _rms(lat_ref[:, 0:256], gq_ref[...])
        kvl, _, _ = _rms(lat_ref[:, 256:384], gkv_ref[...])
        qlb, kvlb = ql.astype(BF16), kvl.astype(BF16)
        ql_ref[...] = qlb
        kvl_ref[...] = kvlb
        q = _dot(qlb, wuq_ref[...])
        kn = _dot(kvlb, wuk_ref[...])
        v_ref[...] = _dot(kvlb, wuv_ref[...]).astype(BF16)
        kr = pltpu.roll(lat_ref[:, K_R_OFF:K_R_OFF + LANES], 64, 1)
        kr = kr * cosv + _rope_rot(kr, lane) * sinv
        for h in range(MLA_HEADS):
            sl = slice(h * HEAD_PAD, (h + 1) * HEAD_PAD)
            blk = q[:, sl]
            q_ref[:, sl] = ((blk * cosv + _rope_rot(blk, lane) * sinv) * MLA_Q_FOLD).astype(BF16)
            k_ref[:, sl] = (kn[:, sl] + kr).astype(BF16)

    full = lambda shape: pl.BlockSpec(shape, lambda i: (0, 0))
    rowb = lambda n: pl.BlockSpec((tm, n), lambda i: (i, 0))
    return pl.pallas_call(
        body, name="mla_prep", grid=(s_len // tm,),
        in_specs=[rowb(512), full((1, 256)), full((1, 128)), full((256, 1024)), full((128, 1024)),
                  full((128, 512)), rowb(128), rowb(128)],
        out_specs=[rowb(1024), rowb(1024), rowb(512), rowb(256), rowb(128)],
        out_shape=[jax.ShapeDtypeStruct((s_len, 1024), BF16), jax.ShapeDtypeStruct((s_len, 1024), BF16),
                   jax.ShapeDtypeStruct((s_len, 512), BF16), jax.ShapeDtypeStruct((s_len, 256), BF16),
                   jax.ShapeDtypeStruct((s_len, 128), BF16)],
        compiler_params=_cparams(dimension_semantics=("parallel",)),
    )(lat, g_q, g_kv, w_uq, w_uk, w_uv, cosf, sinf)


ATTN_TQ = 1024
ATTN_TH = 512
ATTN_TK = 256
MLA_TK = 512


def _attn_blocks(s_len, tk=ATTN_TK):
    tq, th, tk = min(s_len, ATTN_TQ), min(s_len, ATTN_TH), min(s_len, tk)
    return tq, th, tk, tq // tk


def _chains(tq, th):
    return [(hh, r0) for hh in range(2) for r0 in range(0, tq, th)]


def _diag_mask(th, tk, r0, sub, strict):
    lo, hi = sub * tk, (sub + 1) * tk - 1
    last, first = r0 + th - 1, r0
    if (lo >= last) if strict else (lo > last):
        return "none"
    if (hi < first) if strict else (hi <= first):
        return "all"
    row = lax.broadcasted_iota(jnp.int32, (th, tk), 0) + r0
    col = lax.broadcasted_iota(jnp.int32, (th, tk), 1) + lo
    return col < row if strict else col <= row


def _mla_fwd(q, k, v, ride):
    s_len = q.shape[0]
    tq, th, tk, nsub = _attn_blocks(s_len, MLA_TK)
    chains = _chains(tq, th)
    nh = tq // th
    nq = s_len // tq

    def body(q_ref, k_ref, v_ref, *rest):
        o_ref, lse_ref = rest[ride.n:ride.n + 2]
        ride.bind(rest[:ride.n], rest[ride.n + 2:2 * ride.n + 2], rest[2 * ride.n + 2:])
        pl.when((pl.program_id(0) == 0) & (pl.program_id(1) == 0))(ride.issue)
        i = pl.program_id(1)
        lane = lax.broadcasted_iota(jnp.int32, (th, LANES), 1)
        hsl = [slice(hh * HEAD_PAD, (hh + 1) * HEAD_PAD) for hh in range(2)]

        def step(kb, carry, sub):
            rows = pl.ds(pl.multiple_of(kb * tk, tk), tk)
            vblk = v_ref[rows, :]
            masks = ["all" if sub is None else _diag_mask(th, tk, r0, sub, strict=False) for _, r0 in chains]
            live = [n for n, m in enumerate(masks) if not (isinstance(m, str) and m == "none")]
            s = {n: _dot_nt(q_ref[chains[n][1]:chains[n][1] + th, hsl[chains[n][0]]], k_ref[rows, hsl[chains[n][0]]])
                 for n in live}
            new = list(carry)
            pb, alpha = {}, {}
            for n in live:
                m, l, _ = carry[n]
                sn = s[n]
                if not isinstance(masks[n], str):
                    sn = jnp.where(masks[n], sn, NEG_BIG)
                m_new = jnp.maximum(m, jnp.max(sn, axis=-1, keepdims=True))
                alpha[n] = jnp.exp2(m - m_new)
                p = jnp.exp2(sn - m_new)
                pb[n] = p.astype(BF16)
                new[n] = (m_new, alpha[n] * l + jnp.sum(p, axis=-1, keepdims=True), None)
            pv = {n: _dot(pb[n], vblk) for n in live}
            for n in live:
                new[n] = (new[n][0], new[n][1], alpha[n] * carry[n][2] + pv[n])
            return tuple(new)

        init = (jnp.full((th, 1), NEG_BIG, F32), jnp.zeros((th, 1), F32), jnp.zeros((th, LANES), F32))
        carry = lax.fori_loop(0, i * nsub, lambda kb, cy: step(kb, cy, None), (init,) * len(chains))
        for sub in range(nsub):
            carry = step(i * nsub + sub, carry, sub)
        for c in range(nh):
            (m0, l0, a0), (m1, l1, a1) = carry[c], carry[nh + c]
            rs = slice(c * th, (c + 1) * th)
            o_ref[rs, :] = jnp.where(lane < 64, a0 / l0, a1 / l1).astype(BF16)
            lse_ref[rs, :] = jnp.where(lane < 64, m0 + jnp.log2(l0), m1 + jnp.log2(l1))
        pl.when((pl.program_id(0) == 3) & (i == nq - 1))(ride.finish)

    outs = pl.pallas_call(
        body, name="mla_fwd", grid=(4, nq),
        in_specs=[pl.BlockSpec((tq, 2 * HEAD_PAD), lambda p, i: (i, p)),
                  pl.BlockSpec((s_len, 2 * HEAD_PAD), lambda p, i: (0, p)),
                  pl.BlockSpec((s_len, LANES), lambda p, i: (0, p))] + ride.specs,
        out_specs=[pl.BlockSpec((tq, LANES), lambda p, i: (i, p)),
                   pl.BlockSpec((None, tq, LANES), lambda p, i: (p, i, 0))] + ride.specs,
        out_shape=[jax.ShapeDtypeStruct((s_len, 512), BF16),
                   jax.ShapeDtypeStruct((4, s_len, LANES), F32)] + ride.out_shape,
        scratch_shapes=ride.scratch,
        compiler_params=_cparams(40, dimension_semantics=("arbitrary", "arbitrary")),
    )(q, k, v, *ride.srcs)
    return outs[0], outs[1], outs[2:]


def _mla_bwd(q, k, v, o, do, lse, ride):
    s_len = q.shape[0]
    tq, th, tk, nsub = _attn_blocks(s_len, MLA_TK)
    chains = _chains(tq, th)
    nq = s_len // tq

    def body(q_ref, k_ref, v_ref, o_ref, do_ref, lse_ref, *rest):
        dq_ref, dk_ref, dv_ref = rest[ride.n:ride.n + 3]
        ride.bind(rest[:ride.n], rest[ride.n + 3:2 * ride.n + 3], rest[2 * ride.n + 3:])
        pl.when((pl.program_id(0) == 0) & (pl.program_id(1) == 0))(ride.issue)
        i = pl.program_id(1)
        lane = lax.broadcasted_iota(jnp.int32, (th, LANES), 1)

        @pl.when(i == 0)
        def _():
            dk_ref[...] = jnp.zeros_like(dk_ref)
            dv_ref[...] = jnp.zeros_like(dv_ref)

        hsl = [slice(hh * HEAD_PAD, (hh + 1) * HEAD_PAD) for hh in range(2)]
        qs, dos, deltas, lses = [], [], [], []
        for hh, r0 in chains:
            rs = slice(r0, r0 + th)
            qs.append(q_ref[rs, hsl[hh]])
            doh = jnp.where((lane // 64) == hh, do_ref[rs, :], jnp.zeros((), BF16))
            dos.append(doh)
            deltas.append(jnp.sum(doh.astype(F32) * o_ref[rs, :].astype(F32), axis=-1, keepdims=True))
            lses.append(lse_ref[rs, 64 * hh:64 * hh + 1])

        def step(kb, dqs, sub):
            rows = pl.ds(pl.multiple_of(kb * tk, tk), tk)
            vblk = v_ref[rows, :]
            new, p_all, do_all = [], [], []
            ds_h, q_h = [[], []], [[], []]
            for c, (hh, r0) in enumerate(chains):
                mask = "all" if sub is None else _diag_mask(th, tk, r0, sub, strict=False)
                if isinstance(mask, str) and mask == "none":
                    new.append(dqs[c])
                    continue
                kblk = k_ref[rows, hsl[hh]]
                s = _dot_nt(qs[c], kblk)
                if not isinstance(mask, str):
                    s = jnp.where(mask, s, NEG_BIG)
                p = jnp.exp2(s - lses[c])
                dp = _dot_nt(dos[c], vblk)
                ds = (p * (dp - deltas[c]) * MLA_SCALE).astype(BF16)
                p_all.append(p.astype(BF16))
                do_all.append(dos[c])
                ds_h[hh].append(ds)
                q_h[hh].append(qs[c])
                new.append(dqs[c] + _dot(ds, kblk))
            dv_ref[rows, :] += _dot_tn(jnp.concatenate(p_all, axis=0), jnp.concatenate(do_all, axis=0))
            for hh in range(2):
                dk_ref[rows, hsl[hh]] += _dot_tn(jnp.concatenate(ds_h[hh], axis=0),
                                                 jnp.concatenate(q_h[hh], axis=0))
            return tuple(new)

        zero = jnp.zeros((th, LANES), F32)
        dqs = lax.fori_loop(0, i * nsub, lambda kb, cy: step(kb, cy, None), (zero,) * len(chains))
        for sub in range(nsub):
            dqs = step(i * nsub + sub, dqs, sub)
        for c, (hh, r0) in enumerate(chains):
            dq_ref[r0:r0 + th, hsl[hh]] = dqs[c]
        pl.when((pl.program_id(0) == 3) & (i == nq - 1))(ride.finish)

    outs = pl.pallas_call(
        body, name="mla_bwd", grid=(4, nq),
        in_specs=[pl.BlockSpec((tq, 2 * HEAD_PAD), lambda p, i: (i, p)),
                  pl.BlockSpec((s_len, 2 * HEAD_PAD), lambda p, i: (0, p)),
                  pl.BlockSpec((s_len, LANES), lambda p, i: (0, p)),
                  pl.BlockSpec((tq, LANES), lambda p, i: (i, p)),
                  pl.BlockSpec((tq, LANES), lambda p, i: (i, p)),
                  pl.BlockSpec((None, tq, LANES), lambda p, i: (p, i, 0))] + ride.specs,
        out_specs=[pl.BlockSpec((tq, 2 * HEAD_PAD), lambda p, i: (i, p)),
                   pl.BlockSpec((s_len, 2 * HEAD_PAD), lambda p, i: (0, p)),
                   pl.BlockSpec((s_len, LANES), lambda p, i: (0, p))] + ride.specs,
        out_shape=[jax.ShapeDtypeStruct((s_len, 1024), F32), jax.ShapeDtypeStruct((s_len, 1024), F32),
                   jax.ShapeDtypeStruct((s_len, 512), F32)] + ride.out_shape,
        scratch_shapes=ride.scratch,
        compiler_params=_cparams(56, dimension_semantics=("arbitrary", "arbitrary")),
    )(q, k, v, o, do, lse, *ride.srcs)
    return outs[0], outs[1], outs[2], outs[3:]


def _log_sigmoids(z2):
    sp = jnp.log2(1.0 + jnp.exp2(-jnp.abs(z2)))
    lb = jnp.minimum(z2, 0.0) - sp
    return lb, lb - z2


def _split_dot(x, w, parts, nt=False):
    dot = _dot_nt if nt else _dot
    out = None
    for _ in range(parts):
        xb = x.astype(BF16)
        t = dot(xb, w)
        out = t if out is None else out + t
        x = x - xb.astype(F32)
    return out


def _sb_fwd(sb):
    s_len = sb.shape[0]
    tq, th, tk, nsub = _attn_blocks(s_len)
    chains = _chains(tq, th)
    nh = tq // th
    assert s_len // tk <= 64

    def body(q_ref, k_ref, v_ref, o_ref, r_ref):
        i = pl.program_id(1)
        lane = lax.broadcasted_iota(jnp.int32, (th, LANES), 1)
        upper = (lax.broadcasted_iota(jnp.int32, (tk, tk), 0)
                 > lax.broadcasted_iota(jnp.int32, (tk, tk), 1)).astype(BF16)
        qs = [jnp.where((lane // 64) == hh, q_ref[r0:r0 + th, :], jnp.zeros((), BF16)) for hh, r0 in chains]

        def step(kb, carry, sub):
            rows = pl.ds(pl.multiple_of(kb * tk, tk), tk)
            kblk, vblk = k_ref[rows, :], v_ref[rows, :]
            masks = ["all" if sub is None else _diag_mask(th, tk, r0, sub, strict=True) for _, r0 in chains]
            live = [n for n, m in enumerate(masks) if not (isinstance(m, str) and m == "none")]
            masked = {n: not isinstance(masks[n], str) for n in live}
            z = {n: _dot_nt(qs[n], kblk) for n in live}
            lb, lom = {}, {}
            for n in live:
                lb[n], lom[n] = _log_sigmoids(z[n])
                if masked[n]:
                    lom[n] = jnp.where(masks[n], lom[n], 0.0)
            suf = {n: _split_dot(lom[n], upper, 2) for n in live}
            a = {}
            for n in live:
                a[n] = jnp.exp2(lb[n] + suf[n] + carry[n][0])
                if masked[n]:
                    a[n] = jnp.where(masks[n], a[n], 0.0)
            pv = {n: _dot(a[n].astype(BF16), vblk) for n in live}
            new = list(carry)
            for n in live:
                c, acc, r = carry[n]
                rs = suf[n][:, 0:1] + lom[n][:, 0:1]
                new[n] = (c + rs, acc + pv[n], jnp.where(lane == 64 * chains[n][0] + kb, rs, r))
            return tuple(new)

        init = (jnp.zeros((th, 1), F32), jnp.zeros((th, LANES), F32), jnp.zeros((th, LANES), F32))
        carry = (init,) * len(chains)
        for sub in reversed(range(nsub)):
            carry = step(i * nsub + sub, carry, sub)

        def spent(cy):
            top = functools.reduce(jnp.maximum, [jnp.max(c) for c, _, _ in cy])
            return (top < SB_CUT).astype(jnp.int32)

        def walk(state):
            t, _, cy = state
            cy = step(i * nsub - 1 - t, cy, None)
            return t + 1, spent(cy), cy

        _, _, carry = lax.while_loop(lambda st: (st[0] < i * nsub) & (st[1] == 0), walk,
                                     (jnp.int32(0), spent(carry), carry))
        for n in range(nh):
            rs = slice(n * th, (n + 1) * th)
            o_ref[rs, :] = jnp.where(lane < 64, carry[n][1], carry[nh + n][1]).astype(BF16)
            r_ref[rs, :] = jnp.where(lane < 64, carry[n][2], carry[nh + n][2])

    return pl.pallas_call(
        body, name="sb_fwd", grid=(4, s_len // tq),
        in_specs=[pl.BlockSpec((tq, LANES), lambda p, i: (i, p)),
                  pl.BlockSpec((s_len, LANES), lambda p, i: (0, 4 + p)),
                  pl.BlockSpec((s_len, LANES), lambda p, i: (0, 8 + p))],
        out_specs=[pl.BlockSpec((tq, LANES), lambda p, i: (i, p)),
                   pl.BlockSpec((None, tq, LANES), lambda p, i: (p, i, 0))],
        out_shape=[jax.ShapeDtypeStruct((s_len, 512), BF16), jax.ShapeDtypeStruct((4, s_len, LANES), F32)],
        compiler_params=_cparams(40, dimension_semantics=("parallel", "arbitrary")),
    )(sb, sb, sb)


def _sb_bwd(sb, do, r):
    s_len = sb.shape[0]
    tq, th, tk, nsub = _attn_blocks(s_len)
    chains = _chains(tq, th)
    nh = tq // th

    def body(q_ref, k_ref, v_ref, do_ref, r_ref, dq_ref, dk_ref, dv_ref):
        i = pl.program_id(1)
        lane = lax.broadcasted_iota(jnp.int32, (th, LANES), 1)
        upper = (lax.broadcasted_iota(jnp.int32, (tk, tk), 0)
                 > lax.broadcasted_iota(jnp.int32, (tk, tk), 1)).astype(BF16)
        tri = (lax.broadcasted_iota(jnp.int32, (LANES, LANES), 0)
               > lax.broadcasted_iota(jnp.int32, (LANES, LANES), 1)).astype(BF16)

        @pl.when(i == 0)
        def _():
            dk_ref[...] = jnp.zeros_like(dk_ref)
            dv_ref[...] = jnp.zeros_like(dv_ref)

        qs, dos, rights = [], [], []
        for hh, r0 in chains:
            rs = slice(r0, r0 + th)
            hm = (lane // 64) == hh
            qs.append(jnp.where(hm, q_ref[rs, :], jnp.zeros((), BF16)))
            dos.append(jnp.where(hm, do_ref[rs, :], jnp.zeros((), BF16)))
            rights.append(_split_dot(jnp.where(hm, r_ref[rs, :], 0.0), tri, 3))

        def step(kb, carry, sub):
            rows = pl.ds(pl.multiple_of(kb * tk, tk), tk)
            kblk, vblk = k_ref[rows, :], v_ref[rows, :]
            new, a_all, do_all, dz_all, q_all = [], [], [], [], []
            for n, ((hh, r0), (pre, dq)) in enumerate(zip(chains, carry)):
                mask = "all" if sub is None else _diag_mask(th, tk, r0, sub, strict=True)
                if isinstance(mask, str) and mask == "none":
                    new.append((pre, dq))
                    continue
                c = jnp.sum(jnp.where(lane == 64 * hh + kb, rights[n], 0.0), axis=-1, keepdims=True)
                z = _dot_nt(qs[n], kblk)
                lb, lom = _log_sigmoids(z)
                if not isinstance(mask, str):
                    lom = jnp.where(mask, lom, 0.0)
                suf = _split_dot(lom, upper, 2)
                a = jnp.exp2(lb + suf + c)
                if not isinstance(mask, str):
                    a = jnp.where(mask, a, 0.0)
                g = a * _dot_nt(dos[n], vblk)
                left = _split_dot(g, upper, 1, nt=True) + pre
                sig = jnp.exp2(lb)
                dz = g * (1.0 - sig) - sig * left
                if not isinstance(mask, str):
                    dz = jnp.where(mask, dz, 0.0)
                dzb = dz.astype(BF16)
                a_all.append(a.astype(BF16))
                do_all.append(dos[n])
                dz_all.append(dzb)
                q_all.append(qs[n])
                new.append((left[:, tk - 1:tk] + g[:, tk - 1:tk], dq + _dot(dzb, kblk)))
            dv_ref[rows, :] += _dot_tn(jnp.concatenate(a_all, axis=0), jnp.concatenate(do_all, axis=0))
            dk_ref[rows, :] += _dot_tn(jnp.concatenate(dz_all, axis=0), jnp.concatenate(q_all, axis=0))
            return tuple(new)

        lane1 = lax.broadcasted_iota(jnp.int32, (1, LANES), 1)
        first = i * nsub
        for n, (hh, _) in enumerate(chains):
            top = jnp.max(rights[n], axis=0, keepdims=True)
            kb_of = lane1 - 64 * hh
            live = (kb_of >= 0) & (kb_of < i * nsub) & (top >= SB_CUT)
            first = jnp.minimum(first, jnp.min(jnp.where(live, kb_of, i * nsub)))

        init = (jnp.zeros((th, 1), F32), jnp.zeros((th, LANES), F32))
        carry = lax.fori_loop(first, i * nsub, lambda kb, cy: step(kb, cy, None), (init,) * len(chains))
        for sub in range(nsub):
            carry = step(i * nsub + sub, carry, sub)
        for n in range(nh):
            dq_ref[n * th:(n + 1) * th, :] = jnp.where(lane < 64, carry[n][1], carry[nh + n][1]) * SB_SCALE

    return pl.pallas_call(
        body, name="sb_bwd", grid=(4, s_len // tq),
        in_specs=[pl.BlockSpec((tq, LANES), lambda p, i: (i, p)),
                  pl.BlockSpec((s_len, LANES), lambda p, i: (0, 4 + p)),
                  pl.BlockSpec((s_len, LANES), lambda p, i: (0, 8 + p)),
                  pl.BlockSpec((tq, LANES), lambda p, i: (i, p)),
                  pl.BlockSpec((None, tq, LANES), lambda p, i: (p, i, 0))],
        out_specs=[pl.BlockSpec((tq, LANES), lambda p, i: (i, p)),
                   pl.BlockSpec((s_len, LANES), lambda p, i: (0, p)),
                   pl.BlockSpec((s_len, LANES), lambda p, i: (0, p))],
        out_shape=[jax.ShapeDtypeStruct((s_len, 512), F32)] * 3,
        compiler_params=_cparams(48, dimension_semantics=("arbitrary", "arbitrary")),
    )(sb, sb, sb, do, r)


def _merge_fwd(x, oa, ob, gates, bg, wa, wb, wo):
    s_len = x.shape[0]
    tm = _row_block(s_len)

    def body(x_ref, oa_ref, ob_ref, g_ref, bg_ref, wa_ref, wb_ref, wo_ref, y_ref):
        pa = _dot(oa_ref[...], wa_ref[...])
        pb = _dot(ob_ref[...], wb_ref[...])
        merged = (_sigmoid(g_ref[:, 0:D_MODEL] + bg_ref[0:1, :]) * pa
                  + _sigmoid(g_ref[:, D_MODEL:2 * D_MODEL] + bg_ref[1:2, :]) * pb)
        y_ref[...] = x_ref[...] + _dot(merged.astype(BF16), wo_ref[...])

    full = lambda shape: pl.BlockSpec(shape, lambda i: (0, 0))
    rowb = lambda n: pl.BlockSpec((tm, n), lambda i: (i, 0))
    return pl.pallas_call(
        body, name="merge_fwd", grid=(s_len // tm,),
        in_specs=[rowb(1024), rowb(512), rowb(512), rowb(2048), full((2, 1024)), full((512, 1024)),
                  full((512, 1024)), full((1024, 1024))],
        out_specs=rowb(1024),
        out_shape=jax.ShapeDtypeStruct((s_len, D_MODEL), F32),
        compiler_params=_cparams(48, dimension_semantics=("parallel",)),
    )(x, oa, ob, gates, bg, wa, wb, wo)


def _merge_bwd(dx1, oa, ob, gates, bg, wa, wb, wo):
    s_len = dx1.shape[0]
    tm = _row_block(s_len)

    def body(dx_ref, oa_ref, ob_ref, g_ref, bg_ref, wa_ref, wb_ref, wo_ref,
             doa_ref, dob_ref, dgate_ref, dpa_ref, dpb_ref, merged_ref, dxb_ref, dbg_ref):
        first = pl.program_id(0) == 0
        dxb = dx_ref[...].astype(BF16)
        dxb_ref[...] = dxb
        pa = _dot(oa_ref[...], wa_ref[...])
        pb = _dot(ob_ref[...], wb_ref[...])
        sa = _sigmoid(g_ref[:, 0:D_MODEL] + bg_ref[0:1, :])
        sbg = _sigmoid(g_ref[:, D_MODEL:2 * D_MODEL] + bg_ref[1:2, :])
        merged_ref[...] = (sa * pa + sbg * pb).astype(BF16)
        dm = _dot_nt(dxb, wo_ref[...])
        dpa = (dm * sa).astype(BF16)
        dpb = (dm * sbg).astype(BF16)
        dpa_ref[...] = dpa
        dpb_ref[...] = dpb
        dga = dm * pa * sa * (1.0 - sa)
        dgb = dm * pb * sbg * (1.0 - sbg)
        dgate_ref[:, 0:D_MODEL] = dga.astype(BF16)
        dgate_ref[:, D_MODEL:2 * D_MODEL] = dgb.astype(BF16)
        _acc_rows(dbg_ref.at[0:1, :], dga, first)
        _acc_rows(dbg_ref.at[1:2, :], dgb, first)
        doa_ref[...] = _dot_nt(dpa, wa_ref[...]).astype(BF16)
        dob_ref[...] = _dot_nt(dpb, wb_ref[...]).astype(BF16)

    full = lambda shape: pl.BlockSpec(shape, lambda i: (0, 0))
    rowb = lambda n: pl.BlockSpec((tm, n), lambda i: (i, 0))
    sds = lambda n, dt: jax.ShapeDtypeStruct((s_len, n), dt)
    return pl.pallas_call(
        body, name="merge_bwd", grid=(s_len // tm,),
        in_specs=[rowb(1024), rowb(512), rowb(512), rowb(2048), full((2, 1024)), full((512, 1024)),
                  full((512, 1024)), full((1024, 1024))],
        out_specs=[rowb(512), rowb(512), rowb(2048), rowb(1024), rowb(1024), rowb(1024), rowb(1024),
                   full((2, 1024))],
        out_shape=[sds(512, BF16), sds(512, BF16), sds(2048, BF16), sds(1024, BF16), sds(1024, BF16),
                   sds(1024, BF16), sds(1024, BF16), jax.ShapeDtypeStruct((2, 1024), F32)],
        compiler_params=_cparams(48, dimension_semantics=("arbitrary",)),
    )(dx1, oa, ob, gates, bg, wa, wb, wo)


def _mem_kv(mem, g, w):
    m_len = mem.shape[0]

    def body(mem_ref, g_ref, w_ref, mn_ref, kv_ref):
        mn, _, _ = _rms(mem_ref[...], g_ref[...])
        mnb = mn.astype(BF16)
        mn_ref[...] = mnb
        kv_ref[...] = _dot(mnb, w_ref[...]).astype(BF16)

    return pl.pallas_call(
        body, name="mem_kv",
        out_shape=[jax.ShapeDtypeStruct((m_len, D_MODEL), BF16), jax.ShapeDtypeStruct((m_len, 1024), BF16)],
    )(mem, g, w)


def _mem_bwd(mem, g, w, mn, dkv):
    def body(mem_ref, g_ref, w_ref, mn_ref, dkv_ref, dw_ref, dg_ref):
        dkvb = dkv_ref[...].astype(BF16)
        dw_ref[...] = _dot_tn(mn_ref[...], dkvb)
        dmn = _dot_nt(dkvb, w_ref[...])
        _, xh, _ = _rms(mem_ref[...], g_ref[...])
        dg_ref[...] = jnp.sum(dmn * xh, axis=0, keepdims=True)

    return pl.pallas_call(
        body, name="mem_bwd",
        out_shape=[jax.ShapeDtypeStruct((D_MODEL, 1024), F32), jax.ShapeDtypeStruct((1, D_MODEL), F32)],
    )(mem, g, w, mn, dkv)


def _xattn_heads(xqb, kv_ref, m_len):
    ps = []
    for h in range(X_HEADS):
        hs = slice(h * X_HEAD_DIM, (h + 1) * X_HEAD_DIM)
        s = _dot_nt(xqb[:, hs], kv_ref[:, hs]) * X_SCALE
        e = jnp.exp(s - jnp.max(s, axis=-1, keepdims=True))
        ps.append(e / jnp.sum(e, axis=-1, keepdims=True))
    return ps


def _xattn_fwd(x1, g, wxq, kv, wxo):
    s_len, m_len = x1.shape[0], kv.shape[0]
    tm = _row_block(s_len)

    def body(x_ref, g_ref, wq_ref, kv_ref, wo_ref, y_ref):
        hx, _, _ = _rms(x_ref[...], g_ref[...])
        xqb = _dot(hx.astype(BF16), wq_ref[...]).astype(BF16)
        ps = _xattn_heads(xqb, kv_ref, m_len)
        xo = jnp.concatenate(
            [_dot(ps[h].astype(BF16), kv_ref[:, 512 + h * X_HEAD_DIM:512 + (h + 1) * X_HEAD_DIM])
             for h in range(X_HEADS)], axis=-1)
        y_ref[...] = x_ref[...] + _dot(xo.astype(BF16), wo_ref[...])

    full = lambda shape: pl.BlockSpec(shape, lambda i: (0, 0))
    rowb = lambda n: pl.BlockSpec((tm, n), lambda i: (i, 0))
    return pl.pallas_call(
        body, name="xattn_fwd", grid=(s_len // tm,),
        in_specs=[rowb(1024), full((1, 1024)), full((1024, 512)), full((m_len, 1024)), full((512, 1024))],
        out_specs=rowb(1024),
        out_shape=jax.ShapeDtypeStruct((s_len, D_MODEL), F32),
        compiler_params=_cparams(48, dimension_semantics=("parallel",)),
    )(x1, g, wxq, kv, wxo)


def _xattn_bwd(x1, dx2, g, wxq, kv, wxo):
    s_len, m_len = x1.shape[0], kv.shape[0]
    tm = _row_block(s_len)

    def body(x_ref, dy_ref, g_ref, wq_ref, kv_ref, wo_ref, dx_ref, dwq_ref, dwo_ref, dkv_ref, dg_ref):
        first = pl.program_id(0) == 0
        gv = g_ref[...]
        hx, xh, r = _rms(x_ref[...], gv)
        hxb = hx.astype(BF16)
        xqb = _dot(hxb, wq_ref[...]).astype(BF16)
        ps = _xattn_heads(xqb, kv_ref, m_len)
        dy = dy_ref[...]
        dyb = dy.astype(BF16)
        dxo = _dot_nt(dyb, wo_ref[...])
        xos, dqs, dks, dvs = [], [], [], []
        for h in range(X_HEADS):
            hs = slice(h * X_HEAD_DIM, (h + 1) * X_HEAD_DIM)
            vs = slice(512 + h * X_HEAD_DIM, 512 + (h + 1) * X_HEAD_DIM)
            p = ps[h]
            pb = p.astype(BF16)
            dxoh = dxo[:, hs].astype(BF16)
            xos.append(_dot(pb, kv_ref[:, vs]))
            dp = _dot_nt(dxoh, kv_ref[:, vs])
            ds = (p * (dp - jnp.sum(dp * p, axis=-1, keepdims=True)) * X_SCALE).astype(BF16)
            dvs.append(_dot_tn(pb, dxoh))
            dks.append(_dot_tn(ds, xqb[:, hs]))
            dqs.append(_dot(ds, kv_ref[:, hs]))
        xob = jnp.concatenate(xos, axis=-1).astype(BF16)
        dxqb = jnp.concatenate(dqs, axis=-1).astype(BF16)
        _acc(dwo_ref, _dot_tn(xob, dyb), first)
        _acc(dwq_ref, _dot_tn(hxb, dxqb), first)
        _acc(dkv_ref, jnp.concatenate(dks + dvs, axis=-1), first)
        dhx = _dot_nt(dxqb, wq_ref[...])
        dx, dgr = _rms_bwd(dhx, xh, r, gv)
        dx_ref[...] = dy + dx
        _acc_rows(dg_ref, dgr, first)

    full = lambda shape: pl.BlockSpec(shape, lambda i: (0, 0))
    rowb = lambda n: pl.BlockSpec((tm, n), lambda i: (i, 0))
    return pl.pallas_call(
        body, name="xattn_bwd", grid=(s_len // tm,),
        in_specs=[rowb(1024), rowb(1024), full((1, 1024)), full((1024, 512)), full((m_len, 1024)),
                  full((512, 1024))],
        out_specs=[rowb(1024), full((1024, 512)), full((512, 1024)), full((m_len, 1024)), full((1, 1024))],
        out_shape=[jax.ShapeDtypeStruct((s_len, D_MODEL), F32), jax.ShapeDtypeStruct((1024, 512), F32),
                   jax.ShapeDtypeStruct((512, 1024), F32), jax.ShapeDtypeStruct((m_len, 1024), F32),
                   jax.ShapeDtypeStruct((1, D_MODEL), F32)],
        compiler_params=_cparams(48, dimension_semantics=("arbitrary",)),
    )(x1, dx2, g, wxq, kv, wxo)


FF_TILE = 1408


def _ffn_fwd(x2, g, wg, wu, wd):
    s_len = x2.shape[0]
    tm, tf = _row_block(s_len), FF_TILE

    def body(x_ref, g_ref, wg_ref, wu_ref, wd_ref, y_ref, h_scr):
        j = pl.program_id(1)

        @pl.when(j == 0)
        def _():
            hf, _, _ = _rms(x_ref[...], g_ref[...])
            h_scr[...] = hf.astype(BF16)
            y_ref[...] = x_ref[...]

        hb = h_scr[...]
        gt = _dot(hb, wg_ref[...])
        up = _dot(hb, wu_ref[...])
        act = gt * _sigmoid(gt) * up
        y_ref[...] += _dot(act.astype(BF16), wd_ref[...])

    return pl.pallas_call(
        body, name="ffn_fwd", grid=(s_len // tm, D_FF // tf),
        in_specs=[pl.BlockSpec((tm, D_MODEL), lambda i, j: (i, 0)),
                  pl.BlockSpec((1, D_MODEL), lambda i, j: (0, 0)),
                  pl.BlockSpec((D_MODEL, tf), lambda i, j: (0, j)),
                  pl.BlockSpec((D_MODEL, tf), lambda i, j: (0, j)),
                  pl.BlockSpec((tf, D_MODEL), lambda i, j: (j, 0))],
        out_specs=pl.BlockSpec((tm, D_MODEL), lambda i, j: (i, 0)),
        out_shape=jax.ShapeDtypeStruct((s_len, D_MODEL), F32),
        scratch_shapes=[pltpu.VMEM((tm, D_MODEL), BF16)],
        compiler_params=_cparams(48, dimension_semantics=("parallel", "arbitrary")),
    )(x2, g, wg, wu, wd)


def _ffn_bwd(x2, dx3, g, wg, wu, wd):
    s_len = x2.shape[0]
    tm, tf = min(s_len, 256), FF_TILE
    nf = D_FF // tf

    def act_body(x_ref, dy_ref, g_ref, wg_ref, wu_ref, wd_ref, h_ref, dgt_ref, dup_ref, act_ref, h_scr, dyb_scr):
        @pl.when(pl.program_id(1) == 0)
        def _():
            hf, _, _ = _rms(x_ref[...], g_ref[...])
            hb = hf.astype(BF16)
            h_scr[...] = hb
            h_ref[...] = hb
            dyb_scr[...] = dy_ref[...].astype(BF16)

        hb = h_scr[...]
        gt = _dot(hb, wg_ref[...])
        up = _dot(hb, wu_ref[...])
        sg = _sigmoid(gt)
        silu = gt * sg
        dact = _dot_nt(dyb_scr[...], wd_ref[...])
        dgt_ref[...] = (dact * up * (sg * (1.0 + gt * (1.0 - sg)))).astype(BF16)
        dup_ref[...] = (dact * silu).astype(BF16)
        act_ref[...] = (silu * up).astype(BF16)

    rowb = pl.BlockSpec((tm, D_MODEL), lambda i, j: (i, 0))
    ffb = pl.BlockSpec((tm, tf), lambda i, j: (i, j))
    hf, dgt, dup, act = pl.pallas_call(
        act_body, name="ffn_bwd_act", grid=(s_len // tm, nf),
        in_specs=[rowb, rowb, pl.BlockSpec((1, D_MODEL), lambda i, j: (0, 0)),
                  pl.BlockSpec((D_MODEL, tf), lambda i, j: (0, j)),
                  pl.BlockSpec((D_MODEL, tf), lambda i, j: (0, j)),
                  pl.BlockSpec((tf, D_MODEL), lambda i, j: (j, 0))],
        out_specs=[rowb, ffb, ffb, ffb],
        out_shape=[jax.ShapeDtypeStruct((s_len, D_MODEL), BF16), jax.ShapeDtypeStruct((s_len, D_FF), BF16),
                   jax.ShapeDtypeStruct((s_len, D_FF), BF16), jax.ShapeDtypeStruct((s_len, D_FF), BF16)],
        scratch_shapes=[pltpu.VMEM((tm, D_MODEL), BF16), pltpu.VMEM((tm, D_MODEL), BF16)],
        compiler_params=_cparams(56, dimension_semantics=("parallel", "arbitrary")),
    )(x2, dx3, g, wg, wu, wd)

    def in_body(x_ref, dy_ref, g_ref, wg_ref, wu_ref, dgt_ref, dup_ref, dx_ref, dg_ref):
        dh = _dot_nt(dgt_ref[...], wg_ref[...]) + _dot_nt(dup_ref[...], wu_ref[...])
        gv = g_ref[...]
        _, xh, r = _rms(x_ref[...], gv)
        dx, dgr = _rms_bwd(dh, xh, r, gv)
        dx_ref[...] = dy_ref[...] + dx
        _acc_rows(dg_ref, dgr, pl.program_id(0) == 0)

    row1 = lambda n: pl.BlockSpec((tm, n), lambda i: (i, 0))
    full = lambda shape: pl.BlockSpec(shape, lambda i: (0, 0))
    dx2, dg = pl.pallas_call(
        in_body, name="ffn_bwd_in", grid=(s_len // tm,),
        in_specs=[row1(D_MODEL), row1(D_MODEL), full((1, D_MODEL)), full((D_MODEL, D_FF)), full((D_MODEL, D_FF)),
                  row1(D_FF), row1(D_FF)],
        out_specs=[row1(D_MODEL), full((1, D_MODEL))],
        out_shape=[jax.ShapeDtypeStruct((s_len, D_MODEL), F32), jax.ShapeDtypeStruct((1, D_MODEL), F32)],
        compiler_params=_cparams(48, dimension_semantics=("arbitrary",)),
    )(x2, dx3, g, wg, wu, dgt, dup)
    return dx2, hf, dgt, dup, act, dg


def _loss_head(x3, g, target):
    s_len = x3.shape[0]
    tm = _row_block(s_len)

    def body(x_ref, g_ref, t_ref, sse_ref, dx_ref, dxb_ref, dg_ref):
        first = pl.program_id(0) == 0
        gv = g_ref[...]
        y, xh, r = _rms(x_ref[...], gv)
        err = y - t_ref[...]
        _acc(sse_ref, jnp.broadcast_to(jnp.sum(err * err), (8, LANES)), first)
        dx, dgr = _rms_bwd(err * (1.0 / D_MODEL), xh, r, gv)
        dx_ref[...] = dx
        dxb_ref[...] = dx.astype(BF16)
        _acc_rows(dg_ref, dgr, first)

    rowb = pl.BlockSpec((tm, D_MODEL), lambda i: (i, 0))
    return pl.pallas_call(
        body, name="loss_head", grid=(s_len // tm,),
        in_specs=[rowb, pl.BlockSpec((1, D_MODEL), lambda i: (0, 0)), rowb],
        out_specs=[pl.BlockSpec((8, LANES), lambda i: (0, 0)), rowb, rowb,
                   pl.BlockSpec((1, D_MODEL), lambda i: (0, 0))],
        out_shape=[jax.ShapeDtypeStruct((8, LANES), F32), jax.ShapeDtypeStruct((s_len, D_MODEL), F32),
                   jax.ShapeDtypeStruct((s_len, D_MODEL), BF16), jax.ShapeDtypeStruct((1, D_MODEL), F32)],
        compiler_params=_cparams(dimension_semantics=("arbitrary",)),
    )(x3, g, target)


def _mla_prep_bwd(lat, g_q, g_kv, w_uq, w_uk, w_uv, cosf, sinf, dq, dk, dv):
    s_len = lat.shape[0]
    tm = _row_block(s_len)

    def body(lat_ref, gq_ref, gkv_ref, wuq_ref, wuk_ref, wuv_ref, cos_ref, sin_ref, dq_ref, dk_ref, dv_ref,
             dlat_ref, dqb_ref, dkb_ref, dvb_ref, dgq_ref, dgkv_ref):
        first = pl.program_id(0) == 0
        lane = lax.broadcasted_iota(jnp.int32, (tm, LANES), 1)
        cosv, sinv = cos_ref[...], sin_ref[...]
        gq, gkv = gq_ref[...], gkv_ref[...]
        _, qxh, qr = _rms(lat_ref[:, 0:256], gq)
        _, kxh, kr_ = _rms(lat_ref[:, 256:384], gkv)
        dkr = jnp.zeros((tm, LANES), F32)
        for h in range(MLA_HEADS):
            sl = slice(h * HEAD_PAD, (h + 1) * HEAD_PAD)
            blk = dq_ref[:, sl]
            dqb_ref[:, sl] = (blk * cosv + _rope_rot_t(blk, lane) * sinv).astype(BF16)
            kblk = dk_ref[:, sl] * (1.0 / MLA_Q_FOLD)
            dkb_ref[:, sl] = kblk.astype(BF16)
            dkr = dkr + kblk
        dvb = dv_ref[...].astype(BF16)
        dvb_ref[...] = dvb
        dkr = jnp.where((lane >= 64) & (lane < 96), dkr, 0.0)
        dkr = dkr * cosv + _rope_rot_t(dkr, lane) * sinv
        dql = _dot_nt(dqb_ref[...], wuq_ref[...])
        dkvl = _dot_nt(dkb_ref[...], wuk_ref[...]) + _dot_nt(dvb, wuv_ref[...])
        dcq, dgqr = _rms_bwd(dql, qxh, qr, gq)
        dckv, dgkvr = _rms_bwd(dkvl, kxh, kr_, gkv)
        dlat_ref[:, 0:256] = dcq
        dlat_ref[:, 256:384] = dckv
        dlat_ref[:, K_R_OFF:K_R_OFF + LANES] = pltpu.roll(dkr, 64, 1)
        _acc_rows(dgq_ref, dgqr, first)
        _acc_rows(dgkv_ref, dgkvr, first)

    full = lambda shape: pl.BlockSpec(shape, lambda i: (0, 0))
    rowb = lambda n: pl.BlockSpec((tm, n), lambda i: (i, 0))
    sds = lambda n, dt: jax.ShapeDtypeStruct((s_len, n), dt)
    return pl.pallas_call(
        body, name="mla_prep_bwd", grid=(s_len // tm,),
        in_specs=[rowb(512), full((1, 256)), full((1, 128)), full((256, 1024)), full((128, 1024)),
                  full((128, 512)), rowb(128), rowb(128), rowb(1024), rowb(1024), rowb(512)],
        out_specs=[rowb(512), rowb(1024), rowb(1024), rowb(512), full((1, 256)), full((1, 128))],
        out_shape=[sds(512, F32), sds(1024, BF16), sds(1024, BF16), sds(512, BF16),
                   jax.ShapeDtypeStruct((1, 256), F32), jax.ShapeDtypeStruct((1, 128), F32)],
        compiler_params=_cparams(48, dimension_semantics=("arbitrary",)),
    )(lat, g_q, g_kv, w_uq, w_uk, w_uv, cosf, sinf, dq, dk, dv)


def _in_proj_bwd(x, g, w, dx1, dlat, dsbq, dsbk, dsbv, dgates):
    s_len = x.shape[0]
    tm = min(s_len, 256)

    def body(x_ref, g_ref, w_ref, dx1_ref, dlat_ref, dq_ref, dk_ref, dv_ref, dgate_ref,
             gx_ref, dproj_ref, dg_ref):
        dproj_ref[:, 0:512] = dlat_ref[...].astype(BF16)
        dproj_ref[:, 512:1024] = dq_ref[...].astype(BF16)
        dproj_ref[:, 1024:1536] = (dk_ref[...] * LN2).astype(BF16)
        dproj_ref[:, 1536:2048] = dv_ref[...].astype(BF16)
        dproj_ref[:, 2048:4096] = dgate_ref[...]
        dh = _dot_nt(dproj_ref[...], w_ref[...])
        gv = g_ref[...]
        _, xh, r = _rms(x_ref[...], gv)
        dx, dgr = _rms_bwd(dh, xh, r, gv)
        gx_ref[...] = dx1_ref[...] + dx
        _acc_rows(dg_ref, dgr, pl.program_id(0) == 0)

    rowb = lambda n: pl.BlockSpec((tm, n), lambda i: (i, 0))
    full = lambda shape: pl.BlockSpec(shape, lambda i: (0, 0))
    return pl.pallas_call(
        body, name="in_proj_bwd", grid=(s_len // tm,),
        in_specs=[rowb(D_MODEL), full((1, D_MODEL)), full((D_MODEL, D_IN_PAD)), rowb(D_MODEL),
                  rowb(512), rowb(512), rowb(512), rowb(512), rowb(2 * D_MODEL)],
        out_specs=[rowb(D_MODEL), rowb(D_IN_PAD), full((1, D_MODEL))],
        out_shape=[jax.ShapeDtypeStruct((s_len, D_MODEL), F32), jax.ShapeDtypeStruct((s_len, D_IN_PAD), BF16),
                   jax.ShapeDtypeStruct((1, D_MODEL), F32)],
        compiler_params=_cparams(48, dimension_semantics=("arbitrary",)),
    )(x, g, w, dx1, dlat, dsbq, dsbk, dsbv, dgates)


def _adamw(landed, w, m, v, name):
    r, c = w.shape
    lanes = _round_up(c, LANES)
    tb = r
    for cand in range(r, 0, -1):
        if r % cand == 0 and (cand % 8 == 0 or cand == r) and N_DEV * cand * lanes * 4 <= ADAM_BLOCK_BYTES:
            tb = cand
            break
    c1 = 1.0 - ADAM_B1 ** ADAM_STEP
    c2 = 1.0 - ADAM_B2 ** ADAM_STEP

    def body(l_ref, w_ref, m_ref, v_ref, g_ref, d_ref, nm_ref, nv_ref):
        g = l_ref[0]
        for k in range(1, N_DEV):
            g = g + l_ref[k]
        nm = ADAM_B1 * m_ref[...] + (1.0 - ADAM_B1) * g
        nv = ADAM_B2 * v_ref[...] + (1.0 - ADAM_B2) * (g * g)
        g_ref[...] = g
        nm_ref[...] = nm
        nv_ref[...] = nv
        d_ref[...] = -ADAM_LR * ((nm / c1) / (jnp.sqrt(nv / c2) + ADAM_EPS) + ADAM_WD * w_ref[...])

    blk = pl.BlockSpec((tb, c), lambda i: (i, 0))
    return pl.pallas_call(
        body, name=name, grid=(r // tb,),
        in_specs=[pl.BlockSpec((N_DEV, tb, c), lambda i: (0, i, 0)), blk, blk, blk],
        out_specs=[blk, blk, blk, blk],
        out_shape=[jax.ShapeDtypeStruct((r, c), F32)] * 4,
        compiler_params=_cparams(dimension_semantics=("parallel",)),
    )(landed, w, m, v)


def _shard_shape(shape, axis):
    return tuple(d // N_DEV if a == axis else d for a, d in enumerate(shape))


def _split_pieces(full, axis):
    r, c = full.shape
    if axis == 0:
        return full.reshape(N_DEV, r // N_DEV, c)
    return full.reshape(r, N_DEV, c // N_DEV).transpose(1, 0, 2)


def _join_shards(gathered, axis):
    _, r, c = gathered.shape
    if axis == 0:
        return gathered.reshape(N_DEV * r, c)
    return gathered.transpose(1, 0, 2).reshape(r, N_DEV * c)


def kernel(x, mem, positions, g_mix, w_in, b_gate, g_q_lat, w_uq, g_kv_lat, w_ukv, w_a_proj, w_b_proj, w_o, g_x, g_mem, w_xq, w_xkv, w_xo, g_ffn, w_gate, w_up, w_down, g_final, loss_target, m_g_mix, m_w_in, m_b_gate, m_g_q_lat, m_w_uq, m_g_kv_lat, m_w_ukv, m_w_a_proj, m_w_b_proj, m_w_o, m_g_x, m_g_mem, m_w_xq, m_w_xkv, m_w_xo, m_g_ffn, m_w_gate, m_w_up, m_w_down, m_g_final, v_g_mix, v_w_in, v_b_gate, v_g_q_lat, v_w_uq, v_g_kv_lat, v_w_ukv, v_w_a_proj, v_w_b_proj, v_w_o, v_g_x, v_g_mem, v_w_xq, v_w_xkv, v_w_xo, v_g_ffn, v_w_gate, v_w_up, v_w_down, v_g_final):
    given = dict(locals())
    s_len = x.shape[1]
    x2d = x.reshape(s_len, D_MODEL)
    mem2d = mem.reshape(-1, D_MODEL)
    target = loss_target.reshape(s_len, D_MODEL)

    names = [name for name, _, _ in SHARDED]
    axis_of = {name: axis for name, _, axis in SHARDED}
    shard2d = lambda name, prefix="": given[prefix + name].reshape(
        _shard_shape(dict((n, s) for n, s, _ in SHARDED)[name], axis_of[name]))

    wire = lambda name: shard2d(name) if name == "b_gate" else shard2d(name).astype(BF16)
    early = [n for n in names if n in NEEDED_FIRST]
    late = [n for n in names if n not in NEEDED_FIRST]
    gathered = _exchange(True, [wire(n) for n in early], "weights_gather_first")
    wts = {n: _join_shards(g, axis_of[n]) for n, g in zip(early, gathered)}

    w_in_p = jnp.concatenate([wts["w_in"][:, :416], jnp.zeros((D_MODEL, 96), BF16), wts["w_in"][:, 416:]], axis=1)
    w_uq_p = jnp.pad(wts["w_uq"].reshape(256, MLA_HEADS, 96), ((0, 0), (0, 0), (0, 32))).reshape(256, 1024)
    ukv = wts["w_ukv"].reshape(128, MLA_HEADS, 128)
    w_uk_p = jnp.pad(ukv[:, :, :64], ((0, 0), (0, 0), (0, 64))).reshape(128, 1024)
    w_uv = ukv[:, :, 64:].reshape(128, 512)
    bg = wts["b_gate"]

    inv_freq = ROPE_THETA ** (-jnp.arange(0, MLA_ROPE, 2, dtype=F32) / MLA_ROPE)
    ang = positions.reshape(s_len).astype(F32)[:, None] * inv_freq
    cos16, sin16 = jnp.cos(ang), jnp.sin(ang)
    cosf = jnp.concatenate([jnp.ones((s_len, 64), F32), cos16, cos16, jnp.ones((s_len, 32), F32)], axis=1)
    sinf = jnp.concatenate([jnp.zeros((s_len, 64), F32), sin16, sin16, jnp.zeros((s_len, 32), F32)], axis=1)

    h1, lat, sb, gates = _in_proj(x2d, g_mix, w_in_p)
    qa, ka, va, q_lat, kv_lat = _mla_prep(lat, g_q_lat, g_kv_lat, w_uq_p, w_uk_p, w_uv, cosf, sinf)
    oa, lse, gathered = _mla_fwd(qa, ka, va, _Exchange(True, [wire(n) for n in late]))
    wts.update({n: _join_shards(g, axis_of[n]) for n, g in zip(late, gathered)})
    ob, sb_r = _sb_fwd(sb)
    x1 = _merge_fwd(x2d, oa, ob, gates, bg, wts["w_a_proj"], wts["w_b_proj"], wts["w_o"])
    mn, xkv = _mem_kv(mem2d, g_mem, wts["w_xkv"])
    x2 = _xattn_fwd(x1, g_x, wts["w_xq"], xkv, wts["w_xo"])
    x3 = _ffn_fwd(x2, g_ffn, wts["w_gate"], wts["w_up"], wts["w_down"])
    g_final2d = g_final.reshape(1, D_MODEL)
    sse, dx3, dx3b, dg_final = _loss_head(x3, g_final2d, target)
    loss = lax.psum(sse[0, 0] * (0.5 / D_MODEL), ("x", "y", "c"))

    dx2, hf, dgt, dup, act, dg_ffn = _ffn_bwd(x2, dx3, g_ffn, wts["w_gate"], wts["w_up"], wts["w_down"])
    dx1, dw_xq, dw_xo, dxkv, dg_x = _xattn_bwd(x1, dx2, g_x, wts["w_xq"], xkv, wts["w_xo"])
    dw_xkv, dg_mem = _mem_bwd(mem2d, g_mem, wts["w_xkv"], mn, dxkv)
    doa, dob, dgates, dpa, dpb, merged, dx1b, dbg = _merge_bwd(
        dx1, oa, ob, gates, bg, wts["w_a_proj"], wts["w_b_proj"], wts["w_o"])
    dsbq, dsbk, dsbv = _sb_bwd(sb, dob, sb_r)
    full_grads = {
        "w_a_proj": _tn_matmul(oa, dpa, "dw_a"),
        "w_b_proj": _tn_matmul(ob, dpb, "dw_b"),
        "w_o": _tn_matmul(merged, dx1b, "dw_o"),
        "w_xq": dw_xq,
        "w_xkv": dw_xkv,
        "w_xo": dw_xo,
        "w_gate": _tn_matmul(hf, dgt, "dw_gate", tn=FF_TILE),
        "w_up": _tn_matmul(hf, dup, "dw_up", tn=FF_TILE),
        "w_down": _tn_matmul(act, dx3b, "dw_down", tka=FF_TILE),
    }
    dqa, dka, dva, got = _mla_bwd(
        qa, ka, va, oa, doa, lse, _Exchange(False, [_split_pieces(full_grads[n], axis_of[n]) for n in late]))
    landed = dict(zip(late, got))
    dlat, dqb, dkb, dvb, dg_q, dg_kv = _mla_prep_bwd(
        lat, g_q_lat, g_kv_lat, w_uq_p, w_uk_p, w_uv, cosf, sinf, dqa, dka, dva)
    grad_x, dproj, dg_mix = _in_proj_bwd(x2d, g_mix, w_in_p, dx1, dlat, dsbq, dsbk, dsbv, dgates)
    dw_in_p = _tn_matmul(h1, dproj, "dw_in")
    dw_uq_p = _tn_matmul(q_lat, dqb, "dw_uq")
    dw_uk_p = _tn_matmul(kv_lat, dkb, "dw_uk")
    dw_uv = _tn_matmul(kv_lat, dvb, "dw_uv")
    full_grads.update({
        "w_in": jnp.concatenate([dw_in_p[:, :416], dw_in_p[:, 512:]], axis=1),
        "b_gate": dbg,
        "w_uq": dw_uq_p.reshape(256, MLA_HEADS, 128)[:, :, :96].reshape(256, 768),
        "w_ukv": jnp.concatenate([dw_uk_p.reshape(128, MLA_HEADS, 128)[:, :, :64],
                                  dw_uv.reshape(128, MLA_HEADS, 64)], axis=2).reshape(128, 1024),
    })
    rep_grads = {"g_mix": dg_mix, "g_q_lat": dg_q, "g_kv_lat": dg_kv, "g_x": dg_x, "g_mem": dg_mem,
                 "g_ffn": dg_ffn, "g_final": dg_final}
    rep_cat = lambda prefix, src: jnp.concatenate(
        [src[prefix + n].reshape(-1) for n, _ in REPLICATED]).reshape(-1, LANES)
    rep_src = jnp.broadcast_to(rep_cat("", rep_grads), (N_DEV,) + rep_cat("", rep_grads).shape)
    got = _exchange(False, [_split_pieces(full_grads[n], axis_of[n]) for n in early] + [rep_src], "grads_last")
    landed.update(zip(early, got[:-1]))
    rep_landed = got[-1]

    res = {}
    for name, _, _ in SHARDED:
        outs = _adamw(landed[name], shard2d(name), shard2d(name, "m_"), shard2d(name, "v_"), "adamw_" + name)
        res[name] = [o.reshape(given[name].shape) for o in outs]
    rep_outs = _adamw(rep_landed, rep_cat("", given), rep_cat("m_", given), rep_cat("v_", given), "adamw_gains")
    off = 0
    for name, n in REPLICATED:
        res[name] = [o.reshape(-1)[off:off + n].reshape(given[name].shape) for o in rep_outs]
        off += n
    result = [loss, grad_x.reshape(x.shape)]
    for k in range(4):
        result.extend(res[name][k] for name in WEIGHT_ORDER)
    return tuple(result)
```

```python
import functools
import math

import jax
import jax.numpy as jnp
from jax import lax
from jax.experimental import pallas as pl
from jax.experimental.pallas import tpu as pltpu

F32 = jnp.float32
BF16 = jnp.bfloat16

D_MODEL = 1024
MLA_HEADS = 8
MLA_Q_RANK = 256
MLA_KV_RANK = 128
MLA_NOPE = 64
MLA_ROPE = 32
MLA_V = 64
ROPE_THETA = 10000.0
SB_WIDTH = 512
X_HEADS = 4
X_HEAD_DIM = 128
D_FF = 2816
EPS = 1e-6
D_IN = 4000
D_IN_PAD = 4096
K_R_OFF = 384
LANES = 128
HEAD_PAD = 128
MLA_SCALE = 1.0 / math.sqrt(MLA_NOPE + MLA_ROPE)
SB_SCALE = 0.125
LOG2E = math.log2(math.e)
LN2 = math.log(2.0)
MLA_Q_FOLD = MLA_SCALE * LOG2E
SB_Q_FOLD = SB_SCALE * LOG2E
SB_CUT = -160.0
X_SCALE = 1.0 / math.sqrt(X_HEAD_DIM)
NEG_BIG = -1e30

ADAM_LR = 0.001
ADAM_B1 = 0.9
ADAM_B2 = 0.999
ADAM_EPS = 1e-08
ADAM_WD = 0.01
ADAM_STEP = 10

N_DEV = 8
MIB = 1024 * 1024
ADAM_BLOCK_BYTES = 4 * MIB

SHARDED = (
    ("w_in", (D_MODEL, D_IN), 1),
    ("b_gate", (2, D_MODEL), 1),
    ("w_uq", (MLA_Q_RANK, 768), 1),
    ("w_ukv", (MLA_KV_RANK, 1024), 1),
    ("w_a_proj", (512, D_MODEL), 1),
    ("w_b_proj", (512, D_MODEL), 1),
    ("w_o", (D_MODEL, D_MODEL), 0),
    ("w_xq", (D_MODEL, 512), 0),
    ("w_xkv", (D_MODEL, 1024), 0),
    ("w_xo", (512, D_MODEL), 1),
    ("w_gate", (D_MODEL, D_FF), 1),
    ("w_up", (D_MODEL, D_FF), 1),
    ("w_down", (D_FF, D_MODEL), 0),
)
NEEDED_FIRST = ("w_in", "b_gate", "w_uq", "w_ukv")
REPLICATED = (
    ("g_mix", 1024), ("g_q_lat", 256), ("g_kv_lat", 128), ("g_x", 1024),
    ("g_mem", 1024), ("g_ffn", 1024), ("g_final", 1024),
)
WEIGHT_ORDER = ("g_mix", "w_in", "b_gate", "g_q_lat", "w_uq", "g_kv_lat", "w_ukv", "w_a_proj",
                "w_b_proj", "w_o", "g_x", "g_mem", "w_xq", "w_xkv", "w_xo", "g_ffn", "w_gate",
                "w_up", "w_down", "g_final")


def _round_up(n, m):
    return -(-n // m) * m


def _cparams(vmem_mib=None, **kw):
    if vmem_mib is not None:
        kw["vmem_limit_bytes"] = vmem_mib * MIB
    return pltpu.CompilerParams(**kw)


def _dot(a, b):
    return jnp.dot(a, b, preferred_element_type=F32)


def _dot_nt(a, b):
    return lax.dot_general(a, b, (((1,), (1,)), ((), ())), preferred_element_type=F32)


def _dot_tn(a, b):
    return lax.dot_general(a, b, (((0,), (0,)), ((), ())), preferred_element_type=F32)


def _rms(x, g):
    r = lax.rsqrt(jnp.mean(x * x, axis=-1, keepdims=True) + EPS)
    xh = x * r
    return xh * g, xh, r


def _rms_bwd(dy, xh, r, g):
    u = dy * g
    dx = r * (u - xh * jnp.mean(u * xh, axis=-1, keepdims=True))
    return dx, dy * xh


def _sigmoid(z):
    return 1.0 / (1.0 + jnp.exp(-z))


def _acc_rows(ref, val, first):
    s = jnp.sum(val, axis=0, keepdims=True)

    @pl.when(first)
    def _():
        ref[...] = s

    @pl.when(jnp.logical_not(first))
    def _():
        ref[...] += s


def _acc(ref, val, first):
    @pl.when(first)
    def _():
        ref[...] = val

    @pl.when(jnp.logical_not(first))
    def _():
        ref[...] += val


def _peer(k):
    x, y, c = lax.axis_index("x"), lax.axis_index("y"), lax.axis_index("c")
    px = 1 - x if (k >> 2) & 1 else x
    py = 1 - y if (k >> 1) & 1 else y
    pc = 1 - c if k & 1 else c
    return (px, py, pc), 4 * px + 2 * py + pc


N_PEERS = N_DEV - 1
OTHER_CHIPS = (2, 4, 6)


def _land_shape(gather, src):
    return (N_DEV,) + src.shape if gather else src.shape


class _Exchange:
    def __init__(self, gather, srcs):
        self.gather, self.n, self.srcs = gather, len(srcs), list(srcs)
        self.out_shape = [jax.ShapeDtypeStruct(_land_shape(gather, s), s.dtype) for s in srcs]
        self.specs = [pl.BlockSpec(memory_space=pl.ANY)] * self.n
        self.scratch = [pltpu.SemaphoreType.DMA((self.n * N_PEERS,)), pltpu.SemaphoreType.DMA((self.n * N_PEERS,)),
                        pltpu.SemaphoreType.DMA((self.n,))]

    def bind(self, src, land, sems):
        self.src, self.land = src, land
        self.send_sems, self.recv_sems, self.local_sems = sems

    def _copy(self, a, k, source, to, target=1):
        return pltpu.make_async_remote_copy(
            src_ref=source, dst_ref=to,
            send_sem=self.send_sems.at[a * N_PEERS + k - 1], recv_sem=self.recv_sems.at[a * N_PEERS + k - 1],
            device_id=_peer(target)[0], device_id_type=pl.DeviceIdType.MESH)

    def _row(self, a, k):
        return self.land[a].at[_peer(k)[1]]

    def _mine(self, a):
        me = _peer(0)[1]
        return pltpu.make_async_copy(self.src[a] if self.gather else self.src[a].at[me], self.land[a].at[me],
                                     self.local_sems.at[a])

    def issue(self):
        me = _peer(0)[1]
        for a in range(self.n):
            self._mine(a).start()
            for k in ((1,) + OTHER_CHIPS if self.gather else range(1, N_DEV)):
                source = self.src[a] if self.gather else self.src[a].at[_peer(k)[1]]
                self._copy(a, k, source, self.land[a].at[me], target=k).start()

    def finish(self):
        me = _peer(0)[1]
        part = lambda a: self.src[a] if self.gather else self.src[a].at[me]
        if self.gather:
            for a in range(self.n):
                for k in OTHER_CHIPS:
                    self._copy(a, k, part(a), self._row(a, k)).wait_recv()
                    self._copy(a, k + 1, self._row(a, k), self._row(a, k), target=1).start()
        for a in range(self.n):
            for k in ((1, 3, 5, 7) if self.gather else range(1, N_DEV)):
                self._copy(a, k, part(a), self._row(a, k)).wait_recv()
        for a in range(self.n):
            for k in range(1, N_DEV):
                self._copy(a, k, part(a), self.land[a].at[me]).wait_send()
            self._mine(a).wait()


def _exchange(gather, srcs, name):
    ex = _Exchange(gather, srcs)

    def body(*refs):
        ex.bind(refs[:ex.n], refs[ex.n:2 * ex.n], refs[2 * ex.n:])
        ex.issue()
        ex.finish()

    return pl.pallas_call(body, name=name, out_shape=ex.out_shape, in_specs=ex.specs, out_specs=ex.specs,
                          scratch_shapes=ex.scratch)(*ex.srcs)


def _tn_matmul(a, b, name, tka=512, tn=1024, ts=2048):
    s_len, ka = a.shape
    n = b.shape[1]
    tka, tn, ts = min(tka, ka), min(tn, n), min(ts, s_len)
    assert ka % tka == 0 and n % tn == 0 and s_len % ts == 0

    def body(a_ref, b_ref, o_ref):
        _acc(o_ref, _dot_tn(a_ref[...], b_ref[...]), pl.program_id(2) == 0)

    return pl.pallas_call(
        body, name=name, grid=(ka // tka, n // tn, s_len // ts),
        in_specs=[pl.BlockSpec((ts, tka), lambda i, j, s: (s, i)),
                  pl.BlockSpec((ts, tn), lambda i, j, s: (s, j))],
        out_specs=pl.BlockSpec((tka, tn), lambda i, j, s: (i, j)),
        out_shape=jax.ShapeDtypeStruct((ka, n), F32),
        compiler_params=_cparams(dimension_semantics=("parallel", "parallel", "arbitrary")),
    )(a, b)


def _row_block(s_len):
    return min(s_len, 512)


def _in_proj(x, g, w):
    s_len = x.shape[0]
    tm = _row_block(s_len)

    def body(x_ref, g_ref, w_ref, h_ref, lat_ref, sb_ref, gate_ref):
        h, _, _ = _rms(x_ref[...], g_ref[...])
        hb = h.astype(BF16)
        h_ref[...] = hb
        p = _dot(hb, w_ref[:, 0:1024])
        lat_ref[...] = p[:, 0:512]
        sb_ref[:, 0:512] = (p[:, 512:1024] * SB_Q_FOLD).astype(BF16)
        sb_ref[:, 512:1536] = _dot(hb, w_ref[:, 1024:2048]).astype(BF16)
        gate_ref[:, 0:1024] = _dot(hb, w_ref[:, 2048:3072])
        gate_ref[:, 1024:2048] = _dot(hb, w_ref[:, 3072:4096])

    rowb = lambda n: pl.BlockSpec((tm, n), lambda i: (i, 0))
    return pl.pallas_call(
        body, name="in_proj", grid=(s_len // tm,),
        in_specs=[rowb(D_MODEL), pl.BlockSpec((1, D_MODEL), lambda i: (0, 0)),
                  pl.BlockSpec((D_MODEL, D_IN_PAD), lambda i: (0, 0))],
        out_specs=[rowb(D_MODEL), rowb(512), rowb(3 * SB_WIDTH), rowb(2 * D_MODEL)],
        out_shape=[jax.ShapeDtypeStruct((s_len, D_MODEL), BF16),
                   jax.ShapeDtypeStruct((s_len, 512), F32),
                   jax.ShapeDtypeStruct((s_len, 3 * SB_WIDTH), BF16),
                   jax.ShapeDtypeStruct((s_len, 2 * D_MODEL), F32)],
        compiler_params=_cparams(48, dimension_semantics=("parallel",)),
    )(x, g, w)


def _rope_rot(blk, lane):
    return jnp.where(lane < 80, -pltpu.roll(blk, 112, 1), pltpu.roll(blk, 16, 1))


def _rope_rot_t(blk, lane):
    return jnp.where(lane < 80, pltpu.roll(blk, 112, 1), -pltpu.roll(blk, 16, 1))


def _mla_prep(lat, g_q, g_kv, w_uq, w_uk, w_uv, cosf, sinf):
    s_len = lat.shape[0]
    tm = _row_block(s_len)

    def body(lat_ref, gq_ref, gkv_ref, wuq_ref, wuk_ref, wuv_ref, cos_ref, sin_ref,
             q_ref, k_ref, v_ref, ql_ref, kvl_ref):
        lane = lax.broadcasted_iota(jnp.int32, (tm, LANES), 1)
        cosv, sinv = cos_ref[...], sin_ref[...]
        ql, _, _ = _rms(lat_ref[:, 0:256], gq_ref[...])
        kvl, _, _ = _rms(lat_ref[:, 256:384], gkv_ref[...])
        qlb, kvlb = ql.astype(BF16), kvl.astype(BF16)
        ql_ref[...] = qlb
        kvl_ref[...] = kvlb
        q = _dot(qlb, wuq_ref[...])
        kn = _dot(kvlb, wuk_ref[...])
        v_ref[...] = _dot(kvlb, wuv_ref[...]).astype(BF16)
        kr = pltpu.roll(lat_ref[:, K_R_OFF:K_R_OFF + LANES], 64, 1)
        kr = kr * cosv + _rope_rot(kr, lane) * sinv
        for h in range(MLA_HEADS):
            sl = slice(h * HEAD_PAD, (h + 1) * HEAD_PAD)
            blk = q[:, sl]
            q_ref[:, sl] = ((blk * cosv + _rope_rot(blk, lane) * sinv) * MLA_Q_FOLD).astype(BF16)
            k_ref[:, sl] = (kn[:, sl] + kr).astype(BF16)

    full = lambda shape: pl.BlockSpec(shape, lambda i: (0, 0))
    rowb = lambda n: pl.BlockSpec((tm, n), lambda i: (i, 0))
    return pl.pallas_call(
        body, name="mla_prep", grid=(s_len // tm,),
        in_specs=[rowb(512), full((1, 256)), full((1, 128)), full((256, 1024)), full((128, 1024)),
                  full((128, 512)), rowb(128), rowb(128)],
        out_specs=[rowb(1024), rowb(1024), rowb(512), rowb(256), rowb(128)],
        out_shape=[jax.ShapeDtypeStruct((s_len, 1024), BF16), jax.ShapeDtypeStruct((s_len, 1024), BF16),
                   jax.ShapeDtypeStruct((s_len, 512), BF16), jax.ShapeDtypeStruct((s_len, 256), BF16),
                   jax.ShapeDtypeStruct((s_len, 128), BF16)],
        compiler_params=_cparams(dimension_semantics=("parallel",)),
    )(lat, g_q, g_kv, w_uq, w_uk, w_uv, cosf, sinf)


ATTN_TQ = 1024
ATTN_TH = 512
ATTN_TK = 256
SB_TQ = 512
SB_TH = 256
MLA_TK = 512


def _attn_blocks(s_len, tk=ATTN_TK, tq=ATTN_TQ, th=ATTN_TH):
    tq, th, tk = min(s_len, tq), min(s_len, th), min(s_len, tk)
    return tq, th, tk, tq // tk


def _chains(tq, th):
    return [(hh, r0) for hh in range(2) for r0 in range(0, tq, th)]


def _diag_mask(th, tk, r0, sub, strict):
    lo, hi = sub * tk, (sub + 1) * tk - 1
    last, first = r0 + th - 1, r0
    if (lo >= last) if strict else (lo > last):
        return "none"
    if (hi < first) if strict else (hi <= first):
        return "all"
    row = lax.broadcasted_iota(jnp.int32, (th, tk), 0) + r0
    col = lax.broadcasted_iota(jnp.int32, (th, tk), 1) + lo
    return col < row if strict else col <= row


def _mla_fwd(q, k, v, ride):
    s_len = q.shape[0]
    tq, th, tk, nsub = _attn_blocks(s_len, MLA_TK)
    chains = _chains(tq, th)
    nh = tq // th
    nq = s_len // tq

    def body(q_ref, k_ref, v_ref, *rest):
        o_ref, lse_ref = rest[ride.n:ride.n + 2]
        ride.bind(rest[:ride.n], rest[ride.n + 2:2 * ride.n + 2], rest[2 * ride.n + 2:])
        pl.when((pl.program_id(0) == 0) & (pl.program_id(1) == 0))(ride.issue)
        i = pl.program_id(1)
        lane = lax.broadcasted_iota(jnp.int32, (th, LANES), 1)
        hsl = [slice(hh * HEAD_PAD, (hh + 1) * HEAD_PAD) for hh in range(2)]

        def step(kb, carry, sub):
            rows = pl.ds(pl.multiple_of(kb * tk, tk), tk)
            vblk = v_ref[rows, :]
            masks = ["all" if sub is None else _diag_mask(th, tk, r0, sub, strict=False) for _, r0 in chains]
            live = [n for n, m in enumerate(masks) if not (isinstance(m, str) and m == "none")]
            s = {n: _dot_nt(q_ref[chains[n][1]:chains[n][1] + th, hsl[chains[n][0]]], k_ref[rows, hsl[chains[n][0]]])
                 for n in live}
            new = list(carry)
            pb, alpha = {}, {}
            for n in live:
                m, l, _ = carry[n]
                sn = s[n]
                if not isinstance(masks[n], str):
                    sn = jnp.where(masks[n], sn, NEG_BIG)
                m_new = jnp.maximum(m, jnp.max(sn, axis=-1, keepdims=True))
                alpha[n] = jnp.exp2(m - m_new)
                p = jnp.exp2(sn - m_new)
                pb[n] = p.astype(BF16)
                new[n] = (m_new, alpha[n] * l + jnp.sum(p, axis=-1, keepdims=True), None)
            pv = {n: _dot(pb[n], vblk) for n in live}
            for n in live:
                new[n] = (new[n][0], new[n][1], alpha[n] * carry[n][2] + pv[n])
            return tuple(new)

        init = (jnp.full((th, 1), NEG_BIG, F32), jnp.zeros((th, 1), F32), jnp.zeros((th, LANES), F32))
        carry = lax.fori_loop(0, i * nsub, lambda kb, cy: step(kb, cy, None), (init,) * len(chains))
        for sub in range(nsub):
            carry = step(i * nsub + sub, carry, sub)
        for c in range(nh):
            (m0, l0, a0), (m1, l1, a1) = carry[c], carry[nh + c]
            rs = slice(c * th, (c + 1) * th)
            o_ref[rs, :] = jnp.where(lane < 64, a0 / l0, a1 / l1).astype(BF16)
            lse_ref[rs, :] = jnp.where(lane < 64, m0 + jnp.log2(l0), m1 + jnp.log2(l1))
        pl.when((pl.program_id(0) == 3) & (i == nq - 1))(ride.finish)

    outs = pl.pallas_call(
        body, name="mla_fwd", grid=(4, nq),
        in_specs=[pl.BlockSpec((tq, 2 * HEAD_PAD), lambda p, i: (i, p)),
                  pl.BlockSpec((s_len, 2 * HEAD_PAD), lambda p, i: (0, p)),
                  pl.BlockSpec((s_len, LANES), lambda p, i: (0, p))] + ride.specs,
        out_specs=[pl.BlockSpec((tq, LANES), lambda p, i: (i, p)),
                   pl.BlockSpec((None, tq, LANES), lambda p, i: (p, i, 0))] + ride.specs,
        out_shape=[jax.ShapeDtypeStruct((s_len, 512), BF16),
                   jax.ShapeDtypeStruct((4, s_len, LANES), F32)] + ride.out_shape,
        scratch_shapes=ride.scratch,
        compiler_params=_cparams(40, dimension_semantics=("arbitrary", "arbitrary")),
    )(q, k, v, *ride.srcs)
    return outs[0], outs[1], outs[2:]


def _mla_bwd(q, k, v, o, do, lse, ride):
    s_len = q.shape[0]
    tq, th, tk, nsub = _attn_blocks(s_len, MLA_TK)
    chains = _chains(tq, th)
    nq = s_len // tq

    def body(q_ref, k_ref, v_ref, o_ref, do_ref, lse_ref, *rest):
        dq_ref, dk_ref, dv_ref = rest[ride.n:ride.n + 3]
        ride.bind(rest[:ride.n], rest[ride.n + 3:2 * ride.n + 3], rest[2 * ride.n + 3:])
        pl.when((pl.program_id(0) == 0) & (pl.program_id(1) == 0))(ride.issue)
        i = pl.program_id(1)
        lane = lax.broadcasted_iota(jnp.int32, (th, LANES), 1)

        @pl.when(i == 0)
        def _():
            dk_ref[...] = jnp.zeros_like(dk_ref)
            dv_ref[...] = jnp.zeros_like(dv_ref)

        hsl = [slice(hh * HEAD_PAD, (hh + 1) * HEAD_PAD) for hh in range(2)]
        qs, dos, deltas, lses = [], [], [], []
        for hh, r0 in chains:
            rs = slice(r0, r0 + th)
            qs.append(q_ref[rs, hsl[hh]])
            doh = jnp.where((lane // 64) == hh, do_ref[rs, :], jnp.zeros((), BF16))
            dos.append(doh)
            deltas.append(jnp.sum(doh.astype(F32) * o_ref[rs, :].astype(F32), axis=-1, keepdims=True))
            lses.append(lse_ref[rs, 64 * hh:64 * hh + 1])

        def step(kb, dqs, sub):
            rows = pl.ds(pl.multiple_of(kb * tk, tk), tk)
            vblk = v_ref[rows, :]
            new, p_all, do_all = [], [], []
            ds_h, q_h = [[], []], [[], []]
            for c, (hh, r0) in enumerate(chains):
                mask = "all" if sub is None else _diag_mask(th, tk, r0, sub, strict=False)
                if isinstance(mask, str) and mask == "none":
                    new.append(dqs[c])
                    continue
                kblk = k_ref[rows, hsl[hh]]
                s = _dot_nt(qs[c], kblk)
                if not isinstance(mask, str):
                    s = jnp.where(mask, s, NEG_BIG)
                p = jnp.exp2(s - lses[c])
                dp = _dot_nt(dos[c], vblk)
                ds = (p * (dp - deltas[c]) * MLA_SCALE).astype(BF16)
                p_all.append(p.astype(BF16))
                do_all.append(dos[c])
                ds_h[hh].append(ds)
                q_h[hh].append(qs[c])
                new.append(dqs[c] + _dot(ds, kblk))
            dv_ref[rows, :] += _dot_tn(jnp.concatenate(p_all, axis=0), jnp.concatenate(do_all, axis=0))
            for hh in range(2):
                dk_ref[rows, hsl[hh]] += _dot_tn(jnp.concatenate(ds_h[hh], axis=0),
                                                 jnp.concatenate(q_h[hh], axis=0))
            return tuple(new)

        zero = jnp.zeros((th, LANES), F32)
        dqs = lax.fori_loop(0, i * nsub, lambda kb, cy: step(kb, cy, None), (zero,) * len(chains))
        for sub in range(nsub):
            dqs = step(i * nsub + sub, dqs, sub)
        for c, (hh, r0) in enumerate(chains):
            dq_ref[r0:r0 + th, hsl[hh]] = dqs[c]
        pl.when((pl.program_id(0) == 3) & (i == nq - 1))(ride.finish)

    outs = pl.pallas_call(
        body, name="mla_bwd", grid=(4, nq),
        in_specs=[pl.BlockSpec((tq, 2 * HEAD_PAD), lambda p, i: (i, p)),
                  pl.BlockSpec((s_len, 2 * HEAD_PAD), lambda p, i: (0, p)),
                  pl.BlockSpec((s_len, LANES), lambda p, i: (0, p)),
                  pl.BlockSpec((tq, LANES), lambda p, i: (i, p)),
                  pl.BlockSpec((tq, LANES), lambda p, i: (i, p)),
                  pl.BlockSpec((None, tq, LANES), lambda p, i: (p, i, 0))] + ride.specs,
        out_specs=[pl.BlockSpec((tq, 2 * HEAD_PAD), lambda p, i: (i, p)),
                   pl.BlockSpec((s_len, 2 * HEAD_PAD), lambda p, i: (0, p)),
                   pl.BlockSpec((s_len, LANES), lambda p, i: (0, p))] + ride.specs,
        out_shape=[jax.ShapeDtypeStruct((s_len, 1024), F32), jax.ShapeDtypeStruct((s_len, 1024), F32),
                   jax.ShapeDtypeStruct((s_len, 512), F32)] + ride.out_shape,
        scratch_shapes=ride.scratch,
        compiler_params=_cparams(56, dimension_semantics=("arbitrary", "arbitrary")),
    )(q, k, v, o, do, lse, *ride.srcs)
    return outs[0], outs[1], outs[2], outs[3:]


def _log_sigmoids(z2):
    sp = jnp.log2(1.0 + jnp.exp2(-jnp.abs(z2)))
    lb = jnp.minimum(z2, 0.0) - sp
    return lb, lb - z2


def _split_dot(x, w, parts, nt=False):
    dot = _dot_nt if nt else _dot
    out = None
    for _ in range(parts):
        xb = x.astype(BF16)
        t = dot(xb, w)
        out = t if out is None else out + t
        x = x - xb.astype(F32)
    return out


def _sb_fwd(sb):
    s_len = sb.shape[0]
    tq, th, tk, nsub = _attn_blocks(s_len, tq=SB_TQ, th=SB_TH)
    chains = _chains(tq, th)
    nh = tq // th
    assert s_len // tk <= 64

    def body(q_ref, k_ref, v_ref, o_ref, r_ref):
        i = pl.program_id(1)
        lane = lax.broadcasted_iota(jnp.int32, (th, LANES), 1)
        upper = (lax.broadcasted_iota(jnp.int32, (tk, tk), 0)
                 > lax.broadcasted_iota(jnp.int32, (tk, tk), 1)).astype(BF16)
        qs = [jnp.where((lane // 64) == hh, q_ref[r0:r0 + th, :], jnp.zeros((), BF16)) for hh, r0 in chains]

        def step(kb, carry, sub):
            rows = pl.ds(pl.multiple_of(kb * tk, tk), tk)
            kblk, vblk = k_ref[rows, :], v_ref[rows, :]
            masks = ["all" if sub is None else _diag_mask(th, tk, r0, sub, strict=True) for _, r0 in chains]
            live = [n for n, m in enumerate(masks) if not (isinstance(m, str) and m == "none")]
            masked = {n: not isinstance(masks[n], str) for n in live}
            z = {n: _dot_nt(qs[n], kblk) for n in live}
            lb, lom = {}, {}
            for n in live:
                lb[n], lom[n] = _log_sigmoids(z[n])
                if masked[n]:
                    lom[n] = jnp.where(masks[n], lom[n], 0.0)
            suf = {n: _split_dot(lom[n], upper, 2) for n in live}
            a = {}
            for n in live:
                a[n] = jnp.exp2(lb[n] + suf[n] + carry[n][0])
                if masked[n]:
                    a[n] = jnp.where(masks[n], a[n], 0.0)
            pv = {n: _dot(a[n].astype(BF16), vblk) for n in live}
            new = list(carry)
            for n in live:
                c, acc, r = carry[n]
                rs = suf[n][:, 0:1] + lom[n][:, 0:1]
                new[n] = (c + rs, acc + pv[n], jnp.where(lane == 64 * chains[n][0] + kb, rs, r))
            return tuple(new)

        init = (jnp.zeros((th, 1), F32), jnp.zeros((th, LANES), F32), jnp.zeros((th, LANES), F32))
        carry = (init,) * len(chains)
        for sub in reversed(range(nsub)):
            carry = step(i * nsub + sub, carry, sub)

        def spent(cy):
            top = functools.reduce(jnp.maximum, [jnp.max(c) for c, _, _ in cy])
            return (top < SB_CUT).astype(jnp.int32)

        def walk(state):
            t, _, cy = state
            cy = step(i * nsub - 1 - t, cy, None)
            return t + 1, spent(cy), cy

        _, _, carry = lax.while_loop(lambda st: (st[0] < i * nsub) & (st[1] == 0), walk,
                                     (jnp.int32(0), spent(carry), carry))
        for n in range(nh):
            rs = slice(n * th, (n + 1) * th)
            o_ref[rs, :] = jnp.where(lane < 64, carry[n][1], carry[nh + n][1]).astype(BF16)
            r_ref[rs, :] = jnp.where(lane < 64, carry[n][2], carry[nh + n][2])

    return pl.pallas_call(
        body, name="sb_fwd", grid=(4, s_len // tq),
        in_specs=[pl.BlockSpec((tq, LANES), lambda p, i: (i, p)),
                  pl.BlockSpec((s_len, LANES), lambda p, i: (0, 4 + p)),
                  pl.BlockSpec((s_len, LANES), lambda p, i: (0, 8 + p))],
        out_specs=[pl.BlockSpec((tq, LANES), lambda p, i: (i, p)),
                   pl.BlockSpec((None, tq, LANES), lambda p, i: (p, i, 0))],
        out_shape=[jax.ShapeDtypeStruct((s_len, 512), BF16), jax.ShapeDtypeStruct((4, s_len, LANES), F32)],
        compiler_params=_cparams(40, dimension_semantics=("parallel", "arbitrary")),
    )(sb, sb, sb)


def _sb_bwd(sb, do, r):
    s_len = sb.shape[0]
    tq, th, tk, nsub = _attn_blocks(s_len)
    chains = _chains(tq, th)
    nh = tq // th

    def body(q_ref, k_ref, v_ref, do_ref, r_ref, dq_ref, dk_ref, dv_ref):
        i = pl.program_id(1)
        lane = lax.broadcasted_iota(jnp.int32, (th, LANES), 1)
        upper = (lax.broadcasted_iota(jnp.int32, (tk, tk), 0)
                 > lax.broadcasted_iota(jnp.int32, (tk, tk), 1)).astype(BF16)
        tri = (lax.broadcasted_iota(jnp.int32, (LANES, LANES), 0)
               > lax.broadcasted_iota(jnp.int32, (LANES, LANES), 1)).astype(BF16)

        @pl.when(i == 0)
        def _():
            dk_ref[...] = jnp.zeros_like(dk_ref)
            dv_ref[...] = jnp.zeros_like(dv_ref)

        qs, dos, rights = [], [], []
        for hh, r0 in chains:
            rs = slice(r0, r0 + th)
            hm = (lane // 64) == hh
            qs.append(jnp.where(hm, q_ref[rs, :], jnp.zeros((), BF16)))
            dos.append(jnp.where(hm, do_ref[rs, :], jnp.zeros((), BF16)))
            rights.append(_split_dot(jnp.where(hm, r_ref[rs, :], 0.0), tri, 3))

        def step(kb, carry, sub):
            rows = pl.ds(pl.multiple_of(kb * tk, tk), tk)
            kblk, vblk = k_ref[rows, :], v_ref[rows, :]
            new, a_all, do_all, dz_all, q_all = [], [], [], [], []
            for n, ((hh, r0), (pre, dq)) in enumerate(zip(chains, carry)):
                mask = "all" if sub is None else _diag_mask(th, tk, r0, sub, strict=True)
                if isinstance(mask, str) and mask == "none":
                    new.append((pre, dq))
                    continue
                c = jnp.sum(jnp.where(lane == 64 * hh + kb, rights[n], 0.0), axis=-1, keepdims=True)
                z = _dot_nt(qs[n], kblk)
                lb, lom = _log_sigmoids(z)
                if not isinstance(mask, str):
                    lom = jnp.where(mask, lom, 0.0)
                suf = _split_dot(lom, upper, 2)
                a = jnp.exp2(lb + suf + c)
                if not isinstance(mask, str):
                    a = jnp.where(mask, a, 0.0)
                g = a * _dot_nt(dos[n], vblk)
                left = _split_dot(g, upper, 1, nt=True) + pre
                sig = jnp.exp2(lb)
                dz = g * (1.0 - sig) - sig * left
                if not isinstance(mask, str):
                    dz = jnp.where(mask, dz, 0.0)
                dzb = dz.astype(BF16)
                a_all.append(a.astype(BF16))
                do_all.append(dos[n])
                dz_all.append(dzb)
                q_all.append(qs[n])
                new.append((left[:, tk - 1:tk] + g[:, tk - 1:tk], dq + _dot(dzb, kblk)))
            dv_ref[rows, :] += _dot_tn(jnp.concatenate(a_all, axis=0), jnp.concatenate(do_all, axis=0))
            dk_ref[rows, :] += _dot_tn(jnp.concatenate(dz_all, axis=0), jnp.concatenate(q_all, axis=0))
            return tuple(new)

        lane1 = lax.broadcasted_iota(jnp.int32, (1, LANES), 1)
        first = i * nsub
        for n, (hh, _) in enumerate(chains):
            top = jnp.max(rights[n], axis=0, keepdims=True)
            kb_of = lane1 - 64 * hh
            live = (kb_of >= 0) & (kb_of < i * nsub) & (top >= SB_CUT)
            first = jnp.minimum(first, jnp.min(jnp.where(live, kb_of, i * nsub)))

        init = (jnp.zeros((th, 1), F32), jnp.zeros((th, LANES), F32))
        carry = lax.fori_loop(first, i * nsub, lambda kb, cy: step(kb, cy, None), (init,) * len(chains))
        for sub in range(nsub):
            carry = step(i * nsub + sub, carry, sub)
        for n in range(nh):
            dq_ref[n * th:(n + 1) * th, :] = jnp.where(lane < 64, carry[n][1], carry[nh + n][1]) * SB_SCALE

    return pl.pallas_call(
        body, name="sb_bwd", grid=(4, s_len // tq),
        in_specs=[pl.BlockSpec((tq, LANES), lambda p, i: (i, p)),
                  pl.BlockSpec((s_len, LANES), lambda p, i: (0, 4 + p)),
                  pl.BlockSpec((s_len, LANES), lambda p, i: (0, 8 + p)),
                  pl.BlockSpec((tq, LANES), lambda p, i: (i, p)),
                  pl.BlockSpec((None, tq, LANES), lambda p, i: (p, i, 0))],
        out_specs=[pl.BlockSpec((tq, LANES), lambda p, i: (i, p)),
                   pl.BlockSpec((s_len, LANES), lambda p, i: (0, p)),
                   pl.BlockSpec((s_len, LANES), lambda p, i: (0, p))],
        out_shape=[jax.ShapeDtypeStruct((s_len, 512), F32)] * 3,
        compiler_params=_cparams(48, dimension_semantics=("arbitrary", "arbitrary")),
    )(sb, sb, sb, do, r)


def _merge_fwd(x, oa, ob, gates, bg, wa, wb, wo):
    s_len = x.shape[0]
    tm = _row_block(s_len)

    def body(x_ref, oa_ref, ob_ref, g_ref, bg_ref, wa_ref, wb_ref, wo_ref, y_ref):
        pa = _dot(oa_ref[...], wa_ref[...])
        pb = _dot(ob_ref[...], wb_ref[...])
        merged = (_sigmoid(g_ref[:, 0:D_MODEL] + bg_ref[0:1, :]) * pa
                  + _sigmoid(g_ref[:, D_MODEL:2 * D_MODEL] + bg_ref[1:2, :]) * pb)
        y_ref[...] = x_ref[...] + _dot(merged.astype(BF16), wo_ref[...])

    full = lambda shape: pl.BlockSpec(shape, lambda i: (0, 0))
    rowb = lambda n: pl.BlockSpec((tm, n), lambda i: (i, 0))
    return pl.pallas_call(
        body, name="merge_fwd", grid=(s_len // tm,),
        in_specs=[rowb(1024), rowb(512), rowb(512), rowb(2048), full((2, 1024)), full((512, 1024)),
                  full((512, 1024)), full((1024, 1024))],
        out_specs=rowb(1024),
        out_shape=jax.ShapeDtypeStruct((s_len, D_MODEL), F32),
        compiler_params=_cparams(48, dimension_semantics=("parallel",)),
    )(x, oa, ob, gates, bg, wa, wb, wo)


def _merge_bwd(dx1, oa, ob, gates, bg, wa, wb, wo):
    s_len = dx1.shape[0]
    tm = _row_block(s_len)

    def body(dx_ref, oa_ref, ob_ref, g_ref, bg_ref, wa_ref, wb_ref, wo_ref,
             doa_ref, dob_ref, dgate_ref, dpa_ref, dpb_ref, merged_ref, dxb_ref, dbg_ref):
        first = pl.program_id(0) == 0
        dxb = dx_ref[...].astype(BF16)
        dxb_ref[...] = dxb
        pa = _dot(oa_ref[...], wa_ref[...])
        pb = _dot(ob_ref[...], wb_ref[...])
        sa = _sigmoid(g_ref[:, 0:D_MODEL] + bg_ref[0:1, :])
        sbg = _sigmoid(g_ref[:, D_MODEL:2 * D_MODEL] + bg_ref[1:2, :])
        merged_ref[...] = (sa * pa + sbg * pb).astype(BF16)
        dm = _dot_nt(dxb, wo_ref[...])
        dpa = (dm * sa).astype(BF16)
        dpb = (dm * sbg).astype(BF16)
        dpa_ref[...] = dpa
        dpb_ref[...] = dpb
        dga = dm * pa * sa * (1.0 - sa)
        dgb = dm * pb * sbg * (1.0 - sbg)
        dgate_ref[:, 0:D_MODEL] = dga.astype(BF16)
        dgate_ref[:, D_MODEL:2 * D_MODEL] = dgb.astype(BF16)
        _acc_rows(dbg_ref.at[0:1, :], dga, first)
        _acc_rows(dbg_ref.at[1:2, :], dgb, first)
        doa_ref[...] = _dot_nt(dpa, wa_ref[...]).astype(BF16)
        dob_ref[...] = _dot_nt(dpb, wb_ref[...]).astype(BF16)

    full = lambda shape: pl.BlockSpec(shape, lambda i: (0, 0))
    rowb = lambda n: pl.BlockSpec((tm, n), lambda i: (i, 0))
    sds = lambda n, dt: jax.ShapeDtypeStruct((s_len, n), dt)
    return pl.pallas_call(
        body, name="merge_bwd", grid=(s_len // tm,),
        in_specs=[rowb(1024), rowb(512), rowb(512), rowb(2048), full((2, 1024)), full((512, 1024)),
                  full((512, 1024)), full((1024, 1024))],
        out_specs=[rowb(512), rowb(512), rowb(2048), rowb(1024), rowb(1024), rowb(1024), rowb(1024),
                   full((2, 1024))],
        out_shape=[sds(512, BF16), sds(512, BF16), sds(2048, BF16), sds(1024, BF16), sds(1024, BF16),
                   sds(1024, BF16), sds(1024, BF16), jax.ShapeDtypeStruct((2, 1024), F32)],
        compiler_params=_cparams(48, dimension_semantics=("arbitrary",)),
    )(dx1, oa, ob, gates, bg, wa, wb, wo)


def _mem_kv(mem, g, w):
    m_len = mem.shape[0]

    def body(mem_ref, g_ref, w_ref, mn_ref, kv_ref):
        mn, _, _ = _rms(mem_ref[...], g_ref[...])
        mnb = mn.astype(BF16)
        mn_ref[...] = mnb
        kv_ref[...] = _dot(mnb, w_ref[...]).astype(BF16)

    return pl.pallas_call(
        body, name="mem_kv",
        out_shape=[jax.ShapeDtypeStruct((m_len, D_MODEL), BF16), jax.ShapeDtypeStruct((m_len, 1024), BF16)],
    )(mem, g, w)


def _mem_bwd(mem, g, w, mn, dkv):
    def body(mem_ref, g_ref, w_ref, mn_ref, dkv_ref, dw_ref, dg_ref):
        dkvb = dkv_ref[...].astype(BF16)
        dw_ref[...] = _dot_tn(mn_ref[...], dkvb)
        dmn = _dot_nt(dkvb, w_ref[...])
        _, xh, _ = _rms(mem_ref[...], g_ref[...])
        dg_ref[...] = jnp.sum(dmn * xh, axis=0, keepdims=True)

    return pl.pallas_call(
        body, name="mem_bwd",
        out_shape=[jax.ShapeDtypeStruct((D_MODEL, 1024), F32), jax.ShapeDtypeStruct((1, D_MODEL), F32)],
    )(mem, g, w, mn, dkv)


def _xattn_heads(xqb, kv_ref, m_len):
    ps = []
    for h in range(X_HEADS):
        hs = slice(h * X_HEAD_DIM, (h + 1) * X_HEAD_DIM)
        s = _dot_nt(xqb[:, hs], kv_ref[:, hs]) * X_SCALE
        e = jnp.exp(s - jnp.max(s, axis=-1, keepdims=True))
        ps.append(e / jnp.sum(e, axis=-1, keepdims=True))
    return ps


def _xattn_fwd(x1, g, wxq, kv, wxo):
    s_len, m_len = x1.shape[0], kv.shape[0]
    tm = _row_block(s_len)

    def body(x_ref, g_ref, wq_ref, kv_ref, wo_ref, y_ref):
        hx, _, _ = _rms(x_ref[...], g_ref[...])
        xqb = _dot(hx.astype(BF16), wq_ref[...]).astype(BF16)
        ps = _xattn_heads(xqb, kv_ref, m_len)
        xo = jnp.concatenate(
            [_dot(ps[h].astype(BF16), kv_ref[:, 512 + h * X_HEAD_DIM:512 + (h + 1) * X_HEAD_DIM])
             for h in range(X_HEADS)], axis=-1)
        y_ref[...] = x_ref[...] + _dot(xo.astype(BF16), wo_ref[...])

    full = lambda shape: pl.BlockSpec(shape, lambda i: (0, 0))
    rowb = lambda n: pl.BlockSpec((tm, n), lambda i: (i, 0))
    return pl.pallas_call(
        body, name="xattn_fwd", grid=(s_len // tm,),
        in_specs=[rowb(1024), full((1, 1024)), full((1024, 512)), full((m_len, 1024)), full((512, 1024))],
        out_specs=rowb(1024),
        out_shape=jax.ShapeDtypeStruct((s_len, D_MODEL), F32),
        compiler_params=_cparams(48, dimension_semantics=("parallel",)),
    )(x1, g, wxq, kv, wxo)


def _xattn_bwd(x1, dx2, g, wxq, kv, wxo):
    s_len, m_len = x1.shape[0], kv.shape[0]
    tm = _row_block(s_len)

    def body(x_ref, dy_ref, g_ref, wq_ref, kv_ref, wo_ref, dx_ref, dwq_ref, dwo_ref, dkv_ref, dg_ref):
        first = pl.program_id(0) == 0
        gv = g_ref[...]
        hx, xh, r = _rms(x_ref[...], gv)
        hxb = hx.astype(BF16)
        xqb = _dot(hxb, wq_ref[...]).astype(BF16)
        ps = _xattn_heads(xqb, kv_ref, m_len)
        dy = dy_ref[...]
        dyb = dy.astype(BF16)
        dxo = _dot_nt(dyb, wo_ref[...])
        xos, dqs, dks, dvs = [], [], [], []
        for h in range(X_HEADS):
            hs = slice(h * X_HEAD_DIM, (h + 1) * X_HEAD_DIM)
            vs = slice(512 + h * X_HEAD_DIM, 512 + (h + 1) * X_HEAD_DIM)
            p = ps[h]
            pb = p.astype(BF16)
            dxoh = dxo[:, hs].astype(BF16)
            xos.append(_dot(pb, kv_ref[:, vs]))
            dp = _dot_nt(dxoh, kv_ref[:, vs])
            ds = (p * (dp - jnp.sum(dp * p, axis=-1, keepdims=True)) * X_SCALE).astype(BF16)
            dvs.append(_dot_tn(pb, dxoh))
            dks.append(_dot_tn(ds, xqb[:, hs]))
            dqs.append(_dot(ds, kv_ref[:, hs]))
        xob = jnp.concatenate(xos, axis=-1).astype(BF16)
        dxqb = jnp.concatenate(dqs, axis=-1).astype(BF16)
        _acc(dwo_ref, _dot_tn(xob, dyb), first)
        _acc(dwq_ref, _dot_tn(hxb, dxqb), first)
        _acc(dkv_ref, jnp.concatenate(dks + dvs, axis=-1), first)
        dhx = _dot_nt(dxqb, wq_ref[...])
        dx, dgr = _rms_bwd(dhx, xh, r, gv)
        dx_ref[...] = dy + dx
        _acc_rows(dg_ref, dgr, first)

    full = lambda shape: pl.BlockSpec(shape, lambda i: (0, 0))
    rowb = lambda n: pl.BlockSpec((tm, n), lambda i: (i, 0))
    return pl.pallas_call(
        body, name="xattn_bwd", grid=(s_len // tm,),
        in_specs=[rowb(1024), rowb(1024), full((1, 1024)), full((1024, 512)), full((m_len, 1024)),
                  full((512, 1024))],
        out_specs=[rowb(1024), full((1024, 512)), full((512, 1024)), full((m_len, 1024)), full((1, 1024))],
        out_shape=[jax.ShapeDtypeStruct((s_len, D_MODEL), F32), jax.ShapeDtypeStruct((1024, 512), F32),
                   jax.ShapeDtypeStruct((512, 1024), F32), jax.ShapeDtypeStruct((m_len, 1024), F32),
                   jax.ShapeDtypeStruct((1, D_MODEL), F32)],
        compiler_params=_cparams(48, dimension_semantics=("arbitrary",)),
    )(x1, dx2, g, wxq, kv, wxo)


FF_TILE = 1408


def _ffn_fwd(x2, g, wg, wu, wd):
    s_len = x2.shape[0]
    tm, tf = _row_block(s_len), FF_TILE

    def body(x_ref, g_ref, wg_ref, wu_ref, wd_ref, y_ref, h_scr):
        j = pl.program_id(1)

        @pl.when(j == 0)
        def _():
            hf, _, _ = _rms(x_ref[...], g_ref[...])
            h_scr[...] = hf.astype(BF16)
            y_ref[...] = x_ref[...]

        hb = h_scr[...]
        gt = _dot(hb, wg_ref[...])
        up = _dot(hb, wu_ref[...])
        act = gt * _sigmoid(gt) * up
        y_ref[...] += _dot(act.astype(BF16), wd_ref[...])

    return pl.pallas_call(
        body, name="ffn_fwd", grid=(s_len // tm, D_FF // tf),
        in_specs=[pl.BlockSpec((tm, D_MODEL), lambda i, j: (i, 0)),
                  pl.BlockSpec((1, D_MODEL), lambda i, j: (0, 0)),
                  pl.BlockSpec((D_MODEL, tf), lambda i, j: (0, j)),
                  pl.BlockSpec((D_MODEL, tf), lambda i, j: (0, j)),
                  pl.BlockSpec((tf, D_MODEL), lambda i, j: (j, 0))],
        out_specs=pl.BlockSpec((tm, D_MODEL), lambda i, j: (i, 0)),
        out_shape=jax.ShapeDtypeStruct((s_len, D_MODEL), F32),
        scratch_shapes=[pltpu.VMEM((tm, D_MODEL), BF16)],
        compiler_params=_cparams(48, dimension_semantics=("parallel", "arbitrary")),
    )(x2, g, wg, wu, wd)


def _ffn_bwd(x2, dx3, g, wg, wu, wd):
    s_len = x2.shape[0]
    tm, tf = min(s_len, 256), FF_TILE
    nf = D_FF // tf

    def act_body(x_ref, dy_ref, g_ref, wg_ref, wu_ref, wd_ref, h_ref, dgt_ref, dup_ref, act_ref, h_scr, dyb_scr):
        @pl.when(pl.program_id(1) == 0)
        def _():
            hf, _, _ = _rms(x_ref[...], g_ref[...])
            hb = hf.astype(BF16)
            h_scr[...] = hb
            h_ref[...] = hb
            dyb_scr[...] = dy_ref[...].astype(BF16)

        hb = h_scr[...]
        gt = _dot(hb, wg_ref[...])
        up = _dot(hb, wu_ref[...])
        sg = _sigmoid(gt)
        silu = gt * sg
        dact = _dot_nt(dyb_scr[...], wd_ref[...])
        dgt_ref[...] = (dact * up * (sg * (1.0 + gt * (1.0 - sg)))).astype(BF16)
        dup_ref[...] = (dact * silu).astype(BF16)
        act_ref[...] = (silu * up).astype(BF16)

    rowb = pl.BlockSpec((tm, D_MODEL), lambda i, j: (i, 0))
    ffb = pl.BlockSpec((tm, tf), lambda i, j: (i, j))
    hf, dgt, dup, act = pl.pallas_call(
        act_body, name="ffn_bwd_act", grid=(s_len // tm, nf),
        in_specs=[rowb, rowb, pl.BlockSpec((1, D_MODEL), lambda i, j: (0, 0)),
                  pl.BlockSpec((D_MODEL, tf), lambda i, j: (0, j)),
                  pl.BlockSpec((D_MODEL, tf), lambda i, j: (0, j)),
                  pl.BlockSpec((tf, D_MODEL), lambda i, j: (j, 0))],
        out_specs=[rowb, ffb, ffb, ffb],
        out_shape=[jax.ShapeDtypeStruct((s_len, D_MODEL), BF16), jax.ShapeDtypeStruct((s_len, D_FF), BF16),
                   jax.ShapeDtypeStruct((s_len, D_FF), BF16), jax.ShapeDtypeStruct((s_len, D_FF), BF16)],
        scratch_shapes=[pltpu.VMEM((tm, D_MODEL), BF16), pltpu.VMEM((tm, D_MODEL), BF16)],
        compiler_params=_cparams(56, dimension_semantics=("parallel", "arbitrary")),
    )(x2, dx3, g, wg, wu, wd)

    def in_body(x_ref, dy_ref, g_ref, wg_ref, wu_ref, dgt_ref, dup_ref, dx_ref, dg_ref):
        dh = _dot_nt(dgt_ref[...], wg_ref[...]) + _dot_nt(dup_ref[...], wu_ref[...])
        gv = g_ref[...]
        _, xh, r = _rms(x_ref[...], gv)
        dx, dgr = _rms_bwd(dh, xh, r, gv)
        dx_ref[...] = dy_ref[...] + dx
        _acc_rows(dg_ref, dgr, pl.program_id(0) == 0)

    row1 = lambda n: pl.BlockSpec((tm, n), lambda i: (i, 0))
    full = lambda shape: pl.BlockSpec(shape, lambda i: (0, 0))
    dx2, dg = pl.pallas_call(
        in_body, name="ffn_bwd_in", grid=(s_len // tm,),
        in_specs=[row1(D_MODEL), row1(D_MODEL), full((1, D_MODEL)), full((D_MODEL, D_FF)), full((D_MODEL, D_FF)),
                  row1(D_FF), row1(D_FF)],
        out_specs=[row1(D_MODEL), full((1, D_MODEL))],
        out_shape=[jax.ShapeDtypeStruct((s_len, D_MODEL), F32), jax.ShapeDtypeStruct((1, D_MODEL), F32)],
        compiler_params=_cparams(48, dimension_semantics=("arbitrary",)),
    )(x2, dx3, g, wg, wu, dgt, dup)
    return dx2, hf, dgt, dup, act, dg


def _loss_head(x3, g, target):
    s_len = x3.shape[0]
    tm = _row_block(s_len)

    def body(x_ref, g_ref, t_ref, sse_ref, dx_ref, dxb_ref, dg_ref):
        first = pl.program_id(0) == 0
        gv = g_ref[...]
        y, xh, r = _rms(x_ref[...], gv)
        err = y - t_ref[...]
        _acc(sse_ref, jnp.broadcast_to(jnp.sum(err * err), (8, LANES)), first)
        dx, dgr = _rms_bwd(err * (1.0 / D_MODEL), xh, r, gv)
        dx_ref[...] = dx
        dxb_ref[...] = dx.astype(BF16)
        _acc_rows(dg_ref, dgr, first)

    rowb = pl.BlockSpec((tm, D_MODEL), lambda i: (i, 0))
    return pl.pallas_call(
        body, name="loss_head", grid=(s_len // tm,),
        in_specs=[rowb, pl.BlockSpec((1, D_MODEL), lambda i: (0, 0)), rowb],
        out_specs=[pl.BlockSpec((8, LANES), lambda i: (0, 0)), rowb, rowb,
                   pl.BlockSpec((1, D_MODEL), lambda i: (0, 0))],
        out_shape=[jax.ShapeDtypeStruct((8, LANES), F32), jax.ShapeDtypeStruct((s_len, D_MODEL), F32),
                   jax.ShapeDtypeStruct((s_len, D_MODEL), BF16), jax.ShapeDtypeStruct((1, D_MODEL), F32)],
        compiler_params=_cparams(dimension_semantics=("arbitrary",)),
    )(x3, g, target)


def _mla_prep_bwd(lat, g_q, g_kv, w_uq, w_uk, w_uv, cosf, sinf, dq, dk, dv):
    s_len = lat.shape[0]
    tm = _row_block(s_len)

    def body(lat_ref, gq_ref, gkv_ref, wuq_ref, wuk_ref, wuv_ref, cos_ref, sin_ref, dq_ref, dk_ref, dv_ref,
             dlat_ref, dqb_ref, dkb_ref, dvb_ref, dgq_ref, dgkv_ref):
        first = pl.program_id(0) == 0
        lane = lax.broadcasted_iota(jnp.int32, (tm, LANES), 1)
        cosv, sinv = cos_ref[...], sin_ref[...]
        gq, gkv = gq_ref[...], gkv_ref[...]
        _, qxh, qr = _rms(lat_ref[:, 0:256], gq)
        _, kxh, kr_ = _rms(lat_ref[:, 256:384], gkv)
        dkr = jnp.zeros((tm, LANES), F32)
        for h in range(MLA_HEADS):
            sl = slice(h * HEAD_PAD, (h + 1) * HEAD_PAD)
            blk = dq_ref[:, sl]
            dqb_ref[:, sl] = (blk * cosv + _rope_rot_t(blk, lane) * sinv).astype(BF16)
            kblk = dk_ref[:, sl] * (1.0 / MLA_Q_FOLD)
            dkb_ref[:, sl] = kblk.astype(BF16)
            dkr = dkr + kblk
        dvb = dv_ref[...].astype(BF16)
        dvb_ref[...] = dvb
        dkr = jnp.where((lane >= 64) & (lane < 96), dkr, 0.0)
        dkr = dkr * cosv + _rope_rot_t(dkr, lane) * sinv
        dql = _dot_nt(dqb_ref[...], wuq_ref[...])
        dkvl = _dot_nt(dkb_ref[...], wuk_ref[...]) + _dot_nt(dvb, wuv_ref[...])
        dcq, dgqr = _rms_bwd(dql, qxh, qr, gq)
        dckv, dgkvr = _rms_bwd(dkvl, kxh, kr_, gkv)
        dlat_ref[:, 0:256] = dcq
        dlat_ref[:, 256:384] = dckv
        dlat_ref[:, K_R_OFF:K_R_OFF + LANES] = pltpu.roll(dkr, 64, 1)
        _acc_rows(dgq_ref, dgqr, first)
        _acc_rows(dgkv_ref, dgkvr, first)

    full = lambda shape: pl.BlockSpec(shape, lambda i: (0, 0))
    rowb = lambda n: pl.BlockSpec((tm, n), lambda i: (i, 0))
    sds = lambda n, dt: jax.ShapeDtypeStruct((s_len, n), dt)
    return pl.pallas_call(
        body, name="mla_prep_bwd", grid=(s_len // tm,),
        in_specs=[rowb(512), full((1, 256)), full((1, 128)), full((256, 1024)), full((128, 1024)),
                  full((128, 512)), rowb(128), rowb(128), rowb(1024), rowb(1024), rowb(512)],
        out_specs=[rowb(512), rowb(1024), rowb(1024), rowb(512), full((1, 256)), full((1, 128))],
        out_shape=[sds(512, F32), sds(1024, BF16), sds(1024, BF16), sds(512, BF16),
                   jax.ShapeDtypeStruct((1, 256), F32), jax.ShapeDtypeStruct((1, 128), F32)],
        compiler_params=_cparams(48, dimension_semantics=("arbitrary",)),
    )(lat, g_q, g_kv, w_uq, w_uk, w_uv, cosf, sinf, dq, dk, dv)


def _in_proj_bwd(x, g, w, dx1, dlat, dsbq, dsbk, dsbv, dgates):
    s_len = x.shape[0]
    tm = min(s_len, 256)

    def body(x_ref, g_ref, w_ref, dx1_ref, dlat_ref, dq_ref, dk_ref, dv_ref, dgate_ref,
             gx_ref, dproj_ref, dg_ref):
        dproj_ref[:, 0:512] = dlat_ref[...].astype(BF16)
        dproj_ref[:, 512:1024] = dq_ref[...].astype(BF16)
        dproj_ref[:, 1024:1536] = (dk_ref[...] * LN2).astype(BF16)
        dproj_ref[:, 1536:2048] = dv_ref[...].astype(BF16)
        dproj_ref[:, 2048:4096] = dgate_ref[...]
        dh = _dot_nt(dproj_ref[...], w_ref[...])
        gv = g_ref[...]
        _, xh, r = _rms(x_ref[...], gv)
        dx, dgr = _rms_bwd(dh, xh, r, gv)
        gx_ref[...] = dx1_ref[...] + dx
        _acc_rows(dg_ref, dgr, pl.program_id(0) == 0)

    rowb = lambda n: pl.BlockSpec((tm, n), lambda i: (i, 0))
    full = lambda shape: pl.BlockSpec(shape, lambda i: (0, 0))
    return pl.pallas_call(
        body, name="in_proj_bwd", grid=(s_len // tm,),
        in_specs=[rowb(D_MODEL), full((1, D_MODEL)), full((D_MODEL, D_IN_PAD)), rowb(D_MODEL),
                  rowb(512), rowb(512), rowb(512), rowb(512), rowb(2 * D_MODEL)],
        out_specs=[rowb(D_MODEL), rowb(D_IN_PAD), full((1, D_MODEL))],
        out_shape=[jax.ShapeDtypeStruct((s_len, D_MODEL), F32), jax.ShapeDtypeStruct((s_len, D_IN_PAD), BF16),
                   jax.ShapeDtypeStruct((1, D_MODEL), F32)],
        compiler_params=_cparams(48, dimension_semantics=("arbitrary",)),
    )(x, g, w, dx1, dlat, dsbq, dsbk, dsbv, dgates)


def _adamw(landed, w, m, v, name):
    r, c = w.shape
    lanes = _round_up(c, LANES)
    tb = r
    for cand in range(r, 0, -1):
        if r % cand == 0 and (cand % 8 == 0 or cand == r) and N_DEV * cand * lanes * 4 <= ADAM_BLOCK_BYTES:
            tb = cand
            break
    c1 = 1.0 - ADAM_B1 ** ADAM_STEP
    c2 = 1.0 - ADAM_B2 ** ADAM_STEP

    def body(l_ref, w_ref, m_ref, v_ref, g_ref, d_ref, nm_ref, nv_ref):
        g = l_ref[0]
        for k in range(1, N_DEV):
            g = g + l_ref[k]
        nm = ADAM_B1 * m_ref[...] + (1.0 - ADAM_B1) * g
        nv = ADAM_B2 * v_ref[...] + (1.0 - ADAM_B2) * (g * g)
        g_ref[...] = g
        nm_ref[...] = nm
        nv_ref[...] = nv
        d_ref[...] = -ADAM_LR * ((nm / c1) / (jnp.sqrt(nv / c2) + ADAM_EPS) + ADAM_WD * w_ref[...])

    blk = pl.BlockSpec((tb, c), lambda i: (i, 0))
    return pl.pallas_call(
        body, name=name, grid=(r // tb,),
        in_specs=[pl.BlockSpec((N_DEV, tb, c), lambda i: (0, i, 0)), blk, blk, blk],
        out_specs=[blk, blk, blk, blk],
        out_shape=[jax.ShapeDtypeStruct((r, c), F32)] * 4,
        compiler_params=_cparams(dimension_semantics=("parallel",)),
    )(landed, w, m, v)


def _shard_shape(shape, axis):
    return tuple(d // N_DEV if a == axis else d for a, d in enumerate(shape))


def _split_pieces(full, axis):
    r, c = full.shape
    if axis == 0:
        return full.reshape(N_DEV, r // N_DEV, c)
    return full.reshape(r, N_DEV, c // N_DEV).transpose(1, 0, 2)


def _join_shards(gathered, axis):
    _, r, c = gathered.shape
    if axis == 0:
        return gathered.reshape(N_DEV * r, c)
    return gathered.transpose(1, 0, 2).reshape(r, N_DEV * c)


def kernel(x, mem, positions, g_mix, w_in, b_gate, g_q_lat, w_uq, g_kv_lat, w_ukv, w_a_proj, w_b_proj, w_o, g_x, g_mem, w_xq, w_xkv, w_xo, g_ffn, w_gate, w_up, w_down, g_final, loss_target, m_g_mix, m_w_in, m_b_gate, m_g_q_lat, m_w_uq, m_g_kv_lat, m_w_ukv, m_w_a_proj, m_w_b_proj, m_w_o, m_g_x, m_g_mem, m_w_xq, m_w_xkv, m_w_xo, m_g_ffn, m_w_gate, m_w_up, m_w_down, m_g_final, v_g_mix, v_w_in, v_b_gate, v_g_q_lat, v_w_uq, v_g_kv_lat, v_w_ukv, v_w_a_proj, v_w_b_proj, v_w_o, v_g_x, v_g_mem, v_w_xq, v_w_xkv, v_w_xo, v_g_ffn, v_w_gate, v_w_up, v_w_down, v_g_final):
    given = dict(locals())
    s_len = x.shape[1]
    x2d = x.reshape(s_len, D_MODEL)
    mem2d = mem.reshape(-1, D_MODEL)
    target = loss_target.reshape(s_len, D_MODEL)

    names = [name for name, _, _ in SHARDED]
    axis_of = {name: axis for name, _, axis in SHARDED}
    shard2d = lambda name, prefix="": given[prefix + name].reshape(
        _shard_shape(dict((n, s) for n, s, _ in SHARDED)[name], axis_of[name]))

    wire = lambda name: shard2d(name) if name == "b_gate" else shard2d(name).astype(BF16)
    early = [n for n in names if n in NEEDED_FIRST]
    late = [n for n in names if n not in NEEDED_FIRST]
    gathered = _exchange(True, [wire(n) for n in early], "weights_gather_first")
    wts = {n: _join_shards(g, axis_of[n]) for n, g in zip(early, gathered)}

    w_in_p = jnp.concatenate([wts["w_in"][:, :416], jnp.zeros((D_MODEL, 96), BF16), wts["w_in"][:, 416:]], axis=1)
    w_uq_p = jnp.pad(wts["w_uq"].reshape(256, MLA_HEADS, 96), ((0, 0), (0, 0), (0, 32))).reshape(256, 1024)
    ukv = wts["w_ukv"].reshape(128, MLA_HEADS, 128)
    w_uk_p = jnp.pad(ukv[:, :, :64], ((0, 0), (0, 0), (0, 64))).reshape(128, 1024)
    w_uv = ukv[:, :, 64:].reshape(128, 512)
    bg = wts["b_gate"]

    inv_freq = ROPE_THETA ** (-jnp.arange(0, MLA_ROPE, 2, dtype=F32) / MLA_ROPE)
    ang = positions.reshape(s_len).astype(F32)[:, None] * inv_freq
    cos16, sin16 = jnp.cos(ang), jnp.sin(ang)
    cosf = jnp.concatenate([jnp.ones((s_len, 64), F32), cos16, cos16, jnp.ones((s_len, 32), F32)], axis=1)
    sinf = jnp.concatenate([jnp.zeros((s_len, 64), F32), sin16, sin16, jnp.zeros((s_len, 32), F32)], axis=1)

    h1, lat, sb, gates = _in_proj(x2d, g_mix, w_in_p)
    qa, ka, va, q_lat, kv_lat = _mla_prep(lat, g_q_lat, g_kv_lat, w_uq_p, w_uk_p, w_uv, cosf, sinf)
    oa, lse, gathered = _mla_fwd(qa, ka, va, _Exchange(True, [wire(n) for n in late]))
    wts.update({n: _join_shards(g, axis_of[n]) for n, g in zip(late, gathered)})
    ob, sb_r = _sb_fwd(sb)
    x1 = _merge_fwd(x2d, oa, ob, gates, bg, wts["w_a_proj"], wts["w_b_proj"], wts["w_o"])
    mn, xkv = _mem_kv(mem2d, g_mem, wts["w_xkv"])
    x2 = _xattn_fwd(x1, g_x, wts["w_xq"], xkv, wts["w_xo"])
    x3 = _ffn_fwd(x2, g_ffn, wts["w_gate"], wts["w_up"], wts["w_down"])
    g_final2d = g_final.reshape(1, D_MODEL)
    sse, dx3, dx3b, dg_final = _loss_head(x3, g_final2d, target)
    loss = lax.psum(sse[0, 0] * (0.5 / D_MODEL), ("x", "y", "c"))

    dx2, hf, dgt, dup, act, dg_ffn = _ffn_bwd(x2, dx3, g_ffn, wts["w_gate"], wts["w_up"], wts["w_down"])
    dx1, dw_xq, dw_xo, dxkv, dg_x = _xattn_bwd(x1, dx2, g_x, wts["w_xq"], xkv, wts["w_xo"])
    dw_xkv, dg_mem = _mem_bwd(mem2d, g_mem, wts["w_xkv"], mn, dxkv)
    doa, dob, dgates, dpa, dpb, merged, dx1b, dbg = _merge_bwd(
        dx1, oa, ob, gates, bg, wts["w_a_proj"], wts["w_b_proj"], wts["w_o"])
    dsbq, dsbk, dsbv = _sb_bwd(sb, dob, sb_r)
    full_grads = {
        "w_a_proj": _tn_matmul(oa, dpa, "dw_a"),
        "w_b_proj": _tn_matmul(ob, dpb, "dw_b"),
        "w_o": _tn_matmul(merged, dx1b, "dw_o"),
        "w_xq": dw_xq,
        "w_xkv": dw_xkv,
        "w_xo": dw_xo,
        "w_gate": _tn_matmul(hf, dgt, "dw_gate", tn=FF_TILE),
        "w_up": _tn_matmul(hf, dup, "dw_up", tn=FF_TILE),
        "w_down": _tn_matmul(act, dx3b, "dw_down", tka=FF_TILE),
    }
    dqa, dka, dva, got = _mla_bwd(
        qa, ka, va, oa, doa, lse, _Exchange(False, [_split_pieces(full_grads[n], axis_of[n]) for n in late]))
    landed = dict(zip(late, got))
    dlat, dqb, dkb, dvb, dg_q, dg_kv = _mla_prep_bwd(
        lat, g_q_lat, g_kv_lat, w_uq_p, w_uk_p, w_uv, cosf, sinf, dqa, dka, dva)
    grad_x, dproj, dg_mix = _in_proj_bwd(x2d, g_mix, w_in_p, dx1, dlat, dsbq, dsbk, dsbv, dgates)
    dw_in_p = _tn_matmul(h1, dproj, "dw_in")
    dw_uq_p = _tn_matmul(q_lat, dqb, "dw_uq")
    dw_uk_p = _tn_matmul(kv_lat, dkb, "dw_uk")
    dw_uv = _tn_matmul(kv_lat, dvb, "dw_uv")
    full_grads.update({
        "w_in": jnp.concatenate([dw_in_p[:, :416], dw_in_p[:, 512:]], axis=1),
        "b_gate": dbg,
        "w_uq": dw_uq_p.reshape(256, MLA_HEADS, 128)[:, :, :96].reshape(256, 768),
        "w_ukv": jnp.concatenate([dw_uk_p.reshape(128, MLA_HEADS, 128)[:, :, :64],
                                  dw_uv.reshape(128, MLA_HEADS, 64)], axis=2).reshape(128, 1024),
    })
    rep_grads = {"g_mix": dg_mix, "g_q_lat": dg_q, "g_kv_lat": dg_kv, "g_x": dg_x, "g_mem": dg_mem,
                 "g_ffn": dg_ffn, "g_final": dg_final}
    rep_cat = lambda prefix, src: jnp.concatenate(
        [src[prefix + n].reshape(-1) for n, _ in REPLICATED]).reshape(-1, LANES)
    rep_src = jnp.broadcast_to(rep_cat("", rep_grads), (N_DEV,) + rep_cat("", rep_grads).shape)
    got = _exchange(False, [_split_pieces(full_grads[n], axis_of[n]) for n in early] + [rep_src], "grads_last")
    landed.update(zip(early, got[:-1]))
    rep_landed = got[-1]

    res = {}
    for name, _, _ in SHARDED:
        outs = _adamw(landed[name], shard2d(name), shard2d(name, "m_"), shard2d(name, "v_"), "adamw_" + name)
        res[name] = [o.reshape(given[name].shape) for o in outs]
    rep_outs = _adamw(rep_landed, rep_cat("", given), rep_cat("m_", given), rep_cat("v_", given), "adamw_gains")
    off = 0
    for name, n in REPLICATED:
        res[name] = [o.reshape(-1)[off:off + n].reshape(given[name].shape) for o in rep_outs]
        off += n
    result = [loss, grad_x.reshape(x.shape)]
    for k in range(4):
        result.extend(res[name][k] for name in WEIGHT_ORDER)
    return tuple(result)
```

```python
import functools
import math

import jax
import jax.numpy as jnp
from jax import lax
from jax.experimental import pallas as pl
from jax.experimental.pallas import tpu as pltpu

F32 = jnp.float32
BF16 = jnp.bfloat16

D_MODEL = 1024
MLA_HEADS = 8
MLA_Q_RANK = 256
MLA_KV_RANK = 128
MLA_NOPE = 64
MLA_ROPE = 32
ROPE_THETA = 10000.0
SB_WIDTH = 512
X_HEADS = 4
X_HEAD_DIM = 128
D_FF = 2816
EPS = 1e-6
D_IN = 4000
D_IN_PAD = 4096
K_R_OFF = 384
LANES = 128
HEAD_PAD = 128
MLA_SCALE = 1.0 / math.sqrt(MLA_NOPE + MLA_ROPE)
SB_SCALE = 0.125
LOG2E = math.log2(math.e)
LN2 = math.log(2.0)
MLA_Q_FOLD = MLA_SCALE * LOG2E
SB_Q_FOLD = SB_SCALE * LOG2E
SB_CUT = -160.0
X_SCALE = 1.0 / math.sqrt(X_HEAD_DIM)
NEG_BIG = -1e30

ADAM_LR = 0.001
ADAM_B1 = 0.9
ADAM_B2 = 0.999
ADAM_EPS = 1e-08
ADAM_WD = 0.01
ADAM_STEP = 10

N_DEV = 8
MIB = 1024 * 1024
ADAM_BLOCK_BYTES = 4 * MIB

SHARDED = (
    ("w_in", (D_MODEL, D_IN), 1),
    ("b_gate", (2, D_MODEL), 1),
    ("w_uq", (MLA_Q_RANK, 768), 1),
    ("w_ukv", (MLA_KV_RANK, 1024), 1),
    ("w_a_proj", (512, D_MODEL), 1),
    ("w_b_proj", (512, D_MODEL), 1),
    ("w_o", (D_MODEL, D_MODEL), 0),
    ("w_xq", (D_MODEL, 512), 0),
    ("w_xkv", (D_MODEL, 1024), 0),
    ("w_xo", (512, D_MODEL), 1),
    ("w_gate", (D_MODEL, D_FF), 1),
    ("w_up", (D_MODEL, D_FF), 1),
    ("w_down", (D_FF, D_MODEL), 0),
)
NEEDED_FIRST = ("w_in", "b_gate", "w_uq", "w_ukv")
REPLICATED = (
    ("g_mix", 1024), ("g_q_lat", 256), ("g_kv_lat", 128), ("g_x", 1024),
    ("g_mem", 1024), ("g_ffn", 1024), ("g_final", 1024),
)
WEIGHT_ORDER = ("g_mix", "w_in", "b_gate", "g_q_lat", "w_uq", "g_kv_lat", "w_ukv", "w_a_proj",
                "w_b_proj", "w_o", "g_x", "g_mem", "w_xq", "w_xkv", "w_xo", "g_ffn", "w_gate",
                "w_up", "w_down", "g_final")


def _round_up(n, m):
    return -(-n // m) * m


def _cparams(vmem_mib=None, **kw):
    if vmem_mib is not None:
        kw["vmem_limit_bytes"] = vmem_mib * MIB
    return pltpu.CompilerParams(**kw)


def _dot(a, b):
    return jnp.dot(a, b, preferred_element_type=F32)


def _dot_nt(a, b):
    return lax.dot_general(a, b, (((1,), (1,)), ((), ())), preferred_element_type=F32)


def _dot_tn(a, b):
    return lax.dot_general(a, b, (((0,), (0,)), ((), ())), preferred_element_type=F32)


def _rms(x, g):
    r = lax.rsqrt(jnp.mean(x * x, axis=-1, keepdims=True) + EPS)
    xh = x * r
    return xh * g, xh, r


def _rms_bwd(dy, xh, r, g):
    u = dy * g
    dx = r * (u - xh * jnp.mean(u * xh, axis=-1, keepdims=True))
    return dx, dy * xh


def _sigmoid(z):
    return 1.0 / (1.0 + jnp.exp(-z))


def _acc_rows(ref, val, first):
    s = jnp.sum(val, axis=0, keepdims=True)

    @pl.when(first)
    def _():
        ref[...] = s

    @pl.when(jnp.logical_not(first))
    def _():
        ref[...] += s


def _acc(ref, val, first):
    @pl.when(first)
    def _():
        ref[...] = val

    @pl.when(jnp.logical_not(first))
    def _():
        ref[...] += val


def _peer(k):
    x, y, c = lax.axis_index("x"), lax.axis_index("y"), lax.axis_index("c")
    px = 1 - x if (k >> 2) & 1 else x
    py = 1 - y if (k >> 1) & 1 else y
    pc = 1 - c if k & 1 else c
    return (px, py, pc), 4 * px + 2 * py + pc


N_PEERS = N_DEV - 1
OTHER_CHIPS = (2, 4, 6)


def _land_shape(gather, src):
    return (N_DEV,) + src.shape if gather else src.shape


class _Exchange:
    def __init__(self, gather, srcs):
        self.gather, self.n, self.srcs = gather, len(srcs), list(srcs)
        self.out_shape = [jax.ShapeDtypeStruct(_land_shape(gather, s), s.dtype) for s in srcs]
        self.specs = [pl.BlockSpec(memory_space=pl.ANY)] * self.n
        self.scratch = [pltpu.SemaphoreType.DMA((self.n * N_PEERS,)), pltpu.SemaphoreType.DMA((self.n * N_PEERS,)),
                        pltpu.SemaphoreType.DMA((self.n,))]

    def bind(self, src, land, sems):
        self.src, self.land = src, land
        self.send_sems, self.recv_sems, self.local_sems = sems

    def _copy(self, a, k, source, to, target=1):
        return pltpu.make_async_remote_copy(
            src_ref=source, dst_ref=to,
            send_sem=self.send_sems.at[a * N_PEERS + k - 1], recv_sem=self.recv_sems.at[a * N_PEERS + k - 1],
            device_id=_peer(target)[0], device_id_type=pl.DeviceIdType.MESH)

    def _row(self, a, k):
        return self.land[a].at[_peer(k)[1]]

    def _mine(self, a):
        me = _peer(0)[1]
        return pltpu.make_async_copy(self.src[a] if self.gather else self.src[a].at[me], self.land[a].at[me],
                                     self.local_sems.at[a])

    def issue(self):
        me = _peer(0)[1]
        for a in range(self.n):
            self._mine(a).start()
            for k in ((1,) + OTHER_CHIPS if self.gather else range(1, N_DEV)):
                source = self.src[a] if self.gather else self.src[a].at[_peer(k)[1]]
                self._copy(a, k, source, self.land[a].at[me], target=k).start()

    def finish(self):
        me = _peer(0)[1]
        part = lambda a: self.src[a] if self.gather else self.src[a].at[me]
        if self.gather:
            for a in range(self.n):
                for k in OTHER_CHIPS:
                    self._copy(a, k, part(a), self._row(a, k)).wait_recv()
                    self._copy(a, k + 1, self._row(a, k), self._row(a, k), target=1).start()
        for a in range(self.n):
            for k in ((1, 3, 5, 7) if self.gather else range(1, N_DEV)):
                self._copy(a, k, part(a), self._row(a, k)).wait_recv()
        for a in range(self.n):
            for k in range(1, N_DEV):
                self._copy(a, k, part(a), self.land[a].at[me]).wait_send()
            self._mine(a).wait()


def _exchange(gather, srcs, name):
    ex = _Exchange(gather, srcs)

    def body(*refs):
        ex.bind(refs[:ex.n], refs[ex.n:2 * ex.n], refs[2 * ex.n:])
        ex.issue()
        ex.finish()

    return pl.pallas_call(body, name=name, out_shape=ex.out_shape, in_specs=ex.specs, out_specs=ex.specs,
                          scratch_shapes=ex.scratch)(*ex.srcs)


def _tn_matmul(a, b, name, tka=512, tn=1024, ts=2048, scale=None):
    s_len, ka = a.shape
    n = b.shape[1]
    tka, tn, ts = min(tka, ka), min(tn, n), min(ts, s_len)
    assert ka % tka == 0 and n % tn == 0 and s_len % ts == 0

    def body(a_ref, b_ref, o_ref):
        bv = b_ref[...]
        if scale is not None:
            bv = bv * scale
        _acc(o_ref, _dot_tn(a_ref[...], bv.astype(BF16)), pl.program_id(2) == 0)

    return pl.pallas_call(
        body, name=name, grid=(ka // tka, n // tn, s_len // ts),
        in_specs=[pl.BlockSpec((ts, tka), lambda i, j, s: (s, i)),
                  pl.BlockSpec((ts, tn), lambda i, j, s: (s, j))],
        out_specs=pl.BlockSpec((tka, tn), lambda i, j, s: (i, j)),
        out_shape=jax.ShapeDtypeStruct((ka, n), F32),
        compiler_params=_cparams(dimension_semantics=("parallel", "parallel", "arbitrary")),
    )(a, b)


def _row_block(s_len):
    return min(s_len, 512)


def _in_proj(x, g, w):
    s_len = x.shape[0]
    tm = _row_block(s_len)

    def body(x_ref, g_ref, w_ref, h_ref, lat_ref, sb_ref, gate_ref):
        h, _, _ = _rms(x_ref[...], g_ref[...])
        hb = h.astype(BF16)
        h_ref[...] = hb
        p = _dot(hb, w_ref[:, 0:1024])
        lat_ref[...] = p[:, 0:512]
        sb_ref[:, 0:512] = (p[:, 512:1024] * SB_Q_FOLD).astype(BF16)
        sb_ref[:, 512:1536] = _dot(hb, w_ref[:, 1024:2048]).astype(BF16)
        gate_ref[:, 0:1024] = _dot(hb, w_ref[:, 2048:3072])
        gate_ref[:, 1024:2048] = _dot(hb, w_ref[:, 3072:4096])

    rowb = lambda n: pl.BlockSpec((tm, n), lambda i: (i, 0))
    return pl.pallas_call(
        body, name="in_proj", grid=(s_len // tm,),
        in_specs=[rowb(D_MODEL), pl.BlockSpec((1, D_MODEL), lambda i: (0, 0)),
                  pl.BlockSpec((D_MODEL, D_IN_PAD), lambda i: (0, 0))],
        out_specs=[rowb(D_MODEL), rowb(512), rowb(3 * SB_WIDTH), rowb(2 * D_MODEL)],
        out_shape=[jax.ShapeDtypeStruct((s_len, D_MODEL), BF16),
                   jax.ShapeDtypeStruct((s_len, 512), F32),
                   jax.ShapeDtypeStruct((s_len, 3 * SB_WIDTH), BF16),
                   jax.ShapeDtypeStruct((s_len, 2 * D_MODEL), F32)],
        compiler_params=_cparams(48, dimension_semantics=("parallel",)),
    )(x, g, w)


def _rope_rot(blk, lane):
    return jnp.where(lane < 80, -pltpu.roll(blk, 112, 1), pltpu.roll(blk, 16, 1))


def _rope_rot_t(blk, lane):
    return jnp.where(lane < 80, pltpu.roll(blk, 112, 1), -pltpu.roll(blk, 16, 1))


def _mla_prep(lat, g_q, g_kv, w_uq, w_uk, w_uv, cosf, sinf):
    s_len = lat.shape[0]
    tm = _row_block(s_len)

    def body(lat_ref, gq_ref, gkv_ref, wuq_ref, wuk_ref, wuv_ref, cos_ref, sin_ref,
             q_ref, k_ref, v_ref, ql_ref, kvl_ref):
        lane = lax.broadcasted_iota(jnp.int32, (tm, LANES), 1)
        cosv, sinv = cos_ref[...], sin_ref[...]
        ql, _, _ = _rms(lat_ref[:, 0:256], gq_ref[...])
        kvl, _, _ = _rms(lat_ref[:, 256:384], gkv_ref[...])
        qlb, kvlb = ql.astype(BF16), kvl.astype(BF16)
        ql_ref[...] = qlb
        kvl_ref[...] = kvlb
        q = _dot(qlb, wuq_ref[...])
        kn = _dot(kvlb, wuk_ref[...])
        v_ref[...] = _dot(kvlb, wuv_ref[...]).astype(BF16)
        kr = pltpu.roll(lat_ref[:, K_R_OFF:K_R_OFF + LANES], 64, 1)
        kr = kr * cosv + _rope_rot(kr, lane) * sinv
        for h in range(MLA_HEADS):
            sl = slice(h * HEAD_PAD, (h + 1) * HEAD_PAD)
            blk = q[:, sl]
            q_ref[:, sl] = ((blk * cosv + _rope_rot(blk, lane) * sinv) * MLA_Q_FOLD).astype(BF16)
            k_ref[:, sl] = (kn[:, sl] + kr).astype(BF16)

    full = lambda shape: pl.BlockSpec(shape, lambda i: (0, 0))
    rowb = lambda n: pl.BlockSpec((tm, n), lambda i: (i, 0))
    return pl.pallas_call(
        body, name="mla_prep", grid=(s_len // tm,),
        in_specs=[rowb(512), full((1, 256)), full((1, 128)), full((256, 1024)), full((128, 1024)),
                  full((128, 512)), rowb(128), rowb(128)],
        out_specs=[rowb(1024), rowb(1024), rowb(512), rowb(256), rowb(128)],
        out_shape=[jax.ShapeDtypeStruct((s_len, 1024), BF16), jax.ShapeDtypeStruct((s_len, 1024), BF16),
                   jax.ShapeDtypeStruct((s_len, 512), BF16), jax.ShapeDtypeStruct((s_len, 256), BF16),
                   jax.ShapeDtypeStruct((s_len, 128), BF16)],
        compiler_params=_cparams(dimension_semantics=("parallel",)),
    )(lat, g_q, g_kv, w_uq, w_uk, w_uv, cosf, sinf)


ATTN_TQ = 1024
ATTN_TH = 512
ATTN_TK = 256
SB_TQ = 512
SB_TH = 256
MLA_TK = 512


def _attn_blocks(s_len, tk=ATTN_TK, tq=ATTN_TQ, th=ATTN_TH):
    tq, th, tk = min(s_len, tq), min(s_len, th), min(s_len, tk)
    return tq, th, tk, tq // tk


def _chains(tq, th):
    return [(hh, r0) for hh in range(2) for r0 in range(0, tq, th)]


def _diag_mask(th, tk, r0, sub, strict):
    lo, hi = sub * tk, (sub + 1) * tk - 1
    last, first = r0 + th - 1, r0
    if (lo >= last) if strict else (lo > last):
        return "none"
    if (hi < first) if strict else (hi <= first):
        return "all"
    row = lax.broadcasted_iota(jnp.int32, (th, tk), 0) + r0
    col = lax.broadcasted_iota(jnp.int32, (th, tk), 1) + lo
    return col < row if strict else col <= row


def _mla_fwd(q, k, v, ride):
    s_len = q.shape[0]
    tq, th, tk, nsub = _attn_blocks(s_len, MLA_TK)
    chains = _chains(tq, th)
    nh = tq // th
    nq = s_len // tq

    def body(q_ref, k_ref, v_ref, *rest):
        o_ref, lse_ref = rest[ride.n:ride.n + 2]
        ride.bind(rest[:ride.n], rest[ride.n + 2:2 * ride.n + 2], rest[2 * ride.n + 2:])
        pl.when((pl.program_id(0) == 0) & (pl.program_id(1) == 0))(ride.issue)
        i = pl.program_id(1)
        lane = lax.broadcasted_iota(jnp.int32, (th, LANES), 1)
        hsl = [slice(hh * HEAD_PAD, (hh + 1) * HEAD_PAD) for hh in range(2)]

        def step(kb, carry, sub):
            rows = pl.ds(pl.multiple_of(kb * tk, tk), tk)
            vblk = v_ref[rows, :]
            masks = ["all" if sub is None else _diag_mask(th, tk, r0, sub, strict=False) for _, r0 in chains]
            live = [n for n, m in enumerate(masks) if not (isinstance(m, str) and m == "none")]
            s = {n: _dot_nt(q_ref[chains[n][1]:chains[n][1] + th, hsl[chains[n][0]]], k_ref[rows, hsl[chains[n][0]]])
                 for n in live}
            new = list(carry)
            pb, alpha = {}, {}
            for n in live:
                m, l, _ = carry[n]
                sn = s[n]
                if not isinstance(masks[n], str):
                    sn = jnp.where(masks[n], sn, NEG_BIG)
                m_new = jnp.maximum(m, jnp.max(sn, axis=-1, keepdims=True))
                alpha[n] = jnp.exp2(m - m_new)
                p = jnp.exp2(sn - m_new)
                pb[n] = p.astype(BF16)
                new[n] = (m_new, alpha[n] * l + jnp.sum(p, axis=-1, keepdims=True), None)
            pv = {n: _dot(pb[n], vblk) for n in live}
            for n in live:
                new[n] = (new[n][0], new[n][1], alpha[n] * carry[n][2] + pv[n])
            return tuple(new)

        init = (jnp.full((th, 1), NEG_BIG, F32), jnp.zeros((th, 1), F32), jnp.zeros((th, LANES), F32))
        carry = lax.fori_loop(0, i * nsub, lambda kb, cy: step(kb, cy, None), (init,) * len(chains))
        for sub in range(nsub):
            carry = step(i * nsub + sub, carry, sub)
        for c in range(nh):
            (m0, l0, a0), (m1, l1, a1) = carry[c], carry[nh + c]
            rs = slice(c * th, (c + 1) * th)
            o_ref[rs, :] = jnp.where(lane < 64, a0 / l0, a1 / l1).astype(BF16)
            lse_ref[rs, :] = jnp.where(lane < 64, m0 + jnp.log2(l0), m1 + jnp.log2(l1))
        pl.when((pl.program_id(0) == 3) & (i == nq - 1))(ride.finish)

    outs = pl.pallas_call(
        body, name="mla_fwd", grid=(4, nq),
        in_specs=[pl.BlockSpec((tq, 2 * HEAD_PAD), lambda p, i: (i, p)),
                  pl.BlockSpec((s_len, 2 * HEAD_PAD), lambda p, i: (0, p)),
                  pl.BlockSpec((s_len, LANES), lambda p, i: (0, p))] + ride.specs,
        out_specs=[pl.BlockSpec((tq, LANES), lambda p, i: (i, p)),
                   pl.BlockSpec((None, tq, LANES), lambda p, i: (p, i, 0))] + ride.specs,
        out_shape=[jax.ShapeDtypeStruct((s_len, 512), BF16),
                   jax.ShapeDtypeStruct((4, s_len, LANES), F32)] + ride.out_shape,
        scratch_shapes=ride.scratch,
        compiler_params=_cparams(40, dimension_semantics=("arbitrary", "arbitrary")),
    )(q, k, v, *ride.srcs)
    return outs[0], outs[1], outs[2:]


def _mla_bwd(q, k, v, o, do, lse, ride):
    s_len = q.shape[0]
    tq, th, tk, nsub = _attn_blocks(s_len, MLA_TK)
    chains = _chains(tq, th)
    nq = s_len // tq

    def body(q_ref, k_ref, v_ref, o_ref, do_ref, lse_ref, *rest):
        dq_ref, dk_ref, dv_ref = rest[ride.n:ride.n + 3]
        ride.bind(rest[:ride.n], rest[ride.n + 3:2 * ride.n + 3], rest[2 * ride.n + 3:])
        pl.when((pl.program_id(0) == 0) & (pl.program_id(1) == 0))(ride.issue)
        i = pl.program_id(1)
        lane = lax.broadcasted_iota(jnp.int32, (th, LANES), 1)

        @pl.when(i == 0)
        def _():
            dk_ref[...] = jnp.zeros_like(dk_ref)
            dv_ref[...] = jnp.zeros_like(dv_ref)

        hsl = [slice(hh * HEAD_PAD, (hh + 1) * HEAD_PAD) for hh in range(2)]
        qs, dos, deltas, lses = [], [], [], []
        for hh, r0 in chains:
            rs = slice(r0, r0 + th)
            qs.append(q_ref[rs, hsl[hh]])
            doh = jnp.where((lane // 64) == hh, do_ref[rs, :], jnp.zeros((), BF16))
            dos.append(doh)
            deltas.append(jnp.sum(doh.astype(F32) * o_ref[rs, :].astype(F32), axis=-1, keepdims=True))
            lses.append(lse_ref[rs, 64 * hh:64 * hh + 1])

        def step(kb, dqs, sub):
            rows = pl.ds(pl.multiple_of(kb * tk, tk), tk)
            vblk = v_ref[rows, :]
            new, p_all, do_all = [], [], []
            ds_h, q_h = [[], []], [[], []]
            for c, (hh, r0) in enumerate(chains):
                mask = "all" if sub is None else _diag_mask(th, tk, r0, sub, strict=False)
                if isinstance(mask, str) and mask == "none":
                    new.append(dqs[c])
                    continue
                kblk = k_ref[rows, hsl[hh]]
                s = _dot_nt(qs[c], kblk)
                if not isinstance(mask, str):
                    s = jnp.where(mask, s, NEG_BIG)
                p = jnp.exp2(s - lses[c])
                dp = _dot_nt(dos[c], vblk)
                ds = (p * (dp - deltas[c]) * MLA_SCALE).astype(BF16)
                p_all.append(p.astype(BF16))
                do_all.append(dos[c])
                ds_h[hh].append(ds)
                q_h[hh].append(qs[c])
                new.append(dqs[c] + _dot(ds, kblk))
            dv_ref[rows, :] += _dot_tn(jnp.concatenate(p_all, axis=0), jnp.concatenate(do_all, axis=0))
            for hh in range(2):
                dk_ref[rows, hsl[hh]] += _dot_tn(jnp.concatenate(ds_h[hh], axis=0),
                                                 jnp.concatenate(q_h[hh], axis=0))
            return tuple(new)

        zero = jnp.zeros((th, LANES), F32)
        dqs = lax.fori_loop(0, i * nsub, lambda kb, cy: step(kb, cy, None), (zero,) * len(chains))
        for sub in range(nsub):
            dqs = step(i * nsub + sub, dqs, sub)
        for c, (hh, r0) in enumerate(chains):
            dq_ref[r0:r0 + th, hsl[hh]] = dqs[c]
        pl.when((pl.program_id(0) == 3) & (i == nq - 1))(ride.finish)

    outs = pl.pallas_call(
        body, name="mla_bwd", grid=(4, nq),
        in_specs=[pl.BlockSpec((tq, 2 * HEAD_PAD), lambda p, i: (i, p)),
                  pl.BlockSpec((s_len, 2 * HEAD_PAD), lambda p, i: (0, p)),
                  pl.BlockSpec((s_len, LANES), lambda p, i: (0, p)),
                  pl.BlockSpec((tq, LANES), lambda p, i: (i, p)),
                  pl.BlockSpec((tq, LANES), lambda p, i: (i, p)),
                  pl.BlockSpec((None, tq, LANES), lambda p, i: (p, i, 0))] + ride.specs,
        out_specs=[pl.BlockSpec((tq, 2 * HEAD_PAD), lambda p, i: (i, p)),
                   pl.BlockSpec((s_len, 2 * HEAD_PAD), lambda p, i: (0, p)),
                   pl.BlockSpec((s_len, LANES), lambda p, i: (0, p))] + ride.specs,
        out_shape=[jax.ShapeDtypeStruct((s_len, 1024), F32), jax.ShapeDtypeStruct((s_len, 1024), F32),
                   jax.ShapeDtypeStruct((s_len, 512), F32)] + ride.out_shape,
        scratch_shapes=ride.scratch,
        compiler_params=_cparams(56, dimension_semantics=("arbitrary", "arbitrary")),
    )(q, k, v, o, do, lse, *ride.srcs)
    return outs[0], outs[1], outs[2], outs[3:]


def _log_sigmoids(z2):
    sp = jnp.log2(1.0 + jnp.exp2(-jnp.abs(z2)))
    lb = jnp.minimum(z2, 0.0) - sp
    return lb, lb - z2


def _split_dot(x, w, parts, nt=False):
    dot = _dot_nt if nt else _dot
    out = None
    for _ in range(parts):
        xb = x.astype(BF16)
        t = dot(xb, w)
        out = t if out is None else out + t
        x = x - xb.astype(F32)
    return out


def _sb_fwd(sb):
    s_len = sb.shape[0]
    tq, th, tk, nsub = _attn_blocks(s_len, tq=SB_TQ, th=SB_TH)
    chains = _chains(tq, th)
    nh = tq // th
    assert s_len // tk <= 64

    def body(q_ref, k_ref, v_ref, o_ref, r_ref):
        i = pl.program_id(1)
        lane = lax.broadcasted_iota(jnp.int32, (th, LANES), 1)
        upper = (lax.broadcasted_iota(jnp.int32, (tk, tk), 0)
                 > lax.broadcasted_iota(jnp.int32, (tk, tk), 1)).astype(BF16)
        qs = [jnp.where((lane // 64) == hh, q_ref[r0:r0 + th, :], jnp.zeros((), BF16)) for hh, r0 in chains]

        def step(kb, carry, sub):
            rows = pl.ds(pl.multiple_of(kb * tk, tk), tk)
            kblk, vblk = k_ref[rows, :], v_ref[rows, :]
            masks = ["all" if sub is None else _diag_mask(th, tk, r0, sub, strict=True) for _, r0 in chains]
            live = [n for n, m in enumerate(masks) if not (isinstance(m, str) and m == "none")]
            masked = {n: not isinstance(masks[n], str) for n in live}
            z = {n: _dot_nt(qs[n], kblk) for n in live}
            lb, lom = {}, {}
            for n in live:
                lb[n], lom[n] = _log_sigmoids(z[n])
                if masked[n]:
                    lom[n] = jnp.where(masks[n], lom[n], 0.0)
            suf = {n: _split_dot(lom[n], upper, 2) for n in live}
            a = {}
            for n in live:
                a[n] = jnp.exp2(lb[n] + suf[n] + carry[n][0])
                if masked[n]:
                    a[n] = jnp.where(masks[n], a[n], 0.0)
            pv = {n: _dot(a[n].astype(BF16), vblk) for n in live}
            new = list(carry)
            for n in live:
                c, acc, r = carry[n]
                rs = suf[n][:, 0:1] + lom[n][:, 0:1]
                new[n] = (c + rs, acc + pv[n], jnp.where(lane == 64 * chains[n][0] + kb, rs, r))
            return tuple(new)

        init = (jnp.zeros((th, 1), F32), jnp.zeros((th, LANES), F32), jnp.zeros((th, LANES), F32))
        carry = (init,) * len(chains)
        for sub in reversed(range(nsub)):
            carry = step(i * nsub + sub, carry, sub)

        def spent(cy):
            top = functools.reduce(jnp.maximum, [jnp.max(c) for c, _, _ in cy])
            return (top < SB_CUT).astype(jnp.int32)

        def walk(state):
            t, _, cy = state
            cy = step(i * nsub - 1 - t, cy, None)
            return t + 1, spent(cy), cy

        _, _, carry = lax.while_loop(lambda st: (st[0] < i * nsub) & (st[1] == 0), walk,
                                     (jnp.int32(0), spent(carry), carry))
        for n in range(nh):
            rs = slice(n * th, (n + 1) * th)
            o_ref[rs, :] = jnp.where(lane < 64, carry[n][1], carry[nh + n][1]).astype(BF16)
            r_ref[rs, :] = jnp.where(lane < 64, carry[n][2], carry[nh + n][2])

    return pl.pallas_call(
        body, name="sb_fwd", grid=(4, s_len // tq),
        in_specs=[pl.BlockSpec((tq, LANES), lambda p, i: (i, p)),
                  pl.BlockSpec((s_len, LANES), lambda p, i: (0, 4 + p)),
                  pl.BlockSpec((s_len, LANES), lambda p, i: (0, 8 + p))],
        out_specs=[pl.BlockSpec((tq, LANES), lambda p, i: (i, p)),
                   pl.BlockSpec((None, tq, LANES), lambda p, i: (p, i, 0))],
        out_shape=[jax.ShapeDtypeStruct((s_len, 512), BF16), jax.ShapeDtypeStruct((4, s_len, LANES), F32)],
        compiler_params=_cparams(40, dimension_semantics=("parallel", "arbitrary")),
    )(sb, sb, sb)


def _sb_bwd(sb, do, r):
    s_len = sb.shape[0]
    tq, th, tk, nsub = _attn_blocks(s_len)
    chains = _chains(tq, th)
    nh = tq // th

    def body(q_ref, k_ref, v_ref, do_ref, r_ref, dq_ref, dk_ref, dv_ref):
        i = pl.program_id(1)
        lane = lax.broadcasted_iota(jnp.int32, (th, LANES), 1)
        upper = (lax.broadcasted_iota(jnp.int32, (tk, tk), 0)
                 > lax.broadcasted_iota(jnp.int32, (tk, tk), 1)).astype(BF16)
        tri = (lax.broadcasted_iota(jnp.int32, (LANES, LANES), 0)
               > lax.broadcasted_iota(jnp.int32, (LANES, LANES), 1)).astype(BF16)

        @pl.when(i == 0)
        def _():
            dk_ref[...] = jnp.zeros_like(dk_ref)
            dv_ref[...] = jnp.zeros_like(dv_ref)

        qs, dos, rights = [], [], []
        for hh, r0 in chains:
            rs = slice(r0, r0 + th)
            hm = (lane // 64) == hh
            qs.append(jnp.where(hm, q_ref[rs, :], jnp.zeros((), BF16)))
            dos.append(jnp.where(hm, do_ref[rs, :], jnp.zeros((), BF16)))
            rights.append(_split_dot(jnp.where(hm, r_ref[rs, :], 0.0), tri, 3))

        def step(kb, carry, sub):
            rows = pl.ds(pl.multiple_of(kb * tk, tk), tk)
            kblk, vblk = k_ref[rows, :], v_ref[rows, :]
            new, a_all, do_all, dz_all, q_all = [], [], [], [], []
            for n, ((hh, r0), (pre, dq)) in enumerate(zip(chains, carry)):
                mask = "all" if sub is None else _diag_mask(th, tk, r0, sub, strict=True)
                if isinstance(mask, str) and mask == "none":
                    new.append((pre, dq))
                    continue
                c = jnp.sum(jnp.where(lane == 64 * hh + kb, rights[n], 0.0), axis=-1, keepdims=True)
                z = _dot_nt(qs[n], kblk)
                lb, lom = _log_sigmoids(z)
                if not isinstance(mask, str):
                    lom = jnp.where(mask, lom, 0.0)
                suf = _split_dot(lom, upper, 2)
                a = jnp.exp2(lb + suf + c)
                if not isinstance(mask, str):
                    a = jnp.where(mask, a, 0.0)
                g = a * _dot_nt(dos[n], vblk)
                left = _split_dot(g, upper, 1, nt=True) + pre
                sig = jnp.exp2(lb)
                dz = g * (1.0 - sig) - sig * left
                if not isinstance(mask, str):
                    dz = jnp.where(mask, dz, 0.0)
                dzb = dz.astype(BF16)
                a_all.append(a.astype(BF16))
                do_all.append(dos[n])
                dz_all.append(dzb)
                q_all.append(qs[n])
                new.append((left[:, tk - 1:tk] + g[:, tk - 1:tk], dq + _dot(dzb, kblk)))
            dv_ref[rows, :] += _dot_tn(jnp.concatenate(a_all, axis=0), jnp.concatenate(do_all, axis=0))
            dk_ref[rows, :] += _dot_tn(jnp.concatenate(dz_all, axis=0), jnp.concatenate(q_all, axis=0))
            return tuple(new)

        lane1 = lax.broadcasted_iota(jnp.int32, (1, LANES), 1)
        first = i * nsub
        for n, (hh, _) in enumerate(chains):
            top = jnp.max(rights[n], axis=0, keepdims=True)
            kb_of = lane1 - 64 * hh
            live = (kb_of >= 0) & (kb_of < i * nsub) & (top >= SB_CUT)
            first = jnp.minimum(first, jnp.min(jnp.where(live, kb_of, i * nsub)))

        init = (jnp.zeros((th, 1), F32), jnp.zeros((th, LANES), F32))
        carry = lax.fori_loop(first, i * nsub, lambda kb, cy: step(kb, cy, None), (init,) * len(chains))
        for sub in range(nsub):
            carry = step(i * nsub + sub, carry, sub)
        for n in range(nh):
            dq_ref[n * th:(n + 1) * th, :] = jnp.where(lane < 64, carry[n][1], carry[nh + n][1]) * SB_SCALE

    return pl.pallas_call(
        body, name="sb_bwd", grid=(4, s_len // tq),
        in_specs=[pl.BlockSpec((tq, LANES), lambda p, i: (i, p)),
                  pl.BlockSpec((s_len, LANES), lambda p, i: (0, 4 + p)),
                  pl.BlockSpec((s_len, LANES), lambda p, i: (0, 8 + p)),
                  pl.BlockSpec((tq, LANES), lambda p, i: (i, p)),
                  pl.BlockSpec((None, tq, LANES), lambda p, i: (p, i, 0))],
        out_specs=[pl.BlockSpec((tq, LANES), lambda p, i: (i, p)),
                   pl.BlockSpec((s_len, LANES), lambda p, i: (0, p)),
                   pl.BlockSpec((s_len, LANES), lambda p, i: (0, p))],
        out_shape=[jax.ShapeDtypeStruct((s_len, 512), F32)] * 3,
        compiler_params=_cparams(48, dimension_semantics=("arbitrary", "arbitrary")),
    )(sb, sb, sb, do, r)


def _merge_fwd(x, oa, ob, gates, bg, wa, wb, wo):
    s_len = x.shape[0]
    tm = _row_block(s_len)

    def body(x_ref, oa_ref, ob_ref, g_ref, bg_ref, wa_ref, wb_ref, wo_ref, y_ref):
        pa = _dot(oa_ref[...], wa_ref[...])
        pb = _dot(ob_ref[...], wb_ref[...])
        merged = (_sigmoid(g_ref[:, 0:D_MODEL] + bg_ref[0:1, :]) * pa
                  + _sigmoid(g_ref[:, D_MODEL:2 * D_MODEL] + bg_ref[1:2, :]) * pb)
        y_ref[...] = x_ref[...] + _dot(merged.astype(BF16), wo_ref[...])

    full = lambda shape: pl.BlockSpec(shape, lambda i: (0, 0))
    rowb = lambda n: pl.BlockSpec((tm, n), lambda i: (i, 0))
    return pl.pallas_call(
        body, name="merge_fwd", grid=(s_len // tm,),
        in_specs=[rowb(1024), rowb(512), rowb(512), rowb(2048), full((2, 1024)), full((512, 1024)),
                  full((512, 1024)), full((1024, 1024))],
        out_specs=rowb(1024),
        out_shape=jax.ShapeDtypeStruct((s_len, D_MODEL), F32),
        compiler_params=_cparams(48, dimension_semantics=("parallel",)),
    )(x, oa, ob, gates, bg, wa, wb, wo)


def _merge_bwd(dx1, oa, ob, gates, bg, wa, wb, wo):
    s_len = dx1.shape[0]
    tm = _row_block(s_len)

    def body(dx_ref, oa_ref, ob_ref, g_ref, bg_ref, wa_ref, wb_ref, wo_ref,
             doa_ref, dob_ref, dgate_ref, dpa_ref, dpb_ref, merged_ref, dxb_ref, dbg_ref):
        first = pl.program_id(0) == 0
        dxb = dx_ref[...].astype(BF16)
        dxb_ref[...] = dxb
        pa = _dot(oa_ref[...], wa_ref[...])
        pb = _dot(ob_ref[...], wb_ref[...])
        sa = _sigmoid(g_ref[:, 0:D_MODEL] + bg_ref[0:1, :])
        sbg = _sigmoid(g_ref[:, D_MODEL:2 * D_MODEL] + bg_ref[1:2, :])
        merged_ref[...] = (sa * pa + sbg * pb).astype(BF16)
        dm = _dot_nt(dxb, wo_ref[...])
        dpa = (dm * sa).astype(BF16)
        dpb = (dm * sbg).astype(BF16)
        dpa_ref[...] = dpa
        dpb_ref[...] = dpb
        dga = dm * pa * sa * (1.0 - sa)
        dgb = dm * pb * sbg * (1.0 - sbg)
        dgate_ref[:, 0:D_MODEL] = dga.astype(BF16)
        dgate_ref[:, D_MODEL:2 * D_MODEL] = dgb.astype(BF16)
        _acc_rows(dbg_ref.at[0:1, :], dga, first)
        _acc_rows(dbg_ref.at[1:2, :], dgb, first)
        doa_ref[...] = _dot_nt(dpa, wa_ref[...]).astype(BF16)
        dob_ref[...] = _dot_nt(dpb, wb_ref[...]).astype(BF16)

    full = lambda shape: pl.BlockSpec(shape, lambda i: (0, 0))
    rowb = lambda n: pl.BlockSpec((tm, n), lambda i: (i, 0))
    sds = lambda n, dt: jax.ShapeDtypeStruct((s_len, n), dt)
    return pl.pallas_call(
        body, name="merge_bwd", grid=(s_len // tm,),
        in_specs=[rowb(1024), rowb(512), rowb(512), rowb(2048), full((2, 1024)), full((512, 1024)),
                  full((512, 1024)), full((1024, 1024))],
        out_specs=[rowb(512), rowb(512), rowb(2048), rowb(1024), rowb(1024), rowb(1024), rowb(1024),
                   full((2, 1024))],
        out_shape=[sds(512, BF16), sds(512, BF16), sds(2048, BF16), sds(1024, BF16), sds(1024, BF16),
                   sds(1024, BF16), sds(1024, BF16), jax.ShapeDtypeStruct((2, 1024), F32)],
        compiler_params=_cparams(48, dimension_semantics=("arbitrary",)),
    )(dx1, oa, ob, gates, bg, wa, wb, wo)


def _mem_kv(mem, g, w):
    m_len = mem.shape[0]

    def body(mem_ref, g_ref, w_ref, mn_ref, kv_ref):
        mn, _, _ = _rms(mem_ref[...], g_ref[...])
        mnb = mn.astype(BF16)
        mn_ref[...] = mnb
        kv_ref[...] = _dot(mnb, w_ref[...]).astype(BF16)

    return pl.pallas_call(
        body, name="mem_kv",
        out_shape=[jax.ShapeDtypeStruct((m_len, D_MODEL), BF16), jax.ShapeDtypeStruct((m_len, 1024), BF16)],
    )(mem, g, w)


def _mem_bwd(mem, g, w, mn, dkv):
    def body(mem_ref, g_ref, w_ref, mn_ref, dkv_ref, dw_ref, dg_ref):
        dkvb = dkv_ref[...].astype(BF16)
        dw_ref[...] = _dot_tn(mn_ref[...], dkvb)
        dmn = _dot_nt(dkvb, w_ref[...])
        _, xh, _ = _rms(mem_ref[...], g_ref[...])
        dg_ref[...] = jnp.sum(dmn * xh, axis=0, keepdims=True)

    return pl.pallas_call(
        body, name="mem_bwd",
        out_shape=[jax.ShapeDtypeStruct((D_MODEL, 1024), F32), jax.ShapeDtypeStruct((1, D_MODEL), F32)],
    )(mem, g, w, mn, dkv)


def _xattn_heads(xqb, kv_ref, m_len):
    ps = []
    for h in range(X_HEADS):
        hs = slice(h * X_HEAD_DIM, (h + 1) * X_HEAD_DIM)
        s = _dot_nt(xqb[:, hs], kv_ref[:, hs]) * X_SCALE
        e = jnp.exp(s - jnp.max(s, axis=-1, keepdims=True))
        ps.append(e / jnp.sum(e, axis=-1, keepdims=True))
    return ps


def _xattn_fwd(x1, g, wxq, kv, wxo):
    s_len, m_len = x1.shape[0], kv.shape[0]
    tm = _row_block(s_len)

    def body(x_ref, g_ref, wq_ref, kv_ref, wo_ref, y_ref):
        hx, _, _ = _rms(x_ref[...], g_ref[...])
        xqb = _dot(hx.astype(BF16), wq_ref[...]).astype(BF16)
        ps = _xattn_heads(xqb, kv_ref, m_len)
        xo = jnp.concatenate(
            [_dot(ps[h].astype(BF16), kv_ref[:, 512 + h * X_HEAD_DIM:512 + (h + 1) * X_HEAD_DIM])
             for h in range(X_HEADS)], axis=-1)
        y_ref[...] = x_ref[...] + _dot(xo.astype(BF16), wo_ref[...])

    full = lambda shape: pl.BlockSpec(shape, lambda i: (0, 0))
    rowb = lambda n: pl.BlockSpec((tm, n), lambda i: (i, 0))
    return pl.pallas_call(
        body, name="xattn_fwd", grid=(s_len // tm,),
        in_specs=[rowb(1024), full((1, 1024)), full((1024, 512)), full((m_len, 1024)), full((512, 1024))],
        out_specs=rowb(1024),
        out_shape=jax.ShapeDtypeStruct((s_len, D_MODEL), F32),
        compiler_params=_cparams(48, dimension_semantics=("parallel",)),
    )(x1, g, wxq, kv, wxo)


def _xattn_bwd(x1, dx2, g, wxq, kv, wxo):
    s_len, m_len = x1.shape[0], kv.shape[0]
    tm = _row_block(s_len)

    def body(x_ref, dy_ref, g_ref, wq_ref, kv_ref, wo_ref, dx_ref, dwq_ref, dwo_ref, dkv_ref, dg_ref):
        first = pl.program_id(0) == 0
        gv = g_ref[...]
        hx, xh, r = _rms(x_ref[...], gv)
        hxb = hx.astype(BF16)
        xqb = _dot(hxb, wq_ref[...]).astype(BF16)
        ps = _xattn_heads(xqb, kv_ref, m_len)
        dy = dy_ref[...]
        dyb = dy.astype(BF16)
        dxo = _dot_nt(dyb, wo_ref[...])
        xos, dqs, dks, dvs = [], [], [], []
        for h in range(X_HEADS):
            hs = slice(h * X_HEAD_DIM, (h + 1) * X_HEAD_DIM)
            vs = slice(512 + h * X_HEAD_DIM, 512 + (h + 1) * X_HEAD_DIM)
            p = ps[h]
            pb = p.astype(BF16)
            dxoh = dxo[:, hs].astype(BF16)
            xos.append(_dot(pb, kv_ref[:, vs]))
            dp = _dot_nt(dxoh, kv_ref[:, vs])
            ds = (p * (dp - jnp.sum(dp * p, axis=-1, keepdims=True)) * X_SCALE).astype(BF16)
            dvs.append(_dot_tn(pb, dxoh))
            dks.append(_dot_tn(ds, xqb[:, hs]))
            dqs.append(_dot(ds, kv_ref[:, hs]))
        xob = jnp.concatenate(xos, axis=-1).astype(BF16)
        dxqb = jnp.concatenate(dqs, axis=-1).astype(BF16)
        _acc(dwo_ref, _dot_tn(xob, dyb), first)
        _acc(dwq_ref, _dot_tn(hxb, dxqb), first)
        _acc(dkv_ref, jnp.concatenate(dks + dvs, axis=-1), first)
        dhx = _dot_nt(dxqb, wq_ref[...])
        dx, dgr = _rms_bwd(dhx, xh, r, gv)
        dx_ref[...] = dy + dx
        _acc_rows(dg_ref, dgr, first)

    full = lambda shape: pl.BlockSpec(shape, lambda i: (0, 0))
    rowb = lambda n: pl.BlockSpec((tm, n), lambda i: (i, 0))
    return pl.pallas_call(
        body, name="xattn_bwd", grid=(s_len // tm,),
        in_specs=[rowb(1024), rowb(1024), full((1, 1024)), full((1024, 512)), full((m_len, 1024)),
                  full((512, 1024))],
        out_specs=[rowb(1024), full((1024, 512)), full((512, 1024)), full((m_len, 1024)), full((1, 1024))],
        out_shape=[jax.ShapeDtypeStruct((s_len, D_MODEL), F32), jax.ShapeDtypeStruct((1024, 512), F32),
                   jax.ShapeDtypeStruct((512, 1024), F32), jax.ShapeDtypeStruct((m_len, 1024), F32),
                   jax.ShapeDtypeStruct((1, D_MODEL), F32)],
        compiler_params=_cparams(48, dimension_semantics=("arbitrary",)),
    )(x1, dx2, g, wxq, kv, wxo)


FF_TILE = 1408


def _ffn_fwd(x2, g, wg, wu, wd):
    s_len = x2.shape[0]
    tm, tf = _row_block(s_len), FF_TILE

    def body(x_ref, g_ref, wg_ref, wu_ref, wd_ref, y_ref, h_scr):
        j = pl.program_id(1)

        @pl.when(j == 0)
        def _():
            hf, _, _ = _rms(x_ref[...], g_ref[...])
            h_scr[...] = hf.astype(BF16)
            y_ref[...] = x_ref[...]

        hb = h_scr[...]
        gt = _dot(hb, wg_ref[...])
        up = _dot(hb, wu_ref[...])
        act = gt * _sigmoid(gt) * up
        y_ref[...] += _dot(act.astype(BF16), wd_ref[...])

    return pl.pallas_call(
        body, name="ffn_fwd", grid=(s_len // tm, D_FF // tf),
        in_specs=[pl.BlockSpec((tm, D_MODEL), lambda i, j: (i, 0)),
                  pl.BlockSpec((1, D_MODEL), lambda i, j: (0, 0)),
                  pl.BlockSpec((D_MODEL, tf), lambda i, j: (0, j)),
                  pl.BlockSpec((D_MODEL, tf), lambda i, j: (0, j)),
                  pl.BlockSpec((tf, D_MODEL), lambda i, j: (j, 0))],
        out_specs=pl.BlockSpec((tm, D_MODEL), lambda i, j: (i, 0)),
        out_shape=jax.ShapeDtypeStruct((s_len, D_MODEL), F32),
        scratch_shapes=[pltpu.VMEM((tm, D_MODEL), BF16)],
        compiler_params=_cparams(48, dimension_semantics=("parallel", "arbitrary")),
    )(x2, g, wg, wu, wd)


def _ffn_bwd(x2, dx3, g, wg, wu, wd):
    s_len = x2.shape[0]
    tm, tf = min(s_len, 256), FF_TILE
    nf = D_FF // tf

    def act_body(x_ref, dy_ref, g_ref, wg_ref, wu_ref, wd_ref, h_ref, dgt_ref, dup_ref, act_ref, h_scr, dyb_scr):
        @pl.when(pl.program_id(1) == 0)
        def _():
            hf, _, _ = _rms(x_ref[...], g_ref[...])
            hb = hf.astype(BF16)
            h_scr[...] = hb
            h_ref[...] = hb
            dyb_scr[...] = dy_ref[...].astype(BF16)

        hb = h_scr[...]
        gt = _dot(hb, wg_ref[...])
        up = _dot(hb, wu_ref[...])
        sg = _sigmoid(gt)
        silu = gt * sg
        dact = _dot_nt(dyb_scr[...], wd_ref[...])
        dgt_ref[...] = (dact * up * (sg * (1.0 + gt * (1.0 - sg)))).astype(BF16)
        dup_ref[...] = (dact * silu).astype(BF16)
        act_ref[...] = (silu * up).astype(BF16)

    rowb = pl.BlockSpec((tm, D_MODEL), lambda i, j: (i, 0))
    ffb = pl.BlockSpec((tm, tf), lambda i, j: (i, j))
    hf, dgt, dup, act = pl.pallas_call(
        act_body, name="ffn_bwd_act", grid=(s_len // tm, nf),
        in_specs=[rowb, rowb, pl.BlockSpec((1, D_MODEL), lambda i, j: (0, 0)),
                  pl.BlockSpec((D_MODEL, tf), lambda i, j: (0, j)),
                  pl.BlockSpec((D_MODEL, tf), lambda i, j: (0, j)),
                  pl.BlockSpec((tf, D_MODEL), lambda i, j: (j, 0))],
        out_specs=[rowb, ffb, ffb, ffb],
        out_shape=[jax.ShapeDtypeStruct((s_len, D_MODEL), BF16), jax.ShapeDtypeStruct((s_len, D_FF), BF16),
                   jax.ShapeDtypeStruct((s_len, D_FF), BF16), jax.ShapeDtypeStruct((s_len, D_FF), BF16)],
        scratch_shapes=[pltpu.VMEM((tm, D_MODEL), BF16), pltpu.VMEM((tm, D_MODEL), BF16)],
        compiler_params=_cparams(56, dimension_semantics=("parallel", "arbitrary")),
    )(x2, dx3, g, wg, wu, wd)

    def in_body(x_ref, dy_ref, g_ref, wg_ref, wu_ref, dgt_ref, dup_ref, dx_ref, dg_ref):
        dh = _dot_nt(dgt_ref[...], wg_ref[...]) + _dot_nt(dup_ref[...], wu_ref[...])
        gv = g_ref[...]
        _, xh, r = _rms(x_ref[...], gv)
        dx, dgr = _rms_bwd(dh, xh, r, gv)
        dx_ref[...] = dy_ref[...] + dx
        _acc_rows(dg_ref, dgr, pl.program_id(0) == 0)

    row1 = lambda n: pl.BlockSpec((tm, n), lambda i: (i, 0))
    full = lambda shape: pl.BlockSpec(shape, lambda i: (0, 0))
    dx2, dg = pl.pallas_call(
        in_body, name="ffn_bwd_in", grid=(s_len // tm,),
        in_specs=[row1(D_MODEL), row1(D_MODEL), full((1, D_MODEL)), full((D_MODEL, D_FF)), full((D_MODEL, D_FF)),
                  row1(D_FF), row1(D_FF)],
        out_specs=[row1(D_MODEL), full((1, D_MODEL))],
        out_shape=[jax.ShapeDtypeStruct((s_len, D_MODEL), F32), jax.ShapeDtypeStruct((1, D_MODEL), F32)],
        compiler_params=_cparams(48, dimension_semantics=("arbitrary",)),
    )(x2, dx3, g, wg, wu, dgt, dup)
    return dx2, hf, dgt, dup, act, dg


def _loss_head(x3, g, target):
    s_len = x3.shape[0]
    tm = _row_block(s_len)

    def body(x_ref, g_ref, t_ref, sse_ref, dx_ref, dxb_ref, dg_ref):
        first = pl.program_id(0) == 0
        gv = g_ref[...]
        y, xh, r = _rms(x_ref[...], gv)
        err = y - t_ref[...]
        _acc(sse_ref, jnp.broadcast_to(jnp.sum(err * err), (8, LANES)), first)
        dx, dgr = _rms_bwd(err * (1.0 / D_MODEL), xh, r, gv)
        dx_ref[...] = dx
        dxb_ref[...] = dx.astype(BF16)
        _acc_rows(dg_ref, dgr, first)

    rowb = pl.BlockSpec((tm, D_MODEL), lambda i: (i, 0))
    return pl.pallas_call(
        body, name="loss_head", grid=(s_len // tm,),
        in_specs=[rowb, pl.BlockSpec((1, D_MODEL), lambda i: (0, 0)), rowb],
        out_specs=[pl.BlockSpec((8, LANES), lambda i: (0, 0)), rowb, rowb,
                   pl.BlockSpec((1, D_MODEL), lambda i: (0, 0))],
        out_shape=[jax.ShapeDtypeStruct((8, LANES), F32), jax.ShapeDtypeStruct((s_len, D_MODEL), F32),
                   jax.ShapeDtypeStruct((s_len, D_MODEL), BF16), jax.ShapeDtypeStruct((1, D_MODEL), F32)],
        compiler_params=_cparams(dimension_semantics=("arbitrary",)),
    )(x3, g, target)


def _mla_prep_bwd(lat, g_q, g_kv, w_uq, w_uk, w_uv, cosf, sinf, dq, dk, dv):
    s_len = lat.shape[0]
    tm = _row_block(s_len)

    def body(lat_ref, gq_ref, gkv_ref, wuq_ref, wuk_ref, wuv_ref, cos_ref, sin_ref, dq_ref, dk_ref, dv_ref,
             dlat_ref, dqb_ref, dkb_ref, dvb_ref, dgq_ref, dgkv_ref):
        first = pl.program_id(0) == 0
        lane = lax.broadcasted_iota(jnp.int32, (tm, LANES), 1)
        cosv, sinv = cos_ref[...], sin_ref[...]
        gq, gkv = gq_ref[...], gkv_ref[...]
        _, qxh, qr = _rms(lat_ref[:, 0:256], gq)
        _, kxh, kr_ = _rms(lat_ref[:, 256:384], gkv)
        dkr = jnp.zeros((tm, LANES), F32)
        for h in range(MLA_HEADS):
            sl = slice(h * HEAD_PAD, (h + 1) * HEAD_PAD)
            blk = dq_ref[:, sl]
            dqb_ref[:, sl] = (blk * cosv + _rope_rot_t(blk, lane) * sinv).astype(BF16)
            kblk = dk_ref[:, sl] * (1.0 / MLA_Q_FOLD)
            dkb_ref[:, sl] = kblk.astype(BF16)
            dkr = dkr + kblk
        dvb = dv_ref[...].astype(BF16)
        dvb_ref[...] = dvb
        dkr = jnp.where((lane >= 64) & (lane < 96), dkr, 0.0)
        dkr = dkr * cosv + _rope_rot_t(dkr, lane) * sinv
        dql = _dot_nt(dqb_ref[...], wuq_ref[...])
        dkvl = _dot_nt(dkb_ref[...], wuk_ref[...]) + _dot_nt(dvb, wuv_ref[...])
        dcq, dgqr = _rms_bwd(dql, qxh, qr, gq)
        dckv, dgkvr = _rms_bwd(dkvl, kxh, kr_, gkv)
        dlat_ref[:, 0:256] = dcq
        dlat_ref[:, 256:384] = dckv
        dlat_ref[:, K_R_OFF:K_R_OFF + LANES] = pltpu.roll(dkr, 64, 1)
        _acc_rows(dgq_ref, dgqr, first)
        _acc_rows(dgkv_ref, dgkvr, first)

    full = lambda shape: pl.BlockSpec(shape, lambda i: (0, 0))
    rowb = lambda n: pl.BlockSpec((tm, n), lambda i: (i, 0))
    sds = lambda n, dt: jax.ShapeDtypeStruct((s_len, n), dt)
    return pl.pallas_call(
        body, name="mla_prep_bwd", grid=(s_len // tm,),
        in_specs=[rowb(512), full((1, 256)), full((1, 128)), full((256, 1024)), full((128, 1024)),
                  full((128, 512)), rowb(128), rowb(128), rowb(1024), rowb(1024), rowb(512)],
        out_specs=[rowb(512), rowb(1024), rowb(1024), rowb(512), full((1, 256)), full((1, 128))],
        out_shape=[sds(512, F32), sds(1024, BF16), sds(1024, BF16), sds(512, BF16),
                   jax.ShapeDtypeStruct((1, 256), F32), jax.ShapeDtypeStruct((1, 128), F32)],
        compiler_params=_cparams(48, dimension_semantics=("arbitrary",)),
    )(lat, g_q, g_kv, w_uq, w_uk, w_uv, cosf, sinf, dq, dk, dv)


def _in_proj_bwd(x, g, w, dx1, dlat, dsbq, dsbk, dsbv, dgates, ride):
    s_len = x.shape[0]
    tm = min(s_len, 256)
    nb = s_len // tm

    def body(x_ref, g_ref, w_ref, dx1_ref, dlat_ref, dq_ref, dk_ref, dv_ref, dgate_ref, *rest):
        gx_ref, dg_ref = rest[ride.n:ride.n + 2]
        dproj = rest[-1]
        ride.bind(rest[:ride.n], rest[ride.n + 2:2 * ride.n + 2], rest[2 * ride.n + 2:-1])
        pl.when(pl.program_id(0) == 0)(ride.issue)
        dproj[:, 0:512] = dlat_ref[...].astype(BF16)
        dproj[:, 512:1024] = dq_ref[...].astype(BF16)
        dproj[:, 1024:1536] = (dk_ref[...] * LN2).astype(BF16)
        dproj[:, 1536:2048] = dv_ref[...].astype(BF16)
        dproj[:, 2048:4096] = dgate_ref[...]
        dh = _dot_nt(dproj[...], w_ref[...])
        gv = g_ref[...]
        _, xh, r = _rms(x_ref[...], gv)
        dx, dgr = _rms_bwd(dh, xh, r, gv)
        gx_ref[...] = dx1_ref[...] + dx
        _acc_rows(dg_ref, dgr, pl.program_id(0) == 0)
        pl.when(pl.program_id(0) == nb - 1)(ride.finish)

    rowb = lambda n: pl.BlockSpec((tm, n), lambda i: (i, 0))
    full = lambda shape: pl.BlockSpec(shape, lambda i: (0, 0))
    outs = pl.pallas_call(
        body, name="in_proj_bwd", grid=(nb,),
        in_specs=[rowb(D_MODEL), full((1, D_MODEL)), full((D_MODEL, D_IN_PAD)), rowb(D_MODEL),
                  rowb(512), rowb(512), rowb(512), rowb(512), rowb(2 * D_MODEL)] + ride.specs,
        out_specs=[rowb(D_MODEL), full((1, D_MODEL))] + ride.specs,
        out_shape=[jax.ShapeDtypeStruct((s_len, D_MODEL), F32), jax.ShapeDtypeStruct((1, D_MODEL), F32)]
        + ride.out_shape,
        scratch_shapes=ride.scratch + [pltpu.VMEM((tm, D_IN_PAD), BF16)],
        compiler_params=_cparams(48, dimension_semantics=("arbitrary",)),
    )(x, g, w, dx1, dlat, dsbq, dsbk, dsbv, dgates, *ride.srcs)
    return outs[0], outs[1], outs[2:]


def _adamw(landed, w, m, v, name):
    r, c = w.shape
    lanes = _round_up(c, LANES)
    tb = r
    for cand in range(r, 0, -1):
        if r % cand == 0 and (cand % 8 == 0 or cand == r) and N_DEV * cand * lanes * 4 <= ADAM_BLOCK_BYTES:
            tb = cand
            break
    c1 = 1.0 - ADAM_B1 ** ADAM_STEP
    c2 = 1.0 - ADAM_B2 ** ADAM_STEP

    def body(l_ref, w_ref, m_ref, v_ref, g_ref, d_ref, nm_ref, nv_ref):
        g = l_ref[0]
        for k in range(1, N_DEV):
            g = g + l_ref[k]
        nm = ADAM_B1 * m_ref[...] + (1.0 - ADAM_B1) * g
        nv = ADAM_B2 * v_ref[...] + (1.0 - ADAM_B2) * (g * g)
        g_ref[...] = g
        nm_ref[...] = nm
        nv_ref[...] = nv
        d_ref[...] = -ADAM_LR * ((nm / c1) / (jnp.sqrt(nv / c2) + ADAM_EPS) + ADAM_WD * w_ref[...])

    blk = pl.BlockSpec((tb, c), lambda i: (i, 0))
    return pl.pallas_call(
        body, name=name, grid=(r // tb,),
        in_specs=[pl.BlockSpec((N_DEV, tb, c), lambda i: (0, i, 0)), blk, blk, blk],
        out_specs=[blk, blk, blk, blk],
        out_shape=[jax.ShapeDtypeStruct((r, c), F32)] * 4,
        compiler_params=_cparams(dimension_semantics=("parallel",)),
    )(landed, w, m, v)


def _shard_shape(shape, axis):
    return tuple(d // N_DEV if a == axis else d for a, d in enumerate(shape))


def _split_pieces(full, axis):
    r, c = full.shape
    if axis == 0:
        return full.reshape(N_DEV, r // N_DEV, c)
    return full.reshape(r, N_DEV, c // N_DEV).transpose(1, 0, 2)


def _join_shards(gathered, axis):
    _, r, c = gathered.shape
    if axis == 0:
        return gathered.reshape(N_DEV * r, c)
    return gathered.transpose(1, 0, 2).reshape(r, N_DEV * c)


def kernel(x, mem, positions, g_mix, w_in, b_gate, g_q_lat, w_uq, g_kv_lat, w_ukv, w_a_proj, w_b_proj, w_o, g_x, g_mem, w_xq, w_xkv, w_xo, g_ffn, w_gate, w_up, w_down, g_final, loss_target, m_g_mix, m_w_in, m_b_gate, m_g_q_lat, m_w_uq, m_g_kv_lat, m_w_ukv, m_w_a_proj, m_w_b_proj, m_w_o, m_g_x, m_g_mem, m_w_xq, m_w_xkv, m_w_xo, m_g_ffn, m_w_gate, m_w_up, m_w_down, m_g_final, v_g_mix, v_w_in, v_b_gate, v_g_q_lat, v_w_uq, v_g_kv_lat, v_w_ukv, v_w_a_proj, v_w_b_proj, v_w_o, v_g_x, v_g_mem, v_w_xq, v_w_xkv, v_w_xo, v_g_ffn, v_w_gate, v_w_up, v_w_down, v_g_final):
    given = dict(locals())
    s_len = x.shape[1]
    x2d = x.reshape(s_len, D_MODEL)
    mem2d = mem.reshape(-1, D_MODEL)
    target = loss_target.reshape(s_len, D_MODEL)

    names = [name for name, _, _ in SHARDED]
    axis_of = {name: axis for name, _, axis in SHARDED}
    shard2d = lambda name, prefix="": given[prefix + name].reshape(
        _shard_shape(dict((n, s) for n, s, _ in SHARDED)[name], axis_of[name]))

    wire = lambda name: shard2d(name) if name == "b_gate" else shard2d(name).astype(BF16)
    early = [n for n in names if n in NEEDED_FIRST]
    late = [n for n in names if n not in NEEDED_FIRST]
    gathered = _exchange(True, [wire(n) for n in early], "weights_gather_first")
    wts = {n: _join_shards(g, axis_of[n]) for n, g in zip(early, gathered)}

    w_in_p = jnp.concatenate([wts["w_in"][:, :416], jnp.zeros((D_MODEL, 96), BF16), wts["w_in"][:, 416:]], axis=1)
    w_uq_p = jnp.pad(wts["w_uq"].reshape(256, MLA_HEADS, 96), ((0, 0), (0, 0), (0, 32))).reshape(256, 1024)
    ukv = wts["w_ukv"].reshape(128, MLA_HEADS, 128)
    w_uk_p = jnp.pad(ukv[:, :, :64], ((0, 0), (0, 0), (0, 64))).reshape(128, 1024)
    w_uv = ukv[:, :, 64:].reshape(128, 512)
    bg = wts["b_gate"]

    inv_freq = ROPE_THETA ** (-jnp.arange(0, MLA_ROPE, 2, dtype=F32) / MLA_ROPE)
    ang = positions.reshape(s_len).astype(F32)[:, None] * inv_freq
    cos16, sin16 = jnp.cos(ang), jnp.sin(ang)
    cosf = jnp.concatenate([jnp.ones((s_len, 64), F32), cos16, cos16, jnp.ones((s_len, 32), F32)], axis=1)
    sinf = jnp.concatenate([jnp.zeros((s_len, 64), F32), sin16, sin16, jnp.zeros((s_len, 32), F32)], axis=1)

    h1, lat, sb, gates = _in_proj(x2d, g_mix, w_in_p)
    qa, ka, va, q_lat, kv_lat = _mla_prep(lat, g_q_lat, g_kv_lat, w_uq_p, w_uk_p, w_uv, cosf, sinf)
    oa, lse, gathered = _mla_fwd(qa, ka, va, _Exchange(True, [wire(n) for n in late]))
    wts.update({n: _join_shards(g, axis_of[n]) for n, g in zip(late, gathered)})
    ob, sb_r = _sb_fwd(sb)
    x1 = _merge_fwd(x2d, oa, ob, gates, bg, wts["w_a_proj"], wts["w_b_proj"], wts["w_o"])
    mn, xkv = _mem_kv(mem2d, g_mem, wts["w_xkv"])
    x2 = _xattn_fwd(x1, g_x, wts["w_xq"], xkv, wts["w_xo"])
    x3 = _ffn_fwd(x2, g_ffn, wts["w_gate"], wts["w_up"], wts["w_down"])
    g_final2d = g_final.reshape(1, D_MODEL)
    sse, dx3, dx3b, dg_final = _loss_head(x3, g_final2d, target)
    loss = lax.psum(sse[0, 0] * (0.5 / D_MODEL), ("x", "y", "c"))

    dx2, hf, dgt, dup, act, dg_ffn = _ffn_bwd(x2, dx3, g_ffn, wts["w_gate"], wts["w_up"], wts["w_down"])
    dx1, dw_xq, dw_xo, dxkv, dg_x = _xattn_bwd(x1, dx2, g_x, wts["w_xq"], xkv, wts["w_xo"])
    dw_xkv, dg_mem = _mem_bwd(mem2d, g_mem, wts["w_xkv"], mn, dxkv)
    doa, dob, dgates, dpa, dpb, merged, dx1b, dbg = _merge_bwd(
        dx1, oa, ob, gates, bg, wts["w_a_proj"], wts["w_b_proj"], wts["w_o"])
    dsbq, dsbk, dsbv = _sb_bwd(sb, dob, sb_r)
    full_grads = {
        "w_a_proj": _tn_matmul(oa, dpa, "dw_a"),
        "w_b_proj": _tn_matmul(ob, dpb, "dw_b"),
        "w_o": _tn_matmul(merged, dx1b, "dw_o"),
        "w_xq": dw_xq,
        "w_xkv": dw_xkv,
        "w_xo": dw_xo,
        "w_gate": _tn_matmul(hf, dgt, "dw_gate", tn=FF_TILE),
        "w_up": _tn_matmul(hf, dup, "dw_up", tn=FF_TILE),
        "w_down": _tn_matmul(act, dx3b, "dw_down", tka=FF_TILE),
    }
    dqa, dka, dva, got = _mla_bwd(
        qa, ka, va, oa, doa, lse, _Exchange(False, [_split_pieces(full_grads[n], axis_of[n]) for n in late]))
    landed = dict(zip(late, got))
    dlat, dqb, dkb, dvb, dg_q, dg_kv = _mla_prep_bwd(
        lat, g_q_lat, g_kv_lat, w_uq_p, w_uk_p, w_uv, cosf, sinf, dqa, dka, dva)
    dw_in = jnp.concatenate([
        _tn_matmul(h1, dlat, "dw_in_lat")[:, :416],
        _tn_matmul(h1, dsbq, "dw_in_sbq"),
        _tn_matmul(h1, dsbk, "dw_in_sbk", scale=LN2),
        _tn_matmul(h1, dsbv, "dw_in_sbv"),
        _tn_matmul(h1, dgates, "dw_in_gates")], axis=1)
    dw_uq_p = _tn_matmul(q_lat, dqb, "dw_uq")
    dw_uk_p = _tn_matmul(kv_lat, dkb, "dw_uk")
    dw_uv = _tn_matmul(kv_lat, dvb, "dw_uv")
    full_grads.update({
        "w_in": dw_in,
        "b_gate": dbg,
        "w_uq": dw_uq_p.reshape(256, MLA_HEADS, 128)[:, :, :96].reshape(256, 768),
        "w_ukv": jnp.concatenate([dw_uk_p.reshape(128, MLA_HEADS, 128)[:, :, :64],
                                  dw_uv.reshape(128, MLA_HEADS, 64)], axis=2).reshape(128, 1024),
    })
    grad_x, dg_mix, got = _in_proj_bwd(
        x2d, g_mix, w_in_p, dx1, dlat, dsbq, dsbk, dsbv, dgates,
        _Exchange(False, [_split_pieces(full_grads[n], axis_of[n]) for n in early]))
    landed.update(zip(early, got))
    rep_grads = {"g_mix": dg_mix, "g_q_lat": dg_q, "g_kv_lat": dg_kv, "g_x": dg_x, "g_mem": dg_mem,
                 "g_ffn": dg_ffn, "g_final": dg_final}
    rep_cat = lambda prefix, src: jnp.concatenate(
        [src[prefix + n].reshape(-1) for n, _ in REPLICATED]).reshape(-1, LANES)
    rep_src = jnp.broadcast_to(rep_cat("", rep_grads), (N_DEV,) + rep_cat("", rep_grads).shape)
    rep_landed = _exchange(False, [rep_src], "grads_gains")[0]

    res = {}
    for name, _, _ in SHARDED:
        outs = _adamw(landed[name], shard2d(name), shard2d(name, "m_"), shard2d(name, "v_"), "adamw_" + name)
        res[name] = [o.reshape(given[name].shape) for o in outs]
    rep_outs = _adamw(rep_landed, rep_cat("", given), rep_cat("m_", given), rep_cat("v_", given), "adamw_gains")
    off = 0
    for name, n in REPLICATED:
        res[name] = [o.reshape(-1)[off:off + n].reshape(given[name].shape) for o in rep_outs]
        off += n
    result = [loss, grad_x.reshape(x.shape)]
    for k in range(4):
        result.extend(res[name][k] for name in WEIGHT_ORDER)
    return tuple(result)
```

```python
import functools
import math

import jax
import jax.numpy as jnp
from jax import lax
from jax.experimental import pallas as pl
from jax.experimental.pallas import tpu as pltpu

F32 = jnp.float32
BF16 = jnp.bfloat16

D_MODEL = 1024
MLA_HEADS = 8
MLA_Q_RANK = 256
MLA_KV_RANK = 128
MLA_NOPE = 64
MLA_ROPE = 32
ROPE_THETA = 10000.0
SB_WIDTH = 512
X_HEADS = 4
X_HEAD_DIM = 128
D_FF = 2816
EPS = 1e-6
D_IN = 4000
D_IN_PAD = 4096
K_R_OFF = 384
LANES = 128
HEAD_PAD = 128
MLA_SCALE = 1.0 / math.sqrt(MLA_NOPE + MLA_ROPE)
SB_SCALE = 0.125
LOG2E = math.log2(math.e)
LN2 = math.log(2.0)
MLA_Q_FOLD = MLA_SCALE * LOG2E
SB_Q_FOLD = SB_SCALE * LOG2E
SB_CUT = -160.0
X_SCALE = 1.0 / math.sqrt(X_HEAD_DIM)
NEG_BIG = -1e30

ADAM_LR = 0.001
ADAM_B1 = 0.9
ADAM_B2 = 0.999
ADAM_EPS = 1e-08
ADAM_WD = 0.01
ADAM_STEP = 10

N_DEV = 8
MIB = 1024 * 1024
ADAM_BLOCK_BYTES = 4 * MIB

SHARDED = (
    ("w_in", (D_MODEL, D_IN), 1),
    ("b_gate", (2, D_MODEL), 1),
    ("w_uq", (MLA_Q_RANK, 768), 1),
    ("w_ukv", (MLA_KV_RANK, 1024), 1),
    ("w_a_proj", (512, D_MODEL), 1),
    ("w_b_proj", (512, D_MODEL), 1),
    ("w_o", (D_MODEL, D_MODEL), 0),
    ("w_xq", (D_MODEL, 512), 0),
    ("w_xkv", (D_MODEL, 1024), 0),
    ("w_xo", (512, D_MODEL), 1),
    ("w_gate", (D_MODEL, D_FF), 1),
    ("w_up", (D_MODEL, D_FF), 1),
    ("w_down", (D_FF, D_MODEL), 0),
)
NEEDED_FIRST = ("w_in", "b_gate", "w_uq", "w_ukv")
REPLICATED = (
    ("g_mix", 1024), ("g_q_lat", 256), ("g_kv_lat", 128), ("g_x", 1024),
    ("g_mem", 1024), ("g_ffn", 1024), ("g_final", 1024),
)
WEIGHT_ORDER = ("g_mix", "w_in", "b_gate", "g_q_lat", "w_uq", "g_kv_lat", "w_ukv", "w_a_proj",
                "w_b_proj", "w_o", "g_x", "g_mem", "w_xq", "w_xkv", "w_xo", "g_ffn", "w_gate",
                "w_up", "w_down", "g_final")


def _round_up(n, m):
    return -(-n // m) * m


def _cparams(vmem_mib=None, **kw):
    if vmem_mib is not None:
        kw["vmem_limit_bytes"] = vmem_mib * MIB
    return pltpu.CompilerParams(**kw)


def _dot(a, b):
    return jnp.dot(a, b, preferred_element_type=F32)


def _dot_nt(a, b):
    return lax.dot_general(a, b, (((1,), (1,)), ((), ())), preferred_element_type=F32)


def _dot_tn(a, b):
    return lax.dot_general(a, b, (((0,), (0,)), ((), ())), preferred_element_type=F32)


def _rms(x, g):
    r = lax.rsqrt(jnp.mean(x * x, axis=-1, keepdims=True) + EPS)
    xh = x * r
    return xh * g, xh, r


def _rms_bwd(dy, xh, r, g):
    u = dy * g
    dx = r * (u - xh * jnp.mean(u * xh, axis=-1, keepdims=True))
    return dx, dy * xh


def _sigmoid(z):
    return 1.0 / (1.0 + jnp.exp(-z))


def _acc_rows(ref, val, first):
    s = jnp.sum(val, axis=0, keepdims=True)

    @pl.when(first)
    def _():
        ref[...] = s

    @pl.when(jnp.logical_not(first))
    def _():
        ref[...] += s


def _acc(ref, val, first):
    @pl.when(first)
    def _():
        ref[...] = val

    @pl.when(jnp.logical_not(first))
    def _():
        ref[...] += val


def _peer(k):
    x, y, c = lax.axis_index("x"), lax.axis_index("y"), lax.axis_index("c")
    px = 1 - x if (k >> 2) & 1 else x
    py = 1 - y if (k >> 1) & 1 else y
    pc = 1 - c if k & 1 else c
    return (px, py, pc), 4 * px + 2 * py + pc


N_PEERS = N_DEV - 1
OTHER_CHIPS = (2, 4, 6)


def _land_shape(gather, src):
    return (N_DEV,) + src.shape if gather else src.shape


class _Exchange:
    def __init__(self, gather, srcs):
        self.gather, self.n, self.srcs = gather, len(srcs), list(srcs)
        self.out_shape = [jax.ShapeDtypeStruct(_land_shape(gather, s), s.dtype) for s in srcs]
        self.specs = [pl.BlockSpec(memory_space=pl.ANY)] * self.n
        self.scratch = [pltpu.SemaphoreType.DMA((self.n * N_PEERS,)), pltpu.SemaphoreType.DMA((self.n * N_PEERS,)),
                        pltpu.SemaphoreType.DMA((self.n,))]

    def bind(self, src, land, sems):
        self.src, self.land = src, land
        self.send_sems, self.recv_sems, self.local_sems = sems

    def _copy(self, a, k, source, to, target=1):
        return pltpu.make_async_remote_copy(
            src_ref=source, dst_ref=to,
            send_sem=self.send_sems.at[a * N_PEERS + k - 1], recv_sem=self.recv_sems.at[a * N_PEERS + k - 1],
            device_id=_peer(target)[0], device_id_type=pl.DeviceIdType.MESH)

    def _row(self, a, k):
        return self.land[a].at[_peer(k)[1]]

    def _mine(self, a):
        me = _peer(0)[1]
        return pltpu.make_async_copy(self.src[a] if self.gather else self.src[a].at[me], self.land[a].at[me],
                                     self.local_sems.at[a])

    def issue(self):
        me = _peer(0)[1]
        for a in range(self.n):
            self._mine(a).start()
            for k in ((1,) + OTHER_CHIPS if self.gather else range(1, N_DEV)):
                source = self.src[a] if self.gather else self.src[a].at[_peer(k)[1]]
                self._copy(a, k, source, self.land[a].at[me], target=k).start()

    def finish(self):
        me = _peer(0)[1]
        part = lambda a: self.src[a] if self.gather else self.src[a].at[me]
        if self.gather:
            for a in range(self.n):
                for k in OTHER_CHIPS:
                    self._copy(a, k, part(a), self._row(a, k)).wait_recv()
                    self._copy(a, k + 1, self._row(a, k), self._row(a, k), target=1).start()
        for a in range(self.n):
            for k in ((1, 3, 5, 7) if self.gather else range(1, N_DEV)):
                self._copy(a, k, part(a), self._row(a, k)).wait_recv()
        for a in range(self.n):
            for k in range(1, N_DEV):
                self._copy(a, k, part(a), self.land[a].at[me]).wait_send()
            self._mine(a).wait()


def _exchange(gather, srcs, name):
    ex = _Exchange(gather, srcs)

    def body(*refs):
        ex.bind(refs[:ex.n], refs[ex.n:2 * ex.n], refs[2 * ex.n:])
        ex.issue()
        ex.finish()

    return pl.pallas_call(body, name=name, out_shape=ex.out_shape, in_specs=ex.specs, out_specs=ex.specs,
                          scratch_shapes=ex.scratch)(*ex.srcs)


def _tn_matmul(a, b, name, tka=512, tn=1024, ts=2048, scale=None):
    s_len, ka = a.shape
    n = b.shape[1]
    tka, tn, ts = min(tka, ka), min(tn, n), min(ts, s_len)
    assert ka % tka == 0 and n % tn == 0 and s_len % ts == 0

    def body(a_ref, b_ref, o_ref):
        bv = b_ref[...]
        if scale is not None:
            bv = bv * scale
        _acc(o_ref, _dot_tn(a_ref[...], bv.astype(BF16)), pl.program_id(2) == 0)

    return pl.pallas_call(
        body, name=name, grid=(ka // tka, n // tn, s_len // ts),
        in_specs=[pl.BlockSpec((ts, tka), lambda i, j, s: (s, i)),
                  pl.BlockSpec((ts, tn), lambda i, j, s: (s, j))],
        out_specs=pl.BlockSpec((tka, tn), lambda i, j, s: (i, j)),
        out_shape=jax.ShapeDtypeStruct((ka, n), F32),
        compiler_params=_cparams(dimension_semantics=("parallel", "parallel", "arbitrary")),
    )(a, b)


def _row_block(s_len):
    return min(s_len, 512)


def _in_proj(x, g, w):
    s_len = x.shape[0]
    tm = _row_block(s_len)

    def body(x_ref, g_ref, w_ref, h_ref, lat_ref, sb_ref, gate_ref):
        h, _, _ = _rms(x_ref[...], g_ref[...])
        hb = h.astype(BF16)
        h_ref[...] = hb
        p = _dot(hb, w_ref[:, 0:1024])
        lat_ref[...] = p[:, 0:512]
        sb_ref[:, 0:512] = (p[:, 512:1024] * SB_Q_FOLD).astype(BF16)
        sb_ref[:, 512:1536] = _dot(hb, w_ref[:, 1024:2048]).astype(BF16)
        gate_ref[:, 0:1024] = _dot(hb, w_ref[:, 2048:3072])
        gate_ref[:, 1024:2048] = _dot(hb, w_ref[:, 3072:4096])

    rowb = lambda n: pl.BlockSpec((tm, n), lambda i: (i, 0))
    return pl.pallas_call(
        body, name="in_proj", grid=(s_len // tm,),
        in_specs=[rowb(D_MODEL), pl.BlockSpec((1, D_MODEL), lambda i: (0, 0)),
                  pl.BlockSpec((D_MODEL, D_IN_PAD), lambda i: (0, 0))],
        out_specs=[rowb(D_MODEL), rowb(512), rowb(3 * SB_WIDTH), rowb(2 * D_MODEL)],
        out_shape=[jax.ShapeDtypeStruct((s_len, D_MODEL), BF16),
                   jax.ShapeDtypeStruct((s_len, 512), F32),
                   jax.ShapeDtypeStruct((s_len, 3 * SB_WIDTH), BF16),
                   jax.ShapeDtypeStruct((s_len, 2 * D_MODEL), F32)],
        compiler_params=_cparams(48, dimension_semantics=("parallel",)),
    )(x, g, w)


def _rope_rot(blk, lane):
    return jnp.where(lane < 80, -pltpu.roll(blk, 112, 1), pltpu.roll(blk, 16, 1))


def _rope_rot_t(blk, lane):
    return jnp.where(lane < 80, pltpu.roll(blk, 112, 1), -pltpu.roll(blk, 16, 1))


def _mla_prep(lat, g_q, g_kv, w_uq, w_uk, w_uv, cosf, sinf):
    s_len = lat.shape[0]
    tm = _row_block(s_len)

    def body(lat_ref, gq_ref, gkv_ref, wuq_ref, wuk_ref, wuv_ref, cos_ref, sin_ref,
             q_ref, k_ref, v_ref, ql_ref, kvl_ref):
        lane = lax.broadcasted_iota(jnp.int32, (tm, LANES), 1)
        cosv, sinv = cos_ref[...], sin_ref[...]
        ql, _, _ = _rms(lat_ref[:, 0:256], gq_ref[...])
        kvl, _, _ = _rms(lat_ref[:, 256:384], gkv_ref[...])
        qlb, kvlb = ql.astype(BF16), kvl.astype(BF16)
        ql_ref[...] = qlb
        kvl_ref[...] = kvlb
        q = _dot(qlb, wuq_ref[...])
        kn = _dot(kvlb, wuk_ref[...])
        v_ref[...] = _dot(kvlb, wuv_ref[...]).astype(BF16)
        kr = pltpu.roll(lat_ref[:, K_R_OFF:K_R_OFF + LANES], 64, 1)
        kr = kr * cosv + _rope_rot(kr, lane) * sinv
        for h in range(MLA_HEADS):
            sl = slice(h * HEAD_PAD, (h + 1) * HEAD_PAD)
            blk = q[:, sl]
            q_ref[:, sl] = ((blk * cosv + _rope_rot(blk, lane) * sinv) * MLA_Q_FOLD).astype(BF16)
            k_ref[:, sl] = (kn[:, sl] + kr).astype(BF16)

    full = lambda shape: pl.BlockSpec(shape, lambda i: (0, 0))
    rowb = lambda n: pl.BlockSpec((tm, n), lambda i: (i, 0))
    return pl.pallas_call(
        body, name="mla_prep", grid=(s_len // tm,),
        in_specs=[rowb(512), full((1, 256)), full((1, 128)), full((256, 1024)), full((128, 1024)),
                  full((128, 512)), rowb(128), rowb(128)],
        out_specs=[rowb(1024), rowb(1024), rowb(512), rowb(256), rowb(128)],
        out_shape=[jax.ShapeDtypeStruct((s_len, 1024), BF16), jax.ShapeDtypeStruct((s_len, 1024), BF16),
                   jax.ShapeDtypeStruct((s_len, 512), BF16), jax.ShapeDtypeStruct((s_len, 256), BF16),
                   jax.ShapeDtypeStruct((s_len, 128), BF16)],
        compiler_params=_cparams(dimension_semantics=("parallel",)),
    )(lat, g_q, g_kv, w_uq, w_uk, w_uv, cosf, sinf)


ATTN_TQ = 1024
ATTN_TH = 512
ATTN_TK = 256
SB_TQ = 512
SB_TH = 256
MLA_TK = 512


def _attn_blocks(s_len, tk=ATTN_TK, tq=ATTN_TQ, th=ATTN_TH):
    tq, th, tk = min(s_len, tq), min(s_len, th), min(s_len, tk)
    return tq, th, tk, tq // tk


def _chains(tq, th):
    return [(hh, r0) for hh in range(2) for r0 in range(0, tq, th)]


def _diag_mask(th, tk, r0, sub, strict):
    lo, hi = sub * tk, (sub + 1) * tk - 1
    last, first = r0 + th - 1, r0
    if (lo >= last) if strict else (lo > last):
        return "none"
    if (hi < first) if strict else (hi <= first):
        return "all"
    row = lax.broadcasted_iota(jnp.int32, (th, tk), 0) + r0
    col = lax.broadcasted_iota(jnp.int32, (th, tk), 1) + lo
    return col < row if strict else col <= row


def _mla_fwd(q, k, v, ride):
    s_len = q.shape[0]
    tq, th, tk, nsub = _attn_blocks(s_len, MLA_TK)
    chains = _chains(tq, th)
    nh = tq // th
    nq = s_len // tq

    def body(q_ref, k_ref, v_ref, *rest):
        o_ref, lse_ref = rest[ride.n:ride.n + 2]
        ride.bind(rest[:ride.n], rest[ride.n + 2:2 * ride.n + 2], rest[2 * ride.n + 2:])
        pl.when((pl.program_id(0) == 0) & (pl.program_id(1) == 0))(ride.issue)
        i = pl.program_id(1)
        lane = lax.broadcasted_iota(jnp.int32, (th, LANES), 1)
        hsl = [slice(hh * HEAD_PAD, (hh + 1) * HEAD_PAD) for hh in range(2)]

        def step(kb, carry, sub):
            rows = pl.ds(pl.multiple_of(kb * tk, tk), tk)
            vblk = v_ref[rows, :]
            masks = ["all" if sub is None else _diag_mask(th, tk, r0, sub, strict=False) for _, r0 in chains]
            live = [n for n, m in enumerate(masks) if not (isinstance(m, str) and m == "none")]
            s = {n: _dot_nt(q_ref[chains[n][1]:chains[n][1] + th, hsl[chains[n][0]]], k_ref[rows, hsl[chains[n][0]]])
                 for n in live}
            new = list(carry)
            pb, alpha = {}, {}
            for n in live:
                m, l, _ = carry[n]
                sn = s[n]
                if not isinstance(masks[n], str):
                    sn = jnp.where(masks[n], sn, NEG_BIG)
                m_new = jnp.maximum(m, jnp.max(sn, axis=-1, keepdims=True))
                alpha[n] = jnp.exp2(m - m_new)
                p = jnp.exp2(sn - m_new)
                pb[n] = p.astype(BF16)
                new[n] = (m_new, alpha[n] * l + jnp.sum(p, axis=-1, keepdims=True), None)
            pv = {n: _dot(pb[n], vblk) for n in live}
            for n in live:
                new[n] = (new[n][0], new[n][1], alpha[n] * carry[n][2] + pv[n])
            return tuple(new)

        init = (jnp.full((th, 1), NEG_BIG, F32), jnp.zeros((th, 1), F32), jnp.zeros((th, LANES), F32))
        carry = lax.fori_loop(0, i * nsub, lambda kb, cy: step(kb, cy, None), (init,) * len(chains))
        for sub in range(nsub):
            carry = step(i * nsub + sub, carry, sub)
        for c in range(nh):
            (m0, l0, a0), (m1, l1, a1) = carry[c], carry[nh + c]
            rs = slice(c * th, (c + 1) * th)
            o_ref[rs, :] = jnp.where(lane < 64, a0 / l0, a1 / l1).astype(BF16)
            lse_ref[rs, :] = jnp.where(lane < 64, m0 + jnp.log2(l0), m1 + jnp.log2(l1))
        pl.when((pl.program_id(0) == 3) & (i == nq - 1))(ride.finish)

    outs = pl.pallas_call(
        body, name="mla_fwd", grid=(4, nq),
        in_specs=[pl.BlockSpec((tq, 2 * HEAD_PAD), lambda p, i: (i, p)),
                  pl.BlockSpec((s_len, 2 * HEAD_PAD), lambda p, i: (0, p)),
                  pl.BlockSpec((s_len, LANES), lambda p, i: (0, p))] + ride.specs,
        out_specs=[pl.BlockSpec((tq, LANES), lambda p, i: (i, p)),
                   pl.BlockSpec((None, tq, LANES), lambda p, i: (p, i, 0))] + ride.specs,
        out_shape=[jax.ShapeDtypeStruct((s_len, 512), BF16),
                   jax.ShapeDtypeStruct((4, s_len, LANES), F32)] + ride.out_shape,
        scratch_shapes=ride.scratch,
        compiler_params=_cparams(40, dimension_semantics=("arbitrary", "arbitrary")),
    )(q, k, v, *ride.srcs)
    return outs[0], outs[1], outs[2:]


def _mla_bwd(q, k, v, o, do, lse, ride):
    s_len = q.shape[0]
    tq, th, tk, nsub = _attn_blocks(s_len, MLA_TK)
    chains = _chains(tq, th)
    nq = s_len // tq

    def body(q_ref, k_ref, v_ref, o_ref, do_ref, lse_ref, *rest):
        dq_ref, dk_ref, dv_ref = rest[ride.n:ride.n + 3]
        ride.bind(rest[:ride.n], rest[ride.n + 3:2 * ride.n + 3], rest[2 * ride.n + 3:])
        pl.when((pl.program_id(0) == 0) & (pl.program_id(1) == 0))(ride.issue)
        i = pl.program_id(1)
        lane = lax.broadcasted_iota(jnp.int32, (th, LANES), 1)

        @pl.when(i == 0)
        def _():
            dk_ref[...] = jnp.zeros_like(dk_ref)
            dv_ref[...] = jnp.zeros_like(dv_ref)

        hsl = [slice(hh * HEAD_PAD, (hh + 1) * HEAD_PAD) for hh in range(2)]
        qs, dos, deltas, lses = [], [], [], []
        for hh, r0 in chains:
            rs = slice(r0, r0 + th)
            qs.append(q_ref[rs, hsl[hh]])
            doh = jnp.where((lane // 64) == hh, do_ref[rs, :], jnp.zeros((), BF16))
            dos.append(doh)
            deltas.append(jnp.sum(doh.astype(F32) * o_ref[rs, :].astype(F32), axis=-1, keepdims=True))
            lses.append(lse_ref[rs, 64 * hh:64 * hh + 1])

        def step(kb, dqs, sub):
            rows = pl.ds(pl.multiple_of(kb * tk, tk), tk)
            vblk = v_ref[rows, :]
            new, p_all, do_all = [], [], []
            ds_h, q_h = [[], []], [[], []]
            for c, (hh, r0) in enumerate(chains):
                mask = "all" if sub is None else _diag_mask(th, tk, r0, sub, strict=False)
                if isinstance(mask, str) and mask == "none":
                    new.append(dqs[c])
                    continue
                kblk = k_ref[rows, hsl[hh]]
                s = _dot_nt(qs[c], kblk)
                if not isinstance(mask, str):
                    s = jnp.where(mask, s, NEG_BIG)
                p = jnp.exp2(s - lses[c])
                dp = _dot_nt(dos[c], vblk)
                ds = (p * (dp - deltas[c]) * MLA_SCALE).astype(BF16)
                p_all.append(p.astype(BF16))
                do_all.append(dos[c])
                ds_h[hh].append(ds)
                q_h[hh].append(qs[c])
                new.append(dqs[c] + _dot(ds, kblk))
            dv_ref[rows, :] += _dot_tn(jnp.concatenate(p_all, axis=0), jnp.concatenate(do_all, axis=0))
            for hh in range(2):
                dk_ref[rows, hsl[hh]] += _dot_tn(jnp.concatenate(ds_h[hh], axis=0),
                                                 jnp.concatenate(q_h[hh], axis=0))
            return tuple(new)

        zero = jnp.zeros((th, LANES), F32)
        dqs = lax.fori_loop(0, i * nsub, lambda kb, cy: step(kb, cy, None), (zero,) * len(chains))
        for sub in range(nsub):
            dqs = step(i * nsub + sub, dqs, sub)
        for c, (hh, r0) in enumerate(chains):
            dq_ref[r0:r0 + th, hsl[hh]] = dqs[c]
        pl.when((pl.program_id(0) == 3) & (i == nq - 1))(ride.finish)

    outs = pl.pallas_call(
        body, name="mla_bwd", grid=(4, nq),
        in_specs=[pl.BlockSpec((tq, 2 * HEAD_PAD), lambda p, i: (i, p)),
                  pl.BlockSpec((s_len, 2 * HEAD_PAD), lambda p, i: (0, p)),
                  pl.BlockSpec((s_len, LANES), lambda p, i: (0, p)),
                  pl.BlockSpec((tq, LANES), lambda p, i: (i, p)),
                  pl.BlockSpec((tq, LANES), lambda p, i: (i, p)),
                  pl.BlockSpec((None, tq, LANES), lambda p, i: (p, i, 0))] + ride.specs,
        out_specs=[pl.BlockSpec((tq, 2 * HEAD_PAD), lambda p, i: (i, p)),
                   pl.BlockSpec((s_len, 2 * HEAD_PAD), lambda p, i: (0, p)),
                   pl.BlockSpec((s_len, LANES), lambda p, i: (0, p))] + ride.specs,
        out_shape=[jax.ShapeDtypeStruct((s_len, 1024), F32), jax.ShapeDtypeStruct((s_len, 1024), F32),
                   jax.ShapeDtypeStruct((s_len, 512), F32)] + ride.out_shape,
        scratch_shapes=ride.scratch,
        compiler_params=_cparams(56, dimension_semantics=("arbitrary", "arbitrary")),
    )(q, k, v, o, do, lse, *ride.srcs)
    return outs[0], outs[1], outs[2], outs[3:]


def _log_sigmoids(z2):
    sp = jnp.log2(1.0 + jnp.exp2(-jnp.abs(z2)))
    lb = jnp.minimum(z2, 0.0) - sp
    return lb, lb - z2


def _split_dot(x, w, parts, nt=False):
    dot = _dot_nt if nt else _dot
    out = None
    for _ in range(parts):
        xb = x.astype(BF16)
        t = dot(xb, w)
        out = t if out is None else out + t
        x = x - xb.astype(F32)
    return out


def _sb_fwd(sb):
    s_len = sb.shape[0]
    tq, th, tk, nsub = _attn_blocks(s_len, tq=SB_TQ, th=SB_TH)
    chains = _chains(tq, th)
    nh = tq // th
    assert s_len // tk <= 64

    def body(q_ref, k_ref, v_ref, o_ref, r_ref):
        i = pl.program_id(1)
        lane = lax.broadcasted_iota(jnp.int32, (th, LANES), 1)
        upper = (lax.broadcasted_iota(jnp.int32, (tk, tk), 0)
                 > lax.broadcasted_iota(jnp.int32, (tk, tk), 1)).astype(BF16)
        qs = [jnp.where((lane // 64) == hh, q_ref[r0:r0 + th, :], jnp.zeros((), BF16)) for hh, r0 in chains]

        def step(kb, carry, sub):
            rows = pl.ds(pl.multiple_of(kb * tk, tk), tk)
            kblk, vblk = k_ref[rows, :], v_ref[rows, :]
            masks = ["all" if sub is None else _diag_mask(th, tk, r0, sub, strict=True) for _, r0 in chains]
            live = [n for n, m in enumerate(masks) if not (isinstance(m, str) and m == "none")]
            masked = {n: not isinstance(masks[n], str) for n in live}
            z = {n: _dot_nt(qs[n], kblk) for n in live}
            lb, lom = {}, {}
            for n in live:
                lb[n], lom[n] = _log_sigmoids(z[n])
                if masked[n]:
                    lom[n] = jnp.where(masks[n], lom[n], 0.0)
            suf = {n: _split_dot(lom[n], upper, 2) for n in live}
            a = {}
            for n in live:
                a[n] = jnp.exp2(lb[n] + suf[n] + carry[n][0])
                if masked[n]:
                    a[n] = jnp.where(masks[n], a[n], 0.0)
            pv = {n: _dot(a[n].astype(BF16), vblk) for n in live}
            new = list(carry)
            for n in live:
                c, acc, r = carry[n]
                rs = suf[n][:, 0:1] + lom[n][:, 0:1]
                new[n] = (c + rs, acc + pv[n], jnp.where(lane == 64 * chains[n][0] + kb, rs, r))
            return tuple(new)

        init = (jnp.zeros((th, 1), F32), jnp.zeros((th, LANES), F32), jnp.zeros((th, LANES), F32))
        carry = (init,) * len(chains)
        for sub in reversed(range(nsub)):
            carry = step(i * nsub + sub, carry, sub)

        def spent(cy):
            top = functools.reduce(jnp.maximum, [jnp.max(c) for c, _, _ in cy])
            return (top < SB_CUT).astype(jnp.int32)

        def walk(state):
            t, _, cy = state
            cy = step(i * nsub - 1 - t, cy, None)
            return t + 1, spent(cy), cy

        _, _, carry = lax.while_loop(lambda st: (st[0] < i * nsub) & (st[1] == 0), walk,
                                     (jnp.int32(0), spent(carry), carry))
        for n in range(nh):
            rs = slice(n * th, (n + 1) * th)
            o_ref[rs, :] = jnp.where(lane < 64, carry[n][1], carry[nh + n][1]).astype(BF16)
            r_ref[rs, :] = jnp.where(lane < 64, carry[n][2], carry[nh + n][2])

    return pl.pallas_call(
        body, name="sb_fwd", grid=(4, s_len // tq),
        in_specs=[pl.BlockSpec((tq, LANES), lambda p, i: (i, p)),
                  pl.BlockSpec((s_len, LANES), lambda p, i: (0, 4 + p)),
                  pl.BlockSpec((s_len, LANES), lambda p, i: (0, 8 + p))],
        out_specs=[pl.BlockSpec((tq, LANES), lambda p, i: (i, p)),
                   pl.BlockSpec((None, tq, LANES), lambda p, i: (p, i, 0))],
        out_shape=[jax.ShapeDtypeStruct((s_len, 512), BF16), jax.ShapeDtypeStruct((4, s_len, LANES), F32)],
        compiler_params=_cparams(40, dimension_semantics=("parallel", "arbitrary")),
    )(sb, sb, sb)


def _sb_bwd(sb, do, r):
    s_len = sb.shape[0]
    tq, th, tk, nsub = _attn_blocks(s_len)
    chains = _chains(tq, th)
    nh = tq // th

    def body(q_ref, k_ref, v_ref, do_ref, r_ref, dq_ref, dk_ref, dv_ref):
        i = pl.program_id(1)
        lane = lax.broadcasted_iota(jnp.int32, (th, LANES), 1)
        upper = (lax.broadcasted_iota(jnp.int32, (tk, tk), 0)
                 > lax.broadcasted_iota(jnp.int32, (tk, tk), 1)).astype(BF16)
        tri = (lax.broadcasted_iota(jnp.int32, (LANES, LANES), 0)
               > lax.broadcasted_iota(jnp.int32, (LANES, LANES), 1)).astype(BF16)

        @pl.when(i == 0)
        def _():
            dk_ref[...] = jnp.zeros_like(dk_ref)
            dv_ref[...] = jnp.zeros_like(dv_ref)

        qs, dos, rights = [], [], []
        for hh, r0 in chains:
            rs = slice(r0, r0 + th)
            hm = (lane // 64) == hh
            qs.append(jnp.where(hm, q_ref[rs, :], jnp.zeros((), BF16)))
            dos.append(jnp.where(hm, do_ref[rs, :], jnp.zeros((), BF16)))
            rights.append(_split_dot(jnp.where(hm, r_ref[rs, :], 0.0), tri, 3))

        def step(kb, carry, sub):
            rows = pl.ds(pl.multiple_of(kb * tk, tk), tk)
            kblk, vblk = k_ref[rows, :], v_ref[rows, :]
            new, a_all, do_all, dz_all, q_all = [], [], [], [], []
            for n, ((hh, r0), (pre, dq)) in enumerate(zip(chains, carry)):
                mask = "all" if sub is None else _diag_mask(th, tk, r0, sub, strict=True)
                if isinstance(mask, str) and mask == "none":
                    new.append((pre, dq))
                    continue
                c = jnp.sum(jnp.where(lane == 64 * hh + kb, rights[n], 0.0), axis=-1, keepdims=True)
                z = _dot_nt(qs[n], kblk)
                lb, lom = _log_sigmoids(z)
                if not isinstance(mask, str):
                    lom = jnp.where(mask, lom, 0.0)
                suf = _split_dot(lom, upper, 2)
                a = jnp.exp2(lb + suf + c)
                if not isinstance(mask, str):
                    a = jnp.where(mask, a, 0.0)
                g = a * _dot_nt(dos[n], vblk)
                left = _split_dot(g, upper, 1, nt=True) + pre
                sig = jnp.exp2(lb)
                dz = g * (1.0 - sig) - sig * left
                if not isinstance(mask, str):
                    dz = jnp.where(mask, dz, 0.0)
                dzb = dz.astype(BF16)
                a_all.append(a.astype(BF16))
                do_all.append(dos[n])
                dz_all.append(dzb)
                q_all.append(qs[n])
                new.append((left[:, tk - 1:tk] + g[:, tk - 1:tk], dq + _dot(dzb, kblk)))
            dv_ref[rows, :] += _dot_tn(jnp.concatenate(a_all, axis=0), jnp.concatenate(do_all, axis=0))
            dk_ref[rows, :] += _dot_tn(jnp.concatenate(dz_all, axis=0), jnp.concatenate(q_all, axis=0))
            return tuple(new)

        lane1 = lax.broadcasted_iota(jnp.int32, (1, LANES), 1)
        first = i * nsub
        for n, (hh, _) in enumerate(chains):
            top = jnp.max(rights[n], axis=0, keepdims=True)
            kb_of = lane1 - 64 * hh
            live = (kb_of >= 0) & (kb_of < i * nsub) & (top >= SB_CUT)
            first = jnp.minimum(first, jnp.min(jnp.where(live, kb_of, i * nsub)))

        init = (jnp.zeros((th, 1), F32), jnp.zeros((th, LANES), F32))
        carry = lax.fori_loop(first, i * nsub, lambda kb, cy: step(kb, cy, None), (init,) * len(chains))
        for sub in range(nsub):
            carry = step(i * nsub + sub, carry, sub)
        for n in range(nh):
            dq_ref[n * th:(n + 1) * th, :] = jnp.where(lane < 64, carry[n][1], carry[nh + n][1]) * SB_SCALE

    return pl.pallas_call(
        body, name="sb_bwd", grid=(4, s_len // tq),
        in_specs=[pl.BlockSpec((tq, LANES), lambda p, i: (i, p)),
                  pl.BlockSpec((s_len, LANES), lambda p, i: (0, 4 + p)),
                  pl.BlockSpec((s_len, LANES), lambda p, i: (0, 8 + p)),
                  pl.BlockSpec((tq, LANES), lambda p, i: (i, p)),
                  pl.BlockSpec((None, tq, LANES), lambda p, i: (p, i, 0))],
        out_specs=[pl.BlockSpec((tq, LANES), lambda p, i: (i, p)),
                   pl.BlockSpec((s_len, LANES), lambda p, i: (0, p)),
                   pl.BlockSpec((s_len, LANES), lambda p, i: (0, p))],
        out_shape=[jax.ShapeDtypeStruct((s_len, 512), F32)] * 3,
        compiler_params=_cparams(48, dimension_semantics=("arbitrary", "arbitrary")),
    )(sb, sb, sb, do, r)


def _merge_fwd(x, oa, ob, gates, bg, wa, wb, wo):
    s_len = x.shape[0]
    tm = _row_block(s_len)

    def body(x_ref, oa_ref, ob_ref, g_ref, bg_ref, wa_ref, wb_ref, wo_ref, y_ref):
        pa = _dot(oa_ref[...], wa_ref[...])
        pb = _dot(ob_ref[...], wb_ref[...])
        merged = (_sigmoid(g_ref[:, 0:D_MODEL] + bg_ref[0:1, :]) * pa
                  + _sigmoid(g_ref[:, D_MODEL:2 * D_MODEL] + bg_ref[1:2, :]) * pb)
        y_ref[...] = x_ref[...] + _dot(merged.astype(BF16), wo_ref[...])

    full = lambda shape: pl.BlockSpec(shape, lambda i: (0, 0))
    rowb = lambda n: pl.BlockSpec((tm, n), lambda i: (i, 0))
    return pl.pallas_call(
        body, name="merge_fwd", grid=(s_len // tm,),
        in_specs=[rowb(1024), rowb(512), rowb(512), rowb(2048), full((2, 1024)), full((512, 1024)),
                  full((512, 1024)), full((1024, 1024))],
        out_specs=rowb(1024),
        out_shape=jax.ShapeDtypeStruct((s_len, D_MODEL), F32),
        compiler_params=_cparams(48, dimension_semantics=("parallel",)),
    )(x, oa, ob, gates, bg, wa, wb, wo)


def _merge_bwd(dx1, oa, ob, gates, bg, wa, wb, wo):
    s_len = dx1.shape[0]
    tm = _row_block(s_len)

    def body(dx_ref, oa_ref, ob_ref, g_ref, bg_ref, wa_ref, wb_ref, wo_ref,
             doa_ref, dob_ref, dgate_ref, dpa_ref, dpb_ref, merged_ref, dxb_ref, dbg_ref):
        first = pl.program_id(0) == 0
        dxb = dx_ref[...].astype(BF16)
        dxb_ref[...] = dxb
        pa = _dot(oa_ref[...], wa_ref[...])
        pb = _dot(ob_ref[...], wb_ref[...])
        sa = _sigmoid(g_ref[:, 0:D_MODEL] + bg_ref[0:1, :])
        sbg = _sigmoid(g_ref[:, D_MODEL:2 * D_MODEL] + bg_ref[1:2, :])
        merged_ref[...] = (sa * pa + sbg * pb).astype(BF16)
        dm = _dot_nt(dxb, wo_ref[...])
        dpa = (dm * sa).astype(BF16)
        dpb = (dm * sbg).astype(BF16)
        dpa_ref[...] = dpa
        dpb_ref[...] = dpb
        dga = dm * pa * sa * (1.0 - sa)
        dgb = dm * pb * sbg * (1.0 - sbg)
        dgate_ref[:, 0:D_MODEL] = dga.astype(BF16)
        dgate_ref[:, D_MODEL:2 * D_MODEL] = dgb.astype(BF16)
        _acc_rows(dbg_ref.at[0:1, :], dga, first)
        _acc_rows(dbg_ref.at[1:2, :], dgb, first)
        doa_ref[...] = _dot_nt(dpa, wa_ref[...]).astype(BF16)
        dob_ref[...] = _dot_nt(dpb, wb_ref[...]).astype(BF16)

    full = lambda shape: pl.BlockSpec(shape, lambda i: (0, 0))
    rowb = lambda n: pl.BlockSpec((tm, n), lambda i: (i, 0))
    sds = lambda n, dt: jax.ShapeDtypeStruct((s_len, n), dt)
    return pl.pallas_call(
        body, name="merge_bwd", grid=(s_len // tm,),
        in_specs=[rowb(1024), rowb(512), rowb(512), rowb(2048), full((2, 1024)), full((512, 1024)),
                  full((512, 1024)), full((1024, 1024))],
        out_specs=[rowb(512), rowb(512), rowb(2048), rowb(1024), rowb(1024), rowb(1024), rowb(1024),
                   full((2, 1024))],
        out_shape=[sds(512, BF16), sds(512, BF16), sds(2048, BF16), sds(1024, BF16), sds(1024, BF16),
                   sds(1024, BF16), sds(1024, BF16), jax.ShapeDtypeStruct((2, 1024), F32)],
        compiler_params=_cparams(48, dimension_semantics=("arbitrary",)),
    )(dx1, oa, ob, gates, bg, wa, wb, wo)


def _mem_kv(mem, g, w):
    m_len = mem.shape[0]

    def body(mem_ref, g_ref, w_ref, mn_ref, kv_ref):
        mn, _, _ = _rms(mem_ref[...], g_ref[...])
        mnb = mn.astype(BF16)
        mn_ref[...] = mnb
        kv_ref[...] = _dot(mnb, w_ref[...]).astype(BF16)

    return pl.pallas_call(
        body, name="mem_kv",
        out_shape=[jax.ShapeDtypeStruct((m_len, D_MODEL), BF16), jax.ShapeDtypeStruct((m_len, 1024), BF16)],
    )(mem, g, w)


def _mem_bwd(mem, g, w, mn, dkv):
    def body(mem_ref, g_ref, w_ref, mn_ref, dkv_ref, dw_ref, dg_ref):
        dkvb = dkv_ref[...].astype(BF16)
        dw_ref[...] = _dot_tn(mn_ref[...], dkvb)
        dmn = _dot_nt(dkvb, w_ref[...])
        _, xh, _ = _rms(mem_ref[...], g_ref[...])
        dg_ref[...] = jnp.sum(dmn * xh, axis=0, keepdims=True)

    return pl.pallas_call(
        body, name="mem_bwd",
        out_shape=[jax.ShapeDtypeStruct((D_MODEL, 1024), F32), jax.ShapeDtypeStruct((1, D_MODEL), F32)],
    )(mem, g, w, mn, dkv)


def _xattn_heads(xqb, kv_ref, m_len):
    ps = []
    for h in range(X_HEADS):
        hs = slice(h * X_HEAD_DIM, (h + 1) * X_HEAD_DIM)
        s = _dot_nt(xqb[:, hs], kv_ref[:, hs]) * X_SCALE
        e = jnp.exp(s - jnp.max(s, axis=-1, keepdims=True))
        ps.append(e / jnp.sum(e, axis=-1, keepdims=True))
    return ps


def _xattn_fwd(x1, g, wxq, kv, wxo):
    s_len, m_len = x1.shape[0], kv.shape[0]
    tm = _row_block(s_len)

    def body(x_ref, g_ref, wq_ref, kv_ref, wo_ref, y_ref):
        hx, _, _ = _rms(x_ref[...], g_ref[...])
        xqb = _dot(hx.astype(BF16), wq_ref[...]).astype(BF16)
        ps = _xattn_heads(xqb, kv_ref, m_len)
        xo = jnp.concatenate(
            [_dot(ps[h].astype(BF16), kv_ref[:, 512 + h * X_HEAD_DIM:512 + (h + 1) * X_HEAD_DIM])
             for h in range(X_HEADS)], axis=-1)
        y_ref[...] = x_ref[...] + _dot(xo.astype(BF16), wo_ref[...])

    full = lambda shape: pl.BlockSpec(shape, lambda i: (0, 0))
    rowb = lambda n: pl.BlockSpec((tm, n), lambda i: (i, 0))
    return pl.pallas_call(
        body, name="xattn_fwd", grid=(s_len // tm,),
        in_specs=[rowb(1024), full((1, 1024)), full((1024, 512)), full((m_len, 1024)), full((512, 1024))],
        out_specs=rowb(1024),
        out_shape=jax.ShapeDtypeStruct((s_len, D_MODEL), F32),
        compiler_params=_cparams(48, dimension_semantics=("parallel",)),
    )(x1, g, wxq, kv, wxo)


def _xattn_bwd(x1, dx2, g, wxq, kv, wxo):
    s_len, m_len = x1.shape[0], kv.shape[0]
    tm = _row_block(s_len)

    def body(x_ref, dy_ref, g_ref, wq_ref, kv_ref, wo_ref, dx_ref, dwq_ref, dwo_ref, dkv_ref, dg_ref):
        first = pl.program_id(0) == 0
        gv = g_ref[...]
        hx, xh, r = _rms(x_ref[...], gv)
        hxb = hx.astype(BF16)
        xqb = _dot(hxb, wq_ref[...]).astype(BF16)
        ps = _xattn_heads(xqb, kv_ref, m_len)
        dy = dy_ref[...]
        dyb = dy.astype(BF16)
        dxo = _dot_nt(dyb, wo_ref[...])
        xos, dqs, dks, dvs = [], [], [], []
        for h in range(X_HEADS):
            hs = slice(h * X_HEAD_DIM, (h + 1) * X_HEAD_DIM)
            vs = slice(512 + h * X_HEAD_DIM, 512 + (h + 1) * X_HEAD_DIM)
            p = ps[h]
            pb = p.astype(BF16)
            dxoh = dxo[:, hs].astype(BF16)
            xos.append(_dot(pb, kv_ref[:, vs]))
            dp = _dot_nt(dxoh, kv_ref[:, vs])
            ds = (p * (dp - jnp.sum(dp * p, axis=-1, keepdims=True)) * X_SCALE).astype(BF16)
            dvs.append(_dot_tn(pb, dxoh))
            dks.append(_dot_tn(ds, xqb[:, hs]))
            dqs.append(_dot(ds, kv_ref[:, hs]))
        xob = jnp.concatenate(xos, axis=-1).astype(BF16)
        dxqb = jnp.concatenate(dqs, axis=-1).astype(BF16)
        _acc(dwo_ref, _dot_tn(xob, dyb), first)
        _acc(dwq_ref, _dot_tn(hxb, dxqb), first)
        _acc(dkv_ref, jnp.concatenate(dks + dvs, axis=-1), first)
        dhx = _dot_nt(dxqb, wq_ref[...])
        dx, dgr = _rms_bwd(dhx, xh, r, gv)
        dx_ref[...] = dy + dx
        _acc_rows(dg_ref, dgr, first)

    full = lambda shape: pl.BlockSpec(shape, lambda i: (0, 0))
    rowb = lambda n: pl.BlockSpec((tm, n), lambda i: (i, 0))
    return pl.pallas_call(
        body, name="xattn_bwd", grid=(s_len // tm,),
        in_specs=[rowb(1024), rowb(1024), full((1, 1024)), full((1024, 512)), full((m_len, 1024)),
                  full((512, 1024))],
        out_specs=[rowb(1024), full((1024, 512)), full((512, 1024)), full((m_len, 1024)), full((1, 1024))],
        out_shape=[jax.ShapeDtypeStruct((s_len, D_MODEL), F32), jax.ShapeDtypeStruct((1024, 512), F32),
                   jax.ShapeDtypeStruct((512, 1024), F32), jax.ShapeDtypeStruct((m_len, 1024), F32),
                   jax.ShapeDtypeStruct((1, D_MODEL), F32)],
        compiler_params=_cparams(48, dimension_semantics=("arbitrary",)),
    )(x1, dx2, g, wxq, kv, wxo)


FF_TILE = 1408


def _ffn_fwd(x2, g, wg, wu, wd):
    s_len = x2.shape[0]
    tm, tf = _row_block(s_len), FF_TILE

    def body(x_ref, g_ref, wg_ref, wu_ref, wd_ref, y_ref, h_ref):
        j = pl.program_id(1)

        @pl.when(j == 0)
        def _():
            hf, _, _ = _rms(x_ref[...], g_ref[...])
            h_ref[...] = hf.astype(BF16)
            y_ref[...] = x_ref[...]

        hb = h_ref[...]
        gt = _dot(hb, wg_ref[...])
        up = _dot(hb, wu_ref[...])
        act = gt * _sigmoid(gt) * up
        y_ref[...] += _dot(act.astype(BF16), wd_ref[...])

    rowb = pl.BlockSpec((tm, D_MODEL), lambda i, j: (i, 0))
    return pl.pallas_call(
        body, name="ffn_fwd", grid=(s_len // tm, D_FF // tf),
        in_specs=[rowb, pl.BlockSpec((1, D_MODEL), lambda i, j: (0, 0)),
                  pl.BlockSpec((D_MODEL, tf), lambda i, j: (0, j)),
                  pl.BlockSpec((D_MODEL, tf), lambda i, j: (0, j)),
                  pl.BlockSpec((tf, D_MODEL), lambda i, j: (j, 0))],
        out_specs=[rowb, rowb],
        out_shape=[jax.ShapeDtypeStruct((s_len, D_MODEL), F32), jax.ShapeDtypeStruct((s_len, D_MODEL), BF16)],
        compiler_params=_cparams(48, dimension_semantics=("parallel", "arbitrary")),
    )(x2, g, wg, wu, wd)


def _ffn_bwd(x2, hf, dx3, dx3b, g, wg, wu, wd):
    s_len = x2.shape[0]
    tm, tf = _row_block(s_len), FF_TILE
    nf = D_FF // tf

    def act_body(h_ref, dy_ref, wg_ref, wu_ref, wd_ref, dgt_ref, dup_ref, act_ref):
        hb = h_ref[...]
        gt = _dot(hb, wg_ref[...])
        up = _dot(hb, wu_ref[...])
        sg = _sigmoid(gt)
        silu = gt * sg
        dact = _dot_nt(dy_ref[...], wd_ref[...])
        dgt_ref[...] = (dact * up * (sg * (1.0 + gt * (1.0 - sg)))).astype(BF16)
        dup_ref[...] = (dact * silu).astype(BF16)
        act_ref[...] = (silu * up).astype(BF16)

    rowb = pl.BlockSpec((tm, D_MODEL), lambda i, j: (i, 0))
    ffb = pl.BlockSpec((tm, tf), lambda i, j: (i, j))
    dgt, dup, act = pl.pallas_call(
        act_body, name="ffn_bwd_act", grid=(s_len // tm, nf),
        in_specs=[rowb, rowb,
                  pl.BlockSpec((D_MODEL, tf), lambda i, j: (0, j)),
                  pl.BlockSpec((D_MODEL, tf), lambda i, j: (0, j)),
                  pl.BlockSpec((tf, D_MODEL), lambda i, j: (j, 0))],
        out_specs=[ffb, ffb, ffb],
        out_shape=[jax.ShapeDtypeStruct((s_len, D_FF), BF16)] * 3,
        compiler_params=_cparams(56, dimension_semantics=("parallel", "arbitrary")),
    )(hf, dx3b, wg, wu, wd)
    tm = min(s_len, 256)

    def in_body(x_ref, dy_ref, g_ref, wg_ref, wu_ref, dgt_ref, dup_ref, dx_ref, dg_ref):
        dh = _dot_nt(dgt_ref[...], wg_ref[...]) + _dot_nt(dup_ref[...], wu_ref[...])
        gv = g_ref[...]
        _, xh, r = _rms(x_ref[...], gv)
        dx, dgr = _rms_bwd(dh, xh, r, gv)
        dx_ref[...] = dy_ref[...] + dx
        _acc_rows(dg_ref, dgr, pl.program_id(0) == 0)

    row1 = lambda n: pl.BlockSpec((tm, n), lambda i: (i, 0))
    full = lambda shape: pl.BlockSpec(shape, lambda i: (0, 0))
    dx2, dg = pl.pallas_call(
        in_body, name="ffn_bwd_in", grid=(s_len // tm,),
        in_specs=[row1(D_MODEL), row1(D_MODEL), full((1, D_MODEL)), full((D_MODEL, D_FF)), full((D_MODEL, D_FF)),
                  row1(D_FF), row1(D_FF)],
        out_specs=[row1(D_MODEL), full((1, D_MODEL))],
        out_shape=[jax.ShapeDtypeStruct((s_len, D_MODEL), F32), jax.ShapeDtypeStruct((1, D_MODEL), F32)],
        compiler_params=_cparams(48, dimension_semantics=("arbitrary",)),
    )(x2, dx3, g, wg, wu, dgt, dup)
    return dx2, dgt, dup, act, dg


def _loss_head(x3, g, target):
    s_len = x3.shape[0]
    tm = _row_block(s_len)

    def body(x_ref, g_ref, t_ref, sse_ref, dx_ref, dxb_ref, dg_ref):
        first = pl.program_id(0) == 0
        gv = g_ref[...]
        y, xh, r = _rms(x_ref[...], gv)
        err = y - t_ref[...]
        _acc(sse_ref, jnp.broadcast_to(jnp.sum(err * err), (8, LANES)), first)
        dx, dgr = _rms_bwd(err * (1.0 / D_MODEL), xh, r, gv)
        dx_ref[...] = dx
        dxb_ref[...] = dx.astype(BF16)
        _acc_rows(dg_ref, dgr, first)

    rowb = pl.BlockSpec((tm, D_MODEL), lambda i: (i, 0))
    return pl.pallas_call(
        body, name="loss_head", grid=(s_len // tm,),
        in_specs=[rowb, pl.BlockSpec((1, D_MODEL), lambda i: (0, 0)), rowb],
        out_specs=[pl.BlockSpec((8, LANES), lambda i: (0, 0)), rowb, rowb,
                   pl.BlockSpec((1, D_MODEL), lambda i: (0, 0))],
        out_shape=[jax.ShapeDtypeStruct((8, LANES), F32), jax.ShapeDtypeStruct((s_len, D_MODEL), F32),
                   jax.ShapeDtypeStruct((s_len, D_MODEL), BF16), jax.ShapeDtypeStruct((1, D_MODEL), F32)],
        compiler_params=_cparams(dimension_semantics=("arbitrary",)),
    )(x3, g, target)


def _mla_prep_bwd(lat, g_q, g_kv, w_uq, w_uk, w_uv, cosf, sinf, dq, dk, dv):
    s_len = lat.shape[0]
    tm = _row_block(s_len)

    def body(lat_ref, gq_ref, gkv_ref, wuq_ref, wuk_ref, wuv_ref, cos_ref, sin_ref, dq_ref, dk_ref, dv_ref,
             dlat_ref, dqb_ref, dkb_ref, dvb_ref, dgq_ref, dgkv_ref):
        first = pl.program_id(0) == 0
        lane = lax.broadcasted_iota(jnp.int32, (tm, LANES), 1)
        cosv, sinv = cos_ref[...], sin_ref[...]
        gq, gkv = gq_ref[...], gkv_ref[...]
        _, qxh, qr = _rms(lat_ref[:, 0:256], gq)
        _, kxh, kr_ = _rms(lat_ref[:, 256:384], gkv)
        dkr = jnp.zeros((tm, LANES), F32)
        for h in range(MLA_HEADS):
            sl = slice(h * HEAD_PAD, (h + 1) * HEAD_PAD)
            blk = dq_ref[:, sl]
            dqb_ref[:, sl] = (blk * cosv + _rope_rot_t(blk, lane) * sinv).astype(BF16)
            kblk = dk_ref[:, sl] * (1.0 / MLA_Q_FOLD)
            dkb_ref[:, sl] = kblk.astype(BF16)
            dkr = dkr + kblk
        dvb = dv_ref[...].astype(BF16)
        dvb_ref[...] = dvb
        dkr = jnp.where((lane >= 64) & (lane < 96), dkr, 0.0)
        dkr = dkr * cosv + _rope_rot_t(dkr, lane) * sinv
        dql = _dot_nt(dqb_ref[...], wuq_ref[...])
        dkvl = _dot_nt(dkb_ref[...], wuk_ref[...]) + _dot_nt(dvb, wuv_ref[...])
        dcq, dgqr = _rms_bwd(dql, qxh, qr, gq)
        dckv, dgkvr = _rms_bwd(dkvl, kxh, kr_, gkv)
        dlat_ref[:, 0:256] = dcq
        dlat_ref[:, 256:384] = dckv
        dlat_ref[:, K_R_OFF:K_R_OFF + LANES] = pltpu.roll(dkr, 64, 1)
        _acc_rows(dgq_ref, dgqr, first)
        _acc_rows(dgkv_ref, dgkvr, first)

    full = lambda shape: pl.BlockSpec(shape, lambda i: (0, 0))
    rowb = lambda n: pl.BlockSpec((tm, n), lambda i: (i, 0))
    sds = lambda n, dt: jax.ShapeDtypeStruct((s_len, n), dt)
    return pl.pallas_call(
        body, name="mla_prep_bwd", grid=(s_len // tm,),
        in_specs=[rowb(512), full((1, 256)), full((1, 128)), full((256, 1024)), full((128, 1024)),
                  full((128, 512)), rowb(128), rowb(128), rowb(1024), rowb(1024), rowb(512)],
        out_specs=[rowb(512), rowb(1024), rowb(1024), rowb(512), full((1, 256)), full((1, 128))],
        out_shape=[sds(512, F32), sds(1024, BF16), sds(1024, BF16), sds(512, BF16),
                   jax.ShapeDtypeStruct((1, 256), F32), jax.ShapeDtypeStruct((1, 128), F32)],
        compiler_params=_cparams(48, dimension_semantics=("arbitrary",)),
    )(lat, g_q, g_kv, w_uq, w_uk, w_uv, cosf, sinf, dq, dk, dv)


def _in_proj_bwd(x, g, w, dx1, dlat, dsbq, dsbk, dsbv, dgates, ride):
    s_len = x.shape[0]
    tm = min(s_len, 256)
    nb = s_len // tm

    def body(x_ref, g_ref, w_ref, dx1_ref, dlat_ref, dq_ref, dk_ref, dv_ref, dgate_ref, *rest):
        gx_ref, dg_ref = rest[ride.n:ride.n + 2]
        dproj = rest[-1]
        ride.bind(rest[:ride.n], rest[ride.n + 2:2 * ride.n + 2], rest[2 * ride.n + 2:-1])
        pl.when(pl.program_id(0) == 0)(ride.issue)
        dproj[:, 0:512] = dlat_ref[...].astype(BF16)
        dproj[:, 512:1024] = dq_ref[...].astype(BF16)
        dproj[:, 1024:1536] = (dk_ref[...] * LN2).astype(BF16)
        dproj[:, 1536:2048] = dv_ref[...].astype(BF16)
        dproj[:, 2048:4096] = dgate_ref[...]
        dh = _dot_nt(dproj[...], w_ref[...])
        gv = g_ref[...]
        _, xh, r = _rms(x_ref[...], gv)
        dx, dgr = _rms_bwd(dh, xh, r, gv)
        gx_ref[...] = dx1_ref[...] + dx
        _acc_rows(dg_ref, dgr, pl.program_id(0) == 0)
        pl.when(pl.program_id(0) == nb - 1)(ride.finish)

    rowb = lambda n: pl.BlockSpec((tm, n), lambda i: (i, 0))
    full = lambda shape: pl.BlockSpec(shape, lambda i: (0, 0))
    outs = pl.pallas_call(
        body, name="in_proj_bwd", grid=(nb,),
        in_specs=[rowb(D_MODEL), full((1, D_MODEL)), full((D_MODEL, D_IN_PAD)), rowb(D_MODEL),
                  rowb(512), rowb(512), rowb(512), rowb(512), rowb(2 * D_MODEL)] + ride.specs,
        out_specs=[rowb(D_MODEL), full((1, D_MODEL))] + ride.specs,
        out_shape=[jax.ShapeDtypeStruct((s_len, D_MODEL), F32), jax.ShapeDtypeStruct((1, D_MODEL), F32)]
        + ride.out_shape,
        scratch_shapes=ride.scratch + [pltpu.VMEM((tm, D_IN_PAD), BF16)],
        compiler_params=_cparams(48, dimension_semantics=("arbitrary",)),
    )(x, g, w, dx1, dlat, dsbq, dsbk, dsbv, dgates, *ride.srcs)
    return outs[0], outs[1], outs[2:]


def _adamw(landed, w, m, v, name):
    r, c = w.shape
    lanes = _round_up(c, LANES)
    tb = r
    for cand in range(r, 0, -1):
        if r % cand == 0 and (cand % 8 == 0 or cand == r) and N_DEV * cand * lanes * 4 <= ADAM_BLOCK_BYTES:
            tb = cand
            break
    c1 = 1.0 - ADAM_B1 ** ADAM_STEP
    c2 = 1.0 - ADAM_B2 ** ADAM_STEP

    def body(l_ref, w_ref, m_ref, v_ref, g_ref, d_ref, nm_ref, nv_ref):
        g = l_ref[0]
        for k in range(1, N_DEV):
            g = g + l_ref[k]
        nm = ADAM_B1 * m_ref[...] + (1.0 - ADAM_B1) * g
        nv = ADAM_B2 * v_ref[...] + (1.0 - ADAM_B2) * (g * g)
        g_ref[...] = g
        nm_ref[...] = nm
        nv_ref[...] = nv
        d_ref[...] = -ADAM_LR * ((nm / c1) / (jnp.sqrt(nv / c2) + ADAM_EPS) + ADAM_WD * w_ref[...])

    blk = pl.BlockSpec((tb, c), lambda i: (i, 0))
    return pl.pallas_call(
        body, name=name, grid=(r // tb,),
        in_specs=[pl.BlockSpec((N_DEV, tb, c), lambda i: (0, i, 0)), blk, blk, blk],
        out_specs=[blk, blk, blk, blk],
        out_shape=[jax.ShapeDtypeStruct((r, c), F32)] * 4,
        compiler_params=_cparams(dimension_semantics=("parallel",)),
    )(landed, w, m, v)


def _shard_shape(shape, axis):
    return tuple(d // N_DEV if a == axis else d for a, d in enumerate(shape))


def _split_pieces(full, axis):
    r, c = full.shape
    if axis == 0:
        return full.reshape(N_DEV, r // N_DEV, c)
    return full.reshape(r, N_DEV, c // N_DEV).transpose(1, 0, 2)


def _join_shards(gathered, axis):
    _, r, c = gathered.shape
    if axis == 0:
        return gathered.reshape(N_DEV * r, c)
    return gathered.transpose(1, 0, 2).reshape(r, N_DEV * c)


def kernel(x, mem, positions, g_mix, w_in, b_gate, g_q_lat, w_uq, g_kv_lat, w_ukv, w_a_proj, w_b_proj, w_o, g_x, g_mem, w_xq, w_xkv, w_xo, g_ffn, w_gate, w_up, w_down, g_final, loss_target, m_g_mix, m_w_in, m_b_gate, m_g_q_lat, m_w_uq, m_g_kv_lat, m_w_ukv, m_w_a_proj, m_w_b_proj, m_w_o, m_g_x, m_g_mem, m_w_xq, m_w_xkv, m_w_xo, m_g_ffn, m_w_gate, m_w_up, m_w_down, m_g_final, v_g_mix, v_w_in, v_b_gate, v_g_q_lat, v_w_uq, v_g_kv_lat, v_w_ukv, v_w_a_proj, v_w_b_proj, v_w_o, v_g_x, v_g_mem, v_w_xq, v_w_xkv, v_w_xo, v_g_ffn, v_w_gate, v_w_up, v_w_down, v_g_final):
    given = dict(locals())
    s_len = x.shape[1]
    x2d = x.reshape(s_len, D_MODEL)
    mem2d = mem.reshape(-1, D_MODEL)
    target = loss_target.reshape(s_len, D_MODEL)

    names = [name for name, _, _ in SHARDED]
    axis_of = {name: axis for name, _, axis in SHARDED}
    shard2d = lambda name, prefix="": given[prefix + name].reshape(
        _shard_shape(dict((n, s) for n, s, _ in SHARDED)[name], axis_of[name]))

    wire = lambda name: shard2d(name) if name == "b_gate" else shard2d(name).astype(BF16)
    early = [n for n in names if n in NEEDED_FIRST]
    late = [n for n in names if n not in NEEDED_FIRST]
    gathered = _exchange(True, [wire(n) for n in early], "weights_gather_first")
    wts = {n: _join_shards(g, axis_of[n]) for n, g in zip(early, gathered)}

    w_in_p = jnp.concatenate([wts["w_in"][:, :416], jnp.zeros((D_MODEL, 96), BF16), wts["w_in"][:, 416:]], axis=1)
    w_uq_p = jnp.pad(wts["w_uq"].reshape(256, MLA_HEADS, 96), ((0, 0), (0, 0), (0, 32))).reshape(256, 1024)
    ukv = wts["w_ukv"].reshape(128, MLA_HEADS, 128)
    w_uk_p = jnp.pad(ukv[:, :, :64], ((0, 0), (0, 0), (0, 64))).reshape(128, 1024)
    w_uv = ukv[:, :, 64:].reshape(128, 512)
    bg = wts["b_gate"]

    inv_freq = ROPE_THETA ** (-jnp.arange(0, MLA_ROPE, 2, dtype=F32) / MLA_ROPE)
    ang = positions.reshape(s_len).astype(F32)[:, None] * inv_freq
    cos16, sin16 = jnp.cos(ang), jnp.sin(ang)
    cosf = jnp.concatenate([jnp.ones((s_len, 64), F32), cos16, cos16, jnp.ones((s_len, 32), F32)], axis=1)
    sinf = jnp.concatenate([jnp.zeros((s_len, 64), F32), sin16, sin16, jnp.zeros((s_len, 32), F32)], axis=1)

    h1, lat, sb, gates = _in_proj(x2d, g_mix, w_in_p)
    qa, ka, va, q_lat, kv_lat = _mla_prep(lat, g_q_lat, g_kv_lat, w_uq_p, w_uk_p, w_uv, cosf, sinf)
    oa, lse, gathered = _mla_fwd(qa, ka, va, _Exchange(True, [wire(n) for n in late]))
    wts.update({n: _join_shards(g, axis_of[n]) for n, g in zip(late, gathered)})
    ob, sb_r = _sb_fwd(sb)
    x1 = _merge_fwd(x2d, oa, ob, gates, bg, wts["w_a_proj"], wts["w_b_proj"], wts["w_o"])
    mn, xkv = _mem_kv(mem2d, g_mem, wts["w_xkv"])
    x2 = _xattn_fwd(x1, g_x, wts["w_xq"], xkv, wts["w_xo"])
    x3, hf = _ffn_fwd(x2, g_ffn, wts["w_gate"], wts["w_up"], wts["w_down"])
    g_final2d = g_final.reshape(1, D_MODEL)
    sse, dx3, dx3b, dg_final = _loss_head(x3, g_final2d, target)
    loss = lax.psum(sse[0, 0] * (0.5 / D_MODEL), ("x", "y", "c"))

    dx2, dgt, dup, act, dg_ffn = _ffn_bwd(x2, hf, dx3, dx3b, g_ffn, wts["w_gate"], wts["w_up"], wts["w_down"])
    dx1, dw_xq, dw_xo, dxkv, dg_x = _xattn_bwd(x1, dx2, g_x, wts["w_xq"], xkv, wts["w_xo"])
    dw_xkv, dg_mem = _mem_bwd(mem2d, g_mem, wts["w_xkv"], mn, dxkv)
    doa, dob, dgates, dpa, dpb, merged, dx1b, dbg = _merge_bwd(
        dx1, oa, ob, gates, bg, wts["w_a_proj"], wts["w_b_proj"], wts["w_o"])
    dsbq, dsbk, dsbv = _sb_bwd(sb, dob, sb_r)
    full_grads = {
        "w_a_proj": _tn_matmul(oa, dpa, "dw_a"),
        "w_b_proj": _tn_matmul(ob, dpb, "dw_b"),
        "w_o": _tn_matmul(merged, dx1b, "dw_o"),
        "w_xq": dw_xq,
        "w_xkv": dw_xkv,
        "w_xo": dw_xo,
        "w_gate": _tn_matmul(hf, dgt, "dw_gate", tn=FF_TILE),
        "w_up": _tn_matmul(hf, dup, "dw_up", tn=FF_TILE),
        "w_down": _tn_matmul(act, dx3b, "dw_down", tka=FF_TILE),
    }
    dqa, dka, dva, got = _mla_bwd(
        qa, ka, va, oa, doa, lse, _Exchange(False, [_split_pieces(full_grads[n], axis_of[n]) for n in late]))
    landed = dict(zip(late, got))
    dlat, dqb, dkb, dvb, dg_q, dg_kv = _mla_prep_bwd(
        lat, g_q_lat, g_kv_lat, w_uq_p, w_uk_p, w_uv, cosf, sinf, dqa, dka, dva)
    dw_in = jnp.concatenate([
        _tn_matmul(h1, dlat, "dw_in_lat")[:, :416],
        _tn_matmul(h1, dsbq, "dw_in_sbq"),
        _tn_matmul(h1, dsbk, "dw_in_sbk", scale=LN2),
        _tn_matmul(h1, dsbv, "dw_in_sbv"),
        _tn_matmul(h1, dgates, "dw_in_gates")], axis=1)
    dw_uq_p = _tn_matmul(q_lat, dqb, "dw_uq")
    dw_uk_p = _tn_matmul(kv_lat, dkb, "dw_uk")
    dw_uv = _tn_matmul(kv_lat, dvb, "dw_uv")
    full_grads.update({
        "w_in": dw_in,
        "b_gate": dbg,
        "w_uq": dw_uq_p.reshape(256, MLA_HEADS, 128)[:, :, :96].reshape(256, 768),
        "w_ukv": jnp.concatenate([dw_uk_p.reshape(128, MLA_HEADS, 128)[:, :, :64],
                                  dw_uv.reshape(128, MLA_HEADS, 64)], axis=2).reshape(128, 1024),
    })
    grad_x, dg_mix, got = _in_proj_bwd(
        x2d, g_mix, w_in_p, dx1, dlat, dsbq, dsbk, dsbv, dgates,
        _Exchange(False, [_split_pieces(full_grads[n], axis_of[n]) for n in early]))
    landed.update(zip(early, got))
    rep_grads = {"g_mix": dg_mix, "g_q_lat": dg_q, "g_kv_lat": dg_kv, "g_x": dg_x, "g_mem": dg_mem,
                 "g_ffn": dg_ffn, "g_final": dg_final}
    rep_cat = lambda prefix, src: jnp.concatenate(
        [src[prefix + n].reshape(-1) for n, _ in REPLICATED]).reshape(-1, LANES)
    rep_src = jnp.broadcast_to(rep_cat("", rep_grads), (N_DEV,) + rep_cat("", rep_grads).shape)
    rep_landed = _exchange(False, [rep_src], "grads_gains")[0]

    res = {}
    for name, _, _ in SHARDED:
        outs = _adamw(landed[name], shard2d(name), shard2d(name, "m_"), shard2d(name, "v_"), "adamw_" + name)
        res[name] = [o.reshape(given[name].shape) for o in outs]
    rep_outs = _adamw(rep_landed, rep_cat("", given), rep_cat("m_", given), rep_cat("v_", given), "adamw_gains")
    off = 0
    for name, n in REPLICATED:
        res[name] = [o.reshape(-1)[off:off + n].reshape(given[name].shape) for o in rep_outs]
        off += n
    result = [loss, grad_x.reshape(x.shape)]
    for k in range(4):
        result.extend(res[name][k] for name in WEIGHT_ORDER)
    return tuple(result)
```

```python
import functools
import math

import jax
import jax.numpy as jnp
from jax import lax
from jax.experimental import pallas as pl
from jax.experimental.pallas import tpu as pltpu

F32 = jnp.float32
BF16 = jnp.bfloat16

D_MODEL = 1024
MLA_HEADS = 8
MLA_Q_RANK = 256
MLA_KV_RANK = 128
MLA_NOPE = 64
MLA_ROPE = 32
ROPE_THETA = 10000.0
SB_WIDTH = 512
X_HEADS = 4
X_HEAD_DIM = 128
D_FF = 2816
EPS = 1e-6
D_IN = 4000
D_IN_PAD = 4096
K_R_OFF = 384
LANES = 128
HEAD_PAD = 128
MLA_SCALE = 1.0 / math.sqrt(MLA_NOPE + MLA_ROPE)
SB_SCALE = 0.125
LOG2E = math.log2(math.e)
LN2 = math.log(2.0)
MLA_Q_FOLD = MLA_SCALE * LOG2E
SB_Q_FOLD = SB_SCALE * LOG2E
SB_CUT = -160.0
X_SCALE = 1.0 / math.sqrt(X_HEAD_DIM)
NEG_BIG = -1e30

ADAM_LR = 0.001
ADAM_B1 = 0.9
ADAM_B2 = 0.999
ADAM_EPS = 1e-08
ADAM_WD = 0.01
ADAM_STEP = 10

N_DEV = 8
MIB = 1024 * 1024
ADAM_BLOCK_BYTES = 4 * MIB

SHARDED = (
    ("w_in", (D_MODEL, D_IN), 1),
    ("b_gate", (2, D_MODEL), 1),
    ("w_uq", (MLA_Q_RANK, 768), 1),
    ("w_ukv", (MLA_KV_RANK, 1024), 1),
    ("w_a_proj", (512, D_MODEL), 1),
    ("w_b_proj", (512, D_MODEL), 1),
    ("w_o", (D_MODEL, D_MODEL), 0),
    ("w_xq", (D_MODEL, 512), 0),
    ("w_xkv", (D_MODEL, 1024), 0),
    ("w_xo", (512, D_MODEL), 1),
    ("w_gate", (D_MODEL, D_FF), 1),
    ("w_up", (D_MODEL, D_FF), 1),
    ("w_down", (D_FF, D_MODEL), 0),
)
NEEDED_FIRST = ("w_in", "b_gate", "w_uq", "w_ukv")
REPLICATED = (
    ("g_mix", 1024), ("g_q_lat", 256), ("g_kv_lat", 128), ("g_x", 1024),
    ("g_mem", 1024), ("g_ffn", 1024), ("g_final", 1024),
)
WEIGHT_ORDER = ("g_mix", "w_in", "b_gate", "g_q_lat", "w_uq", "g_kv_lat", "w_ukv", "w_a_proj",
                "w_b_proj", "w_o", "g_x", "g_mem", "w_xq", "w_xkv", "w_xo", "g_ffn", "w_gate",
                "w_up", "w_down", "g_final")


def _round_up(n, m):
    return -(-n // m) * m


def _cparams(vmem_mib=None, **kw):
    if vmem_mib is not None:
        kw["vmem_limit_bytes"] = vmem_mib * MIB
    return pltpu.CompilerParams(**kw)


def _dot(a, b):
    return jnp.dot(a, b, preferred_element_type=F32)


def _dot_nt(a, b):
    return lax.dot_general(a, b, (((1,), (1,)), ((), ())), preferred_element_type=F32)


def _dot_tn(a, b):
    return lax.dot_general(a, b, (((0,), (0,)), ((), ())), preferred_element_type=F32)


def _rms(x, g):
    r = lax.rsqrt(jnp.mean(x * x, axis=-1, keepdims=True) + EPS)
    xh = x * r
    return xh * g, xh, r


def _rms_bwd(dy, xh, r, g):
    u = dy * g
    dx = r * (u - xh * jnp.mean(u * xh, axis=-1, keepdims=True))
    return dx, dy * xh


def _sigmoid(z):
    return 1.0 / (1.0 + jnp.exp(-z))


def _acc_rows(ref, val, first):
    s = jnp.sum(val, axis=0, keepdims=True)

    @pl.when(first)
    def _():
        ref[...] = s

    @pl.when(jnp.logical_not(first))
    def _():
        ref[...] += s


def _acc(ref, val, first):
    @pl.when(first)
    def _():
        ref[...] = val

    @pl.when(jnp.logical_not(first))
    def _():
        ref[...] += val


def _peer(k):
    x, y, c = lax.axis_index("x"), lax.axis_index("y"), lax.axis_index("c")
    px = 1 - x if (k >> 2) & 1 else x
    py = 1 - y if (k >> 1) & 1 else y
    pc = 1 - c if k & 1 else c
    return (px, py, pc), 4 * px + 2 * py + pc


N_PEERS = N_DEV - 1
OTHER_CHIPS = (2, 4, 6)


def _land_shape(gather, src):
    return (N_DEV,) + src.shape if gather else src.shape


class _Exchange:
    def __init__(self, gather, srcs, to_first=()):
        self.gather, self.m, self.srcs = gather, len(srcs), list(srcs) + list(to_first)
        self.n = len(self.srcs)
        self.out_shape = ([jax.ShapeDtypeStruct(_land_shape(gather, s), s.dtype) for s in srcs]
                          + [jax.ShapeDtypeStruct(_land_shape(True, s), s.dtype) for s in to_first])
        self.specs = [pl.BlockSpec(memory_space=pl.ANY)] * self.n
        self.scratch = [pltpu.SemaphoreType.DMA((self.n * N_PEERS,)), pltpu.SemaphoreType.DMA((self.n * N_PEERS,)),
                        pltpu.SemaphoreType.DMA((self.n,))]

    def bind(self, src, land, sems):
        self.src, self.land = src, land
        self.send_sems, self.recv_sems, self.local_sems = sems

    def _copy(self, a, k, source, to, target=1):
        return pltpu.make_async_remote_copy(
            src_ref=source, dst_ref=to,
            send_sem=self.send_sems.at[a * N_PEERS + k - 1], recv_sem=self.recv_sems.at[a * N_PEERS + k - 1],
            device_id=_peer(target)[0], device_id_type=pl.DeviceIdType.MESH)

    def _row(self, a, k):
        return self.land[a].at[_peer(k)[1]]

    def _mine(self, a):
        me = _peer(0)[1]
        whole = self.gather or a >= self.m
        return pltpu.make_async_copy(self.src[a] if whole else self.src[a].at[me], self.land[a].at[me],
                                     self.local_sems.at[a])

    def issue(self):
        me = _peer(0)[1]
        for a in range(self.m):
            self._mine(a).start()
            for k in ((1,) + OTHER_CHIPS if self.gather else range(1, N_DEV)):
                source = self.src[a] if self.gather else self.src[a].at[_peer(k)[1]]
                self._copy(a, k, source, self.land[a].at[me], target=k).start()
        for a in range(self.m, self.n):
            pl.when(me == 0)(self._mine(a).start)
            for k in range(1, N_DEV):
                pl.when(me == k)(self._copy(a, k, self.src[a], self.land[a].at[me], target=k).start)

    def finish(self):
        me = _peer(0)[1]
        part = lambda a: self.src[a] if self.gather or a >= self.m else self.src[a].at[me]
        if self.gather:
            for a in range(self.m):
                for k in OTHER_CHIPS:
                    self._copy(a, k, part(a), self._row(a, k)).wait_recv()
                    self._copy(a, k + 1, self._row(a, k), self._row(a, k), target=1).start()
        for a in range(self.m):
            for k in ((1, 3, 5, 7) if self.gather else range(1, N_DEV)):
                self._copy(a, k, part(a), self._row(a, k)).wait_recv()
        for a in range(self.m):
            for k in range(1, N_DEV):
                self._copy(a, k, part(a), self.land[a].at[me]).wait_send()
            self._mine(a).wait()
        for a in range(self.m, self.n):
            for k in range(1, N_DEV):
                pl.when(me == 0)(self._copy(a, k, part(a), self._row(a, k)).wait_recv)
                pl.when(me == k)(self._copy(a, k, part(a), self.land[a].at[me]).wait_send)
            pl.when(me == 0)(self._mine(a).wait)


def _exchange(gather, srcs, name):
    ex = _Exchange(gather, srcs)

    def body(*refs):
        ex.bind(refs[:ex.n], refs[ex.n:2 * ex.n], refs[2 * ex.n:])
        ex.issue()
        ex.finish()

    return pl.pallas_call(body, name=name, out_shape=ex.out_shape, in_specs=ex.specs, out_specs=ex.specs,
                          scratch_shapes=ex.scratch)(*ex.srcs)


def _tn_matmul(a, b, name, tka=512, tn=1024, ts=2048, scale=None):
    s_len, ka = a.shape
    n = b.shape[1]
    tka, tn, ts = min(tka, ka), min(tn, n), min(ts, s_len)
    assert ka % tka == 0 and n % tn == 0 and s_len % ts == 0

    def body(a_ref, b_ref, o_ref):
        bv = b_ref[...]
        if scale is not None:
            bv = bv * scale
        _acc(o_ref, _dot_tn(a_ref[...], bv.astype(BF16)), pl.program_id(2) == 0)

    return pl.pallas_call(
        body, name=name, grid=(ka // tka, n // tn, s_len // ts),
        in_specs=[pl.BlockSpec((ts, tka), lambda i, j, s: (s, i)),
                  pl.BlockSpec((ts, tn), lambda i, j, s: (s, j))],
        out_specs=pl.BlockSpec((tka, tn), lambda i, j, s: (i, j)),
        out_shape=jax.ShapeDtypeStruct((ka, n), F32),
        compiler_params=_cparams(dimension_semantics=("parallel", "parallel", "arbitrary")),
    )(a, b)


def _row_block(s_len):
    return min(s_len, 512)


def _in_proj(x, g, w):
    s_len = x.shape[0]
    tm = _row_block(s_len)

    def body(x_ref, g_ref, w_ref, h_ref, lat_ref, sb_ref, gate_ref):
        h, _, _ = _rms(x_ref[...], g_ref[...])
        hb = h.astype(BF16)
        h_ref[...] = hb
        p = _dot(hb, w_ref[:, 0:1024])
        lat_ref[...] = p[:, 0:512]
        sb_ref[:, 0:512] = (p[:, 512:1024] * SB_Q_FOLD).astype(BF16)
        sb_ref[:, 512:1536] = _dot(hb, w_ref[:, 1024:2048]).astype(BF16)
        gate_ref[:, 0:1024] = _dot(hb, w_ref[:, 2048:3072])
        gate_ref[:, 1024:2048] = _dot(hb, w_ref[:, 3072:4096])

    rowb = lambda n: pl.BlockSpec((tm, n), lambda i: (i, 0))
    return pl.pallas_call(
        body, name="in_proj", grid=(s_len // tm,),
        in_specs=[rowb(D_MODEL), pl.BlockSpec((1, D_MODEL), lambda i: (0, 0)),
                  pl.BlockSpec((D_MODEL, D_IN_PAD), lambda i: (0, 0))],
        out_specs=[rowb(D_MODEL), rowb(512), rowb(3 * SB_WIDTH), rowb(2 * D_MODEL)],
        out_shape=[jax.ShapeDtypeStruct((s_len, D_MODEL), BF16),
                   jax.ShapeDtypeStruct((s_len, 512), F32),
                   jax.ShapeDtypeStruct((s_len, 3 * SB_WIDTH), BF16),
                   jax.ShapeDtypeStruct((s_len, 2 * D_MODEL), F32)],
        compiler_params=_cparams(48, dimension_semantics=("parallel",)),
    )(x, g, w)


def _rope_rot(blk, lane):
    return jnp.where(lane < 80, -pltpu.roll(blk, 112, 1), pltpu.roll(blk, 16, 1))


def _rope_rot_t(blk, lane):
    return jnp.where(lane < 80, pltpu.roll(blk, 112, 1), -pltpu.roll(blk, 16, 1))


def _mla_prep(lat, g_q, g_kv, w_uq, w_uk, w_uv, cosf, sinf):
    s_len = lat.shape[0]
    tm = _row_block(s_len)

    def body(lat_ref, gq_ref, gkv_ref, wuq_ref, wuk_ref, wuv_ref, cos_ref, sin_ref,
             q_ref, k_ref, v_ref, ql_ref, kvl_ref):
        lane = lax.broadcasted_iota(jnp.int32, (tm, LANES), 1)
        cosv, sinv = cos_ref[...], sin_ref[...]
        ql, _, _ = _rms(lat_ref[:, 0:256], gq_ref[...])
        kvl, _, _ = _rms(lat_ref[:, 256:384], gkv_ref[...])
        qlb, kvlb = ql.astype(BF16), kvl.astype(BF16)
        ql_ref[...] = qlb
        kvl_ref[...] = kvlb
        q = _dot(qlb, wuq_ref[...])
        kn = _dot(kvlb, wuk_ref[...])
        v_ref[...] = _dot(kvlb, wuv_ref[...]).astype(BF16)
        kr = pltpu.roll(lat_ref[:, K_R_OFF:K_R_OFF + LANES], 64, 1)
        kr = kr * cosv + _rope_rot(kr, lane) * sinv
        for h in range(MLA_HEADS):
            sl = slice(h * HEAD_PAD, (h + 1) * HEAD_PAD)
            blk = q[:, sl]
            q_ref[:, sl] = ((blk * cosv + _rope_rot(blk, lane) * sinv) * MLA_Q_FOLD).astype(BF16)
            k_ref[:, sl] = (kn[:, sl] + kr).astype(BF16)

    full = lambda shape: pl.BlockSpec(shape, lambda i: (0, 0))
    rowb = lambda n: pl.BlockSpec((tm, n), lambda i: (i, 0))
    return pl.pallas_call(
        body, name="mla_prep", grid=(s_len // tm,),
        in_specs=[rowb(512), full((1, 256)), full((1, 128)), full((256, 1024)), full((128, 1024)),
                  full((128, 512)), rowb(128), rowb(128)],
        out_specs=[rowb(1024), rowb(1024), rowb(512), rowb(256), rowb(128)],
        out_shape=[jax.ShapeDtypeStruct((s_len, 1024), BF16), jax.ShapeDtypeStruct((s_len, 1024), BF16),
                   jax.ShapeDtypeStruct((s_len, 512), BF16), jax.ShapeDtypeStruct((s_len, 256), BF16),
                   jax.ShapeDtypeStruct((s_len, 128), BF16)],
        compiler_params=_cparams(dimension_semantics=("parallel",)),
    )(lat, g_q, g_kv, w_uq, w_uk, w_uv, cosf, sinf)


ATTN_TQ = 1024
ATTN_TH = 512
ATTN_TK = 256
SB_TQ = 512
SB_TH = 256
MLA_TK = 512


def _attn_blocks(s_len, tk=ATTN_TK, tq=ATTN_TQ, th=ATTN_TH):
    tq, th, tk = min(s_len, tq), min(s_len, th), min(s_len, tk)
    return tq, th, tk, tq // tk


def _chains(tq, th):
    return [(hh, r0) for hh in range(2) for r0 in range(0, tq, th)]


def _diag_mask(th, tk, r0, sub, strict):
    lo, hi = sub * tk, (sub + 1) * tk - 1
    last, first = r0 + th - 1, r0
    if (lo >= last) if strict else (lo > last):
        return "none"
    if (hi < first) if strict else (hi <= first):
        return "all"
    row = lax.broadcasted_iota(jnp.int32, (th, tk), 0) + r0
    col = lax.broadcasted_iota(jnp.int32, (th, tk), 1) + lo
    return col < row if strict else col <= row


def _mla_fwd(q, k, v, ride):
    s_len = q.shape[0]
    tq, th, tk, nsub = _attn_blocks(s_len, MLA_TK)
    chains = _chains(tq, th)
    nh = tq // th
    nq = s_len // tq

    def body(q_ref, k_ref, v_ref, *rest):
        o_ref, lse_ref = rest[ride.n:ride.n + 2]
        ride.bind(rest[:ride.n], rest[ride.n + 2:2 * ride.n + 2], rest[2 * ride.n + 2:])
        pl.when((pl.program_id(0) == 0) & (pl.program_id(1) == 0))(ride.issue)
        i = pl.program_id(1)
        lane = lax.broadcasted_iota(jnp.int32, (th, LANES), 1)
        hsl = [slice(hh * HEAD_PAD, (hh + 1) * HEAD_PAD) for hh in range(2)]

        def step(kb, carry, sub):
            rows = pl.ds(pl.multiple_of(kb * tk, tk), tk)
            vblk = v_ref[rows, :]
            masks = ["all" if sub is None else _diag_mask(th, tk, r0, sub, strict=False) for _, r0 in chains]
            live = [n for n, m in enumerate(masks) if not (isinstance(m, str) and m == "none")]
            s = {n: _dot_nt(q_ref[chains[n][1]:chains[n][1] + th, hsl[chains[n][0]]], k_ref[rows, hsl[chains[n][0]]])
                 for n in live}
            new = list(carry)
            pb, alpha = {}, {}
            for n in live:
                m, l, _ = carry[n]
                sn = s[n]
                if not isinstance(masks[n], str):
                    sn = jnp.where(masks[n], sn, NEG_BIG)
                m_new = jnp.maximum(m, jnp.max(sn, axis=-1, keepdims=True))
                alpha[n] = jnp.exp2(m - m_new)
                p = jnp.exp2(sn - m_new)
                pb[n] = p.astype(BF16)
                new[n] = (m_new, alpha[n] * l + jnp.sum(p, axis=-1, keepdims=True), None)
            pv = {n: _dot(pb[n], vblk) for n in live}
            for n in live:
                new[n] = (new[n][0], new[n][1], alpha[n] * carry[n][2] + pv[n])
            return tuple(new)

        init = (jnp.full((th, 1), NEG_BIG, F32), jnp.zeros((th, 1), F32), jnp.zeros((th, LANES), F32))
        carry = lax.fori_loop(0, i * nsub, lambda kb, cy: step(kb, cy, None), (init,) * len(chains))
        for sub in range(nsub):
            carry = step(i * nsub + sub, carry, sub)
        for c in range(nh):
            (m0, l0, a0), (m1, l1, a1) = carry[c], carry[nh + c]
            rs = slice(c * th, (c + 1) * th)
            o_ref[rs, :] = jnp.where(lane < 64, a0 / l0, a1 / l1).astype(BF16)
            lse_ref[rs, :] = jnp.where(lane < 64, m0 + jnp.log2(l0), m1 + jnp.log2(l1))
        pl.when((pl.program_id(0) == 3) & (i == nq - 1))(ride.finish)

    outs = pl.pallas_call(
        body, name="mla_fwd", grid=(4, nq),
        in_specs=[pl.BlockSpec((tq, 2 * HEAD_PAD), lambda p, i: (i, p)),
                  pl.BlockSpec((s_len, 2 * HEAD_PAD), lambda p, i: (0, p)),
                  pl.BlockSpec((s_len, LANES), lambda p, i: (0, p))] + ride.specs,
        out_specs=[pl.BlockSpec((tq, LANES), lambda p, i: (i, p)),
                   pl.BlockSpec((None, tq, LANES), lambda p, i: (p, i, 0))] + ride.specs,
        out_shape=[jax.ShapeDtypeStruct((s_len, 512), BF16),
                   jax.ShapeDtypeStruct((4, s_len, LANES), F32)] + ride.out_shape,
        scratch_shapes=ride.scratch,
        compiler_params=_cparams(40, dimension_semantics=("arbitrary", "arbitrary")),
    )(q, k, v, *ride.srcs)
    return outs[0], outs[1], outs[2:]


def _mla_bwd(q, k, v, o, do, lse, ride):
    s_len = q.shape[0]
    tq, th, tk, nsub = _attn_blocks(s_len, MLA_TK)
    chains = _chains(tq, th)
    nq = s_len // tq

    def body(q_ref, k_ref, v_ref, o_ref, do_ref, lse_ref, *rest):
        dq_ref, dk_ref, dv_ref = rest[ride.n:ride.n + 3]
        ride.bind(rest[:ride.n], rest[ride.n + 3:2 * ride.n + 3], rest[2 * ride.n + 3:])
        pl.when((pl.program_id(0) == 0) & (pl.program_id(1) == 0))(ride.issue)
        i = pl.program_id(1)
        lane = lax.broadcasted_iota(jnp.int32, (th, LANES), 1)

        @pl.when(i == 0)
        def _():
            dk_ref[...] = jnp.zeros_like(dk_ref)
            dv_ref[...] = jnp.zeros_like(dv_ref)

        hsl = [slice(hh * HEAD_PAD, (hh + 1) * HEAD_PAD) for hh in range(2)]
        qs, dos, deltas, lses = [], [], [], []
        for hh, r0 in chains:
            rs = slice(r0, r0 + th)
            qs.append(q_ref[rs, hsl[hh]])
            doh = jnp.where((lane // 64) == hh, do_ref[rs, :], jnp.zeros((), BF16))
            dos.append(doh)
            deltas.append(jnp.sum(doh.astype(F32) * o_ref[rs, :].astype(F32), axis=-1, keepdims=True))
            lses.append(lse_ref[rs, 64 * hh:64 * hh + 1])

        def step(kb, dqs, sub):
            rows = pl.ds(pl.multiple_of(kb * tk, tk), tk)
            vblk = v_ref[rows, :]
            new, p_all, do_all = [], [], []
            ds_h, q_h = [[], []], [[], []]
            for c, (hh, r0) in enumerate(chains):
                mask = "all" if sub is None else _diag_mask(th, tk, r0, sub, strict=False)
                if isinstance(mask, str) and mask == "none":
                    new.append(dqs[c])
                    continue
                kblk = k_ref[rows, hsl[hh]]
                s = _dot_nt(qs[c], kblk)
                if not isinstance(mask, str):
                    s = jnp.where(mask, s, NEG_BIG)
                p = jnp.exp2(s - lses[c])
                dp = _dot_nt(dos[c], vblk)
                ds = (p * (dp - deltas[c]) * MLA_SCALE).astype(BF16)
                p_all.append(p.astype(BF16))
                do_all.append(dos[c])
                ds_h[hh].append(ds)
                q_h[hh].append(qs[c])
                new.append(dqs[c] + _dot(ds, kblk))
            dv_ref[rows, :] += _dot_tn(jnp.concatenate(p_all, axis=0), jnp.concatenate(do_all, axis=0))
            for hh in range(2):
                dk_ref[rows, hsl[hh]] += _dot_tn(jnp.concatenate(ds_h[hh], axis=0),
                                                 jnp.concatenate(q_h[hh], axis=0))
            return tuple(new)

        zero = jnp.zeros((th, LANES), F32)
        dqs = lax.fori_loop(0, i * nsub, lambda kb, cy: step(kb, cy, None), (zero,) * len(chains))
        for sub in range(nsub):
            dqs = step(i * nsub + sub, dqs, sub)
        for c, (hh, r0) in enumerate(chains):
            dq_ref[r0:r0 + th, hsl[hh]] = dqs[c]
        pl.when((pl.program_id(0) == 3) & (i == nq - 1))(ride.finish)

    outs = pl.pallas_call(
        body, name="mla_bwd", grid=(4, nq),
        in_specs=[pl.BlockSpec((tq, 2 * HEAD_PAD), lambda p, i: (i, p)),
                  pl.BlockSpec((s_len, 2 * HEAD_PAD), lambda p, i: (0, p)),
                  pl.BlockSpec((s_len, LANES), lambda p, i: (0, p)),
                  pl.BlockSpec((tq, LANES), lambda p, i: (i, p)),
                  pl.BlockSpec((tq, LANES), lambda p, i: (i, p)),
                  pl.BlockSpec((None, tq, LANES), lambda p, i: (p, i, 0))] + ride.specs,
        out_specs=[pl.BlockSpec((tq, 2 * HEAD_PAD), lambda p, i: (i, p)),
                   pl.BlockSpec((s_len, 2 * HEAD_PAD), lambda p, i: (0, p)),
                   pl.BlockSpec((s_len, LANES), lambda p, i: (0, p))] + ride.specs,
        out_shape=[jax.ShapeDtypeStruct((s_len, 1024), F32), jax.ShapeDtypeStruct((s_len, 1024), F32),
                   jax.ShapeDtypeStruct((s_len, 512), F32)] + ride.out_shape,
        scratch_shapes=ride.scratch,
        compiler_params=_cparams(56, dimension_semantics=("arbitrary", "arbitrary")),
    )(q, k, v, o, do, lse, *ride.srcs)
    return outs[0], outs[1], outs[2], outs[3:]


def _log_sigmoids(z2):
    sp = jnp.log2(1.0 + jnp.exp2(-jnp.abs(z2)))
    lb = jnp.minimum(z2, 0.0) - sp
    return lb, lb - z2


def _split_dot(x, w, parts, nt=False):
    dot = _dot_nt if nt else _dot
    out = None
    for _ in range(parts):
        xb = x.astype(BF16)
        t = dot(xb, w)
        out = t if out is None else out + t
        x = x - xb.astype(F32)
    return out


def _sb_fwd(sb):
    s_len = sb.shape[0]
    tq, th, tk, nsub = _attn_blocks(s_len, tq=SB_TQ, th=SB_TH)
    chains = _chains(tq, th)
    nh = tq // th
    assert s_len // tk <= 64

    def body(q_ref, k_ref, v_ref, o_ref, r_ref):
        i = pl.program_id(1)
        lane = lax.broadcasted_iota(jnp.int32, (th, LANES), 1)
        upper = (lax.broadcasted_iota(jnp.int32, (tk, tk), 0)
                 > lax.broadcasted_iota(jnp.int32, (tk, tk), 1)).astype(BF16)
        qs = [jnp.where((lane // 64) == hh, q_ref[r0:r0 + th, :], jnp.zeros((), BF16)) for hh, r0 in chains]

        def step(kb, carry, sub):
            rows = pl.ds(pl.multiple_of(kb * tk, tk), tk)
            kblk, vblk = k_ref[rows, :], v_ref[rows, :]
            masks = ["all" if sub is None else _diag_mask(th, tk, r0, sub, strict=True) for _, r0 in chains]
            live = [n for n, m in enumerate(masks) if not (isinstance(m, str) and m == "none")]
            masked = {n: not isinstance(masks[n], str) for n in live}
            z = {n: _dot_nt(qs[n], kblk) for n in live}
            lb, lom = {}, {}
            for n in live:
                lb[n], lom[n] = _log_sigmoids(z[n])
                if masked[n]:
                    lom[n] = jnp.where(masks[n], lom[n], 0.0)
            suf = {n: _split_dot(lom[n], upper, 2) for n in live}
            a = {}
            for n in live:
                a[n] = jnp.exp2(lb[n] + suf[n] + carry[n][0])
                if masked[n]:
                    a[n] = jnp.where(masks[n], a[n], 0.0)
            pv = {n: _dot(a[n].astype(BF16), vblk) for n in live}
            new = list(carry)
            for n in live:
                c, acc, r = carry[n]
                rs = suf[n][:, 0:1] + lom[n][:, 0:1]
                new[n] = (c + rs, acc + pv[n], jnp.where(lane == 64 * chains[n][0] + kb, rs, r))
            return tuple(new)

        init = (jnp.zeros((th, 1), F32), jnp.zeros((th, LANES), F32), jnp.zeros((th, LANES), F32))
        carry = (init,) * len(chains)
        for sub in reversed(range(nsub)):
            carry = step(i * nsub + sub, carry, sub)

        def spent(cy):
            top = functools.reduce(jnp.maximum, [jnp.max(c) for c, _, _ in cy])
            return (top < SB_CUT).astype(jnp.int32)

        def walk(state):
            t, _, cy = state
            cy = step(i * nsub - 1 - t, cy, None)
            return t + 1, spent(cy), cy

        _, _, carry = lax.while_loop(lambda st: (st[0] < i * nsub) & (st[1] == 0), walk,
                                     (jnp.int32(0), spent(carry), carry))
        for n in range(nh):
            rs = slice(n * th, (n + 1) * th)
            o_ref[rs, :] = jnp.where(lane < 64, carry[n][1], carry[nh + n][1]).astype(BF16)
            r_ref[rs, :] = jnp.where(lane < 64, carry[n][2], carry[nh + n][2])

    return pl.pallas_call(
        body, name="sb_fwd", grid=(4, s_len // tq),
        in_specs=[pl.BlockSpec((tq, LANES), lambda p, i: (i, p)),
                  pl.BlockSpec((s_len, LANES), lambda p, i: (0, 4 + p)),
                  pl.BlockSpec((s_len, LANES), lambda p, i: (0, 8 + p))],
        out_specs=[pl.BlockSpec((tq, LANES), lambda p, i: (i, p)),
                   pl.BlockSpec((None, tq, LANES), lambda p, i: (p, i, 0))],
        out_shape=[jax.ShapeDtypeStruct((s_len, 512), BF16), jax.ShapeDtypeStruct((4, s_len, LANES), F32)],
        compiler_params=_cparams(40, dimension_semantics=("parallel", "arbitrary")),
    )(sb, sb, sb)


def _sb_bwd(sb, do, r):
    s_len = sb.shape[0]
    tq, th, tk, nsub = _attn_blocks(s_len)
    chains = _chains(tq, th)
    nh = tq // th

    def body(q_ref, k_ref, v_ref, do_ref, r_ref, dq_ref, dk_ref, dv_ref):
        i = pl.program_id(1)
        lane = lax.broadcasted_iota(jnp.int32, (th, LANES), 1)
        upper = (lax.broadcasted_iota(jnp.int32, (tk, tk), 0)
                 > lax.broadcasted_iota(jnp.int32, (tk, tk), 1)).astype(BF16)
        tri = (lax.broadcasted_iota(jnp.int32, (LANES, LANES), 0)
               > lax.broadcasted_iota(jnp.int32, (LANES, LANES), 1)).astype(BF16)

        @pl.when(i == 0)
        def _():
            dk_ref[...] = jnp.zeros_like(dk_ref)
            dv_ref[...] = jnp.zeros_like(dv_ref)

        qs, dos, rights = [], [], []
        for hh, r0 in chains:
            rs = slice(r0, r0 + th)
            hm = (lane // 64) == hh
            qs.append(jnp.where(hm, q_ref[rs, :], jnp.zeros((), BF16)))
            dos.append(jnp.where(hm, do_ref[rs, :], jnp.zeros((), BF16)))
            rights.append(_split_dot(jnp.where(hm, r_ref[rs, :], 0.0), tri, 3))

        def step(kb, carry, sub):
            rows = pl.ds(pl.multiple_of(kb * tk, tk), tk)
            kblk, vblk = k_ref[rows, :], v_ref[rows, :]
            new, a_all, do_all, dz_all, q_all = [], [], [], [], []
            for n, ((hh, r0), (pre, dq)) in enumerate(zip(chains, carry)):
                mask = "all" if sub is None else _diag_mask(th, tk, r0, sub, strict=True)
                if isinstance(mask, str) and mask == "none":
                    new.append((pre, dq))
                    continue
                c = jnp.sum(jnp.where(lane == 64 * hh + kb, rights[n], 0.0), axis=-1, keepdims=True)
                z = _dot_nt(qs[n], kblk)
                lb, lom = _log_sigmoids(z)
                if not isinstance(mask, str):
                    lom = jnp.where(mask, lom, 0.0)
                suf = _split_dot(lom, upper, 2)
                a = jnp.exp2(lb + suf + c)
                if not isinstance(mask, str):
                    a = jnp.where(mask, a, 0.0)
                g = a * _dot_nt(dos[n], vblk)
                left = _split_dot(g, upper, 1, nt=True) + pre
                sig = jnp.exp2(lb)
                dz = g * (1.0 - sig) - sig * left
                if not isinstance(mask, str):
                    dz = jnp.where(mask, dz, 0.0)
                dzb = dz.astype(BF16)
                a_all.append(a.astype(BF16))
                do_all.append(dos[n])
                dz_all.append(dzb)
                q_all.append(qs[n])
                new.append((left[:, tk - 1:tk] + g[:, tk - 1:tk], dq + _dot(dzb, kblk)))
            dv_ref[rows, :] += _dot_tn(jnp.concatenate(a_all, axis=0), jnp.concatenate(do_all, axis=0))
            dk_ref[rows, :] += _dot_tn(jnp.concatenate(dz_all, axis=0), jnp.concatenate(q_all, axis=0))
            return tuple(new)

        lane1 = lax.broadcasted_iota(jnp.int32, (1, LANES), 1)
        first = i * nsub
        for n, (hh, _) in enumerate(chains):
            top = jnp.max(rights[n], axis=0, keepdims=True)
            kb_of = lane1 - 64 * hh
            live = (kb_of >= 0) & (kb_of < i * nsub) & (top >= SB_CUT)
            first = jnp.minimum(first, jnp.min(jnp.where(live, kb_of, i * nsub)))

        init = (jnp.zeros((th, 1), F32), jnp.zeros((th, LANES), F32))
        carry = lax.fori_loop(first, i * nsub, lambda kb, cy: step(kb, cy, None), (init,) * len(chains))
        for sub in range(nsub):
            carry = step(i * nsub + sub, carry, sub)
        for n in range(nh):
            dq_ref[n * th:(n + 1) * th, :] = jnp.where(lane < 64, carry[n][1], carry[nh + n][1]) * SB_SCALE

    return pl.pallas_call(
        body, name="sb_bwd", grid=(4, s_len // tq),
        in_specs=[pl.BlockSpec((tq, LANES), lambda p, i: (i, p)),
                  pl.BlockSpec((s_len, LANES), lambda p, i: (0, 4 + p)),
                  pl.BlockSpec((s_len, LANES), lambda p, i: (0, 8 + p)),
                  pl.BlockSpec((tq, LANES), lambda p, i: (i, p)),
                  pl.BlockSpec((None, tq, LANES), lambda p, i: (p, i, 0))],
        out_specs=[pl.BlockSpec((tq, LANES), lambda p, i: (i, p)),
                   pl.BlockSpec((s_len, LANES), lambda p, i: (0, p)),
                   pl.BlockSpec((s_len, LANES), lambda p, i: (0, p))],
        out_shape=[jax.ShapeDtypeStruct((s_len, 512), F32)] * 3,
        compiler_params=_cparams(48, dimension_semantics=("arbitrary", "arbitrary")),
    )(sb, sb, sb, do, r)


def _merge_fwd(x, oa, ob, gates, bg, wa, wb, wo):
    s_len = x.shape[0]
    tm = _row_block(s_len)

    def body(x_ref, oa_ref, ob_ref, g_ref, bg_ref, wa_ref, wb_ref, wo_ref, y_ref):
        pa = _dot(oa_ref[...], wa_ref[...])
        pb = _dot(ob_ref[...], wb_ref[...])
        merged = (_sigmoid(g_ref[:, 0:D_MODEL] + bg_ref[0:1, :]) * pa
                  + _sigmoid(g_ref[:, D_MODEL:2 * D_MODEL] + bg_ref[1:2, :]) * pb)
        y_ref[...] = x_ref[...] + _dot(merged.astype(BF16), wo_ref[...])

    full = lambda shape: pl.BlockSpec(shape, lambda i: (0, 0))
    rowb = lambda n: pl.BlockSpec((tm, n), lambda i: (i, 0))
    return pl.pallas_call(
        body, name="merge_fwd", grid=(s_len // tm,),
        in_specs=[rowb(1024), rowb(512), rowb(512), rowb(2048), full((2, 1024)), full((512, 1024)),
                  full((512, 1024)), full((1024, 1024))],
        out_specs=rowb(1024),
        out_shape=jax.ShapeDtypeStruct((s_len, D_MODEL), F32),
        compiler_params=_cparams(48, dimension_semantics=("parallel",)),
    )(x, oa, ob, gates, bg, wa, wb, wo)


def _merge_bwd(dx1, oa, ob, gates, bg, wa, wb, wo):
    s_len = dx1.shape[0]
    tm = _row_block(s_len)

    def body(dx_ref, oa_ref, ob_ref, g_ref, bg_ref, wa_ref, wb_ref, wo_ref,
             doa_ref, dob_ref, dgate_ref, dpa_ref, dpb_ref, merged_ref, dxb_ref, dbg_ref):
        first = pl.program_id(0) == 0
        dxb = dx_ref[...].astype(BF16)
        dxb_ref[...] = dxb
        pa = _dot(oa_ref[...], wa_ref[...])
        pb = _dot(ob_ref[...], wb_ref[...])
        sa = _sigmoid(g_ref[:, 0:D_MODEL] + bg_ref[0:1, :])
        sbg = _sigmoid(g_ref[:, D_MODEL:2 * D_MODEL] + bg_ref[1:2, :])
        merged_ref[...] = (sa * pa + sbg * pb).astype(BF16)
        dm = _dot_nt(dxb, wo_ref[...])
        dpa = (dm * sa).astype(BF16)
        dpb = (dm * sbg).astype(BF16)
        dpa_ref[...] = dpa
        dpb_ref[...] = dpb
        dga = dm * pa * sa * (1.0 - sa)
        dgb = dm * pb * sbg * (1.0 - sbg)
        dgate_ref[:, 0:D_MODEL] = dga.astype(BF16)
        dgate_ref[:, D_MODEL:2 * D_MODEL] = dgb.astype(BF16)
        _acc_rows(dbg_ref.at[0:1, :], dga, first)
        _acc_rows(dbg_ref.at[1:2, :], dgb, first)
        doa_ref[...] = _dot_nt(dpa, wa_ref[...]).astype(BF16)
        dob_ref[...] = _dot_nt(dpb, wb_ref[...]).astype(BF16)

    full = lambda shape: pl.BlockSpec(shape, lambda i: (0, 0))
    rowb = lambda n: pl.BlockSpec((tm, n), lambda i: (i, 0))
    sds = lambda n, dt: jax.ShapeDtypeStruct((s_len, n), dt)
    return pl.pallas_call(
        body, name="merge_bwd", grid=(s_len // tm,),
        in_specs=[rowb(1024), rowb(512), rowb(512), rowb(2048), full((2, 1024)), full((512, 1024)),
                  full((512, 1024)), full((1024, 1024))],
        out_specs=[rowb(512), rowb(512), rowb(2048), rowb(1024), rowb(1024), rowb(1024), rowb(1024),
                   full((2, 1024))],
        out_shape=[sds(512, BF16), sds(512, BF16), sds(2048, BF16), sds(1024, BF16), sds(1024, BF16),
                   sds(1024, BF16), sds(1024, BF16), jax.ShapeDtypeStruct((2, 1024), F32)],
        compiler_params=_cparams(48, dimension_semantics=("arbitrary",)),
    )(dx1, oa, ob, gates, bg, wa, wb, wo)


def _mem_kv(mem, g, w):
    m_len = mem.shape[0]

    def body(mem_ref, g_ref, w_ref, mn_ref, kv_ref):
        mn, _, _ = _rms(mem_ref[...], g_ref[...])
        mnb = mn.astype(BF16)
        mn_ref[...] = mnb
        kv_ref[...] = _dot(mnb, w_ref[...]).astype(BF16)

    return pl.pallas_call(
        body, name="mem_kv",
        out_shape=[jax.ShapeDtypeStruct((m_len, D_MODEL), BF16), jax.ShapeDtypeStruct((m_len, 1024), BF16)],
    )(mem, g, w)


def _mem_bwd(mem, g, w, mn, dkv):
    def body(mem_ref, g_ref, w_ref, mn_ref, dkv_ref, dw_ref, dg_ref):
        dkvb = dkv_ref[...].astype(BF16)
        dw_ref[...] = _dot_tn(mn_ref[...], dkvb)
        dmn = _dot_nt(dkvb, w_ref[...])
        _, xh, _ = _rms(mem_ref[...], g_ref[...])
        dg_ref[...] = jnp.sum(dmn * xh, axis=0, keepdims=True)

    return pl.pallas_call(
        body, name="mem_bwd",
        out_shape=[jax.ShapeDtypeStruct((D_MODEL, 1024), F32), jax.ShapeDtypeStruct((1, D_MODEL), F32)],
    )(mem, g, w, mn, dkv)


def _xattn_heads(xqb, kv_ref, m_len):
    ps = []
    for h in range(X_HEADS):
        hs = slice(h * X_HEAD_DIM, (h + 1) * X_HEAD_DIM)
        s = _dot_nt(xqb[:, hs], kv_ref[:, hs]) * X_SCALE
        e = jnp.exp(s - jnp.max(s, axis=-1, keepdims=True))
        ps.append(e / jnp.sum(e, axis=-1, keepdims=True))
    return ps


def _xattn_fwd(x1, g, wxq, kv, wxo):
    s_len, m_len = x1.shape[0], kv.shape[0]
    tm = _row_block(s_len)

    def body(x_ref, g_ref, wq_ref, kv_ref, wo_ref, y_ref):
        hx, _, _ = _rms(x_ref[...], g_ref[...])
        xqb = _dot(hx.astype(BF16), wq_ref[...]).astype(BF16)
        ps = _xattn_heads(xqb, kv_ref, m_len)
        xo = jnp.concatenate(
            [_dot(ps[h].astype(BF16), kv_ref[:, 512 + h * X_HEAD_DIM:512 + (h + 1) * X_HEAD_DIM])
             for h in range(X_HEADS)], axis=-1)
        y_ref[...] = x_ref[...] + _dot(xo.astype(BF16), wo_ref[...])

    full = lambda shape: pl.BlockSpec(shape, lambda i: (0, 0))
    rowb = lambda n: pl.BlockSpec((tm, n), lambda i: (i, 0))
    return pl.pallas_call(
        body, name="xattn_fwd", grid=(s_len // tm,),
        in_specs=[rowb(1024), full((1, 1024)), full((1024, 512)), full((m_len, 1024)), full((512, 1024))],
        out_specs=rowb(1024),
        out_shape=jax.ShapeDtypeStruct((s_len, D_MODEL), F32),
        compiler_params=_cparams(48, dimension_semantics=("parallel",)),
    )(x1, g, wxq, kv, wxo)


def _xattn_bwd(x1, dx2, g, wxq, kv, wxo):
    s_len, m_len = x1.shape[0], kv.shape[0]
    tm = _row_block(s_len)

    def body(x_ref, dy_ref, g_ref, wq_ref, kv_ref, wo_ref, dx_ref, dwq_ref, dwo_ref, dkv_ref, dg_ref):
        first = pl.program_id(0) == 0
        gv = g_ref[...]
        hx, xh, r = _rms(x_ref[...], gv)
        hxb = hx.astype(BF16)
        xqb = _dot(hxb, wq_ref[...]).astype(BF16)
        ps = _xattn_heads(xqb, kv_ref, m_len)
        dy = dy_ref[...]
        dyb = dy.astype(BF16)
        dxo = _dot_nt(dyb, wo_ref[...])
        xos, dqs, dks, dvs = [], [], [], []
        for h in range(X_HEADS):
            hs = slice(h * X_HEAD_DIM, (h + 1) * X_HEAD_DIM)
            vs = slice(512 + h * X_HEAD_DIM, 512 + (h + 1) * X_HEAD_DIM)
            p = ps[h]
            pb = p.astype(BF16)
            dxoh = dxo[:, hs].astype(BF16)
            xos.append(_dot(pb, kv_ref[:, vs]))
            dp = _dot_nt(dxoh, kv_ref[:, vs])
            ds = (p * (dp - jnp.sum(dp * p, axis=-1, keepdims=True)) * X_SCALE).astype(BF16)
            dvs.append(_dot_tn(pb, dxoh))
            dks.append(_dot_tn(ds, xqb[:, hs]))
            dqs.append(_dot(ds, kv_ref[:, hs]))
        xob = jnp.concatenate(xos, axis=-1).astype(BF16)
        dxqb = jnp.concatenate(dqs, axis=-1).astype(BF16)
        _acc(dwo_ref, _dot_tn(xob, dyb), first)
        _acc(dwq_ref, _dot_tn(hxb, dxqb), first)
        _acc(dkv_ref, jnp.concatenate(dks + dvs, axis=-1), first)
        dhx = _dot_nt(dxqb, wq_ref[...])
        dx, dgr = _rms_bwd(dhx, xh, r, gv)
        dx_ref[...] = dy + dx
        _acc_rows(dg_ref, dgr, first)

    full = lambda shape: pl.BlockSpec(shape, lambda i: (0, 0))
    rowb = lambda n: pl.BlockSpec((tm, n), lambda i: (i, 0))
    return pl.pallas_call(
        body, name="xattn_bwd", grid=(s_len // tm,),
        in_specs=[rowb(1024), rowb(1024), full((1, 1024)), full((1024, 512)), full((m_len, 1024)),
                  full((512, 1024))],
        out_specs=[rowb(1024), full((1024, 512)), full((512, 1024)), full((m_len, 1024)), full((1, 1024))],
        out_shape=[jax.ShapeDtypeStruct((s_len, D_MODEL), F32), jax.ShapeDtypeStruct((1024, 512), F32),
                   jax.ShapeDtypeStruct((512, 1024), F32), jax.ShapeDtypeStruct((m_len, 1024), F32),
                   jax.ShapeDtypeStruct((1, D_MODEL), F32)],
        compiler_params=_cparams(48, dimension_semantics=("arbitrary",)),
    )(x1, dx2, g, wxq, kv, wxo)


FF_TILE = 1408


def _ffn_fwd(x2, g, wg, wu, wd):
    s_len = x2.shape[0]
    tm, tf = _row_block(s_len), FF_TILE

    def body(x_ref, g_ref, wg_ref, wu_ref, wd_ref, y_ref, h_ref):
        j = pl.program_id(1)

        @pl.when(j == 0)
        def _():
            hf, _, _ = _rms(x_ref[...], g_ref[...])
            h_ref[...] = hf.astype(BF16)
            y_ref[...] = x_ref[...]

        hb = h_ref[...]
        gt = _dot(hb, wg_ref[...])
        up = _dot(hb, wu_ref[...])
        act = gt * _sigmoid(gt) * up
        y_ref[...] += _dot(act.astype(BF16), wd_ref[...])

    rowb = pl.BlockSpec((tm, D_MODEL), lambda i, j: (i, 0))
    return pl.pallas_call(
        body, name="ffn_fwd", grid=(s_len // tm, D_FF // tf),
        in_specs=[rowb, pl.BlockSpec((1, D_MODEL), lambda i, j: (0, 0)),
                  pl.BlockSpec((D_MODEL, tf), lambda i, j: (0, j)),
                  pl.BlockSpec((D_MODEL, tf), lambda i, j: (0, j)),
                  pl.BlockSpec((tf, D_MODEL), lambda i, j: (j, 0))],
        out_specs=[rowb, rowb],
        out_shape=[jax.ShapeDtypeStruct((s_len, D_MODEL), F32), jax.ShapeDtypeStruct((s_len, D_MODEL), BF16)],
        compiler_params=_cparams(48, dimension_semantics=("parallel", "arbitrary")),
    )(x2, g, wg, wu, wd)


def _ffn_bwd(x2, hf, dx3, dx3b, g, wg, wu, wd):
    s_len = x2.shape[0]
    tm, tf = _row_block(s_len), FF_TILE
    nf = D_FF // tf

    def act_body(h_ref, dy_ref, wg_ref, wu_ref, wd_ref, dgt_ref, dup_ref, act_ref):
        hb = h_ref[...]
        gt = _dot(hb, wg_ref[...])
        up = _dot(hb, wu_ref[...])
        sg = _sigmoid(gt)
        silu = gt * sg
        dact = _dot_nt(dy_ref[...], wd_ref[...])
        dgt_ref[...] = (dact * up * (sg * (1.0 + gt * (1.0 - sg)))).astype(BF16)
        dup_ref[...] = (dact * silu).astype(BF16)
        act_ref[...] = (silu * up).astype(BF16)

    rowb = pl.BlockSpec((tm, D_MODEL), lambda i, j: (i, 0))
    ffb = pl.BlockSpec((tm, tf), lambda i, j: (i, j))
    dgt, dup, act = pl.pallas_call(
        act_body, name="ffn_bwd_act", grid=(s_len // tm, nf),
        in_specs=[rowb, rowb,
                  pl.BlockSpec((D_MODEL, tf), lambda i, j: (0, j)),
                  pl.BlockSpec((D_MODEL, tf), lambda i, j: (0, j)),
                  pl.BlockSpec((tf, D_MODEL), lambda i, j: (j, 0))],
        out_specs=[ffb, ffb, ffb],
        out_shape=[jax.ShapeDtypeStruct((s_len, D_FF), BF16)] * 3,
        compiler_params=_cparams(56, dimension_semantics=("parallel", "arbitrary")),
    )(hf, dx3b, wg, wu, wd)
    tm = min(s_len, 256)

    def in_body(x_ref, dy_ref, g_ref, wg_ref, wu_ref, dgt_ref, dup_ref, dx_ref, dg_ref):
        dh = _dot_nt(dgt_ref[...], wg_ref[...]) + _dot_nt(dup_ref[...], wu_ref[...])
        gv = g_ref[...]
        _, xh, r = _rms(x_ref[...], gv)
        dx, dgr = _rms_bwd(dh, xh, r, gv)
        dx_ref[...] = dy_ref[...] + dx
        _acc_rows(dg_ref, dgr, pl.program_id(0) == 0)

    row1 = lambda n: pl.BlockSpec((tm, n), lambda i: (i, 0))
    full = lambda shape: pl.BlockSpec(shape, lambda i: (0, 0))
    dx2, dg = pl.pallas_call(
        in_body, name="ffn_bwd_in", grid=(s_len // tm,),
        in_specs=[row1(D_MODEL), row1(D_MODEL), full((1, D_MODEL)), full((D_MODEL, D_FF)), full((D_MODEL, D_FF)),
                  row1(D_FF), row1(D_FF)],
        out_specs=[row1(D_MODEL), full((1, D_MODEL))],
        out_shape=[jax.ShapeDtypeStruct((s_len, D_MODEL), F32), jax.ShapeDtypeStruct((1, D_MODEL), F32)],
        compiler_params=_cparams(48, dimension_semantics=("arbitrary",)),
    )(x2, dx3, g, wg, wu, dgt, dup)
    return dx2, dgt, dup, act, dg


def _loss_head(x3, g, target):
    s_len = x3.shape[0]
    tm = _row_block(s_len)

    def body(x_ref, g_ref, t_ref, sse_ref, dx_ref, dxb_ref, dg_ref):
        first = pl.program_id(0) == 0
        gv = g_ref[...]
        y, xh, r = _rms(x_ref[...], gv)
        err = y - t_ref[...]
        _acc(sse_ref, jnp.broadcast_to(jnp.sum(err * err), (8, LANES)), first)
        dx, dgr = _rms_bwd(err * (1.0 / D_MODEL), xh, r, gv)
        dx_ref[...] = dx
        dxb_ref[...] = dx.astype(BF16)
        _acc_rows(dg_ref, dgr, first)

    rowb = pl.BlockSpec((tm, D_MODEL), lambda i: (i, 0))
    return pl.pallas_call(
        body, name="loss_head", grid=(s_len // tm,),
        in_specs=[rowb, pl.BlockSpec((1, D_MODEL), lambda i: (0, 0)), rowb],
        out_specs=[pl.BlockSpec((8, LANES), lambda i: (0, 0)), rowb, rowb,
                   pl.BlockSpec((1, D_MODEL), lambda i: (0, 0))],
        out_shape=[jax.ShapeDtypeStruct((8, LANES), F32), jax.ShapeDtypeStruct((s_len, D_MODEL), F32),
                   jax.ShapeDtypeStruct((s_len, D_MODEL), BF16), jax.ShapeDtypeStruct((1, D_MODEL), F32)],
        compiler_params=_cparams(dimension_semantics=("arbitrary",)),
    )(x3, g, target)


def _mla_prep_bwd(lat, g_q, g_kv, w_uq, w_uk, w_uv, cosf, sinf, dq, dk, dv):
    s_len = lat.shape[0]
    tm = _row_block(s_len)

    def body(lat_ref, gq_ref, gkv_ref, wuq_ref, wuk_ref, wuv_ref, cos_ref, sin_ref, dq_ref, dk_ref, dv_ref,
             dlat_ref, dqb_ref, dkb_ref, dvb_ref, dgq_ref, dgkv_ref):
        first = pl.program_id(0) == 0
        lane = lax.broadcasted_iota(jnp.int32, (tm, LANES), 1)
        cosv, sinv = cos_ref[...], sin_ref[...]
        gq, gkv = gq_ref[...], gkv_ref[...]
        _, qxh, qr = _rms(lat_ref[:, 0:256], gq)
        _, kxh, kr_ = _rms(lat_ref[:, 256:384], gkv)
        dkr = jnp.zeros((tm, LANES), F32)
        for h in range(MLA_HEADS):
            sl = slice(h * HEAD_PAD, (h + 1) * HEAD_PAD)
            blk = dq_ref[:, sl]
            dqb_ref[:, sl] = (blk * cosv + _rope_rot_t(blk, lane) * sinv).astype(BF16)
            kblk = dk_ref[:, sl] * (1.0 / MLA_Q_FOLD)
            dkb_ref[:, sl] = kblk.astype(BF16)
            dkr = dkr + kblk
        dvb = dv_ref[...].astype(BF16)
        dvb_ref[...] = dvb
        dkr = jnp.where((lane >= 64) & (lane < 96), dkr, 0.0)
        dkr = dkr * cosv + _rope_rot_t(dkr, lane) * sinv
        dql = _dot_nt(dqb_ref[...], wuq_ref[...])
        dkvl = _dot_nt(dkb_ref[...], wuk_ref[...]) + _dot_nt(dvb, wuv_ref[...])
        dcq, dgqr = _rms_bwd(dql, qxh, qr, gq)
        dckv, dgkvr = _rms_bwd(dkvl, kxh, kr_, gkv)
        dlat_ref[:, 0:256] = dcq
        dlat_ref[:, 256:384] = dckv
        dlat_ref[:, K_R_OFF:K_R_OFF + LANES] = pltpu.roll(dkr, 64, 1)
        _acc_rows(dgq_ref, dgqr, first)
        _acc_rows(dgkv_ref, dgkvr, first)

    full = lambda shape: pl.BlockSpec(shape, lambda i: (0, 0))
    rowb = lambda n: pl.BlockSpec((tm, n), lambda i: (i, 0))
    sds = lambda n, dt: jax.ShapeDtypeStruct((s_len, n), dt)
    return pl.pallas_call(
        body, name="mla_prep_bwd", grid=(s_len // tm,),
        in_specs=[rowb(512), full((1, 256)), full((1, 128)), full((256, 1024)), full((128, 1024)),
                  full((128, 512)), rowb(128), rowb(128), rowb(1024), rowb(1024), rowb(512)],
        out_specs=[rowb(512), rowb(1024), rowb(1024), rowb(512), full((1, 256)), full((1, 128))],
        out_shape=[sds(512, F32), sds(1024, BF16), sds(1024, BF16), sds(512, BF16),
                   jax.ShapeDtypeStruct((1, 256), F32), jax.ShapeDtypeStruct((1, 128), F32)],
        compiler_params=_cparams(48, dimension_semantics=("arbitrary",)),
    )(lat, g_q, g_kv, w_uq, w_uk, w_uv, cosf, sinf, dq, dk, dv)


def _in_proj_bwd(x, g, w, dx1, dlat, dsbq, dsbk, dsbv, dgates, ride):
    s_len = x.shape[0]
    tm = min(s_len, 256)
    nb = s_len // tm

    def body(x_ref, g_ref, w_ref, dx1_ref, dlat_ref, dq_ref, dk_ref, dv_ref, dgate_ref, *rest):
        gx_ref, dg_ref = rest[ride.n:ride.n + 2]
        dproj = rest[-1]
        ride.bind(rest[:ride.n], rest[ride.n + 2:2 * ride.n + 2], rest[2 * ride.n + 2:-1])
        pl.when(pl.program_id(0) == 0)(ride.issue)
        dproj[:, 0:512] = dlat_ref[...].astype(BF16)
        dproj[:, 512:1024] = dq_ref[...].astype(BF16)
        dproj[:, 1024:1536] = (dk_ref[...] * LN2).astype(BF16)
        dproj[:, 1536:2048] = dv_ref[...].astype(BF16)
        dproj[:, 2048:4096] = dgate_ref[...]
        dh = _dot_nt(dproj[...], w_ref[...])
        gv = g_ref[...]
        _, xh, r = _rms(x_ref[...], gv)
        dx, dgr = _rms_bwd(dh, xh, r, gv)
        gx_ref[...] = dx1_ref[...] + dx
        _acc_rows(dg_ref, dgr, pl.program_id(0) == 0)
        pl.when(pl.program_id(0) == nb - 1)(ride.finish)

    rowb = lambda n: pl.BlockSpec((tm, n), lambda i: (i, 0))
    full = lambda shape: pl.BlockSpec(shape, lambda i: (0, 0))
    outs = pl.pallas_call(
        body, name="in_proj_bwd", grid=(nb,),
        in_specs=[rowb(D_MODEL), full((1, D_MODEL)), full((D_MODEL, D_IN_PAD)), rowb(D_MODEL),
                  rowb(512), rowb(512), rowb(512), rowb(512), rowb(2 * D_MODEL)] + ride.specs,
        out_specs=[rowb(D_MODEL), full((1, D_MODEL))] + ride.specs,
        out_shape=[jax.ShapeDtypeStruct((s_len, D_MODEL), F32), jax.ShapeDtypeStruct((1, D_MODEL), F32)]
        + ride.out_shape,
        scratch_shapes=ride.scratch + [pltpu.VMEM((tm, D_IN_PAD), BF16)],
        compiler_params=_cparams(48, dimension_semantics=("arbitrary",)),
    )(x, g, w, dx1, dlat, dsbq, dsbk, dsbv, dgates, *ride.srcs)
    return outs[0], outs[1], outs[2:]


def _adamw(landed, w, m, v, name):
    r, c = w.shape
    lanes = _round_up(c, LANES)
    tb = r
    for cand in range(r, 0, -1):
        if r % cand == 0 and (cand % 8 == 0 or cand == r) and N_DEV * cand * lanes * 4 <= ADAM_BLOCK_BYTES:
            tb = cand
            break
    c1 = 1.0 - ADAM_B1 ** ADAM_STEP
    c2 = 1.0 - ADAM_B2 ** ADAM_STEP

    def body(l_ref, w_ref, m_ref, v_ref, g_ref, d_ref, nm_ref, nv_ref):
        g = l_ref[0]
        for k in range(1, N_DEV):
            g = g + l_ref[k]
        nm = ADAM_B1 * m_ref[...] + (1.0 - ADAM_B1) * g
        nv = ADAM_B2 * v_ref[...] + (1.0 - ADAM_B2) * (g * g)
        g_ref[...] = g
        nm_ref[...] = nm
        nv_ref[...] = nv
        d_ref[...] = -ADAM_LR * ((nm / c1) / (jnp.sqrt(nv / c2) + ADAM_EPS) + ADAM_WD * w_ref[...])

    blk = pl.BlockSpec((tb, c), lambda i: (i, 0))
    return pl.pallas_call(
        body, name=name, grid=(r // tb,),
        in_specs=[pl.BlockSpec((N_DEV, tb, c), lambda i: (0, i, 0)), blk, blk, blk],
        out_specs=[blk, blk, blk, blk],
        out_shape=[jax.ShapeDtypeStruct((r, c), F32)] * 4,
        compiler_params=_cparams(dimension_semantics=("parallel",)),
    )(landed, w, m, v)


def _shard_shape(shape, axis):
    return tuple(d // N_DEV if a == axis else d for a, d in enumerate(shape))


def _split_pieces(full, axis):
    r, c = full.shape
    if axis == 0:
        return full.reshape(N_DEV, r // N_DEV, c)
    return full.reshape(r, N_DEV, c // N_DEV).transpose(1, 0, 2)


def _join_shards(gathered, axis):
    _, r, c = gathered.shape
    if axis == 0:
        return gathered.reshape(N_DEV * r, c)
    return gathered.transpose(1, 0, 2).reshape(r, N_DEV * c)


def kernel(x, mem, positions, g_mix, w_in, b_gate, g_q_lat, w_uq, g_kv_lat, w_ukv, w_a_proj, w_b_proj, w_o, g_x, g_mem, w_xq, w_xkv, w_xo, g_ffn, w_gate, w_up, w_down, g_final, loss_target, m_g_mix, m_w_in, m_b_gate, m_g_q_lat, m_w_uq, m_g_kv_lat, m_w_ukv, m_w_a_proj, m_w_b_proj, m_w_o, m_g_x, m_g_mem, m_w_xq, m_w_xkv, m_w_xo, m_g_ffn, m_w_gate, m_w_up, m_w_down, m_g_final, v_g_mix, v_w_in, v_b_gate, v_g_q_lat, v_w_uq, v_g_kv_lat, v_w_ukv, v_w_a_proj, v_w_b_proj, v_w_o, v_g_x, v_g_mem, v_w_xq, v_w_xkv, v_w_xo, v_g_ffn, v_w_gate, v_w_up, v_w_down, v_g_final):
    given = dict(locals())
    s_len = x.shape[1]
    x2d = x.reshape(s_len, D_MODEL)
    mem2d = mem.reshape(-1, D_MODEL)
    target = loss_target.reshape(s_len, D_MODEL)

    names = [name for name, _, _ in SHARDED]
    axis_of = {name: axis for name, _, axis in SHARDED}
    shard2d = lambda name, prefix="": given[prefix + name].reshape(
        _shard_shape(dict((n, s) for n, s, _ in SHARDED)[name], axis_of[name]))

    wire = lambda name: shard2d(name) if name == "b_gate" else shard2d(name).astype(BF16)
    early = [n for n in names if n in NEEDED_FIRST]
    late = [n for n in names if n not in NEEDED_FIRST]
    gathered = _exchange(True, [wire(n) for n in early], "weights_gather_first")
    wts = {n: _join_shards(g, axis_of[n]) for n, g in zip(early, gathered)}

    w_in_p = jnp.concatenate([wts["w_in"][:, :416], jnp.zeros((D_MODEL, 96), BF16), wts["w_in"][:, 416:]], axis=1)
    w_uq_p = jnp.pad(wts["w_uq"].reshape(256, MLA_HEADS, 96), ((0, 0), (0, 0), (0, 32))).reshape(256, 1024)
    ukv = wts["w_ukv"].reshape(128, MLA_HEADS, 128)
    w_uk_p = jnp.pad(ukv[:, :, :64], ((0, 0), (0, 0), (0, 64))).reshape(128, 1024)
    w_uv = ukv[:, :, 64:].reshape(128, 512)
    bg = wts["b_gate"]

    inv_freq = ROPE_THETA ** (-jnp.arange(0, MLA_ROPE, 2, dtype=F32) / MLA_ROPE)
    ang = positions.reshape(s_len).astype(F32)[:, None] * inv_freq
    cos16, sin16 = jnp.cos(ang), jnp.sin(ang)
    cosf = jnp.concatenate([jnp.ones((s_len, 64), F32), cos16, cos16, jnp.ones((s_len, 32), F32)], axis=1)
    sinf = jnp.concatenate([jnp.zeros((s_len, 64), F32), sin16, sin16, jnp.zeros((s_len, 32), F32)], axis=1)

    h1, lat, sb, gates = _in_proj(x2d, g_mix, w_in_p)
    qa, ka, va, q_lat, kv_lat = _mla_prep(lat, g_q_lat, g_kv_lat, w_uq_p, w_uk_p, w_uv, cosf, sinf)
    oa, lse, gathered = _mla_fwd(qa, ka, va, _Exchange(True, [wire(n) for n in late]))
    wts.update({n: _join_shards(g, axis_of[n]) for n, g in zip(late, gathered)})
    ob, sb_r = _sb_fwd(sb)
    x1 = _merge_fwd(x2d, oa, ob, gates, bg, wts["w_a_proj"], wts["w_b_proj"], wts["w_o"])
    mn, xkv = _mem_kv(mem2d, g_mem, wts["w_xkv"])
    x2 = _xattn_fwd(x1, g_x, wts["w_xq"], xkv, wts["w_xo"])
    x3, hf = _ffn_fwd(x2, g_ffn, wts["w_gate"], wts["w_up"], wts["w_down"])
    g_final2d = g_final.reshape(1, D_MODEL)
    sse, dx3, dx3b, dg_final = _loss_head(x3, g_final2d, target)
    loss = lax.psum(sse[0, 0] * (0.5 / D_MODEL), ("x", "y", "c"))

    dx2, dgt, dup, act, dg_ffn = _ffn_bwd(x2, hf, dx3, dx3b, g_ffn, wts["w_gate"], wts["w_up"], wts["w_down"])
    dx1, dw_xq, dw_xo, dxkv, dg_x = _xattn_bwd(x1, dx2, g_x, wts["w_xq"], xkv, wts["w_xo"])
    dw_xkv, dg_mem = _mem_bwd(mem2d, g_mem, wts["w_xkv"], mn, dxkv)
    doa, dob, dgates, dpa, dpb, merged, dx1b, dbg = _merge_bwd(
        dx1, oa, ob, gates, bg, wts["w_a_proj"], wts["w_b_proj"], wts["w_o"])
    dsbq, dsbk, dsbv = _sb_bwd(sb, dob, sb_r)
    full_grads = {
        "w_a_proj": _tn_matmul(oa, dpa, "dw_a"),
        "w_b_proj": _tn_matmul(ob, dpb, "dw_b"),
        "w_o": _tn_matmul(merged, dx1b, "dw_o"),
        "w_xq": dw_xq,
        "w_xkv": dw_xkv,
        "w_xo": dw_xo,
        "w_gate": _tn_matmul(hf, dgt, "dw_gate", tn=FF_TILE),
        "w_up": _tn_matmul(hf, dup, "dw_up", tn=FF_TILE),
        "w_down": _tn_matmul(act, dx3b, "dw_down", tka=FF_TILE),
    }
    dw_sbq = _tn_matmul(h1, dsbq, "dw_in_sbq")
    own0 = D_IN // N_DEV - 416
    dw_in_rest = jnp.concatenate([
        dw_sbq[:, own0:], _tn_matmul(h1, dsbk, "dw_in_sbk", scale=LN2), _tn_matmul(h1, dsbv, "dw_in_sbv"),
        _tn_matmul(h1, dgates, "dw_in_gates")], axis=1)
    w_in_rest = jnp.pad(dw_in_rest.reshape(D_MODEL, N_DEV - 1, D_IN // N_DEV).transpose(1, 0, 2),
                        ((1, 0), (0, 0), (0, 0)))
    dqa, dka, dva, got = _mla_bwd(
        qa, ka, va, oa, doa, lse,
        _Exchange(False, [_split_pieces(full_grads[n], axis_of[n]) for n in late] + [w_in_rest]))
    landed = dict(zip(late, got[:-1]))
    dlat, dqb, dkb, dvb, dg_q, dg_kv = _mla_prep_bwd(
        lat, g_q_lat, g_kv_lat, w_uq_p, w_uk_p, w_uv, cosf, sinf, dqa, dka, dva)
    w_in_first = jnp.concatenate([_tn_matmul(h1, dlat, "dw_in_lat")[:, :416], dw_sbq[:, :own0]], axis=1)
    dw_uq_p = _tn_matmul(q_lat, dqb, "dw_uq")
    dw_uk_p = _tn_matmul(kv_lat, dkb, "dw_uk")
    dw_uv = _tn_matmul(kv_lat, dvb, "dw_uv")
    full_grads.update({
        "b_gate": dbg,
        "w_uq": dw_uq_p.reshape(256, MLA_HEADS, 128)[:, :, :96].reshape(256, 768),
        "w_ukv": jnp.concatenate([dw_uk_p.reshape(128, MLA_HEADS, 128)[:, :, :64],
                                  dw_uv.reshape(128, MLA_HEADS, 64)], axis=2).reshape(128, 1024),
    })
    small = [n for n in early if n != "w_in"]
    grad_x, dg_mix, got2 = _in_proj_bwd(
        x2d, g_mix, w_in_p, dx1, dlat, dsbq, dsbk, dsbv, dgates,
        _Exchange(False, [_split_pieces(full_grads[n], axis_of[n]) for n in small], to_first=[w_in_first]))
    landed.update(zip(small, got2[:-1]))
    me = 4 * lax.axis_index("x") + 2 * lax.axis_index("y") + lax.axis_index("c")
    landed["w_in"] = jnp.where(me == 0, got2[-1], got[-1])
    rep_grads = {"g_mix": dg_mix, "g_q_lat": dg_q, "g_kv_lat": dg_kv, "g_x": dg_x, "g_mem": dg_mem,
                 "g_ffn": dg_ffn, "g_final": dg_final}
    rep_cat = lambda prefix, src: jnp.concatenate(
        [src[prefix + n].reshape(-1) for n, _ in REPLICATED]).reshape(-1, LANES)
    rep_src = jnp.broadcast_to(rep_cat("", rep_grads), (N_DEV,) + rep_cat("", rep_grads).shape)
    rep_landed = _exchange(False, [rep_src], "grads_gains")[0]

    res = {}
    for name, _, _ in SHARDED:
        outs = _adamw(landed[name], shard2d(name), shard2d(name, "m_"), shard2d(name, "v_"), "adamw_" + name)
        res[name] = [o.reshape(given[name].shape) for o in outs]
    rep_outs = _adamw(rep_landed, rep_cat("", given), rep_cat("m_", given), rep_cat("v_", given), "adamw_gains")
    off = 0
    for name, n in REPLICATED:
        res[name] = [o.reshape(-1)[off:off + n].reshape(given[name].shape) for o in rep_outs]
        off += n
    result = [loss, grad_x.reshape(x.shape)]
    for k in range(4):
        result.extend(res[name][k] for name in WEIGHT_ORDER)
    return tuple(result)
```

```python
import functools
import math

import jax
import jax.numpy as jnp
from jax import lax
from jax.experimental import pallas as pl
from jax.experimental.pallas import tpu as pltpu

F32 = jnp.float32
BF16 = jnp.bfloat16

D_MODEL = 1024
MLA_HEADS = 8
MLA_Q_RANK = 256
MLA_KV_RANK = 128
MLA_NOPE = 64
MLA_ROPE = 32
ROPE_THETA = 10000.0
SB_WIDTH = 512
X_HEADS = 4
X_HEAD_DIM = 128
D_FF = 2816
EPS = 1e-6
D_IN = 4000
D_IN_PAD = 4096
K_R_OFF = 384
LANES = 128
HEAD_PAD = 128
MLA_SCALE = 1.0 / math.sqrt(MLA_NOPE + MLA_ROPE)
SB_SCALE = 0.125
LOG2E = math.log2(math.e)
LN2 = math.log(2.0)
MLA_Q_FOLD = MLA_SCALE * LOG2E
SB_Q_FOLD = SB_SCALE * LOG2E
SB_CUT = -160.0
X_SCALE = 1.0 / math.sqrt(X_HEAD_DIM)
NEG_BIG = -1e30

ADAM_LR = 0.001
ADAM_B1 = 0.9
ADAM_B2 = 0.999
ADAM_EPS = 1e-08
ADAM_WD = 0.01
ADAM_STEP = 10

N_DEV = 8
MIB = 1024 * 1024
ADAM_BLOCK_BYTES = 4 * MIB

SHARDED = (
    ("w_in", (D_MODEL, D_IN), 1),
    ("b_gate", (2, D_MODEL), 1),
    ("w_uq", (MLA_Q_RANK, 768), 1),
    ("w_ukv", (MLA_KV_RANK, 1024), 1),
    ("w_a_proj", (512, D_MODEL), 1),
    ("w_b_proj", (512, D_MODEL), 1),
    ("w_o", (D_MODEL, D_MODEL), 0),
    ("w_xq", (D_MODEL, 512), 0),
    ("w_xkv", (D_MODEL, 1024), 0),
    ("w_xo", (512, D_MODEL), 1),
    ("w_gate", (D_MODEL, D_FF), 1),
    ("w_up", (D_MODEL, D_FF), 1),
    ("w_down", (D_FF, D_MODEL), 0),
)
NEEDED_FIRST = ("w_in", "b_gate", "w_uq", "w_ukv")
REPLICATED = (
    ("g_mix", 1024), ("g_q_lat", 256), ("g_kv_lat", 128), ("g_x", 1024),
    ("g_mem", 1024), ("g_ffn", 1024), ("g_final", 1024),
)
WEIGHT_ORDER = ("g_mix", "w_in", "b_gate", "g_q_lat", "w_uq", "g_kv_lat", "w_ukv", "w_a_proj",
                "w_b_proj", "w_o", "g_x", "g_mem", "w_xq", "w_xkv", "w_xo", "g_ffn", "w_gate",
                "w_up", "w_down", "g_final")


def _round_up(n, m):
    return -(-n // m) * m


def _cparams(vmem_mib=None, **kw):
    if vmem_mib is not None:
        kw["vmem_limit_bytes"] = vmem_mib * MIB
    return pltpu.CompilerParams(**kw)


def _dot(a, b):
    return jnp.dot(a, b, preferred_element_type=F32)


def _dot_nt(a, b):
    return lax.dot_general(a, b, (((1,), (1,)), ((), ())), preferred_element_type=F32)


def _dot_tn(a, b):
    return lax.dot_general(a, b, (((0,), (0,)), ((), ())), preferred_element_type=F32)


def _rms(x, g):
    r = lax.rsqrt(jnp.mean(x * x, axis=-1, keepdims=True) + EPS)
    xh = x * r
    return xh * g, xh, r


def _rms_bwd(dy, xh, r, g):
    u = dy * g
    dx = r * (u - xh * jnp.mean(u * xh, axis=-1, keepdims=True))
    return dx, dy * xh


def _sigmoid(z):
    return 1.0 / (1.0 + jnp.exp(-z))


def _acc_rows(ref, val, first):
    s = jnp.sum(val, axis=0, keepdims=True)

    @pl.when(first)
    def _():
        ref[...] = s

    @pl.when(jnp.logical_not(first))
    def _():
        ref[...] += s


def _acc(ref, val, first):
    @pl.when(first)
    def _():
        ref[...] = val

    @pl.when(jnp.logical_not(first))
    def _():
        ref[...] += val


def _peer(k):
    x, y, c = lax.axis_index("x"), lax.axis_index("y"), lax.axis_index("c")
    px = 1 - x if (k >> 2) & 1 else x
    py = 1 - y if (k >> 1) & 1 else y
    pc = 1 - c if k & 1 else c
    return (px, py, pc), 4 * px + 2 * py + pc


N_PEERS = N_DEV - 1
OTHER_CHIPS = (2, 4, 6)


def _land_shape(gather, src):
    return (N_DEV,) + src.shape if gather else src.shape


class _Exchange:
    def __init__(self, gather, srcs, to_first=()):
        self.gather, self.m, self.srcs = gather, len(srcs), list(srcs) + list(to_first)
        self.n = len(self.srcs)
        self.out_shape = ([jax.ShapeDtypeStruct(_land_shape(gather, s), s.dtype) for s in srcs]
                          + [jax.ShapeDtypeStruct(_land_shape(True, s), s.dtype) for s in to_first])
        self.specs = [pl.BlockSpec(memory_space=pl.ANY)] * self.n
        self.scratch = [pltpu.SemaphoreType.DMA((self.n * N_PEERS,)), pltpu.SemaphoreType.DMA((self.n * N_PEERS,)),
                        pltpu.SemaphoreType.DMA((self.n,))]

    def bind(self, src, land, sems):
        self.src, self.land = src, land
        self.send_sems, self.recv_sems, self.local_sems = sems

    def _copy(self, a, k, source, to, target=1):
        return pltpu.make_async_remote_copy(
            src_ref=source, dst_ref=to,
            send_sem=self.send_sems.at[a * N_PEERS + k - 1], recv_sem=self.recv_sems.at[a * N_PEERS + k - 1],
            device_id=_peer(target)[0], device_id_type=pl.DeviceIdType.MESH)

    def _row(self, a, k):
        return self.land[a].at[_peer(k)[1]]

    def _mine(self, a):
        me = _peer(0)[1]
        whole = self.gather or a >= self.m
        return pltpu.make_async_copy(self.src[a] if whole else self.src[a].at[me], self.land[a].at[me],
                                     self.local_sems.at[a])

    def issue(self):
        me = _peer(0)[1]
        for a in range(self.m):
            self._mine(a).start()
            for k in ((1,) + OTHER_CHIPS if self.gather else range(1, N_DEV)):
                source = self.src[a] if self.gather else self.src[a].at[_peer(k)[1]]
                self._copy(a, k, source, self.land[a].at[me], target=k).start()
        for a in range(self.m, self.n):
            pl.when(me == 0)(self._mine(a).start)
            for k in range(1, N_DEV):
                pl.when(me == k)(self._copy(a, k, self.src[a], self.land[a].at[me], target=k).start)

    def finish(self):
        me = _peer(0)[1]
        part = lambda a: self.src[a] if self.gather or a >= self.m else self.src[a].at[me]
        if self.gather:
            for a in range(self.m):
                for k in OTHER_CHIPS:
                    self._copy(a, k, part(a), self._row(a, k)).wait_recv()
                    self._copy(a, k + 1, self._row(a, k), self._row(a, k), target=1).start()
        for a in range(self.m):
            for k in ((1, 3, 5, 7) if self.gather else range(1, N_DEV)):
                self._copy(a, k, part(a), self._row(a, k)).wait_recv()
        for a in range(self.m):
            for k in range(1, N_DEV):
                self._copy(a, k, part(a), self.land[a].at[me]).wait_send()
            self._mine(a).wait()
        for a in range(self.m, self.n):
            for k in range(1, N_DEV):
                pl.when(me == 0)(self._copy(a, k, part(a), self._row(a, k)).wait_recv)
                pl.when(me == k)(self._copy(a, k, part(a), self.land[a].at[me]).wait_send)
            pl.when(me == 0)(self._mine(a).wait)


def _exchange(gather, srcs, name):
    ex = _Exchange(gather, srcs)

    def body(*refs):
        ex.bind(refs[:ex.n], refs[ex.n:2 * ex.n], refs[2 * ex.n:])
        ex.issue()
        ex.finish()

    return pl.pallas_call(body, name=name, out_shape=ex.out_shape, in_specs=ex.specs, out_specs=ex.specs,
                          scratch_shapes=ex.scratch)(*ex.srcs)


def _tn_matmul(a, b, name, tka=512, tn=1024, ts=2048, scale=None):
    s_len, ka = a.shape
    n = b.shape[1]
    tka, tn, ts = min(tka, ka), min(tn, n), min(ts, s_len)
    assert ka % tka == 0 and n % tn == 0 and s_len % ts == 0

    def body(a_ref, b_ref, o_ref):
        bv = b_ref[...]
        if scale is not None:
            bv = bv * scale
        _acc(o_ref, _dot_tn(a_ref[...], bv.astype(BF16)), pl.program_id(2) == 0)

    return pl.pallas_call(
        body, name=name, grid=(ka // tka, n // tn, s_len // ts),
        in_specs=[pl.BlockSpec((ts, tka), lambda i, j, s: (s, i)),
                  pl.BlockSpec((ts, tn), lambda i, j, s: (s, j))],
        out_specs=pl.BlockSpec((tka, tn), lambda i, j, s: (i, j)),
        out_shape=jax.ShapeDtypeStruct((ka, n), F32),
        compiler_params=_cparams(dimension_semantics=("parallel", "parallel", "arbitrary")),
    )(a, b)


def _row_block(s_len):
    return min(s_len, 512)


def _in_proj(x, g, w):
    s_len = x.shape[0]
    tm = _row_block(s_len)

    def body(x_ref, g_ref, w_ref, h_ref, lat_ref, sb_ref, gate_ref):
        h, _, _ = _rms(x_ref[...], g_ref[...])
        hb = h.astype(BF16)
        h_ref[...] = hb
        p = _dot(hb, w_ref[:, 0:1024])
        lat_ref[...] = p[:, 0:512]
        sb_ref[:, 0:512] = (p[:, 512:1024] * SB_Q_FOLD).astype(BF16)
        sb_ref[:, 512:1536] = _dot(hb, w_ref[:, 1024:2048]).astype(BF16)
        gate_ref[:, 0:1024] = _dot(hb, w_ref[:, 2048:3072])
        gate_ref[:, 1024:2048] = _dot(hb, w_ref[:, 3072:4096])

    rowb = lambda n: pl.BlockSpec((tm, n), lambda i: (i, 0))
    return pl.pallas_call(
        body, name="in_proj", grid=(s_len // tm,),
        in_specs=[rowb(D_MODEL), pl.BlockSpec((1, D_MODEL), lambda i: (0, 0)),
                  pl.BlockSpec((D_MODEL, D_IN_PAD), lambda i: (0, 0))],
        out_specs=[rowb(D_MODEL), rowb(512), rowb(3 * SB_WIDTH), rowb(2 * D_MODEL)],
        out_shape=[jax.ShapeDtypeStruct((s_len, D_MODEL), BF16),
                   jax.ShapeDtypeStruct((s_len, 512), F32),
                   jax.ShapeDtypeStruct((s_len, 3 * SB_WIDTH), BF16),
                   jax.ShapeDtypeStruct((s_len, 2 * D_MODEL), F32)],
        compiler_params=_cparams(48, dimension_semantics=("parallel",)),
    )(x, g, w)


def _rope_rot(blk, lane):
    return jnp.where(lane < 80, -pltpu.roll(blk, 112, 1), pltpu.roll(blk, 16, 1))


def _rope_rot_t(blk, lane):
    return jnp.where(lane < 80, pltpu.roll(blk, 112, 1), -pltpu.roll(blk, 16, 1))


def _mla_prep(lat, g_q, g_kv, w_uq, w_uk, w_uv, w_uv1, cosf, sinf):
    s_len = lat.shape[0]
    tm = _row_block(s_len)

    def body(lat_ref, gq_ref, gkv_ref, wuq_ref, wuk_ref, wuv_ref, wuv1_ref, cos_ref, sin_ref,
             q_ref, k_ref, v_ref, v1_ref, ql_ref, kvl_ref):
        lane = lax.broadcasted_iota(jnp.int32, (tm, LANES), 1)
        cosv, sinv = cos_ref[...], sin_ref[...]
        ql, _, _ = _rms(lat_ref[:, 0:256], gq_ref[...])
        kvl, _, _ = _rms(lat_ref[:, 256:384], gkv_ref[...])
        qlb, kvlb = ql.astype(BF16), kvl.astype(BF16)
        ql_ref[...] = qlb
        kvl_ref[...] = kvlb
        q = _dot(qlb, wuq_ref[...])
        kn = _dot(kvlb, wuk_ref[...])
        v_ref[...] = _dot(kvlb, wuv_ref[...]).astype(BF16)
        v1 = _dot(kvlb, wuv1_ref[...])
        wide = lax.broadcasted_iota(jnp.int32, (tm, MLA_HEADS * HEAD_PAD), 1)
        v1_ref[...] = jnp.where(wide % HEAD_PAD == MLA_NOPE, 1.0, v1).astype(BF16)
        kr = pltpu.roll(lat_ref[:, K_R_OFF:K_R_OFF + LANES], 64, 1)
        kr = kr * cosv + _rope_rot(kr, lane) * sinv
        for h in range(MLA_HEADS):
            sl = slice(h * HEAD_PAD, (h + 1) * HEAD_PAD)
            blk = q[:, sl]
            q_ref[:, sl] = ((blk * cosv + _rope_rot(blk, lane) * sinv) * MLA_Q_FOLD).astype(BF16)
            k_ref[:, sl] = (kn[:, sl] + kr).astype(BF16)

    full = lambda shape: pl.BlockSpec(shape, lambda i: (0, 0))
    rowb = lambda n: pl.BlockSpec((tm, n), lambda i: (i, 0))
    return pl.pallas_call(
        body, name="mla_prep", grid=(s_len // tm,),
        in_specs=[rowb(512), full((1, 256)), full((1, 128)), full((256, 1024)), full((128, 1024)),
                  full((128, 512)), full((128, 1024)), rowb(128), rowb(128)],
        out_specs=[rowb(1024), rowb(1024), rowb(512), rowb(1024), rowb(256), rowb(128)],
        out_shape=[jax.ShapeDtypeStruct((s_len, 1024), BF16), jax.ShapeDtypeStruct((s_len, 1024), BF16),
                   jax.ShapeDtypeStruct((s_len, 512), BF16), jax.ShapeDtypeStruct((s_len, 1024), BF16),
                   jax.ShapeDtypeStruct((s_len, 256), BF16), jax.ShapeDtypeStruct((s_len, 128), BF16)],
        compiler_params=_cparams(dimension_semantics=("parallel",)),
    )(lat, g_q, g_kv, w_uq, w_uk, w_uv, w_uv1, cosf, sinf)


ATTN_TQ = 1024
ATTN_TH = 512
ATTN_TK = 256
SB_TQ = 512
SB_TH = 256
MLA_TK = 512


def _attn_blocks(s_len, tk=ATTN_TK, tq=ATTN_TQ, th=ATTN_TH):
    tq, th, tk = min(s_len, tq), min(s_len, th), min(s_len, tk)
    return tq, th, tk, tq // tk


def _chains(tq, th):
    return [(hh, r0) for hh in range(2) for r0 in range(0, tq, th)]


def _diag_mask(th, tk, r0, sub, strict):
    lo, hi = sub * tk, (sub + 1) * tk - 1
    last, first = r0 + th - 1, r0
    if (lo >= last) if strict else (lo > last):
        return "none"
    if (hi < first) if strict else (hi <= first):
        return "all"
    row = lax.broadcasted_iota(jnp.int32, (th, tk), 0) + r0
    col = lax.broadcasted_iota(jnp.int32, (th, tk), 1) + lo
    return col < row if strict else col <= row


def _mla_fwd(q, k, v, ride):
    s_len = q.shape[0]
    tq, th, tk, nsub = _attn_blocks(s_len, MLA_TK)
    chains = _chains(tq, th)
    nh = tq // th
    nq = s_len // tq

    def body(q_ref, k_ref, v_ref, *rest):
        o_ref, lse_ref = rest[ride.n:ride.n + 2]
        ride.bind(rest[:ride.n], rest[ride.n + 2:2 * ride.n + 2], rest[2 * ride.n + 2:])
        pl.when((pl.program_id(0) == 0) & (pl.program_id(1) == 0))(ride.issue)
        i = pl.program_id(1)
        lane = lax.broadcasted_iota(jnp.int32, (th, LANES), 1)
        hsl = [slice(hh * HEAD_PAD, (hh + 1) * HEAD_PAD) for hh in range(2)]

        def step(kb, carry, sub):
            rows = pl.ds(pl.multiple_of(kb * tk, tk), tk)
            masks = ["all" if sub is None else _diag_mask(th, tk, r0, sub, strict=False) for _, r0 in chains]
            live = [n for n, m in enumerate(masks) if not (isinstance(m, str) and m == "none")]
            s = {n: _dot_nt(q_ref[chains[n][1]:chains[n][1] + th, hsl[chains[n][0]]], k_ref[rows, hsl[chains[n][0]]])
                 for n in live}
            new = list(carry)
            pb, alpha = {}, {}
            for n in live:
                m = carry[n][0]
                sn = s[n]
                if not isinstance(masks[n], str):
                    sn = jnp.where(masks[n], sn, NEG_BIG)
                m_new = jnp.maximum(m, jnp.max(sn, axis=-1, keepdims=True))
                alpha[n] = jnp.exp2(m - m_new)
                pb[n] = jnp.exp2(sn - m_new).astype(BF16)
                new[n] = (m_new, None)
            pv = {n: _dot(pb[n], v_ref[rows, hsl[chains[n][0]]]) for n in live}
            for n in live:
                new[n] = (new[n][0], alpha[n] * carry[n][1] + pv[n])
            return tuple(new)

        init = (jnp.full((th, 1), NEG_BIG, F32), jnp.zeros((th, LANES), F32))
        carry = lax.fori_loop(0, i * nsub, lambda kb, cy: step(kb, cy, None), (init,) * len(chains))
        for sub in range(nsub):
            carry = step(i * nsub + sub, carry, sub)
        for c in range(nh):
            (m0, a0), (m1, a1) = carry[c], carry[nh + c]
            l0, l1 = a0[:, MLA_NOPE:MLA_NOPE + 1], a1[:, MLA_NOPE:MLA_NOPE + 1]
            rs = slice(c * th, (c + 1) * th)
            o_ref[rs, :] = jnp.where(lane < 64, a0 / l0, pltpu.roll(a1 / l1, 64, 1)).astype(BF16)
            lse_ref[rs, :] = jnp.where(lane < 64, m0 + jnp.log2(l0), m1 + jnp.log2(l1))
        pl.when((pl.program_id(0) == 3) & (i == nq - 1))(ride.finish)

    outs = pl.pallas_call(
        body, name="mla_fwd", grid=(4, nq),
        in_specs=[pl.BlockSpec((tq, 2 * HEAD_PAD), lambda p, i: (i, p)),
                  pl.BlockSpec((s_len, 2 * HEAD_PAD), lambda p, i: (0, p)),
                  pl.BlockSpec((s_len, 2 * HEAD_PAD), lambda p, i: (0, p))] + ride.specs,
        out_specs=[pl.BlockSpec((tq, LANES), lambda p, i: (i, p)),
                   pl.BlockSpec((None, tq, LANES), lambda p, i: (p, i, 0))] + ride.specs,
        out_shape=[jax.ShapeDtypeStruct((s_len, 512), BF16),
                   jax.ShapeDtypeStruct((4, s_len, LANES), F32)] + ride.out_shape,
        scratch_shapes=ride.scratch,
        compiler_params=_cparams(40, dimension_semantics=("arbitrary", "arbitrary")),
    )(q, k, v, *ride.srcs)
    return outs[0], outs[1], outs[2:]


def _mla_bwd(q, k, v, o, do, lse, ride):
    s_len = q.shape[0]
    tq, th, tk, nsub = _attn_blocks(s_len, MLA_TK)
    chains = _chains(tq, th)
    nq = s_len // tq

    def body(q_ref, k_ref, v_ref, o_ref, do_ref, lse_ref, *rest):
        dq_ref, dk_ref, dv_ref = rest[ride.n:ride.n + 3]
        ride.bind(rest[:ride.n], rest[ride.n + 3:2 * ride.n + 3], rest[2 * ride.n + 3:])
        pl.when((pl.program_id(0) == 0) & (pl.program_id(1) == 0))(ride.issue)
        i = pl.program_id(1)
        lane = lax.broadcasted_iota(jnp.int32, (th, LANES), 1)

        @pl.when(i == 0)
        def _():
            dk_ref[...] = jnp.zeros_like(dk_ref)
            dv_ref[...] = jnp.zeros_like(dv_ref)

        hsl = [slice(hh * HEAD_PAD, (hh + 1) * HEAD_PAD) for hh in range(2)]
        qs, dos, deltas, lses = [], [], [], []
        for hh, r0 in chains:
            rs = slice(r0, r0 + th)
            qs.append(q_ref[rs, hsl[hh]])
            doh = jnp.where((lane // 64) == hh, do_ref[rs, :], jnp.zeros((), BF16))
            dos.append(doh)
            deltas.append(jnp.sum(doh.astype(F32) * o_ref[rs, :].astype(F32), axis=-1, keepdims=True))
            lses.append(lse_ref[rs, 64 * hh:64 * hh + 1])

        def step(kb, dqs, sub):
            rows = pl.ds(pl.multiple_of(kb * tk, tk), tk)
            vblk = v_ref[rows, :]
            new, p_all, do_all = [], [], []
            ds_h, q_h = [[], []], [[], []]
            for c, (hh, r0) in enumerate(chains):
                mask = "all" if sub is None else _diag_mask(th, tk, r0, sub, strict=False)
                if isinstance(mask, str) and mask == "none":
                    new.append(dqs[c])
                    continue
                kblk = k_ref[rows, hsl[hh]]
                s = _dot_nt(qs[c], kblk)
                if not isinstance(mask, str):
                    s = jnp.where(mask, s, NEG_BIG)
                p = jnp.exp2(s - lses[c])
                dp = _dot_nt(dos[c], vblk)
                ds = (p * (dp - deltas[c]) * MLA_SCALE).astype(BF16)
                p_all.append(p.astype(BF16))
                do_all.append(dos[c])
                ds_h[hh].append(ds)
                q_h[hh].append(qs[c])
                new.append(dqs[c] + _dot(ds, kblk))
            dv_ref[rows, :] += _dot_tn(jnp.concatenate(p_all, axis=0), jnp.concatenate(do_all, axis=0))
            for hh in range(2):
                dk_ref[rows, hsl[hh]] += _dot_tn(jnp.concatenate(ds_h[hh], axis=0),
                                                 jnp.concatenate(q_h[hh], axis=0))
            return tuple(new)

        zero = jnp.zeros((th, LANES), F32)
        dqs = lax.fori_loop(0, i * nsub, lambda kb, cy: step(kb, cy, None), (zero,) * len(chains))
        for sub in range(nsub):
            dqs = step(i * nsub + sub, dqs, sub)
        for c, (hh, r0) in enumerate(chains):
            dq_ref[r0:r0 + th, hsl[hh]] = dqs[c]
        pl.when((pl.program_id(0) == 3) & (i == nq - 1))(ride.finish)

    outs = pl.pallas_call(
        body, name="mla_bwd", grid=(4, nq),
        in_specs=[pl.BlockSpec((tq, 2 * HEAD_PAD), lambda p, i: (i, p)),
                  pl.BlockSpec((s_len, 2 * HEAD_PAD), lambda p, i: (0, p)),
                  pl.BlockSpec((s_len, LANES), lambda p, i: (0, p)),
                  pl.BlockSpec((tq, LANES), lambda p, i: (i, p)),
                  pl.BlockSpec((tq, LANES), lambda p, i: (i, p)),
                  pl.BlockSpec((None, tq, LANES), lambda p, i: (p, i, 0))] + ride.specs,
        out_specs=[pl.BlockSpec((tq, 2 * HEAD_PAD), lambda p, i: (i, p)),
                   pl.BlockSpec((s_len, 2 * HEAD_PAD), lambda p, i: (0, p)),
                   pl.BlockSpec((s_len, LANES), lambda p, i: (0, p))] + ride.specs,
        out_shape=[jax.ShapeDtypeStruct((s_len, 1024), F32), jax.ShapeDtypeStruct((s_len, 1024), F32),
                   jax.ShapeDtypeStruct((s_len, 512), F32)] + ride.out_shape,
        scratch_shapes=ride.scratch,
        compiler_params=_cparams(56, dimension_semantics=("arbitrary", "arbitrary")),
    )(q, k, v, o, do, lse, *ride.srcs)
    return outs[0], outs[1], outs[2], outs[3:]


def _log_sigmoids(z2):
    sp = jnp.log2(1.0 + jnp.exp2(-jnp.abs(z2)))
    lb = jnp.minimum(z2, 0.0) - sp
    return lb, lb - z2


def _split_dot(x, w, parts, nt=False):
    dot = _dot_nt if nt else _dot
    out = None
    for _ in range(parts):
        xb = x.astype(BF16)
        t = dot(xb, w)
        out = t if out is None else out + t
        x = x - xb.astype(F32)
    return out


def _sb_fwd(sb):
    s_len = sb.shape[0]
    tq, th, tk, nsub = _attn_blocks(s_len, tq=SB_TQ, th=SB_TH)
    chains = _chains(tq, th)
    nh = tq // th
    assert s_len // tk <= 64

    def body(q_ref, k_ref, v_ref, o_ref, r_ref):
        i = pl.program_id(1)
        lane = lax.broadcasted_iota(jnp.int32, (th, LANES), 1)
        upper = (lax.broadcasted_iota(jnp.int32, (tk, tk), 0)
                 > lax.broadcasted_iota(jnp.int32, (tk, tk), 1)).astype(BF16)
        qs = [jnp.where((lane // 64) == hh, q_ref[r0:r0 + th, :], jnp.zeros((), BF16)) for hh, r0 in chains]

        def step(kb, carry, sub):
            rows = pl.ds(pl.multiple_of(kb * tk, tk), tk)
            kblk, vblk = k_ref[rows, :], v_ref[rows, :]
            masks = ["all" if sub is None else _diag_mask(th, tk, r0, sub, strict=True) for _, r0 in chains]
            live = [n for n, m in enumerate(masks) if not (isinstance(m, str) and m == "none")]
            masked = {n: not isinstance(masks[n], str) for n in live}
            z = {n: _dot_nt(qs[n], kblk) for n in live}
            lb, lom = {}, {}
            for n in live:
                lb[n], lom[n] = _log_sigmoids(z[n])
                if masked[n]:
                    lom[n] = jnp.where(masks[n], lom[n], 0.0)
            suf = {n: _split_dot(lom[n], upper, 2) for n in live}
            a = {}
            for n in live:
                a[n] = jnp.exp2(lb[n] + suf[n] + carry[n][0])
                if masked[n]:
                    a[n] = jnp.where(masks[n], a[n], 0.0)
            pv = {n: _dot(a[n].astype(BF16), vblk) for n in live}
            new = list(carry)
            for n in live:
                c, acc, r = carry[n]
                rs = suf[n][:, 0:1] + lom[n][:, 0:1]
                new[n] = (c + rs, acc + pv[n], jnp.where(lane == 64 * chains[n][0] + kb, rs, r))
            return tuple(new)

        init = (jnp.zeros((th, 1), F32), jnp.zeros((th, LANES), F32), jnp.zeros((th, LANES), F32))
        carry = (init,) * len(chains)
        for sub in reversed(range(nsub)):
            carry = step(i * nsub + sub, carry, sub)

        def spent(cy):
            top = functools.reduce(jnp.maximum, [jnp.max(c) for c, _, _ in cy])
            return (top < SB_CUT).astype(jnp.int32)

        def walk(state):
            t, _, cy = state
            cy = step(i * nsub - 1 - t, cy, None)
            return t + 1, spent(cy), cy

        _, _, carry = lax.while_loop(lambda st: (st[0] < i * nsub) & (st[1] == 0), walk,
                                     (jnp.int32(0), spent(carry), carry))
        for n in range(nh):
            rs = slice(n * th, (n + 1) * th)
            o_ref[rs, :] = jnp.where(lane < 64, carry[n][1], carry[nh + n][1]).astype(BF16)
            r_ref[rs, :] = jnp.where(lane < 64, carry[n][2], carry[nh + n][2])

    return pl.pallas_call(
        body, name="sb_fwd", grid=(4, s_len // tq),
        in_specs=[pl.BlockSpec((tq, LANES), lambda p, i: (i, p)),
                  pl.BlockSpec((s_len, LANES), lambda p, i: (0, 4 + p)),
                  pl.BlockSpec((s_len, LANES), lambda p, i: (0, 8 + p))],
        out_specs=[pl.BlockSpec((tq, LANES), lambda p, i: (i, p)),
                   pl.BlockSpec((None, tq, LANES), lambda p, i: (p, i, 0))],
        out_shape=[jax.ShapeDtypeStruct((s_len, 512), BF16), jax.ShapeDtypeStruct((4, s_len, LANES), F32)],
        compiler_params=_cparams(40, dimension_semantics=("parallel", "arbitrary")),
    )(sb, sb, sb)


def _sb_bwd(sb, do, r):
    s_len = sb.shape[0]
    tq, th, tk, nsub = _attn_blocks(s_len)
    chains = _chains(tq, th)
    nh = tq // th

    def body(q_ref, k_ref, v_ref, do_ref, r_ref, dq_ref, dk_ref, dv_ref):
        i = pl.program_id(1)
        lane = lax.broadcasted_iota(jnp.int32, (th, LANES), 1)
        upper = (lax.broadcasted_iota(jnp.int32, (tk, tk), 0)
                 > lax.broadcasted_iota(jnp.int32, (tk, tk), 1)).astype(BF16)
        tri = (lax.broadcasted_iota(jnp.int32, (LANES, LANES), 0)
               > lax.broadcasted_iota(jnp.int32, (LANES, LANES), 1)).astype(BF16)

        @pl.when(i == 0)
        def _():
            dk_ref[...] = jnp.zeros_like(dk_ref)
            dv_ref[...] = jnp.zeros_like(dv_ref)

        qs, dos, rights = [], [], []
        for hh, r0 in chains:
            rs = slice(r0, r0 + th)
            hm = (lane // 64) == hh
            qs.append(jnp.where(hm, q_ref[rs, :], jnp.zeros((), BF16)))
            dos.append(jnp.where(hm, do_ref[rs, :], jnp.zeros((), BF16)))
            rights.append(_split_dot(jnp.where(hm, r_ref[rs, :], 0.0), tri, 3))

        def step(kb, carry, sub):
            rows = pl.ds(pl.multiple_of(kb * tk, tk), tk)
            kblk, vblk = k_ref[rows, :], v_ref[rows, :]
            new, a_all, do_all, dz_all, q_all = [], [], [], [], []
            for n, ((hh, r0), (pre, dq)) in enumerate(zip(chains, carry)):
                mask = "all" if sub is None else _diag_mask(th, tk, r0, sub, strict=True)
                if isinstance(mask, str) and mask == "none":
                    new.append((pre, dq))
                    continue
                c = jnp.sum(jnp.where(lane == 64 * hh + kb, rights[n], 0.0), axis=-1, keepdims=True)
                z = _dot_nt(qs[n], kblk)
                lb, lom = _log_sigmoids(z)
                if not isinstance(mask, str):
                    lom = jnp.where(mask, lom, 0.0)
                suf = _split_dot(lom, upper, 2)
                a = jnp.exp2(lb + suf + c)
                if not isinstance(mask, str):
                    a = jnp.where(mask, a, 0.0)
                g = a * _dot_nt(dos[n], vblk)
                left = _split_dot(g, upper, 1, nt=True) + pre
                sig = jnp.exp2(lb)
                dz = g * (1.0 - sig) - sig * left
                if not isinstance(mask, str):
                    dz = jnp.where(mask, dz, 0.0)
                dzb = dz.astype(BF16)
                a_all.append(a.astype(BF16))
                do_all.append(dos[n])
                dz_all.append(dzb)
                q_all.append(qs[n])
                new.append((left[:, tk - 1:tk] + g[:, tk - 1:tk], dq + _dot(dzb, kblk)))
            dv_ref[rows, :] += _dot_tn(jnp.concatenate(a_all, axis=0), jnp.concatenate(do_all, axis=0))
            dk_ref[rows, :] += _dot_tn(jnp.concatenate(dz_all, axis=0), jnp.concatenate(q_all, axis=0))
            return tuple(new)

        lane1 = lax.broadcasted_iota(jnp.int32, (1, LANES), 1)
        first = i * nsub
        for n, (hh, _) in enumerate(chains):
            top = jnp.max(rights[n], axis=0, keepdims=True)
            kb_of = lane1 - 64 * hh
            live = (kb_of >= 0) & (kb_of < i * nsub) & (top >= SB_CUT)
            first = jnp.minimum(first, jnp.min(jnp.where(live, kb_of, i * nsub)))

        init = (jnp.zeros((th, 1), F32), jnp.zeros((th, LANES), F32))
        carry = lax.fori_loop(first, i * nsub, lambda kb, cy: step(kb, cy, None), (init,) * len(chains))
        for sub in range(nsub):
            carry = step(i * nsub + sub, carry, sub)
        for n in range(nh):
            dq_ref[n * th:(n + 1) * th, :] = jnp.where(lane < 64, carry[n][1], carry[nh + n][1]) * SB_SCALE

    return pl.pallas_call(
        body, name="sb_bwd", grid=(4, s_len // tq),
        in_specs=[pl.BlockSpec((tq, LANES), lambda p, i: (i, p)),
                  pl.BlockSpec((s_len, LANES), lambda p, i: (0, 4 + p)),
                  pl.BlockSpec((s_len, LANES), lambda p, i: (0, 8 + p)),
                  pl.BlockSpec((tq, LANES), lambda p, i: (i, p)),
                  pl.BlockSpec((None, tq, LANES), lambda p, i: (p, i, 0))],
        out_specs=[pl.BlockSpec((tq, LANES), lambda p, i: (i, p)),
                   pl.BlockSpec((s_len, LANES), lambda p, i: (0, p)),
                   pl.BlockSpec((s_len, LANES), lambda p, i: (0, p))],
        out_shape=[jax.ShapeDtypeStruct((s_len, 512), F32)] * 3,
        compiler_params=_cparams(48, dimension_semantics=("arbitrary", "arbitrary")),
    )(sb, sb, sb, do, r)


def _merge_fwd(x, oa, ob, gates, bg, wa, wb, wo):
    s_len = x.shape[0]
    tm = _row_block(s_len)

    def body(x_ref, oa_ref, ob_ref, g_ref, bg_ref, wa_ref, wb_ref, wo_ref, y_ref):
        pa = _dot(oa_ref[...], wa_ref[...])
        pb = _dot(ob_ref[...], wb_ref[...])
        merged = (_sigmoid(g_ref[:, 0:D_MODEL] + bg_ref[0:1, :]) * pa
                  + _sigmoid(g_ref[:, D_MODEL:2 * D_MODEL] + bg_ref[1:2, :]) * pb)
        y_ref[...] = x_ref[...] + _dot(merged.astype(BF16), wo_ref[...])

    full = lambda shape: pl.BlockSpec(shape, lambda i: (0, 0))
    rowb = lambda n: pl.BlockSpec((tm, n), lambda i: (i, 0))
    return pl.pallas_call(
        body, name="merge_fwd", grid=(s_len // tm,),
        in_specs=[rowb(1024), rowb(512), rowb(512), rowb(2048), full((2, 1024)), full((512, 1024)),
                  full((512, 1024)), full((1024, 1024))],
        out_specs=rowb(1024),
        out_shape=jax.ShapeDtypeStruct((s_len, D_MODEL), F32),
        compiler_params=_cparams(48, dimension_semantics=("parallel",)),
    )(x, oa, ob, gates, bg, wa, wb, wo)


def _merge_bwd(dx1, oa, ob, gates, bg, wa, wb, wo):
    s_len = dx1.shape[0]
    tm = _row_block(s_len)

    def body(dx_ref, oa_ref, ob_ref, g_ref, bg_ref, wa_ref, wb_ref, wo_ref,
             doa_ref, dob_ref, dgate_ref, dpa_ref, dpb_ref, merged_ref, dxb_ref, dbg_ref):
        first = pl.program_id(0) == 0
        dxb = dx_ref[...].astype(BF16)
        dxb_ref[...] = dxb
        pa = _dot(oa_ref[...], wa_ref[...])
        pb = _dot(ob_ref[...], wb_ref[...])
        sa = _sigmoid(g_ref[:, 0:D_MODEL] + bg_ref[0:1, :])
        sbg = _sigmoid(g_ref[:, D_MODEL:2 * D_MODEL] + bg_ref[1:2, :])
        merged_ref[...] = (sa * pa + sbg * pb).astype(BF16)
        dm = _dot_nt(dxb, wo_ref[...])
        dpa = (dm * sa).astype(BF16)
        dpb = (dm * sbg).astype(BF16)
        dpa_ref[...] = dpa
        dpb_ref[...] = dpb
        dga = dm * pa * sa * (1.0 - sa)
        dgb = dm * pb * sbg * (1.0 - sbg)
        dgate_ref[:, 0:D_MODEL] = dga.astype(BF16)
        dgate_ref[:, D_MODEL:2 * D_MODEL] = dgb.astype(BF16)
        _acc_rows(dbg_ref.at[0:1, :], dga, first)
        _acc_rows(dbg_ref.at[1:2, :], dgb, first)
        doa_ref[...] = _dot_nt(dpa, wa_ref[...]).astype(BF16)
        dob_ref[...] = _dot_nt(dpb, wb_ref[...]).astype(BF16)

    full = lambda shape: pl.BlockSpec(shape, lambda i: (0, 0))
    rowb = lambda n: pl.BlockSpec((tm, n), lambda i: (i, 0))
    sds = lambda n, dt: jax.ShapeDtypeStruct((s_len, n), dt)
    return pl.pallas_call(
        body, name="merge_bwd", grid=(s_len // tm,),
        in_specs=[rowb(1024), rowb(512), rowb(512), rowb(2048), full((2, 1024)), full((512, 1024)),
                  full((512, 1024)), full((1024, 1024))],
        out_specs=[rowb(512), rowb(512), rowb(2048), rowb(1024), rowb(1024), rowb(1024), rowb(1024),
                   full((2, 1024))],
        out_shape=[sds(512, BF16), sds(512, BF16), sds(2048, BF16), sds(1024, BF16), sds(1024, BF16),
                   sds(1024, BF16), sds(1024, BF16), jax.ShapeDtypeStruct((2, 1024), F32)],
        compiler_params=_cparams(48, dimension_semantics=("arbitrary",)),
    )(dx1, oa, ob, gates, bg, wa, wb, wo)


def _mem_kv(mem, g, w):
    m_len = mem.shape[0]

    def body(mem_ref, g_ref, w_ref, mn_ref, kv_ref):
        mn, _, _ = _rms(mem_ref[...], g_ref[...])
        mnb = mn.astype(BF16)
        mn_ref[...] = mnb
        kv_ref[...] = _dot(mnb, w_ref[...]).astype(BF16)

    return pl.pallas_call(
        body, name="mem_kv",
        out_shape=[jax.ShapeDtypeStruct((m_len, D_MODEL), BF16), jax.ShapeDtypeStruct((m_len, 1024), BF16)],
    )(mem, g, w)


def _mem_bwd(mem, g, w, mn, dkv):
    def body(mem_ref, g_ref, w_ref, mn_ref, dkv_ref, dw_ref, dg_ref):
        dkvb = dkv_ref[...].astype(BF16)
        dw_ref[...] = _dot_tn(mn_ref[...], dkvb)
        dmn = _dot_nt(dkvb, w_ref[...])
        _, xh, _ = _rms(mem_ref[...], g_ref[...])
        dg_ref[...] = jnp.sum(dmn * xh, axis=0, keepdims=True)

    return pl.pallas_call(
        body, name="mem_bwd",
        out_shape=[jax.ShapeDtypeStruct((D_MODEL, 1024), F32), jax.ShapeDtypeStruct((1, D_MODEL), F32)],
    )(mem, g, w, mn, dkv)


def _xattn_heads(xqb, kv_ref, m_len):
    ps = []
    for h in range(X_HEADS):
        hs = slice(h * X_HEAD_DIM, (h + 1) * X_HEAD_DIM)
        s = _dot_nt(xqb[:, hs], kv_ref[:, hs]) * X_SCALE
        e = jnp.exp(s - jnp.max(s, axis=-1, keepdims=True))
        ps.append(e / jnp.sum(e, axis=-1, keepdims=True))
    return ps


def _xattn_fwd(x1, g, wxq, kv, wxo):
    s_len, m_len = x1.shape[0], kv.shape[0]
    tm = _row_block(s_len)

    def body(x_ref, g_ref, wq_ref, kv_ref, wo_ref, y_ref):
        hx, _, _ = _rms(x_ref[...], g_ref[...])
        xqb = _dot(hx.astype(BF16), wq_ref[...]).astype(BF16)
        ps = _xattn_heads(xqb, kv_ref, m_len)
        xo = jnp.concatenate(
            [_dot(ps[h].astype(BF16), kv_ref[:, 512 + h * X_HEAD_DIM:512 + (h + 1) * X_HEAD_DIM])
             for h in range(X_HEADS)], axis=-1)
        y_ref[...] = x_ref[...] + _dot(xo.astype(BF16), wo_ref[...])

    full = lambda shape: pl.BlockSpec(shape, lambda i: (0, 0))
    rowb = lambda n: pl.BlockSpec((tm, n), lambda i: (i, 0))
    return pl.pallas_call(
        body, name="xattn_fwd", grid=(s_len // tm,),
        in_specs=[rowb(1024), full((1, 1024)), full((1024, 512)), full((m_len, 1024)), full((512, 1024))],
        out_specs=rowb(1024),
        out_shape=jax.ShapeDtypeStruct((s_len, D_MODEL), F32),
        compiler_params=_cparams(48, dimension_semantics=("parallel",)),
    )(x1, g, wxq, kv, wxo)


def _xattn_bwd(x1, dx2, g, wxq, kv, wxo):
    s_len, m_len = x1.shape[0], kv.shape[0]
    tm = _row_block(s_len)

    def body(x_ref, dy_ref, g_ref, wq_ref, kv_ref, wo_ref, dx_ref, dwq_ref, dwo_ref, dkv_ref, dg_ref):
        first = pl.program_id(0) == 0
        gv = g_ref[...]
        hx, xh, r = _rms(x_ref[...], gv)
        hxb = hx.astype(BF16)
        xqb = _dot(hxb, wq_ref[...]).astype(BF16)
        ps = _xattn_heads(xqb, kv_ref, m_len)
        dy = dy_ref[...]
        dyb = dy.astype(BF16)
        dxo = _dot_nt(dyb, wo_ref[...])
        xos, dqs, dks, dvs = [], [], [], []
        for h in range(X_HEADS):
            hs = slice(h * X_HEAD_DIM, (h + 1) * X_HEAD_DIM)
            vs = slice(512 + h * X_HEAD_DIM, 512 + (h + 1) * X_HEAD_DIM)
            p = ps[h]
            pb = p.astype(BF16)
            dxoh = dxo[:, hs].astype(BF16)
            xos.append(_dot(pb, kv_ref[:, vs]))
            dp = _dot_nt(dxoh, kv_ref[:, vs])
            ds = (p * (dp - jnp.sum(dp * p, axis=-1, keepdims=True)) * X_SCALE).astype(BF16)
            dvs.append(_dot_tn(pb, dxoh))
            dks.append(_dot_tn(ds, xqb[:, hs]))
            dqs.append(_dot(ds, kv_ref[:, hs]))
        xob = jnp.concatenate(xos, axis=-1).astype(BF16)
        dxqb = jnp.concatenate(dqs, axis=-1).astype(BF16)
        _acc(dwo_ref, _dot_tn(xob, dyb), first)
        _acc(dwq_ref, _dot_tn(hxb, dxqb), first)
        _acc(dkv_ref, jnp.concatenate(dks + dvs, axis=-1), first)
        dhx = _dot_nt(dxqb, wq_ref[...])
        dx, dgr = _rms_bwd(dhx, xh, r, gv)
        dx_ref[...] = dy + dx
        _acc_rows(dg_ref, dgr, first)

    full = lambda shape: pl.BlockSpec(shape, lambda i: (0, 0))
    rowb = lambda n: pl.BlockSpec((tm, n), lambda i: (i, 0))
    return pl.pallas_call(
        body, name="xattn_bwd", grid=(s_len // tm,),
        in_specs=[rowb(1024), rowb(1024), full((1, 1024)), full((1024, 512)), full((m_len, 1024)),
                  full((512, 1024))],
        out_specs=[rowb(1024), full((1024, 512)), full((512, 1024)), full((m_len, 1024)), full((1, 1024))],
        out_shape=[jax.ShapeDtypeStruct((s_len, D_MODEL), F32), jax.ShapeDtypeStruct((1024, 512), F32),
                   jax.ShapeDtypeStruct((512, 1024), F32), jax.ShapeDtypeStruct((m_len, 1024), F32),
                   jax.ShapeDtypeStruct((1, D_MODEL), F32)],
        compiler_params=_cparams(48, dimension_semantics=("arbitrary",)),
    )(x1, dx2, g, wxq, kv, wxo)


FF_TILE = 1408


def _ffn_fwd(x2, g, wg, wu, wd):
    s_len = x2.shape[0]
    tm, tf = _row_block(s_len), FF_TILE

    def body(x_ref, g_ref, wg_ref, wu_ref, wd_ref, y_ref, h_ref):
        j = pl.program_id(1)

        @pl.when(j == 0)
        def _():
            hf, _, _ = _rms(x_ref[...], g_ref[...])
            h_ref[...] = hf.astype(BF16)
            y_ref[...] = x_ref[...]

        hb = h_ref[...]
        gt = _dot(hb, wg_ref[...])
        up = _dot(hb, wu_ref[...])
        act = gt * _sigmoid(gt) * up
        y_ref[...] += _dot(act.astype(BF16), wd_ref[...])

    rowb = pl.BlockSpec((tm, D_MODEL), lambda i, j: (i, 0))
    return pl.pallas_call(
        body, name="ffn_fwd", grid=(s_len // tm, D_FF // tf),
        in_specs=[rowb, pl.BlockSpec((1, D_MODEL), lambda i, j: (0, 0)),
                  pl.BlockSpec((D_MODEL, tf), lambda i, j: (0, j)),
                  pl.BlockSpec((D_MODEL, tf), lambda i, j: (0, j)),
                  pl.BlockSpec((tf, D_MODEL), lambda i, j: (j, 0))],
        out_specs=[rowb, rowb],
        out_shape=[jax.ShapeDtypeStruct((s_len, D_MODEL), F32), jax.ShapeDtypeStruct((s_len, D_MODEL), BF16)],
        compiler_params=_cparams(48, dimension_semantics=("parallel", "arbitrary")),
    )(x2, g, wg, wu, wd)


def _ffn_bwd(x2, hf, dx3, dx3b, g, wg, wu, wd):
    s_len = x2.shape[0]
    tm, tf = _row_block(s_len), FF_TILE
    nf = D_FF // tf

    def act_body(h_ref, dy_ref, wg_ref, wu_ref, wd_ref, dgt_ref, dup_ref, act_ref):
        hb = h_ref[...]
        gt = _dot(hb, wg_ref[...])
        up = _dot(hb, wu_ref[...])
        sg = _sigmoid(gt)
        silu = gt * sg
        dact = _dot_nt(dy_ref[...], wd_ref[...])
        dgt_ref[...] = (dact * up * (sg * (1.0 + gt * (1.0 - sg)))).astype(BF16)
        dup_ref[...] = (dact * silu).astype(BF16)
        act_ref[...] = (silu * up).astype(BF16)

    rowb = pl.BlockSpec((tm, D_MODEL), lambda i, j: (i, 0))
    ffb = pl.BlockSpec((tm, tf), lambda i, j: (i, j))
    dgt, dup, act = pl.pallas_call(
        act_body, name="ffn_bwd_act", grid=(s_len // tm, nf),
        in_specs=[rowb, rowb,
                  pl.BlockSpec((D_MODEL, tf), lambda i, j: (0, j)),
                  pl.BlockSpec((D_MODEL, tf), lambda i, j: (0, j)),
                  pl.BlockSpec((tf, D_MODEL), lambda i, j: (j, 0))],
        out_specs=[ffb, ffb, ffb],
        out_shape=[jax.ShapeDtypeStruct((s_len, D_FF), BF16)] * 3,
        compiler_params=_cparams(56, dimension_semantics=("parallel", "arbitrary")),
    )(hf, dx3b, wg, wu, wd)
    tm = min(s_len, 256)

    def in_body(x_ref, dy_ref, g_ref, wg_ref, wu_ref, dgt_ref, dup_ref, dx_ref, dg_ref):
        dh = _dot_nt(dgt_ref[...], wg_ref[...]) + _dot_nt(dup_ref[...], wu_ref[...])
        gv = g_ref[...]
        _, xh, r = _rms(x_ref[...], gv)
        dx, dgr = _rms_bwd(dh, xh, r, gv)
        dx_ref[...] = dy_ref[...] + dx
        _acc_rows(dg_ref, dgr, pl.program_id(0) == 0)

    row1 = lambda n: pl.BlockSpec((tm, n), lambda i: (i, 0))
    full = lambda shape: pl.BlockSpec(shape, lambda i: (0, 0))
    dx2, dg = pl.pallas_call(
        in_body, name="ffn_bwd_in", grid=(s_len // tm,),
        in_specs=[row1(D_MODEL), row1(D_MODEL), full((1, D_MODEL)), full((D_MODEL, D_FF)), full((D_MODEL, D_FF)),
                  row1(D_FF), row1(D_FF)],
        out_specs=[row1(D_MODEL), full((1, D_MODEL))],
        out_shape=[jax.ShapeDtypeStruct((s_len, D_MODEL), F32), jax.ShapeDtypeStruct((1, D_MODEL), F32)],
        compiler_params=_cparams(48, dimension_semantics=("arbitrary",)),
    )(x2, dx3, g, wg, wu, dgt, dup)
    return dx2, dgt, dup, act, dg


def _loss_head(x3, g, target):
    s_len = x3.shape[0]
    tm = _row_block(s_len)

    def body(x_ref, g_ref, t_ref, sse_ref, dx_ref, dxb_ref, dg_ref):
        first = pl.program_id(0) == 0
        gv = g_ref[...]
        y, xh, r = _rms(x_ref[...], gv)
        err = y - t_ref[...]
        _acc(sse_ref, jnp.broadcast_to(jnp.sum(err * err), (8, LANES)), first)
        dx, dgr = _rms_bwd(err * (1.0 / D_MODEL), xh, r, gv)
        dx_ref[...] = dx
        dxb_ref[...] = dx.astype(BF16)
        _acc_rows(dg_ref, dgr, first)

    rowb = pl.BlockSpec((tm, D_MODEL), lambda i: (i, 0))
    return pl.pallas_call(
        body, name="loss_head", grid=(s_len // tm,),
        in_specs=[rowb, pl.BlockSpec((1, D_MODEL), lambda i: (0, 0)), rowb],
        out_specs=[pl.BlockSpec((8, LANES), lambda i: (0, 0)), rowb, rowb,
                   pl.BlockSpec((1, D_MODEL), lambda i: (0, 0))],
        out_shape=[jax.ShapeDtypeStruct((8, LANES), F32), jax.ShapeDtypeStruct((s_len, D_MODEL), F32),
                   jax.ShapeDtypeStruct((s_len, D_MODEL), BF16), jax.ShapeDtypeStruct((1, D_MODEL), F32)],
        compiler_params=_cparams(dimension_semantics=("arbitrary",)),
    )(x3, g, target)


def _mla_prep_bwd(lat, g_q, g_kv, w_uq, w_uk, w_uv, cosf, sinf, dq, dk, dv):
    s_len = lat.shape[0]
    tm = _row_block(s_len)

    def body(lat_ref, gq_ref, gkv_ref, wuq_ref, wuk_ref, wuv_ref, cos_ref, sin_ref, dq_ref, dk_ref, dv_ref,
             dlat_ref, dqb_ref, dkb_ref, dvb_ref, dgq_ref, dgkv_ref):
        first = pl.program_id(0) == 0
        lane = lax.broadcasted_iota(jnp.int32, (tm, LANES), 1)
        cosv, sinv = cos_ref[...], sin_ref[...]
        gq, gkv = gq_ref[...], gkv_ref[...]
        _, qxh, qr = _rms(lat_ref[:, 0:256], gq)
        _, kxh, kr_ = _rms(lat_ref[:, 256:384], gkv)
        dkr = jnp.zeros((tm, LANES), F32)
        for h in range(MLA_HEADS):
            sl = slice(h * HEAD_PAD, (h + 1) * HEAD_PAD)
            blk = dq_ref[:, sl]
            dqb_ref[:, sl] = (blk * cosv + _rope_rot_t(blk, lane) * sinv).astype(BF16)
            kblk = dk_ref[:, sl] * (1.0 / MLA_Q_FOLD)
            dkb_ref[:, sl] = kblk.astype(BF16)
            dkr = dkr + kblk
        dvb = dv_ref[...].astype(BF16)
        dvb_ref[...] = dvb
        dkr = jnp.where((lane >= 64) & (lane < 96), dkr, 0.0)
        dkr = dkr * cosv + _rope_rot_t(dkr, lane) * sinv
        dql = _dot_nt(dqb_ref[...], wuq_ref[...])
        dkvl = _dot_nt(dkb_ref[...], wuk_ref[...]) + _dot_nt(dvb, wuv_ref[...])
        dcq, dgqr = _rms_bwd(dql, qxh, qr, gq)
        dckv, dgkvr = _rms_bwd(dkvl, kxh, kr_, gkv)
        dlat_ref[:, 0:256] = dcq
        dlat_ref[:, 256:384] = dckv
        dlat_ref[:, K_R_OFF:K_R_OFF + LANES] = pltpu.roll(dkr, 64, 1)
        _acc_rows(dgq_ref, dgqr, first)
        _acc_rows(dgkv_ref, dgkvr, first)

    full = lambda shape: pl.BlockSpec(shape, lambda i: (0, 0))
    rowb = lambda n: pl.BlockSpec((tm, n), lambda i: (i, 0))
    sds = lambda n, dt: jax.ShapeDtypeStruct((s_len, n), dt)
    return pl.pallas_call(
        body, name="mla_prep_bwd", grid=(s_len // tm,),
        in_specs=[rowb(512), full((1, 256)), full((1, 128)), full((256, 1024)), full((128, 1024)),
                  full((128, 512)), rowb(128), rowb(128), rowb(1024), rowb(1024), rowb(512)],
        out_specs=[rowb(512), rowb(1024), rowb(1024), rowb(512), full((1, 256)), full((1, 128))],
        out_shape=[sds(512, F32), sds(1024, BF16), sds(1024, BF16), sds(512, BF16),
                   jax.ShapeDtypeStruct((1, 256), F32), jax.ShapeDtypeStruct((1, 128), F32)],
        compiler_params=_cparams(48, dimension_semantics=("arbitrary",)),
    )(lat, g_q, g_kv, w_uq, w_uk, w_uv, cosf, sinf, dq, dk, dv)


def _in_proj_bwd(x, g, w, dx1, dlat, dsbq, dsbk, dsbv, dgates, ride):
    s_len = x.shape[0]
    tm = min(s_len, 256)
    nb = s_len // tm

    def body(x_ref, g_ref, w_ref, dx1_ref, dlat_ref, dq_ref, dk_ref, dv_ref, dgate_ref, *rest):
        gx_ref, dg_ref = rest[ride.n:ride.n + 2]
        dproj = rest[-1]
        ride.bind(rest[:ride.n], rest[ride.n + 2:2 * ride.n + 2], rest[2 * ride.n + 2:-1])
        pl.when(pl.program_id(0) == 0)(ride.issue)
        dproj[:, 0:512] = dlat_ref[...].astype(BF16)
        dproj[:, 512:1024] = dq_ref[...].astype(BF16)
        dproj[:, 1024:1536] = (dk_ref[...] * LN2).astype(BF16)
        dproj[:, 1536:2048] = dv_ref[...].astype(BF16)
        dproj[:, 2048:4096] = dgate_ref[...]
        dh = _dot_nt(dproj[...], w_ref[...])
        gv = g_ref[...]
        _, xh, r = _rms(x_ref[...], gv)
        dx, dgr = _rms_bwd(dh, xh, r, gv)
        gx_ref[...] = dx1_ref[...] + dx
        _acc_rows(dg_ref, dgr, pl.program_id(0) == 0)
        pl.when(pl.program_id(0) == nb - 1)(ride.finish)

    rowb = lambda n: pl.BlockSpec((tm, n), lambda i: (i, 0))
    full = lambda shape: pl.BlockSpec(shape, lambda i: (0, 0))
    outs = pl.pallas_call(
        body, name="in_proj_bwd", grid=(nb,),
        in_specs=[rowb(D_MODEL), full((1, D_MODEL)), full((D_MODEL, D_IN_PAD)), rowb(D_MODEL),
                  rowb(512), rowb(512), rowb(512), rowb(512), rowb(2 * D_MODEL)] + ride.specs,
        out_specs=[rowb(D_MODEL), full((1, D_MODEL))] + ride.specs,
        out_shape=[jax.ShapeDtypeStruct((s_len, D_MODEL), F32), jax.ShapeDtypeStruct((1, D_MODEL), F32)]
        + ride.out_shape,
        scratch_shapes=ride.scratch + [pltpu.VMEM((tm, D_IN_PAD), BF16)],
        compiler_params=_cparams(48, dimension_semantics=("arbitrary",)),
    )(x, g, w, dx1, dlat, dsbq, dsbk, dsbv, dgates, *ride.srcs)
    return outs[0], outs[1], outs[2:]


def _adamw(landed, w, m, v, name):
    r, c = w.shape
    lanes = _round_up(c, LANES)
    tb = r
    for cand in range(r, 0, -1):
        if r % cand == 0 and (cand % 8 == 0 or cand == r) and N_DEV * cand * lanes * 4 <= ADAM_BLOCK_BYTES:
            tb = cand
            break
    c1 = 1.0 - ADAM_B1 ** ADAM_STEP
    c2 = 1.0 - ADAM_B2 ** ADAM_STEP

    def body(l_ref, w_ref, m_ref, v_ref, g_ref, d_ref, nm_ref, nv_ref):
        g = l_ref[0]
        for k in range(1, N_DEV):
            g = g + l_ref[k]
        nm = ADAM_B1 * m_ref[...] + (1.0 - ADAM_B1) * g
        nv = ADAM_B2 * v_ref[...] + (1.0 - ADAM_B2) * (g * g)
        g_ref[...] = g
        nm_ref[...] = nm
        nv_ref[...] = nv
        d_ref[...] = -ADAM_LR * ((nm / c1) / (jnp.sqrt(nv / c2) + ADAM_EPS) + ADAM_WD * w_ref[...])

    blk = pl.BlockSpec((tb, c), lambda i: (i, 0))
    return pl.pallas_call(
        body, name=name, grid=(r // tb,),
        in_specs=[pl.BlockSpec((N_DEV, tb, c), lambda i: (0, i, 0)), blk, blk, blk],
        out_specs=[blk, blk, blk, blk],
        out_shape=[jax.ShapeDtypeStruct((r, c), F32)] * 4,
        compiler_params=_cparams(dimension_semantics=("parallel",)),
    )(landed, w, m, v)


def _shard_shape(shape, axis):
    return tuple(d // N_DEV if a == axis else d for a, d in enumerate(shape))


def _split_pieces(full, axis):
    r, c = full.shape
    if axis == 0:
        return full.reshape(N_DEV, r // N_DEV, c)
    return full.reshape(r, N_DEV, c // N_DEV).transpose(1, 0, 2)


def _join_shards(gathered, axis):
    _, r, c = gathered.shape
    if axis == 0:
        return gathered.reshape(N_DEV * r, c)
    return gathered.transpose(1, 0, 2).reshape(r, N_DEV * c)


def kernel(x, mem, positions, g_mix, w_in, b_gate, g_q_lat, w_uq, g_kv_lat, w_ukv, w_a_proj, w_b_proj, w_o, g_x, g_mem, w_xq, w_xkv, w_xo, g_ffn, w_gate, w_up, w_down, g_final, loss_target, m_g_mix, m_w_in, m_b_gate, m_g_q_lat, m_w_uq, m_g_kv_lat, m_w_ukv, m_w_a_proj, m_w_b_proj, m_w_o, m_g_x, m_g_mem, m_w_xq, m_w_xkv, m_w_xo, m_g_ffn, m_w_gate, m_w_up, m_w_down, m_g_final, v_g_mix, v_w_in, v_b_gate, v_g_q_lat, v_w_uq, v_g_kv_lat, v_w_ukv, v_w_a_proj, v_w_b_proj, v_w_o, v_g_x, v_g_mem, v_w_xq, v_w_xkv, v_w_xo, v_g_ffn, v_w_gate, v_w_up, v_w_down, v_g_final):
    given = dict(locals())
    s_len = x.shape[1]
    x2d = x.reshape(s_len, D_MODEL)
    mem2d = mem.reshape(-1, D_MODEL)
    target = loss_target.reshape(s_len, D_MODEL)

    names = [name for name, _, _ in SHARDED]
    axis_of = {name: axis for name, _, axis in SHARDED}
    shard2d = lambda name, prefix="": given[prefix + name].reshape(
        _shard_shape(dict((n, s) for n, s, _ in SHARDED)[name], axis_of[name]))

    wire = lambda name: shard2d(name) if name == "b_gate" else shard2d(name).astype(BF16)
    early = [n for n in names if n in NEEDED_FIRST]
    late = [n for n in names if n not in NEEDED_FIRST]
    gathered = _exchange(True, [wire(n) for n in early], "weights_gather_first")
    wts = {n: _join_shards(g, axis_of[n]) for n, g in zip(early, gathered)}

    w_in_p = jnp.concatenate([wts["w_in"][:, :416], jnp.zeros((D_MODEL, 96), BF16), wts["w_in"][:, 416:]], axis=1)
    w_uq_p = jnp.pad(wts["w_uq"].reshape(256, MLA_HEADS, 96), ((0, 0), (0, 0), (0, 32))).reshape(256, 1024)
    ukv = wts["w_ukv"].reshape(128, MLA_HEADS, 128)
    w_uk_p = jnp.pad(ukv[:, :, :64], ((0, 0), (0, 0), (0, 64))).reshape(128, 1024)
    w_uv = ukv[:, :, 64:].reshape(128, 512)
    w_uv1 = jnp.pad(ukv[:, :, 64:], ((0, 0), (0, 0), (0, 64))).reshape(128, 1024)
    bg = wts["b_gate"]

    inv_freq = ROPE_THETA ** (-jnp.arange(0, MLA_ROPE, 2, dtype=F32) / MLA_ROPE)
    ang = positions.reshape(s_len).astype(F32)[:, None] * inv_freq
    cos16, sin16 = jnp.cos(ang), jnp.sin(ang)
    cosf = jnp.concatenate([jnp.ones((s_len, 64), F32), cos16, cos16, jnp.ones((s_len, 32), F32)], axis=1)
    sinf = jnp.concatenate([jnp.zeros((s_len, 64), F32), sin16, sin16, jnp.zeros((s_len, 32), F32)], axis=1)

    h1, lat, sb, gates = _in_proj(x2d, g_mix, w_in_p)
    qa, ka, va, va1, q_lat, kv_lat = _mla_prep(lat, g_q_lat, g_kv_lat, w_uq_p, w_uk_p, w_uv, w_uv1, cosf, sinf)
    oa, lse, gathered = _mla_fwd(qa, ka, va1, _Exchange(True, [wire(n) for n in late]))
    wts.update({n: _join_shards(g, axis_of[n]) for n, g in zip(late, gathered)})
    ob, sb_r = _sb_fwd(sb)
    x1 = _merge_fwd(x2d, oa, ob, gates, bg, wts["w_a_proj"], wts["w_b_proj"], wts["w_o"])
    mn, xkv = _mem_kv(mem2d, g_mem, wts["w_xkv"])
    x2 = _xattn_fwd(x1, g_x, wts["w_xq"], xkv, wts["w_xo"])
    x3, hf = _ffn_fwd(x2, g_ffn, wts["w_gate"], wts["w_up"], wts["w_down"])
    g_final2d = g_final.reshape(1, D_MODEL)
    sse, dx3, dx3b, dg_final = _loss_head(x3, g_final2d, target)
    loss = lax.psum(sse[0, 0] * (0.5 / D_MODEL), ("x", "y", "c"))

    dx2, dgt, dup, act, dg_ffn = _ffn_bwd(x2, hf, dx3, dx3b, g_ffn, wts["w_gate"], wts["w_up"], wts["w_down"])
    dx1, dw_xq, dw_xo, dxkv, dg_x = _xattn_bwd(x1, dx2, g_x, wts["w_xq"], xkv, wts["w_xo"])
    dw_xkv, dg_mem = _mem_bwd(mem2d, g_mem, wts["w_xkv"], mn, dxkv)
    doa, dob, dgates, dpa, dpb, merged, dx1b, dbg = _merge_bwd(
        dx1, oa, ob, gates, bg, wts["w_a_proj"], wts["w_b_proj"], wts["w_o"])
    dsbq, dsbk, dsbv = _sb_bwd(sb, dob, sb_r)
    full_grads = {
        "w_a_proj": _tn_matmul(oa, dpa, "dw_a"),
        "w_b_proj": _tn_matmul(ob, dpb, "dw_b"),
        "w_o": _tn_matmul(merged, dx1b, "dw_o"),
        "w_xq": dw_xq,
        "w_xkv": dw_xkv,
        "w_xo": dw_xo,
        "w_gate": _tn_matmul(hf, dgt, "dw_gate", tn=FF_TILE),
        "w_up": _tn_matmul(hf, dup, "dw_up", tn=FF_TILE),
        "w_down": _tn_matmul(act, dx3b, "dw_down", tka=FF_TILE),
    }
    dw_sbq = _tn_matmul(h1, dsbq, "dw_in_sbq")
    own0 = D_IN // N_DEV - 416
    dw_in_rest = jnp.concatenate([
        dw_sbq[:, own0:], _tn_matmul(h1, dsbk, "dw_in_sbk", scale=LN2), _tn_matmul(h1, dsbv, "dw_in_sbv"),
        _tn_matmul(h1, dgates, "dw_in_gates")], axis=1)
    w_in_rest = jnp.pad(dw_in_rest.reshape(D_MODEL, N_DEV - 1, D_IN // N_DEV).transpose(1, 0, 2),
                        ((1, 0), (0, 0), (0, 0)))
    dqa, dka, dva, got = _mla_bwd(
        qa, ka, va, oa, doa, lse,
        _Exchange(False, [_split_pieces(full_grads[n], axis_of[n]) for n in late] + [w_in_rest]))
    landed = dict(zip(late, got[:-1]))
    dlat, dqb, dkb, dvb, dg_q, dg_kv = _mla_prep_bwd(
        lat, g_q_lat, g_kv_lat, w_uq_p, w_uk_p, w_uv, cosf, sinf, dqa, dka, dva)
    w_in_first = jnp.concatenate([_tn_matmul(h1, dlat, "dw_in_lat")[:, :416], dw_sbq[:, :own0]], axis=1)
    dw_uq_p = _tn_matmul(q_lat, dqb, "dw_uq")
    dw_uk_p = _tn_matmul(kv_lat, dkb, "dw_uk")
    dw_uv = _tn_matmul(kv_lat, dvb, "dw_uv")
    full_grads.update({
        "b_gate": dbg,
        "w_uq": dw_uq_p.reshape(256, MLA_HEADS, 128)[:, :, :96].reshape(256, 768),
        "w_ukv": jnp.concatenate([dw_uk_p.reshape(128, MLA_HEADS, 128)[:, :, :64],
                                  dw_uv.reshape(128, MLA_HEADS, 64)], axis=2).reshape(128, 1024),
    })
    small = [n for n in early if n != "w_in"]
    grad_x, dg_mix, got2 = _in_proj_bwd(
        x2d, g_mix, w_in_p, dx1, dlat, dsbq, dsbk, dsbv, dgates,
        _Exchange(False, [_split_pieces(full_grads[n], axis_of[n]) for n in small], to_first=[w_in_first]))
    landed.update(zip(small, got2[:-1]))
    me = 4 * lax.axis_index("x") + 2 * lax.axis_index("y") + lax.axis_index("c")
    landed["w_in"] = jnp.where(me == 0, got2[-1], got[-1])
    rep_grads = {"g_mix": dg_mix, "g_q_lat": dg_q, "g_kv_lat": dg_kv, "g_x": dg_x, "g_mem": dg_mem,
                 "g_ffn": dg_ffn, "g_final": dg_final}
    rep_cat = lambda prefix, src: jnp.concatenate(
        [src[prefix + n].reshape(-1) for n, _ in REPLICATED]).reshape(-1, LANES)
    rep_src = jnp.broadcast_to(rep_cat("", rep_grads), (N_DEV,) + rep_cat("", rep_grads).shape)
    rep_landed = _exchange(False, [rep_src], "grads_gains")[0]

    res = {}
    for name, _, _ in SHARDED:
        outs = _adamw(landed[name], shard2d(name), shard2d(name, "m_"), shard2d(name, "v_"), "adamw_" + name)
        res[name] = [o.reshape(given[name].shape) for o in outs]
    rep_outs = _adamw(rep_landed, rep_cat("", given), rep_cat("m_", given), rep_cat("v_", given), "adamw_gains")
    off = 0
    for name, n in REPLICATED:
        res[name] = [o.reshape(-1)[off:off + n].reshape(given[name].shape) for o in rep_outs]
        off += n
    result = [loss, grad_x.reshape(x.shape)]
    for k in range(4):
        result.extend(res[name][k] for name in WEIGHT_ORDER)
    return tuple(result)
```

```python
import functools
import math

import jax
import jax.numpy as jnp
from jax import lax
from jax.experimental import pallas as pl
from jax.experimental.pallas import tpu as pltpu

F32 = jnp.float32
BF16 = jnp.bfloat16

D_MODEL = 1024
MLA_HEADS = 8
MLA_Q_RANK = 256
MLA_KV_RANK = 128
MLA_NOPE = 64
MLA_ROPE = 32
ROPE_THETA = 10000.0
SB_WIDTH = 512
X_HEADS = 4
X_HEAD_DIM = 128
D_FF = 2816
EPS = 1e-6
D_IN = 4000
D_IN_PAD = 4096
K_R_OFF = 384
LAT_COLS = 416
LANES = 128
HEAD_PAD = 128
MLA_SCALE = 1.0 / math.sqrt(MLA_NOPE + MLA_ROPE)
SB_SCALE = 0.125
LOG2E = math.log2(math.e)
LN2 = math.log(2.0)
MLA_Q_FOLD = MLA_SCALE * LOG2E
SB_Q_FOLD = SB_SCALE * LOG2E
SB_CUT = -160.0
X_SCALE = 1.0 / math.sqrt(X_HEAD_DIM)
NEG_BIG = -1e30

ADAM_LR = 0.001
ADAM_B1 = 0.9
ADAM_B2 = 0.999
ADAM_EPS = 1e-08
ADAM_WD = 0.01
ADAM_STEP = 10

N_DEV = 8
MIB = 1024 * 1024
ADAM_BLOCK_BYTES = 4 * MIB

SHARDED = (
    ("w_in", (D_MODEL, D_IN), 1),
    ("b_gate", (2, D_MODEL), 1),
    ("w_uq", (MLA_Q_RANK, 768), 1),
    ("w_ukv", (MLA_KV_RANK, 1024), 1),
    ("w_a_proj", (512, D_MODEL), 1),
    ("w_b_proj", (512, D_MODEL), 1),
    ("w_o", (D_MODEL, D_MODEL), 0),
    ("w_xq", (D_MODEL, 512), 0),
    ("w_xkv", (D_MODEL, 1024), 0),
    ("w_xo", (512, D_MODEL), 1),
    ("w_gate", (D_MODEL, D_FF), 1),
    ("w_up", (D_MODEL, D_FF), 1),
    ("w_down", (D_FF, D_MODEL), 0),
)
NEEDED_FIRST = ("w_in", "b_gate", "w_uq", "w_ukv")
REPLICATED = (
    ("g_mix", 1024), ("g_q_lat", 256), ("g_kv_lat", 128), ("g_x", 1024),
    ("g_mem", 1024), ("g_ffn", 1024), ("g_final", 1024),
)
WEIGHT_ORDER = ("g_mix", "w_in", "b_gate", "g_q_lat", "w_uq", "g_kv_lat", "w_ukv", "w_a_proj",
                "w_b_proj", "w_o", "g_x", "g_mem", "w_xq", "w_xkv", "w_xo", "g_ffn", "w_gate",
                "w_up", "w_down", "g_final")


def _round_up(n, m):
    return -(-n // m) * m


def _cparams(vmem_mib=None, **kw):
    if vmem_mib is not None:
        kw["vmem_limit_bytes"] = vmem_mib * MIB
    return pltpu.CompilerParams(**kw)


def _dot(a, b):
    return jnp.dot(a, b, preferred_element_type=F32)


def _dot_nt(a, b):
    return lax.dot_general(a, b, (((1,), (1,)), ((), ())), preferred_element_type=F32)


def _dot_tn(a, b):
    return lax.dot_general(a, b, (((0,), (0,)), ((), ())), preferred_element_type=F32)


def _rms(x, g):
    r = lax.rsqrt(jnp.mean(x * x, axis=-1, keepdims=True) + EPS)
    xh = x * r
    return xh * g, xh, r


def _rms_bwd(dy, xh, r, g):
    u = dy * g
    dx = r * (u - xh * jnp.mean(u * xh, axis=-1, keepdims=True))
    return dx, dy * xh


def _sigmoid(z):
    return 1.0 / (1.0 + jnp.exp(-z))


def _acc_rows(ref, val, first):
    s = jnp.sum(val, axis=0, keepdims=True)

    @pl.when(first)
    def _():
        ref[...] = s

    @pl.when(jnp.logical_not(first))
    def _():
        ref[...] += s


def _acc(ref, val, first):
    @pl.when(first)
    def _():
        ref[...] = val

    @pl.when(jnp.logical_not(first))
    def _():
        ref[...] += val


def _peer(k):
    x, y, c = lax.axis_index("x"), lax.axis_index("y"), lax.axis_index("c")
    px = 1 - x if (k >> 2) & 1 else x
    py = 1 - y if (k >> 1) & 1 else y
    pc = 1 - c if k & 1 else c
    return (px, py, pc), 4 * px + 2 * py + pc


N_PEERS = N_DEV - 1
OTHER_CHIPS = (2, 4, 6)


def _land_shape(gather, src):
    return (N_DEV,) + src.shape if gather else src.shape


class _Exchange:
    def __init__(self, gather, srcs, to_first=()):
        self.gather, self.m, self.srcs = gather, len(srcs), list(srcs) + list(to_first)
        self.n = len(self.srcs)
        self.out_shape = ([jax.ShapeDtypeStruct(_land_shape(gather, s), s.dtype) for s in srcs]
                          + [jax.ShapeDtypeStruct(_land_shape(True, s), s.dtype) for s in to_first])
        self.specs = [pl.BlockSpec(memory_space=pl.ANY)] * self.n
        self.scratch = [pltpu.SemaphoreType.DMA((self.n * N_PEERS,)), pltpu.SemaphoreType.DMA((self.n * N_PEERS,)),
                        pltpu.SemaphoreType.DMA((self.n,))]

    def bind(self, src, land, sems):
        self.src, self.land = src, land
        self.send_sems, self.recv_sems, self.local_sems = sems

    def _copy(self, a, k, source, to, target=1):
        return pltpu.make_async_remote_copy(
            src_ref=source, dst_ref=to,
            send_sem=self.send_sems.at[a * N_PEERS + k - 1], recv_sem=self.recv_sems.at[a * N_PEERS + k - 1],
            device_id=_peer(target)[0], device_id_type=pl.DeviceIdType.MESH)

    def _row(self, a, k):
        return self.land[a].at[_peer(k)[1]]

    def _mine(self, a):
        me = _peer(0)[1]
        whole = self.gather or a >= self.m
        return pltpu.make_async_copy(self.src[a] if whole else self.src[a].at[me], self.land[a].at[me],
                                     self.local_sems.at[a])

    def issue(self):
        me = _peer(0)[1]
        for a in range(self.m):
            self._mine(a).start()
            for k in ((1,) + OTHER_CHIPS if self.gather else range(1, N_DEV)):
                source = self.src[a] if self.gather else self.src[a].at[_peer(k)[1]]
                self._copy(a, k, source, self.land[a].at[me], target=k).start()
        for a in range(self.m, self.n):
            pl.when(me == 0)(self._mine(a).start)
            for k in range(1, N_DEV):
                pl.when(me == k)(self._copy(a, k, self.src[a], self.land[a].at[me], target=k).start)

    def finish(self):
        me = _peer(0)[1]
        part = lambda a: self.src[a] if self.gather or a >= self.m else self.src[a].at[me]
        if self.gather:
            for a in range(self.m):
                for k in OTHER_CHIPS:
                    self._copy(a, k, part(a), self._row(a, k)).wait_recv()
                    self._copy(a, k + 1, self._row(a, k), self._row(a, k), target=1).start()
        for a in range(self.m):
            for k in ((1, 3, 5, 7) if self.gather else range(1, N_DEV)):
                self._copy(a, k, part(a), self._row(a, k)).wait_recv()
        for a in range(self.m):
            for k in range(1, N_DEV):
                self._copy(a, k, part(a), self.land[a].at[me]).wait_send()
            self._mine(a).wait()
        for a in range(self.m, self.n):
            for k in range(1, N_DEV):
                pl.when(me == 0)(self._copy(a, k, part(a), self._row(a, k)).wait_recv)
                pl.when(me == k)(self._copy(a, k, part(a), self.land[a].at[me]).wait_send)
            pl.when(me == 0)(self._mine(a).wait)


def _exchange(gather, srcs, name):
    ex = _Exchange(gather, srcs)

    def body(*refs):
        ex.bind(refs[:ex.n], refs[ex.n:2 * ex.n], refs[2 * ex.n:])
        ex.issue()
        ex.finish()

    return pl.pallas_call(body, name=name, out_shape=ex.out_shape, in_specs=ex.specs, out_specs=ex.specs,
                          scratch_shapes=ex.scratch)(*ex.srcs)


def _tn_matmul(a, b, name, tka=512, tn=1024, ts=2048):
    s_len, ka = a.shape
    n = b.shape[1]
    tka, tn, ts = min(tka, ka), min(tn, n), min(ts, s_len)
    assert ka % tka == 0 and n % tn == 0 and s_len % ts == 0

    def body(a_ref, b_ref, o_ref):
        _acc(o_ref, _dot_tn(a_ref[...], b_ref[...].astype(BF16)), pl.program_id(2) == 0)

    return pl.pallas_call(
        body, name=name, grid=(ka // tka, n // tn, s_len // ts),
        in_specs=[pl.BlockSpec((ts, tka), lambda i, j, s: (s, i)),
                  pl.BlockSpec((ts, tn), lambda i, j, s: (s, j))],
        out_specs=pl.BlockSpec((tka, tn), lambda i, j, s: (i, j)),
        out_shape=jax.ShapeDtypeStruct((ka, n), F32),
        compiler_params=_cparams(dimension_semantics=("parallel", "parallel", "arbitrary")),
    )(a, b)


def _tn_matmul_sb(a, dq, dk, dv, name, tka=512, ts=2048):
    s_len, ka = a.shape
    tka, ts = min(tka, ka), min(ts, s_len)
    assert ka % tka == 0 and s_len % ts == 0

    def body(a_ref, q_ref, k_ref, v_ref, o_ref):
        first = pl.program_id(1) == 0
        av = a_ref[...]
        for c, val in enumerate((q_ref[...], k_ref[...] * LN2, v_ref[...])):
            _acc(o_ref.at[:, c * SB_WIDTH:(c + 1) * SB_WIDTH], _dot_tn(av, val.astype(BF16)), first)

    colb = pl.BlockSpec((ts, SB_WIDTH), lambda i, s: (s, 0))
    return pl.pallas_call(
        body, name=name, grid=(ka // tka, s_len // ts),
        in_specs=[pl.BlockSpec((ts, tka), lambda i, s: (s, i)), colb, colb, colb],
        out_specs=pl.BlockSpec((tka, 3 * SB_WIDTH), lambda i, s: (i, 0)),
        out_shape=jax.ShapeDtypeStruct((ka, 3 * SB_WIDTH), F32),
        compiler_params=_cparams(48, dimension_semantics=("parallel", "arbitrary")),
    )(a, dq, dk, dv)


def _row_block(s_len):
    return min(s_len, 512)


def _in_proj(x, g, w):
    s_len = x.shape[0]
    tm = _row_block(s_len)

    def body(x_ref, g_ref, w_ref, h_ref, lat_ref, sb_ref, gate_ref):
        h, _, _ = _rms(x_ref[...], g_ref[...])
        hb = h.astype(BF16)
        h_ref[...] = hb
        p = _dot(hb, w_ref[:, 0:1024])
        lat_ref[...] = p[:, 0:512]
        sb_ref[:, 0:512] = (p[:, 512:1024] * SB_Q_FOLD).astype(BF16)
        sb_ref[:, 512:1536] = _dot(hb, w_ref[:, 1024:2048]).astype(BF16)
        gate_ref[:, 0:1024] = _dot(hb, w_ref[:, 2048:3072])
        gate_ref[:, 1024:2048] = _dot(hb, w_ref[:, 3072:4096])

    rowb = lambda n: pl.BlockSpec((tm, n), lambda i: (i, 0))
    return pl.pallas_call(
        body, name="in_proj", grid=(s_len // tm,),
        in_specs=[rowb(D_MODEL), pl.BlockSpec((1, D_MODEL), lambda i: (0, 0)),
                  pl.BlockSpec((D_MODEL, D_IN_PAD), lambda i: (0, 0))],
        out_specs=[rowb(D_MODEL), rowb(512), rowb(3 * SB_WIDTH), rowb(2 * D_MODEL)],
        out_shape=[jax.ShapeDtypeStruct((s_len, D_MODEL), BF16),
                   jax.ShapeDtypeStruct((s_len, 512), F32),
                   jax.ShapeDtypeStruct((s_len, 3 * SB_WIDTH), BF16),
                   jax.ShapeDtypeStruct((s_len, 2 * D_MODEL), F32)],
        compiler_params=_cparams(48, dimension_semantics=("parallel",)),
    )(x, g, w)


def _rope_rot(blk, lane):
    return jnp.where(lane < 80, -pltpu.roll(blk, 112, 1), pltpu.roll(blk, 16, 1))


def _rope_rot_t(blk, lane):
    return jnp.where(lane < 80, pltpu.roll(blk, 112, 1), -pltpu.roll(blk, 16, 1))


def _mla_prep(lat, g_q, g_kv, w_uq, w_uk, w_uv, w_uv1, cosf, sinf):
    s_len = lat.shape[0]
    tm = _row_block(s_len)

    def body(lat_ref, gq_ref, gkv_ref, wuq_ref, wuk_ref, wuv_ref, wuv1_ref, cos_ref, sin_ref,
             q_ref, k_ref, v_ref, v1_ref, ql_ref, kvl_ref):
        lane = lax.broadcasted_iota(jnp.int32, (tm, LANES), 1)
        cosv, sinv = cos_ref[...], sin_ref[...]
        ql, _, _ = _rms(lat_ref[:, 0:256], gq_ref[...])
        kvl, _, _ = _rms(lat_ref[:, 256:384], gkv_ref[...])
        qlb, kvlb = ql.astype(BF16), kvl.astype(BF16)
        ql_ref[...] = qlb
        kvl_ref[...] = kvlb
        q = _dot(qlb, wuq_ref[...])
        kn = _dot(kvlb, wuk_ref[...])
        v_ref[...] = _dot(kvlb, wuv_ref[...]).astype(BF16)
        v1 = _dot(kvlb, wuv1_ref[...])
        wide = lax.broadcasted_iota(jnp.int32, (tm, MLA_HEADS * HEAD_PAD), 1)
        v1_ref[...] = jnp.where(wide % HEAD_PAD == MLA_NOPE, 1.0, v1).astype(BF16)
        kr = pltpu.roll(lat_ref[:, K_R_OFF:K_R_OFF + LANES], 64, 1)
        kr = kr * cosv + _rope_rot(kr, lane) * sinv
        for h in range(MLA_HEADS):
            sl = slice(h * HEAD_PAD, (h + 1) * HEAD_PAD)
            blk = q[:, sl]
            q_ref[:, sl] = ((blk * cosv + _rope_rot(blk, lane) * sinv) * MLA_Q_FOLD).astype(BF16)
            k_ref[:, sl] = (kn[:, sl] + kr).astype(BF16)

    full = lambda shape: pl.BlockSpec(shape, lambda i: (0, 0))
    rowb = lambda n: pl.BlockSpec((tm, n), lambda i: (i, 0))
    return pl.pallas_call(
        body, name="mla_prep", grid=(s_len // tm,),
        in_specs=[rowb(512), full((1, 256)), full((1, 128)), full((256, 1024)), full((128, 1024)),
                  full((128, 512)), full((128, 1024)), rowb(128), rowb(128)],
        out_specs=[rowb(1024), rowb(1024), rowb(512), rowb(1024), rowb(256), rowb(128)],
        out_shape=[jax.ShapeDtypeStruct((s_len, 1024), BF16), jax.ShapeDtypeStruct((s_len, 1024), BF16),
                   jax.ShapeDtypeStruct((s_len, 512), BF16), jax.ShapeDtypeStruct((s_len, 1024), BF16),
                   jax.ShapeDtypeStruct((s_len, 256), BF16), jax.ShapeDtypeStruct((s_len, 128), BF16)],
        compiler_params=_cparams(dimension_semantics=("parallel",)),
    )(lat, g_q, g_kv, w_uq, w_uk, w_uv, w_uv1, cosf, sinf)


ATTN_TQ = 1024
ATTN_TH = 512
ATTN_TK = 256
SB_TQ = 512
SB_TH = 256
MLA_TK = 512


def _attn_blocks(s_len, tk=ATTN_TK, tq=ATTN_TQ, th=ATTN_TH):
    tq, th, tk = min(s_len, tq), min(s_len, th), min(s_len, tk)
    return tq, th, tk, tq // tk


def _chains(tq, th):
    return [(hh, r0) for hh in range(2) for r0 in range(0, tq, th)]


def _diag_mask(th, tk, r0, sub, strict):
    lo, hi = sub * tk, (sub + 1) * tk - 1
    last, first = r0 + th - 1, r0
    if (lo >= last) if strict else (lo > last):
        return "none"
    if (hi < first) if strict else (hi <= first):
        return "all"
    row = lax.broadcasted_iota(jnp.int32, (th, tk), 0) + r0
    col = lax.broadcasted_iota(jnp.int32, (th, tk), 1) + lo
    return col < row if strict else col <= row


def _mla_fwd(q, k, v, ride):
    s_len = q.shape[0]
    tq, th, tk, nsub = _attn_blocks(s_len, MLA_TK)
    chains = _chains(tq, th)
    nh = tq // th
    nq = s_len // tq

    def body(q_ref, k_ref, v_ref, *rest):
        o_ref, lse_ref = rest[ride.n:ride.n + 2]
        ride.bind(rest[:ride.n], rest[ride.n + 2:2 * ride.n + 2], rest[2 * ride.n + 2:])
        pl.when((pl.program_id(0) == 0) & (pl.program_id(1) == 0))(ride.issue)
        i = pl.program_id(1)
        lane = lax.broadcasted_iota(jnp.int32, (th, LANES), 1)
        hsl = [slice(hh * HEAD_PAD, (hh + 1) * HEAD_PAD) for hh in range(2)]

        def step(kb, carry, sub):
            rows = pl.ds(pl.multiple_of(kb * tk, tk), tk)
            masks = ["all" if sub is None else _diag_mask(th, tk, r0, sub, strict=False) for _, r0 in chains]
            live = [n for n, m in enumerate(masks) if not (isinstance(m, str) and m == "none")]
            s = {n: _dot_nt(q_ref[chains[n][1]:chains[n][1] + th, hsl[chains[n][0]]], k_ref[rows, hsl[chains[n][0]]])
                 for n in live}
            new = list(carry)
            pb, alpha = {}, {}
            for n in live:
                m = carry[n][0]
                sn = s[n]
                if not isinstance(masks[n], str):
                    sn = jnp.where(masks[n], sn, NEG_BIG)
                m_new = jnp.maximum(m, jnp.max(sn, axis=-1, keepdims=True))
                alpha[n] = jnp.exp2(m - m_new)
                pb[n] = jnp.exp2(sn - m_new).astype(BF16)
                new[n] = (m_new, None)
            pv = {n: _dot(pb[n], v_ref[rows, hsl[chains[n][0]]]) for n in live}
            for n in live:
                new[n] = (new[n][0], alpha[n] * carry[n][1] + pv[n])
            return tuple(new)

        init = (jnp.full((th, 1), NEG_BIG, F32), jnp.zeros((th, LANES), F32))
        carry = lax.fori_loop(0, i * nsub, lambda kb, cy: step(kb, cy, None), (init,) * len(chains))
        for sub in range(nsub):
            carry = step(i * nsub + sub, carry, sub)
        for c in range(nh):
            (m0, a0), (m1, a1) = carry[c], carry[nh + c]
            l0, l1 = a0[:, MLA_NOPE:MLA_NOPE + 1], a1[:, MLA_NOPE:MLA_NOPE + 1]
            rs = slice(c * th, (c + 1) * th)
            o_ref[rs, :] = jnp.where(lane < 64, a0 / l0, pltpu.roll(a1 / l1, 64, 1)).astype(BF16)
            lse_ref[rs, :] = jnp.where(lane < 64, m0 + jnp.log2(l0), m1 + jnp.log2(l1))
        pl.when((pl.program_id(0) == 3) & (i == nq - 1))(ride.finish)

    outs = pl.pallas_call(
        body, name="mla_fwd", grid=(4, nq),
        in_specs=[pl.BlockSpec((tq, 2 * HEAD_PAD), lambda p, i: (i, p)),
                  pl.BlockSpec((s_len, 2 * HEAD_PAD), lambda p, i: (0, p)),
                  pl.BlockSpec((s_len, 2 * HEAD_PAD), lambda p, i: (0, p))] + ride.specs,
        out_specs=[pl.BlockSpec((tq, LANES), lambda p, i: (i, p)),
                   pl.BlockSpec((None, tq, LANES), lambda p, i: (p, i, 0))] + ride.specs,
        out_shape=[jax.ShapeDtypeStruct((s_len, 512), BF16),
                   jax.ShapeDtypeStruct((4, s_len, LANES), F32)] + ride.out_shape,
        scratch_shapes=ride.scratch,
        compiler_params=_cparams(40, dimension_semantics=("arbitrary", "arbitrary")),
    )(q, k, v, *ride.srcs)
    return outs[0], outs[1], outs[2:]


def _mla_bwd(q, k, v, o, do, lse, ride):
    s_len = q.shape[0]
    tq, th, tk, nsub = _attn_blocks(s_len, MLA_TK)
    chains = _chains(tq, th)
    nq = s_len // tq

    def body(q_ref, k_ref, v_ref, o_ref, do_ref, lse_ref, *rest):
        dq_ref, dk_ref, dv_ref = rest[ride.n:ride.n + 3]
        ride.bind(rest[:ride.n], rest[ride.n + 3:2 * ride.n + 3], rest[2 * ride.n + 3:])
        pl.when((pl.program_id(0) == 0) & (pl.program_id(1) == 0))(ride.issue)
        i = pl.program_id(1)
        lane = lax.broadcasted_iota(jnp.int32, (th, LANES), 1)

        @pl.when(i == 0)
        def _():
            dk_ref[...] = jnp.zeros_like(dk_ref)
            dv_ref[...] = jnp.zeros_like(dv_ref)

        hsl = [slice(hh * HEAD_PAD, (hh + 1) * HEAD_PAD) for hh in range(2)]
        qs, dos, deltas, lses = [], [], [], []
        for hh, r0 in chains:
            rs = slice(r0, r0 + th)
            qs.append(q_ref[rs, hsl[hh]])
            doh = jnp.where((lane // 64) == hh, do_ref[rs, :], jnp.zeros((), BF16))
            dos.append(doh)
            deltas.append(jnp.sum(doh.astype(F32) * o_ref[rs, :].astype(F32), axis=-1, keepdims=True))
            lses.append(lse_ref[rs, 64 * hh:64 * hh + 1])

        def step(kb, dqs, sub):
            rows = pl.ds(pl.multiple_of(kb * tk, tk), tk)
            vblk = v_ref[rows, :]
            new, p_all, do_all = [], [], []
            ds_h, q_h = [[], []], [[], []]
            for c, (hh, r0) in enumerate(chains):
                mask = "all" if sub is None else _diag_mask(th, tk, r0, sub, strict=False)
                if isinstance(mask, str) and mask == "none":
                    new.append(dqs[c])
                    continue
                kblk = k_ref[rows, hsl[hh]]
                s = _dot_nt(qs[c], kblk)
                if not isinstance(mask, str):
                    s = jnp.where(mask, s, NEG_BIG)
                p = jnp.exp2(s - lses[c])
                dp = _dot_nt(dos[c], vblk)
                ds = (p * (dp - deltas[c]) * MLA_SCALE).astype(BF16)
                p_all.append(p.astype(BF16))
                do_all.append(dos[c])
                ds_h[hh].append(ds)
                q_h[hh].append(qs[c])
                new.append(dqs[c] + _dot(ds, kblk))
            dv_ref[rows, :] += _dot_tn(jnp.concatenate(p_all, axis=0), jnp.concatenate(do_all, axis=0))
            for hh in range(2):
                dk_ref[rows, hsl[hh]] += _dot_tn(jnp.concatenate(ds_h[hh], axis=0),
                                                 jnp.concatenate(q_h[hh], axis=0))
            return tuple(new)

        zero = jnp.zeros((th, LANES), F32)
        dqs = lax.fori_loop(0, i * nsub, lambda kb, cy: step(kb, cy, None), (zero,) * len(chains))
        for sub in range(nsub):
            dqs = step(i * nsub + sub, dqs, sub)
        for c, (hh, r0) in enumerate(chains):
            dq_ref[r0:r0 + th, hsl[hh]] = dqs[c]
        pl.when((pl.program_id(0) == 3) & (i == nq - 1))(ride.finish)

    outs = pl.pallas_call(
        body, name="mla_bwd", grid=(4, nq),
        in_specs=[pl.BlockSpec((tq, 2 * HEAD_PAD), lambda p, i: (i, p)),
                  pl.BlockSpec((s_len, 2 * HEAD_PAD), lambda p, i: (0, p)),
                  pl.BlockSpec((s_len, LANES), lambda p, i: (0, p)),
                  pl.BlockSpec((tq, LANES), lambda p, i: (i, p)),
                  pl.BlockSpec((tq, LANES), lambda p, i: (i, p)),
                  pl.BlockSpec((None, tq, LANES), lambda p, i: (p, i, 0))] + ride.specs,
        out_specs=[pl.BlockSpec((tq, 2 * HEAD_PAD), lambda p, i: (i, p)),
                   pl.BlockSpec((s_len, 2 * HEAD_PAD), lambda p, i: (0, p)),
                   pl.BlockSpec((s_len, LANES), lambda p, i: (0, p))] + ride.specs,
        out_shape=[jax.ShapeDtypeStruct((s_len, 1024), F32), jax.ShapeDtypeStruct((s_len, 1024), F32),
                   jax.ShapeDtypeStruct((s_len, 512), F32)] + ride.out_shape,
        scratch_shapes=ride.scratch,
        compiler_params=_cparams(56, dimension_semantics=("arbitrary", "arbitrary")),
    )(q, k, v, o, do, lse, *ride.srcs)
    return outs[0], outs[1], outs[2], outs[3:]


def _log_sigmoids(z2):
    sp = jnp.log2(1.0 + jnp.exp2(-jnp.abs(z2)))
    lb = jnp.minimum(z2, 0.0) - sp
    return lb, lb - z2


def _split_dot(x, w, parts, nt=False):
    dot = _dot_nt if nt else _dot
    out = None
    for _ in range(parts):
        xb = x.astype(BF16)
        t = dot(xb, w)
        out = t if out is None else out + t
        x = x - xb.astype(F32)
    return out


def _sb_fwd(sb):
    s_len = sb.shape[0]
    tq, th, tk, nsub = _attn_blocks(s_len, tq=SB_TQ, th=SB_TH)
    chains = _chains(tq, th)
    nh = tq // th
    assert s_len // tk <= 64

    def body(q_ref, k_ref, v_ref, o_ref, r_ref):
        i = pl.program_id(1)
        lane = lax.broadcasted_iota(jnp.int32, (th, LANES), 1)
        upper = (lax.broadcasted_iota(jnp.int32, (tk, tk), 0)
                 > lax.broadcasted_iota(jnp.int32, (tk, tk), 1)).astype(BF16)
        qs = [jnp.where((lane // 64) == hh, q_ref[r0:r0 + th, :], jnp.zeros((), BF16)) for hh, r0 in chains]

        def step(kb, carry, sub):
            rows = pl.ds(pl.multiple_of(kb * tk, tk), tk)
            kblk, vblk = k_ref[rows, :], v_ref[rows, :]
            masks = ["all" if sub is None else _diag_mask(th, tk, r0, sub, strict=True) for _, r0 in chains]
            live = [n for n, m in enumerate(masks) if not (isinstance(m, str) and m == "none")]
            masked = {n: not isinstance(masks[n], str) for n in live}
            z = {n: _dot_nt(qs[n], kblk) for n in live}
            lb, lom = {}, {}
            for n in live:
                lb[n], lom[n] = _log_sigmoids(z[n])
                if masked[n]:
                    lom[n] = jnp.where(masks[n], lom[n], 0.0)
            suf = {n: _split_dot(lom[n], upper, 2) for n in live}
            a = {}
            for n in live:
                a[n] = jnp.exp2(lb[n] + suf[n] + carry[n][0])
                if masked[n]:
                    a[n] = jnp.where(masks[n], a[n], 0.0)
            pv = {n: _dot(a[n].astype(BF16), vblk) for n in live}
            new = list(carry)
            for n in live:
                c, acc, r = carry[n]
                rs = suf[n][:, 0:1] + lom[n][:, 0:1]
                new[n] = (c + rs, acc + pv[n], jnp.where(lane == 64 * chains[n][0] + kb, rs, r))
            return tuple(new)

        init = (jnp.zeros((th, 1), F32), jnp.zeros((th, LANES), F32), jnp.zeros((th, LANES), F32))
        carry = (init,) * len(chains)
        for sub in reversed(range(nsub)):
            carry = step(i * nsub + sub, carry, sub)

        def spent(cy):
            top = functools.reduce(jnp.maximum, [jnp.max(c) for c, _, _ in cy])
            return (top < SB_CUT).astype(jnp.int32)

        def walk(state):
            t, _, cy = state
            cy = step(i * nsub - 1 - t, cy, None)
            return t + 1, spent(cy), cy

        _, _, carry = lax.while_loop(lambda st: (st[0] < i * nsub) & (st[1] == 0), walk,
                                     (jnp.int32(0), spent(carry), carry))
        for n in range(nh):
            rs = slice(n * th, (n + 1) * th)
            o_ref[rs, :] = jnp.where(lane < 64, carry[n][1], carry[nh + n][1]).astype(BF16)
            r_ref[rs, :] = jnp.where(lane < 64, carry[n][2], carry[nh + n][2])

    return pl.pallas_call(
        body, name="sb_fwd", grid=(4, s_len // tq),
        in_specs=[pl.BlockSpec((tq, LANES), lambda p, i: (i, p)),
                  pl.BlockSpec((s_len, LANES), lambda p, i: (0, 4 + p)),
                  pl.BlockSpec((s_len, LANES), lambda p, i: (0, 8 + p))],
        out_specs=[pl.BlockSpec((tq, LANES), lambda p, i: (i, p)),
                   pl.BlockSpec((None, tq, LANES), lambda p, i: (p, i, 0))],
        out_shape=[jax.ShapeDtypeStruct((s_len, 512), BF16), jax.ShapeDtypeStruct((4, s_len, LANES), F32)],
        compiler_params=_cparams(40, dimension_semantics=("parallel", "arbitrary")),
    )(sb, sb, sb)


def _sb_bwd(sb, do, r):
    s_len = sb.shape[0]
    tq, th, tk, nsub = _attn_blocks(s_len)
    chains = _chains(tq, th)
    nh = tq // th

    def body(q_ref, k_ref, v_ref, do_ref, r_ref, dq_ref, dk_ref, dv_ref):
        i = pl.program_id(1)
        lane = lax.broadcasted_iota(jnp.int32, (th, LANES), 1)
        upper = (lax.broadcasted_iota(jnp.int32, (tk, tk), 0)
                 > lax.broadcasted_iota(jnp.int32, (tk, tk), 1)).astype(BF16)
        tri = (lax.broadcasted_iota(jnp.int32, (LANES, LANES), 0)
               > lax.broadcasted_iota(jnp.int32, (LANES, LANES), 1)).astype(BF16)

        @pl.when(i == 0)
        def _():
            dk_ref[...] = jnp.zeros_like(dk_ref)
            dv_ref[...] = jnp.zeros_like(dv_ref)

        qs, dos, rights = [], [], []
        for hh, r0 in chains:
            rs = slice(r0, r0 + th)
            hm = (lane // 64) == hh
            qs.append(jnp.where(hm, q_ref[rs, :], jnp.zeros((), BF16)))
            dos.append(jnp.where(hm, do_ref[rs, :], jnp.zeros((), BF16)))
            rights.append(_split_dot(jnp.where(hm, r_ref[rs, :], 0.0), tri, 3))

        def step(kb, carry, sub):
            rows = pl.ds(pl.multiple_of(kb * tk, tk), tk)
            kblk, vblk = k_ref[rows, :], v_ref[rows, :]
            new, a_all, do_all, dz_all, q_all = [], [], [], [], []
            for n, ((hh, r0), (pre, dq)) in enumerate(zip(chains, carry)):
                mask = "all" if sub is None else _diag_mask(th, tk, r0, sub, strict=True)
                if isinstance(mask, str) and mask == "none":
                    new.append((pre, dq))
                    continue
                c = jnp.sum(jnp.where(lane == 64 * hh + kb, rights[n], 0.0), axis=-1, keepdims=True)
                z = _dot_nt(qs[n], kblk)
                lb, lom = _log_sigmoids(z)
                if not isinstance(mask, str):
                    lom = jnp.where(mask, lom, 0.0)
                suf = _split_dot(lom, upper, 2)
                a = jnp.exp2(lb + suf + c)
                if not isinstance(mask, str):
                    a = jnp.where(mask, a, 0.0)
                g = a * _dot_nt(dos[n], vblk)
                left = _split_dot(g, upper, 1, nt=True) + pre
                sig = jnp.exp2(lb)
                dz = g * (1.0 - sig) - sig * left
                if not isinstance(mask, str):
                    dz = jnp.where(mask, dz, 0.0)
                dzb = dz.astype(BF16)
                a_all.append(a.astype(BF16))
                do_all.append(dos[n])
                dz_all.append(dzb)
                q_all.append(qs[n])
                new.append((left[:, tk - 1:tk] + g[:, tk - 1:tk], dq + _dot(dzb, kblk)))
            dv_ref[rows, :] += _dot_tn(jnp.concatenate(a_all, axis=0), jnp.concatenate(do_all, axis=0))
            dk_ref[rows, :] += _dot_tn(jnp.concatenate(dz_all, axis=0), jnp.concatenate(q_all, axis=0))
            return tuple(new)

        lane1 = lax.broadcasted_iota(jnp.int32, (1, LANES), 1)
        first = i * nsub
        for n, (hh, _) in enumerate(chains):
            top = jnp.max(rights[n], axis=0, keepdims=True)
            kb_of = lane1 - 64 * hh
            live = (kb_of >= 0) & (kb_of < i * nsub) & (top >= SB_CUT)
            first = jnp.minimum(first, jnp.min(jnp.where(live, kb_of, i * nsub)))

        init = (jnp.zeros((th, 1), F32), jnp.zeros((th, LANES), F32))
        carry = lax.fori_loop(first, i * nsub, lambda kb, cy: step(kb, cy, None), (init,) * len(chains))
        for sub in range(nsub):
            carry = step(i * nsub + sub, carry, sub)
        for n in range(nh):
            dq_ref[n * th:(n + 1) * th, :] = jnp.where(lane < 64, carry[n][1], carry[nh + n][1]) * SB_SCALE

    return pl.pallas_call(
        body, name="sb_bwd", grid=(4, s_len // tq),
        in_specs=[pl.BlockSpec((tq, LANES), lambda p, i: (i, p)),
                  pl.BlockSpec((s_len, LANES), lambda p, i: (0, 4 + p)),
                  pl.BlockSpec((s_len, LANES), lambda p, i: (0, 8 + p)),
                  pl.BlockSpec((tq, LANES), lambda p, i: (i, p)),
                  pl.BlockSpec((None, tq, LANES), lambda p, i: (p, i, 0))],
        out_specs=[pl.BlockSpec((tq, LANES), lambda p, i: (i, p)),
                   pl.BlockSpec((s_len, LANES), lambda p, i: (0, p)),
                   pl.BlockSpec((s_len, LANES), lambda p, i: (0, p))],
        out_shape=[jax.ShapeDtypeStruct((s_len, 512), F32)] * 3,
        compiler_params=_cparams(48, dimension_semantics=("arbitrary", "arbitrary")),
    )(sb, sb, sb, do, r)


def _merge_fwd(x, oa, ob, gates, bg, wa, wb, wo):
    s_len = x.shape[0]
    tm = _row_block(s_len)

    def body(x_ref, oa_ref, ob_ref, g_ref, bg_ref, wa_ref, wb_ref, wo_ref, y_ref):
        pa = _dot(oa_ref[...], wa_ref[...])
        pb = _dot(ob_ref[...], wb_ref[...])
        merged = (_sigmoid(g_ref[:, 0:D_MODEL] + bg_ref[0:1, :]) * pa
                  + _sigmoid(g_ref[:, D_MODEL:2 * D_MODEL] + bg_ref[1:2, :]) * pb)
        y_ref[...] = x_ref[...] + _dot(merged.astype(BF16), wo_ref[...])

    full = lambda shape: pl.BlockSpec(shape, lambda i: (0, 0))
    rowb = lambda n: pl.BlockSpec((tm, n), lambda i: (i, 0))
    return pl.pallas_call(
        body, name="merge_fwd", grid=(s_len // tm,),
        in_specs=[rowb(1024), rowb(512), rowb(512), rowb(2048), full((2, 1024)), full((512, 1024)),
                  full((512, 1024)), full((1024, 1024))],
        out_specs=rowb(1024),
        out_shape=jax.ShapeDtypeStruct((s_len, D_MODEL), F32),
        compiler_params=_cparams(48, dimension_semantics=("parallel",)),
    )(x, oa, ob, gates, bg, wa, wb, wo)


def _merge_bwd(dx1, oa, ob, gates, bg, wa, wb, wo):
    s_len = dx1.shape[0]
    tm = _row_block(s_len)

    def body(dx_ref, oa_ref, ob_ref, g_ref, bg_ref, wa_ref, wb_ref, wo_ref,
             doa_ref, dob_ref, dgate_ref, dpa_ref, dpb_ref, merged_ref, dxb_ref, dbg_ref):
        first = pl.program_id(0) == 0
        dxb = dx_ref[...].astype(BF16)
        dxb_ref[...] = dxb
        pa = _dot(oa_ref[...], wa_ref[...])
        pb = _dot(ob_ref[...], wb_ref[...])
        sa = _sigmoid(g_ref[:, 0:D_MODEL] + bg_ref[0:1, :])
        sbg = _sigmoid(g_ref[:, D_MODEL:2 * D_MODEL] + bg_ref[1:2, :])
        merged_ref[...] = (sa * pa + sbg * pb).astype(BF16)
        dm = _dot_nt(dxb, wo_ref[...])
        dpa = (dm * sa).astype(BF16)
        dpb = (dm * sbg).astype(BF16)
        dpa_ref[...] = dpa
        dpb_ref[...] = dpb
        dga = dm * pa * sa * (1.0 - sa)
        dgb = dm * pb * sbg * (1.0 - sbg)
        dgate_ref[:, 0:D_MODEL] = dga.astype(BF16)
        dgate_ref[:, D_MODEL:2 * D_MODEL] = dgb.astype(BF16)
        _acc_rows(dbg_ref.at[0:1, :], dga, first)
        _acc_rows(dbg_ref.at[1:2, :], dgb, first)
        doa_ref[...] = _dot_nt(dpa, wa_ref[...]).astype(BF16)
        dob_ref[...] = _dot_nt(dpb, wb_ref[...]).astype(BF16)

    full = lambda shape: pl.BlockSpec(shape, lambda i: (0, 0))
    rowb = lambda n: pl.BlockSpec((tm, n), lambda i: (i, 0))
    sds = lambda n, dt: jax.ShapeDtypeStruct((s_len, n), dt)
    return pl.pallas_call(
        body, name="merge_bwd", grid=(s_len // tm,),
        in_specs=[rowb(1024), rowb(512), rowb(512), rowb(2048), full((2, 1024)), full((512, 1024)),
                  full((512, 1024)), full((1024, 1024))],
        out_specs=[rowb(512), rowb(512), rowb(2048), rowb(1024), rowb(1024), rowb(1024), rowb(1024),
                   full((2, 1024))],
        out_shape=[sds(512, BF16), sds(512, BF16), sds(2048, BF16), sds(1024, BF16), sds(1024, BF16),
                   sds(1024, BF16), sds(1024, BF16), jax.ShapeDtypeStruct((2, 1024), F32)],
        compiler_params=_cparams(48, dimension_semantics=("arbitrary",)),
    )(dx1, oa, ob, gates, bg, wa, wb, wo)


def _mem_kv(mem, g, w):
    m_len = mem.shape[0]

    def body(mem_ref, g_ref, w_ref, mn_ref, kv_ref):
        mn, _, _ = _rms(mem_ref[...], g_ref[...])
        mnb = mn.astype(BF16)
        mn_ref[...] = mnb
        kv_ref[...] = _dot(mnb, w_ref[...]).astype(BF16)

    return pl.pallas_call(
        body, name="mem_kv",
        out_shape=[jax.ShapeDtypeStruct((m_len, D_MODEL), BF16), jax.ShapeDtypeStruct((m_len, 1024), BF16)],
    )(mem, g, w)


def _mem_bwd(mem, g, w, mn, dkv):
    def body(mem_ref, g_ref, w_ref, mn_ref, dkv_ref, dw_ref, dg_ref):
        dkvb = dkv_ref[...].astype(BF16)
        dw_ref[...] = _dot_tn(mn_ref[...], dkvb)
        dmn = _dot_nt(dkvb, w_ref[...])
        _, xh, _ = _rms(mem_ref[...], g_ref[...])
        dg_ref[...] = jnp.sum(dmn * xh, axis=0, keepdims=True)

    return pl.pallas_call(
        body, name="mem_bwd",
        out_shape=[jax.ShapeDtypeStruct((D_MODEL, 1024), F32), jax.ShapeDtypeStruct((1, D_MODEL), F32)],
    )(mem, g, w, mn, dkv)


def _xattn_heads(xqb, kv_ref, m_len):
    ps = []
    for h in range(X_HEADS):
        hs = slice(h * X_HEAD_DIM, (h + 1) * X_HEAD_DIM)
        s = _dot_nt(xqb[:, hs], kv_ref[:, hs]) * X_SCALE
        e = jnp.exp(s - jnp.max(s, axis=-1, keepdims=True))
        ps.append(e / jnp.sum(e, axis=-1, keepdims=True))
    return ps


def _xattn_fwd(x1, g, wxq, kv, wxo):
    s_len, m_len = x1.shape[0], kv.shape[0]
    tm = _row_block(s_len)

    def body(x_ref, g_ref, wq_ref, kv_ref, wo_ref, y_ref):
        hx, _, _ = _rms(x_ref[...], g_ref[...])
        xqb = _dot(hx.astype(BF16), wq_ref[...]).astype(BF16)
        ps = _xattn_heads(xqb, kv_ref, m_len)
        xo = jnp.concatenate(
            [_dot(ps[h].astype(BF16), kv_ref[:, 512 + h * X_HEAD_DIM:512 + (h + 1) * X_HEAD_DIM])
             for h in range(X_HEADS)], axis=-1)
        y_ref[...] = x_ref[...] + _dot(xo.astype(BF16), wo_ref[...])

    full = lambda shape: pl.BlockSpec(shape, lambda i: (0, 0))
    rowb = lambda n: pl.BlockSpec((tm, n), lambda i: (i, 0))
    return pl.pallas_call(
        body, name="xattn_fwd", grid=(s_len // tm,),
        in_specs=[rowb(1024), full((1, 1024)), full((1024, 512)), full((m_len, 1024)), full((512, 1024))],
        out_specs=rowb(1024),
        out_shape=jax.ShapeDtypeStruct((s_len, D_MODEL), F32),
        compiler_params=_cparams(48, dimension_semantics=("parallel",)),
    )(x1, g, wxq, kv, wxo)


def _xattn_bwd(x1, dx2, g, wxq, kv, wxo):
    s_len, m_len = x1.shape[0], kv.shape[0]
    tm = _row_block(s_len)

    def body(x_ref, dy_ref, g_ref, wq_ref, kv_ref, wo_ref, dx_ref, dwq_ref, dwo_ref, dkv_ref, dg_ref):
        first = pl.program_id(0) == 0
        gv = g_ref[...]
        hx, xh, r = _rms(x_ref[...], gv)
        hxb = hx.astype(BF16)
        xqb = _dot(hxb, wq_ref[...]).astype(BF16)
        ps = _xattn_heads(xqb, kv_ref, m_len)
        dy = dy_ref[...]
        dyb = dy.astype(BF16)
        dxo = _dot_nt(dyb, wo_ref[...])
        xos, dqs, dks, dvs = [], [], [], []
        for h in range(X_HEADS):
            hs = slice(h * X_HEAD_DIM, (h + 1) * X_HEAD_DIM)
            vs = slice(512 + h * X_HEAD_DIM, 512 + (h + 1) * X_HEAD_DIM)
            p = ps[h]
            pb = p.astype(BF16)
            dxoh = dxo[:, hs].astype(BF16)
            xos.append(_dot(pb, kv_ref[:, vs]))
            dp = _dot_nt(dxoh, kv_ref[:, vs])
            ds = (p * (dp - jnp.sum(dp * p, axis=-1, keepdims=True)) * X_SCALE).astype(BF16)
            dvs.append(_dot_tn(pb, dxoh))
            dks.append(_dot_tn(ds, xqb[:, hs]))
            dqs.append(_dot(ds, kv_ref[:, hs]))
        xob = jnp.concatenate(xos, axis=-1).astype(BF16)
        dxqb = jnp.concatenate(dqs, axis=-1).astype(BF16)
        _acc(dwo_ref, _dot_tn(xob, dyb), first)
        _acc(dwq_ref, _dot_tn(hxb, dxqb), first)
        _acc(dkv_ref, jnp.concatenate(dks + dvs, axis=-1), first)
        dhx = _dot_nt(dxqb, wq_ref[...])
        dx, dgr = _rms_bwd(dhx, xh, r, gv)
        dx_ref[...] = dy + dx
        _acc_rows(dg_ref, dgr, first)

    full = lambda shape: pl.BlockSpec(shape, lambda i: (0, 0))
    rowb = lambda n: pl.BlockSpec((tm, n), lambda i: (i, 0))
    return pl.pallas_call(
        body, name="xattn_bwd", grid=(s_len // tm,),
        in_specs=[rowb(1024), rowb(1024), full((1, 1024)), full((1024, 512)), full((m_len, 1024)),
                  full((512, 1024))],
        out_specs=[rowb(1024), full((1024, 512)), full((512, 1024)), full((m_len, 1024)), full((1, 1024))],
        out_shape=[jax.ShapeDtypeStruct((s_len, D_MODEL), F32), jax.ShapeDtypeStruct((1024, 512), F32),
                   jax.ShapeDtypeStruct((512, 1024), F32), jax.ShapeDtypeStruct((m_len, 1024), F32),
                   jax.ShapeDtypeStruct((1, D_MODEL), F32)],
        compiler_params=_cparams(48, dimension_semantics=("arbitrary",)),
    )(x1, dx2, g, wxq, kv, wxo)


FF_TILE = 1408


def _ffn_fwd(x2, g, wg, wu, wd):
    s_len = x2.shape[0]
    tm, tf = _row_block(s_len), FF_TILE

    def body(x_ref, g_ref, wg_ref, wu_ref, wd_ref, y_ref, h_ref):
        j = pl.program_id(1)

        @pl.when(j == 0)
        def _():
            hf, _, _ = _rms(x_ref[...], g_ref[...])
            h_ref[...] = hf.astype(BF16)
            y_ref[...] = x_ref[...]

        hb = h_ref[...]
        gt = _dot(hb, wg_ref[...])
        up = _dot(hb, wu_ref[...])
        act = gt * _sigmoid(gt) * up
        y_ref[...] += _dot(act.astype(BF16), wd_ref[...])

    rowb = pl.BlockSpec((tm, D_MODEL), lambda i, j: (i, 0))
    return pl.pallas_call(
        body, name="ffn_fwd", grid=(s_len // tm, D_FF // tf),
        in_specs=[rowb, pl.BlockSpec((1, D_MODEL), lambda i, j: (0, 0)),
                  pl.BlockSpec((D_MODEL, tf), lambda i, j: (0, j)),
                  pl.BlockSpec((D_MODEL, tf), lambda i, j: (0, j)),
                  pl.BlockSpec((tf, D_MODEL), lambda i, j: (j, 0))],
        out_specs=[rowb, rowb],
        out_shape=[jax.ShapeDtypeStruct((s_len, D_MODEL), F32), jax.ShapeDtypeStruct((s_len, D_MODEL), BF16)],
        compiler_params=_cparams(48, dimension_semantics=("parallel", "arbitrary")),
    )(x2, g, wg, wu, wd)


def _ffn_bwd(x2, hf, dx3, dx3b, g, wg, wu, wd):
    s_len = x2.shape[0]
    tm, tf = _row_block(s_len), FF_TILE
    nf = D_FF // tf

    def act_body(h_ref, dy_ref, wg_ref, wu_ref, wd_ref, dgt_ref, dup_ref, act_ref):
        hb = h_ref[...]
        gt = _dot(hb, wg_ref[...])
        up = _dot(hb, wu_ref[...])
        sg = _sigmoid(gt)
        silu = gt * sg
        dact = _dot_nt(dy_ref[...], wd_ref[...])
        dgt_ref[...] = (dact * up * (sg * (1.0 + gt * (1.0 - sg)))).astype(BF16)
        dup_ref[...] = (dact * silu).astype(BF16)
        act_ref[...] = (silu * up).astype(BF16)

    rowb = pl.BlockSpec((tm, D_MODEL), lambda i, j: (i, 0))
    ffb = pl.BlockSpec((tm, tf), lambda i, j: (i, j))
    dgt, dup, act = pl.pallas_call(
        act_body, name="ffn_bwd_act", grid=(s_len // tm, nf),
        in_specs=[rowb, rowb,
                  pl.BlockSpec((D_MODEL, tf), lambda i, j: (0, j)),
                  pl.BlockSpec((D_MODEL, tf), lambda i, j: (0, j)),
                  pl.BlockSpec((tf, D_MODEL), lambda i, j: (j, 0))],
        out_specs=[ffb, ffb, ffb],
        out_shape=[jax.ShapeDtypeStruct((s_len, D_FF), BF16)] * 3,
        compiler_params=_cparams(56, dimension_semantics=("parallel", "arbitrary")),
    )(hf, dx3b, wg, wu, wd)
    tm = min(s_len, 256)

    def in_body(x_ref, dy_ref, g_ref, wg_ref, wu_ref, dgt_ref, dup_ref, dx_ref, dg_ref):
        dh = _dot_nt(dgt_ref[...], wg_ref[...]) + _dot_nt(dup_ref[...], wu_ref[...])
        gv = g_ref[...]
        _, xh, r = _rms(x_ref[...], gv)
        dx, dgr = _rms_bwd(dh, xh, r, gv)
        dx_ref[...] = dy_ref[...] + dx
        _acc_rows(dg_ref, dgr, pl.program_id(0) == 0)

    row1 = lambda n: pl.BlockSpec((tm, n), lambda i: (i, 0))
    full = lambda shape: pl.BlockSpec(shape, lambda i: (0, 0))
    dx2, dg = pl.pallas_call(
        in_body, name="ffn_bwd_in", grid=(s_len // tm,),
        in_specs=[row1(D_MODEL), row1(D_MODEL), full((1, D_MODEL)), full((D_MODEL, D_FF)), full((D_MODEL, D_FF)),
                  row1(D_FF), row1(D_FF)],
        out_specs=[row1(D_MODEL), full((1, D_MODEL))],
        out_shape=[jax.ShapeDtypeStruct((s_len, D_MODEL), F32), jax.ShapeDtypeStruct((1, D_MODEL), F32)],
        compiler_params=_cparams(48, dimension_semantics=("arbitrary",)),
    )(x2, dx3, g, wg, wu, dgt, dup)
    return dx2, dgt, dup, act, dg


def _loss_head(x3, g, target):
    s_len = x3.shape[0]
    tm = _row_block(s_len)

    def body(x_ref, g_ref, t_ref, sse_ref, dx_ref, dxb_ref, dg_ref):
        first = pl.program_id(0) == 0
        gv = g_ref[...]
        y, xh, r = _rms(x_ref[...], gv)
        err = y - t_ref[...]
        _acc(sse_ref, jnp.broadcast_to(jnp.sum(err * err), (8, LANES)), first)
        dx, dgr = _rms_bwd(err * (1.0 / D_MODEL), xh, r, gv)
        dx_ref[...] = dx
        dxb_ref[...] = dx.astype(BF16)
        _acc_rows(dg_ref, dgr, first)

    rowb = pl.BlockSpec((tm, D_MODEL), lambda i: (i, 0))
    return pl.pallas_call(
        body, name="loss_head", grid=(s_len // tm,),
        in_specs=[rowb, pl.BlockSpec((1, D_MODEL), lambda i: (0, 0)), rowb],
        out_specs=[pl.BlockSpec((8, LANES), lambda i: (0, 0)), rowb, rowb,
                   pl.BlockSpec((1, D_MODEL), lambda i: (0, 0))],
        out_shape=[jax.ShapeDtypeStruct((8, LANES), F32), jax.ShapeDtypeStruct((s_len, D_MODEL), F32),
                   jax.ShapeDtypeStruct((s_len, D_MODEL), BF16), jax.ShapeDtypeStruct((1, D_MODEL), F32)],
        compiler_params=_cparams(dimension_semantics=("arbitrary",)),
    )(x3, g, target)


def _mla_prep_bwd(lat, g_q, g_kv, w_uq, w_uk, w_uv, cosf, sinf, dq, dk, dv):
    s_len = lat.shape[0]
    tm = _row_block(s_len)

    def body(lat_ref, gq_ref, gkv_ref, wuq_ref, wuk_ref, wuv_ref, cos_ref, sin_ref, dq_ref, dk_ref, dv_ref,
             dlat_ref, dqb_ref, dkb_ref, dvb_ref, dgq_ref, dgkv_ref):
        first = pl.program_id(0) == 0
        lane = lax.broadcasted_iota(jnp.int32, (tm, LANES), 1)
        cosv, sinv = cos_ref[...], sin_ref[...]
        gq, gkv = gq_ref[...], gkv_ref[...]
        _, qxh, qr = _rms(lat_ref[:, 0:256], gq)
        _, kxh, kr_ = _rms(lat_ref[:, 256:384], gkv)
        dkr = jnp.zeros((tm, LANES), F32)
        for h in range(MLA_HEADS):
            sl = slice(h * HEAD_PAD, (h + 1) * HEAD_PAD)
            blk = dq_ref[:, sl]
            dqb_ref[:, sl] = (blk * cosv + _rope_rot_t(blk, lane) * sinv).astype(BF16)
            kblk = dk_ref[:, sl] * (1.0 / MLA_Q_FOLD)
            dkb_ref[:, sl] = kblk.astype(BF16)
            dkr = dkr + kblk
        dvb = dv_ref[...].astype(BF16)
        dvb_ref[...] = dvb
        dkr = jnp.where((lane >= 64) & (lane < 96), dkr, 0.0)
        dkr = dkr * cosv + _rope_rot_t(dkr, lane) * sinv
        dql = _dot_nt(dqb_ref[...], wuq_ref[...])
        dkvl = _dot_nt(dkb_ref[...], wuk_ref[...]) + _dot_nt(dvb, wuv_ref[...])
        dcq, dgqr = _rms_bwd(dql, qxh, qr, gq)
        dckv, dgkvr = _rms_bwd(dkvl, kxh, kr_, gkv)
        dlat_ref[:, 0:256] = dcq
        dlat_ref[:, 256:384] = dckv
        dlat_ref[:, K_R_OFF:K_R_OFF + LANES] = pltpu.roll(dkr, 64, 1)
        _acc_rows(dgq_ref, dgqr, first)
        _acc_rows(dgkv_ref, dgkvr, first)

    full = lambda shape: pl.BlockSpec(shape, lambda i: (0, 0))
    rowb = lambda n: pl.BlockSpec((tm, n), lambda i: (i, 0))
    sds = lambda n, dt: jax.ShapeDtypeStruct((s_len, n), dt)
    return pl.pallas_call(
        body, name="mla_prep_bwd", grid=(s_len // tm,),
        in_specs=[rowb(512), full((1, 256)), full((1, 128)), full((256, 1024)), full((128, 1024)),
                  full((128, 512)), rowb(128), rowb(128), rowb(1024), rowb(1024), rowb(512)],
        out_specs=[rowb(512), rowb(1024), rowb(1024), rowb(512), full((1, 256)), full((1, 128))],
        out_shape=[sds(512, F32), sds(1024, BF16), sds(1024, BF16), sds(512, BF16),
                   jax.ShapeDtypeStruct((1, 256), F32), jax.ShapeDtypeStruct((1, 128), F32)],
        compiler_params=_cparams(48, dimension_semantics=("arbitrary",)),
    )(lat, g_q, g_kv, w_uq, w_uk, w_uv, cosf, sinf, dq, dk, dv)


def _in_proj_bwd(x, g, w, dx1, dlat, dsbq, dsbk, dsbv, dgates, ride):
    s_len = x.shape[0]
    tm = min(s_len, 256)
    nb = s_len // tm

    def body(x_ref, g_ref, w_ref, dx1_ref, dlat_ref, dq_ref, dk_ref, dv_ref, dgate_ref, *rest):
        gx_ref, dg_ref = rest[ride.n:ride.n + 2]
        dproj = rest[-1]
        ride.bind(rest[:ride.n], rest[ride.n + 2:2 * ride.n + 2], rest[2 * ride.n + 2:-1])
        pl.when(pl.program_id(0) == 0)(ride.issue)
        dproj[:, 0:512] = dlat_ref[...].astype(BF16)
        dproj[:, 512:1024] = dq_ref[...].astype(BF16)
        dproj[:, 1024:1536] = (dk_ref[...] * LN2).astype(BF16)
        dproj[:, 1536:2048] = dv_ref[...].astype(BF16)
        dproj[:, 2048:4096] = dgate_ref[...]
        dh = _dot_nt(dproj[...], w_ref[...])
        gv = g_ref[...]
        _, xh, r = _rms(x_ref[...], gv)
        dx, dgr = _rms_bwd(dh, xh, r, gv)
        gx_ref[...] = dx1_ref[...] + dx
        _acc_rows(dg_ref, dgr, pl.program_id(0) == 0)
        pl.when(pl.program_id(0) == nb - 1)(ride.finish)

    rowb = lambda n: pl.BlockSpec((tm, n), lambda i: (i, 0))
    full = lambda shape: pl.BlockSpec(shape, lambda i: (0, 0))
    outs = pl.pallas_call(
        body, name="in_proj_bwd", grid=(nb,),
        in_specs=[rowb(D_MODEL), full((1, D_MODEL)), full((D_MODEL, D_IN_PAD)), rowb(D_MODEL),
                  rowb(512), rowb(512), rowb(512), rowb(512), rowb(2 * D_MODEL)] + ride.specs,
        out_specs=[rowb(D_MODEL), full((1, D_MODEL))] + ride.specs,
        out_shape=[jax.ShapeDtypeStruct((s_len, D_MODEL), F32), jax.ShapeDtypeStruct((1, D_MODEL), F32)]
        + ride.out_shape,
        scratch_shapes=ride.scratch + [pltpu.VMEM((tm, D_IN_PAD), BF16)],
        compiler_params=_cparams(48, dimension_semantics=("arbitrary",)),
    )(x, g, w, dx1, dlat, dsbq, dsbk, dsbv, dgates, *ride.srcs)
    return outs[0], outs[1], outs[2:]


def _adamw(landed, w, m, v, name):
    r, c = w.shape
    lanes = _round_up(c, LANES)
    tb = r
    for cand in range(r, 0, -1):
        if r % cand == 0 and (cand % 8 == 0 or cand == r) and N_DEV * cand * lanes * 4 <= ADAM_BLOCK_BYTES:
            tb = cand
            break
    c1 = 1.0 - ADAM_B1 ** ADAM_STEP
    c2 = 1.0 - ADAM_B2 ** ADAM_STEP

    def body(l_ref, w_ref, m_ref, v_ref, g_ref, d_ref, nm_ref, nv_ref):
        g = l_ref[0]
        for k in range(1, N_DEV):
            g = g + l_ref[k]
        nm = ADAM_B1 * m_ref[...] + (1.0 - ADAM_B1) * g
        nv = ADAM_B2 * v_ref[...] + (1.0 - ADAM_B2) * (g * g)
        g_ref[...] = g
        nm_ref[...] = nm
        nv_ref[...] = nv
        d_ref[...] = -ADAM_LR * ((nm / c1) / (jnp.sqrt(nv / c2) + ADAM_EPS) + ADAM_WD * w_ref[...])

    blk = pl.BlockSpec((tb, c), lambda i: (i, 0))
    return pl.pallas_call(
        body, name=name, grid=(r // tb,),
        in_specs=[pl.BlockSpec((N_DEV, tb, c), lambda i: (0, i, 0)), blk, blk, blk],
        out_specs=[blk, blk, blk, blk],
        out_shape=[jax.ShapeDtypeStruct((r, c), F32)] * 4,
        compiler_params=_cparams(dimension_semantics=("parallel",)),
    )(landed, w, m, v)


def _shard_shape(shape, axis):
    return tuple(d // N_DEV if a == axis else d for a, d in enumerate(shape))


def _split_pieces(full, axis):
    r, c = full.shape
    if axis == 0:
        return full.reshape(N_DEV, r // N_DEV, c)
    return full.reshape(r, N_DEV, c // N_DEV).transpose(1, 0, 2)


def _join_shards(gathered, axis):
    _, r, c = gathered.shape
    if axis == 0:
        return gathered.reshape(N_DEV * r, c)
    return gathered.transpose(1, 0, 2).reshape(r, N_DEV * c)


def kernel(x, mem, positions, g_mix, w_in, b_gate, g_q_lat, w_uq, g_kv_lat, w_ukv, w_a_proj, w_b_proj, w_o, g_x, g_mem, w_xq, w_xkv, w_xo, g_ffn, w_gate, w_up, w_down, g_final, loss_target, m_g_mix, m_w_in, m_b_gate, m_g_q_lat, m_w_uq, m_g_kv_lat, m_w_ukv, m_w_a_proj, m_w_b_proj, m_w_o, m_g_x, m_g_mem, m_w_xq, m_w_xkv, m_w_xo, m_g_ffn, m_w_gate, m_w_up, m_w_down, m_g_final, v_g_mix, v_w_in, v_b_gate, v_g_q_lat, v_w_uq, v_g_kv_lat, v_w_ukv, v_w_a_proj, v_w_b_proj, v_w_o, v_g_x, v_g_mem, v_w_xq, v_w_xkv, v_w_xo, v_g_ffn, v_w_gate, v_w_up, v_w_down, v_g_final):
    given = dict(locals())
    s_len = x.shape[1]
    x2d = x.reshape(s_len, D_MODEL)
    mem2d = mem.reshape(-1, D_MODEL)
    target = loss_target.reshape(s_len, D_MODEL)

    names = [name for name, _, _ in SHARDED]
    axis_of = {name: axis for name, _, axis in SHARDED}
    shard2d = lambda name, prefix="": given[prefix + name].reshape(
        _shard_shape(dict((n, s) for n, s, _ in SHARDED)[name], axis_of[name]))

    wire = lambda name: shard2d(name) if name == "b_gate" else shard2d(name).astype(BF16)
    early = [n for n in names if n in NEEDED_FIRST]
    late = [n for n in names if n not in NEEDED_FIRST]
    gathered = _exchange(True, [wire(n) for n in early], "weights_gather_first")
    wts = {n: _join_shards(g, axis_of[n]) for n, g in zip(early, gathered)}

    w_in_p = jnp.concatenate([wts["w_in"][:, :LAT_COLS], jnp.zeros((D_MODEL, D_IN_PAD - D_IN), BF16),
                              wts["w_in"][:, LAT_COLS:]], axis=1)
    w_uq_p = jnp.pad(wts["w_uq"].reshape(256, MLA_HEADS, 96), ((0, 0), (0, 0), (0, 32))).reshape(256, 1024)
    ukv = wts["w_ukv"].reshape(128, MLA_HEADS, 128)
    w_uk_p = jnp.pad(ukv[:, :, :64], ((0, 0), (0, 0), (0, 64))).reshape(128, 1024)
    w_uv = ukv[:, :, 64:].reshape(128, 512)
    w_uv1 = jnp.pad(ukv[:, :, 64:], ((0, 0), (0, 0), (0, 64))).reshape(128, 1024)
    bg = wts["b_gate"]

    inv_freq = ROPE_THETA ** (-jnp.arange(0, MLA_ROPE, 2, dtype=F32) / MLA_ROPE)
    ang = positions.reshape(s_len).astype(F32)[:, None] * inv_freq
    cos16, sin16 = jnp.cos(ang), jnp.sin(ang)
    cosf = jnp.concatenate([jnp.ones((s_len, 64), F32), cos16, cos16, jnp.ones((s_len, 32), F32)], axis=1)
    sinf = jnp.concatenate([jnp.zeros((s_len, 64), F32), sin16, sin16, jnp.zeros((s_len, 32), F32)], axis=1)

    h1, lat, sb, gates = _in_proj(x2d, g_mix, w_in_p)
    qa, ka, va, va1, q_lat, kv_lat = _mla_prep(lat, g_q_lat, g_kv_lat, w_uq_p, w_uk_p, w_uv, w_uv1, cosf, sinf)
    oa, lse, gathered = _mla_fwd(qa, ka, va1, _Exchange(True, [wire(n) for n in late]))
    wts.update({n: _join_shards(g, axis_of[n]) for n, g in zip(late, gathered)})
    ob, sb_r = _sb_fwd(sb)
    x1 = _merge_fwd(x2d, oa, ob, gates, bg, wts["w_a_proj"], wts["w_b_proj"], wts["w_o"])
    mn, xkv = _mem_kv(mem2d, g_mem, wts["w_xkv"])
    x2 = _xattn_fwd(x1, g_x, wts["w_xq"], xkv, wts["w_xo"])
    x3, hf = _ffn_fwd(x2, g_ffn, wts["w_gate"], wts["w_up"], wts["w_down"])
    g_final2d = g_final.reshape(1, D_MODEL)
    sse, dx3, dx3b, dg_final = _loss_head(x3, g_final2d, target)
    loss = lax.psum(sse[0, 0] * (0.5 / D_MODEL), ("x", "y", "c"))

    dx2, dgt, dup, act, dg_ffn = _ffn_bwd(x2, hf, dx3, dx3b, g_ffn, wts["w_gate"], wts["w_up"], wts["w_down"])
    dx1, dw_xq, dw_xo, dxkv, dg_x = _xattn_bwd(x1, dx2, g_x, wts["w_xq"], xkv, wts["w_xo"])
    dw_xkv, dg_mem = _mem_bwd(mem2d, g_mem, wts["w_xkv"], mn, dxkv)
    doa, dob, dgates, dpa, dpb, merged, dx1b, dbg = _merge_bwd(
        dx1, oa, ob, gates, bg, wts["w_a_proj"], wts["w_b_proj"], wts["w_o"])
    dsbq, dsbk, dsbv = _sb_bwd(sb, dob, sb_r)
    full_grads = {
        "w_a_proj": _tn_matmul(oa, dpa, "dw_a"),
        "w_b_proj": _tn_matmul(ob, dpb, "dw_b"),
        "w_o": _tn_matmul(merged, dx1b, "dw_o"),
        "w_xq": dw_xq,
        "w_xkv": dw_xkv,
        "w_xo": dw_xo,
        "w_gate": _tn_matmul(hf, dgt, "dw_gate", tn=FF_TILE),
        "w_up": _tn_matmul(hf, dup, "dw_up", tn=FF_TILE),
        "w_down": _tn_matmul(act, dx3b, "dw_down", tka=FF_TILE),
    }
    dw_sb = _tn_matmul_sb(h1, dsbq, dsbk, dsbv, "dw_in_sb")
    own0 = D_IN // N_DEV - LAT_COLS
    w_in_rest = _split_pieces(
        jnp.concatenate([dw_sb[:, :LAT_COLS], dw_sb, _tn_matmul(h1, dgates, "dw_in_gates")], axis=1), 1)
    dqa, dka, dva, got = _mla_bwd(
        qa, ka, va, oa, doa, lse,
        _Exchange(False, [_split_pieces(full_grads[n], axis_of[n]) for n in late] + [w_in_rest]))
    landed = dict(zip(late, got[:-1]))
    dlat, dqb, dkb, dvb, dg_q, dg_kv = _mla_prep_bwd(
        lat, g_q_lat, g_kv_lat, w_uq_p, w_uk_p, w_uv, cosf, sinf, dqa, dka, dva)
    w_in_first = jnp.concatenate([_tn_matmul(h1, dlat, "dw_in_lat")[:, :LAT_COLS], dw_sb[:, :own0]], axis=1)
    dw_uq_p = _tn_matmul(q_lat, dqb, "dw_uq")
    dw_uk_p = _tn_matmul(kv_lat, dkb, "dw_uk")
    dw_uv = _tn_matmul(kv_lat, dvb, "dw_uv")
    full_grads.update({
        "b_gate": dbg,
        "w_uq": dw_uq_p.reshape(256, MLA_HEADS, 128)[:, :, :96].reshape(256, 768),
        "w_ukv": jnp.concatenate([dw_uk_p.reshape(128, MLA_HEADS, 128)[:, :, :64],
                                  dw_uv.reshape(128, MLA_HEADS, 64)], axis=2).reshape(128, 1024),
    })
    small = [n for n in early if n != "w_in"]
    grad_x, dg_mix, got2 = _in_proj_bwd(
        x2d, g_mix, w_in_p, dx1, dlat, dsbq, dsbk, dsbv, dgates,
        _Exchange(False, [_split_pieces(full_grads[n], axis_of[n]) for n in small], to_first=[w_in_first]))
    landed.update(zip(small, got2[:-1]))
    me = 4 * lax.axis_index("x") + 2 * lax.axis_index("y") + lax.axis_index("c")
    landed["w_in"] = jnp.where(me == 0, got2[-1], got[-1])
    rep_grads = {"g_mix": dg_mix, "g_q_lat": dg_q, "g_kv_lat": dg_kv, "g_x": dg_x, "g_mem": dg_mem,
                 "g_ffn": dg_ffn, "g_final": dg_final}
    rep_cat = lambda prefix, src: jnp.concatenate(
        [src[prefix + n].reshape(-1) for n, _ in REPLICATED]).reshape(-1, LANES)
    rep_src = jnp.broadcast_to(rep_cat("", rep_grads), (N_DEV,) + rep_cat("", rep_grads).shape)
    rep_landed = _exchange(False, [rep_src], "grads_gains")[0]

    res = {}
    for name, _, _ in SHARDED:
        outs = _adamw(landed[name], shard2d(name), shard2d(name, "m_"), shard2d(name, "v_"), "adamw_" + name)
        res[name] = [o.reshape(given[name].shape) for o in outs]
    rep_outs = _adamw(rep_landed, rep_cat("", given), rep_cat("m_", given), rep_cat("v_", given), "adamw_gains")
    off = 0
    for name, n in REPLICATED:
        res[name] = [o.reshape(-1)[off:off + n].reshape(given[name].shape) for o in rep_outs]
        off += n
    result = [loss, grad_x.reshape(x.shape)]
    for k in range(4):
        result.extend(res[name][k] for name in WEIGHT_ORDER)
    return tuple(result)
```

```python
import functools
import math

import jax
import jax.numpy as jnp
from jax import lax
from jax.experimental import pallas as pl
from jax.experimental.pallas import tpu as pltpu

F32 = jnp.float32
BF16 = jnp.bfloat16

D_MODEL = 1024
MLA_HEADS = 8
MLA_Q_RANK = 256
MLA_KV_RANK = 128
MLA_NOPE = 64
MLA_ROPE = 32
ROPE_THETA = 10000.0
SB_WIDTH = 512
X_HEADS = 4
X_HEAD_DIM = 128
D_FF = 2816
EPS = 1e-6
D_IN = 4000
D_IN_PAD = 4096
K_R_OFF = 384
LAT_COLS = 416
LANES = 128
HEAD_PAD = 128
MLA_SCALE = 1.0 / math.sqrt(MLA_NOPE + MLA_ROPE)
SB_SCALE = 0.125
LOG2E = math.log2(math.e)
LN2 = math.log(2.0)
MLA_Q_FOLD = MLA_SCALE * LOG2E
SB_Q_FOLD = SB_SCALE * LOG2E
SB_CUT = -160.0
X_SCALE = 1.0 / math.sqrt(X_HEAD_DIM)
NEG_BIG = -1e30

ADAM_LR = 0.001
ADAM_B1 = 0.9
ADAM_B2 = 0.999
ADAM_EPS = 1e-08
ADAM_WD = 0.01
ADAM_STEP = 10

N_DEV = 8
MIB = 1024 * 1024
ADAM_BLOCK_BYTES = 4 * MIB

SHARDED = (
    ("w_in", (D_MODEL, D_IN), 1),
    ("b_gate", (2, D_MODEL), 1),
    ("w_uq", (MLA_Q_RANK, 768), 1),
    ("w_ukv", (MLA_KV_RANK, 1024), 1),
    ("w_a_proj", (512, D_MODEL), 1),
    ("w_b_proj", (512, D_MODEL), 1),
    ("w_o", (D_MODEL, D_MODEL), 0),
    ("w_xq", (D_MODEL, 512), 0),
    ("w_xkv", (D_MODEL, 1024), 0),
    ("w_xo", (512, D_MODEL), 1),
    ("w_gate", (D_MODEL, D_FF), 1),
    ("w_up", (D_MODEL, D_FF), 1),
    ("w_down", (D_FF, D_MODEL), 0),
)
NEEDED_FIRST = ("w_in", "b_gate", "w_uq", "w_ukv")
REPLICATED = (
    ("g_mix", 1024), ("g_q_lat", 256), ("g_kv_lat", 128), ("g_x", 1024),
    ("g_mem", 1024), ("g_ffn", 1024), ("g_final", 1024),
)
WEIGHT_ORDER = ("g_mix", "w_in", "b_gate", "g_q_lat", "w_uq", "g_kv_lat", "w_ukv", "w_a_proj",
                "w_b_proj", "w_o", "g_x", "g_mem", "w_xq", "w_xkv", "w_xo", "g_ffn", "w_gate",
                "w_up", "w_down", "g_final")


def _round_up(n, m):
    return -(-n // m) * m


def _cparams(vmem_mib=None, **kw):
    if vmem_mib is not None:
        kw["vmem_limit_bytes"] = vmem_mib * MIB
    return pltpu.CompilerParams(**kw)


def _dot(a, b):
    return jnp.dot(a, b, preferred_element_type=F32)


def _dot_nt(a, b):
    return lax.dot_general(a, b, (((1,), (1,)), ((), ())), preferred_element_type=F32)


def _dot_tn(a, b):
    return lax.dot_general(a, b, (((0,), (0,)), ((), ())), preferred_element_type=F32)


def _rms(x, g):
    r = lax.rsqrt(jnp.mean(x * x, axis=-1, keepdims=True) + EPS)
    xh = x * r
    return xh * g, xh, r


def _rms_bwd(dy, xh, r, g):
    u = dy * g
    dx = r * (u - xh * jnp.mean(u * xh, axis=-1, keepdims=True))
    return dx, dy * xh


def _sigmoid(z):
    return 1.0 / (1.0 + jnp.exp(-z))


def _acc_rows(ref, val, first):
    s = jnp.sum(val, axis=0, keepdims=True)

    @pl.when(first)
    def _():
        ref[...] = s

    @pl.when(jnp.logical_not(first))
    def _():
        ref[...] += s


def _acc(ref, val, first):
    @pl.when(first)
    def _():
        ref[...] = val

    @pl.when(jnp.logical_not(first))
    def _():
        ref[...] += val


def _peer(k):
    x, y, c = lax.axis_index("x"), lax.axis_index("y"), lax.axis_index("c")
    px = 1 - x if (k >> 2) & 1 else x
    py = 1 - y if (k >> 1) & 1 else y
    pc = 1 - c if k & 1 else c
    return (px, py, pc), 4 * px + 2 * py + pc


N_PEERS = N_DEV - 1
OTHER_CHIPS = (2, 4, 6)


def _land_shape(gather, src):
    return (N_DEV,) + src.shape if gather else src.shape


class _Exchange:
    def __init__(self, gather, srcs, to_first=()):
        self.gather, self.m, self.srcs = gather, len(srcs), list(srcs) + list(to_first)
        self.n = len(self.srcs)
        self.out_shape = ([jax.ShapeDtypeStruct(_land_shape(gather, s), s.dtype) for s in srcs]
                          + [jax.ShapeDtypeStruct(_land_shape(True, s), s.dtype) for s in to_first])
        self.specs = [pl.BlockSpec(memory_space=pl.ANY)] * self.n
        self.scratch = [pltpu.SemaphoreType.DMA((self.n * N_PEERS,)), pltpu.SemaphoreType.DMA((self.n * N_PEERS,)),
                        pltpu.SemaphoreType.DMA((self.n,))]

    def bind(self, src, land, sems):
        self.src, self.land = src, land
        self.send_sems, self.recv_sems, self.local_sems = sems

    def _copy(self, a, k, source, to, target=1):
        return pltpu.make_async_remote_copy(
            src_ref=source, dst_ref=to,
            send_sem=self.send_sems.at[a * N_PEERS + k - 1], recv_sem=self.recv_sems.at[a * N_PEERS + k - 1],
            device_id=_peer(target)[0], device_id_type=pl.DeviceIdType.MESH)

    def _row(self, a, k):
        return self.land[a].at[_peer(k)[1]]

    def _mine(self, a):
        me = _peer(0)[1]
        whole = self.gather or a >= self.m
        return pltpu.make_async_copy(self.src[a] if whole else self.src[a].at[me], self.land[a].at[me],
                                     self.local_sems.at[a])

    def issue(self):
        me = _peer(0)[1]
        for a in range(self.m):
            self._mine(a).start()
            for k in ((1,) + OTHER_CHIPS if self.gather else range(1, N_DEV)):
                source = self.src[a] if self.gather else self.src[a].at[_peer(k)[1]]
                self._copy(a, k, source, self.land[a].at[me], target=k).start()
        for a in range(self.m, self.n):
            pl.when(me == 0)(self._mine(a).start)
            for k in range(1, N_DEV):
                pl.when(me == k)(self._copy(a, k, self.src[a], self.land[a].at[me], target=k).start)

    def finish(self):
        me = _peer(0)[1]
        part = lambda a: self.src[a] if self.gather or a >= self.m else self.src[a].at[me]
        if self.gather:
            for a in range(self.m):
                for k in OTHER_CHIPS:
                    self._copy(a, k, part(a), self._row(a, k)).wait_recv()
                    self._copy(a, k + 1, self._row(a, k), self._row(a, k), target=1).start()
        for a in range(self.m):
            for k in ((1, 3, 5, 7) if self.gather else range(1, N_DEV)):
                self._copy(a, k, part(a), self._row(a, k)).wait_recv()
        for a in range(self.m):
            for k in range(1, N_DEV):
                self._copy(a, k, part(a), self.land[a].at[me]).wait_send()
            self._mine(a).wait()
        for a in range(self.m, self.n):
            for k in range(1, N_DEV):
                pl.when(me == 0)(self._copy(a, k, part(a), self._row(a, k)).wait_recv)
                pl.when(me == k)(self._copy(a, k, part(a), self.land[a].at[me]).wait_send)
            pl.when(me == 0)(self._mine(a).wait)


def _exchange(gather, srcs, name):
    ex = _Exchange(gather, srcs)

    def body(*refs):
        ex.bind(refs[:ex.n], refs[ex.n:2 * ex.n], refs[2 * ex.n:])
        ex.issue()
        ex.finish()

    return pl.pallas_call(body, name=name, out_shape=ex.out_shape, in_specs=ex.specs, out_specs=ex.specs,
                          scratch_shapes=ex.scratch)(*ex.srcs)


def _tn_matmul(a, b, name, tka=512, tn=1024, ts=2048):
    s_len, ka = a.shape
    n = b.shape[1]
    tka, tn, ts = min(tka, ka), min(tn, n), min(ts, s_len)
    assert ka % tka == 0 and n % tn == 0 and s_len % ts == 0

    def body(a_ref, b_ref, o_ref):
        _acc(o_ref, _dot_tn(a_ref[...], b_ref[...].astype(BF16)), pl.program_id(2) == 0)

    return pl.pallas_call(
        body, name=name, grid=(ka // tka, n // tn, s_len // ts),
        in_specs=[pl.BlockSpec((ts, tka), lambda i, j, s: (s, i)),
                  pl.BlockSpec((ts, tn), lambda i, j, s: (s, j))],
        out_specs=pl.BlockSpec((tka, tn), lambda i, j, s: (i, j)),
        out_shape=jax.ShapeDtypeStruct((ka, n), F32),
        compiler_params=_cparams(dimension_semantics=("parallel", "parallel", "arbitrary")),
    )(a, b)


def _tn_matmul_sb(a, dq, dk, dv, name, tka=512, ts=2048):
    s_len, ka = a.shape
    tka, ts = min(tka, ka), min(ts, s_len)
    assert ka % tka == 0 and s_len % ts == 0

    def body(a_ref, q_ref, k_ref, v_ref, o_ref):
        first = pl.program_id(1) == 0
        av = a_ref[...]
        for c, val in enumerate((q_ref[...], k_ref[...] * LN2, v_ref[...])):
            _acc(o_ref.at[:, c * SB_WIDTH:(c + 1) * SB_WIDTH], _dot_tn(av, val.astype(BF16)), first)

    colb = pl.BlockSpec((ts, SB_WIDTH), lambda i, s: (s, 0))
    return pl.pallas_call(
        body, name=name, grid=(ka // tka, s_len // ts),
        in_specs=[pl.BlockSpec((ts, tka), lambda i, s: (s, i)), colb, colb, colb],
        out_specs=pl.BlockSpec((tka, 3 * SB_WIDTH), lambda i, s: (i, 0)),
        out_shape=jax.ShapeDtypeStruct((ka, 3 * SB_WIDTH), F32),
        compiler_params=_cparams(48, dimension_semantics=("parallel", "arbitrary")),
    )(a, dq, dk, dv)


def _row_block(s_len):
    return min(s_len, 512)


def _in_proj(x, g, w):
    s_len = x.shape[0]
    tm = _row_block(s_len)

    def body(x_ref, g_ref, w_ref, h_ref, lat_ref, sb_ref, gate_ref):
        h, _, _ = _rms(x_ref[...], g_ref[...])
        hb = h.astype(BF16)
        h_ref[...] = hb
        p = _dot(hb, w_ref[:, 0:1024])
        lat_ref[...] = p[:, 0:512]
        sb_ref[:, 0:512] = (p[:, 512:1024] * SB_Q_FOLD).astype(BF16)
        sb_ref[:, 512:1536] = _dot(hb, w_ref[:, 1024:2048]).astype(BF16)
        gate_ref[:, 0:1024] = _dot(hb, w_ref[:, 2048:3072])
        gate_ref[:, 1024:2048] = _dot(hb, w_ref[:, 3072:4096])

    rowb = lambda n: pl.BlockSpec((tm, n), lambda i: (i, 0))
    return pl.pallas_call(
        body, name="in_proj", grid=(s_len // tm,),
        in_specs=[rowb(D_MODEL), pl.BlockSpec((1, D_MODEL), lambda i: (0, 0)),
                  pl.BlockSpec((D_MODEL, D_IN_PAD), lambda i: (0, 0))],
        out_specs=[rowb(D_MODEL), rowb(512), rowb(3 * SB_WIDTH), rowb(2 * D_MODEL)],
        out_shape=[jax.ShapeDtypeStruct((s_len, D_MODEL), BF16),
                   jax.ShapeDtypeStruct((s_len, 512), F32),
                   jax.ShapeDtypeStruct((s_len, 3 * SB_WIDTH), BF16),
                   jax.ShapeDtypeStruct((s_len, 2 * D_MODEL), F32)],
        compiler_params=_cparams(48, dimension_semantics=("parallel",)),
    )(x, g, w)


def _rope_rot(blk, lane):
    return jnp.where(lane < 80, -pltpu.roll(blk, 112, 1), pltpu.roll(blk, 16, 1))


def _rope_rot_t(blk, lane):
    return jnp.where(lane < 80, pltpu.roll(blk, 112, 1), -pltpu.roll(blk, 16, 1))


def _mla_prep(lat, g_q, g_kv, w_uq, w_uk, w_uv, w_uv1, cosf, sinf):
    s_len = lat.shape[0]
    tm = _row_block(s_len)

    def body(lat_ref, gq_ref, gkv_ref, wuq_ref, wuk_ref, wuv_ref, wuv1_ref, cos_ref, sin_ref,
             q_ref, k_ref, v_ref, v1_ref, ql_ref, kvl_ref):
        lane = lax.broadcasted_iota(jnp.int32, (tm, LANES), 1)
        cosv, sinv = cos_ref[...], sin_ref[...]
        ql, _, _ = _rms(lat_ref[:, 0:256], gq_ref[...])
        kvl, _, _ = _rms(lat_ref[:, 256:384], gkv_ref[...])
        qlb, kvlb = ql.astype(BF16), kvl.astype(BF16)
        ql_ref[...] = qlb
        kvl_ref[...] = kvlb
        q = _dot(qlb, wuq_ref[...])
        kn = _dot(kvlb, wuk_ref[...])
        v_ref[...] = _dot(kvlb, wuv_ref[...]).astype(BF16)
        v1 = _dot(kvlb, wuv1_ref[...])
        wide = lax.broadcasted_iota(jnp.int32, (tm, MLA_HEADS * HEAD_PAD), 1)
        v1_ref[...] = jnp.where(wide % HEAD_PAD == MLA_NOPE, 1.0, v1).astype(BF16)
        kr = pltpu.roll(lat_ref[:, K_R_OFF:K_R_OFF + LANES], 64, 1)
        kr = kr * cosv + _rope_rot(kr, lane) * sinv
        for h in range(MLA_HEADS):
            sl = slice(h * HEAD_PAD, (h + 1) * HEAD_PAD)
            blk = q[:, sl]
            q_ref[:, sl] = ((blk * cosv + _rope_rot(blk, lane) * sinv) * MLA_Q_FOLD).astype(BF16)
            k_ref[:, sl] = (kn[:, sl] + kr).astype(BF16)

    full = lambda shape: pl.BlockSpec(shape, lambda i: (0, 0))
    rowb = lambda n: pl.BlockSpec((tm, n), lambda i: (i, 0))
    return pl.pallas_call(
        body, name="mla_prep", grid=(s_len // tm,),
        in_specs=[rowb(512), full((1, 256)), full((1, 128)), full((256, 1024)), full((128, 1024)),
                  full((128, 512)), full((128, 1024)), rowb(128), rowb(128)],
        out_specs=[rowb(1024), rowb(1024), rowb(512), rowb(1024), rowb(256), rowb(128)],
        out_shape=[jax.ShapeDtypeStruct((s_len, 1024), BF16), jax.ShapeDtypeStruct((s_len, 1024), BF16),
                   jax.ShapeDtypeStruct((s_len, 512), BF16), jax.ShapeDtypeStruct((s_len, 1024), BF16),
                   jax.ShapeDtypeStruct((s_len, 256), BF16), jax.ShapeDtypeStruct((s_len, 128), BF16)],
        compiler_params=_cparams(dimension_semantics=("parallel",)),
    )(lat, g_q, g_kv, w_uq, w_uk, w_uv, w_uv1, cosf, sinf)


ATTN_TQ = 1024
ATTN_TH = 512
ATTN_TK = 256
SB_TQ = 512
SB_TH = 256
MLA_TK = 512


def _attn_blocks(s_len, tk=ATTN_TK, tq=ATTN_TQ, th=ATTN_TH):
    tq, th, tk = min(s_len, tq), min(s_len, th), min(s_len, tk)
    return tq, th, tk, tq // tk


def _chains(tq, th):
    return [(hh, r0) for hh in range(2) for r0 in range(0, tq, th)]


def _diag_mask(th, tk, r0, sub, strict):
    lo, hi = sub * tk, (sub + 1) * tk - 1
    last, first = r0 + th - 1, r0
    if (lo >= last) if strict else (lo > last):
        return "none"
    if (hi < first) if strict else (hi <= first):
        return "all"
    row = lax.broadcasted_iota(jnp.int32, (th, tk), 0) + r0
    col = lax.broadcasted_iota(jnp.int32, (th, tk), 1) + lo
    return col < row if strict else col <= row


def _mla_fwd(q, k, v, ride):
    s_len = q.shape[0]
    tq, th, tk, nsub = _attn_blocks(s_len, MLA_TK)
    chains = _chains(tq, th)
    nh = tq // th
    nq = s_len // tq

    def body(q_ref, k_ref, v_ref, *rest):
        o_ref, lse_ref = rest[ride.n:ride.n + 2]
        ride.bind(rest[:ride.n], rest[ride.n + 2:2 * ride.n + 2], rest[2 * ride.n + 2:])
        pl.when((pl.program_id(0) == 0) & (pl.program_id(1) == 0))(ride.issue)
        i = pl.program_id(1)
        lane = lax.broadcasted_iota(jnp.int32, (th, LANES), 1)
        hsl = [slice(hh * HEAD_PAD, (hh + 1) * HEAD_PAD) for hh in range(2)]

        def step(kb, carry, sub):
            rows = pl.ds(pl.multiple_of(kb * tk, tk), tk)
            masks = ["all" if sub is None else _diag_mask(th, tk, r0, sub, strict=False) for _, r0 in chains]
            live = [n for n, m in enumerate(masks) if not (isinstance(m, str) and m == "none")]
            s = {n: _dot_nt(q_ref[chains[n][1]:chains[n][1] + th, hsl[chains[n][0]]], k_ref[rows, hsl[chains[n][0]]])
                 for n in live}
            new = list(carry)
            pb, alpha = {}, {}
            for n in live:
                m = carry[n][0]
                sn = s[n]
                if not isinstance(masks[n], str):
                    sn = jnp.where(masks[n], sn, NEG_BIG)
                m_new = jnp.maximum(m, jnp.max(sn, axis=-1, keepdims=True))
                alpha[n] = jnp.exp2(m - m_new)
                pb[n] = jnp.exp2(sn - m_new).astype(BF16)
                new[n] = (m_new, None)
            pv = {n: _dot(pb[n], v_ref[rows, hsl[chains[n][0]]]) for n in live}
            for n in live:
                new[n] = (new[n][0], alpha[n] * carry[n][1] + pv[n])
            return tuple(new)

        init = (jnp.full((th, 1), NEG_BIG, F32), jnp.zeros((th, LANES), F32))
        carry = lax.fori_loop(0, i * nsub, lambda kb, cy: step(kb, cy, None), (init,) * len(chains))
        for sub in range(nsub):
            carry = step(i * nsub + sub, carry, sub)
        for c in range(nh):
            (m0, a0), (m1, a1) = carry[c], carry[nh + c]
            l0, l1 = a0[:, MLA_NOPE:MLA_NOPE + 1], a1[:, MLA_NOPE:MLA_NOPE + 1]
            rs = slice(c * th, (c + 1) * th)
            o_ref[rs, :] = jnp.where(lane < 64, a0 / l0, pltpu.roll(a1 / l1, 64, 1)).astype(BF16)
            lse_ref[rs, :] = jnp.where(lane < 64, m0 + jnp.log2(l0), m1 + jnp.log2(l1))
        pl.when((pl.program_id(0) == 3) & (i == nq - 1))(ride.finish)

    outs = pl.pallas_call(
        body, name="mla_fwd", grid=(4, nq),
        in_specs=[pl.BlockSpec((tq, 2 * HEAD_PAD), lambda p, i: (i, p)),
                  pl.BlockSpec((s_len, 2 * HEAD_PAD), lambda p, i: (0, p)),
                  pl.BlockSpec((s_len, 2 * HEAD_PAD), lambda p, i: (0, p))] + ride.specs,
        out_specs=[pl.BlockSpec((tq, LANES), lambda p, i: (i, p)),
                   pl.BlockSpec((None, tq, LANES), lambda p, i: (p, i, 0))] + ride.specs,
        out_shape=[jax.ShapeDtypeStruct((s_len, 512), BF16),
                   jax.ShapeDtypeStruct((4, s_len, LANES), F32)] + ride.out_shape,
        scratch_shapes=ride.scratch,
        compiler_params=_cparams(40, dimension_semantics=("arbitrary", "arbitrary")),
    )(q, k, v, *ride.srcs)
    return outs[0], outs[1], outs[2:]


def _mla_bwd(q, k, v, o, do, lse, ride):
    s_len = q.shape[0]
    tq, th, tk, nsub = _attn_blocks(s_len, MLA_TK)
    chains = _chains(tq, th)
    nq = s_len // tq

    def body(q_ref, k_ref, v_ref, o_ref, do_ref, lse_ref, *rest):
        dq_ref, dk_ref, dv_ref = rest[ride.n:ride.n + 3]
        ride.bind(rest[:ride.n], rest[ride.n + 3:2 * ride.n + 3], rest[2 * ride.n + 3:])
        pl.when((pl.program_id(0) == 0) & (pl.program_id(1) == 0))(ride.issue)
        i = pl.program_id(1)
        lane = lax.broadcasted_iota(jnp.int32, (th, LANES), 1)

        @pl.when(i == 0)
        def _():
            dk_ref[...] = jnp.zeros_like(dk_ref)
            dv_ref[...] = jnp.zeros_like(dv_ref)

        hsl = [slice(hh * HEAD_PAD, (hh + 1) * HEAD_PAD) for hh in range(2)]
        qs, dos, deltas, lses = [], [], [], []
        for hh, r0 in chains:
            rs = slice(r0, r0 + th)
            qs.append(q_ref[rs, hsl[hh]])
            doh = jnp.where((lane // 64) == hh, do_ref[rs, :], jnp.zeros((), BF16))
            dos.append(doh)
            deltas.append(jnp.sum(doh.astype(F32) * o_ref[rs, :].astype(F32), axis=-1, keepdims=True))
            lses.append(lse_ref[rs, 64 * hh:64 * hh + 1])

        def step(kb, dqs, sub):
            rows = pl.ds(pl.multiple_of(kb * tk, tk), tk)
            vblk = v_ref[rows, :]
            new, p_all, do_all = [], [], []
            ds_h, q_h = [[], []], [[], []]
            for c, (hh, r0) in enumerate(chains):
                mask = "all" if sub is None else _diag_mask(th, tk, r0, sub, strict=False)
                if isinstance(mask, str) and mask == "none":
                    new.append(dqs[c])
                    continue
                kblk = k_ref[rows, hsl[hh]]
                s = _dot_nt(qs[c], kblk)
                if not isinstance(mask, str):
                    s = jnp.where(mask, s, NEG_BIG)
                p = jnp.exp2(s - lses[c])
                dp = _dot_nt(dos[c], vblk)
                ds = (p * (dp - deltas[c]) * MLA_SCALE).astype(BF16)
                p_all.append(p.astype(BF16))
                do_all.append(dos[c])
                ds_h[hh].append(ds)
                q_h[hh].append(qs[c])
                new.append(dqs[c] + _dot(ds, kblk))
            dv_ref[rows, :] += _dot_tn(jnp.concatenate(p_all, axis=0), jnp.concatenate(do_all, axis=0))
            for hh in range(2):
                dk_ref[rows, hsl[hh]] += _dot_tn(jnp.concatenate(ds_h[hh], axis=0),
                                                 jnp.concatenate(q_h[hh], axis=0))
            return tuple(new)

        zero = jnp.zeros((th, LANES), F32)
        dqs = lax.fori_loop(0, i * nsub, lambda kb, cy: step(kb, cy, None), (zero,) * len(chains))
        for sub in range(nsub):
            dqs = step(i * nsub + sub, dqs, sub)
        for c, (hh, r0) in enumerate(chains):
            dq_ref[r0:r0 + th, hsl[hh]] = dqs[c]
        pl.when((pl.program_id(0) == 3) & (i == nq - 1))(ride.finish)

    outs = pl.pallas_call(
        body, name="mla_bwd", grid=(4, nq),
        in_specs=[pl.BlockSpec((tq, 2 * HEAD_PAD), lambda p, i: (i, p)),
                  pl.BlockSpec((s_len, 2 * HEAD_PAD), lambda p, i: (0, p)),
                  pl.BlockSpec((s_len, LANES), lambda p, i: (0, p)),
                  pl.BlockSpec((tq, LANES), lambda p, i: (i, p)),
                  pl.BlockSpec((tq, LANES), lambda p, i: (i, p)),
                  pl.BlockSpec((None, tq, LANES), lambda p, i: (p, i, 0))] + ride.specs,
        out_specs=[pl.BlockSpec((tq, 2 * HEAD_PAD), lambda p, i: (i, p)),
                   pl.BlockSpec((s_len, 2 * HEAD_PAD), lambda p, i: (0, p)),
                   pl.BlockSpec((s_len, LANES), lambda p, i: (0, p))] + ride.specs,
        out_shape=[jax.ShapeDtypeStruct((s_len, 1024), F32), jax.ShapeDtypeStruct((s_len, 1024), F32),
                   jax.ShapeDtypeStruct((s_len, 512), F32)] + ride.out_shape,
        scratch_shapes=ride.scratch,
        compiler_params=_cparams(56, dimension_semantics=("arbitrary", "arbitrary")),
    )(q, k, v, o, do, lse, *ride.srcs)
    return outs[0], outs[1], outs[2], outs[3:]


def _log_sigmoids(z2):
    sp = jnp.log2(1.0 + jnp.exp2(-jnp.abs(z2)))
    lb = jnp.minimum(z2, 0.0) - sp
    return lb, lb - z2


def _split_dot(x, w, parts, nt=False):
    dot = _dot_nt if nt else _dot
    out = None
    for _ in range(parts):
        xb = x.astype(BF16)
        t = dot(xb, w)
        out = t if out is None else out + t
        x = x - xb.astype(F32)
    return out


def _sb_fwd(sb):
    s_len = sb.shape[0]
    tq, th, tk, nsub = _attn_blocks(s_len, tq=SB_TQ, th=SB_TH)
    chains = _chains(tq, th)
    nh = tq // th
    assert s_len // tk <= 64

    def body(q_ref, k_ref, v_ref, o_ref, r_ref):
        i = pl.program_id(1)
        lane = lax.broadcasted_iota(jnp.int32, (th, LANES), 1)
        upper = (lax.broadcasted_iota(jnp.int32, (tk, tk), 0)
                 > lax.broadcasted_iota(jnp.int32, (tk, tk), 1)).astype(BF16)
        qs = [jnp.where((lane // 64) == hh, q_ref[r0:r0 + th, :], jnp.zeros((), BF16)) for hh, r0 in chains]

        def step(kb, carry, sub):
            rows = pl.ds(pl.multiple_of(kb * tk, tk), tk)
            kblk, vblk = k_ref[rows, :], v_ref[rows, :]
            masks = ["all" if sub is None else _diag_mask(th, tk, r0, sub, strict=True) for _, r0 in chains]
            live = [n for n, m in enumerate(masks) if not (isinstance(m, str) and m == "none")]
            masked = {n: not isinstance(masks[n], str) for n in live}
            z = {n: _dot_nt(qs[n], kblk) for n in live}
            lb, lom = {}, {}
            for n in live:
                lb[n], lom[n] = _log_sigmoids(z[n])
                if masked[n]:
                    lom[n] = jnp.where(masks[n], lom[n], 0.0)
            suf = {n: _split_dot(lom[n], upper, 1) for n in live}
            a = {}
            for n in live:
                a[n] = jnp.exp2(lb[n] + suf[n] + carry[n][0])
                if masked[n]:
                    a[n] = jnp.where(masks[n], a[n], 0.0)
            pv = {n: _dot(a[n].astype(BF16), vblk) for n in live}
            new = list(carry)
            for n in live:
                c, acc, r = carry[n]
                rs = suf[n][:, 0:1] + lom[n][:, 0:1]
                new[n] = (c + rs, acc + pv[n], jnp.where(lane == 64 * chains[n][0] + kb, rs, r))
            return tuple(new)

        init = (jnp.zeros((th, 1), F32), jnp.zeros((th, LANES), F32), jnp.zeros((th, LANES), F32))
        carry = (init,) * len(chains)
        for sub in reversed(range(nsub)):
            carry = step(i * nsub + sub, carry, sub)

        def spent(cy):
            top = functools.reduce(jnp.maximum, [jnp.max(c) for c, _, _ in cy])
            return (top < SB_CUT).astype(jnp.int32)

        def walk(state):
            t, _, cy = state
            cy = step(i * nsub - 1 - t, cy, None)
            return t + 1, spent(cy), cy

        _, _, carry = lax.while_loop(lambda st: (st[0] < i * nsub) & (st[1] == 0), walk,
                                     (jnp.int32(0), spent(carry), carry))
        for n in range(nh):
            rs = slice(n * th, (n + 1) * th)
            o_ref[rs, :] = jnp.where(lane < 64, carry[n][1], carry[nh + n][1]).astype(BF16)
            r_ref[rs, :] = jnp.where(lane < 64, carry[n][2], carry[nh + n][2])

    return pl.pallas_call(
        body, name="sb_fwd", grid=(4, s_len // tq),
        in_specs=[pl.BlockSpec((tq, LANES), lambda p, i: (i, p)),
                  pl.BlockSpec((s_len, LANES), lambda p, i: (0, 4 + p)),
                  pl.BlockSpec((s_len, LANES), lambda p, i: (0, 8 + p))],
        out_specs=[pl.BlockSpec((tq, LANES), lambda p, i: (i, p)),
                   pl.BlockSpec((None, tq, LANES), lambda p, i: (p, i, 0))],
        out_shape=[jax.ShapeDtypeStruct((s_len, 512), BF16), jax.ShapeDtypeStruct((4, s_len, LANES), F32)],
        compiler_params=_cparams(40, dimension_semantics=("parallel", "arbitrary")),
    )(sb, sb, sb)


def _sb_bwd(sb, do, r):
    s_len = sb.shape[0]
    tq, th, tk, nsub = _attn_blocks(s_len)
    chains = _chains(tq, th)
    nh = tq // th

    def body(q_ref, k_ref, v_ref, do_ref, r_ref, dq_ref, dk_ref, dv_ref):
        i = pl.program_id(1)
        lane = lax.broadcasted_iota(jnp.int32, (th, LANES), 1)
        upper = (lax.broadcasted_iota(jnp.int32, (tk, tk), 0)
                 > lax.broadcasted_iota(jnp.int32, (tk, tk), 1)).astype(BF16)
        tri = (lax.broadcasted_iota(jnp.int32, (LANES, LANES), 0)
               > lax.broadcasted_iota(jnp.int32, (LANES, LANES), 1)).astype(BF16)

        @pl.when(i == 0)
        def _():
            dk_ref[...] = jnp.zeros_like(dk_ref)
            dv_ref[...] = jnp.zeros_like(dv_ref)

        qs, dos, rights = [], [], []
        for hh, r0 in chains:
            rs = slice(r0, r0 + th)
            hm = (lane // 64) == hh
            qs.append(jnp.where(hm, q_ref[rs, :], jnp.zeros((), BF16)))
            dos.append(jnp.where(hm, do_ref[rs, :], jnp.zeros((), BF16)))
            rights.append(_split_dot(jnp.where(hm, r_ref[rs, :], 0.0), tri, 3))

        def step(kb, carry, sub):
            rows = pl.ds(pl.multiple_of(kb * tk, tk), tk)
            kblk, vblk = k_ref[rows, :], v_ref[rows, :]
            new, a_all, do_all, dz_all, q_all = [], [], [], [], []
            for n, ((hh, r0), (pre, dq)) in enumerate(zip(chains, carry)):
                mask = "all" if sub is None else _diag_mask(th, tk, r0, sub, strict=True)
                if isinstance(mask, str) and mask == "none":
                    new.append((pre, dq))
                    continue
                c = jnp.sum(jnp.where(lane == 64 * hh + kb, rights[n], 0.0), axis=-1, keepdims=True)
                z = _dot_nt(qs[n], kblk)
                lb, lom = _log_sigmoids(z)
                if not isinstance(mask, str):
                    lom = jnp.where(mask, lom, 0.0)
                suf = _split_dot(lom, upper, 1)
                a = jnp.exp2(lb + suf + c)
                if not isinstance(mask, str):
                    a = jnp.where(mask, a, 0.0)
                g = a * _dot_nt(dos[n], vblk)
                left = _split_dot(g, upper, 1, nt=True) + pre
                sig = jnp.exp2(lb)
                dz = g * (1.0 - sig) - sig * left
                if not isinstance(mask, str):
                    dz = jnp.where(mask, dz, 0.0)
                dzb = dz.astype(BF16)
                a_all.append(a.astype(BF16))
                do_all.append(dos[n])
                dz_all.append(dzb)
                q_all.append(qs[n])
                new.append((left[:, tk - 1:tk] + g[:, tk - 1:tk], dq + _dot(dzb, kblk)))
            dv_ref[rows, :] += _dot_tn(jnp.concatenate(a_all, axis=0), jnp.concatenate(do_all, axis=0))
            dk_ref[rows, :] += _dot_tn(jnp.concatenate(dz_all, axis=0), jnp.concatenate(q_all, axis=0))
            return tuple(new)

        lane1 = lax.broadcasted_iota(jnp.int32, (1, LANES), 1)
        first = i * nsub
        for n, (hh, _) in enumerate(chains):
            top = jnp.max(rights[n], axis=0, keepdims=True)
            kb_of = lane1 - 64 * hh
            live = (kb_of >= 0) & (kb_of < i * nsub) & (top >= SB_CUT)
            first = jnp.minimum(first, jnp.min(jnp.where(live, kb_of, i * nsub)))

        init = (jnp.zeros((th, 1), F32), jnp.zeros((th, LANES), F32))
        carry = lax.fori_loop(first, i * nsub, lambda kb, cy: step(kb, cy, None), (init,) * len(chains))
        for sub in range(nsub):
            carry = step(i * nsub + sub, carry, sub)
        for n in range(nh):
            dq_ref[n * th:(n + 1) * th, :] = jnp.where(lane < 64, carry[n][1], carry[nh + n][1]) * SB_SCALE

    return pl.pallas_call(
        body, name="sb_bwd", grid=(4, s_len // tq),
        in_specs=[pl.BlockSpec((tq, LANES), lambda p, i: (i, p)),
                  pl.BlockSpec((s_len, LANES), lambda p, i: (0, 4 + p)),
                  pl.BlockSpec((s_len, LANES), lambda p, i: (0, 8 + p)),
                  pl.BlockSpec((tq, LANES), lambda p, i: (i, p)),
                  pl.BlockSpec((None, tq, LANES), lambda p, i: (p, i, 0))],
        out_specs=[pl.BlockSpec((tq, LANES), lambda p, i: (i, p)),
                   pl.BlockSpec((s_len, LANES), lambda p, i: (0, p)),
                   pl.BlockSpec((s_len, LANES), lambda p, i: (0, p))],
        out_shape=[jax.ShapeDtypeStruct((s_len, 512), F32)] * 3,
        compiler_params=_cparams(48, dimension_semantics=("arbitrary", "arbitrary")),
    )(sb, sb, sb, do, r)


def _merge_fwd(x, oa, ob, gates, bg, wa, wb, wo):
    s_len = x.shape[0]
    tm = _row_block(s_len)

    def body(x_ref, oa_ref, ob_ref, g_ref, bg_ref, wa_ref, wb_ref, wo_ref, y_ref):
        pa = _dot(oa_ref[...], wa_ref[...])
        pb = _dot(ob_ref[...], wb_ref[...])
        merged = (_sigmoid(g_ref[:, 0:D_MODEL] + bg_ref[0:1, :]) * pa
                  + _sigmoid(g_ref[:, D_MODEL:2 * D_MODEL] + bg_ref[1:2, :]) * pb)
        y_ref[...] = x_ref[...] + _dot(merged.astype(BF16), wo_ref[...])

    full = lambda shape: pl.BlockSpec(shape, lambda i: (0, 0))
    rowb = lambda n: pl.BlockSpec((tm, n), lambda i: (i, 0))
    return pl.pallas_call(
        body, name="merge_fwd", grid=(s_len // tm,),
        in_specs=[rowb(1024), rowb(512), rowb(512), rowb(2048), full((2, 1024)), full((512, 1024)),
                  full((512, 1024)), full((1024, 1024))],
        out_specs=rowb(1024),
        out_shape=jax.ShapeDtypeStruct((s_len, D_MODEL), F32),
        compiler_params=_cparams(48, dimension_semantics=("parallel",)),
    )(x, oa, ob, gates, bg, wa, wb, wo)


def _merge_bwd(dx1, oa, ob, gates, bg, wa, wb, wo):
    s_len = dx1.shape[0]
    tm = _row_block(s_len)

    def body(dx_ref, oa_ref, ob_ref, g_ref, bg_ref, wa_ref, wb_ref, wo_ref,
             doa_ref, dob_ref, dgate_ref, dpa_ref, dpb_ref, merged_ref, dxb_ref, dbg_ref):
        first = pl.program_id(0) == 0
        dxb = dx_ref[...].astype(BF16)
        dxb_ref[...] = dxb
        pa = _dot(oa_ref[...], wa_ref[...])
        pb = _dot(ob_ref[...], wb_ref[...])
        sa = _sigmoid(g_ref[:, 0:D_MODEL] + bg_ref[0:1, :])
        sbg = _sigmoid(g_ref[:, D_MODEL:2 * D_MODEL] + bg_ref[1:2, :])
        merged_ref[...] = (sa * pa + sbg * pb).astype(BF16)
        dm = _dot_nt(dxb, wo_ref[...])
        dpa = (dm * sa).astype(BF16)
        dpb = (dm * sbg).astype(BF16)
        dpa_ref[...] = dpa
        dpb_ref[...] = dpb
        dga = dm * pa * sa * (1.0 - sa)
        dgb = dm * pb * sbg * (1.0 - sbg)
        dgate_ref[:, 0:D_MODEL] = dga.astype(BF16)
        dgate_ref[:, D_MODEL:2 * D_MODEL] = dgb.astype(BF16)
        _acc_rows(dbg_ref.at[0:1, :], dga, first)
        _acc_rows(dbg_ref.at[1:2, :], dgb, first)
        doa_ref[...] = _dot_nt(dpa, wa_ref[...]).astype(BF16)
        dob_ref[...] = _dot_nt(dpb, wb_ref[...]).astype(BF16)

    full = lambda shape: pl.BlockSpec(shape, lambda i: (0, 0))
    rowb = lambda n: pl.BlockSpec((tm, n), lambda i: (i, 0))
    sds = lambda n, dt: jax.ShapeDtypeStruct((s_len, n), dt)
    return pl.pallas_call(
        body, name="merge_bwd", grid=(s_len // tm,),
        in_specs=[rowb(1024), rowb(512), rowb(512), rowb(2048), full((2, 1024)), full((512, 1024)),
                  full((512, 1024)), full((1024, 1024))],
        out_specs=[rowb(512), rowb(512), rowb(2048), rowb(1024), rowb(1024), rowb(1024), rowb(1024),
                   full((2, 1024))],
        out_shape=[sds(512, BF16), sds(512, BF16), sds(2048, BF16), sds(1024, BF16), sds(1024, BF16),
                   sds(1024, BF16), sds(1024, BF16), jax.ShapeDtypeStruct((2, 1024), F32)],
        compiler_params=_cparams(48, dimension_semantics=("arbitrary",)),
    )(dx1, oa, ob, gates, bg, wa, wb, wo)


def _mem_kv(mem, g, w):
    m_len = mem.shape[0]

    def body(mem_ref, g_ref, w_ref, mn_ref, kv_ref):
        mn, _, _ = _rms(mem_ref[...], g_ref[...])
        mnb = mn.astype(BF16)
        mn_ref[...] = mnb
        kv_ref[...] = _dot(mnb, w_ref[...]).astype(BF16)

    return pl.pallas_call(
        body, name="mem_kv",
        out_shape=[jax.ShapeDtypeStruct((m_len, D_MODEL), BF16), jax.ShapeDtypeStruct((m_len, 1024), BF16)],
    )(mem, g, w)


def _mem_bwd(mem, g, w, mn, dkv):
    def body(mem_ref, g_ref, w_ref, mn_ref, dkv_ref, dw_ref, dg_ref):
        dkvb = dkv_ref[...].astype(BF16)
        dw_ref[...] = _dot_tn(mn_ref[...], dkvb)
        dmn = _dot_nt(dkvb, w_ref[...])
        _, xh, _ = _rms(mem_ref[...], g_ref[...])
        dg_ref[...] = jnp.sum(dmn * xh, axis=0, keepdims=True)

    return pl.pallas_call(
        body, name="mem_bwd",
        out_shape=[jax.ShapeDtypeStruct((D_MODEL, 1024), F32), jax.ShapeDtypeStruct((1, D_MODEL), F32)],
    )(mem, g, w, mn, dkv)


def _xattn_heads(xqb, kv_ref, m_len):
    ps = []
    for h in range(X_HEADS):
        hs = slice(h * X_HEAD_DIM, (h + 1) * X_HEAD_DIM)
        s = _dot_nt(xqb[:, hs], kv_ref[:, hs]) * X_SCALE
        e = jnp.exp(s - jnp.max(s, axis=-1, keepdims=True))
        ps.append(e / jnp.sum(e, axis=-1, keepdims=True))
    return ps


def _xattn_fwd(x1, g, wxq, kv, wxo):
    s_len, m_len = x1.shape[0], kv.shape[0]
    tm = _row_block(s_len)

    def body(x_ref, g_ref, wq_ref, kv_ref, wo_ref, y_ref):
        hx, _, _ = _rms(x_ref[...], g_ref[...])
        xqb = _dot(hx.astype(BF16), wq_ref[...]).astype(BF16)
        ps = _xattn_heads(xqb, kv_ref, m_len)
        xo = jnp.concatenate(
            [_dot(ps[h].astype(BF16), kv_ref[:, 512 + h * X_HEAD_DIM:512 + (h + 1) * X_HEAD_DIM])
             for h in range(X_HEADS)], axis=-1)
        y_ref[...] = x_ref[...] + _dot(xo.astype(BF16), wo_ref[...])

    full = lambda shape: pl.BlockSpec(shape, lambda i: (0, 0))
    rowb = lambda n: pl.BlockSpec((tm, n), lambda i: (i, 0))
    return pl.pallas_call(
        body, name="xattn_fwd", grid=(s_len // tm,),
        in_specs=[rowb(1024), full((1, 1024)), full((1024, 512)), full((m_len, 1024)), full((512, 1024))],
        out_specs=rowb(1024),
        out_shape=jax.ShapeDtypeStruct((s_len, D_MODEL), F32),
        compiler_params=_cparams(48, dimension_semantics=("parallel",)),
    )(x1, g, wxq, kv, wxo)


def _xattn_bwd(x1, dx2, g, wxq, kv, wxo):
    s_len, m_len = x1.shape[0], kv.shape[0]
    tm = _row_block(s_len)

    def body(x_ref, dy_ref, g_ref, wq_ref, kv_ref, wo_ref, dx_ref, dwq_ref, dwo_ref, dkv_ref, dg_ref):
        first = pl.program_id(0) == 0
        gv = g_ref[...]
        hx, xh, r = _rms(x_ref[...], gv)
        hxb = hx.astype(BF16)
        xqb = _dot(hxb, wq_ref[...]).astype(BF16)
        ps = _xattn_heads(xqb, kv_ref, m_len)
        dy = dy_ref[...]
        dyb = dy.astype(BF16)
        dxo = _dot_nt(dyb, wo_ref[...])
        xos, dqs, dks, dvs = [], [], [], []
        for h in range(X_HEADS):
            hs = slice(h * X_HEAD_DIM, (h + 1) * X_HEAD_DIM)
            vs = slice(512 + h * X_HEAD_DIM, 512 + (h + 1) * X_HEAD_DIM)
            p = ps[h]
            pb = p.astype(BF16)
            dxoh = dxo[:, hs].astype(BF16)
            xos.append(_dot(pb, kv_ref[:, vs]))
            dp = _dot_nt(dxoh, kv_ref[:, vs])
            ds = (p * (dp - jnp.sum(dp * p, axis=-1, keepdims=True)) * X_SCALE).astype(BF16)
            dvs.append(_dot_tn(pb, dxoh))
            dks.append(_dot_tn(ds, xqb[:, hs]))
            dqs.append(_dot(ds, kv_ref[:, hs]))
        xob = jnp.concatenate(xos, axis=-1).astype(BF16)
        dxqb = jnp.concatenate(dqs, axis=-1).astype(BF16)
        _acc(dwo_ref, _dot_tn(xob, dyb), first)
        _acc(dwq_ref, _dot_tn(hxb, dxqb), first)
        _acc(dkv_ref, jnp.concatenate(dks + dvs, axis=-1), first)
        dhx = _dot_nt(dxqb, wq_ref[...])
        dx, dgr = _rms_bwd(dhx, xh, r, gv)
        dx_ref[...] = dy + dx
        _acc_rows(dg_ref, dgr, first)

    full = lambda shape: pl.BlockSpec(shape, lambda i: (0, 0))
    rowb = lambda n: pl.BlockSpec((tm, n), lambda i: (i, 0))
    return pl.pallas_call(
        body, name="xattn_bwd", grid=(s_len // tm,),
        in_specs=[rowb(1024), rowb(1024), full((1, 1024)), full((1024, 512)), full((m_len, 1024)),
                  full((512, 1024))],
        out_specs=[rowb(1024), full((1024, 512)), full((512, 1024)), full((m_len, 1024)), full((1, 1024))],
        out_shape=[jax.ShapeDtypeStruct((s_len, D_MODEL), F32), jax.ShapeDtypeStruct((1024, 512), F32),
                   jax.ShapeDtypeStruct((512, 1024), F32), jax.ShapeDtypeStruct((m_len, 1024), F32),
                   jax.ShapeDtypeStruct((1, D_MODEL), F32)],
        compiler_params=_cparams(48, dimension_semantics=("arbitrary",)),
    )(x1, dx2, g, wxq, kv, wxo)


FF_TILE = 1408


def _ffn_fwd(x2, g, wg, wu, wd):
    s_len = x2.shape[0]
    tm, tf = _row_block(s_len), FF_TILE

    def body(x_ref, g_ref, wg_ref, wu_ref, wd_ref, y_ref, h_ref):
        j = pl.program_id(1)

        @pl.when(j == 0)
        def _():
            hf, _, _ = _rms(x_ref[...], g_ref[...])
            h_ref[...] = hf.astype(BF16)
            y_ref[...] = x_ref[...]

        hb = h_ref[...]
        gt = _dot(hb, wg_ref[...])
        up = _dot(hb, wu_ref[...])
        act = gt * _sigmoid(gt) * up
        y_ref[...] += _dot(act.astype(BF16), wd_ref[...])

    rowb = pl.BlockSpec((tm, D_MODEL), lambda i, j: (i, 0))
    return pl.pallas_call(
        body, name="ffn_fwd", grid=(s_len // tm, D_FF // tf),
        in_specs=[rowb, pl.BlockSpec((1, D_MODEL), lambda i, j: (0, 0)),
                  pl.BlockSpec((D_MODEL, tf), lambda i, j: (0, j)),
                  pl.BlockSpec((D_MODEL, tf), lambda i, j: (0, j)),
                  pl.BlockSpec((tf, D_MODEL), lambda i, j: (j, 0))],
        out_specs=[rowb, rowb],
        out_shape=[jax.ShapeDtypeStruct((s_len, D_MODEL), F32), jax.ShapeDtypeStruct((s_len, D_MODEL), BF16)],
        compiler_params=_cparams(48, dimension_semantics=("parallel", "arbitrary")),
    )(x2, g, wg, wu, wd)


def _ffn_bwd(x2, hf, dx3, dx3b, g, wg, wu, wd):
    s_len = x2.shape[0]
    tm, tf = _row_block(s_len), FF_TILE
    nf = D_FF // tf

    def act_body(h_ref, dy_ref, wg_ref, wu_ref, wd_ref, dgt_ref, dup_ref, act_ref):
        hb = h_ref[...]
        gt = _dot(hb, wg_ref[...])
        up = _dot(hb, wu_ref[...])
        sg = _sigmoid(gt)
        silu = gt * sg
        dact = _dot_nt(dy_ref[...], wd_ref[...])
        dgt_ref[...] = (dact * up * (sg * (1.0 + gt * (1.0 - sg)))).astype(BF16)
        dup_ref[...] = (dact * silu).astype(BF16)
        act_ref[...] = (silu * up).astype(BF16)

    rowb = pl.BlockSpec((tm, D_MODEL), lambda i, j: (i, 0))
    ffb = pl.BlockSpec((tm, tf), lambda i, j: (i, j))
    dgt, dup, act = pl.pallas_call(
        act_body, name="ffn_bwd_act", grid=(s_len // tm, nf),
        in_specs=[rowb, rowb,
                  pl.BlockSpec((D_MODEL, tf), lambda i, j: (0, j)),
                  pl.BlockSpec((D_MODEL, tf), lambda i, j: (0, j)),
                  pl.BlockSpec((tf, D_MODEL), lambda i, j: (j, 0))],
        out_specs=[ffb, ffb, ffb],
        out_shape=[jax.ShapeDtypeStruct((s_len, D_FF), BF16)] * 3,
        compiler_params=_cparams(56, dimension_semantics=("parallel", "arbitrary")),
    )(hf, dx3b, wg, wu, wd)
    tm = min(s_len, 256)

    def in_body(x_ref, dy_ref, g_ref, wg_ref, wu_ref, dgt_ref, dup_ref, dx_ref, dg_ref):
        dh = _dot_nt(dgt_ref[...], wg_ref[...]) + _dot_nt(dup_ref[...], wu_ref[...])
        gv = g_ref[...]
        _, xh, r = _rms(x_ref[...], gv)
        dx, dgr = _rms_bwd(dh, xh, r, gv)
        dx_ref[...] = dy_ref[...] + dx
        _acc_rows(dg_ref, dgr, pl.program_id(0) == 0)

    row1 = lambda n: pl.BlockSpec((tm, n), lambda i: (i, 0))
    full = lambda shape: pl.BlockSpec(shape, lambda i: (0, 0))
    dx2, dg = pl.pallas_call(
        in_body, name="ffn_bwd_in", grid=(s_len // tm,),
        in_specs=[row1(D_MODEL), row1(D_MODEL), full((1, D_MODEL)), full((D_MODEL, D_FF)), full((D_MODEL, D_FF)),
                  row1(D_FF), row1(D_FF)],
        out_specs=[row1(D_MODEL), full((1, D_MODEL))],
        out_shape=[jax.ShapeDtypeStruct((s_len, D_MODEL), F32), jax.ShapeDtypeStruct((1, D_MODEL), F32)],
        compiler_params=_cparams(48, dimension_semantics=("arbitrary",)),
    )(x2, dx3, g, wg, wu, dgt, dup)
    return dx2, dgt, dup, act, dg


def _loss_head(x3, g, target):
    s_len = x3.shape[0]
    tm = _row_block(s_len)

    def body(x_ref, g_ref, t_ref, sse_ref, dx_ref, dxb_ref, dg_ref):
        first = pl.program_id(0) == 0
        gv = g_ref[...]
        y, xh, r = _rms(x_ref[...], gv)
        err = y - t_ref[...]
        _acc(sse_ref, jnp.broadcast_to(jnp.sum(err * err), (8, LANES)), first)
        dx, dgr = _rms_bwd(err * (1.0 / D_MODEL), xh, r, gv)
        dx_ref[...] = dx
        dxb_ref[...] = dx.astype(BF16)
        _acc_rows(dg_ref, dgr, first)

    rowb = pl.BlockSpec((tm, D_MODEL), lambda i: (i, 0))
    return pl.pallas_call(
        body, name="loss_head", grid=(s_len // tm,),
        in_specs=[rowb, pl.BlockSpec((1, D_MODEL), lambda i: (0, 0)), rowb],
        out_specs=[pl.BlockSpec((8, LANES), lambda i: (0, 0)), rowb, rowb,
                   pl.BlockSpec((1, D_MODEL), lambda i: (0, 0))],
        out_shape=[jax.ShapeDtypeStruct((8, LANES), F32), jax.ShapeDtypeStruct((s_len, D_MODEL), F32),
                   jax.ShapeDtypeStruct((s_len, D_MODEL), BF16), jax.ShapeDtypeStruct((1, D_MODEL), F32)],
        compiler_params=_cparams(dimension_semantics=("arbitrary",)),
    )(x3, g, target)


def _mla_prep_bwd(lat, g_q, g_kv, w_uq, w_uk, w_uv, cosf, sinf, dq, dk, dv):
    s_len = lat.shape[0]
    tm = _row_block(s_len)

    def body(lat_ref, gq_ref, gkv_ref, wuq_ref, wuk_ref, wuv_ref, cos_ref, sin_ref, dq_ref, dk_ref, dv_ref,
             dlat_ref, dqb_ref, dkb_ref, dvb_ref, dgq_ref, dgkv_ref):
        first = pl.program_id(0) == 0
        lane = lax.broadcasted_iota(jnp.int32, (tm, LANES), 1)
        cosv, sinv = cos_ref[...], sin_ref[...]
        gq, gkv = gq_ref[...], gkv_ref[...]
        _, qxh, qr = _rms(lat_ref[:, 0:256], gq)
        _, kxh, kr_ = _rms(lat_ref[:, 256:384], gkv)
        dkr = jnp.zeros((tm, LANES), F32)
        for h in range(MLA_HEADS):
            sl = slice(h * HEAD_PAD, (h + 1) * HEAD_PAD)
            blk = dq_ref[:, sl]
            dqb_ref[:, sl] = (blk * cosv + _rope_rot_t(blk, lane) * sinv).astype(BF16)
            kblk = dk_ref[:, sl] * (1.0 / MLA_Q_FOLD)
            dkb_ref[:, sl] = kblk.astype(BF16)
            dkr = dkr + kblk
        dvb = dv_ref[...].astype(BF16)
        dvb_ref[...] = dvb
        dkr = jnp.where((lane >= 64) & (lane < 96), dkr, 0.0)
        dkr = dkr * cosv + _rope_rot_t(dkr, lane) * sinv
        dql = _dot_nt(dqb_ref[...], wuq_ref[...])
        dkvl = _dot_nt(dkb_ref[...], wuk_ref[...]) + _dot_nt(dvb, wuv_ref[...])
        dcq, dgqr = _rms_bwd(dql, qxh, qr, gq)
        dckv, dgkvr = _rms_bwd(dkvl, kxh, kr_, gkv)
        dlat_ref[:, 0:256] = dcq
        dlat_ref[:, 256:384] = dckv
        dlat_ref[:, K_R_OFF:K_R_OFF + LANES] = pltpu.roll(dkr, 64, 1)
        _acc_rows(dgq_ref, dgqr, first)
        _acc_rows(dgkv_ref, dgkvr, first)

    full = lambda shape: pl.BlockSpec(shape, lambda i: (0, 0))
    rowb = lambda n: pl.BlockSpec((tm, n), lambda i: (i, 0))
    sds = lambda n, dt: jax.ShapeDtypeStruct((s_len, n), dt)
    return pl.pallas_call(
        body, name="mla_prep_bwd", grid=(s_len // tm,),
        in_specs=[rowb(512), full((1, 256)), full((1, 128)), full((256, 1024)), full((128, 1024)),
                  full((128, 512)), rowb(128), rowb(128), rowb(1024), rowb(1024), rowb(512)],
        out_specs=[rowb(512), rowb(1024), rowb(1024), rowb(512), full((1, 256)), full((1, 128))],
        out_shape=[sds(512, F32), sds(1024, BF16), sds(1024, BF16), sds(512, BF16),
                   jax.ShapeDtypeStruct((1, 256), F32), jax.ShapeDtypeStruct((1, 128), F32)],
        compiler_params=_cparams(48, dimension_semantics=("arbitrary",)),
    )(lat, g_q, g_kv, w_uq, w_uk, w_uv, cosf, sinf, dq, dk, dv)


def _in_proj_bwd(x, g, w, dx1, dlat, dsbq, dsbk, dsbv, dgates, ride):
    s_len = x.shape[0]
    tm = min(s_len, 256)
    nb = s_len // tm

    def body(x_ref, g_ref, w_ref, dx1_ref, dlat_ref, dq_ref, dk_ref, dv_ref, dgate_ref, *rest):
        gx_ref, dg_ref = rest[ride.n:ride.n + 2]
        dproj = rest[-1]
        ride.bind(rest[:ride.n], rest[ride.n + 2:2 * ride.n + 2], rest[2 * ride.n + 2:-1])
        pl.when(pl.program_id(0) == 0)(ride.issue)
        dproj[:, 0:512] = dlat_ref[...].astype(BF16)
        dproj[:, 512:1024] = dq_ref[...].astype(BF16)
        dproj[:, 1024:1536] = (dk_ref[...] * LN2).astype(BF16)
        dproj[:, 1536:2048] = dv_ref[...].astype(BF16)
        dproj[:, 2048:4096] = dgate_ref[...]
        dh = _dot_nt(dproj[...], w_ref[...])
        gv = g_ref[...]
        _, xh, r = _rms(x_ref[...], gv)
        dx, dgr = _rms_bwd(dh, xh, r, gv)
        gx_ref[...] = dx1_ref[...] + dx
        _acc_rows(dg_ref, dgr, pl.program_id(0) == 0)
        pl.when(pl.program_id(0) == nb - 1)(ride.finish)

    rowb = lambda n: pl.BlockSpec((tm, n), lambda i: (i, 0))
    full = lambda shape: pl.BlockSpec(shape, lambda i: (0, 0))
    outs = pl.pallas_call(
        body, name="in_proj_bwd", grid=(nb,),
        in_specs=[rowb(D_MODEL), full((1, D_MODEL)), full((D_MODEL, D_IN_PAD)), rowb(D_MODEL),
                  rowb(512), rowb(512), rowb(512), rowb(512), rowb(2 * D_MODEL)] + ride.specs,
        out_specs=[rowb(D_MODEL), full((1, D_MODEL))] + ride.specs,
        out_shape=[jax.ShapeDtypeStruct((s_len, D_MODEL), F32), jax.ShapeDtypeStruct((1, D_MODEL), F32)]
        + ride.out_shape,
        scratch_shapes=ride.scratch + [pltpu.VMEM((tm, D_IN_PAD), BF16)],
        compiler_params=_cparams(48, dimension_semantics=("arbitrary",)),
    )(x, g, w, dx1, dlat, dsbq, dsbk, dsbv, dgates, *ride.srcs)
    return outs[0], outs[1], outs[2:]


def _adamw(landed, w, m, v, name):
    r, c = w.shape
    lanes = _round_up(c, LANES)
    tb = r
    for cand in range(r, 0, -1):
        if r % cand == 0 and (cand % 8 == 0 or cand == r) and N_DEV * cand * lanes * 4 <= ADAM_BLOCK_BYTES:
            tb = cand
            break
    c1 = 1.0 - ADAM_B1 ** ADAM_STEP
    c2 = 1.0 - ADAM_B2 ** ADAM_STEP

    def body(l_ref, w_ref, m_ref, v_ref, g_ref, d_ref, nm_ref, nv_ref):
        g = l_ref[0]
        for k in range(1, N_DEV):
            g = g + l_ref[k]
        nm = ADAM_B1 * m_ref[...] + (1.0 - ADAM_B1) * g
        nv = ADAM_B2 * v_ref[...] + (1.0 - ADAM_B2) * (g * g)
        g_ref[...] = g
        nm_ref[...] = nm
        nv_ref[...] = nv
        d_ref[...] = -ADAM_LR * ((nm / c1) / (jnp.sqrt(nv / c2) + ADAM_EPS) + ADAM_WD * w_ref[...])

    blk = pl.BlockSpec((tb, c), lambda i: (i, 0))
    return pl.pallas_call(
        body, name=name, grid=(r // tb,),
        in_specs=[pl.BlockSpec((N_DEV, tb, c), lambda i: (0, i, 0)), blk, blk, blk],
        out_specs=[blk, blk, blk, blk],
        out_shape=[jax.ShapeDtypeStruct((r, c), F32)] * 4,
        compiler_params=_cparams(dimension_semantics=("parallel",)),
    )(landed, w, m, v)


def _shard_shape(shape, axis):
    return tuple(d // N_DEV if a == axis else d for a, d in enumerate(shape))


def _split_pieces(full, axis):
    r, c = full.shape
    if axis == 0:
        return full.reshape(N_DEV, r // N_DEV, c)
    return full.reshape(r, N_DEV, c // N_DEV).transpose(1, 0, 2)


def _join_shards(gathered, axis):
    _, r, c = gathered.shape
    if axis == 0:
        return gathered.reshape(N_DEV * r, c)
    return gathered.transpose(1, 0, 2).reshape(r, N_DEV * c)


def kernel(x, mem, positions, g_mix, w_in, b_gate, g_q_lat, w_uq, g_kv_lat, w_ukv, w_a_proj, w_b_proj, w_o, g_x, g_mem, w_xq, w_xkv, w_xo, g_ffn, w_gate, w_up, w_down, g_final, loss_target, m_g_mix, m_w_in, m_b_gate, m_g_q_lat, m_w_uq, m_g_kv_lat, m_w_ukv, m_w_a_proj, m_w_b_proj, m_w_o, m_g_x, m_g_mem, m_w_xq, m_w_xkv, m_w_xo, m_g_ffn, m_w_gate, m_w_up, m_w_down, m_g_final, v_g_mix, v_w_in, v_b_gate, v_g_q_lat, v_w_uq, v_g_kv_lat, v_w_ukv, v_w_a_proj, v_w_b_proj, v_w_o, v_g_x, v_g_mem, v_w_xq, v_w_xkv, v_w_xo, v_g_ffn, v_w_gate, v_w_up, v_w_down, v_g_final):
    given = dict(locals())
    s_len = x.shape[1]
    x2d = x.reshape(s_len, D_MODEL)
    mem2d = mem.reshape(-1, D_MODEL)
    target = loss_target.reshape(s_len, D_MODEL)

    names = [name for name, _, _ in SHARDED]
    axis_of = {name: axis for name, _, axis in SHARDED}
    shard2d = lambda name, prefix="": given[prefix + name].reshape(
        _shard_shape(dict((n, s) for n, s, _ in SHARDED)[name], axis_of[name]))

    wire = lambda name: shard2d(name) if name == "b_gate" else shard2d(name).astype(BF16)
    early = [n for n in names if n in NEEDED_FIRST]
    late = [n for n in names if n not in NEEDED_FIRST]
    gathered = _exchange(True, [wire(n) for n in early], "weights_gather_first")
    wts = {n: _join_shards(g, axis_of[n]) for n, g in zip(early, gathered)}

    w_in_p = jnp.concatenate([wts["w_in"][:, :LAT_COLS], jnp.zeros((D_MODEL, D_IN_PAD - D_IN), BF16),
                              wts["w_in"][:, LAT_COLS:]], axis=1)
    w_uq_p = jnp.pad(wts["w_uq"].reshape(256, MLA_HEADS, 96), ((0, 0), (0, 0), (0, 32))).reshape(256, 1024)
    ukv = wts["w_ukv"].reshape(128, MLA_HEADS, 128)
    w_uk_p = jnp.pad(ukv[:, :, :64], ((0, 0), (0, 0), (0, 64))).reshape(128, 1024)
    w_uv = ukv[:, :, 64:].reshape(128, 512)
    w_uv1 = jnp.pad(ukv[:, :, 64:], ((0, 0), (0, 0), (0, 64))).reshape(128, 1024)
    bg = wts["b_gate"]

    inv_freq = ROPE_THETA ** (-jnp.arange(0, MLA_ROPE, 2, dtype=F32) / MLA_ROPE)
    ang = positions.reshape(s_len).astype(F32)[:, None] * inv_freq
    cos16, sin16 = jnp.cos(ang), jnp.sin(ang)
    cosf = jnp.concatenate([jnp.ones((s_len, 64), F32), cos16, cos16, jnp.ones((s_len, 32), F32)], axis=1)
    sinf = jnp.concatenate([jnp.zeros((s_len, 64), F32), sin16, sin16, jnp.zeros((s_len, 32), F32)], axis=1)

    h1, lat, sb, gates = _in_proj(x2d, g_mix, w_in_p)
    qa, ka, va, va1, q_lat, kv_lat = _mla_prep(lat, g_q_lat, g_kv_lat, w_uq_p, w_uk_p, w_uv, w_uv1, cosf, sinf)
    oa, lse, gathered = _mla_fwd(qa, ka, va1, _Exchange(True, [wire(n) for n in late]))
    wts.update({n: _join_shards(g, axis_of[n]) for n, g in zip(late, gathered)})
    ob, sb_r = _sb_fwd(sb)
    x1 = _merge_fwd(x2d, oa, ob, gates, bg, wts["w_a_proj"], wts["w_b_proj"], wts["w_o"])
    mn, xkv = _mem_kv(mem2d, g_mem, wts["w_xkv"])
    x2 = _xattn_fwd(x1, g_x, wts["w_xq"], xkv, wts["w_xo"])
    x3, hf = _ffn_fwd(x2, g_ffn, wts["w_gate"], wts["w_up"], wts["w_down"])
    g_final2d = g_final.reshape(1, D_MODEL)
    sse, dx3, dx3b, dg_final = _loss_head(x3, g_final2d, target)
    loss = lax.psum(sse[0, 0] * (0.5 / D_MODEL), ("x", "y", "c"))

    dx2, dgt, dup, act, dg_ffn = _ffn_bwd(x2, hf, dx3, dx3b, g_ffn, wts["w_gate"], wts["w_up"], wts["w_down"])
    dx1, dw_xq, dw_xo, dxkv, dg_x = _xattn_bwd(x1, dx2, g_x, wts["w_xq"], xkv, wts["w_xo"])
    dw_xkv, dg_mem = _mem_bwd(mem2d, g_mem, wts["w_xkv"], mn, dxkv)
    doa, dob, dgates, dpa, dpb, merged, dx1b, dbg = _merge_bwd(
        dx1, oa, ob, gates, bg, wts["w_a_proj"], wts["w_b_proj"], wts["w_o"])
    dsbq, dsbk, dsbv = _sb_bwd(sb, dob, sb_r)
    full_grads = {
        "w_a_proj": _tn_matmul(oa, dpa, "dw_a"),
        "w_b_proj": _tn_matmul(ob, dpb, "dw_b"),
        "w_o": _tn_matmul(merged, dx1b, "dw_o"),
        "w_xq": dw_xq,
        "w_xkv": dw_xkv,
        "w_xo": dw_xo,
        "w_gate": _tn_matmul(hf, dgt, "dw_gate", tn=FF_TILE),
        "w_up": _tn_matmul(hf, dup, "dw_up", tn=FF_TILE),
        "w_down": _tn_matmul(act, dx3b, "dw_down", tka=FF_TILE),
    }
    dw_sb = _tn_matmul_sb(h1, dsbq, dsbk, dsbv, "dw_in_sb")
    own0 = D_IN // N_DEV - LAT_COLS
    w_in_rest = _split_pieces(
        jnp.concatenate([dw_sb[:, :LAT_COLS], dw_sb, _tn_matmul(h1, dgates, "dw_in_gates")], axis=1), 1)
    dqa, dka, dva, got = _mla_bwd(
        qa, ka, va, oa, doa, lse,
        _Exchange(False, [_split_pieces(full_grads[n], axis_of[n]) for n in late] + [w_in_rest]))
    landed = dict(zip(late, got[:-1]))
    dlat, dqb, dkb, dvb, dg_q, dg_kv = _mla_prep_bwd(
        lat, g_q_lat, g_kv_lat, w_uq_p, w_uk_p, w_uv, cosf, sinf, dqa, dka, dva)
    w_in_first = jnp.concatenate([_tn_matmul(h1, dlat, "dw_in_lat")[:, :LAT_COLS], dw_sb[:, :own0]], axis=1)
    dw_uq_p = _tn_matmul(q_lat, dqb, "dw_uq")
    dw_uk_p = _tn_matmul(kv_lat, dkb, "dw_uk")
    dw_uv = _tn_matmul(kv_lat, dvb, "dw_uv")
    full_grads.update({
        "b_gate": dbg,
        "w_uq": dw_uq_p.reshape(256, MLA_HEADS, 128)[:, :, :96].reshape(256, 768),
        "w_ukv": jnp.concatenate([dw_uk_p.reshape(128, MLA_HEADS, 128)[:, :, :64],
                                  dw_uv.reshape(128, MLA_HEADS, 64)], axis=2).reshape(128, 1024),
    })
    small = [n for n in early if n != "w_in"]
    grad_x, dg_mix, got2 = _in_proj_bwd(
        x2d, g_mix, w_in_p, dx1, dlat, dsbq, dsbk, dsbv, dgates,
        _Exchange(False, [_split_pieces(full_grads[n], axis_of[n]) for n in small], to_first=[w_in_first]))
    landed.update(zip(small, got2[:-1]))
    me = 4 * lax.axis_index("x") + 2 * lax.axis_index("y") + lax.axis_index("c")
    landed["w_in"] = jnp.where(me == 0, got2[-1], got[-1])
    rep_grads = {"g_mix": dg_mix, "g_q_lat": dg_q, "g_kv_lat": dg_kv, "g_x": dg_x, "g_mem": dg_mem,
                 "g_ffn": dg_ffn, "g_final": dg_final}
    rep_cat = lambda prefix, src: jnp.concatenate(
        [src[prefix + n].reshape(-1) for n, _ in REPLICATED]).reshape(-1, LANES)
    rep_src = jnp.broadcast_to(rep_cat("", rep_grads), (N_DEV,) + rep_cat("", rep_grads).shape)
    rep_landed = _exchange(False, [rep_src], "grads_gains")[0]

    res = {}
    for name, _, _ in SHARDED:
        outs = _adamw(landed[name], shard2d(name), shard2d(name, "m_"), shard2d(name, "v_"), "adamw_" + name)
        res[name] = [o.reshape(given[name].shape) for o in outs]
    rep_outs = _adamw(rep_landed, rep_cat("", given), rep_cat("m_", given), rep_cat("v_", given), "adamw_gains")
    off = 0
    for name, n in REPLICATED:
        res[name] = [o.reshape(-1)[off:off + n].reshape(given[name].shape) for o in rep_outs]
        off += n
    result = [loss, grad_x.reshape(x.shape)]
    for k in range(4):
        result.extend(res[name][k] for name in WEIGHT_ORDER)
    return tuple(result)
```

```python
import functools
import math

import jax
import jax.numpy as jnp
from jax import lax
from jax.experimental import pallas as pl
from jax.experimental.pallas import tpu as pltpu

F32 = jnp.float32
BF16 = jnp.bfloat16

D_MODEL = 1024
MLA_HEADS = 8
MLA_Q_RANK = 256
MLA_KV_RANK = 128
MLA_NOPE = 64
MLA_ROPE = 32
ROPE_THETA = 10000.0
SB_WIDTH = 512
X_HEADS = 4
X_HEAD_DIM = 128
D_FF = 2816
EPS = 1e-6
D_IN = 4000
D_IN_PAD = 4096
K_R_OFF = 384
LAT_COLS = 416
LANES = 128
HEAD_PAD = 128
MLA_SCALE = 1.0 / math.sqrt(MLA_NOPE + MLA_ROPE)
SB_SCALE = 0.125
LOG2E = math.log2(math.e)
LN2 = math.log(2.0)
MLA_Q_FOLD = MLA_SCALE * LOG2E
SB_Q_FOLD = SB_SCALE * LOG2E
SB_CUT = -160.0
X_SCALE = 1.0 / math.sqrt(X_HEAD_DIM)
NEG_BIG = -1e30

ADAM_LR = 0.001
ADAM_B1 = 0.9
ADAM_B2 = 0.999
ADAM_EPS = 1e-08
ADAM_WD = 0.01
ADAM_STEP = 10

N_DEV = 8
MIB = 1024 * 1024
ADAM_BLOCK_BYTES = 4 * MIB

SHARDED = (
    ("w_in", (D_MODEL, D_IN), 1),
    ("b_gate", (2, D_MODEL), 1),
    ("w_uq", (MLA_Q_RANK, 768), 1),
    ("w_ukv", (MLA_KV_RANK, 1024), 1),
    ("w_a_proj", (512, D_MODEL), 1),
    ("w_b_proj", (512, D_MODEL), 1),
    ("w_o", (D_MODEL, D_MODEL), 0),
    ("w_xq", (D_MODEL, 512), 0),
    ("w_xkv", (D_MODEL, 1024), 0),
    ("w_xo", (512, D_MODEL), 1),
    ("w_gate", (D_MODEL, D_FF), 1),
    ("w_up", (D_MODEL, D_FF), 1),
    ("w_down", (D_FF, D_MODEL), 0),
)
NEEDED_FIRST = ("w_in", "b_gate", "w_uq", "w_ukv")
REPLICATED = (
    ("g_mix", 1024), ("g_q_lat", 256), ("g_kv_lat", 128), ("g_x", 1024),
    ("g_mem", 1024), ("g_ffn", 1024), ("g_final", 1024),
)
WEIGHT_ORDER = ("g_mix", "w_in", "b_gate", "g_q_lat", "w_uq", "g_kv_lat", "w_ukv", "w_a_proj",
                "w_b_proj", "w_o", "g_x", "g_mem", "w_xq", "w_xkv", "w_xo", "g_ffn", "w_gate",
                "w_up", "w_down", "g_final")


def _round_up(n, m):
    return -(-n // m) * m


def _cparams(vmem_mib=None, **kw):
    if vmem_mib is not None:
        kw["vmem_limit_bytes"] = vmem_mib * MIB
    return pltpu.CompilerParams(**kw)


def _dot(a, b):
    return jnp.dot(a, b, preferred_element_type=F32)


def _dot_nt(a, b):
    return lax.dot_general(a, b, (((1,), (1,)), ((), ())), preferred_element_type=F32)


def _dot_tn(a, b):
    return lax.dot_general(a, b, (((0,), (0,)), ((), ())), preferred_element_type=F32)


def _rms(x, g):
    r = lax.rsqrt(jnp.mean(x * x, axis=-1, keepdims=True) + EPS)
    xh = x * r
    return xh * g, xh, r


def _rms_bwd(dy, xh, r, g):
    u = dy * g
    dx = r * (u - xh * jnp.mean(u * xh, axis=-1, keepdims=True))
    return dx, dy * xh


def _sigmoid(z):
    return 1.0 / (1.0 + jnp.exp(-z))


def _acc_rows(ref, val, first):
    s = jnp.sum(val, axis=0, keepdims=True)

    @pl.when(first)
    def _():
        ref[...] = s

    @pl.when(jnp.logical_not(first))
    def _():
        ref[...] += s


def _acc(ref, val, first):
    @pl.when(first)
    def _():
        ref[...] = val

    @pl.when(jnp.logical_not(first))
    def _():
        ref[...] += val


def _peer(k):
    x, y, c = lax.axis_index("x"), lax.axis_index("y"), lax.axis_index("c")
    px = 1 - x if (k >> 2) & 1 else x
    py = 1 - y if (k >> 1) & 1 else y
    pc = 1 - c if k & 1 else c
    return (px, py, pc), 4 * px + 2 * py + pc


N_PEERS = N_DEV - 1
OTHER_CHIPS = (2, 4, 6)


def _land_shape(gather, src):
    return (N_DEV,) + src.shape if gather else src.shape


class _Exchange:
    def __init__(self, gather, srcs, to_first=()):
        self.gather, self.m, self.srcs = gather, len(srcs), list(srcs) + list(to_first)
        self.n = len(self.srcs)
        self.out_shape = ([jax.ShapeDtypeStruct(_land_shape(gather, s), s.dtype) for s in srcs]
                          + [jax.ShapeDtypeStruct(_land_shape(True, s), s.dtype) for s in to_first])
        self.specs = [pl.BlockSpec(memory_space=pl.ANY)] * self.n
        self.scratch = [pltpu.SemaphoreType.DMA((self.n * N_PEERS,)), pltpu.SemaphoreType.DMA((self.n * N_PEERS,)),
                        pltpu.SemaphoreType.DMA((self.n,))]

    def bind(self, src, land, sems):
        self.src, self.land = src, land
        self.send_sems, self.recv_sems, self.local_sems = sems

    def _copy(self, a, k, source, to, target=1):
        return pltpu.make_async_remote_copy(
            src_ref=source, dst_ref=to,
            send_sem=self.send_sems.at[a * N_PEERS + k - 1], recv_sem=self.recv_sems.at[a * N_PEERS + k - 1],
            device_id=_peer(target)[0], device_id_type=pl.DeviceIdType.MESH)

    def _row(self, a, k):
        return self.land[a].at[_peer(k)[1]]

    def _mine(self, a):
        me = _peer(0)[1]
        whole = self.gather or a >= self.m
        return pltpu.make_async_copy(self.src[a] if whole else self.src[a].at[me], self.land[a].at[me],
                                     self.local_sems.at[a])

    def issue(self):
        me = _peer(0)[1]
        for a in range(self.m):
            self._mine(a).start()
            for k in ((1,) + OTHER_CHIPS if self.gather else range(1, N_DEV)):
                source = self.src[a] if self.gather else self.src[a].at[_peer(k)[1]]
                self._copy(a, k, source, self.land[a].at[me], target=k).start()
        for a in range(self.m, self.n):
            pl.when(me == 0)(self._mine(a).start)
            for k in range(1, N_DEV):
                pl.when(me == k)(self._copy(a, k, self.src[a], self.land[a].at[me], target=k).start)

    def finish(self):
        me = _peer(0)[1]
        part = lambda a: self.src[a] if self.gather or a >= self.m else self.src[a].at[me]
        if self.gather:
            for a in range(self.m):
                for k in OTHER_CHIPS:
                    self._copy(a, k, part(a), self._row(a, k)).wait_recv()
                    self._copy(a, k + 1, self._row(a, k), self._row(a, k), target=1).start()
        for a in range(self.m):
            for k in ((1, 3, 5, 7) if self.gather else range(1, N_DEV)):
                self._copy(a, k, part(a), self._row(a, k)).wait_recv()
        for a in range(self.m):
            for k in range(1, N_DEV):
                self._copy(a, k, part(a), self.land[a].at[me]).wait_send()
            self._mine(a).wait()
        for a in range(self.m, self.n):
            for k in range(1, N_DEV):
                pl.when(me == 0)(self._copy(a, k, part(a), self._row(a, k)).wait_recv)
                pl.when(me == k)(self._copy(a, k, part(a), self.land[a].at[me]).wait_send)
            pl.when(me == 0)(self._mine(a).wait)


def _exchange(gather, srcs, name):
    ex = _Exchange(gather, srcs)

    def body(*refs):
        ex.bind(refs[:ex.n], refs[ex.n:2 * ex.n], refs[2 * ex.n:])
        ex.issue()
        ex.finish()

    return pl.pallas_call(body, name=name, out_shape=ex.out_shape, in_specs=ex.specs, out_specs=ex.specs,
                          scratch_shapes=ex.scratch)(*ex.srcs)


def _tn_matmul(a, b, name, tka=512, tn=1024, ts=2048):
    s_len, ka = a.shape
    n = b.shape[1]
    tka, tn, ts = min(tka, ka), min(tn, n), min(ts, s_len)
    assert ka % tka == 0 and n % tn == 0 and s_len % ts == 0

    def body(a_ref, b_ref, o_ref):
        _acc(o_ref, _dot_tn(a_ref[...], b_ref[...].astype(BF16)), pl.program_id(2) == 0)

    return pl.pallas_call(
        body, name=name, grid=(ka // tka, n // tn, s_len // ts),
        in_specs=[pl.BlockSpec((ts, tka), lambda i, j, s: (s, i)),
                  pl.BlockSpec((ts, tn), lambda i, j, s: (s, j))],
        out_specs=pl.BlockSpec((tka, tn), lambda i, j, s: (i, j)),
        out_shape=jax.ShapeDtypeStruct((ka, n), F32),
        compiler_params=_cparams(dimension_semantics=("parallel", "parallel", "arbitrary")),
    )(a, b)


def _tn_matmul_sb(a, dq, dk, dv, name, tka=512, ts=2048):
    s_len, ka = a.shape
    tka, ts = min(tka, ka), min(ts, s_len)
    assert ka % tka == 0 and s_len % ts == 0

    def body(a_ref, q_ref, k_ref, v_ref, o_ref):
        first = pl.program_id(1) == 0
        av = a_ref[...]
        for c, val in enumerate((q_ref[...], k_ref[...] * LN2, v_ref[...])):
            _acc(o_ref.at[:, c * SB_WIDTH:(c + 1) * SB_WIDTH], _dot_tn(av, val.astype(BF16)), first)

    colb = pl.BlockSpec((ts, SB_WIDTH), lambda i, s: (s, 0))
    return pl.pallas_call(
        body, name=name, grid=(ka // tka, s_len // ts),
        in_specs=[pl.BlockSpec((ts, tka), lambda i, s: (s, i)), colb, colb, colb],
        out_specs=pl.BlockSpec((tka, 3 * SB_WIDTH), lambda i, s: (i, 0)),
        out_shape=jax.ShapeDtypeStruct((ka, 3 * SB_WIDTH), F32),
        compiler_params=_cparams(48, dimension_semantics=("parallel", "arbitrary")),
    )(a, dq, dk, dv)


def _row_block(s_len):
    return min(s_len, 512)


def _in_proj(x, g, w):
    s_len = x.shape[0]
    tm = _row_block(s_len)

    def body(x_ref, g_ref, w_ref, h_ref, lat_ref, sb_ref, gate_ref):
        h, _, _ = _rms(x_ref[...], g_ref[...])
        hb = h.astype(BF16)
        h_ref[...] = hb
        p = _dot(hb, w_ref[:, 0:1024])
        lat_ref[...] = p[:, 0:512]
        sb_ref[:, 0:512] = (p[:, 512:1024] * SB_Q_FOLD).astype(BF16)
        sb_ref[:, 512:1536] = _dot(hb, w_ref[:, 1024:2048]).astype(BF16)
        gate_ref[:, 0:1024] = _dot(hb, w_ref[:, 2048:3072])
        gate_ref[:, 1024:2048] = _dot(hb, w_ref[:, 3072:4096])

    rowb = lambda n: pl.BlockSpec((tm, n), lambda i: (i, 0))
    return pl.pallas_call(
        body, name="in_proj", grid=(s_len // tm,),
        in_specs=[rowb(D_MODEL), pl.BlockSpec((1, D_MODEL), lambda i: (0, 0)),
                  pl.BlockSpec((D_MODEL, D_IN_PAD), lambda i: (0, 0))],
        out_specs=[rowb(D_MODEL), rowb(512), rowb(3 * SB_WIDTH), rowb(2 * D_MODEL)],
        out_shape=[jax.ShapeDtypeStruct((s_len, D_MODEL), BF16),
                   jax.ShapeDtypeStruct((s_len, 512), F32),
                   jax.ShapeDtypeStruct((s_len, 3 * SB_WIDTH), BF16),
                   jax.ShapeDtypeStruct((s_len, 2 * D_MODEL), F32)],
        compiler_params=_cparams(48, dimension_semantics=("parallel",)),
    )(x, g, w)


def _rope_rot(blk, lane):
    return jnp.where(lane < 80, -pltpu.roll(blk, 112, 1), pltpu.roll(blk, 16, 1))


def _rope_rot_t(blk, lane):
    return jnp.where(lane < 80, pltpu.roll(blk, 112, 1), -pltpu.roll(blk, 16, 1))


def _mla_prep(lat, g_q, g_kv, w_uq, w_uk, w_uv, w_uv1, cosf, sinf):
    s_len = lat.shape[0]
    tm = _row_block(s_len)

    def body(lat_ref, gq_ref, gkv_ref, wuq_ref, wuk_ref, wuv_ref, wuv1_ref, cos_ref, sin_ref,
             q_ref, k_ref, v_ref, v1_ref, ql_ref, kvl_ref):
        lane = lax.broadcasted_iota(jnp.int32, (tm, LANES), 1)
        cosv, sinv = cos_ref[...], sin_ref[...]
        ql, _, _ = _rms(lat_ref[:, 0:256], gq_ref[...])
        kvl, _, _ = _rms(lat_ref[:, 256:384], gkv_ref[...])
        qlb, kvlb = ql.astype(BF16), kvl.astype(BF16)
        ql_ref[...] = qlb
        kvl_ref[...] = kvlb
        q = _dot(qlb, wuq_ref[...])
        kn = _dot(kvlb, wuk_ref[...])
        v_ref[...] = _dot(kvlb, wuv_ref[...]).astype(BF16)
        v1 = _dot(kvlb, wuv1_ref[...])
        wide = lax.broadcasted_iota(jnp.int32, (tm, MLA_HEADS * HEAD_PAD), 1)
        v1_ref[...] = jnp.where(wide % HEAD_PAD == MLA_NOPE, 1.0, v1).astype(BF16)
        kr = pltpu.roll(lat_ref[:, K_R_OFF:K_R_OFF + LANES], 64, 1)
        kr = kr * cosv + _rope_rot(kr, lane) * sinv
        for h in range(MLA_HEADS):
            sl = slice(h * HEAD_PAD, (h + 1) * HEAD_PAD)
            blk = q[:, sl]
            q_ref[:, sl] = ((blk * cosv + _rope_rot(blk, lane) * sinv) * MLA_Q_FOLD).astype(BF16)
            k_ref[:, sl] = (kn[:, sl] + kr).astype(BF16)

    full = lambda shape: pl.BlockSpec(shape, lambda i: (0, 0))
    rowb = lambda n: pl.BlockSpec((tm, n), lambda i: (i, 0))
    return pl.pallas_call(
        body, name="mla_prep", grid=(s_len // tm,),
        in_specs=[rowb(512), full((1, 256)), full((1, 128)), full((256, 1024)), full((128, 1024)),
                  full((128, 512)), full((128, 1024)), rowb(128), rowb(128)],
        out_specs=[rowb(1024), rowb(1024), rowb(512), rowb(1024), rowb(256), rowb(128)],
        out_shape=[jax.ShapeDtypeStruct((s_len, 1024), BF16), jax.ShapeDtypeStruct((s_len, 1024), BF16),
                   jax.ShapeDtypeStruct((s_len, 512), BF16), jax.ShapeDtypeStruct((s_len, 1024), BF16),
                   jax.ShapeDtypeStruct((s_len, 256), BF16), jax.ShapeDtypeStruct((s_len, 128), BF16)],
        compiler_params=_cparams(dimension_semantics=("parallel",)),
    )(lat, g_q, g_kv, w_uq, w_uk, w_uv, w_uv1, cosf, sinf)


ATTN_TQ = 1024
ATTN_TH = 512
ATTN_TK = 256
SB_TQ = 512
SB_TH = 256
MLA_TK = 512
MLA_FWD_TK = 1024


def _attn_blocks(s_len, tk=ATTN_TK, tq=ATTN_TQ, th=ATTN_TH):
    tq, th, tk = min(s_len, tq), min(s_len, th), min(s_len, tk)
    return tq, th, tk, tq // tk


def _chains(tq, th):
    return [(hh, r0) for hh in range(2) for r0 in range(0, tq, th)]


def _diag_mask(th, tk, r0, sub, strict):
    lo, hi = sub * tk, (sub + 1) * tk - 1
    last, first = r0 + th - 1, r0
    if (lo >= last) if strict else (lo > last):
        return "none"
    if (hi < first) if strict else (hi <= first):
        return "all"
    row = lax.broadcasted_iota(jnp.int32, (th, tk), 0) + r0
    col = lax.broadcasted_iota(jnp.int32, (th, tk), 1) + lo
    return col < row if strict else col <= row


def _mla_fwd(q, k, v, ride):
    s_len = q.shape[0]
    tq, th, tk, nsub = _attn_blocks(s_len, MLA_FWD_TK)
    chains = _chains(tq, th)
    nh = tq // th
    nq = s_len // tq

    def body(q_ref, k_ref, v_ref, *rest):
        o_ref, lse_ref = rest[ride.n:ride.n + 2]
        ride.bind(rest[:ride.n], rest[ride.n + 2:2 * ride.n + 2], rest[2 * ride.n + 2:])
        pl.when((pl.program_id(0) == 0) & (pl.program_id(1) == 0))(ride.issue)
        i = pl.program_id(1)
        lane = lax.broadcasted_iota(jnp.int32, (th, LANES), 1)
        hsl = [slice(hh * HEAD_PAD, (hh + 1) * HEAD_PAD) for hh in range(2)]

        def step(kb, carry, sub):
            rows = pl.ds(pl.multiple_of(kb * tk, tk), tk)
            masks = ["all" if sub is None else _diag_mask(th, tk, r0, sub, strict=False) for _, r0 in chains]
            live = [n for n, m in enumerate(masks) if not (isinstance(m, str) and m == "none")]
            s = {n: _dot_nt(q_ref[chains[n][1]:chains[n][1] + th, hsl[chains[n][0]]], k_ref[rows, hsl[chains[n][0]]])
                 for n in live}
            new = list(carry)
            pb, alpha = {}, {}
            for n in live:
                m = carry[n][0]
                sn = s[n]
                if not isinstance(masks[n], str):
                    sn = jnp.where(masks[n], sn, NEG_BIG)
                m_new = jnp.maximum(m, jnp.max(sn, axis=-1, keepdims=True))
                alpha[n] = jnp.exp2(m - m_new)
                pb[n] = jnp.exp2(sn - m_new).astype(BF16)
                new[n] = (m_new, None)
            pv = {n: _dot(pb[n], v_ref[rows, hsl[chains[n][0]]]) for n in live}
            for n in live:
                new[n] = (new[n][0], alpha[n] * carry[n][1] + pv[n])
            return tuple(new)

        init = (jnp.full((th, 1), NEG_BIG, F32), jnp.zeros((th, LANES), F32))
        carry = lax.fori_loop(0, i * nsub, lambda kb, cy: step(kb, cy, None), (init,) * len(chains))
        for sub in range(nsub):
            carry = step(i * nsub + sub, carry, sub)
        for c in range(nh):
            (m0, a0), (m1, a1) = carry[c], carry[nh + c]
            l0, l1 = a0[:, MLA_NOPE:MLA_NOPE + 1], a1[:, MLA_NOPE:MLA_NOPE + 1]
            rs = slice(c * th, (c + 1) * th)
            o_ref[rs, :] = jnp.where(lane < 64, a0 / l0, pltpu.roll(a1 / l1, 64, 1)).astype(BF16)
            lse_ref[rs, :] = jnp.where(lane < 64, m0 + jnp.log2(l0), m1 + jnp.log2(l1))
        pl.when((pl.program_id(0) == 3) & (i == nq - 1))(ride.finish)

    outs = pl.pallas_call(
        body, name="mla_fwd", grid=(4, nq),
        in_specs=[pl.BlockSpec((tq, 2 * HEAD_PAD), lambda p, i: (i, p)),
                  pl.BlockSpec((s_len, 2 * HEAD_PAD), lambda p, i: (0, p)),
                  pl.BlockSpec((s_len, 2 * HEAD_PAD), lambda p, i: (0, p))] + ride.specs,
        out_specs=[pl.BlockSpec((tq, LANES), lambda p, i: (i, p)),
                   pl.BlockSpec((None, tq, LANES), lambda p, i: (p, i, 0))] + ride.specs,
        out_shape=[jax.ShapeDtypeStruct((s_len, 512), BF16),
                   jax.ShapeDtypeStruct((4, s_len, LANES), F32)] + ride.out_shape,
        scratch_shapes=ride.scratch,
        compiler_params=_cparams(40, dimension_semantics=("arbitrary", "arbitrary")),
    )(q, k, v, *ride.srcs)
    return outs[0], outs[1], outs[2:]


def _mla_bwd(q, k, v, o, do, lse, ride):
    s_len = q.shape[0]
    tq, th, tk, nsub = _attn_blocks(s_len, MLA_TK)
    chains = _chains(tq, th)
    nq = s_len // tq

    def body(q_ref, k_ref, v_ref, o_ref, do_ref, lse_ref, *rest):
        dq_ref, dk_ref, dv_ref = rest[ride.n:ride.n + 3]
        ride.bind(rest[:ride.n], rest[ride.n + 3:2 * ride.n + 3], rest[2 * ride.n + 3:])
        pl.when((pl.program_id(0) == 0) & (pl.program_id(1) == 0))(ride.issue)
        i = pl.program_id(1)
        lane = lax.broadcasted_iota(jnp.int32, (th, LANES), 1)

        @pl.when(i == 0)
        def _():
            dk_ref[...] = jnp.zeros_like(dk_ref)
            dv_ref[...] = jnp.zeros_like(dv_ref)

        hsl = [slice(hh * HEAD_PAD, (hh + 1) * HEAD_PAD) for hh in range(2)]
        qs, dos, deltas, lses = [], [], [], []
        for hh, r0 in chains:
            rs = slice(r0, r0 + th)
            qs.append(q_ref[rs, hsl[hh]])
            doh = jnp.where((lane // 64) == hh, do_ref[rs, :], jnp.zeros((), BF16))
            dos.append(doh)
            deltas.append(jnp.sum(doh.astype(F32) * o_ref[rs, :].astype(F32), axis=-1, keepdims=True))
            lses.append(lse_ref[rs, 64 * hh:64 * hh + 1])

        def step(kb, dqs, sub):
            rows = pl.ds(pl.multiple_of(kb * tk, tk), tk)
            vblk = v_ref[rows, :]
            new, p_all, do_all = [], [], []
            ds_h, q_h = [[], []], [[], []]
            for c, (hh, r0) in enumerate(chains):
                mask = "all" if sub is None else _diag_mask(th, tk, r0, sub, strict=False)
                if isinstance(mask, str) and mask == "none":
                    new.append(dqs[c])
                    continue
                kblk = k_ref[rows, hsl[hh]]
                s = _dot_nt(qs[c], kblk)
                if not isinstance(mask, str):
                    s = jnp.where(mask, s, NEG_BIG)
                p = jnp.exp2(s - lses[c])
                dp = _dot_nt(dos[c], vblk)
                ds = (p * (dp - deltas[c]) * MLA_SCALE).astype(BF16)
                p_all.append(p.astype(BF16))
                do_all.append(dos[c])
                ds_h[hh].append(ds)
                q_h[hh].append(qs[c])
                new.append(dqs[c] + _dot(ds, kblk))
            dv_ref[rows, :] += _dot_tn(jnp.concatenate(p_all, axis=0), jnp.concatenate(do_all, axis=0))
            for hh in range(2):
                dk_ref[rows, hsl[hh]] += _dot_tn(jnp.concatenate(ds_h[hh], axis=0),
                                                 jnp.concatenate(q_h[hh], axis=0))
            return tuple(new)

        zero = jnp.zeros((th, LANES), F32)
        dqs = lax.fori_loop(0, i * nsub, lambda kb, cy: step(kb, cy, None), (zero,) * len(chains))
        for sub in range(nsub):
            dqs = step(i * nsub + sub, dqs, sub)
        for c, (hh, r0) in enumerate(chains):
            dq_ref[r0:r0 + th, hsl[hh]] = dqs[c]
        pl.when((pl.program_id(0) == 3) & (i == nq - 1))(ride.finish)

    outs = pl.pallas_call(
        body, name="mla_bwd", grid=(4, nq),
        in_specs=[pl.BlockSpec((tq, 2 * HEAD_PAD), lambda p, i: (i, p)),
                  pl.BlockSpec((s_len, 2 * HEAD_PAD), lambda p, i: (0, p)),
                  pl.BlockSpec((s_len, LANES), lambda p, i: (0, p)),
                  pl.BlockSpec((tq, LANES), lambda p, i: (i, p)),
                  pl.BlockSpec((tq, LANES), lambda p, i: (i, p)),
                  pl.BlockSpec((None, tq, LANES), lambda p, i: (p, i, 0))] + ride.specs,
        out_specs=[pl.BlockSpec((tq, 2 * HEAD_PAD), lambda p, i: (i, p)),
                   pl.BlockSpec((s_len, 2 * HEAD_PAD), lambda p, i: (0, p)),
                   pl.BlockSpec((s_len, LANES), lambda p, i: (0, p))] + ride.specs,
        out_shape=[jax.ShapeDtypeStruct((s_len, 1024), F32), jax.ShapeDtypeStruct((s_len, 1024), F32),
                   jax.ShapeDtypeStruct((s_len, 512), F32)] + ride.out_shape,
        scratch_shapes=ride.scratch,
        compiler_params=_cparams(56, dimension_semantics=("arbitrary", "arbitrary")),
    )(q, k, v, o, do, lse, *ride.srcs)
    return outs[0], outs[1], outs[2], outs[3:]


def _log_sigmoids(z2):
    sp = jnp.log2(1.0 + jnp.exp2(-jnp.abs(z2)))
    lb = jnp.minimum(z2, 0.0) - sp
    return lb, lb - z2


def _split_dot(x, w, parts, nt=False):
    dot = _dot_nt if nt else _dot
    out = None
    for _ in range(parts):
        xb = x.astype(BF16)
        t = dot(xb, w)
        out = t if out is None else out + t
        x = x - xb.astype(F32)
    return out


def _sb_fwd(sb):
    s_len = sb.shape[0]
    tq, th, tk, nsub = _attn_blocks(s_len, tq=SB_TQ, th=SB_TH)
    chains = _chains(tq, th)
    nh = tq // th
    assert s_len // tk <= 64

    def body(q_ref, k_ref, v_ref, o_ref, r_ref):
        i = pl.program_id(1)
        lane = lax.broadcasted_iota(jnp.int32, (th, LANES), 1)
        upper = (lax.broadcasted_iota(jnp.int32, (tk, tk), 0)
                 > lax.broadcasted_iota(jnp.int32, (tk, tk), 1)).astype(BF16)
        qs = [jnp.where((lane // 64) == hh, q_ref[r0:r0 + th, :], jnp.zeros((), BF16)) for hh, r0 in chains]

        def step(kb, carry, sub):
            rows = pl.ds(pl.multiple_of(kb * tk, tk), tk)
            kblk, vblk = k_ref[rows, :], v_ref[rows, :]
            masks = ["all" if sub is None else _diag_mask(th, tk, r0, sub, strict=True) for _, r0 in chains]
            live = [n for n, m in enumerate(masks) if not (isinstance(m, str) and m == "none")]
            masked = {n: not isinstance(masks[n], str) for n in live}
            z = {n: _dot_nt(qs[n], kblk) for n in live}
            lb, lom = {}, {}
            for n in live:
                lb[n], lom[n] = _log_sigmoids(z[n])
                if masked[n]:
                    lom[n] = jnp.where(masks[n], lom[n], 0.0)
            suf = {n: _split_dot(lom[n], upper, 1) for n in live}
            a = {}
            for n in live:
                a[n] = jnp.exp2(lb[n] + suf[n] + carry[n][0])
                if masked[n]:
                    a[n] = jnp.where(masks[n], a[n], 0.0)
            pv = {n: _dot(a[n].astype(BF16), vblk) for n in live}
            new = list(carry)
            for n in live:
                c, acc, r = carry[n]
                rs = suf[n][:, 0:1] + lom[n][:, 0:1]
                new[n] = (c + rs, acc + pv[n], jnp.where(lane == 64 * chains[n][0] + kb, rs, r))
            return tuple(new)

        init = (jnp.zeros((th, 1), F32), jnp.zeros((th, LANES), F32), jnp.zeros((th, LANES), F32))
        carry = (init,) * len(chains)
        for sub in reversed(range(nsub)):
            carry = step(i * nsub + sub, carry, sub)

        def spent(cy):
            top = functools.reduce(jnp.maximum, [jnp.max(c) for c, _, _ in cy])
            return (top < SB_CUT).astype(jnp.int32)

        def walk(state):
            t, _, cy = state
            cy = step(i * nsub - 1 - t, cy, None)
            return t + 1, spent(cy), cy

        _, _, carry = lax.while_loop(lambda st: (st[0] < i * nsub) & (st[1] == 0), walk,
                                     (jnp.int32(0), spent(carry), carry))
        for n in range(nh):
            rs = slice(n * th, (n + 1) * th)
            o_ref[rs, :] = jnp.where(lane < 64, carry[n][1], carry[nh + n][1]).astype(BF16)
            r_ref[rs, :] = jnp.where(lane < 64, carry[n][2], carry[nh + n][2])

    return pl.pallas_call(
        body, name="sb_fwd", grid=(4, s_len // tq),
        in_specs=[pl.BlockSpec((tq, LANES), lambda p, i: (i, p)),
                  pl.BlockSpec((s_len, LANES), lambda p, i: (0, 4 + p)),
                  pl.BlockSpec((s_len, LANES), lambda p, i: (0, 8 + p))],
        out_specs=[pl.BlockSpec((tq, LANES), lambda p, i: (i, p)),
                   pl.BlockSpec((None, tq, LANES), lambda p, i: (p, i, 0))],
        out_shape=[jax.ShapeDtypeStruct((s_len, 512), BF16), jax.ShapeDtypeStruct((4, s_len, LANES), F32)],
        compiler_params=_cparams(40, dimension_semantics=("parallel", "arbitrary")),
    )(sb, sb, sb)


def _sb_bwd(sb, do, r):
    s_len = sb.shape[0]
    tq, th, tk, nsub = _attn_blocks(s_len)
    chains = _chains(tq, th)
    nh = tq // th

    def body(q_ref, k_ref, v_ref, do_ref, r_ref, dq_ref, dk_ref, dv_ref):
        i = pl.program_id(1)
        lane = lax.broadcasted_iota(jnp.int32, (th, LANES), 1)
        upper = (lax.broadcasted_iota(jnp.int32, (tk, tk), 0)
                 > lax.broadcasted_iota(jnp.int32, (tk, tk), 1)).astype(BF16)
        tri = (lax.broadcasted_iota(jnp.int32, (LANES, LANES), 0)
               > lax.broadcasted_iota(jnp.int32, (LANES, LANES), 1)).astype(BF16)

        @pl.when(i == 0)
        def _():
            dk_ref[...] = jnp.zeros_like(dk_ref)
            dv_ref[...] = jnp.zeros_like(dv_ref)

        qs, dos, rights = [], [], []
        for hh, r0 in chains:
            rs = slice(r0, r0 + th)
            hm = (lane // 64) == hh
            qs.append(jnp.where(hm, q_ref[rs, :], jnp.zeros((), BF16)))
            dos.append(jnp.where(hm, do_ref[rs, :], jnp.zeros((), BF16)))
            rights.append(_split_dot(jnp.where(hm, r_ref[rs, :], 0.0), tri, 3))

        def step(kb, carry, sub):
            rows = pl.ds(pl.multiple_of(kb * tk, tk), tk)
            kblk, vblk = k_ref[rows, :], v_ref[rows, :]
            new, a_all, do_all, dz_all, q_all = [], [], [], [], []
            for n, ((hh, r0), (pre, dq)) in enumerate(zip(chains, carry)):
                mask = "all" if sub is None else _diag_mask(th, tk, r0, sub, strict=True)
                if isinstance(mask, str) and mask == "none":
                    new.append((pre, dq))
                    continue
                c = jnp.sum(jnp.where(lane == 64 * hh + kb, rights[n], 0.0), axis=-1, keepdims=True)
                z = _dot_nt(qs[n], kblk)
                lb, lom = _log_sigmoids(z)
                if not isinstance(mask, str):
                    lom = jnp.where(mask, lom, 0.0)
                suf = _split_dot(lom, upper, 1)
                a = jnp.exp2(lb + suf + c)
                if not isinstance(mask, str):
                    a = jnp.where(mask, a, 0.0)
                g = a * _dot_nt(dos[n], vblk)
                left = _split_dot(g, upper, 1, nt=True) + pre
                sig = jnp.exp2(lb)
                dz = g * (1.0 - sig) - sig * left
                if not isinstance(mask, str):
                    dz = jnp.where(mask, dz, 0.0)
                dzb = dz.astype(BF16)
                a_all.append(a.astype(BF16))
                do_all.append(dos[n])
                dz_all.append(dzb)
                q_all.append(qs[n])
                new.append((left[:, tk - 1:tk] + g[:, tk - 1:tk], dq + _dot(dzb, kblk)))
            dv_ref[rows, :] += _dot_tn(jnp.concatenate(a_all, axis=0), jnp.concatenate(do_all, axis=0))
            dk_ref[rows, :] += _dot_tn(jnp.concatenate(dz_all, axis=0), jnp.concatenate(q_all, axis=0))
            return tuple(new)

        lane1 = lax.broadcasted_iota(jnp.int32, (1, LANES), 1)
        first = i * nsub
        for n, (hh, _) in enumerate(chains):
            top = jnp.max(rights[n], axis=0, keepdims=True)
            kb_of = lane1 - 64 * hh
            live = (kb_of >= 0) & (kb_of < i * nsub) & (top >= SB_CUT)
            first = jnp.minimum(first, jnp.min(jnp.where(live, kb_of, i * nsub)))

        init = (jnp.zeros((th, 1), F32), jnp.zeros((th, LANES), F32))
        carry = lax.fori_loop(first, i * nsub, lambda kb, cy: step(kb, cy, None), (init,) * len(chains))
        for sub in range(nsub):
            carry = step(i * nsub + sub, carry, sub)
        for n in range(nh):
            dq_ref[n * th:(n + 1) * th, :] = jnp.where(lane < 64, carry[n][1], carry[nh + n][1]) * SB_SCALE

    return pl.pallas_call(
        body, name="sb_bwd", grid=(4, s_len // tq),
        in_specs=[pl.BlockSpec((tq, LANES), lambda p, i: (i, p)),
                  pl.BlockSpec((s_len, LANES), lambda p, i: (0, 4 + p)),
                  pl.BlockSpec((s_len, LANES), lambda p, i: (0, 8 + p)),
                  pl.BlockSpec((tq, LANES), lambda p, i: (i, p)),
                  pl.BlockSpec((None, tq, LANES), lambda p, i: (p, i, 0))],
        out_specs=[pl.BlockSpec((tq, LANES), lambda p, i: (i, p)),
                   pl.BlockSpec((s_len, LANES), lambda p, i: (0, p)),
                   pl.BlockSpec((s_len, LANES), lambda p, i: (0, p))],
        out_shape=[jax.ShapeDtypeStruct((s_len, 512), F32)] * 3,
        compiler_params=_cparams(48, dimension_semantics=("arbitrary", "arbitrary")),
    )(sb, sb, sb, do, r)


def _merge_fwd(x, oa, ob, gates, bg, wa, wb, wo):
    s_len = x.shape[0]
    tm = _row_block(s_len)

    def body(x_ref, oa_ref, ob_ref, g_ref, bg_ref, wa_ref, wb_ref, wo_ref, y_ref):
        pa = _dot(oa_ref[...], wa_ref[...])
        pb = _dot(ob_ref[...], wb_ref[...])
        merged = (_sigmoid(g_ref[:, 0:D_MODEL] + bg_ref[0:1, :]) * pa
                  + _sigmoid(g_ref[:, D_MODEL:2 * D_MODEL] + bg_ref[1:2, :]) * pb)
        y_ref[...] = x_ref[...] + _dot(merged.astype(BF16), wo_ref[...])

    full = lambda shape: pl.BlockSpec(shape, lambda i: (0, 0))
    rowb = lambda n: pl.BlockSpec((tm, n), lambda i: (i, 0))
    return pl.pallas_call(
        body, name="merge_fwd", grid=(s_len // tm,),
        in_specs=[rowb(1024), rowb(512), rowb(512), rowb(2048), full((2, 1024)), full((512, 1024)),
                  full((512, 1024)), full((1024, 1024))],
        out_specs=rowb(1024),
        out_shape=jax.ShapeDtypeStruct((s_len, D_MODEL), F32),
        compiler_params=_cparams(48, dimension_semantics=("parallel",)),
    )(x, oa, ob, gates, bg, wa, wb, wo)


def _merge_bwd(dx1, oa, ob, gates, bg, wa, wb, wo):
    s_len = dx1.shape[0]
    tm = _row_block(s_len)

    def body(dx_ref, oa_ref, ob_ref, g_ref, bg_ref, wa_ref, wb_ref, wo_ref,
             doa_ref, dob_ref, dgate_ref, dpa_ref, dpb_ref, merged_ref, dxb_ref, dbg_ref):
        first = pl.program_id(0) == 0
        dxb = dx_ref[...].astype(BF16)
        dxb_ref[...] = dxb
        pa = _dot(oa_ref[...], wa_ref[...])
        pb = _dot(ob_ref[...], wb_ref[...])
        sa = _sigmoid(g_ref[:, 0:D_MODEL] + bg_ref[0:1, :])
        sbg = _sigmoid(g_ref[:, D_MODEL:2 * D_MODEL] + bg_ref[1:2, :])
        merged_ref[...] = (sa * pa + sbg * pb).astype(BF16)
        dm = _dot_nt(dxb, wo_ref[...])
        dpa = (dm * sa).astype(BF16)
        dpb = (dm * sbg).astype(BF16)
        dpa_ref[...] = dpa
        dpb_ref[...] = dpb
        dga = dm * pa * sa * (1.0 - sa)
        dgb = dm * pb * sbg * (1.0 - sbg)
        dgate_ref[:, 0:D_MODEL] = dga.astype(BF16)
        dgate_ref[:, D_MODEL:2 * D_MODEL] = dgb.astype(BF16)
        _acc_rows(dbg_ref.at[0:1, :], dga, first)
        _acc_rows(dbg_ref.at[1:2, :], dgb, first)
        doa_ref[...] = _dot_nt(dpa, wa_ref[...]).astype(BF16)
        dob_ref[...] = _dot_nt(dpb, wb_ref[...]).astype(BF16)

    full = lambda shape: pl.BlockSpec(shape, lambda i: (0, 0))
    rowb = lambda n: pl.BlockSpec((tm, n), lambda i: (i, 0))
    sds = lambda n, dt: jax.ShapeDtypeStruct((s_len, n), dt)
    return pl.pallas_call(
        body, name="merge_bwd", grid=(s_len // tm,),
        in_specs=[rowb(1024), rowb(512), rowb(512), rowb(2048), full((2, 1024)), full((512, 1024)),
                  full((512, 1024)), full((1024, 1024))],
        out_specs=[rowb(512), rowb(512), rowb(2048), rowb(1024), rowb(1024), rowb(1024), rowb(1024),
                   full((2, 1024))],
        out_shape=[sds(512, BF16), sds(512, BF16), sds(2048, BF16), sds(1024, BF16), sds(1024, BF16),
                   sds(1024, BF16), sds(1024, BF16), jax.ShapeDtypeStruct((2, 1024), F32)],
        compiler_params=_cparams(48, dimension_semantics=("arbitrary",)),
    )(dx1, oa, ob, gates, bg, wa, wb, wo)


def _mem_kv(mem, g, w):
    m_len = mem.shape[0]

    def body(mem_ref, g_ref, w_ref, mn_ref, kv_ref):
        mn, _, _ = _rms(mem_ref[...], g_ref[...])
        mnb = mn.astype(BF16)
        mn_ref[...] = mnb
        kv_ref[...] = _dot(mnb, w_ref[...]).astype(BF16)

    return pl.pallas_call(
        body, name="mem_kv",
        out_shape=[jax.ShapeDtypeStruct((m_len, D_MODEL), BF16), jax.ShapeDtypeStruct((m_len, 1024), BF16)],
    )(mem, g, w)


def _mem_bwd(mem, g, w, mn, dkv):
    def body(mem_ref, g_ref, w_ref, mn_ref, dkv_ref, dw_ref, dg_ref):
        dkvb = dkv_ref[...].astype(BF16)
        dw_ref[...] = _dot_tn(mn_ref[...], dkvb)
        dmn = _dot_nt(dkvb, w_ref[...])
        _, xh, _ = _rms(mem_ref[...], g_ref[...])
        dg_ref[...] = jnp.sum(dmn * xh, axis=0, keepdims=True)

    return pl.pallas_call(
        body, name="mem_bwd",
        out_shape=[jax.ShapeDtypeStruct((D_MODEL, 1024), F32), jax.ShapeDtypeStruct((1, D_MODEL), F32)],
    )(mem, g, w, mn, dkv)


def _xattn_heads(xqb, kv_ref, m_len):
    ps = []
    for h in range(X_HEADS):
        hs = slice(h * X_HEAD_DIM, (h + 1) * X_HEAD_DIM)
        s = _dot_nt(xqb[:, hs], kv_ref[:, hs]) * X_SCALE
        e = jnp.exp(s - jnp.max(s, axis=-1, keepdims=True))
        ps.append(e / jnp.sum(e, axis=-1, keepdims=True))
    return ps


def _xattn_fwd(x1, g, wxq, kv, wxo):
    s_len, m_len = x1.shape[0], kv.shape[0]
    tm = _row_block(s_len)

    def body(x_ref, g_ref, wq_ref, kv_ref, wo_ref, y_ref):
        hx, _, _ = _rms(x_ref[...], g_ref[...])
        xqb = _dot(hx.astype(BF16), wq_ref[...]).astype(BF16)
        ps = _xattn_heads(xqb, kv_ref, m_len)
        xo = jnp.concatenate(
            [_dot(ps[h].astype(BF16), kv_ref[:, 512 + h * X_HEAD_DIM:512 + (h + 1) * X_HEAD_DIM])
             for h in range(X_HEADS)], axis=-1)
        y_ref[...] = x_ref[...] + _dot(xo.astype(BF16), wo_ref[...])

    full = lambda shape: pl.BlockSpec(shape, lambda i: (0, 0))
    rowb = lambda n: pl.BlockSpec((tm, n), lambda i: (i, 0))
    return pl.pallas_call(
        body, name="xattn_fwd", grid=(s_len // tm,),
        in_specs=[rowb(1024), full((1, 1024)), full((1024, 512)), full((m_len, 1024)), full((512, 1024))],
        out_specs=rowb(1024),
        out_shape=jax.ShapeDtypeStruct((s_len, D_MODEL), F32),
        compiler_params=_cparams(48, dimension_semantics=("parallel",)),
    )(x1, g, wxq, kv, wxo)


def _xattn_bwd(x1, dx2, g, wxq, kv, wxo):
    s_len, m_len = x1.shape[0], kv.shape[0]
    tm = _row_block(s_len)

    def body(x_ref, dy_ref, g_ref, wq_ref, kv_ref, wo_ref, dx_ref, dwq_ref, dwo_ref, dkv_ref, dg_ref):
        first = pl.program_id(0) == 0
        gv = g_ref[...]
        hx, xh, r = _rms(x_ref[...], gv)
        hxb = hx.astype(BF16)
        xqb = _dot(hxb, wq_ref[...]).astype(BF16)
        ps = _xattn_heads(xqb, kv_ref, m_len)
        dy = dy_ref[...]
        dyb = dy.astype(BF16)
        dxo = _dot_nt(dyb, wo_ref[...])
        xos, dqs, dks, dvs = [], [], [], []
        for h in range(X_HEADS):
            hs = slice(h * X_HEAD_DIM, (h + 1) * X_HEAD_DIM)
            vs = slice(512 + h * X_HEAD_DIM, 512 + (h + 1) * X_HEAD_DIM)
            p = ps[h]
            pb = p.astype(BF16)
            dxoh = dxo[:, hs].astype(BF16)
            xos.append(_dot(pb, kv_ref[:, vs]))
            dp = _dot_nt(dxoh, kv_ref[:, vs])
            ds = (p * (dp - jnp.sum(dp * p, axis=-1, keepdims=True)) * X_SCALE).astype(BF16)
            dvs.append(_dot_tn(pb, dxoh))
            dks.append(_dot_tn(ds, xqb[:, hs]))
            dqs.append(_dot(ds, kv_ref[:, hs]))
        xob = jnp.concatenate(xos, axis=-1).astype(BF16)
        dxqb = jnp.concatenate(dqs, axis=-1).astype(BF16)
        _acc(dwo_ref, _dot_tn(xob, dyb), first)
        _acc(dwq_ref, _dot_tn(hxb, dxqb), first)
        _acc(dkv_ref, jnp.concatenate(dks + dvs, axis=-1), first)
        dhx = _dot_nt(dxqb, wq_ref[...])
        dx, dgr = _rms_bwd(dhx, xh, r, gv)
        dx_ref[...] = dy + dx
        _acc_rows(dg_ref, dgr, first)

    full = lambda shape: pl.BlockSpec(shape, lambda i: (0, 0))
    rowb = lambda n: pl.BlockSpec((tm, n), lambda i: (i, 0))
    return pl.pallas_call(
        body, name="xattn_bwd", grid=(s_len // tm,),
        in_specs=[rowb(1024), rowb(1024), full((1, 1024)), full((1024, 512)), full((m_len, 1024)),
                  full((512, 1024))],
        out_specs=[rowb(1024), full((1024, 512)), full((512, 1024)), full((m_len, 1024)), full((1, 1024))],
        out_shape=[jax.ShapeDtypeStruct((s_len, D_MODEL), F32), jax.ShapeDtypeStruct((1024, 512), F32),
                   jax.ShapeDtypeStruct((512, 1024), F32), jax.ShapeDtypeStruct((m_len, 1024), F32),
                   jax.ShapeDtypeStruct((1, D_MODEL), F32)],
        compiler_params=_cparams(48, dimension_semantics=("arbitrary",)),
    )(x1, dx2, g, wxq, kv, wxo)


FF_TILE = 1408


def _ffn_fwd(x2, g, wg, wu, wd):
    s_len = x2.shape[0]
    tm, tf = _row_block(s_len), FF_TILE

    def body(x_ref, g_ref, wg_ref, wu_ref, wd_ref, y_ref, h_ref):
        j = pl.program_id(1)

        @pl.when(j == 0)
        def _():
            hf, _, _ = _rms(x_ref[...], g_ref[...])
            h_ref[...] = hf.astype(BF16)
            y_ref[...] = x_ref[...]

        hb = h_ref[...]
        gt = _dot(hb, wg_ref[...])
        up = _dot(hb, wu_ref[...])
        act = gt * _sigmoid(gt) * up
        y_ref[...] += _dot(act.astype(BF16), wd_ref[...])

    rowb = pl.BlockSpec((tm, D_MODEL), lambda i, j: (i, 0))
    return pl.pallas_call(
        body, name="ffn_fwd", grid=(s_len // tm, D_FF // tf),
        in_specs=[rowb, pl.BlockSpec((1, D_MODEL), lambda i, j: (0, 0)),
                  pl.BlockSpec((D_MODEL, tf), lambda i, j: (0, j)),
                  pl.BlockSpec((D_MODEL, tf), lambda i, j: (0, j)),
                  pl.BlockSpec((tf, D_MODEL), lambda i, j: (j, 0))],
        out_specs=[rowb, rowb],
        out_shape=[jax.ShapeDtypeStruct((s_len, D_MODEL), F32), jax.ShapeDtypeStruct((s_len, D_MODEL), BF16)],
        compiler_params=_cparams(48, dimension_semantics=("parallel", "arbitrary")),
    )(x2, g, wg, wu, wd)


def _ffn_bwd(x2, hf, dx3, dx3b, g, wg, wu, wd):
    s_len = x2.shape[0]
    tm, tf = _row_block(s_len), FF_TILE
    nf = D_FF // tf

    def act_body(h_ref, dy_ref, wg_ref, wu_ref, wd_ref, dgt_ref, dup_ref, act_ref):
        hb = h_ref[...]
        gt = _dot(hb, wg_ref[...])
        up = _dot(hb, wu_ref[...])
        sg = _sigmoid(gt)
        silu = gt * sg
        dact = _dot_nt(dy_ref[...], wd_ref[...])
        dgt_ref[...] = (dact * up * (sg * (1.0 + gt * (1.0 - sg)))).astype(BF16)
        dup_ref[...] = (dact * silu).astype(BF16)
        act_ref[...] = (silu * up).astype(BF16)

    rowb = pl.BlockSpec((tm, D_MODEL), lambda i, j: (i, 0))
    ffb = pl.BlockSpec((tm, tf), lambda i, j: (i, j))
    dgt, dup, act = pl.pallas_call(
        act_body, name="ffn_bwd_act", grid=(s_len // tm, nf),
        in_specs=[rowb, rowb,
                  pl.BlockSpec((D_MODEL, tf), lambda i, j: (0, j)),
                  pl.BlockSpec((D_MODEL, tf), lambda i, j: (0, j)),
                  pl.BlockSpec((tf, D_MODEL), lambda i, j: (j, 0))],
        out_specs=[ffb, ffb, ffb],
        out_shape=[jax.ShapeDtypeStruct((s_len, D_FF), BF16)] * 3,
        compiler_params=_cparams(56, dimension_semantics=("parallel", "arbitrary")),
    )(hf, dx3b, wg, wu, wd)
    tm = min(s_len, 256)

    def in_body(x_ref, dy_ref, g_ref, wg_ref, wu_ref, dgt_ref, dup_ref, dx_ref, dg_ref):
        dh = _dot_nt(dgt_ref[...], wg_ref[...]) + _dot_nt(dup_ref[...], wu_ref[...])
        gv = g_ref[...]
        _, xh, r = _rms(x_ref[...], gv)
        dx, dgr = _rms_bwd(dh, xh, r, gv)
        dx_ref[...] = dy_ref[...] + dx
        _acc_rows(dg_ref, dgr, pl.program_id(0) == 0)

    row1 = lambda n: pl.BlockSpec((tm, n), lambda i: (i, 0))
    full = lambda shape: pl.BlockSpec(shape, lambda i: (0, 0))
    dx2, dg = pl.pallas_call(
        in_body, name="ffn_bwd_in", grid=(s_len // tm,),
        in_specs=[row1(D_MODEL), row1(D_MODEL), full((1, D_MODEL)), full((D_MODEL, D_FF)), full((D_MODEL, D_FF)),
                  row1(D_FF), row1(D_FF)],
        out_specs=[row1(D_MODEL), full((1, D_MODEL))],
        out_shape=[jax.ShapeDtypeStruct((s_len, D_MODEL), F32), jax.ShapeDtypeStruct((1, D_MODEL), F32)],
        compiler_params=_cparams(48, dimension_semantics=("arbitrary",)),
    )(x2, dx3, g, wg, wu, dgt, dup)
    return dx2, dgt, dup, act, dg


def _loss_head(x3, g, target):
    s_len = x3.shape[0]
    tm = _row_block(s_len)

    def body(x_ref, g_ref, t_ref, sse_ref, dx_ref, dxb_ref, dg_ref):
        first = pl.program_id(0) == 0
        gv = g_ref[...]
        y, xh, r = _rms(x_ref[...], gv)
        err = y - t_ref[...]
        _acc(sse_ref, jnp.broadcast_to(jnp.sum(err * err), (8, LANES)), first)
        dx, dgr = _rms_bwd(err * (1.0 / D_MODEL), xh, r, gv)
        dx_ref[...] = dx
        dxb_ref[...] = dx.astype(BF16)
        _acc_rows(dg_ref, dgr, first)

    rowb = pl.BlockSpec((tm, D_MODEL), lambda i: (i, 0))
    return pl.pallas_call(
        body, name="loss_head", grid=(s_len // tm,),
        in_specs=[rowb, pl.BlockSpec((1, D_MODEL), lambda i: (0, 0)), rowb],
        out_specs=[pl.BlockSpec((8, LANES), lambda i: (0, 0)), rowb, rowb,
                   pl.BlockSpec((1, D_MODEL), lambda i: (0, 0))],
        out_shape=[jax.ShapeDtypeStruct((8, LANES), F32), jax.ShapeDtypeStruct((s_len, D_MODEL), F32),
                   jax.ShapeDtypeStruct((s_len, D_MODEL), BF16), jax.ShapeDtypeStruct((1, D_MODEL), F32)],
        compiler_params=_cparams(dimension_semantics=("arbitrary",)),
    )(x3, g, target)


def _mla_prep_bwd(lat, g_q, g_kv, w_uq, w_uk, w_uv, cosf, sinf, dq, dk, dv):
    s_len = lat.shape[0]
    tm = _row_block(s_len)

    def body(lat_ref, gq_ref, gkv_ref, wuq_ref, wuk_ref, wuv_ref, cos_ref, sin_ref, dq_ref, dk_ref, dv_ref,
             dlat_ref, dqb_ref, dkb_ref, dvb_ref, dgq_ref, dgkv_ref):
        first = pl.program_id(0) == 0
        lane = lax.broadcasted_iota(jnp.int32, (tm, LANES), 1)
        cosv, sinv = cos_ref[...], sin_ref[...]
        gq, gkv = gq_ref[...], gkv_ref[...]
        _, qxh, qr = _rms(lat_ref[:, 0:256], gq)
        _, kxh, kr_ = _rms(lat_ref[:, 256:384], gkv)
        dkr = jnp.zeros((tm, LANES), F32)
        for h in range(MLA_HEADS):
            sl = slice(h * HEAD_PAD, (h + 1) * HEAD_PAD)
            blk = dq_ref[:, sl]
            dqb_ref[:, sl] = (blk * cosv + _rope_rot_t(blk, lane) * sinv).astype(BF16)
            kblk = dk_ref[:, sl] * (1.0 / MLA_Q_FOLD)
            dkb_ref[:, sl] = kblk.astype(BF16)
            dkr = dkr + kblk
        dvb = dv_ref[...].astype(BF16)
        dvb_ref[...] = dvb
        dkr = jnp.where((lane >= 64) & (lane < 96), dkr, 0.0)
        dkr = dkr * cosv + _rope_rot_t(dkr, lane) * sinv
        dql = _dot_nt(dqb_ref[...], wuq_ref[...])
        dkvl = _dot_nt(dkb_ref[...], wuk_ref[...]) + _dot_nt(dvb, wuv_ref[...])
        dcq, dgqr = _rms_bwd(dql, qxh, qr, gq)
        dckv, dgkvr = _rms_bwd(dkvl, kxh, kr_, gkv)
        dlat_ref[:, 0:256] = dcq
        dlat_ref[:, 256:384] = dckv
        dlat_ref[:, K_R_OFF:K_R_OFF + LANES] = pltpu.roll(dkr, 64, 1)
        _acc_rows(dgq_ref, dgqr, first)
        _acc_rows(dgkv_ref, dgkvr, first)

    full = lambda shape: pl.BlockSpec(shape, lambda i: (0, 0))
    rowb = lambda n: pl.BlockSpec((tm, n), lambda i: (i, 0))
    sds = lambda n, dt: jax.ShapeDtypeStruct((s_len, n), dt)
    return pl.pallas_call(
        body, name="mla_prep_bwd", grid=(s_len // tm,),
        in_specs=[rowb(512), full((1, 256)), full((1, 128)), full((256, 1024)), full((128, 1024)),
                  full((128, 512)), rowb(128), rowb(128), rowb(1024), rowb(1024), rowb(512)],
        out_specs=[rowb(512), rowb(1024), rowb(1024), rowb(512), full((1, 256)), full((1, 128))],
        out_shape=[sds(512, F32), sds(1024, BF16), sds(1024, BF16), sds(512, BF16),
                   jax.ShapeDtypeStruct((1, 256), F32), jax.ShapeDtypeStruct((1, 128), F32)],
        compiler_params=_cparams(48, dimension_semantics=("arbitrary",)),
    )(lat, g_q, g_kv, w_uq, w_uk, w_uv, cosf, sinf, dq, dk, dv)


def _in_proj_bwd(x, g, w, dx1, dlat, dsbq, dsbk, dsbv, dgates, ride):
    s_len = x.shape[0]
    tm = min(s_len, 256)
    nb = s_len // tm

    def body(x_ref, g_ref, w_ref, dx1_ref, dlat_ref, dq_ref, dk_ref, dv_ref, dgate_ref, *rest):
        gx_ref, dg_ref = rest[ride.n:ride.n + 2]
        dproj = rest[-1]
        ride.bind(rest[:ride.n], rest[ride.n + 2:2 * ride.n + 2], rest[2 * ride.n + 2:-1])
        pl.when(pl.program_id(0) == 0)(ride.issue)
        dproj[:, 0:512] = dlat_ref[...].astype(BF16)
        dproj[:, 512:1024] = dq_ref[...].astype(BF16)
        dproj[:, 1024:1536] = (dk_ref[...] * LN2).astype(BF16)
        dproj[:, 1536:2048] = dv_ref[...].astype(BF16)
        dproj[:, 2048:4096] = dgate_ref[...]
        dh = _dot_nt(dproj[...], w_ref[...])
        gv = g_ref[...]
        _, xh, r = _rms(x_ref[...], gv)
        dx, dgr = _rms_bwd(dh, xh, r, gv)
        gx_ref[...] = dx1_ref[...] + dx
        _acc_rows(dg_ref, dgr, pl.program_id(0) == 0)
        pl.when(pl.program_id(0) == nb - 1)(ride.finish)

    rowb = lambda n: pl.BlockSpec((tm, n), lambda i: (i, 0))
    full = lambda shape: pl.BlockSpec(shape, lambda i: (0, 0))
    outs = pl.pallas_call(
        body, name="in_proj_bwd", grid=(nb,),
        in_specs=[rowb(D_MODEL), full((1, D_MODEL)), full((D_MODEL, D_IN_PAD)), rowb(D_MODEL),
                  rowb(512), rowb(512), rowb(512), rowb(512), rowb(2 * D_MODEL)] + ride.specs,
        out_specs=[rowb(D_MODEL), full((1, D_MODEL))] + ride.specs,
        out_shape=[jax.ShapeDtypeStruct((s_len, D_MODEL), F32), jax.ShapeDtypeStruct((1, D_MODEL), F32)]
        + ride.out_shape,
        scratch_shapes=ride.scratch + [pltpu.VMEM((tm, D_IN_PAD), BF16)],
        compiler_params=_cparams(48, dimension_semantics=("arbitrary",)),
    )(x, g, w, dx1, dlat, dsbq, dsbk, dsbv, dgates, *ride.srcs)
    return outs[0], outs[1], outs[2:]


def _adamw(landed, w, m, v, name):
    r, c = w.shape
    lanes = _round_up(c, LANES)
    tb = r
    for cand in range(r, 0, -1):
        if r % cand == 0 and (cand % 8 == 0 or cand == r) and N_DEV * cand * lanes * 4 <= ADAM_BLOCK_BYTES:
            tb = cand
            break
    c1 = 1.0 - ADAM_B1 ** ADAM_STEP
    c2 = 1.0 - ADAM_B2 ** ADAM_STEP

    def body(l_ref, w_ref, m_ref, v_ref, g_ref, d_ref, nm_ref, nv_ref):
        g = l_ref[0]
        for k in range(1, N_DEV):
            g = g + l_ref[k]
        nm = ADAM_B1 * m_ref[...] + (1.0 - ADAM_B1) * g
        nv = ADAM_B2 * v_ref[...] + (1.0 - ADAM_B2) * (g * g)
        g_ref[...] = g
        nm_ref[...] = nm
        nv_ref[...] = nv
        d_ref[...] = -ADAM_LR * ((nm / c1) / (jnp.sqrt(nv / c2) + ADAM_EPS) + ADAM_WD * w_ref[...])

    blk = pl.BlockSpec((tb, c), lambda i: (i, 0))
    return pl.pallas_call(
        body, name=name, grid=(r // tb,),
        in_specs=[pl.BlockSpec((N_DEV, tb, c), lambda i: (0, i, 0)), blk, blk, blk],
        out_specs=[blk, blk, blk, blk],
        out_shape=[jax.ShapeDtypeStruct((r, c), F32)] * 4,
        compiler_params=_cparams(dimension_semantics=("parallel",)),
    )(landed, w, m, v)


def _shard_shape(shape, axis):
    return tuple(d // N_DEV if a == axis else d for a, d in enumerate(shape))


def _split_pieces(full, axis):
    r, c = full.shape
    if axis == 0:
        return full.reshape(N_DEV, r // N_DEV, c)
    return full.reshape(r, N_DEV, c // N_DEV).transpose(1, 0, 2)


def _join_shards(gathered, axis):
    _, r, c = gathered.shape
    if axis == 0:
        return gathered.reshape(N_DEV * r, c)
    return gathered.transpose(1, 0, 2).reshape(r, N_DEV * c)


def kernel(x, mem, positions, g_mix, w_in, b_gate, g_q_lat, w_uq, g_kv_lat, w_ukv, w_a_proj, w_b_proj, w_o, g_x, g_mem, w_xq, w_xkv, w_xo, g_ffn, w_gate, w_up, w_down, g_final, loss_target, m_g_mix, m_w_in, m_b_gate, m_g_q_lat, m_w_uq, m_g_kv_lat, m_w_ukv, m_w_a_proj, m_w_b_proj, m_w_o, m_g_x, m_g_mem, m_w_xq, m_w_xkv, m_w_xo, m_g_ffn, m_w_gate, m_w_up, m_w_down, m_g_final, v_g_mix, v_w_in, v_b_gate, v_g_q_lat, v_w_uq, v_g_kv_lat, v_w_ukv, v_w_a_proj, v_w_b_proj, v_w_o, v_g_x, v_g_mem, v_w_xq, v_w_xkv, v_w_xo, v_g_ffn, v_w_gate, v_w_up, v_w_down, v_g_final):
    given = dict(locals())
    s_len = x.shape[1]
    x2d = x.reshape(s_len, D_MODEL)
    mem2d = mem.reshape(-1, D_MODEL)
    target = loss_target.reshape(s_len, D_MODEL)

    names = [name for name, _, _ in SHARDED]
    axis_of = {name: axis for name, _, axis in SHARDED}
    shard2d = lambda name, prefix="": given[prefix + name].reshape(
        _shard_shape(dict((n, s) for n, s, _ in SHARDED)[name], axis_of[name]))

    wire = lambda name: shard2d(name) if name == "b_gate" else shard2d(name).astype(BF16)
    early = [n for n in names if n in NEEDED_FIRST]
    late = [n for n in names if n not in NEEDED_FIRST]
    gathered = _exchange(True, [wire(n) for n in early], "weights_gather_first")
    wts = {n: _join_shards(g, axis_of[n]) for n, g in zip(early, gathered)}

    w_in_p = jnp.concatenate([wts["w_in"][:, :LAT_COLS], jnp.zeros((D_MODEL, D_IN_PAD - D_IN), BF16),
                              wts["w_in"][:, LAT_COLS:]], axis=1)
    w_uq_p = jnp.pad(wts["w_uq"].reshape(256, MLA_HEADS, 96), ((0, 0), (0, 0), (0, 32))).reshape(256, 1024)
    ukv = wts["w_ukv"].reshape(128, MLA_HEADS, 128)
    w_uk_p = jnp.pad(ukv[:, :, :64], ((0, 0), (0, 0), (0, 64))).reshape(128, 1024)
    w_uv = ukv[:, :, 64:].reshape(128, 512)
    w_uv1 = jnp.pad(ukv[:, :, 64:], ((0, 0), (0, 0), (0, 64))).reshape(128, 1024)
    bg = wts["b_gate"]

    inv_freq = ROPE_THETA ** (-jnp.arange(0, MLA_ROPE, 2, dtype=F32) / MLA_ROPE)
    ang = positions.reshape(s_len).astype(F32)[:, None] * inv_freq
    cos16, sin16 = jnp.cos(ang), jnp.sin(ang)
    cosf = jnp.concatenate([jnp.ones((s_len, 64), F32), cos16, cos16, jnp.ones((s_len, 32), F32)], axis=1)
    sinf = jnp.concatenate([jnp.zeros((s_len, 64), F32), sin16, sin16, jnp.zeros((s_len, 32), F32)], axis=1)

    h1, lat, sb, gates = _in_proj(x2d, g_mix, w_in_p)
    qa, ka, va, va1, q_lat, kv_lat = _mla_prep(lat, g_q_lat, g_kv_lat, w_uq_p, w_uk_p, w_uv, w_uv1, cosf, sinf)
    oa, lse, gathered = _mla_fwd(qa, ka, va1, _Exchange(True, [wire(n) for n in late]))
    wts.update({n: _join_shards(g, axis_of[n]) for n, g in zip(late, gathered)})
    ob, sb_r = _sb_fwd(sb)
    x1 = _merge_fwd(x2d, oa, ob, gates, bg, wts["w_a_proj"], wts["w_b_proj"], wts["w_o"])
    mn, xkv = _mem_kv(mem2d, g_mem, wts["w_xkv"])
    x2 = _xattn_fwd(x1, g_x, wts["w_xq"], xkv, wts["w_xo"])
    x3, hf = _ffn_fwd(x2, g_ffn, wts["w_gate"], wts["w_up"], wts["w_down"])
    g_final2d = g_final.reshape(1, D_MODEL)
    sse, dx3, dx3b, dg_final = _loss_head(x3, g_final2d, target)
    loss = lax.psum(sse[0, 0] * (0.5 / D_MODEL), ("x", "y", "c"))

    dx2, dgt, dup, act, dg_ffn = _ffn_bwd(x2, hf, dx3, dx3b, g_ffn, wts["w_gate"], wts["w_up"], wts["w_down"])
    dx1, dw_xq, dw_xo, dxkv, dg_x = _xattn_bwd(x1, dx2, g_x, wts["w_xq"], xkv, wts["w_xo"])
    dw_xkv, dg_mem = _mem_bwd(mem2d, g_mem, wts["w_xkv"], mn, dxkv)
    doa, dob, dgates, dpa, dpb, merged, dx1b, dbg = _merge_bwd(
        dx1, oa, ob, gates, bg, wts["w_a_proj"], wts["w_b_proj"], wts["w_o"])
    dsbq, dsbk, dsbv = _sb_bwd(sb, dob, sb_r)
    full_grads = {
        "w_a_proj": _tn_matmul(oa, dpa, "dw_a"),
        "w_b_proj": _tn_matmul(ob, dpb, "dw_b"),
        "w_o": _tn_matmul(merged, dx1b, "dw_o"),
        "w_xq": dw_xq,
        "w_xkv": dw_xkv,
        "w_xo": dw_xo,
        "w_gate": _tn_matmul(hf, dgt, "dw_gate", tn=FF_TILE),
        "w_up": _tn_matmul(hf, dup, "dw_up", tn=FF_TILE),
        "w_down": _tn_matmul(act, dx3b, "dw_down", tka=FF_TILE),
    }
    dw_sb = _tn_matmul_sb(h1, dsbq, dsbk, dsbv, "dw_in_sb")
    own0 = D_IN // N_DEV - LAT_COLS
    w_in_rest = _split_pieces(
        jnp.concatenate([dw_sb[:, :LAT_COLS], dw_sb, _tn_matmul(h1, dgates, "dw_in_gates")], axis=1), 1)
    dqa, dka, dva, got = _mla_bwd(
        qa, ka, va, oa, doa, lse,
        _Exchange(False, [_split_pieces(full_grads[n], axis_of[n]) for n in late] + [w_in_rest]))
    landed = dict(zip(late, got[:-1]))
    dlat, dqb, dkb, dvb, dg_q, dg_kv = _mla_prep_bwd(
        lat, g_q_lat, g_kv_lat, w_uq_p, w_uk_p, w_uv, cosf, sinf, dqa, dka, dva)
    w_in_first = jnp.concatenate([_tn_matmul(h1, dlat, "dw_in_lat")[:, :LAT_COLS], dw_sb[:, :own0]], axis=1)
    dw_uq_p = _tn_matmul(q_lat, dqb, "dw_uq")
    dw_uk_p = _tn_matmul(kv_lat, dkb, "dw_uk")
    dw_uv = _tn_matmul(kv_lat, dvb, "dw_uv")
    full_grads.update({
        "b_gate": dbg,
        "w_uq": dw_uq_p.reshape(256, MLA_HEADS, 128)[:, :, :96].reshape(256, 768),
        "w_ukv": jnp.concatenate([dw_uk_p.reshape(128, MLA_HEADS, 128)[:, :, :64],
                                  dw_uv.reshape(128, MLA_HEADS, 64)], axis=2).reshape(128, 1024),
    })
    small = [n for n in early if n != "w_in"]
    grad_x, dg_mix, got2 = _in_proj_bwd(
        x2d, g_mix, w_in_p, dx1, dlat, dsbq, dsbk, dsbv, dgates,
        _Exchange(False, [_split_pieces(full_grads[n], axis_of[n]) for n in small], to_first=[w_in_first]))
    landed.update(zip(small, got2[:-1]))
    me = 4 * lax.axis_index("x") + 2 * lax.axis_index("y") + lax.axis_index("c")
    landed["w_in"] = jnp.where(me == 0, got2[-1], got[-1])
    rep_grads = {"g_mix": dg_mix, "g_q_lat": dg_q, "g_kv_lat": dg_kv, "g_x": dg_x, "g_mem": dg_mem,
                 "g_ffn": dg_ffn, "g_final": dg_final}
    rep_cat = lambda prefix, src: jnp.concatenate(
        [src[prefix + n].reshape(-1) for n, _ in REPLICATED]).reshape(-1, LANES)
    rep_src = jnp.broadcast_to(rep_cat("", rep_grads), (N_DEV,) + rep_cat("", rep_grads).shape)
    rep_landed = _exchange(False, [rep_src], "grads_gains")[0]

    res = {}
    for name, _, _ in SHARDED:
        outs = _adamw(landed[name], shard2d(name), shard2d(name, "m_"), shard2d(name, "v_"), "adamw_" + name)
        res[name] = [o.reshape(given[name].shape) for o in outs]
    rep_outs = _adamw(rep_landed, rep_cat("", given), rep_cat("m_", given), rep_cat("v_", given), "adamw_gains")
    off = 0
    for name, n in REPLICATED:
        res[name] = [o.reshape(-1)[off:off + n].reshape(given[name].shape) for o in rep_outs]
        off += n
    result = [loss, grad_x.reshape(x.shape)]
    for k in range(4):
        result.extend(res[name][k] for name in WEIGHT_ORDER)
    return tuple(result)
```

```python
import functools
import math

import jax
import jax.numpy as jnp
from jax import lax
from jax.experimental import pallas as pl
from jax.experimental.pallas import tpu as pltpu

F32 = jnp.float32
BF16 = jnp.bfloat16

D_MODEL = 1024
MLA_HEADS = 8
MLA_Q_RANK = 256
MLA_KV_RANK = 128
MLA_NOPE = 64
MLA_ROPE = 32
ROPE_THETA = 10000.0
SB_WIDTH = 512
X_HEADS = 4
X_HEAD_DIM = 128
D_FF = 2816
EPS = 1e-6
D_IN = 4000
D_IN_PAD = 4096
K_R_OFF = 384
LAT_COLS = 416
LANES = 128
HEAD_PAD = 128
MLA_SCALE = 1.0 / math.sqrt(MLA_NOPE + MLA_ROPE)
SB_SCALE = 0.125
LOG2E = math.log2(math.e)
LN2 = math.log(2.0)
MLA_Q_FOLD = MLA_SCALE * LOG2E
SB_Q_FOLD = SB_SCALE * LOG2E
SB_CUT = -160.0
X_SCALE = 1.0 / math.sqrt(X_HEAD_DIM)
NEG_BIG = -1e30

ADAM_LR = 0.001
ADAM_B1 = 0.9
ADAM_B2 = 0.999
ADAM_EPS = 1e-08
ADAM_WD = 0.01
ADAM_STEP = 10

N_DEV = 8
MIB = 1024 * 1024
ADAM_BLOCK_BYTES = 4 * MIB

SHARDED = (
    ("w_in", (D_MODEL, D_IN), 1),
    ("b_gate", (2, D_MODEL), 1),
    ("w_uq", (MLA_Q_RANK, 768), 1),
    ("w_ukv", (MLA_KV_RANK, 1024), 1),
    ("w_a_proj", (512, D_MODEL), 1),
    ("w_b_proj", (512, D_MODEL), 1),
    ("w_o", (D_MODEL, D_MODEL), 0),
    ("w_xq", (D_MODEL, 512), 0),
    ("w_xkv", (D_MODEL, 1024), 0),
    ("w_xo", (512, D_MODEL), 1),
    ("w_gate", (D_MODEL, D_FF), 1),
    ("w_up", (D_MODEL, D_FF), 1),
    ("w_down", (D_FF, D_MODEL), 0),
)
NEEDED_FIRST = ("w_in", "b_gate", "w_uq", "w_ukv")
REPLICATED = (
    ("g_mix", 1024), ("g_q_lat", 256), ("g_kv_lat", 128), ("g_x", 1024),
    ("g_mem", 1024), ("g_ffn", 1024), ("g_final", 1024),
)
WEIGHT_ORDER = ("g_mix", "w_in", "b_gate", "g_q_lat", "w_uq", "g_kv_lat", "w_ukv", "w_a_proj",
                "w_b_proj", "w_o", "g_x", "g_mem", "w_xq", "w_xkv", "w_xo", "g_ffn", "w_gate",
                "w_up", "w_down", "g_final")


def _round_up(n, m):
    return -(-n // m) * m


def _cparams(vmem_mib=None, **kw):
    if vmem_mib is not None:
        kw["vmem_limit_bytes"] = vmem_mib * MIB
    return pltpu.CompilerParams(**kw)


def _dot(a, b):
    return jnp.dot(a, b, preferred_element_type=F32)


def _dot_nt(a, b):
    return lax.dot_general(a, b, (((1,), (1,)), ((), ())), preferred_element_type=F32)


def _dot_tn(a, b):
    return lax.dot_general(a, b, (((0,), (0,)), ((), ())), preferred_element_type=F32)


def _rms(x, g):
    r = lax.rsqrt(jnp.mean(x * x, axis=-1, keepdims=True) + EPS)
    xh = x * r
    return xh * g, xh, r


def _rms_bwd(dy, xh, r, g):
    u = dy * g
    dx = r * (u - xh * jnp.mean(u * xh, axis=-1, keepdims=True))
    return dx, dy * xh


def _sigmoid(z):
    return 1.0 / (1.0 + jnp.exp(-z))


def _acc_rows(ref, val, first):
    s = jnp.sum(val, axis=0, keepdims=True)

    @pl.when(first)
    def _():
        ref[...] = s

    @pl.when(jnp.logical_not(first))
    def _():
        ref[...] += s


def _acc(ref, val, first):
    @pl.when(first)
    def _():
        ref[...] = val

    @pl.when(jnp.logical_not(first))
    def _():
        ref[...] += val


def _peer(k):
    x, y, c = lax.axis_index("x"), lax.axis_index("y"), lax.axis_index("c")
    px = 1 - x if (k >> 2) & 1 else x
    py = 1 - y if (k >> 1) & 1 else y
    pc = 1 - c if k & 1 else c
    return (px, py, pc), 4 * px + 2 * py + pc


N_PEERS = N_DEV - 1
OTHER_CHIPS = (2, 4, 6)


def _land_shape(gather, src):
    return (N_DEV,) + src.shape if gather else src.shape


class _Exchange:
    def __init__(self, gather, srcs, to_first=()):
        self.gather, self.m, self.srcs = gather, len(srcs), list(srcs) + list(to_first)
        self.n = len(self.srcs)
        self.out_shape = ([jax.ShapeDtypeStruct(_land_shape(gather, s), s.dtype) for s in srcs]
                          + [jax.ShapeDtypeStruct(_land_shape(True, s), s.dtype) for s in to_first])
        self.specs = [pl.BlockSpec(memory_space=pl.ANY)] * self.n
        self.scratch = [pltpu.SemaphoreType.DMA((self.n * N_PEERS,)), pltpu.SemaphoreType.DMA((self.n * N_PEERS,)),
                        pltpu.SemaphoreType.DMA((self.n,))]

    def bind(self, src, land, sems):
        self.src, self.land = src, land
        self.send_sems, self.recv_sems, self.local_sems = sems

    def _copy(self, a, k, source, to, target=1):
        return pltpu.make_async_remote_copy(
            src_ref=source, dst_ref=to,
            send_sem=self.send_sems.at[a * N_PEERS + k - 1], recv_sem=self.recv_sems.at[a * N_PEERS + k - 1],
            device_id=_peer(target)[0], device_id_type=pl.DeviceIdType.MESH)

    def _row(self, a, k):
        return self.land[a].at[_peer(k)[1]]

    def _mine(self, a):
        me = _peer(0)[1]
        whole = self.gather or a >= self.m
        return pltpu.make_async_copy(self.src[a] if whole else self.src[a].at[me], self.land[a].at[me],
                                     self.local_sems.at[a])

    def issue(self):
        me = _peer(0)[1]
        for a in range(self.m):
            self._mine(a).start()
            for k in ((1,) + OTHER_CHIPS if self.gather else range(1, N_DEV)):
                source = self.src[a] if self.gather else self.src[a].at[_peer(k)[1]]
                self._copy(a, k, source, self.land[a].at[me], target=k).start()
        for a in range(self.m, self.n):
            pl.when(me == 0)(self._mine(a).start)
            for k in range(1, N_DEV):
                pl.when(me == k)(self._copy(a, k, self.src[a], self.land[a].at[me], target=k).start)

    def finish(self):
        me = _peer(0)[1]
        part = lambda a: self.src[a] if self.gather or a >= self.m else self.src[a].at[me]
        if self.gather:
            for a in range(self.m):
                for k in OTHER_CHIPS:
                    self._copy(a, k, part(a), self._row(a, k)).wait_recv()
                    self._copy(a, k + 1, self._row(a, k), self._row(a, k), target=1).start()
        for a in range(self.m):
            for k in ((1, 3, 5, 7) if self.gather else range(1, N_DEV)):
                self._copy(a, k, part(a), self._row(a, k)).wait_recv()
        for a in range(self.m):
            for k in range(1, N_DEV):
                self._copy(a, k, part(a), self.land[a].at[me]).wait_send()
            self._mine(a).wait()
        for a in range(self.m, self.n):
            for k in range(1, N_DEV):
                pl.when(me == 0)(self._copy(a, k, part(a), self._row(a, k)).wait_recv)
                pl.when(me == k)(self._copy(a, k, part(a), self.land[a].at[me]).wait_send)
            pl.when(me == 0)(self._mine(a).wait)


def _exchange(gather, srcs, name):
    ex = _Exchange(gather, srcs)

    def body(*refs):
        ex.bind(refs[:ex.n], refs[ex.n:2 * ex.n], refs[2 * ex.n:])
        ex.issue()
        ex.finish()

    return pl.pallas_call(body, name=name, out_shape=ex.out_shape, in_specs=ex.specs, out_specs=ex.specs,
                          scratch_shapes=ex.scratch)(*ex.srcs)


def _tn_matmul(a, b, name, tka=512, tn=1024, ts=2048):
    s_len, ka = a.shape
    n = b.shape[1]
    tka, tn, ts = min(tka, ka), min(tn, n), min(ts, s_len)
    assert ka % tka == 0 and n % tn == 0 and s_len % ts == 0

    def body(a_ref, b_ref, o_ref):
        _acc(o_ref, _dot_tn(a_ref[...], b_ref[...].astype(BF16)), pl.program_id(2) == 0)

    return pl.pallas_call(
        body, name=name, grid=(ka // tka, n // tn, s_len // ts),
        in_specs=[pl.BlockSpec((ts, tka), lambda i, j, s: (s, i)),
                  pl.BlockSpec((ts, tn), lambda i, j, s: (s, j))],
        out_specs=pl.BlockSpec((tka, tn), lambda i, j, s: (i, j)),
        out_shape=jax.ShapeDtypeStruct((ka, n), F32),
        compiler_params=_cparams(dimension_semantics=("parallel", "parallel", "arbitrary")),
    )(a, b)


def _tn_matmul_sb(a, dq, dk, dv, name, tka=512, ts=2048):
    s_len, ka = a.shape
    tka, ts = min(tka, ka), min(ts, s_len)
    assert ka % tka == 0 and s_len % ts == 0

    def body(a_ref, q_ref, k_ref, v_ref, o_ref):
        first = pl.program_id(1) == 0
        av = a_ref[...]
        for c, val in enumerate((q_ref[...], k_ref[...] * LN2, v_ref[...])):
            _acc(o_ref.at[:, c * SB_WIDTH:(c + 1) * SB_WIDTH], _dot_tn(av, val.astype(BF16)), first)

    colb = pl.BlockSpec((ts, SB_WIDTH), lambda i, s: (s, 0))
    return pl.pallas_call(
        body, name=name, grid=(ka // tka, s_len // ts),
        in_specs=[pl.BlockSpec((ts, tka), lambda i, s: (s, i)), colb, colb, colb],
        out_specs=pl.BlockSpec((tka, 3 * SB_WIDTH), lambda i, s: (i, 0)),
        out_shape=jax.ShapeDtypeStruct((ka, 3 * SB_WIDTH), F32),
        compiler_params=_cparams(48, dimension_semantics=("parallel", "arbitrary")),
    )(a, dq, dk, dv)


def _row_block(s_len):
    return min(s_len, 512)


def _in_proj(x, g, w):
    s_len = x.shape[0]
    tm = _row_block(s_len)

    def body(x_ref, g_ref, w_ref, h_ref, lat_ref, sb_ref, gate_ref):
        h, _, _ = _rms(x_ref[...], g_ref[...])
        hb = h.astype(BF16)
        h_ref[...] = hb
        p = _dot(hb, w_ref[:, 0:1024])
        lat_ref[...] = p[:, 0:512]
        sb_ref[:, 0:512] = (p[:, 512:1024] * SB_Q_FOLD).astype(BF16)
        sb_ref[:, 512:1536] = _dot(hb, w_ref[:, 1024:2048]).astype(BF16)
        gate_ref[:, 0:1024] = _dot(hb, w_ref[:, 2048:3072])
        gate_ref[:, 1024:2048] = _dot(hb, w_ref[:, 3072:4096])

    rowb = lambda n: pl.BlockSpec((tm, n), lambda i: (i, 0))
    return pl.pallas_call(
        body, name="in_proj", grid=(s_len // tm,),
        in_specs=[rowb(D_MODEL), pl.BlockSpec((1, D_MODEL), lambda i: (0, 0)),
                  pl.BlockSpec((D_MODEL, D_IN_PAD), lambda i: (0, 0))],
        out_specs=[rowb(D_MODEL), rowb(512), rowb(3 * SB_WIDTH), rowb(2 * D_MODEL)],
        out_shape=[jax.ShapeDtypeStruct((s_len, D_MODEL), BF16),
                   jax.ShapeDtypeStruct((s_len, 512), F32),
                   jax.ShapeDtypeStruct((s_len, 3 * SB_WIDTH), BF16),
                   jax.ShapeDtypeStruct((s_len, 2 * D_MODEL), F32)],
        compiler_params=_cparams(48, dimension_semantics=("parallel",)),
    )(x, g, w)


def _rope_rot(blk, lane):
    return jnp.where(lane < 80, -pltpu.roll(blk, 112, 1), pltpu.roll(blk, 16, 1))


def _rope_rot_t(blk, lane):
    return jnp.where(lane < 80, pltpu.roll(blk, 112, 1), -pltpu.roll(blk, 16, 1))


def _mla_prep(lat, g_q, g_kv, w_uq, w_uk, w_uv, w_uv1, cosf, sinf):
    s_len = lat.shape[0]
    tm = _row_block(s_len)

    def body(lat_ref, gq_ref, gkv_ref, wuq_ref, wuk_ref, wuv_ref, wuv1_ref, cos_ref, sin_ref,
             q_ref, k_ref, v_ref, v1_ref, ql_ref, kvl_ref):
        lane = lax.broadcasted_iota(jnp.int32, (tm, LANES), 1)
        cosv, sinv = cos_ref[...], sin_ref[...]
        ql, _, _ = _rms(lat_ref[:, 0:256], gq_ref[...])
        kvl, _, _ = _rms(lat_ref[:, 256:384], gkv_ref[...])
        qlb, kvlb = ql.astype(BF16), kvl.astype(BF16)
        ql_ref[...] = qlb
        kvl_ref[...] = kvlb
        q = _dot(qlb, wuq_ref[...])
        kn = _dot(kvlb, wuk_ref[...])
        v_ref[...] = _dot(kvlb, wuv_ref[...]).astype(BF16)
        v1 = _dot(kvlb, wuv1_ref[...])
        wide = lax.broadcasted_iota(jnp.int32, (tm, MLA_HEADS * HEAD_PAD), 1)
        v1_ref[...] = jnp.where(wide % HEAD_PAD == MLA_NOPE, 1.0, v1).astype(BF16)
        kr = pltpu.roll(lat_ref[:, K_R_OFF:K_R_OFF + LANES], 64, 1)
        kr = kr * cosv + _rope_rot(kr, lane) * sinv
        for h in range(MLA_HEADS):
            sl = slice(h * HEAD_PAD, (h + 1) * HEAD_PAD)
            blk = q[:, sl]
            q_ref[:, sl] = ((blk * cosv + _rope_rot(blk, lane) * sinv) * MLA_Q_FOLD).astype(BF16)
            k_ref[:, sl] = (kn[:, sl] + kr).astype(BF16)

    full = lambda shape: pl.BlockSpec(shape, lambda i: (0, 0))
    rowb = lambda n: pl.BlockSpec((tm, n), lambda i: (i, 0))
    return pl.pallas_call(
        body, name="mla_prep", grid=(s_len // tm,),
        in_specs=[rowb(512), full((1, 256)), full((1, 128)), full((256, 1024)), full((128, 1024)),
                  full((128, 512)), full((128, 1024)), rowb(128), rowb(128)],
        out_specs=[rowb(1024), rowb(1024), rowb(512), rowb(1024), rowb(256), rowb(128)],
        out_shape=[jax.ShapeDtypeStruct((s_len, 1024), BF16), jax.ShapeDtypeStruct((s_len, 1024), BF16),
                   jax.ShapeDtypeStruct((s_len, 512), BF16), jax.ShapeDtypeStruct((s_len, 1024), BF16),
                   jax.ShapeDtypeStruct((s_len, 256), BF16), jax.ShapeDtypeStruct((s_len, 128), BF16)],
        compiler_params=_cparams(dimension_semantics=("parallel",)),
    )(lat, g_q, g_kv, w_uq, w_uk, w_uv, w_uv1, cosf, sinf)


ATTN_TQ = 1024
ATTN_TH = 512
ATTN_TK = 256
SB_TQ = 512
SB_TH = 256
MLA_TK = 512
MLA_FWD_TK = 1024


def _attn_blocks(s_len, tk=ATTN_TK, tq=ATTN_TQ, th=ATTN_TH):
    tq, th, tk = min(s_len, tq), min(s_len, th), min(s_len, tk)
    return tq, th, tk, tq // tk


def _chains(tq, th):
    return [(hh, r0) for hh in range(2) for r0 in range(0, tq, th)]


def _diag_mask(th, tk, r0, sub, strict):
    lo, hi = sub * tk, (sub + 1) * tk - 1
    last, first = r0 + th - 1, r0
    if (lo >= last) if strict else (lo > last):
        return "none"
    if (hi < first) if strict else (hi <= first):
        return "all"
    row = lax.broadcasted_iota(jnp.int32, (th, tk), 0) + r0
    col = lax.broadcasted_iota(jnp.int32, (th, tk), 1) + lo
    return col < row if strict else col <= row


def _mla_fwd(q, k, v, ride):
    s_len = q.shape[0]
    tq, th, tk, nsub = _attn_blocks(s_len, MLA_FWD_TK)
    chains = _chains(tq, th)
    nh = tq // th
    nq = s_len // tq

    def body(q_ref, k_ref, v_ref, *rest):
        o_ref, lse_ref = rest[ride.n:ride.n + 2]
        ride.bind(rest[:ride.n], rest[ride.n + 2:2 * ride.n + 2], rest[2 * ride.n + 2:])
        pl.when((pl.program_id(0) == 0) & (pl.program_id(1) == 0))(ride.issue)
        i = pl.program_id(1)
        lane = lax.broadcasted_iota(jnp.int32, (th, LANES), 1)
        hsl = [slice(hh * HEAD_PAD, (hh + 1) * HEAD_PAD) for hh in range(2)]

        def step(kb, carry, sub):
            rows = pl.ds(pl.multiple_of(kb * tk, tk), tk)
            masks = ["all" if sub is None else _diag_mask(th, tk, r0, sub, strict=False) for _, r0 in chains]
            live = [n for n, m in enumerate(masks) if not (isinstance(m, str) and m == "none")]
            s = {n: _dot_nt(q_ref[chains[n][1]:chains[n][1] + th, hsl[chains[n][0]]], k_ref[rows, hsl[chains[n][0]]])
                 for n in live}
            new = list(carry)
            pb, alpha = {}, {}
            for n in live:
                m = carry[n][0]
                sn = s[n]
                if not isinstance(masks[n], str):
                    sn = jnp.where(masks[n], sn, NEG_BIG)
                m_new = jnp.maximum(m, jnp.max(sn, axis=-1, keepdims=True))
                alpha[n] = jnp.exp2(m - m_new)
                pb[n] = jnp.exp2(sn - m_new).astype(BF16)
                new[n] = (m_new, None)
            pv = {n: _dot(pb[n], v_ref[rows, hsl[chains[n][0]]]) for n in live}
            for n in live:
                new[n] = (new[n][0], alpha[n] * carry[n][1] + pv[n])
            return tuple(new)

        init = (jnp.full((th, 1), NEG_BIG, F32), jnp.zeros((th, LANES), F32))
        carry = lax.fori_loop(0, i * nsub, lambda kb, cy: step(kb, cy, None), (init,) * len(chains))
        for sub in range(nsub):
            carry = step(i * nsub + sub, carry, sub)
        for c in range(nh):
            (m0, a0), (m1, a1) = carry[c], carry[nh + c]
            l0, l1 = a0[:, MLA_NOPE:MLA_NOPE + 1], a1[:, MLA_NOPE:MLA_NOPE + 1]
            rs = slice(c * th, (c + 1) * th)
            o_ref[rs, :] = jnp.where(lane < 64, a0 / l0, pltpu.roll(a1 / l1, 64, 1)).astype(BF16)
            lse_ref[rs, :] = jnp.where(lane < 64, m0 + jnp.log2(l0), m1 + jnp.log2(l1))
        pl.when((pl.program_id(0) == 3) & (i == nq - 1))(ride.finish)

    outs = pl.pallas_call(
        body, name="mla_fwd", grid=(4, nq),
        in_specs=[pl.BlockSpec((tq, 2 * HEAD_PAD), lambda p, i: (i, p)),
                  pl.BlockSpec((s_len, 2 * HEAD_PAD), lambda p, i: (0, p)),
                  pl.BlockSpec((s_len, 2 * HEAD_PAD), lambda p, i: (0, p))] + ride.specs,
        out_specs=[pl.BlockSpec((tq, LANES), lambda p, i: (i, p)),
                   pl.BlockSpec((None, tq, LANES), lambda p, i: (p, i, 0))] + ride.specs,
        out_shape=[jax.ShapeDtypeStruct((s_len, 512), BF16),
                   jax.ShapeDtypeStruct((4, s_len, LANES), F32)] + ride.out_shape,
        scratch_shapes=ride.scratch,
        compiler_params=_cparams(40, dimension_semantics=("arbitrary", "arbitrary")),
    )(q, k, v, *ride.srcs)
    return outs[0], outs[1], outs[2:]


def _mla_bwd(q, k, v, o, do, lse, ride):
    s_len = q.shape[0]
    tq, th, tk, nsub = _attn_blocks(s_len, MLA_TK)
    chains = _chains(tq, th)
    nq = s_len // tq

    def body(q_ref, k_ref, v_ref, o_ref, do_ref, lse_ref, *rest):
        dq_ref, dk_ref, dv_ref = rest[ride.n:ride.n + 3]
        ride.bind(rest[:ride.n], rest[ride.n + 3:2 * ride.n + 3], rest[2 * ride.n + 3:])
        pl.when((pl.program_id(0) == 0) & (pl.program_id(1) == 0))(ride.issue)
        i = pl.program_id(1)
        lane = lax.broadcasted_iota(jnp.int32, (th, LANES), 1)

        @pl.when(i == 0)
        def _():
            dk_ref[...] = jnp.zeros_like(dk_ref)
            dv_ref[...] = jnp.zeros_like(dv_ref)

        hsl = [slice(hh * HEAD_PAD, (hh + 1) * HEAD_PAD) for hh in range(2)]
        qs, dos, deltas, lses = [], [], [], []
        for hh, r0 in chains:
            rs = slice(r0, r0 + th)
            qs.append(q_ref[rs, hsl[hh]])
            doh = jnp.where((lane // 64) == hh, do_ref[rs, :], jnp.zeros((), BF16))
            dos.append(doh)
            deltas.append(jnp.sum(doh.astype(F32) * o_ref[rs, :].astype(F32), axis=-1, keepdims=True))
            lses.append(lse_ref[rs, 64 * hh:64 * hh + 1])

        def step(kb, dqs, sub):
            rows = pl.ds(pl.multiple_of(kb * tk, tk), tk)
            vblk = v_ref[rows, :]
            new, p_all, do_all = [], [], []
            ds_h, q_h = [[], []], [[], []]
            for c, (hh, r0) in enumerate(chains):
                mask = "all" if sub is None else _diag_mask(th, tk, r0, sub, strict=False)
                if isinstance(mask, str) and mask == "none":
                    new.append(dqs[c])
                    continue
                kblk = k_ref[rows, hsl[hh]]
                s = _dot_nt(qs[c], kblk)
                if not isinstance(mask, str):
                    s = jnp.where(mask, s, NEG_BIG)
                p = jnp.exp2(s - lses[c])
                dp = _dot_nt(dos[c], vblk)
                ds = (p * (dp - deltas[c]) * MLA_SCALE).astype(BF16)
                p_all.append(p.astype(BF16))
                do_all.append(dos[c])
                ds_h[hh].append(ds)
                q_h[hh].append(qs[c])
                new.append(dqs[c] + _dot(ds, kblk))
            dv_ref[rows, :] += _dot_tn(jnp.concatenate(p_all, axis=0), jnp.concatenate(do_all, axis=0))
            for hh in range(2):
                dk_ref[rows, hsl[hh]] += _dot_tn(jnp.concatenate(ds_h[hh], axis=0),
                                                 jnp.concatenate(q_h[hh], axis=0))
            return tuple(new)

        zero = jnp.zeros((th, LANES), F32)
        dqs = lax.fori_loop(0, i * nsub, lambda kb, cy: step(kb, cy, None), (zero,) * len(chains))
        for sub in range(nsub):
            dqs = step(i * nsub + sub, dqs, sub)
        for c, (hh, r0) in enumerate(chains):
            dq_ref[r0:r0 + th, hsl[hh]] = dqs[c]
        pl.when((pl.program_id(0) == 3) & (i == nq - 1))(ride.finish)

    outs = pl.pallas_call(
        body, name="mla_bwd", grid=(4, nq),
        in_specs=[pl.BlockSpec((tq, 2 * HEAD_PAD), lambda p, i: (i, p)),
                  pl.BlockSpec((s_len, 2 * HEAD_PAD), lambda p, i: (0, p)),
                  pl.BlockSpec((s_len, LANES), lambda p, i: (0, p)),
                  pl.BlockSpec((tq, LANES), lambda p, i: (i, p)),
                  pl.BlockSpec((tq, LANES), lambda p, i: (i, p)),
                  pl.BlockSpec((None, tq, LANES), lambda p, i: (p, i, 0))] + ride.specs,
        out_specs=[pl.BlockSpec((tq, 2 * HEAD_PAD), lambda p, i: (i, p)),
                   pl.BlockSpec((s_len, 2 * HEAD_PAD), lambda p, i: (0, p)),
                   pl.BlockSpec((s_len, LANES), lambda p, i: (0, p))] + ride.specs,
        out_shape=[jax.ShapeDtypeStruct((s_len, 1024), F32), jax.ShapeDtypeStruct((s_len, 1024), F32),
                   jax.ShapeDtypeStruct((s_len, 512), F32)] + ride.out_shape,
        scratch_shapes=ride.scratch,
        compiler_params=_cparams(56, dimension_semantics=("arbitrary", "arbitrary")),
    )(q, k, v, o, do, lse, *ride.srcs)
    return outs[0], outs[1], outs[2], outs[3:]


def _log_sigmoids(z2):
    sp = jnp.log2(1.0 + jnp.exp2(-jnp.abs(z2)))
    lb = jnp.minimum(z2, 0.0) - sp
    return lb, lb - z2


def _split_dot(x, w, parts, nt=False):
    dot = _dot_nt if nt else _dot
    out = None
    for _ in range(parts):
        xb = x.astype(BF16)
        t = dot(xb, w)
        out = t if out is None else out + t
        x = x - xb.astype(F32)
    return out


def _sb_fwd(sb):
    s_len = sb.shape[0]
    tq, th, tk, nsub = _attn_blocks(s_len, tq=SB_TQ, th=SB_TH)
    chains = _chains(tq, th)
    nh = tq // th
    assert s_len // tk <= 64

    def body(q_ref, k_ref, v_ref, o_ref, r_ref):
        i = pl.program_id(1)
        lane = lax.broadcasted_iota(jnp.int32, (th, LANES), 1)
        upper = (lax.broadcasted_iota(jnp.int32, (tk, tk), 0)
                 > lax.broadcasted_iota(jnp.int32, (tk, tk), 1)).astype(BF16)
        qs = [jnp.where((lane // 64) == hh, q_ref[r0:r0 + th, :], jnp.zeros((), BF16)) for hh, r0 in chains]

        def visit(work, carry):
            blocks = {n: (k_ref[pl.ds(pl.multiple_of(kb * tk, tk), tk), :], v_ref[pl.ds(pl.multiple_of(kb * tk, tk), tk), :])
                      for n, kb, _ in work}
            masked = {n: not isinstance(mask, str) for n, _, mask in work}
            masks = {n: mask for n, _, mask in work}
            z = {n: _dot_nt(qs[n], blocks[n][0]) for n, _, _ in work}
            lb, lom = {}, {}
            for n, _, _ in work:
                lb[n], lom[n] = _log_sigmoids(z[n])
                if masked[n]:
                    lom[n] = jnp.where(masks[n], lom[n], 0.0)
            suf = {n: _split_dot(lom[n], upper, 1) for n, _, _ in work}
            a = {}
            for n, _, _ in work:
                a[n] = jnp.exp2(lb[n] + suf[n] + carry[n][0])
                if masked[n]:
                    a[n] = jnp.where(masks[n], a[n], 0.0)
            pv = {n: _dot(a[n].astype(BF16), blocks[n][1]) for n, _, _ in work}
            new = list(carry)
            for n, kb, _ in work:
                c, acc, r = carry[n]
                rs = suf[n][:, 0:1] + lom[n][:, 0:1]
                new[n] = (c + rs, acc + pv[n], jnp.where(lane == 64 * chains[n][0] + kb, rs, r))
            return tuple(new)

        init = (jnp.zeros((th, 1), F32), jnp.zeros((th, LANES), F32), jnp.zeros((th, LANES), F32))
        carry = (init,) * len(chains)
        spb = th // tk
        halves = [[n for n, (_, r0) in enumerate(chains) if r0 == h * th] for h in range(nh)]
        for s in reversed(range(spb)):
            carry = visit([(n, i * nsub + h * spb + s, _diag_mask(th, tk, h * th, h * spb + s, strict=True))
                           for h, mine in enumerate(halves) for n in mine], carry)

        for h, mine in enumerate(halves):
            start = i * nsub + h * spb

            def spent(cy, mine=mine):
                top = functools.reduce(jnp.maximum, [jnp.max(cy[n][0]) for n in mine])
                return (top < SB_CUT).astype(jnp.int32)

            def walk(state, mine=mine, start=start, spent=spent):
                t, _, cy = state
                cy = visit([(n, start - 1 - t, "all") for n in mine], cy)
                return t + 1, spent(cy), cy

            _, _, carry = lax.while_loop(lambda st, start=start: (st[0] < start) & (st[1] == 0), walk,
                                         (jnp.int32(0), spent(carry), carry))
        for n in range(nh):
            rs = slice(n * th, (n + 1) * th)
            o_ref[rs, :] = jnp.where(lane < 64, carry[n][1], carry[nh + n][1]).astype(BF16)
            r_ref[rs, :] = jnp.where(lane < 64, carry[n][2], carry[nh + n][2])

    return pl.pallas_call(
        body, name="sb_fwd", grid=(4, s_len // tq),
        in_specs=[pl.BlockSpec((tq, LANES), lambda p, i: (i, p)),
                  pl.BlockSpec((s_len, LANES), lambda p, i: (0, 4 + p)),
                  pl.BlockSpec((s_len, LANES), lambda p, i: (0, 8 + p))],
        out_specs=[pl.BlockSpec((tq, LANES), lambda p, i: (i, p)),
                   pl.BlockSpec((None, tq, LANES), lambda p, i: (p, i, 0))],
        out_shape=[jax.ShapeDtypeStruct((s_len, 512), BF16), jax.ShapeDtypeStruct((4, s_len, LANES), F32)],
        compiler_params=_cparams(40, dimension_semantics=("parallel", "arbitrary")),
    )(sb, sb, sb)


def _sb_bwd(sb, do, r):
    s_len = sb.shape[0]
    tq, th, tk, nsub = _attn_blocks(s_len)
    chains = _chains(tq, th)
    nh = tq // th

    def body(q_ref, k_ref, v_ref, do_ref, r_ref, dq_ref, dk_ref, dv_ref):
        i = pl.program_id(1)
        lane = lax.broadcasted_iota(jnp.int32, (th, LANES), 1)
        upper = (lax.broadcasted_iota(jnp.int32, (tk, tk), 0)
                 > lax.broadcasted_iota(jnp.int32, (tk, tk), 1)).astype(BF16)
        tri = (lax.broadcasted_iota(jnp.int32, (LANES, LANES), 0)
               > lax.broadcasted_iota(jnp.int32, (LANES, LANES), 1)).astype(BF16)

        @pl.when(i == 0)
        def _():
            dk_ref[...] = jnp.zeros_like(dk_ref)
            dv_ref[...] = jnp.zeros_like(dv_ref)

        qs, dos, rights = [], [], []
        for hh, r0 in chains:
            rs = slice(r0, r0 + th)
            hm = (lane // 64) == hh
            qs.append(jnp.where(hm, q_ref[rs, :], jnp.zeros((), BF16)))
            dos.append(jnp.where(hm, do_ref[rs, :], jnp.zeros((), BF16)))
            rights.append(_split_dot(jnp.where(hm, r_ref[rs, :], 0.0), tri, 3))

        def visit(groups, carry):
            new = list(carry)
            for kb, members in groups:
                rows = pl.ds(pl.multiple_of(kb * tk, tk), tk)
                kblk, vblk = k_ref[rows, :], v_ref[rows, :]
                a_all, do_all, dz_all, q_all = [], [], [], []
                for n, mask in members:
                    pre, dq = carry[n]
                    c = jnp.sum(jnp.where(lane == 64 * chains[n][0] + kb, rights[n], 0.0), axis=-1, keepdims=True)
                    z = _dot_nt(qs[n], kblk)
                    lb, lom = _log_sigmoids(z)
                    if not isinstance(mask, str):
                        lom = jnp.where(mask, lom, 0.0)
                    suf = _split_dot(lom, upper, 1)
                    a = jnp.exp2(lb + suf + c)
                    if not isinstance(mask, str):
                        a = jnp.where(mask, a, 0.0)
                    g = a * _dot_nt(dos[n], vblk)
                    left = _split_dot(g, upper, 1, nt=True) + pre
                    sig = jnp.exp2(lb)
                    dz = g * (1.0 - sig) - sig * left
                    if not isinstance(mask, str):
                        dz = jnp.where(mask, dz, 0.0)
                    dzb = dz.astype(BF16)
                    a_all.append(a.astype(BF16))
                    do_all.append(dos[n])
                    dz_all.append(dzb)
                    q_all.append(qs[n])
                    new[n] = (left[:, tk - 1:tk] + g[:, tk - 1:tk], dq + _dot(dzb, kblk))
                dv_ref[rows, :] += _dot_tn(jnp.concatenate(a_all, axis=0), jnp.concatenate(do_all, axis=0))
                dk_ref[rows, :] += _dot_tn(jnp.concatenate(dz_all, axis=0), jnp.concatenate(q_all, axis=0))
            return tuple(new)

        spb = th // tk
        lane1 = lax.broadcasted_iota(jnp.int32, (1, LANES), 1)
        init = (jnp.zeros((th, 1), F32), jnp.zeros((th, LANES), F32))
        carry = (init,) * len(chains)
        halves = [[n for n, (_, r0) in enumerate(chains) if r0 == h * th] for h in range(nh)]
        for h, mine in enumerate(halves):
            end = i * nsub + h * spb
            first = end
            for n in mine:
                top = jnp.max(rights[n], axis=0, keepdims=True)
                kb_of = lane1 - 64 * chains[n][0]
                live = (kb_of >= 0) & (kb_of < end) & (top >= SB_CUT)
                first = jnp.minimum(first, jnp.min(jnp.where(live, kb_of, end)))
            carry = lax.fori_loop(
                first, end, lambda kb, cy, mine=mine: visit([(kb, [(n, "all") for n in mine])], cy), carry)
        for s in range(spb):
            carry = visit([(i * nsub + h * spb + s,
                            [(n, _diag_mask(th, tk, h * th, h * spb + s, strict=True)) for n in mine])
                           for h, mine in enumerate(halves)], carry)
        for n in range(nh):
            dq_ref[n * th:(n + 1) * th, :] = jnp.where(lane < 64, carry[n][1], carry[nh + n][1]) * SB_SCALE

    return pl.pallas_call(
        body, name="sb_bwd", grid=(4, s_len // tq),
        in_specs=[pl.BlockSpec((tq, LANES), lambda p, i: (i, p)),
                  pl.BlockSpec((s_len, LANES), lambda p, i: (0, 4 + p)),
                  pl.BlockSpec((s_len, LANES), lambda p, i: (0, 8 + p)),
                  pl.BlockSpec((tq, LANES), lambda p, i: (i, p)),
                  pl.BlockSpec((None, tq, LANES), lambda p, i: (p, i, 0))],
        out_specs=[pl.BlockSpec((tq, LANES), lambda p, i: (i, p)),
                   pl.BlockSpec((s_len, LANES), lambda p, i: (0, p)),
                   pl.BlockSpec((s_len, LANES), lambda p, i: (0, p))],
        out_shape=[jax.ShapeDtypeStruct((s_len, 512), F32)] * 3,
        compiler_params=_cparams(48, dimension_semantics=("arbitrary", "arbitrary")),
    )(sb, sb, sb, do, r)


def _merge_fwd(x, oa, ob, gates, bg, wa, wb, wo):
    s_len = x.shape[0]
    tm = _row_block(s_len)

    def body(x_ref, oa_ref, ob_ref, g_ref, bg_ref, wa_ref, wb_ref, wo_ref, y_ref):
        pa = _dot(oa_ref[...], wa_ref[...])
        pb = _dot(ob_ref[...], wb_ref[...])
        merged = (_sigmoid(g_ref[:, 0:D_MODEL] + bg_ref[0:1, :]) * pa
                  + _sigmoid(g_ref[:, D_MODEL:2 * D_MODEL] + bg_ref[1:2, :]) * pb)
        y_ref[...] = x_ref[...] + _dot(merged.astype(BF16), wo_ref[...])

    full = lambda shape: pl.BlockSpec(shape, lambda i: (0, 0))
    rowb = lambda n: pl.BlockSpec((tm, n), lambda i: (i, 0))
    return pl.pallas_call(
        body, name="merge_fwd", grid=(s_len // tm,),
        in_specs=[rowb(1024), rowb(512), rowb(512), rowb(2048), full((2, 1024)), full((512, 1024)),
                  full((512, 1024)), full((1024, 1024))],
        out_specs=rowb(1024),
        out_shape=jax.ShapeDtypeStruct((s_len, D_MODEL), F32),
        compiler_params=_cparams(48, dimension_semantics=("parallel",)),
    )(x, oa, ob, gates, bg, wa, wb, wo)


def _merge_bwd(dx1, oa, ob, gates, bg, wa, wb, wo):
    s_len = dx1.shape[0]
    tm = _row_block(s_len)

    def body(dx_ref, oa_ref, ob_ref, g_ref, bg_ref, wa_ref, wb_ref, wo_ref,
             doa_ref, dob_ref, dgate_ref, dpa_ref, dpb_ref, merged_ref, dxb_ref, dbg_ref):
        first = pl.program_id(0) == 0
        dxb = dx_ref[...].astype(BF16)
        dxb_ref[...] = dxb
        pa = _dot(oa_ref[...], wa_ref[...])
        pb = _dot(ob_ref[...], wb_ref[...])
        sa = _sigmoid(g_ref[:, 0:D_MODEL] + bg_ref[0:1, :])
        sbg = _sigmoid(g_ref[:, D_MODEL:2 * D_MODEL] + bg_ref[1:2, :])
        merged_ref[...] = (sa * pa + sbg * pb).astype(BF16)
        dm = _dot_nt(dxb, wo_ref[...])
        dpa = (dm * sa).astype(BF16)
        dpb = (dm * sbg).astype(BF16)
        dpa_ref[...] = dpa
        dpb_ref[...] = dpb
        dga = dm * pa * sa * (1.0 - sa)
        dgb = dm * pb * sbg * (1.0 - sbg)
        dgate_ref[:, 0:D_MODEL] = dga.astype(BF16)
        dgate_ref[:, D_MODEL:2 * D_MODEL] = dgb.astype(BF16)
        _acc_rows(dbg_ref.at[0:1, :], dga, first)
        _acc_rows(dbg_ref.at[1:2, :], dgb, first)
        doa_ref[...] = _dot_nt(dpa, wa_ref[...]).astype(BF16)
        dob_ref[...] = _dot_nt(dpb, wb_ref[...]).astype(BF16)

    full = lambda shape: pl.BlockSpec(shape, lambda i: (0, 0))
    rowb = lambda n: pl.BlockSpec((tm, n), lambda i: (i, 0))
    sds = lambda n, dt: jax.ShapeDtypeStruct((s_len, n), dt)
    return pl.pallas_call(
        body, name="merge_bwd", grid=(s_len // tm,),
        in_specs=[rowb(1024), rowb(512), rowb(512), rowb(2048), full((2, 1024)), full((512, 1024)),
                  full((512, 1024)), full((1024, 1024))],
        out_specs=[rowb(512), rowb(512), rowb(2048), rowb(1024), rowb(1024), rowb(1024), rowb(1024),
                   full((2, 1024))],
        out_shape=[sds(512, BF16), sds(512, BF16), sds(2048, BF16), sds(1024, BF16), sds(1024, BF16),
                   sds(1024, BF16), sds(1024, BF16), jax.ShapeDtypeStruct((2, 1024), F32)],
        compiler_params=_cparams(48, dimension_semantics=("arbitrary",)),
    )(dx1, oa, ob, gates, bg, wa, wb, wo)


def _mem_kv(mem, g, w):
    m_len = mem.shape[0]

    def body(mem_ref, g_ref, w_ref, mn_ref, kv_ref):
        mn, _, _ = _rms(mem_ref[...], g_ref[...])
        mnb = mn.astype(BF16)
        mn_ref[...] = mnb
        kv_ref[...] = _dot(mnb, w_ref[...]).astype(BF16)

    return pl.pallas_call(
        body, name="mem_kv",
        out_shape=[jax.ShapeDtypeStruct((m_len, D_MODEL), BF16), jax.ShapeDtypeStruct((m_len, 1024), BF16)],
    )(mem, g, w)


def _mem_bwd(mem, g, w, mn, dkv):
    def body(mem_ref, g_ref, w_ref, mn_ref, dkv_ref, dw_ref, dg_ref):
        dkvb = dkv_ref[...].astype(BF16)
        dw_ref[...] = _dot_tn(mn_ref[...], dkvb)
        dmn = _dot_nt(dkvb, w_ref[...])
        _, xh, _ = _rms(mem_ref[...], g_ref[...])
        dg_ref[...] = jnp.sum(dmn * xh, axis=0, keepdims=True)

    return pl.pallas_call(
        body, name="mem_bwd",
        out_shape=[jax.ShapeDtypeStruct((D_MODEL, 1024), F32), jax.ShapeDtypeStruct((1, D_MODEL), F32)],
    )(mem, g, w, mn, dkv)


def _xattn_heads(xqb, kv_ref, m_len):
    ps = []
    for h in range(X_HEADS):
        hs = slice(h * X_HEAD_DIM, (h + 1) * X_HEAD_DIM)
        s = _dot_nt(xqb[:, hs], kv_ref[:, hs]) * X_SCALE
        e = jnp.exp(s - jnp.max(s, axis=-1, keepdims=True))
        ps.append(e / jnp.sum(e, axis=-1, keepdims=True))
    return ps


def _xattn_fwd(x1, g, wxq, kv, wxo):
    s_len, m_len = x1.shape[0], kv.shape[0]
    tm = _row_block(s_len)

    def body(x_ref, g_ref, wq_ref, kv_ref, wo_ref, y_ref):
        hx, _, _ = _rms(x_ref[...], g_ref[...])
        xqb = _dot(hx.astype(BF16), wq_ref[...]).astype(BF16)
        ps = _xattn_heads(xqb, kv_ref, m_len)
        xo = jnp.concatenate(
            [_dot(ps[h].astype(BF16), kv_ref[:, 512 + h * X_HEAD_DIM:512 + (h + 1) * X_HEAD_DIM])
             for h in range(X_HEADS)], axis=-1)
        y_ref[...] = x_ref[...] + _dot(xo.astype(BF16), wo_ref[...])

    full = lambda shape: pl.BlockSpec(shape, lambda i: (0, 0))
    rowb = lambda n: pl.BlockSpec((tm, n), lambda i: (i, 0))
    return pl.pallas_call(
        body, name="xattn_fwd", grid=(s_len // tm,),
        in_specs=[rowb(1024), full((1, 1024)), full((1024, 512)), full((m_len, 1024)), full((512, 1024))],
        out_specs=rowb(1024),
        out_shape=jax.ShapeDtypeStruct((s_len, D_MODEL), F32),
        compiler_params=_cparams(48, dimension_semantics=("parallel",)),
    )(x1, g, wxq, kv, wxo)


def _xattn_bwd(x1, dx2, g, wxq, kv, wxo):
    s_len, m_len = x1.shape[0], kv.shape[0]
    tm = _row_block(s_len)

    def body(x_ref, dy_ref, g_ref, wq_ref, kv_ref, wo_ref, dx_ref, dwq_ref, dwo_ref, dkv_ref, dg_ref):
        first = pl.program_id(0) == 0
        gv = g_ref[...]
        hx, xh, r = _rms(x_ref[...], gv)
        hxb = hx.astype(BF16)
        xqb = _dot(hxb, wq_ref[...]).astype(BF16)
        ps = _xattn_heads(xqb, kv_ref, m_len)
        dy = dy_ref[...]
        dyb = dy.astype(BF16)
        dxo = _dot_nt(dyb, wo_ref[...])
        xos, dqs, dks, dvs = [], [], [], []
        for h in range(X_HEADS):
            hs = slice(h * X_HEAD_DIM, (h + 1) * X_HEAD_DIM)
            vs = slice(512 + h * X_HEAD_DIM, 512 + (h + 1) * X_HEAD_DIM)
            p = ps[h]
            pb = p.astype(BF16)
            dxoh = dxo[:, hs].astype(BF16)
            xos.append(_dot(pb, kv_ref[:, vs]))
            dp = _dot_nt(dxoh, kv_ref[:, vs])
            ds = (p * (dp - jnp.sum(dp * p, axis=-1, keepdims=True)) * X_SCALE).astype(BF16)
            dvs.append(_dot_tn(pb, dxoh))
            dks.append(_dot_tn(ds, xqb[:, hs]))
            dqs.append(_dot(ds, kv_ref[:, hs]))
        xob = jnp.concatenate(xos, axis=-1).astype(BF16)
        dxqb = jnp.concatenate(dqs, axis=-1).astype(BF16)
        _acc(dwo_ref, _dot_tn(xob, dyb), first)
        _acc(dwq_ref, _dot_tn(hxb, dxqb), first)
        _acc(dkv_ref, jnp.concatenate(dks + dvs, axis=-1), first)
        dhx = _dot_nt(dxqb, wq_ref[...])
        dx, dgr = _rms_bwd(dhx, xh, r, gv)
        dx_ref[...] = dy + dx
        _acc_rows(dg_ref, dgr, first)

    full = lambda shape: pl.BlockSpec(shape, lambda i: (0, 0))
    rowb = lambda n: pl.BlockSpec((tm, n), lambda i: (i, 0))
    return pl.pallas_call(
        body, name="xattn_bwd", grid=(s_len // tm,),
        in_specs=[rowb(1024), rowb(1024), full((1, 1024)), full((1024, 512)), full((m_len, 1024)),
                  full((512, 1024))],
        out_specs=[rowb(1024), full((1024, 512)), full((512, 1024)), full((m_len, 1024)), full((1, 1024))],
        out_shape=[jax.ShapeDtypeStruct((s_len, D_MODEL), F32), jax.ShapeDtypeStruct((1024, 512), F32),
                   jax.ShapeDtypeStruct((512, 1024), F32), jax.ShapeDtypeStruct((m_len, 1024), F32),
                   jax.ShapeDtypeStruct((1, D_MODEL), F32)],
        compiler_params=_cparams(48, dimension_semantics=("arbitrary",)),
    )(x1, dx2, g, wxq, kv, wxo)


FF_TILE = 1408


def _ffn_fwd(x2, g, wg, wu, wd):
    s_len = x2.shape[0]
    tm, tf = _row_block(s_len), FF_TILE

    def body(x_ref, g_ref, wg_ref, wu_ref, wd_ref, y_ref, h_ref):
        j = pl.program_id(1)

        @pl.when(j == 0)
        def _():
            hf, _, _ = _rms(x_ref[...], g_ref[...])
            h_ref[...] = hf.astype(BF16)
            y_ref[...] = x_ref[...]

        hb = h_ref[...]
        gt = _dot(hb, wg_ref[...])
        up = _dot(hb, wu_ref[...])
        act = gt * _sigmoid(gt) * up
        y_ref[...] += _dot(act.astype(BF16), wd_ref[...])

    rowb = pl.BlockSpec((tm, D_MODEL), lambda i, j: (i, 0))
    return pl.pallas_call(
        body, name="ffn_fwd", grid=(s_len // tm, D_FF // tf),
        in_specs=[rowb, pl.BlockSpec((1, D_MODEL), lambda i, j: (0, 0)),
                  pl.BlockSpec((D_MODEL, tf), lambda i, j: (0, j)),
                  pl.BlockSpec((D_MODEL, tf), lambda i, j: (0, j)),
                  pl.BlockSpec((tf, D_MODEL), lambda i, j: (j, 0))],
        out_specs=[rowb, rowb],
        out_shape=[jax.ShapeDtypeStruct((s_len, D_MODEL), F32), jax.ShapeDtypeStruct((s_len, D_MODEL), BF16)],
        compiler_params=_cparams(48, dimension_semantics=("parallel", "arbitrary")),
    )(x2, g, wg, wu, wd)


def _ffn_bwd(x2, hf, dx3, dx3b, g, wg, wu, wd):
    s_len = x2.shape[0]
    tm, tf = _row_block(s_len), FF_TILE
    nf = D_FF // tf

    def act_body(h_ref, dy_ref, wg_ref, wu_ref, wd_ref, dgt_ref, dup_ref, act_ref):
        hb = h_ref[...]
        gt = _dot(hb, wg_ref[...])
        up = _dot(hb, wu_ref[...])
        sg = _sigmoid(gt)
        silu = gt * sg
        dact = _dot_nt(dy_ref[...], wd_ref[...])
        dgt_ref[...] = (dact * up * (sg * (1.0 + gt * (1.0 - sg)))).astype(BF16)
        dup_ref[...] = (dact * silu).astype(BF16)
        act_ref[...] = (silu * up).astype(BF16)

    rowb = pl.BlockSpec((tm, D_MODEL), lambda i, j: (i, 0))
    ffb = pl.BlockSpec((tm, tf), lambda i, j: (i, j))
    dgt, dup, act = pl.pallas_call(
        act_body, name="ffn_bwd_act", grid=(s_len // tm, nf),
        in_specs=[rowb, rowb,
                  pl.BlockSpec((D_MODEL, tf), lambda i, j: (0, j)),
                  pl.BlockSpec((D_MODEL, tf), lambda i, j: (0, j)),
                  pl.BlockSpec((tf, D_MODEL), lambda i, j: (j, 0))],
        out_specs=[ffb, ffb, ffb],
        out_shape=[jax.ShapeDtypeStruct((s_len, D_FF), BF16)] * 3,
        compiler_params=_cparams(56, dimension_semantics=("parallel", "arbitrary")),
    )(hf, dx3b, wg, wu, wd)
    tm = min(s_len, 256)

    def in_body(x_ref, dy_ref, g_ref, wg_ref, wu_ref, dgt_ref, dup_ref, dx_ref, dg_ref):
        dh = _dot_nt(dgt_ref[...], wg_ref[...]) + _dot_nt(dup_ref[...], wu_ref[...])
        gv = g_ref[...]
        _, xh, r = _rms(x_ref[...], gv)
        dx, dgr = _rms_bwd(dh, xh, r, gv)
        dx_ref[...] = dy_ref[...] + dx
        _acc_rows(dg_ref, dgr, pl.program_id(0) == 0)

    row1 = lambda n: pl.BlockSpec((tm, n), lambda i: (i, 0))
    full = lambda shape: pl.BlockSpec(shape, lambda i: (0, 0))
    dx2, dg = pl.pallas_call(
        in_body, name="ffn_bwd_in", grid=(s_len // tm,),
        in_specs=[row1(D_MODEL), row1(D_MODEL), full((1, D_MODEL)), full((D_MODEL, D_FF)), full((D_MODEL, D_FF)),
                  row1(D_FF), row1(D_FF)],
        out_specs=[row1(D_MODEL), full((1, D_MODEL))],
        out_shape=[jax.ShapeDtypeStruct((s_len, D_MODEL), F32), jax.ShapeDtypeStruct((1, D_MODEL), F32)],
        compiler_params=_cparams(48, dimension_semantics=("arbitrary",)),
    )(x2, dx3, g, wg, wu, dgt, dup)
    return dx2, dgt, dup, act, dg


def _loss_head(x3, g, target):
    s_len = x3.shape[0]
    tm = _row_block(s_len)

    def body(x_ref, g_ref, t_ref, sse_ref, dx_ref, dxb_ref, dg_ref):
        first = pl.program_id(0) == 0
        gv = g_ref[...]
        y, xh, r = _rms(x_ref[...], gv)
        err = y - t_ref[...]
        _acc(sse_ref, jnp.broadcast_to(jnp.sum(err * err), (8, LANES)), first)
        dx, dgr = _rms_bwd(err * (1.0 / D_MODEL), xh, r, gv)
        dx_ref[...] = dx
        dxb_ref[...] = dx.astype(BF16)
        _acc_rows(dg_ref, dgr, first)

    rowb = pl.BlockSpec((tm, D_MODEL), lambda i: (i, 0))
    return pl.pallas_call(
        body, name="loss_head", grid=(s_len // tm,),
        in_specs=[rowb, pl.BlockSpec((1, D_MODEL), lambda i: (0, 0)), rowb],
        out_specs=[pl.BlockSpec((8, LANES), lambda i: (0, 0)), rowb, rowb,
                   pl.BlockSpec((1, D_MODEL), lambda i: (0, 0))],
        out_shape=[jax.ShapeDtypeStruct((8, LANES), F32), jax.ShapeDtypeStruct((s_len, D_MODEL), F32),
                   jax.ShapeDtypeStruct((s_len, D_MODEL), BF16), jax.ShapeDtypeStruct((1, D_MODEL), F32)],
        compiler_params=_cparams(dimension_semantics=("arbitrary",)),
    )(x3, g, target)


def _mla_prep_bwd(lat, g_q, g_kv, w_uq, w_uk, w_uv, cosf, sinf, dq, dk, dv):
    s_len = lat.shape[0]
    tm = _row_block(s_len)

    def body(lat_ref, gq_ref, gkv_ref, wuq_ref, wuk_ref, wuv_ref, cos_ref, sin_ref, dq_ref, dk_ref, dv_ref,
             dlat_ref, dqb_ref, dkb_ref, dvb_ref, dgq_ref, dgkv_ref):
        first = pl.program_id(0) == 0
        lane = lax.broadcasted_iota(jnp.int32, (tm, LANES), 1)
        cosv, sinv = cos_ref[...], sin_ref[...]
        gq, gkv = gq_ref[...], gkv_ref[...]
        _, qxh, qr = _rms(lat_ref[:, 0:256], gq)
        _, kxh, kr_ = _rms(lat_ref[:, 256:384], gkv)
        dkr = jnp.zeros((tm, LANES), F32)
        for h in range(MLA_HEADS):
            sl = slice(h * HEAD_PAD, (h + 1) * HEAD_PAD)
            blk = dq_ref[:, sl]
            dqb_ref[:, sl] = (blk * cosv + _rope_rot_t(blk, lane) * sinv).astype(BF16)
            kblk = dk_ref[:, sl] * (1.0 / MLA_Q_FOLD)
            dkb_ref[:, sl] = kblk.astype(BF16)
            dkr = dkr + kblk
        dvb = dv_ref[...].astype(BF16)
        dvb_ref[...] = dvb
        dkr = jnp.where((lane >= 64) & (lane < 96), dkr, 0.0)
        dkr = dkr * cosv + _rope_rot_t(dkr, lane) * sinv
        dql = _dot_nt(dqb_ref[...], wuq_ref[...])
        dkvl = _dot_nt(dkb_ref[...], wuk_ref[...]) + _dot_nt(dvb, wuv_ref[...])
        dcq, dgqr = _rms_bwd(dql, qxh, qr, gq)
        dckv, dgkvr = _rms_bwd(dkvl, kxh, kr_, gkv)
        dlat_ref[:, 0:256] = dcq
        dlat_ref[:, 256:384] = dckv
        dlat_ref[:, K_R_OFF:K_R_OFF + LANES] = pltpu.roll(dkr, 64, 1)
        _acc_rows(dgq_ref, dgqr, first)
        _acc_rows(dgkv_ref, dgkvr, first)

    full = lambda shape: pl.BlockSpec(shape, lambda i: (0, 0))
    rowb = lambda n: pl.BlockSpec((tm, n), lambda i: (i, 0))
    sds = lambda n, dt: jax.ShapeDtypeStruct((s_len, n), dt)
    return pl.pallas_call(
        body, name="mla_prep_bwd", grid=(s_len // tm,),
        in_specs=[rowb(512), full((1, 256)), full((1, 128)), full((256, 1024)), full((128, 1024)),
                  full((128, 512)), rowb(128), rowb(128), rowb(1024), rowb(1024), rowb(512)],
        out_specs=[rowb(512), rowb(1024), rowb(1024), rowb(512), full((1, 256)), full((1, 128))],
        out_shape=[sds(512, F32), sds(1024, BF16), sds(1024, BF16), sds(512, BF16),
                   jax.ShapeDtypeStruct((1, 256), F32), jax.ShapeDtypeStruct((1, 128), F32)],
        compiler_params=_cparams(48, dimension_semantics=("arbitrary",)),
    )(lat, g_q, g_kv, w_uq, w_uk, w_uv, cosf, sinf, dq, dk, dv)


def _in_proj_bwd(x, g, w, dx1, dlat, dsbq, dsbk, dsbv, dgates, ride):
    s_len = x.shape[0]
    tm = min(s_len, 256)
    nb = s_len // tm

    def body(x_ref, g_ref, w_ref, dx1_ref, dlat_ref, dq_ref, dk_ref, dv_ref, dgate_ref, *rest):
        gx_ref, dg_ref = rest[ride.n:ride.n + 2]
        dproj = rest[-1]
        ride.bind(rest[:ride.n], rest[ride.n + 2:2 * ride.n + 2], rest[2 * ride.n + 2:-1])
        pl.when(pl.program_id(0) == 0)(ride.issue)
        dproj[:, 0:512] = dlat_ref[...].astype(BF16)
        dproj[:, 512:1024] = dq_ref[...].astype(BF16)
        dproj[:, 1024:1536] = (dk_ref[...] * LN2).astype(BF16)
        dproj[:, 1536:2048] = dv_ref[...].astype(BF16)
        dproj[:, 2048:4096] = dgate_ref[...]
        dh = _dot_nt(dproj[...], w_ref[...])
        gv = g_ref[...]
        _, xh, r = _rms(x_ref[...], gv)
        dx, dgr = _rms_bwd(dh, xh, r, gv)
        gx_ref[...] = dx1_ref[...] + dx
        _acc_rows(dg_ref, dgr, pl.program_id(0) == 0)
        pl.when(pl.program_id(0) == nb - 1)(ride.finish)

    rowb = lambda n: pl.BlockSpec((tm, n), lambda i: (i, 0))
    full = lambda shape: pl.BlockSpec(shape, lambda i: (0, 0))
    outs = pl.pallas_call(
        body, name="in_proj_bwd", grid=(nb,),
        in_specs=[rowb(D_MODEL), full((1, D_MODEL)), full((D_MODEL, D_IN_PAD)), rowb(D_MODEL),
                  rowb(512), rowb(512), rowb(512), rowb(512), rowb(2 * D_MODEL)] + ride.specs,
        out_specs=[rowb(D_MODEL), full((1, D_MODEL))] + ride.specs,
        out_shape=[jax.ShapeDtypeStruct((s_len, D_MODEL), F32), jax.ShapeDtypeStruct((1, D_MODEL), F32)]
        + ride.out_shape,
        scratch_shapes=ride.scratch + [pltpu.VMEM((tm, D_IN_PAD), BF16)],
        compiler_params=_cparams(48, dimension_semantics=("arbitrary",)),
    )(x, g, w, dx1, dlat, dsbq, dsbk, dsbv, dgates, *ride.srcs)
    return outs[0], outs[1], outs[2:]


def _adamw(landed, w, m, v, name):
    r, c = w.shape
    lanes = _round_up(c, LANES)
    tb = r
    for cand in range(r, 0, -1):
        if r % cand == 0 and (cand % 8 == 0 or cand == r) and N_DEV * cand * lanes * 4 <= ADAM_BLOCK_BYTES:
            tb = cand
            break
    c1 = 1.0 - ADAM_B1 ** ADAM_STEP
    c2 = 1.0 - ADAM_B2 ** ADAM_STEP

    def body(l_ref, w_ref, m_ref, v_ref, g_ref, d_ref, nm_ref, nv_ref):
        g = l_ref[0]
        for k in range(1, N_DEV):
            g = g + l_ref[k]
        nm = ADAM_B1 * m_ref[...] + (1.0 - ADAM_B1) * g
        nv = ADAM_B2 * v_ref[...] + (1.0 - ADAM_B2) * (g * g)
        g_ref[...] = g
        nm_ref[...] = nm
        nv_ref[...] = nv
        d_ref[...] = -ADAM_LR * ((nm / c1) / (jnp.sqrt(nv / c2) + ADAM_EPS) + ADAM_WD * w_ref[...])

    blk = pl.BlockSpec((tb, c), lambda i: (i, 0))
    return pl.pallas_call(
        body, name=name, grid=(r // tb,),
        in_specs=[pl.BlockSpec((N_DEV, tb, c), lambda i: (0, i, 0)), blk, blk, blk],
        out_specs=[blk, blk, blk, blk],
        out_shape=[jax.ShapeDtypeStruct((r, c), F32)] * 4,
        compiler_params=_cparams(dimension_semantics=("parallel",)),
    )(landed, w, m, v)


def _shard_shape(shape, axis):
    return tuple(d // N_DEV if a == axis else d for a, d in enumerate(shape))


def _split_pieces(full, axis):
    r, c = full.shape
    if axis == 0:
        return full.reshape(N_DEV, r // N_DEV, c)
    return full.reshape(r, N_DEV, c // N_DEV).transpose(1, 0, 2)


def _join_shards(gathered, axis):
    _, r, c = gathered.shape
    if axis == 0:
        return gathered.reshape(N_DEV * r, c)
    return gathered.transpose(1, 0, 2).reshape(r, N_DEV * c)


def kernel(x, mem, positions, g_mix, w_in, b_gate, g_q_lat, w_uq, g_kv_lat, w_ukv, w_a_proj, w_b_proj, w_o, g_x, g_mem, w_xq, w_xkv, w_xo, g_ffn, w_gate, w_up, w_down, g_final, loss_target, m_g_mix, m_w_in, m_b_gate, m_g_q_lat, m_w_uq, m_g_kv_lat, m_w_ukv, m_w_a_proj, m_w_b_proj, m_w_o, m_g_x, m_g_mem, m_w_xq, m_w_xkv, m_w_xo, m_g_ffn, m_w_gate, m_w_up, m_w_down, m_g_final, v_g_mix, v_w_in, v_b_gate, v_g_q_lat, v_w_uq, v_g_kv_lat, v_w_ukv, v_w_a_proj, v_w_b_proj, v_w_o, v_g_x, v_g_mem, v_w_xq, v_w_xkv, v_w_xo, v_g_ffn, v_w_gate, v_w_up, v_w_down, v_g_final):
    given = dict(locals())
    s_len = x.shape[1]
    x2d = x.reshape(s_len, D_MODEL)
    mem2d = mem.reshape(-1, D_MODEL)
    target = loss_target.reshape(s_len, D_MODEL)

    names = [name for name, _, _ in SHARDED]
    axis_of = {name: axis for name, _, axis in SHARDED}
    shard2d = lambda name, prefix="": given[prefix + name].reshape(
        _shard_shape(dict((n, s) for n, s, _ in SHARDED)[name], axis_of[name]))

    wire = lambda name: shard2d(name) if name == "b_gate" else shard2d(name).astype(BF16)
    early = [n for n in names if n in NEEDED_FIRST]
    late = [n for n in names if n not in NEEDED_FIRST]
    gathered = _exchange(True, [wire(n) for n in early], "weights_gather_first")
    wts = {n: _join_shards(g, axis_of[n]) for n, g in zip(early, gathered)}

    w_in_p = jnp.concatenate([wts["w_in"][:, :LAT_COLS], jnp.zeros((D_MODEL, D_IN_PAD - D_IN), BF16),
                              wts["w_in"][:, LAT_COLS:]], axis=1)
    w_uq_p = jnp.pad(wts["w_uq"].reshape(256, MLA_HEADS, 96), ((0, 0), (0, 0), (0, 32))).reshape(256, 1024)
    ukv = wts["w_ukv"].reshape(128, MLA_HEADS, 128)
    w_uk_p = jnp.pad(ukv[:, :, :64], ((0, 0), (0, 0), (0, 64))).reshape(128, 1024)
    w_uv = ukv[:, :, 64:].reshape(128, 512)
    w_uv1 = jnp.pad(ukv[:, :, 64:], ((0, 0), (0, 0), (0, 64))).reshape(128, 1024)
    bg = wts["b_gate"]

    inv_freq = ROPE_THETA ** (-jnp.arange(0, MLA_ROPE, 2, dtype=F32) / MLA_ROPE)
    ang = positions.reshape(s_len).astype(F32)[:, None] * inv_freq
    cos16, sin16 = jnp.cos(ang), jnp.sin(ang)
    cosf = jnp.concatenate([jnp.ones((s_len, 64), F32), cos16, cos16, jnp.ones((s_len, 32), F32)], axis=1)
    sinf = jnp.concatenate([jnp.zeros((s_len, 64), F32), sin16, sin16, jnp.zeros((s_len, 32), F32)], axis=1)

    h1, lat, sb, gates = _in_proj(x2d, g_mix, w_in_p)
    qa, ka, va, va1, q_lat, kv_lat = _mla_prep(lat, g_q_lat, g_kv_lat, w_uq_p, w_uk_p, w_uv, w_uv1, cosf, sinf)
    oa, lse, gathered = _mla_fwd(qa, ka, va1, _Exchange(True, [wire(n) for n in late]))
    wts.update({n: _join_shards(g, axis_of[n]) for n, g in zip(late, gathered)})
    ob, sb_r = _sb_fwd(sb)
    x1 = _merge_fwd(x2d, oa, ob, gates, bg, wts["w_a_proj"], wts["w_b_proj"], wts["w_o"])
    mn, xkv = _mem_kv(mem2d, g_mem, wts["w_xkv"])
    x2 = _xattn_fwd(x1, g_x, wts["w_xq"], xkv, wts["w_xo"])
    x3, hf = _ffn_fwd(x2, g_ffn, wts["w_gate"], wts["w_up"], wts["w_down"])
    g_final2d = g_final.reshape(1, D_MODEL)
    sse, dx3, dx3b, dg_final = _loss_head(x3, g_final2d, target)
    loss = lax.psum(sse[0, 0] * (0.5 / D_MODEL), ("x", "y", "c"))

    dx2, dgt, dup, act, dg_ffn = _ffn_bwd(x2, hf, dx3, dx3b, g_ffn, wts["w_gate"], wts["w_up"], wts["w_down"])
    dx1, dw_xq, dw_xo, dxkv, dg_x = _xattn_bwd(x1, dx2, g_x, wts["w_xq"], xkv, wts["w_xo"])
    dw_xkv, dg_mem = _mem_bwd(mem2d, g_mem, wts["w_xkv"], mn, dxkv)
    doa, dob, dgates, dpa, dpb, merged, dx1b, dbg = _merge_bwd(
        dx1, oa, ob, gates, bg, wts["w_a_proj"], wts["w_b_proj"], wts["w_o"])
    dsbq, dsbk, dsbv = _sb_bwd(sb, dob, sb_r)
    full_grads = {
        "w_a_proj": _tn_matmul(oa, dpa, "dw_a"),
        "w_b_proj": _tn_matmul(ob, dpb, "dw_b"),
        "w_o": _tn_matmul(merged, dx1b, "dw_o"),
        "w_xq": dw_xq,
        "w_xkv": dw_xkv,
        "w_xo": dw_xo,
        "w_gate": _tn_matmul(hf, dgt, "dw_gate", tn=FF_TILE),
        "w_up": _tn_matmul(hf, dup, "dw_up", tn=FF_TILE),
        "w_down": _tn_matmul(act, dx3b, "dw_down", tka=FF_TILE),
    }
    dw_sb = _tn_matmul_sb(h1, dsbq, dsbk, dsbv, "dw_in_sb")
    own0 = D_IN // N_DEV - LAT_COLS
    w_in_rest = _split_pieces(
        jnp.concatenate([dw_sb[:, :LAT_COLS], dw_sb, _tn_matmul(h1, dgates, "dw_in_gates")], axis=1), 1)
    dqa, dka, dva, got = _mla_bwd(
        qa, ka, va, oa, doa, lse,
        _Exchange(False, [_split_pieces(full_grads[n], axis_of[n]) for n in late] + [w_in_rest]))
    landed = dict(zip(late, got[:-1]))
    dlat, dqb, dkb, dvb, dg_q, dg_kv = _mla_prep_bwd(
        lat, g_q_lat, g_kv_lat, w_uq_p, w_uk_p, w_uv, cosf, sinf, dqa, dka, dva)
    w_in_first = jnp.concatenate([_tn_matmul(h1, dlat, "dw_in_lat")[:, :LAT_COLS], dw_sb[:, :own0]], axis=1)
    dw_uq_p = _tn_matmul(q_lat, dqb, "dw_uq")
    dw_uk_p = _tn_matmul(kv_lat, dkb, "dw_uk")
    dw_uv = _tn_matmul(kv_lat, dvb, "dw_uv")
    full_grads.update({
        "b_gate": dbg,
        "w_uq": dw_uq_p.reshape(256, MLA_HEADS, 128)[:, :, :96].reshape(256, 768),
        "w_ukv": jnp.concatenate([dw_uk_p.reshape(128, MLA_HEADS, 128)[:, :, :64],
                                  dw_uv.reshape(128, MLA_HEADS, 64)], axis=2).reshape(128, 1024),
    })
    small = [n for n in early if n != "w_in"]
    grad_x, dg_mix, got2 = _in_proj_bwd(
        x2d, g_mix, w_in_p, dx1, dlat, dsbq, dsbk, dsbv, dgates,
        _Exchange(False, [_split_pieces(full_grads[n], axis_of[n]) for n in small], to_first=[w_in_first]))
    landed.update(zip(small, got2[:-1]))
    me = 4 * lax.axis_index("x") + 2 * lax.axis_index("y") + lax.axis_index("c")
    landed["w_in"] = jnp.where(me == 0, got2[-1], got[-1])
    rep_grads = {"g_mix": dg_mix, "g_q_lat": dg_q, "g_kv_lat": dg_kv, "g_x": dg_x, "g_mem": dg_mem,
                 "g_ffn": dg_ffn, "g_final": dg_final}
    rep_cat = lambda prefix, src: jnp.concatenate(
        [src[prefix + n].reshape(-1) for n, _ in REPLICATED]).reshape(-1, LANES)
    rep_src = jnp.broadcast_to(rep_cat("", rep_grads), (N_DEV,) + rep_cat("", rep_grads).shape)
    rep_landed = _exchange(False, [rep_src], "grads_gains")[0]

    res = {}
    for name, _, _ in SHARDED:
        outs = _adamw(landed[name], shard2d(name), shard2d(name, "m_"), shard2d(name, "v_"), "adamw_" + name)
        res[name] = [o.reshape(given[name].shape) for o in outs]
    rep_outs = _adamw(rep_landed, rep_cat("", given), rep_cat("m_", given), rep_cat("v_", given), "adamw_gains")
    off = 0
    for name, n in REPLICATED:
        res[name] = [o.reshape(-1)[off:off + n].reshape(given[name].shape) for o in rep_outs]
        off += n
    result = [loss, grad_x.reshape(x.shape)]
    for k in range(4):
        result.extend(res[name][k] for name in WEIGHT_ORDER)
    return tuple(result)
```

```python
import functools
import math

import jax
import jax.numpy as jnp
from jax import lax
from jax.experimental import pallas as pl
from jax.experimental.pallas import tpu as pltpu

F32 = jnp.float32
BF16 = jnp.bfloat16

D_MODEL = 1024
MLA_HEADS = 8
MLA_Q_RANK = 256
MLA_KV_RANK = 128
MLA_NOPE = 64
MLA_ROPE = 32
ROPE_THETA = 10000.0
SB_WIDTH = 512
X_HEADS = 4
X_HEAD_DIM = 128
D_FF = 2816
EPS = 1e-6
D_IN = 4000
D_IN_PAD = 4096
K_R_OFF = 384
LAT_COLS = 416
LANES = 128
HEAD_PAD = 128
MLA_SCALE = 1.0 / math.sqrt(MLA_NOPE + MLA_ROPE)
SB_SCALE = 0.125
LOG2E = math.log2(math.e)
LN2 = math.log(2.0)
MLA_Q_FOLD = MLA_SCALE * LOG2E
SB_Q_FOLD = SB_SCALE * LOG2E
SB_CUT = -160.0
X_SCALE = 1.0 / math.sqrt(X_HEAD_DIM)
NEG_BIG = -1e30

ADAM_LR = 0.001
ADAM_B1 = 0.9
ADAM_B2 = 0.999
ADAM_EPS = 1e-08
ADAM_WD = 0.01
ADAM_STEP = 10

N_DEV = 8
MIB = 1024 * 1024
ADAM_BLOCK_BYTES = 4 * MIB

SHARDED = (
    ("w_in", (D_MODEL, D_IN), 1),
    ("b_gate", (2, D_MODEL), 1),
    ("w_uq", (MLA_Q_RANK, 768), 1),
    ("w_ukv", (MLA_KV_RANK, 1024), 1),
    ("w_a_proj", (512, D_MODEL), 1),
    ("w_b_proj", (512, D_MODEL), 1),
    ("w_o", (D_MODEL, D_MODEL), 0),
    ("w_xq", (D_MODEL, 512), 0),
    ("w_xkv", (D_MODEL, 1024), 0),
    ("w_xo", (512, D_MODEL), 1),
    ("w_gate", (D_MODEL, D_FF), 1),
    ("w_up", (D_MODEL, D_FF), 1),
    ("w_down", (D_FF, D_MODEL), 0),
)
NEEDED_FIRST = ("w_in", "b_gate", "w_uq", "w_ukv")
REPLICATED = (
    ("g_mix", 1024), ("g_q_lat", 256), ("g_kv_lat", 128), ("g_x", 1024),
    ("g_mem", 1024), ("g_ffn", 1024), ("g_final", 1024),
)
WEIGHT_ORDER = ("g_mix", "w_in", "b_gate", "g_q_lat", "w_uq", "g_kv_lat", "w_ukv", "w_a_proj",
                "w_b_proj", "w_o", "g_x", "g_mem", "w_xq", "w_xkv", "w_xo", "g_ffn", "w_gate",
                "w_up", "w_down", "g_final")


def _round_up(n, m):
    return -(-n // m) * m


def _cparams(vmem_mib=None, **kw):
    if vmem_mib is not None:
        kw["vmem_limit_bytes"] = vmem_mib * MIB
    return pltpu.CompilerParams(**kw)


def _dot(a, b):
    return jnp.dot(a, b, preferred_element_type=F32)


def _dot_nt(a, b):
    return lax.dot_general(a, b, (((1,), (1,)), ((), ())), preferred_element_type=F32)


def _dot_tn(a, b):
    return lax.dot_general(a, b, (((0,), (0,)), ((), ())), preferred_element_type=F32)


def _rms(x, g):
    r = lax.rsqrt(jnp.mean(x * x, axis=-1, keepdims=True) + EPS)
    xh = x * r
    return xh * g, xh, r


def _rms_bwd(dy, xh, r, g):
    u = dy * g
    dx = r * (u - xh * jnp.mean(u * xh, axis=-1, keepdims=True))
    return dx, dy * xh


def _sigmoid(z):
    return 1.0 / (1.0 + jnp.exp(-z))


def _acc_rows(ref, val, first):
    s = jnp.sum(val, axis=0, keepdims=True)

    @pl.when(first)
    def _():
        ref[...] = s

    @pl.when(jnp.logical_not(first))
    def _():
        ref[...] += s


def _acc(ref, val, first):
    @pl.when(first)
    def _():
        ref[...] = val

    @pl.when(jnp.logical_not(first))
    def _():
        ref[...] += val


def _peer(k):
    x, y, c = lax.axis_index("x"), lax.axis_index("y"), lax.axis_index("c")
    px = 1 - x if (k >> 2) & 1 else x
    py = 1 - y if (k >> 1) & 1 else y
    pc = 1 - c if k & 1 else c
    return (px, py, pc), 4 * px + 2 * py + pc


N_PEERS = N_DEV - 1
OTHER_CHIPS = (2, 4, 6)


def _land_shape(gather, src):
    return (N_DEV,) + src.shape if gather else src.shape


class _Exchange:
    def __init__(self, gather, srcs, to_first=()):
        self.gather, self.m, self.srcs = gather, len(srcs), list(srcs) + list(to_first)
        self.n = len(self.srcs)
        self.out_shape = ([jax.ShapeDtypeStruct(_land_shape(gather, s), s.dtype) for s in srcs]
                          + [jax.ShapeDtypeStruct(_land_shape(True, s), s.dtype) for s in to_first])
        self.specs = [pl.BlockSpec(memory_space=pl.ANY)] * self.n
        self.scratch = [pltpu.SemaphoreType.DMA((self.n * N_PEERS,)), pltpu.SemaphoreType.DMA((self.n * N_PEERS,)),
                        pltpu.SemaphoreType.DMA((self.n,))]

    def bind(self, src, land, sems):
        self.src, self.land = src, land
        self.send_sems, self.recv_sems, self.local_sems = sems

    def _copy(self, a, k, source, to, target=1):
        return pltpu.make_async_remote_copy(
            src_ref=source, dst_ref=to,
            send_sem=self.send_sems.at[a * N_PEERS + k - 1], recv_sem=self.recv_sems.at[a * N_PEERS + k - 1],
            device_id=_peer(target)[0], device_id_type=pl.DeviceIdType.MESH)

    def _row(self, a, k):
        return self.land[a].at[_peer(k)[1]]

    def _mine(self, a):
        me = _peer(0)[1]
        whole = self.gather or a >= self.m
        return pltpu.make_async_copy(self.src[a] if whole else self.src[a].at[me], self.land[a].at[me],
                                     self.local_sems.at[a])

    def issue(self):
        me = _peer(0)[1]
        for a in range(self.m):
            self._mine(a).start()
            for k in ((1,) + OTHER_CHIPS if self.gather else range(1, N_DEV)):
                source = self.src[a] if self.gather else self.src[a].at[_peer(k)[1]]
                self._copy(a, k, source, self.land[a].at[me], target=k).start()
        for a in range(self.m, self.n):
            pl.when(me == 0)(self._mine(a).start)
            for k in range(1, N_DEV):
                pl.when(me == k)(self._copy(a, k, self.src[a], self.land[a].at[me], target=k).start)

    def finish(self):
        me = _peer(0)[1]
        part = lambda a: self.src[a] if self.gather or a >= self.m else self.src[a].at[me]
        if self.gather:
            for a in range(self.m):
                for k in OTHER_CHIPS:
                    self._copy(a, k, part(a), self._row(a, k)).wait_recv()
                    self._copy(a, k + 1, self._row(a, k), self._row(a, k), target=1).start()
        for a in range(self.m):
            for k in ((1, 3, 5, 7) if self.gather else range(1, N_DEV)):
                self._copy(a, k, part(a), self._row(a, k)).wait_recv()
        for a in range(self.m):
            for k in range(1, N_DEV):
                self._copy(a, k, part(a), self.land[a].at[me]).wait_send()
            self._mine(a).wait()
        for a in range(self.m, self.n):
            for k in range(1, N_DEV):
                pl.when(me == 0)(self._copy(a, k, part(a), self._row(a, k)).wait_recv)
                pl.when(me == k)(self._copy(a, k, part(a), self.land[a].at[me]).wait_send)
            pl.when(me == 0)(self._mine(a).wait)


def _exchange(gather, srcs, name):
    ex = _Exchange(gather, srcs)

    def body(*refs):
        ex.bind(refs[:ex.n], refs[ex.n:2 * ex.n], refs[2 * ex.n:])
        ex.issue()
        ex.finish()

    return pl.pallas_call(body, name=name, out_shape=ex.out_shape, in_specs=ex.specs, out_specs=ex.specs,
                          scratch_shapes=ex.scratch)(*ex.srcs)


def _tn_matmul(a, b, name, tka=512, tn=1024, ts=2048):
    s_len, ka = a.shape
    n = b.shape[1]
    tka, tn, ts = min(tka, ka), min(tn, n), min(ts, s_len)
    assert ka % tka == 0 and n % tn == 0 and s_len % ts == 0

    def body(a_ref, b_ref, o_ref):
        _acc(o_ref, _dot_tn(a_ref[...], b_ref[...].astype(BF16)), pl.program_id(2) == 0)

    return pl.pallas_call(
        body, name=name, grid=(ka // tka, n // tn, s_len // ts),
        in_specs=[pl.BlockSpec((ts, tka), lambda i, j, s: (s, i)),
                  pl.BlockSpec((ts, tn), lambda i, j, s: (s, j))],
        out_specs=pl.BlockSpec((tka, tn), lambda i, j, s: (i, j)),
        out_shape=jax.ShapeDtypeStruct((ka, n), F32),
        compiler_params=_cparams(dimension_semantics=("parallel", "parallel", "arbitrary")),
    )(a, b)


def _tn_matmul_sb(a, dq, dk, dv, name, tka=512, ts=2048):
    s_len, ka = a.shape
    tka, ts = min(tka, ka), min(ts, s_len)
    assert ka % tka == 0 and s_len % ts == 0

    def body(a_ref, q_ref, k_ref, v_ref, o_ref):
        first = pl.program_id(1) == 0
        av = a_ref[...]
        for c, val in enumerate((q_ref[...], k_ref[...] * LN2, v_ref[...])):
            _acc(o_ref.at[:, c * SB_WIDTH:(c + 1) * SB_WIDTH], _dot_tn(av, val.astype(BF16)), first)

    colb = pl.BlockSpec((ts, SB_WIDTH), lambda i, s: (s, 0))
    return pl.pallas_call(
        body, name=name, grid=(ka // tka, s_len // ts),
        in_specs=[pl.BlockSpec((ts, tka), lambda i, s: (s, i)), colb, colb, colb],
        out_specs=pl.BlockSpec((tka, 3 * SB_WIDTH), lambda i, s: (i, 0)),
        out_shape=jax.ShapeDtypeStruct((ka, 3 * SB_WIDTH), F32),
        compiler_params=_cparams(48, dimension_semantics=("parallel", "arbitrary")),
    )(a, dq, dk, dv)


def _row_block(s_len):
    return min(s_len, 512)


def _in_proj(x, g, w):
    s_len = x.shape[0]
    tm = _row_block(s_len)

    def body(x_ref, g_ref, w_ref, h_ref, lat_ref, sb_ref, gate_ref):
        h, _, _ = _rms(x_ref[...], g_ref[...])
        hb = h.astype(BF16)
        h_ref[...] = hb
        p = _dot(hb, w_ref[:, 0:1024])
        lat_ref[...] = p[:, 0:512]
        sb_ref[:, 0:512] = (p[:, 512:1024] * SB_Q_FOLD).astype(BF16)
        sb_ref[:, 512:1536] = _dot(hb, w_ref[:, 1024:2048]).astype(BF16)
        gate_ref[:, 0:1024] = _dot(hb, w_ref[:, 2048:3072])
        gate_ref[:, 1024:2048] = _dot(hb, w_ref[:, 3072:4096])

    rowb = lambda n: pl.BlockSpec((tm, n), lambda i: (i, 0))
    return pl.pallas_call(
        body, name="in_proj", grid=(s_len // tm,),
        in_specs=[rowb(D_MODEL), pl.BlockSpec((1, D_MODEL), lambda i: (0, 0)),
                  pl.BlockSpec((D_MODEL, D_IN_PAD), lambda i: (0, 0))],
        out_specs=[rowb(D_MODEL), rowb(512), rowb(3 * SB_WIDTH), rowb(2 * D_MODEL)],
        out_shape=[jax.ShapeDtypeStruct((s_len, D_MODEL), BF16),
                   jax.ShapeDtypeStruct((s_len, 512), F32),
                   jax.ShapeDtypeStruct((s_len, 3 * SB_WIDTH), BF16),
                   jax.ShapeDtypeStruct((s_len, 2 * D_MODEL), F32)],
        compiler_params=_cparams(48, dimension_semantics=("parallel",)),
    )(x, g, w)


def _rope_rot(blk, lane):
    return jnp.where(lane < 80, -pltpu.roll(blk, 112, 1), pltpu.roll(blk, 16, 1))


def _rope_rot_t(blk, lane):
    return jnp.where(lane < 80, pltpu.roll(blk, 112, 1), -pltpu.roll(blk, 16, 1))


def _mla_prep(lat, g_q, g_kv, w_uq, w_uk, w_uv, w_uv1, cosf, sinf):
    s_len = lat.shape[0]
    tm = _row_block(s_len)

    def body(lat_ref, gq_ref, gkv_ref, wuq_ref, wuk_ref, wuv_ref, wuv1_ref, cos_ref, sin_ref,
             q_ref, k_ref, v_ref, v1_ref, ql_ref, kvl_ref):
        lane = lax.broadcasted_iota(jnp.int32, (tm, LANES), 1)
        cosv, sinv = cos_ref[...], sin_ref[...]
        ql, _, _ = _rms(lat_ref[:, 0:256], gq_ref[...])
        kvl, _, _ = _rms(lat_ref[:, 256:384], gkv_ref[...])
        qlb, kvlb = ql.astype(BF16), kvl.astype(BF16)
        ql_ref[...] = qlb
        kvl_ref[...] = kvlb
        q = _dot(qlb, wuq_ref[...])
        kn = _dot(kvlb, wuk_ref[...])
        v_ref[...] = _dot(kvlb, wuv_ref[...]).astype(BF16)
        v1 = _dot(kvlb, wuv1_ref[...])
        wide = lax.broadcasted_iota(jnp.int32, (tm, MLA_HEADS * HEAD_PAD), 1)
        v1_ref[...] = jnp.where(wide % HEAD_PAD == MLA_NOPE, 1.0, v1).astype(BF16)
        kr = pltpu.roll(lat_ref[:, K_R_OFF:K_R_OFF + LANES], 64, 1)
        kr = kr * cosv + _rope_rot(kr, lane) * sinv
        for h in range(MLA_HEADS):
            sl = slice(h * HEAD_PAD, (h + 1) * HEAD_PAD)
            blk = q[:, sl]
            q_ref[:, sl] = ((blk * cosv + _rope_rot(blk, lane) * sinv) * MLA_Q_FOLD).astype(BF16)
            k_ref[:, sl] = (kn[:, sl] + kr).astype(BF16)

    full = lambda shape: pl.BlockSpec(shape, lambda i: (0, 0))
    rowb = lambda n: pl.BlockSpec((tm, n), lambda i: (i, 0))
    return pl.pallas_call(
        body, name="mla_prep", grid=(s_len // tm,),
        in_specs=[rowb(512), full((1, 256)), full((1, 128)), full((256, 1024)), full((128, 1024)),
                  full((128, 512)), full((128, 1024)), rowb(128), rowb(128)],
        out_specs=[rowb(1024), rowb(1024), rowb(512), rowb(1024), rowb(256), rowb(128)],
        out_shape=[jax.ShapeDtypeStruct((s_len, 1024), BF16), jax.ShapeDtypeStruct((s_len, 1024), BF16),
                   jax.ShapeDtypeStruct((s_len, 512), BF16), jax.ShapeDtypeStruct((s_len, 1024), BF16),
                   jax.ShapeDtypeStruct((s_len, 256), BF16), jax.ShapeDtypeStruct((s_len, 128), BF16)],
        compiler_params=_cparams(dimension_semantics=("parallel",)),
    )(lat, g_q, g_kv, w_uq, w_uk, w_uv, w_uv1, cosf, sinf)


ATTN_TQ = 1024
ATTN_TH = 512
ATTN_TK = 256
SB_TQ = 512
SB_TH = 256
MLA_TK = 512
MLA_FWD_TK = 1024


def _attn_blocks(s_len, tk=ATTN_TK, tq=ATTN_TQ, th=ATTN_TH):
    tq, th, tk = min(s_len, tq), min(s_len, th), min(s_len, tk)
    return tq, th, tk, tq // tk


def _chains(tq, th):
    return [(hh, r0) for hh in range(2) for r0 in range(0, tq, th)]


def _diag_mask(th, tk, r0, sub, strict):
    lo, hi = sub * tk, (sub + 1) * tk - 1
    last, first = r0 + th - 1, r0
    if (lo >= last) if strict else (lo > last):
        return "none"
    if (hi < first) if strict else (hi <= first):
        return "all"
    row = lax.broadcasted_iota(jnp.int32, (th, tk), 0) + r0
    col = lax.broadcasted_iota(jnp.int32, (th, tk), 1) + lo
    return col < row if strict else col <= row


def _mla_fwd(q, k, v, ride):
    s_len = q.shape[0]
    tq, th, tk, nsub = _attn_blocks(s_len, MLA_FWD_TK)
    chains = _chains(tq, th)
    nh = tq // th
    nq = s_len // tq

    def body(q_ref, k_ref, v_ref, *rest):
        o_ref, lse_ref = rest[ride.n:ride.n + 2]
        ride.bind(rest[:ride.n], rest[ride.n + 2:2 * ride.n + 2], rest[2 * ride.n + 2:])
        pl.when((pl.program_id(0) == 0) & (pl.program_id(1) == 0))(ride.issue)
        i = pl.program_id(1)
        lane = lax.broadcasted_iota(jnp.int32, (th, LANES), 1)
        hsl = [slice(hh * HEAD_PAD, (hh + 1) * HEAD_PAD) for hh in range(2)]

        def step(kb, carry, sub):
            rows = pl.ds(pl.multiple_of(kb * tk, tk), tk)
            masks = ["all" if sub is None else _diag_mask(th, tk, r0, sub, strict=False) for _, r0 in chains]
            live = [n for n, m in enumerate(masks) if not (isinstance(m, str) and m == "none")]
            s = {n: _dot_nt(q_ref[chains[n][1]:chains[n][1] + th, hsl[chains[n][0]]], k_ref[rows, hsl[chains[n][0]]])
                 for n in live}
            new = list(carry)
            pb, alpha = {}, {}
            for n in live:
                m = carry[n][0]
                sn = s[n]
                if not isinstance(masks[n], str):
                    sn = jnp.where(masks[n], sn, NEG_BIG)
                m_new = jnp.maximum(m, jnp.max(sn, axis=-1, keepdims=True))
                alpha[n] = jnp.exp2(m - m_new)
                pb[n] = jnp.exp2(sn - m_new).astype(BF16)
                new[n] = (m_new, None)
            pv = {n: _dot(pb[n], v_ref[rows, hsl[chains[n][0]]]) for n in live}
            for n in live:
                new[n] = (new[n][0], alpha[n] * carry[n][1] + pv[n])
            return tuple(new)

        init = (jnp.full((th, 1), NEG_BIG, F32), jnp.zeros((th, LANES), F32))
        carry = lax.fori_loop(0, i * nsub, lambda kb, cy: step(kb, cy, None), (init,) * len(chains))
        for sub in range(nsub):
            carry = step(i * nsub + sub, carry, sub)
        for c in range(nh):
            (m0, a0), (m1, a1) = carry[c], carry[nh + c]
            l0, l1 = a0[:, MLA_NOPE:MLA_NOPE + 1], a1[:, MLA_NOPE:MLA_NOPE + 1]
            rs = slice(c * th, (c + 1) * th)
            o_ref[rs, :] = jnp.where(lane < 64, a0 / l0, pltpu.roll(a1 / l1, 64, 1)).astype(BF16)
            lse_ref[rs, :] = jnp.where(lane < 64, m0 + jnp.log2(l0), m1 + jnp.log2(l1))
        pl.when((pl.program_id(0) == 3) & (i == nq - 1))(ride.finish)

    outs = pl.pallas_call(
        body, name="mla_fwd", grid=(4, nq),
        in_specs=[pl.BlockSpec((tq, 2 * HEAD_PAD), lambda p, i: (i, p)),
                  pl.BlockSpec((s_len, 2 * HEAD_PAD), lambda p, i: (0, p)),
                  pl.BlockSpec((s_len, 2 * HEAD_PAD), lambda p, i: (0, p))] + ride.specs,
        out_specs=[pl.BlockSpec((tq, LANES), lambda p, i: (i, p)),
                   pl.BlockSpec((None, tq, LANES), lambda p, i: (p, i, 0))] + ride.specs,
        out_shape=[jax.ShapeDtypeStruct((s_len, 512), BF16),
                   jax.ShapeDtypeStruct((4, s_len, LANES), F32)] + ride.out_shape,
        scratch_shapes=ride.scratch,
        compiler_params=_cparams(40, dimension_semantics=("arbitrary", "arbitrary")),
    )(q, k, v, *ride.srcs)
    return outs[0], outs[1], outs[2:]


def _mla_bwd(q, k, v, o, do, lse, ride):
    s_len = q.shape[0]
    tq, th, tk, nsub = _attn_blocks(s_len, MLA_TK)
    chains = _chains(tq, th)
    nq = s_len // tq

    def body(q_ref, k_ref, v_ref, o_ref, do_ref, lse_ref, *rest):
        dq_ref, dk_ref, dv_ref = rest[ride.n:ride.n + 3]
        ride.bind(rest[:ride.n], rest[ride.n + 3:2 * ride.n + 3], rest[2 * ride.n + 3:])
        pl.when((pl.program_id(0) == 0) & (pl.program_id(1) == 0))(ride.issue)
        i = pl.program_id(1)
        lane = lax.broadcasted_iota(jnp.int32, (th, LANES), 1)

        @pl.when(i == 0)
        def _():
            dk_ref[...] = jnp.zeros_like(dk_ref)
            dv_ref[...] = jnp.zeros_like(dv_ref)

        hsl = [slice(hh * HEAD_PAD, (hh + 1) * HEAD_PAD) for hh in range(2)]
        qs, dos, deltas, lses = [], [], [], []
        for hh, r0 in chains:
            rs = slice(r0, r0 + th)
            qs.append(q_ref[rs, hsl[hh]])
            doh = jnp.where((lane // 64) == hh, do_ref[rs, :], jnp.zeros((), BF16))
            dos.append(doh)
            deltas.append(jnp.sum(doh.astype(F32) * o_ref[rs, :].astype(F32), axis=-1, keepdims=True))
            lses.append(lse_ref[rs, 64 * hh:64 * hh + 1])

        def step(kb, dqs, sub):
            rows = pl.ds(pl.multiple_of(kb * tk, tk), tk)
            vblk = v_ref[rows, :]
            new, p_all, do_all = [], [], []
            ds_h, q_h = [[], []], [[], []]
            for c, (hh, r0) in enumerate(chains):
                mask = "all" if sub is None else _diag_mask(th, tk, r0, sub, strict=False)
                if isinstance(mask, str) and mask == "none":
                    new.append(dqs[c])
                    continue
                kblk = k_ref[rows, hsl[hh]]
                s = _dot_nt(qs[c], kblk)
                if not isinstance(mask, str):
                    s = jnp.where(mask, s, NEG_BIG)
                p = jnp.exp2(s - lses[c])
                dp = _dot_nt(dos[c], vblk)
                ds = (p * (dp - deltas[c]) * MLA_SCALE).astype(BF16)
                p_all.append(p.astype(BF16))
                do_all.append(dos[c])
                ds_h[hh].append(ds)
                q_h[hh].append(qs[c])
                new.append(dqs[c] + _dot(ds, kblk))
            dv_ref[rows, :] += _dot_tn(jnp.concatenate(p_all, axis=0), jnp.concatenate(do_all, axis=0))
            for hh in range(2):
                dk_ref[rows, hsl[hh]] += _dot_tn(jnp.concatenate(ds_h[hh], axis=0),
                                                 jnp.concatenate(q_h[hh], axis=0))
            return tuple(new)

        zero = jnp.zeros((th, LANES), F32)
        dqs = lax.fori_loop(0, i * nsub, lambda kb, cy: step(kb, cy, None), (zero,) * len(chains))
        for sub in range(nsub):
            dqs = step(i * nsub + sub, dqs, sub)
        for c, (hh, r0) in enumerate(chains):
            dq_ref[r0:r0 + th, hsl[hh]] = dqs[c]
        pl.when((pl.program_id(0) == 3) & (i == nq - 1))(ride.finish)

    outs = pl.pallas_call(
        body, name="mla_bwd", grid=(4, nq),
        in_specs=[pl.BlockSpec((tq, 2 * HEAD_PAD), lambda p, i: (i, p)),
                  pl.BlockSpec((s_len, 2 * HEAD_PAD), lambda p, i: (0, p)),
                  pl.BlockSpec((s_len, LANES), lambda p, i: (0, p)),
                  pl.BlockSpec((tq, LANES), lambda p, i: (i, p)),
                  pl.BlockSpec((tq, LANES), lambda p, i: (i, p)),
                  pl.BlockSpec((None, tq, LANES), lambda p, i: (p, i, 0))] + ride.specs,
        out_specs=[pl.BlockSpec((tq, 2 * HEAD_PAD), lambda p, i: (i, p)),
                   pl.BlockSpec((s_len, 2 * HEAD_PAD), lambda p, i: (0, p)),
                   pl.BlockSpec((s_len, LANES), lambda p, i: (0, p))] + ride.specs,
        out_shape=[jax.ShapeDtypeStruct((s_len, 1024), F32), jax.ShapeDtypeStruct((s_len, 1024), F32),
                   jax.ShapeDtypeStruct((s_len, 512), F32)] + ride.out_shape,
        scratch_shapes=ride.scratch,
        compiler_params=_cparams(56, dimension_semantics=("arbitrary", "arbitrary")),
    )(q, k, v, o, do, lse, *ride.srcs)
    return outs[0], outs[1], outs[2], outs[3:]


def _log_sigmoids(z2):
    sp = jnp.log2(1.0 + jnp.exp2(-jnp.abs(z2)))
    lb = jnp.minimum(z2, 0.0) - sp
    return lb, lb - z2


def _split_dot(x, w, parts, nt=False):
    dot = _dot_nt if nt else _dot
    out = None
    for _ in range(parts):
        xb = x.astype(BF16)
        t = dot(xb, w)
        out = t if out is None else out + t
        x = x - xb.astype(F32)
    return out


def _sb_fwd(sb):
    s_len = sb.shape[0]
    tq, th, tk, nsub = _attn_blocks(s_len, tq=SB_TQ, th=SB_TH)
    chains = _chains(tq, th)
    nh = tq // th
    assert s_len // tk <= 64

    def body(q_ref, k_ref, v_ref, o_ref, r_ref):
        i = pl.program_id(1)
        lane = lax.broadcasted_iota(jnp.int32, (th, LANES), 1)
        upper = (lax.broadcasted_iota(jnp.int32, (tk, tk), 0)
                 > lax.broadcasted_iota(jnp.int32, (tk, tk), 1)).astype(BF16)
        qs = [jnp.where((lane // 64) == hh, q_ref[r0:r0 + th, :], jnp.zeros((), BF16)) for hh, r0 in chains]

        def visit(work, carry):
            blocks = {n: (k_ref[pl.ds(pl.multiple_of(kb * tk, tk), tk), :], v_ref[pl.ds(pl.multiple_of(kb * tk, tk), tk), :])
                      for n, kb, _ in work}
            masked = {n: not isinstance(mask, str) for n, _, mask in work}
            masks = {n: mask for n, _, mask in work}
            z = {n: _dot_nt(qs[n], blocks[n][0]) for n, _, _ in work}
            lb, lom = {}, {}
            for n, _, _ in work:
                lb[n], lom[n] = _log_sigmoids(z[n])
                if masked[n]:
                    lom[n] = jnp.where(masks[n], lom[n], 0.0)
            suf = {n: _split_dot(lom[n], upper, 1) for n, _, _ in work}
            a = {}
            for n, _, _ in work:
                a[n] = jnp.exp2(lb[n] + suf[n] + carry[n][0])
                if masked[n]:
                    a[n] = jnp.where(masks[n], a[n], 0.0)
            pv = {n: _dot(a[n].astype(BF16), blocks[n][1]) for n, _, _ in work}
            new = list(carry)
            for n, kb, _ in work:
                c, acc, r = carry[n]
                rs = suf[n][:, 0:1] + lom[n][:, 0:1]
                new[n] = (c + rs, acc + pv[n], jnp.where(lane == 64 * chains[n][0] + kb, rs, r))
            return tuple(new)

        init = (jnp.zeros((th, 1), F32), jnp.zeros((th, LANES), F32), jnp.zeros((th, LANES), F32))
        carry = (init,) * len(chains)
        spb = th // tk
        halves = [[n for n, (_, r0) in enumerate(chains) if r0 == h * th] for h in range(nh)]
        for s in reversed(range(spb)):
            carry = visit([(n, i * nsub + h * spb + s, _diag_mask(th, tk, h * th, h * spb + s, strict=True))
                           for h, mine in enumerate(halves) for n in mine], carry)

        for h, mine in enumerate(halves):
            start = i * nsub + h * spb

            def spent(cy, mine=mine):
                top = functools.reduce(jnp.maximum, [jnp.max(cy[n][0]) for n in mine])
                return (top < SB_CUT).astype(jnp.int32)

            def walk(state, mine=mine, start=start, spent=spent):
                t, _, cy = state
                cy = visit([(n, start - 1 - t, "all") for n in mine], cy)
                return t + 1, spent(cy), cy

            _, _, carry = lax.while_loop(lambda st, start=start: (st[0] < start) & (st[1] == 0), walk,
                                         (jnp.int32(0), spent(carry), carry))
        for n in range(nh):
            rs = slice(n * th, (n + 1) * th)
            o_ref[rs, :] = jnp.where(lane < 64, carry[n][1], carry[nh + n][1]).astype(BF16)
            r_ref[rs, :] = jnp.where(lane < 64, carry[n][2], carry[nh + n][2])

    return pl.pallas_call(
        body, name="sb_fwd", grid=(4, s_len // tq),
        in_specs=[pl.BlockSpec((tq, LANES), lambda p, i: (i, p)),
                  pl.BlockSpec((s_len, LANES), lambda p, i: (0, 4 + p)),
                  pl.BlockSpec((s_len, LANES), lambda p, i: (0, 8 + p))],
        out_specs=[pl.BlockSpec((tq, LANES), lambda p, i: (i, p)),
                   pl.BlockSpec((None, tq, LANES), lambda p, i: (p, i, 0))],
        out_shape=[jax.ShapeDtypeStruct((s_len, 512), BF16), jax.ShapeDtypeStruct((4, s_len, LANES), F32)],
        compiler_params=_cparams(40, dimension_semantics=("parallel", "arbitrary")),
    )(sb, sb, sb)


def _sb_bwd(sb, do, r):
    s_len = sb.shape[0]
    tq, th, tk, nsub = _attn_blocks(s_len)
    chains = _chains(tq, th)
    nh = tq // th

    def body(q_ref, k_ref, v_ref, do_ref, r_ref, dq_ref, dk_ref, dv_ref):
        i = pl.program_id(1)
        lane = lax.broadcasted_iota(jnp.int32, (th, LANES), 1)
        upper = (lax.broadcasted_iota(jnp.int32, (tk, tk), 0)
                 > lax.broadcasted_iota(jnp.int32, (tk, tk), 1)).astype(BF16)
        tri = (lax.broadcasted_iota(jnp.int32, (LANES, LANES), 0)
               > lax.broadcasted_iota(jnp.int32, (LANES, LANES), 1)).astype(BF16)

        @pl.when(i == 0)
        def _():
            dk_ref[...] = jnp.zeros_like(dk_ref)
            dv_ref[...] = jnp.zeros_like(dv_ref)

        qs, dos, rights = [], [], []
        for hh, r0 in chains:
            rs = slice(r0, r0 + th)
            hm = (lane // 64) == hh
            qs.append(jnp.where(hm, q_ref[rs, :], jnp.zeros((), BF16)))
            dos.append(jnp.where(hm, do_ref[rs, :], jnp.zeros((), BF16)))
            rights.append(_split_dot(jnp.where(hm, r_ref[rs, :], 0.0), tri, 3))

        def visit(groups, carry):
            new = list(carry)
            for kb, members in groups:
                rows = pl.ds(pl.multiple_of(kb * tk, tk), tk)
                kblk, vblk = k_ref[rows, :], v_ref[rows, :]
                a_all, do_all, dz_all, q_all = [], [], [], []
                for n, mask in members:
                    pre, dq = carry[n]
                    c = jnp.sum(jnp.where(lane == 64 * chains[n][0] + kb, rights[n], 0.0), axis=-1, keepdims=True)
                    z = _dot_nt(qs[n], kblk)
                    lb, lom = _log_sigmoids(z)
                    if not isinstance(mask, str):
                        lom = jnp.where(mask, lom, 0.0)
                    suf = _split_dot(lom, upper, 1)
                    a = jnp.exp2(lb + suf + c)
                    if not isinstance(mask, str):
                        a = jnp.where(mask, a, 0.0)
                    g = a * _dot_nt(dos[n], vblk)
                    left = _split_dot(g, upper, 1, nt=True) + pre
                    sig = jnp.exp2(lb)
                    dz = g * (1.0 - sig) - sig * left
                    if not isinstance(mask, str):
                        dz = jnp.where(mask, dz, 0.0)
                    dzb = dz.astype(BF16)
                    a_all.append(a.astype(BF16))
                    do_all.append(dos[n])
                    dz_all.append(dzb)
                    q_all.append(qs[n])
                    new[n] = (left[:, tk - 1:tk] + g[:, tk - 1:tk], dq + _dot(dzb, kblk))
                dv_ref[rows, :] += _dot_tn(jnp.concatenate(a_all, axis=0), jnp.concatenate(do_all, axis=0))
                dk_ref[rows, :] += _dot_tn(jnp.concatenate(dz_all, axis=0), jnp.concatenate(q_all, axis=0))
            return tuple(new)

        spb = th // tk
        lane1 = lax.broadcasted_iota(jnp.int32, (1, LANES), 1)
        init = (jnp.zeros((th, 1), F32), jnp.zeros((th, LANES), F32))
        carry = (init,) * len(chains)
        halves = [[n for n, (_, r0) in enumerate(chains) if r0 == h * th] for h in range(nh)]
        for h, mine in enumerate(halves):
            end = i * nsub + h * spb
            first = end
            for n in mine:
                top = jnp.max(rights[n], axis=0, keepdims=True)
                kb_of = lane1 - 64 * chains[n][0]
                live = (kb_of >= 0) & (kb_of < end) & (top >= SB_CUT)
                first = jnp.minimum(first, jnp.min(jnp.where(live, kb_of, end)))
            carry = lax.fori_loop(
                first, end, lambda kb, cy, mine=mine: visit([(kb, [(n, "all") for n in mine])], cy), carry)
        for s in range(spb):
            carry = visit([(i * nsub + h * spb + s,
                            [(n, _diag_mask(th, tk, h * th, h * spb + s, strict=True)) for n in mine])
                           for h, mine in enumerate(halves)], carry)
        for n in range(nh):
            dq_ref[n * th:(n + 1) * th, :] = jnp.where(lane < 64, carry[n][1], carry[nh + n][1]) * SB_SCALE

    return pl.pallas_call(
        body, name="sb_bwd", grid=(4, s_len // tq),
        in_specs=[pl.BlockSpec((tq, LANES), lambda p, i: (i, p)),
                  pl.BlockSpec((s_len, LANES), lambda p, i: (0, 4 + p)),
                  pl.BlockSpec((s_len, LANES), lambda p, i: (0, 8 + p)),
                  pl.BlockSpec((tq, LANES), lambda p, i: (i, p)),
                  pl.BlockSpec((None, tq, LANES), lambda p, i: (p, i, 0))],
        out_specs=[pl.BlockSpec((tq, LANES), lambda p, i: (i, p)),
                   pl.BlockSpec((s_len, LANES), lambda p, i: (0, p)),
                   pl.BlockSpec((s_len, LANES), lambda p, i: (0, p))],
        out_shape=[jax.ShapeDtypeStruct((s_len, 512), F32)] * 3,
        compiler_params=_cparams(48, dimension_semantics=("arbitrary", "arbitrary")),
    )(sb, sb, sb, do, r)


def _merge_fwd(x, oa, ob, gates, bg, wa, wb, wo):
    s_len = x.shape[0]
    tm = _row_block(s_len)

    def body(x_ref, oa_ref, ob_ref, g_ref, bg_ref, wa_ref, wb_ref, wo_ref, y_ref):
        pa = _dot(oa_ref[...], wa_ref[...])
        pb = _dot(ob_ref[...], wb_ref[...])
        merged = (_sigmoid(g_ref[:, 0:D_MODEL] + bg_ref[0:1, :]) * pa
                  + _sigmoid(g_ref[:, D_MODEL:2 * D_MODEL] + bg_ref[1:2, :]) * pb)
        y_ref[...] = x_ref[...] + _dot(merged.astype(BF16), wo_ref[...])

    full = lambda shape: pl.BlockSpec(shape, lambda i: (0, 0))
    rowb = lambda n: pl.BlockSpec((tm, n), lambda i: (i, 0))
    return pl.pallas_call(
        body, name="merge_fwd", grid=(s_len // tm,),
        in_specs=[rowb(1024), rowb(512), rowb(512), rowb(2048), full((2, 1024)), full((512, 1024)),
                  full((512, 1024)), full((1024, 1024))],
        out_specs=rowb(1024),
        out_shape=jax.ShapeDtypeStruct((s_len, D_MODEL), F32),
        compiler_params=_cparams(48, dimension_semantics=("parallel",)),
    )(x, oa, ob, gates, bg, wa, wb, wo)


def _merge_bwd(dx1, oa, ob, gates, bg, wa, wb, wo):
    s_len = dx1.shape[0]
    tm = _row_block(s_len)

    def body(dx_ref, oa_ref, ob_ref, g_ref, bg_ref, wa_ref, wb_ref, wo_ref,
             doa_ref, dob_ref, dgate_ref, dpa_ref, dpb_ref, merged_ref, dxb_ref, dbg_ref):
        first = pl.program_id(0) == 0
        dxb = dx_ref[...].astype(BF16)
        dxb_ref[...] = dxb
        pa = _dot(oa_ref[...], wa_ref[...])
        pb = _dot(ob_ref[...], wb_ref[...])
        sa = _sigmoid(g_ref[:, 0:D_MODEL] + bg_ref[0:1, :])
        sbg = _sigmoid(g_ref[:, D_MODEL:2 * D_MODEL] + bg_ref[1:2, :])
        merged_ref[...] = (sa * pa + sbg * pb).astype(BF16)
        dm = _dot_nt(dxb, wo_ref[...])
        dpa = (dm * sa).astype(BF16)
        dpb = (dm * sbg).astype(BF16)
        dpa_ref[...] = dpa
        dpb_ref[...] = dpb
        dga = dm * pa * sa * (1.0 - sa)
        dgb = dm * pb * sbg * (1.0 - sbg)
        dgate_ref[:, 0:D_MODEL] = dga.astype(BF16)
        dgate_ref[:, D_MODEL:2 * D_MODEL] = dgb.astype(BF16)
        _acc_rows(dbg_ref.at[0:1, :], dga, first)
        _acc_rows(dbg_ref.at[1:2, :], dgb, first)
        doa_ref[...] = _dot_nt(dpa, wa_ref[...]).astype(BF16)
        dob_ref[...] = _dot_nt(dpb, wb_ref[...]).astype(BF16)

    full = lambda shape: pl.BlockSpec(shape, lambda i: (0, 0))
    rowb = lambda n: pl.BlockSpec((tm, n), lambda i: (i, 0))
    sds = lambda n, dt: jax.ShapeDtypeStruct((s_len, n), dt)
    return pl.pallas_call(
        body, name="merge_bwd", grid=(s_len // tm,),
        in_specs=[rowb(1024), rowb(512), rowb(512), rowb(2048), full((2, 1024)), full((512, 1024)),
                  full((512, 1024)), full((1024, 1024))],
        out_specs=[rowb(512), rowb(512), rowb(2048), rowb(1024), rowb(1024), rowb(1024), rowb(1024),
                   full((2, 1024))],
        out_shape=[sds(512, BF16), sds(512, BF16), sds(2048, BF16), sds(1024, BF16), sds(1024, BF16),
                   sds(1024, BF16), sds(1024, BF16), jax.ShapeDtypeStruct((2, 1024), F32)],
        compiler_params=_cparams(48, dimension_semantics=("arbitrary",)),
    )(dx1, oa, ob, gates, bg, wa, wb, wo)


def _mem_kv(mem, g, w):
    m_len = mem.shape[0]

    def body(mem_ref, g_ref, w_ref, mn_ref, kv_ref):
        mn, _, _ = _rms(mem_ref[...], g_ref[...])
        mnb = mn.astype(BF16)
        mn_ref[...] = mnb
        kv_ref[...] = _dot(mnb, w_ref[...]).astype(BF16)

    return pl.pallas_call(
        body, name="mem_kv",
        out_shape=[jax.ShapeDtypeStruct((m_len, D_MODEL), BF16), jax.ShapeDtypeStruct((m_len, 1024), BF16)],
    )(mem, g, w)


def _mem_bwd(mem, g, w, mn, dkv):
    def body(mem_ref, g_ref, w_ref, mn_ref, dkv_ref, dw_ref, dg_ref):
        dkvb = dkv_ref[...].astype(BF16)
        dw_ref[...] = _dot_tn(mn_ref[...], dkvb)
        dmn = _dot_nt(dkvb, w_ref[...])
        _, xh, _ = _rms(mem_ref[...], g_ref[...])
        dg_ref[...] = jnp.sum(dmn * xh, axis=0, keepdims=True)

    return pl.pallas_call(
        body, name="mem_bwd",
        out_shape=[jax.ShapeDtypeStruct((D_MODEL, 1024), F32), jax.ShapeDtypeStruct((1, D_MODEL), F32)],
    )(mem, g, w, mn, dkv)


def _xattn_heads(xqb, kv_ref, m_len):
    ps = []
    for h in range(X_HEADS):
        hs = slice(h * X_HEAD_DIM, (h + 1) * X_HEAD_DIM)
        s = _dot_nt(xqb[:, hs], kv_ref[:, hs]) * X_SCALE
        e = jnp.exp(s - jnp.max(s, axis=-1, keepdims=True))
        ps.append(e / jnp.sum(e, axis=-1, keepdims=True))
    return ps


def _xattn_fwd(x1, g, wxq, kv, wxo):
    s_len, m_len = x1.shape[0], kv.shape[0]
    tm = _row_block(s_len)

    def body(x_ref, g_ref, wq_ref, kv_ref, wo_ref, y_ref):
        hx, _, _ = _rms(x_ref[...], g_ref[...])
        xqb = _dot(hx.astype(BF16), wq_ref[...]).astype(BF16)
        ps = _xattn_heads(xqb, kv_ref, m_len)
        xo = jnp.concatenate(
            [_dot(ps[h].astype(BF16), kv_ref[:, 512 + h * X_HEAD_DIM:512 + (h + 1) * X_HEAD_DIM])
             for h in range(X_HEADS)], axis=-1)
        y_ref[...] = x_ref[...] + _dot(xo.astype(BF16), wo_ref[...])

    full = lambda shape: pl.BlockSpec(shape, lambda i: (0, 0))
    rowb = lambda n: pl.BlockSpec((tm, n), lambda i: (i, 0))
    return pl.pallas_call(
        body, name="xattn_fwd", grid=(s_len // tm,),
        in_specs=[rowb(1024), full((1, 1024)), full((1024, 512)), full((m_len, 1024)), full((512, 1024))],
        out_specs=rowb(1024),
        out_shape=jax.ShapeDtypeStruct((s_len, D_MODEL), F32),
        compiler_params=_cparams(48, dimension_semantics=("parallel",)),
    )(x1, g, wxq, kv, wxo)


def _xattn_bwd(x1, dx2, g, wxq, kv, wxo):
    s_len, m_len = x1.shape[0], kv.shape[0]
    tm = _row_block(s_len)

    def body(x_ref, dy_ref, g_ref, wq_ref, kv_ref, wo_ref, dx_ref, dwq_ref, dwo_ref, dkv_ref, dg_ref):
        first = pl.program_id(0) == 0
        gv = g_ref[...]
        hx, xh, r = _rms(x_ref[...], gv)
        hxb = hx.astype(BF16)
        xqb = _dot(hxb, wq_ref[...]).astype(BF16)
        ps = _xattn_heads(xqb, kv_ref, m_len)
        dy = dy_ref[...]
        dyb = dy.astype(BF16)
        dxo = _dot_nt(dyb, wo_ref[...])
        xos, dqs, dks, dvs = [], [], [], []
        for h in range(X_HEADS):
            hs = slice(h * X_HEAD_DIM, (h + 1) * X_HEAD_DIM)
            vs = slice(512 + h * X_HEAD_DIM, 512 + (h + 1) * X_HEAD_DIM)
            p = ps[h]
            pb = p.astype(BF16)
            dxoh = dxo[:, hs].astype(BF16)
            xos.append(_dot(pb, kv_ref[:, vs]))
            dp = _dot_nt(dxoh, kv_ref[:, vs])
            ds = (p * (dp - jnp.sum(dp * p, axis=-1, keepdims=True)) * X_SCALE).astype(BF16)
            dvs.append(_dot_tn(pb, dxoh))
            dks.append(_dot_tn(ds, xqb[:, hs]))
            dqs.append(_dot(ds, kv_ref[:, hs]))
        xob = jnp.concatenate(xos, axis=-1).astype(BF16)
        dxqb = jnp.concatenate(dqs, axis=-1).astype(BF16)
        _acc(dwo_ref, _dot_tn(xob, dyb), first)
        _acc(dwq_ref, _dot_tn(hxb, dxqb), first)
        _acc(dkv_ref, jnp.concatenate(dks + dvs, axis=-1), first)
        dhx = _dot_nt(dxqb, wq_ref[...])
        dx, dgr = _rms_bwd(dhx, xh, r, gv)
        dx_ref[...] = dy + dx
        _acc_rows(dg_ref, dgr, first)

    full = lambda shape: pl.BlockSpec(shape, lambda i: (0, 0))
    rowb = lambda n: pl.BlockSpec((tm, n), lambda i: (i, 0))
    return pl.pallas_call(
        body, name="xattn_bwd", grid=(s_len // tm,),
        in_specs=[rowb(1024), rowb(1024), full((1, 1024)), full((1024, 512)), full((m_len, 1024)),
                  full((512, 1024))],
        out_specs=[rowb(1024), full((1024, 512)), full((512, 1024)), full((m_len, 1024)), full((1, 1024))],
        out_shape=[jax.ShapeDtypeStruct((s_len, D_MODEL), F32), jax.ShapeDtypeStruct((1024, 512), F32),
                   jax.ShapeDtypeStruct((512, 1024), F32), jax.ShapeDtypeStruct((m_len, 1024), F32),
                   jax.ShapeDtypeStruct((1, D_MODEL), F32)],
        compiler_params=_cparams(48, dimension_semantics=("arbitrary",)),
    )(x1, dx2, g, wxq, kv, wxo)


FF_TILE = 1408


def _ffn_fwd(x2, g, wg, wu, wd):
    s_len = x2.shape[0]
    tm, tf = _row_block(s_len), FF_TILE

    def body(x_ref, g_ref, wg_ref, wu_ref, wd_ref, y_ref, h_ref):
        j = pl.program_id(1)

        @pl.when(j == 0)
        def _():
            hf, _, _ = _rms(x_ref[...], g_ref[...])
            h_ref[...] = hf.astype(BF16)
            y_ref[...] = x_ref[...]

        hb = h_ref[...]
        gt = _dot(hb, wg_ref[...])
        up = _dot(hb, wu_ref[...])
        act = gt * _sigmoid(gt) * up
        y_ref[...] += _dot(act.astype(BF16), wd_ref[...])

    rowb = pl.BlockSpec((tm, D_MODEL), lambda i, j: (i, 0))
    return pl.pallas_call(
        body, name="ffn_fwd", grid=(s_len // tm, D_FF // tf),
        in_specs=[rowb, pl.BlockSpec((1, D_MODEL), lambda i, j: (0, 0)),
                  pl.BlockSpec((D_MODEL, tf), lambda i, j: (0, j)),
                  pl.BlockSpec((D_MODEL, tf), lambda i, j: (0, j)),
                  pl.BlockSpec((tf, D_MODEL), lambda i, j: (j, 0))],
        out_specs=[rowb, rowb],
        out_shape=[jax.ShapeDtypeStruct((s_len, D_MODEL), F32), jax.ShapeDtypeStruct((s_len, D_MODEL), BF16)],
        compiler_params=_cparams(48, dimension_semantics=("parallel", "arbitrary")),
    )(x2, g, wg, wu, wd)


def _ffn_bwd(x2, hf, dx3, dx3b, g, wg, wu, wd):
    s_len = x2.shape[0]
    tm, tf = _row_block(s_len), FF_TILE
    nf = D_FF // tf

    def act_body(h_ref, dy_ref, wg_ref, wu_ref, wd_ref, dgt_ref, dup_ref, act_ref):
        hb = h_ref[...]
        gt = _dot(hb, wg_ref[...])
        up = _dot(hb, wu_ref[...])
        sg = _sigmoid(gt)
        silu = gt * sg
        dact = _dot_nt(dy_ref[...], wd_ref[...])
        dgt_ref[...] = (dact * up * (sg * (1.0 + gt * (1.0 - sg)))).astype(BF16)
        dup_ref[...] = (dact * silu).astype(BF16)
        act_ref[...] = (silu * up).astype(BF16)

    rowb = pl.BlockSpec((tm, D_MODEL), lambda i, j: (i, 0))
    ffb = pl.BlockSpec((tm, tf), lambda i, j: (i, j))
    dgt, dup, act = pl.pallas_call(
        act_body, name="ffn_bwd_act", grid=(s_len // tm, nf),
        in_specs=[rowb, rowb,
                  pl.BlockSpec((D_MODEL, tf), lambda i, j: (0, j)),
                  pl.BlockSpec((D_MODEL, tf), lambda i, j: (0, j)),
                  pl.BlockSpec((tf, D_MODEL), lambda i, j: (j, 0))],
        out_specs=[ffb, ffb, ffb],
        out_shape=[jax.ShapeDtypeStruct((s_len, D_FF), BF16)] * 3,
        compiler_params=_cparams(56, dimension_semantics=("parallel", "arbitrary")),
    )(hf, dx3b, wg, wu, wd)
    tm = min(s_len, 256)

    def in_body(x_ref, dy_ref, g_ref, wg_ref, wu_ref, dgt_ref, dup_ref, dx_ref, dg_ref):
        dh = _dot_nt(dgt_ref[...], wg_ref[...]) + _dot_nt(dup_ref[...], wu_ref[...])
        gv = g_ref[...]
        _, xh, r = _rms(x_ref[...], gv)
        dx, dgr = _rms_bwd(dh, xh, r, gv)
        dx_ref[...] = dy_ref[...] + dx
        _acc_rows(dg_ref, dgr, pl.program_id(0) == 0)

    row1 = lambda n: pl.BlockSpec((tm, n), lambda i: (i, 0))
    full = lambda shape: pl.BlockSpec(shape, lambda i: (0, 0))
    dx2, dg = pl.pallas_call(
        in_body, name="ffn_bwd_in", grid=(s_len // tm,),
        in_specs=[row1(D_MODEL), row1(D_MODEL), full((1, D_MODEL)), full((D_MODEL, D_FF)), full((D_MODEL, D_FF)),
                  row1(D_FF), row1(D_FF)],
        out_specs=[row1(D_MODEL), full((1, D_MODEL))],
        out_shape=[jax.ShapeDtypeStruct((s_len, D_MODEL), F32), jax.ShapeDtypeStruct((1, D_MODEL), F32)],
        compiler_params=_cparams(48, dimension_semantics=("arbitrary",)),
    )(x2, dx3, g, wg, wu, dgt, dup)
    return dx2, dgt, dup, act, dg


def _loss_head(x3, g, target):
    s_len = x3.shape[0]
    tm = _row_block(s_len)

    def body(x_ref, g_ref, t_ref, sse_ref, dx_ref, dxb_ref, dg_ref):
        first = pl.program_id(0) == 0
        gv = g_ref[...]
        y, xh, r = _rms(x_ref[...], gv)
        err = y - t_ref[...]
        _acc(sse_ref, jnp.broadcast_to(jnp.sum(err * err), (8, LANES)), first)
        dx, dgr = _rms_bwd(err * (1.0 / D_MODEL), xh, r, gv)
        dx_ref[...] = dx
        dxb_ref[...] = dx.astype(BF16)
        _acc_rows(dg_ref, dgr, first)

    rowb = pl.BlockSpec((tm, D_MODEL), lambda i: (i, 0))
    return pl.pallas_call(
        body, name="loss_head", grid=(s_len // tm,),
        in_specs=[rowb, pl.BlockSpec((1, D_MODEL), lambda i: (0, 0)), rowb],
        out_specs=[pl.BlockSpec((8, LANES), lambda i: (0, 0)), rowb, rowb,
                   pl.BlockSpec((1, D_MODEL), lambda i: (0, 0))],
        out_shape=[jax.ShapeDtypeStruct((8, LANES), F32), jax.ShapeDtypeStruct((s_len, D_MODEL), F32),
                   jax.ShapeDtypeStruct((s_len, D_MODEL), BF16), jax.ShapeDtypeStruct((1, D_MODEL), F32)],
        compiler_params=_cparams(dimension_semantics=("arbitrary",)),
    )(x3, g, target)


def _mla_prep_bwd(lat, g_q, g_kv, w_uq, w_uk, w_uv, cosf, sinf, dq, dk, dv):
    s_len = lat.shape[0]
    tm = _row_block(s_len)

    def body(lat_ref, gq_ref, gkv_ref, wuq_ref, wuk_ref, wuv_ref, cos_ref, sin_ref, dq_ref, dk_ref, dv_ref,
             dlat_ref, dqb_ref, dkb_ref, dvb_ref, dgq_ref, dgkv_ref):
        first = pl.program_id(0) == 0
        lane = lax.broadcasted_iota(jnp.int32, (tm, LANES), 1)
        cosv, sinv = cos_ref[...], sin_ref[...]
        gq, gkv = gq_ref[...], gkv_ref[...]
        _, qxh, qr = _rms(lat_ref[:, 0:256], gq)
        _, kxh, kr_ = _rms(lat_ref[:, 256:384], gkv)
        dkr = jnp.zeros((tm, LANES), F32)
        for h in range(MLA_HEADS):
            sl = slice(h * HEAD_PAD, (h + 1) * HEAD_PAD)
            blk = dq_ref[:, sl]
            dqb_ref[:, sl] = (blk * cosv + _rope_rot_t(blk, lane) * sinv).astype(BF16)
            kblk = dk_ref[:, sl] * (1.0 / MLA_Q_FOLD)
            dkb_ref[:, sl] = kblk.astype(BF16)
            dkr = dkr + kblk
        dvb = dv_ref[...].astype(BF16)
        dvb_ref[...] = dvb
        dkr = jnp.where((lane >= 64) & (lane < 96), dkr, 0.0)
        dkr = dkr * cosv + _rope_rot_t(dkr, lane) * sinv
        dql = _dot_nt(dqb_ref[...], wuq_ref[...])
        dkvl = _dot_nt(dkb_ref[...], wuk_ref[...]) + _dot_nt(dvb, wuv_ref[...])
        dcq, dgqr = _rms_bwd(dql, qxh, qr, gq)
        dckv, dgkvr = _rms_bwd(dkvl, kxh, kr_, gkv)
        dlat_ref[:, 0:256] = dcq
        dlat_ref[:, 256:384] = dckv
        dlat_ref[:, K_R_OFF:K_R_OFF + LANES] = pltpu.roll(dkr, 64, 1)
        _acc_rows(dgq_ref, dgqr, first)
        _acc_rows(dgkv_ref, dgkvr, first)

    full = lambda shape: pl.BlockSpec(shape, lambda i: (0, 0))
    rowb = lambda n: pl.BlockSpec((tm, n), lambda i: (i, 0))
    sds = lambda n, dt: jax.ShapeDtypeStruct((s_len, n), dt)
    return pl.pallas_call(
        body, name="mla_prep_bwd", grid=(s_len // tm,),
        in_specs=[rowb(512), full((1, 256)), full((1, 128)), full((256, 1024)), full((128, 1024)),
                  full((128, 512)), rowb(128), rowb(128), rowb(1024), rowb(1024), rowb(512)],
        out_specs=[rowb(512), rowb(1024), rowb(1024), rowb(512), full((1, 256)), full((1, 128))],
        out_shape=[sds(512, F32), sds(1024, BF16), sds(1024, BF16), sds(512, BF16),
                   jax.ShapeDtypeStruct((1, 256), F32), jax.ShapeDtypeStruct((1, 128), F32)],
        compiler_params=_cparams(48, dimension_semantics=("arbitrary",)),
    )(lat, g_q, g_kv, w_uq, w_uk, w_uv, cosf, sinf, dq, dk, dv)


def _in_proj_bwd(x, g, w, dx1, dlat, dsbq, dsbk, dsbv, dgates, ride):
    s_len = x.shape[0]
    tm = min(s_len, 256)
    nb = s_len // tm

    def body(x_ref, g_ref, w_ref, dx1_ref, dlat_ref, dq_ref, dk_ref, dv_ref, dgate_ref, *rest):
        gx_ref, dg_ref = rest[ride.n:ride.n + 2]
        dproj = rest[-1]
        ride.bind(rest[:ride.n], rest[ride.n + 2:2 * ride.n + 2], rest[2 * ride.n + 2:-1])
        pl.when(pl.program_id(0) == 0)(ride.issue)
        dproj[:, 0:512] = dlat_ref[...].astype(BF16)
        dproj[:, 512:1024] = dq_ref[...].astype(BF16)
        dproj[:, 1024:1536] = (dk_ref[...] * LN2).astype(BF16)
        dproj[:, 1536:2048] = dv_ref[...].astype(BF16)
        dproj[:, 2048:4096] = dgate_ref[...]
        dh = _dot_nt(dproj[...], w_ref[...])
        gv = g_ref[...]
        _, xh, r = _rms(x_ref[...], gv)
        dx, dgr = _rms_bwd(dh, xh, r, gv)
        gx_ref[...] = dx1_ref[...] + dx
        _acc_rows(dg_ref, dgr, pl.program_id(0) == 0)
        pl.when(pl.program_id(0) == nb - 1)(ride.finish)

    rowb = lambda n: pl.BlockSpec((tm, n), lambda i: (i, 0))
    full = lambda shape: pl.BlockSpec(shape, lambda i: (0, 0))
    outs = pl.pallas_call(
        body, name="in_proj_bwd", grid=(nb,),
        in_specs=[rowb(D_MODEL), full((1, D_MODEL)), full((D_MODEL, D_IN_PAD)), rowb(D_MODEL),
                  rowb(512), rowb(512), rowb(512), rowb(512), rowb(2 * D_MODEL)] + ride.specs,
        out_specs=[rowb(D_MODEL), full((1, D_MODEL))] + ride.specs,
        out_shape=[jax.ShapeDtypeStruct((s_len, D_MODEL), F32), jax.ShapeDtypeStruct((1, D_MODEL), F32)]
        + ride.out_shape,
        scratch_shapes=ride.scratch + [pltpu.VMEM((tm, D_IN_PAD), BF16)],
        compiler_params=_cparams(48, dimension_semantics=("arbitrary",)),
    )(x, g, w, dx1, dlat, dsbq, dsbk, dsbv, dgates, *ride.srcs)
    return outs[0], outs[1], outs[2:]


def _adamw(landed, w, m, v, name):
    r, c = w.shape
    lanes = _round_up(c, LANES)
    tb = r
    for cand in range(r, 0, -1):
        if r % cand == 0 and (cand % 8 == 0 or cand == r) and N_DEV * cand * lanes * 4 <= ADAM_BLOCK_BYTES:
            tb = cand
            break
    c1 = 1.0 - ADAM_B1 ** ADAM_STEP
    c2 = 1.0 - ADAM_B2 ** ADAM_STEP

    def body(l_ref, w_ref, m_ref, v_ref, g_ref, d_ref, nm_ref, nv_ref):
        g = l_ref[0]
        for k in range(1, N_DEV):
            g = g + l_ref[k]
        nm = ADAM_B1 * m_ref[...] + (1.0 - ADAM_B1) * g
        nv = ADAM_B2 * v_ref[...] + (1.0 - ADAM_B2) * (g * g)
        g_ref[...] = g
        nm_ref[...] = nm
        nv_ref[...] = nv
        d_ref[...] = -ADAM_LR * ((nm / c1) / (jnp.sqrt(nv / c2) + ADAM_EPS) + ADAM_WD * w_ref[...])

    blk = pl.BlockSpec((tb, c), lambda i: (i, 0))
    return pl.pallas_call(
        body, name=name, grid=(r // tb,),
        in_specs=[pl.BlockSpec((N_DEV, tb, c), lambda i: (0, i, 0)), blk, blk, blk],
        out_specs=[blk, blk, blk, blk],
        out_shape=[jax.ShapeDtypeStruct((r, c), F32)] * 4,
        compiler_params=_cparams(dimension_semantics=("parallel",)),
    )(landed, w, m, v)


def _shard_shape(shape, axis):
    return tuple(d // N_DEV if a == axis else d for a, d in enumerate(shape))


def _split_pieces(full, axis):
    r, c = full.shape
    if axis == 0:
        return full.reshape(N_DEV, r // N_DEV, c)
    return full.reshape(r, N_DEV, c // N_DEV).transpose(1, 0, 2)


def _join_shards(gathered, axis):
    _, r, c = gathered.shape
    if axis == 0:
        return gathered.reshape(N_DEV * r, c)
    return gathered.transpose(1, 0, 2).reshape(r, N_DEV * c)


def kernel(x, mem, positions, g_mix, w_in, b_gate, g_q_lat, w_uq, g_kv_lat, w_ukv, w_a_proj, w_b_proj, w_o, g_x, g_mem, w_xq, w_xkv, w_xo, g_ffn, w_gate, w_up, w_down, g_final, loss_target, m_g_mix, m_w_in, m_b_gate, m_g_q_lat, m_w_uq, m_g_kv_lat, m_w_ukv, m_w_a_proj, m_w_b_proj, m_w_o, m_g_x, m_g_mem, m_w_xq, m_w_xkv, m_w_xo, m_g_ffn, m_w_gate, m_w_up, m_w_down, m_g_final, v_g_mix, v_w_in, v_b_gate, v_g_q_lat, v_w_uq, v_g_kv_lat, v_w_ukv, v_w_a_proj, v_w_b_proj, v_w_o, v_g_x, v_g_mem, v_w_xq, v_w_xkv, v_w_xo, v_g_ffn, v_w_gate, v_w_up, v_w_down, v_g_final):
    given = dict(locals())
    s_len = x.shape[1]
    x2d = x.reshape(s_len, D_MODEL)
    mem2d = mem.reshape(-1, D_MODEL)
    target = loss_target.reshape(s_len, D_MODEL)

    names = [name for name, _, _ in SHARDED]
    axis_of = {name: axis for name, _, axis in SHARDED}
    shard2d = lambda name, prefix="": given[prefix + name].reshape(
        _shard_shape(dict((n, s) for n, s, _ in SHARDED)[name], axis_of[name]))

    wire = lambda name: shard2d(name) if name == "b_gate" else shard2d(name).astype(BF16)
    early = [n for n in names if n in NEEDED_FIRST]
    late = [n for n in names if n not in NEEDED_FIRST]
    gathered = _exchange(True, [wire(n) for n in early], "weights_gather_first")
    wts = {n: _join_shards(g, axis_of[n]) for n, g in zip(early, gathered)}

    w_in_p = jnp.concatenate([wts["w_in"][:, :LAT_COLS], jnp.zeros((D_MODEL, D_IN_PAD - D_IN), BF16),
                              wts["w_in"][:, LAT_COLS:]], axis=1)
    w_uq_p = jnp.pad(wts["w_uq"].reshape(256, MLA_HEADS, 96), ((0, 0), (0, 0), (0, 32))).reshape(256, 1024)
    ukv = wts["w_ukv"].reshape(128, MLA_HEADS, 128)
    w_uk_p = jnp.pad(ukv[:, :, :64], ((0, 0), (0, 0), (0, 64))).reshape(128, 1024)
    w_uv = ukv[:, :, 64:].reshape(128, 512)
    w_uv1 = jnp.pad(ukv[:, :, 64:], ((0, 0), (0, 0), (0, 64))).reshape(128, 1024)
    bg = wts["b_gate"]

    inv_freq = ROPE_THETA ** (-jnp.arange(0, MLA_ROPE, 2, dtype=F32) / MLA_ROPE)
    ang = positions.reshape(s_len).astype(F32)[:, None] * inv_freq
    cos16, sin16 = jnp.cos(ang), jnp.sin(ang)
    cosf = jnp.concatenate([jnp.ones((s_len, 64), F32), cos16, cos16, jnp.ones((s_len, 32), F32)], axis=1)
    sinf = jnp.concatenate([jnp.zeros((s_len, 64), F32), sin16, sin16, jnp.zeros((s_len, 32), F32)], axis=1)

    h1, lat, sb, gates = _in_proj(x2d, g_mix, w_in_p)
    qa, ka, va, va1, q_lat, kv_lat = _mla_prep(lat, g_q_lat, g_kv_lat, w_uq_p, w_uk_p, w_uv, w_uv1, cosf, sinf)
    oa, lse, gathered = _mla_fwd(qa, ka, va1, _Exchange(True, [wire(n) for n in late]))
    wts.update({n: _join_shards(g, axis_of[n]) for n, g in zip(late, gathered)})
    ob, sb_r = _sb_fwd(sb)
    x1 = _merge_fwd(x2d, oa, ob, gates, bg, wts["w_a_proj"], wts["w_b_proj"], wts["w_o"])
    mn, xkv = _mem_kv(mem2d, g_mem, wts["w_xkv"])
    x2 = _xattn_fwd(x1, g_x, wts["w_xq"], xkv, wts["w_xo"])
    x3, hf = _ffn_fwd(x2, g_ffn, wts["w_gate"], wts["w_up"], wts["w_down"])
    g_final2d = g_final.reshape(1, D_MODEL)
    sse, dx3, dx3b, dg_final = _loss_head(x3, g_final2d, target)

    dx2, dgt, dup, act, dg_ffn = _ffn_bwd(x2, hf, dx3, dx3b, g_ffn, wts["w_gate"], wts["w_up"], wts["w_down"])
    dx1, dw_xq, dw_xo, dxkv, dg_x = _xattn_bwd(x1, dx2, g_x, wts["w_xq"], xkv, wts["w_xo"])
    dw_xkv, dg_mem = _mem_bwd(mem2d, g_mem, wts["w_xkv"], mn, dxkv)
    doa, dob, dgates, dpa, dpb, merged, dx1b, dbg = _merge_bwd(
        dx1, oa, ob, gates, bg, wts["w_a_proj"], wts["w_b_proj"], wts["w_o"])
    dsbq, dsbk, dsbv = _sb_bwd(sb, dob, sb_r)
    full_grads = {
        "w_a_proj": _tn_matmul(oa, dpa, "dw_a"),
        "w_b_proj": _tn_matmul(ob, dpb, "dw_b"),
        "w_o": _tn_matmul(merged, dx1b, "dw_o"),
        "w_xq": dw_xq,
        "w_xkv": dw_xkv,
        "w_xo": dw_xo,
        "w_gate": _tn_matmul(hf, dgt, "dw_gate", tn=FF_TILE),
        "w_up": _tn_matmul(hf, dup, "dw_up", tn=FF_TILE),
        "w_down": _tn_matmul(act, dx3b, "dw_down", tka=FF_TILE),
    }
    dw_sb = _tn_matmul_sb(h1, dsbq, dsbk, dsbv, "dw_in_sb")
    own0 = D_IN // N_DEV - LAT_COLS
    w_in_rest = _split_pieces(
        jnp.concatenate([dw_sb[:, :LAT_COLS], dw_sb, _tn_matmul(h1, dgates, "dw_in_gates")], axis=1), 1)
    dqa, dka, dva, got = _mla_bwd(
        qa, ka, va, oa, doa, lse,
        _Exchange(False, [_split_pieces(full_grads[n], axis_of[n]) for n in late] + [w_in_rest]))
    landed = dict(zip(late, got[:-1]))
    dlat, dqb, dkb, dvb, dg_q, dg_kv = _mla_prep_bwd(
        lat, g_q_lat, g_kv_lat, w_uq_p, w_uk_p, w_uv, cosf, sinf, dqa, dka, dva)
    w_in_first = jnp.concatenate([_tn_matmul(h1, dlat, "dw_in_lat")[:, :LAT_COLS], dw_sb[:, :own0]], axis=1)
    dw_uq_p = _tn_matmul(q_lat, dqb, "dw_uq")
    dw_uk_p = _tn_matmul(kv_lat, dkb, "dw_uk")
    dw_uv = _tn_matmul(kv_lat, dvb, "dw_uv")
    full_grads.update({
        "b_gate": dbg,
        "w_uq": dw_uq_p.reshape(256, MLA_HEADS, 128)[:, :, :96].reshape(256, 768),
        "w_ukv": jnp.concatenate([dw_uk_p.reshape(128, MLA_HEADS, 128)[:, :, :64],
                                  dw_uv.reshape(128, MLA_HEADS, 64)], axis=2).reshape(128, 1024),
    })
    small = [n for n in early if n != "w_in"]
    grad_x, dg_mix, got2 = _in_proj_bwd(
        x2d, g_mix, w_in_p, dx1, dlat, dsbq, dsbk, dsbv, dgates,
        _Exchange(False, [_split_pieces(full_grads[n], axis_of[n]) for n in small], to_first=[w_in_first]))
    landed.update(zip(small, got2[:-1]))
    me = 4 * lax.axis_index("x") + 2 * lax.axis_index("y") + lax.axis_index("c")
    landed["w_in"] = jnp.where(me == 0, got2[-1], got[-1])
    rep_grads = {"g_mix": dg_mix, "g_q_lat": dg_q, "g_kv_lat": dg_kv, "g_x": dg_x, "g_mem": dg_mem,
                 "g_ffn": dg_ffn, "g_final": dg_final}
    tail = jnp.zeros((8, LANES), F32)
    rep_cat = lambda prefix, src, last=tail: jnp.concatenate(
        [jnp.concatenate([src[prefix + n].reshape(-1) for n, _ in REPLICATED]).reshape(-1, LANES), last], axis=0)
    rep_own = rep_cat("", rep_grads, jnp.concatenate([sse[0:1], tail[1:]], axis=0))
    rep_src = jnp.broadcast_to(rep_own, (N_DEV,) + rep_own.shape)
    rep_landed = _exchange(False, [rep_src], "grads_gains")[0]

    res = {}
    for name, _, _ in SHARDED:
        outs = _adamw(landed[name], shard2d(name), shard2d(name, "m_"), shard2d(name, "v_"), "adamw_" + name)
        res[name] = [o.reshape(given[name].shape) for o in outs]
    rep_outs = _adamw(rep_landed, rep_cat("", given), rep_cat("m_", given), rep_cat("v_", given), "adamw_gains")
    off = 0
    for name, n in REPLICATED:
        res[name] = [o.reshape(-1)[off:off + n].reshape(given[name].shape) for o in rep_outs]
        off += n
    loss = rep_outs[0][off // LANES, 0] * (0.5 / D_MODEL)
    result = [loss, grad_x.reshape(x.shape)]
    for k in range(4):
        result.extend(res[name][k] for name in WEIGHT_ORDER)
    return tuple(result)
```

```python
import functools
import math

import jax
import jax.numpy as jnp
from jax import lax
from jax.experimental import pallas as pl
from jax.experimental.pallas import tpu as pltpu

F32 = jnp.float32
BF16 = jnp.bfloat16

D_MODEL = 1024
MLA_HEADS = 8
MLA_Q_RANK = 256
MLA_KV_RANK = 128
MLA_NOPE = 64
MLA_ROPE = 32
ROPE_THETA = 10000.0
SB_WIDTH = 512
X_HEADS = 4
X_HEAD_DIM = 128
D_FF = 2816
EPS = 1e-6
D_IN = 4000
D_IN_PAD = 4096
K_R_OFF = 384
LAT_COLS = 416
LANES = 128
HEAD_PAD = 128
MLA_SCALE = 1.0 / math.sqrt(MLA_NOPE + MLA_ROPE)
SB_SCALE = 0.125
LOG2E = math.log2(math.e)
LN2 = math.log(2.0)
MLA_Q_FOLD = MLA_SCALE * LOG2E
SB_Q_FOLD = SB_SCALE * LOG2E
SB_CUT = -160.0
X_SCALE = 1.0 / math.sqrt(X_HEAD_DIM)
NEG_BIG = -1e30

ADAM_LR = 0.001
ADAM_B1 = 0.9
ADAM_B2 = 0.999
ADAM_EPS = 1e-08
ADAM_WD = 0.01
ADAM_STEP = 10

N_DEV = 8
MIB = 1024 * 1024
ADAM_BLOCK_BYTES = 4 * MIB

SHARDED = (
    ("w_in", (D_MODEL, D_IN), 1),
    ("b_gate", (2, D_MODEL), 1),
    ("w_uq", (MLA_Q_RANK, 768), 1),
    ("w_ukv", (MLA_KV_RANK, 1024), 1),
    ("w_a_proj", (512, D_MODEL), 1),
    ("w_b_proj", (512, D_MODEL), 1),
    ("w_o", (D_MODEL, D_MODEL), 0),
    ("w_xq", (D_MODEL, 512), 0),
    ("w_xkv", (D_MODEL, 1024), 0),
    ("w_xo", (512, D_MODEL), 1),
    ("w_gate", (D_MODEL, D_FF), 1),
    ("w_up", (D_MODEL, D_FF), 1),
    ("w_down", (D_FF, D_MODEL), 0),
)
NEEDED_FIRST = ("w_in", "b_gate", "w_uq", "w_ukv")
REPLICATED = (
    ("g_mix", 1024), ("g_q_lat", 256), ("g_kv_lat", 128), ("g_x", 1024),
    ("g_mem", 1024), ("g_ffn", 1024), ("g_final", 1024),
)
WEIGHT_ORDER = ("g_mix", "w_in", "b_gate", "g_q_lat", "w_uq", "g_kv_lat", "w_ukv", "w_a_proj",
                "w_b_proj", "w_o", "g_x", "g_mem", "w_xq", "w_xkv", "w_xo", "g_ffn", "w_gate",
                "w_up", "w_down", "g_final")


def _round_up(n, m):
    return -(-n // m) * m


def _cparams(vmem_mib=None, **kw):
    if vmem_mib is not None:
        kw["vmem_limit_bytes"] = vmem_mib * MIB
    return pltpu.CompilerParams(**kw)


def _dot(a, b):
    return jnp.dot(a, b, preferred_element_type=F32)


def _dot_nt(a, b):
    return lax.dot_general(a, b, (((1,), (1,)), ((), ())), preferred_element_type=F32)


def _dot_tn(a, b):
    return lax.dot_general(a, b, (((0,), (0,)), ((), ())), preferred_element_type=F32)


def _rms(x, g):
    r = lax.rsqrt(jnp.mean(x * x, axis=-1, keepdims=True) + EPS)
    xh = x * r
    return xh * g, xh, r


def _rms_bwd(dy, xh, r, g):
    u = dy * g
    dx = r * (u - xh * jnp.mean(u * xh, axis=-1, keepdims=True))
    return dx, dy * xh


def _sigmoid(z):
    return 1.0 / (1.0 + jnp.exp(-z))


def _acc_rows(ref, val, first):
    s = jnp.sum(val, axis=0, keepdims=True)

    @pl.when(first)
    def _():
        ref[...] = s

    @pl.when(jnp.logical_not(first))
    def _():
        ref[...] += s


def _acc(ref, val, first):
    @pl.when(first)
    def _():
        ref[...] = val

    @pl.when(jnp.logical_not(first))
    def _():
        ref[...] += val


def _peer(k):
    x, y, c = lax.axis_index("x"), lax.axis_index("y"), lax.axis_index("c")
    px = 1 - x if (k >> 2) & 1 else x
    py = 1 - y if (k >> 1) & 1 else y
    pc = 1 - c if k & 1 else c
    return (px, py, pc), 4 * px + 2 * py + pc


N_PEERS = N_DEV - 1
OTHER_CHIPS = (2, 4, 6)


def _land_shape(gather, src):
    return (N_DEV,) + src.shape if gather else src.shape


class _Exchange:
    def __init__(self, gather, srcs, to_first=()):
        self.gather, self.m, self.srcs = gather, len(srcs), list(srcs) + list(to_first)
        self.n = len(self.srcs)
        self.out_shape = ([jax.ShapeDtypeStruct(_land_shape(gather, s), s.dtype) for s in srcs]
                          + [jax.ShapeDtypeStruct(_land_shape(True, s), s.dtype) for s in to_first])
        self.specs = [pl.BlockSpec(memory_space=pl.ANY)] * self.n
        self.scratch = [pltpu.SemaphoreType.DMA((self.n * N_PEERS,)), pltpu.SemaphoreType.DMA((self.n * N_PEERS,)),
                        pltpu.SemaphoreType.DMA((self.n,))]

    def bind(self, src, land, sems):
        self.src, self.land = src, land
        self.send_sems, self.recv_sems, self.local_sems = sems

    def _copy(self, a, k, source, to, target=1):
        return pltpu.make_async_remote_copy(
            src_ref=source, dst_ref=to,
            send_sem=self.send_sems.at[a * N_PEERS + k - 1], recv_sem=self.recv_sems.at[a * N_PEERS + k - 1],
            device_id=_peer(target)[0], device_id_type=pl.DeviceIdType.MESH)

    def _row(self, a, k):
        return self.land[a].at[_peer(k)[1]]

    def _mine(self, a):
        me = _peer(0)[1]
        whole = self.gather or a >= self.m
        return pltpu.make_async_copy(self.src[a] if whole else self.src[a].at[me], self.land[a].at[me],
                                     self.local_sems.at[a])

    def issue(self):
        me = _peer(0)[1]
        for a in range(self.m):
            self._mine(a).start()
            for k in ((1,) + OTHER_CHIPS if self.gather else range(1, N_DEV)):
                source = self.src[a] if self.gather else self.src[a].at[_peer(k)[1]]
                self._copy(a, k, source, self.land[a].at[me], target=k).start()
        for a in range(self.m, self.n):
            pl.when(me == 0)(self._mine(a).start)
            for k in range(1, N_DEV):
                pl.when(me == k)(self._copy(a, k, self.src[a], self.land[a].at[me], target=k).start)

    def finish(self):
        me = _peer(0)[1]
        part = lambda a: self.src[a] if self.gather or a >= self.m else self.src[a].at[me]
        if self.gather:
            for a in range(self.m):
                for k in OTHER_CHIPS:
                    self._copy(a, k, part(a), self._row(a, k)).wait_recv()
                    self._copy(a, k + 1, self._row(a, k), self._row(a, k), target=1).start()
        for a in range(self.m):
            for k in ((1, 3, 5, 7) if self.gather else range(1, N_DEV)):
                self._copy(a, k, part(a), self._row(a, k)).wait_recv()
        for a in range(self.m):
            for k in range(1, N_DEV):
                self._copy(a, k, part(a), self.land[a].at[me]).wait_send()
            self._mine(a).wait()
        for a in range(self.m, self.n):
            for k in range(1, N_DEV):
                pl.when(me == 0)(self._copy(a, k, part(a), self._row(a, k)).wait_recv)
                pl.when(me == k)(self._copy(a, k, part(a), self.land[a].at[me]).wait_send)
            pl.when(me == 0)(self._mine(a).wait)


def _exchange(gather, srcs, name):
    ex = _Exchange(gather, srcs)

    def body(*refs):
        ex.bind(refs[:ex.n], refs[ex.n:2 * ex.n], refs[2 * ex.n:])
        ex.issue()
        ex.finish()

    return pl.pallas_call(body, name=name, out_shape=ex.out_shape, in_specs=ex.specs, out_specs=ex.specs,
                          scratch_shapes=ex.scratch)(*ex.srcs)


def _tn_matmul(a, b, name, tka=512, tn=1024, ts=2048):
    s_len, ka = a.shape
    n = b.shape[1]
    tka, tn, ts = min(tka, ka), min(tn, n), min(ts, s_len)
    assert ka % tka == 0 and n % tn == 0 and s_len % ts == 0

    def body(a_ref, b_ref, o_ref):
        _acc(o_ref, _dot_tn(a_ref[...], b_ref[...].astype(BF16)), pl.program_id(2) == 0)

    return pl.pallas_call(
        body, name=name, grid=(ka // tka, n // tn, s_len // ts),
        in_specs=[pl.BlockSpec((ts, tka), lambda i, j, s: (s, i)),
                  pl.BlockSpec((ts, tn), lambda i, j, s: (s, j))],
        out_specs=pl.BlockSpec((tka, tn), lambda i, j, s: (i, j)),
        out_shape=jax.ShapeDtypeStruct((ka, n), F32),
        compiler_params=_cparams(dimension_semantics=("parallel", "parallel", "arbitrary")),
    )(a, b)


def _tn_matmul_sb(a, dq, dk, dv, name, tka=512, ts=2048):
    s_len, ka = a.shape
    tka, ts = min(tka, ka), min(ts, s_len)
    assert ka % tka == 0 and s_len % ts == 0

    def body(a_ref, q_ref, k_ref, v_ref, o_ref):
        first = pl.program_id(1) == 0
        av = a_ref[...]
        for c, val in enumerate((q_ref[...], k_ref[...] * LN2, v_ref[...])):
            _acc(o_ref.at[:, c * SB_WIDTH:(c + 1) * SB_WIDTH], _dot_tn(av, val.astype(BF16)), first)

    colb = pl.BlockSpec((ts, SB_WIDTH), lambda i, s: (s, 0))
    return pl.pallas_call(
        body, name=name, grid=(ka // tka, s_len // ts),
        in_specs=[pl.BlockSpec((ts, tka), lambda i, s: (s, i)), colb, colb, colb],
        out_specs=pl.BlockSpec((tka, 3 * SB_WIDTH), lambda i, s: (i, 0)),
        out_shape=jax.ShapeDtypeStruct((ka, 3 * SB_WIDTH), F32),
        compiler_params=_cparams(48, dimension_semantics=("parallel", "arbitrary")),
    )(a, dq, dk, dv)


def _row_block(s_len):
    return min(s_len, 512)


def _rms_cast(x, g, ride):
    s_len = x.shape[0]
    tm = _row_block(s_len)
    nb = s_len // tm

    def body(x_ref, g_ref, *rest):
        h_ref = rest[ride.n]
        ride.bind(rest[:ride.n], rest[ride.n + 1:2 * ride.n + 1], rest[2 * ride.n + 1:])
        pl.when(pl.program_id(0) == 0)(ride.issue)
        h, _, _ = _rms(x_ref[...], g_ref[...])
        h_ref[...] = h.astype(BF16)
        pl.when(pl.program_id(0) == nb - 1)(ride.finish)

    rowb = pl.BlockSpec((tm, D_MODEL), lambda i: (i, 0))
    outs = pl.pallas_call(
        body, name="rms_cast", grid=(nb,),
        in_specs=[rowb, pl.BlockSpec((1, D_MODEL), lambda i: (0, 0))] + ride.specs,
        out_specs=[rowb] + ride.specs,
        out_shape=[jax.ShapeDtypeStruct((s_len, D_MODEL), BF16)] + ride.out_shape,
        scratch_shapes=ride.scratch,
        compiler_params=_cparams(dimension_semantics=("arbitrary",)),
    )(x, g, *ride.srcs)
    return outs[0], outs[1:]


def _in_proj_mm(h, w):
    s_len = h.shape[0]
    tm = _row_block(s_len)

    def body(h_ref, w_ref, lat_ref, sb_ref, gate_ref):
        hb = h_ref[...]
        p = _dot(hb, w_ref[:, 0:1024])
        lat_ref[...] = p[:, 0:512]
        sb_ref[:, 0:512] = (p[:, 512:1024] * SB_Q_FOLD).astype(BF16)
        sb_ref[:, 512:1536] = _dot(hb, w_ref[:, 1024:2048]).astype(BF16)
        gate_ref[:, 0:1024] = _dot(hb, w_ref[:, 2048:3072])
        gate_ref[:, 1024:2048] = _dot(hb, w_ref[:, 3072:4096])

    rowb = lambda n: pl.BlockSpec((tm, n), lambda i: (i, 0))
    return pl.pallas_call(
        body, name="in_proj", grid=(s_len // tm,),
        in_specs=[rowb(D_MODEL), pl.BlockSpec((D_MODEL, D_IN_PAD), lambda i: (0, 0))],
        out_specs=[rowb(512), rowb(3 * SB_WIDTH), rowb(2 * D_MODEL)],
        out_shape=[jax.ShapeDtypeStruct((s_len, 512), F32),
                   jax.ShapeDtypeStruct((s_len, 3 * SB_WIDTH), BF16),
                   jax.ShapeDtypeStruct((s_len, 2 * D_MODEL), F32)],
        compiler_params=_cparams(48, dimension_semantics=("parallel",)),
    )(h, w)


def _rope_rot(blk, lane):
    return jnp.where(lane < 80, -pltpu.roll(blk, 112, 1), pltpu.roll(blk, 16, 1))


def _rope_rot_t(blk, lane):
    return jnp.where(lane < 80, pltpu.roll(blk, 112, 1), -pltpu.roll(blk, 16, 1))


def _mla_prep(lat, g_q, g_kv, w_uq, w_uk, w_uv, w_uv1, cosf, sinf):
    s_len = lat.shape[0]
    tm = _row_block(s_len)

    def body(lat_ref, gq_ref, gkv_ref, wuq_ref, wuk_ref, wuv_ref, wuv1_ref, cos_ref, sin_ref,
             q_ref, k_ref, v_ref, v1_ref, ql_ref, kvl_ref):
        lane = lax.broadcasted_iota(jnp.int32, (tm, LANES), 1)
        cosv, sinv = cos_ref[...], sin_ref[...]
        ql, _, _ = _rms(lat_ref[:, 0:256], gq_ref[...])
        kvl, _, _ = _rms(lat_ref[:, 256:384], gkv_ref[...])
        qlb, kvlb = ql.astype(BF16), kvl.astype(BF16)
        ql_ref[...] = qlb
        kvl_ref[...] = kvlb
        q = _dot(qlb, wuq_ref[...])
        kn = _dot(kvlb, wuk_ref[...])
        v_ref[...] = _dot(kvlb, wuv_ref[...]).astype(BF16)
        v1 = _dot(kvlb, wuv1_ref[...])
        wide = lax.broadcasted_iota(jnp.int32, (tm, MLA_HEADS * HEAD_PAD), 1)
        v1_ref[...] = jnp.where(wide % HEAD_PAD == MLA_NOPE, 1.0, v1).astype(BF16)
        kr = pltpu.roll(lat_ref[:, K_R_OFF:K_R_OFF + LANES], 64, 1)
        kr = kr * cosv + _rope_rot(kr, lane) * sinv
        for h in range(MLA_HEADS):
            sl = slice(h * HEAD_PAD, (h + 1) * HEAD_PAD)
            blk = q[:, sl]
            q_ref[:, sl] = ((blk * cosv + _rope_rot(blk, lane) * sinv) * MLA_Q_FOLD).astype(BF16)
            k_ref[:, sl] = (kn[:, sl] + kr).astype(BF16)

    full = lambda shape: pl.BlockSpec(shape, lambda i: (0, 0))
    rowb = lambda n: pl.BlockSpec((tm, n), lambda i: (i, 0))
    return pl.pallas_call(
        body, name="mla_prep", grid=(s_len // tm,),
        in_specs=[rowb(512), full((1, 256)), full((1, 128)), full((256, 1024)), full((128, 1024)),
                  full((128, 512)), full((128, 1024)), rowb(128), rowb(128)],
        out_specs=[rowb(1024), rowb(1024), rowb(512), rowb(1024), rowb(256), rowb(128)],
        out_shape=[jax.ShapeDtypeStruct((s_len, 1024), BF16), jax.ShapeDtypeStruct((s_len, 1024), BF16),
                   jax.ShapeDtypeStruct((s_len, 512), BF16), jax.ShapeDtypeStruct((s_len, 1024), BF16),
                   jax.ShapeDtypeStruct((s_len, 256), BF16), jax.ShapeDtypeStruct((s_len, 128), BF16)],
        compiler_params=_cparams(dimension_semantics=("parallel",)),
    )(lat, g_q, g_kv, w_uq, w_uk, w_uv, w_uv1, cosf, sinf)


ATTN_TQ = 1024
ATTN_TH = 512
ATTN_TK = 256
SB_TQ = 512
SB_TH = 256
MLA_TK = 512
MLA_FWD_TK = 1024


def _attn_blocks(s_len, tk=ATTN_TK, tq=ATTN_TQ, th=ATTN_TH):
    tq, th, tk = min(s_len, tq), min(s_len, th), min(s_len, tk)
    return tq, th, tk, tq // tk


def _chains(tq, th):
    return [(hh, r0) for hh in range(2) for r0 in range(0, tq, th)]


def _diag_mask(th, tk, r0, sub, strict):
    lo, hi = sub * tk, (sub + 1) * tk - 1
    last, first = r0 + th - 1, r0
    if (lo >= last) if strict else (lo > last):
        return "none"
    if (hi < first) if strict else (hi <= first):
        return "all"
    row = lax.broadcasted_iota(jnp.int32, (th, tk), 0) + r0
    col = lax.broadcasted_iota(jnp.int32, (th, tk), 1) + lo
    return col < row if strict else col <= row


def _mla_fwd(q, k, v, ride):
    s_len = q.shape[0]
    tq, th, tk, nsub = _attn_blocks(s_len, MLA_FWD_TK)
    chains = _chains(tq, th)
    nh = tq // th
    nq = s_len // tq

    def body(q_ref, k_ref, v_ref, *rest):
        o_ref, lse_ref = rest[ride.n:ride.n + 2]
        ride.bind(rest[:ride.n], rest[ride.n + 2:2 * ride.n + 2], rest[2 * ride.n + 2:])
        pl.when((pl.program_id(0) == 0) & (pl.program_id(1) == 0))(ride.issue)
        i = pl.program_id(1)
        lane = lax.broadcasted_iota(jnp.int32, (th, LANES), 1)
        hsl = [slice(hh * HEAD_PAD, (hh + 1) * HEAD_PAD) for hh in range(2)]

        def step(kb, carry, sub):
            rows = pl.ds(pl.multiple_of(kb * tk, tk), tk)
            masks = ["all" if sub is None else _diag_mask(th, tk, r0, sub, strict=False) for _, r0 in chains]
            live = [n for n, m in enumerate(masks) if not (isinstance(m, str) and m == "none")]
            s = {n: _dot_nt(q_ref[chains[n][1]:chains[n][1] + th, hsl[chains[n][0]]], k_ref[rows, hsl[chains[n][0]]])
                 for n in live}
            new = list(carry)
            pb, alpha = {}, {}
            for n in live:
                m = carry[n][0]
                sn = s[n]
                if not isinstance(masks[n], str):
                    sn = jnp.where(masks[n], sn, NEG_BIG)
                m_new = jnp.maximum(m, jnp.max(sn, axis=-1, keepdims=True))
                alpha[n] = jnp.exp2(m - m_new)
                pb[n] = jnp.exp2(sn - m_new).astype(BF16)
                new[n] = (m_new, None)
            pv = {n: _dot(pb[n], v_ref[rows, hsl[chains[n][0]]]) for n in live}
            for n in live:
                new[n] = (new[n][0], alpha[n] * carry[n][1] + pv[n])
            return tuple(new)

        init = (jnp.full((th, 1), NEG_BIG, F32), jnp.zeros((th, LANES), F32))
        carry = lax.fori_loop(0, i * nsub, lambda kb, cy: step(kb, cy, None), (init,) * len(chains))
        for sub in range(nsub):
            carry = step(i * nsub + sub, carry, sub)
        for c in range(nh):
            (m0, a0), (m1, a1) = carry[c], carry[nh + c]
            l0, l1 = a0[:, MLA_NOPE:MLA_NOPE + 1], a1[:, MLA_NOPE:MLA_NOPE + 1]
            rs = slice(c * th, (c + 1) * th)
            o_ref[rs, :] = jnp.where(lane < 64, a0 / l0, pltpu.roll(a1 / l1, 64, 1)).astype(BF16)
            lse_ref[rs, :] = jnp.where(lane < 64, m0 + jnp.log2(l0), m1 + jnp.log2(l1))
        pl.when((pl.program_id(0) == 3) & (i == nq - 1))(ride.finish)

    outs = pl.pallas_call(
        body, name="mla_fwd", grid=(4, nq),
        in_specs=[pl.BlockSpec((tq, 2 * HEAD_PAD), lambda p, i: (i, p)),
                  pl.BlockSpec((s_len, 2 * HEAD_PAD), lambda p, i: (0, p)),
                  pl.BlockSpec((s_len, 2 * HEAD_PAD), lambda p, i: (0, p))] + ride.specs,
        out_specs=[pl.BlockSpec((tq, LANES), lambda p, i: (i, p)),
                   pl.BlockSpec((None, tq, LANES), lambda p, i: (p, i, 0))] + ride.specs,
        out_shape=[jax.ShapeDtypeStruct((s_len, 512), BF16),
                   jax.ShapeDtypeStruct((4, s_len, LANES), F32)] + ride.out_shape,
        scratch_shapes=ride.scratch,
        compiler_params=_cparams(40, dimension_semantics=("arbitrary", "arbitrary")),
    )(q, k, v, *ride.srcs)
    return outs[0], outs[1], outs[2:]


def _mla_bwd(q, k, v, o, do, lse, ride):
    s_len = q.shape[0]
    tq, th, tk, nsub = _attn_blocks(s_len, MLA_TK)
    chains = _chains(tq, th)
    nq = s_len // tq

    def body(q_ref, k_ref, v_ref, o_ref, do_ref, lse_ref, *rest):
        dq_ref, dk_ref, dv_ref = rest[ride.n:ride.n + 3]
        ride.bind(rest[:ride.n], rest[ride.n + 3:2 * ride.n + 3], rest[2 * ride.n + 3:])
        pl.when((pl.program_id(0) == 0) & (pl.program_id(1) == 0))(ride.issue)
        i = pl.program_id(1)
        lane = lax.broadcasted_iota(jnp.int32, (th, LANES), 1)

        @pl.when(i == 0)
        def _():
            dk_ref[...] = jnp.zeros_like(dk_ref)
            dv_ref[...] = jnp.zeros_like(dv_ref)

        hsl = [slice(hh * HEAD_PAD, (hh + 1) * HEAD_PAD) for hh in range(2)]
        qs, dos, deltas, lses = [], [], [], []
        for hh, r0 in chains:
            rs = slice(r0, r0 + th)
            qs.append(q_ref[rs, hsl[hh]])
            doh = jnp.where((lane // 64) == hh, do_ref[rs, :], jnp.zeros((), BF16))
            dos.append(doh)
            deltas.append(jnp.sum(doh.astype(F32) * o_ref[rs, :].astype(F32), axis=-1, keepdims=True))
            lses.append(lse_ref[rs, 64 * hh:64 * hh + 1])

        def step(kb, dqs, sub):
            rows = pl.ds(pl.multiple_of(kb * tk, tk), tk)
            vblk = v_ref[rows, :]
            new, p_all, do_all = [], [], []
            ds_h, q_h = [[], []], [[], []]
            for c, (hh, r0) in enumerate(chains):
                mask = "all" if sub is None else _diag_mask(th, tk, r0, sub, strict=False)
                if isinstance(mask, str) and mask == "none":
                    new.append(dqs[c])
                    continue
                kblk = k_ref[rows, hsl[hh]]
                s = _dot_nt(qs[c], kblk)
                if not isinstance(mask, str):
                    s = jnp.where(mask, s, NEG_BIG)
                p = jnp.exp2(s - lses[c])
                dp = _dot_nt(dos[c], vblk)
                ds = (p * (dp - deltas[c]) * MLA_SCALE).astype(BF16)
                p_all.append(p.astype(BF16))
                do_all.append(dos[c])
                ds_h[hh].append(ds)
                q_h[hh].append(qs[c])
                new.append(dqs[c] + _dot(ds, kblk))
            dv_ref[rows, :] += _dot_tn(jnp.concatenate(p_all, axis=0), jnp.concatenate(do_all, axis=0))
            for hh in range(2):
                dk_ref[rows, hsl[hh]] += _dot_tn(jnp.concatenate(ds_h[hh], axis=0),
                                                 jnp.concatenate(q_h[hh], axis=0))
            return tuple(new)

        zero = jnp.zeros((th, LANES), F32)
        dqs = lax.fori_loop(0, i * nsub, lambda kb, cy: step(kb, cy, None), (zero,) * len(chains))
        for sub in range(nsub):
            dqs = step(i * nsub + sub, dqs, sub)
        for c, (hh, r0) in enumerate(chains):
            dq_ref[r0:r0 + th, hsl[hh]] = dqs[c]
        pl.when((pl.program_id(0) == 3) & (i == nq - 1))(ride.finish)

    outs = pl.pallas_call(
        body, name="mla_bwd", grid=(4, nq),
        in_specs=[pl.BlockSpec((tq, 2 * HEAD_PAD), lambda p, i: (i, p)),
                  pl.BlockSpec((s_len, 2 * HEAD_PAD), lambda p, i: (0, p)),
                  pl.BlockSpec((s_len, LANES), lambda p, i: (0, p)),
                  pl.BlockSpec((tq, LANES), lambda p, i: (i, p)),
                  pl.BlockSpec((tq, LANES), lambda p, i: (i, p)),
                  pl.BlockSpec((None, tq, LANES), lambda p, i: (p, i, 0))] + ride.specs,
        out_specs=[pl.BlockSpec((tq, 2 * HEAD_PAD), lambda p, i: (i, p)),
                   pl.BlockSpec((s_len, 2 * HEAD_PAD), lambda p, i: (0, p)),
                   pl.BlockSpec((s_len, LANES), lambda p, i: (0, p))] + ride.specs,
        out_shape=[jax.ShapeDtypeStruct((s_len, 1024), F32), jax.ShapeDtypeStruct((s_len, 1024), F32),
                   jax.ShapeDtypeStruct((s_len, 512), F32)] + ride.out_shape,
        scratch_shapes=ride.scratch,
        compiler_params=_cparams(56, dimension_semantics=("arbitrary", "arbitrary")),
    )(q, k, v, o, do, lse, *ride.srcs)
    return outs[0], outs[1], outs[2], outs[3:]


def _log_sigmoids(z2):
    sp = jnp.log2(1.0 + jnp.exp2(-jnp.abs(z2)))
    lb = jnp.minimum(z2, 0.0) - sp
    return lb, lb - z2


def _split_dot(x, w, parts, nt=False):
    dot = _dot_nt if nt else _dot
    out = None
    for _ in range(parts):
        xb = x.astype(BF16)
        t = dot(xb, w)
        out = t if out is None else out + t
        x = x - xb.astype(F32)
    return out


def _sb_fwd(sb):
    s_len = sb.shape[0]
    tq, th, tk, nsub = _attn_blocks(s_len, tq=SB_TQ, th=SB_TH)
    chains = _chains(tq, th)
    nh = tq // th
    assert s_len // tk <= 64

    def body(q_ref, k_ref, v_ref, o_ref, r_ref):
        i = pl.program_id(1)
        lane = lax.broadcasted_iota(jnp.int32, (th, LANES), 1)
        upper = (lax.broadcasted_iota(jnp.int32, (tk, tk), 0)
                 > lax.broadcasted_iota(jnp.int32, (tk, tk), 1)).astype(BF16)
        qs = [jnp.where((lane // 64) == hh, q_ref[r0:r0 + th, :], jnp.zeros((), BF16)) for hh, r0 in chains]

        def visit(work, carry):
            blocks = {n: (k_ref[pl.ds(pl.multiple_of(kb * tk, tk), tk), :], v_ref[pl.ds(pl.multiple_of(kb * tk, tk), tk), :])
                      for n, kb, _ in work}
            masked = {n: not isinstance(mask, str) for n, _, mask in work}
            masks = {n: mask for n, _, mask in work}
            z = {n: _dot_nt(qs[n], blocks[n][0]) for n, _, _ in work}
            lb, lom = {}, {}
            for n, _, _ in work:
                lb[n], lom[n] = _log_sigmoids(z[n])
                if masked[n]:
                    lom[n] = jnp.where(masks[n], lom[n], 0.0)
            suf = {n: _split_dot(lom[n], upper, 1) for n, _, _ in work}
            a = {}
            for n, _, _ in work:
                a[n] = jnp.exp2(lb[n] + suf[n] + carry[n][0])
                if masked[n]:
                    a[n] = jnp.where(masks[n], a[n], 0.0)
            pv = {n: _dot(a[n].astype(BF16), blocks[n][1]) for n, _, _ in work}
            new = list(carry)
            for n, kb, _ in work:
                c, acc, r = carry[n]
                rs = suf[n][:, 0:1] + lom[n][:, 0:1]
                new[n] = (c + rs, acc + pv[n], jnp.where(lane == 64 * chains[n][0] + kb, rs, r))
            return tuple(new)

        init = (jnp.zeros((th, 1), F32), jnp.zeros((th, LANES), F32), jnp.zeros((th, LANES), F32))
        carry = (init,) * len(chains)
        spb = th // tk
        halves = [[n for n, (_, r0) in enumerate(chains) if r0 == h * th] for h in range(nh)]
        for s in reversed(range(spb)):
            carry = visit([(n, i * nsub + h * spb + s, _diag_mask(th, tk, h * th, h * spb + s, strict=True))
                           for h, mine in enumerate(halves) for n in mine], carry)

        for h, mine in enumerate(halves):
            start = i * nsub + h * spb

            def spent(cy, mine=mine):
                top = functools.reduce(jnp.maximum, [jnp.max(cy[n][0]) for n in mine])
                return (top < SB_CUT).astype(jnp.int32)

            def walk(state, mine=mine, start=start, spent=spent):
                t, _, cy = state
                cy = visit([(n, start - 1 - t, "all") for n in mine], cy)
                return t + 1, spent(cy), cy

            _, _, carry = lax.while_loop(lambda st, start=start: (st[0] < start) & (st[1] == 0), walk,
                                         (jnp.int32(0), spent(carry), carry))
        for n in range(nh):
            rs = slice(n * th, (n + 1) * th)
            o_ref[rs, :] = jnp.where(lane < 64, carry[n][1], carry[nh + n][1]).astype(BF16)
            r_ref[rs, :] = jnp.where(lane < 64, carry[n][2], carry[nh + n][2])

    return pl.pallas_call(
        body, name="sb_fwd", grid=(4, s_len // tq),
        in_specs=[pl.BlockSpec((tq, LANES), lambda p, i: (i, p)),
                  pl.BlockSpec((s_len, LANES), lambda p, i: (0, 4 + p)),
                  pl.BlockSpec((s_len, LANES), lambda p, i: (0, 8 + p))],
        out_specs=[pl.BlockSpec((tq, LANES), lambda p, i: (i, p)),
                   pl.BlockSpec((None, tq, LANES), lambda p, i: (p, i, 0))],
        out_shape=[jax.ShapeDtypeStruct((s_len, 512), BF16), jax.ShapeDtypeStruct((4, s_len, LANES), F32)],
        compiler_params=_cparams(40, dimension_semantics=("parallel", "arbitrary")),
    )(sb, sb, sb)


def _sb_bwd(sb, do, r):
    s_len = sb.shape[0]
    tq, th, tk, nsub = _attn_blocks(s_len)
    chains = _chains(tq, th)
    nh = tq // th

    def body(q_ref, k_ref, v_ref, do_ref, r_ref, dq_ref, dk_ref, dv_ref):
        i = pl.program_id(1)
        lane = lax.broadcasted_iota(jnp.int32, (th, LANES), 1)
        upper = (lax.broadcasted_iota(jnp.int32, (tk, tk), 0)
                 > lax.broadcasted_iota(jnp.int32, (tk, tk), 1)).astype(BF16)
        tri = (lax.broadcasted_iota(jnp.int32, (LANES, LANES), 0)
               > lax.broadcasted_iota(jnp.int32, (LANES, LANES), 1)).astype(BF16)

        @pl.when(i == 0)
        def _():
            dk_ref[...] = jnp.zeros_like(dk_ref)
            dv_ref[...] = jnp.zeros_like(dv_ref)

        qs, dos, rights = [], [], []
        for hh, r0 in chains:
            rs = slice(r0, r0 + th)
            hm = (lane // 64) == hh
            qs.append(jnp.where(hm, q_ref[rs, :], jnp.zeros((), BF16)))
            dos.append(jnp.where(hm, do_ref[rs, :], jnp.zeros((), BF16)))
            rights.append(_split_dot(jnp.where(hm, r_ref[rs, :], 0.0), tri, 3))

        def visit(groups, carry):
            new = list(carry)
            for kb, members in groups:
                rows = pl.ds(pl.multiple_of(kb * tk, tk), tk)
                kblk, vblk = k_ref[rows, :], v_ref[rows, :]
                a_all, do_all, dz_all, q_all = [], [], [], []
                for n, mask in members:
                    pre, dq = carry[n]
                    c = jnp.sum(jnp.where(lane == 64 * chains[n][0] + kb, rights[n], 0.0), axis=-1, keepdims=True)
                    z = _dot_nt(qs[n], kblk)
                    lb, lom = _log_sigmoids(z)
                    if not isinstance(mask, str):
                        lom = jnp.where(mask, lom, 0.0)
                    suf = _split_dot(lom, upper, 1)
                    a = jnp.exp2(lb + suf + c)
                    if not isinstance(mask, str):
                        a = jnp.where(mask, a, 0.0)
                    g = a * _dot_nt(dos[n], vblk)
                    left = _split_dot(g, upper, 1, nt=True) + pre
                    sig = jnp.exp2(lb)
                    dz = g * (1.0 - sig) - sig * left
                    if not isinstance(mask, str):
                        dz = jnp.where(mask, dz, 0.0)
                    dzb = dz.astype(BF16)
                    a_all.append(a.astype(BF16))
                    do_all.append(dos[n])
                    dz_all.append(dzb)
                    q_all.append(qs[n])
                    new[n] = (left[:, tk - 1:tk] + g[:, tk - 1:tk], dq + _dot(dzb, kblk))
                dv_ref[rows, :] += _dot_tn(jnp.concatenate(a_all, axis=0), jnp.concatenate(do_all, axis=0))
                dk_ref[rows, :] += _dot_tn(jnp.concatenate(dz_all, axis=0), jnp.concatenate(q_all, axis=0))
            return tuple(new)

        spb = th // tk
        lane1 = lax.broadcasted_iota(jnp.int32, (1, LANES), 1)
        init = (jnp.zeros((th, 1), F32), jnp.zeros((th, LANES), F32))
        carry = (init,) * len(chains)
        halves = [[n for n, (_, r0) in enumerate(chains) if r0 == h * th] for h in range(nh)]
        for h, mine in enumerate(halves):
            end = i * nsub + h * spb
            first = end
            for n in mine:
                top = jnp.max(rights[n], axis=0, keepdims=True)
                kb_of = lane1 - 64 * chains[n][0]
                live = (kb_of >= 0) & (kb_of < end) & (top >= SB_CUT)
                first = jnp.minimum(first, jnp.min(jnp.where(live, kb_of, end)))
            carry = lax.fori_loop(
                first, end, lambda kb, cy, mine=mine: visit([(kb, [(n, "all") for n in mine])], cy), carry)
        for s in range(spb):
            carry = visit([(i * nsub + h * spb + s,
                            [(n, _diag_mask(th, tk, h * th, h * spb + s, strict=True)) for n in mine])
                           for h, mine in enumerate(halves)], carry)
        for n in range(nh):
            dq_ref[n * th:(n + 1) * th, :] = jnp.where(lane < 64, carry[n][1], carry[nh + n][1]) * SB_SCALE

    return pl.pallas_call(
        body, name="sb_bwd", grid=(4, s_len // tq),
        in_specs=[pl.BlockSpec((tq, LANES), lambda p, i: (i, p)),
                  pl.BlockSpec((s_len, LANES), lambda p, i: (0, 4 + p)),
                  pl.BlockSpec((s_len, LANES), lambda p, i: (0, 8 + p)),
                  pl.BlockSpec((tq, LANES), lambda p, i: (i, p)),
                  pl.BlockSpec((None, tq, LANES), lambda p, i: (p, i, 0))],
        out_specs=[pl.BlockSpec((tq, LANES), lambda p, i: (i, p)),
                   pl.BlockSpec((s_len, LANES), lambda p, i: (0, p)),
                   pl.BlockSpec((s_len, LANES), lambda p, i: (0, p))],
        out_shape=[jax.ShapeDtypeStruct((s_len, 512), F32)] * 3,
        compiler_params=_cparams(48, dimension_semantics=("arbitrary", "arbitrary")),
    )(sb, sb, sb, do, r)


def _merge_fwd(x, oa, ob, gates, bg, wa, wb, wo):
    s_len = x.shape[0]
    tm = _row_block(s_len)

    def body(x_ref, oa_ref, ob_ref, g_ref, bg_ref, wa_ref, wb_ref, wo_ref, y_ref):
        pa = _dot(oa_ref[...], wa_ref[...])
        pb = _dot(ob_ref[...], wb_ref[...])
        merged = (_sigmoid(g_ref[:, 0:D_MODEL] + bg_ref[0:1, :]) * pa
                  + _sigmoid(g_ref[:, D_MODEL:2 * D_MODEL] + bg_ref[1:2, :]) * pb)
        y_ref[...] = x_ref[...] + _dot(merged.astype(BF16), wo_ref[...])

    full = lambda shape: pl.BlockSpec(shape, lambda i: (0, 0))
    rowb = lambda n: pl.BlockSpec((tm, n), lambda i: (i, 0))
    return pl.pallas_call(
        body, name="merge_fwd", grid=(s_len // tm,),
        in_specs=[rowb(1024), rowb(512), rowb(512), rowb(2048), full((2, 1024)), full((512, 1024)),
                  full((512, 1024)), full((1024, 1024))],
        out_specs=rowb(1024),
        out_shape=jax.ShapeDtypeStruct((s_len, D_MODEL), F32),
        compiler_params=_cparams(48, dimension_semantics=("parallel",)),
    )(x, oa, ob, gates, bg, wa, wb, wo)


def _merge_bwd(dx1, oa, ob, gates, bg, wa, wb, wo):
    s_len = dx1.shape[0]
    tm = _row_block(s_len)

    def body(dx_ref, oa_ref, ob_ref, g_ref, bg_ref, wa_ref, wb_ref, wo_ref,
             doa_ref, dob_ref, dgate_ref, dpa_ref, dpb_ref, merged_ref, dxb_ref, dbg_ref):
        first = pl.program_id(0) == 0
        dxb = dx_ref[...].astype(BF16)
        dxb_ref[...] = dxb
        pa = _dot(oa_ref[...], wa_ref[...])
        pb = _dot(ob_ref[...], wb_ref[...])
        sa = _sigmoid(g_ref[:, 0:D_MODEL] + bg_ref[0:1, :])
        sbg = _sigmoid(g_ref[:, D_MODEL:2 * D_MODEL] + bg_ref[1:2, :])
        merged_ref[...] = (sa * pa + sbg * pb).astype(BF16)
        dm = _dot_nt(dxb, wo_ref[...])
        dpa = (dm * sa).astype(BF16)
        dpb = (dm * sbg).astype(BF16)
        dpa_ref[...] = dpa
        dpb_ref[...] = dpb
        dga = dm * pa * sa * (1.0 - sa)
        dgb = dm * pb * sbg * (1.0 - sbg)
        dgate_ref[:, 0:D_MODEL] = dga.astype(BF16)
        dgate_ref[:, D_MODEL:2 * D_MODEL] = dgb.astype(BF16)
        _acc_rows(dbg_ref.at[0:1, :], dga, first)
        _acc_rows(dbg_ref.at[1:2, :], dgb, first)
        doa_ref[...] = _dot_nt(dpa, wa_ref[...]).astype(BF16)
        dob_ref[...] = _dot_nt(dpb, wb_ref[...]).astype(BF16)

    full = lambda shape: pl.BlockSpec(shape, lambda i: (0, 0))
    rowb = lambda n: pl.BlockSpec((tm, n), lambda i: (i, 0))
    sds = lambda n, dt: jax.ShapeDtypeStruct((s_len, n), dt)
    return pl.pallas_call(
        body, name="merge_bwd", grid=(s_len // tm,),
        in_specs=[rowb(1024), rowb(512), rowb(512), rowb(2048), full((2, 1024)), full((512, 1024)),
                  full((512, 1024)), full((1024, 1024))],
        out_specs=[rowb(512), rowb(512), rowb(2048), rowb(1024), rowb(1024), rowb(1024), rowb(1024),
                   full((2, 1024))],
        out_shape=[sds(512, BF16), sds(512, BF16), sds(2048, BF16), sds(1024, BF16), sds(1024, BF16),
                   sds(1024, BF16), sds(1024, BF16), jax.ShapeDtypeStruct((2, 1024), F32)],
        compiler_params=_cparams(48, dimension_semantics=("arbitrary",)),
    )(dx1, oa, ob, gates, bg, wa, wb, wo)


def _mem_kv(mem, g, w):
    m_len = mem.shape[0]

    def body(mem_ref, g_ref, w_ref, mn_ref, kv_ref):
        mn, _, _ = _rms(mem_ref[...], g_ref[...])
        mnb = mn.astype(BF16)
        mn_ref[...] = mnb
        kv_ref[...] = _dot(mnb, w_ref[...]).astype(BF16)

    return pl.pallas_call(
        body, name="mem_kv",
        out_shape=[jax.ShapeDtypeStruct((m_len, D_MODEL), BF16), jax.ShapeDtypeStruct((m_len, 1024), BF16)],
    )(mem, g, w)


def _mem_bwd(mem, g, w, mn, dkv):
    def body(mem_ref, g_ref, w_ref, mn_ref, dkv_ref, dw_ref, dg_ref):
        dkvb = dkv_ref[...].astype(BF16)
        dw_ref[...] = _dot_tn(mn_ref[...], dkvb)
        dmn = _dot_nt(dkvb, w_ref[...])
        _, xh, _ = _rms(mem_ref[...], g_ref[...])
        dg_ref[...] = jnp.sum(dmn * xh, axis=0, keepdims=True)

    return pl.pallas_call(
        body, name="mem_bwd",
        out_shape=[jax.ShapeDtypeStruct((D_MODEL, 1024), F32), jax.ShapeDtypeStruct((1, D_MODEL), F32)],
    )(mem, g, w, mn, dkv)


def _xattn_heads(xqb, kv_ref, m_len):
    ps = []
    for h in range(X_HEADS):
        hs = slice(h * X_HEAD_DIM, (h + 1) * X_HEAD_DIM)
        s = _dot_nt(xqb[:, hs], kv_ref[:, hs]) * X_SCALE
        e = jnp.exp(s - jnp.max(s, axis=-1, keepdims=True))
        ps.append(e / jnp.sum(e, axis=-1, keepdims=True))
    return ps


def _xattn_fwd(x1, g, wxq, kv, wxo):
    s_len, m_len = x1.shape[0], kv.shape[0]
    tm = _row_block(s_len)

    def body(x_ref, g_ref, wq_ref, kv_ref, wo_ref, y_ref):
        hx, _, _ = _rms(x_ref[...], g_ref[...])
        xqb = _dot(hx.astype(BF16), wq_ref[...]).astype(BF16)
        ps = _xattn_heads(xqb, kv_ref, m_len)
        xo = jnp.concatenate(
            [_dot(ps[h].astype(BF16), kv_ref[:, 512 + h * X_HEAD_DIM:512 + (h + 1) * X_HEAD_DIM])
             for h in range(X_HEADS)], axis=-1)
        y_ref[...] = x_ref[...] + _dot(xo.astype(BF16), wo_ref[...])

    full = lambda shape: pl.BlockSpec(shape, lambda i: (0, 0))
    rowb = lambda n: pl.BlockSpec((tm, n), lambda i: (i, 0))
    return pl.pallas_call(
        body, name="xattn_fwd", grid=(s_len // tm,),
        in_specs=[rowb(1024), full((1, 1024)), full((1024, 512)), full((m_len, 1024)), full((512, 1024))],
        out_specs=rowb(1024),
        out_shape=jax.ShapeDtypeStruct((s_len, D_MODEL), F32),
        compiler_params=_cparams(48, dimension_semantics=("parallel",)),
    )(x1, g, wxq, kv, wxo)


def _xattn_bwd(x1, dx2, g, wxq, kv, wxo):
    s_len, m_len = x1.shape[0], kv.shape[0]
    tm = _row_block(s_len)

    def body(x_ref, dy_ref, g_ref, wq_ref, kv_ref, wo_ref, dx_ref, dwq_ref, dwo_ref, dkv_ref, dg_ref):
        first = pl.program_id(0) == 0
        gv = g_ref[...]
        hx, xh, r = _rms(x_ref[...], gv)
        hxb = hx.astype(BF16)
        xqb = _dot(hxb, wq_ref[...]).astype(BF16)
        ps = _xattn_heads(xqb, kv_ref, m_len)
        dy = dy_ref[...]
        dyb = dy.astype(BF16)
        dxo = _dot_nt(dyb, wo_ref[...])
        xos, dqs, dks, dvs = [], [], [], []
        for h in range(X_HEADS):
            hs = slice(h * X_HEAD_DIM, (h + 1) * X_HEAD_DIM)
            vs = slice(512 + h * X_HEAD_DIM, 512 + (h + 1) * X_HEAD_DIM)
            p = ps[h]
            pb = p.astype(BF16)
            dxoh = dxo[:, hs].astype(BF16)
            xos.append(_dot(pb, kv_ref[:, vs]))
            dp = _dot_nt(dxoh, kv_ref[:, vs])
            ds = (p * (dp - jnp.sum(dp * p, axis=-1, keepdims=True)) * X_SCALE).astype(BF16)
            dvs.append(_dot_tn(pb, dxoh))
            dks.append(_dot_tn(ds, xqb[:, hs]))
            dqs.append(_dot(ds, kv_ref[:, hs]))
        xob = jnp.concatenate(xos, axis=-1).astype(BF16)
        dxqb = jnp.concatenate(dqs, axis=-1).astype(BF16)
        _acc(dwo_ref, _dot_tn(xob, dyb), first)
        _acc(dwq_ref, _dot_tn(hxb, dxqb), first)
        _acc(dkv_ref, jnp.concatenate(dks + dvs, axis=-1), first)
        dhx = _dot_nt(dxqb, wq_ref[...])
        dx, dgr = _rms_bwd(dhx, xh, r, gv)
        dx_ref[...] = dy + dx
        _acc_rows(dg_ref, dgr, first)

    full = lambda shape: pl.BlockSpec(shape, lambda i: (0, 0))
    rowb = lambda n: pl.BlockSpec((tm, n), lambda i: (i, 0))
    return pl.pallas_call(
        body, name="xattn_bwd", grid=(s_len // tm,),
        in_specs=[rowb(1024), rowb(1024), full((1, 1024)), full((1024, 512)), full((m_len, 1024)),
                  full((512, 1024))],
        out_specs=[rowb(1024), full((1024, 512)), full((512, 1024)), full((m_len, 1024)), full((1, 1024))],
        out_shape=[jax.ShapeDtypeStruct((s_len, D_MODEL), F32), jax.ShapeDtypeStruct((1024, 512), F32),
                   jax.ShapeDtypeStruct((512, 1024), F32), jax.ShapeDtypeStruct((m_len, 1024), F32),
                   jax.ShapeDtypeStruct((1, D_MODEL), F32)],
        compiler_params=_cparams(48, dimension_semantics=("arbitrary",)),
    )(x1, dx2, g, wxq, kv, wxo)


FF_TILE = 1408


def _ffn_fwd(x2, g, wg, wu, wd):
    s_len = x2.shape[0]
    tm, tf = _row_block(s_len), FF_TILE

    def body(x_ref, g_ref, wg_ref, wu_ref, wd_ref, y_ref, h_ref):
        j = pl.program_id(1)

        @pl.when(j == 0)
        def _():
            hf, _, _ = _rms(x_ref[...], g_ref[...])
            h_ref[...] = hf.astype(BF16)
            y_ref[...] = x_ref[...]

        hb = h_ref[...]
        gt = _dot(hb, wg_ref[...])
        up = _dot(hb, wu_ref[...])
        act = gt * _sigmoid(gt) * up
        y_ref[...] += _dot(act.astype(BF16), wd_ref[...])

    rowb = pl.BlockSpec((tm, D_MODEL), lambda i, j: (i, 0))
    return pl.pallas_call(
        body, name="ffn_fwd", grid=(s_len // tm, D_FF // tf),
        in_specs=[rowb, pl.BlockSpec((1, D_MODEL), lambda i, j: (0, 0)),
                  pl.BlockSpec((D_MODEL, tf), lambda i, j: (0, j)),
                  pl.BlockSpec((D_MODEL, tf), lambda i, j: (0, j)),
                  pl.BlockSpec((tf, D_MODEL), lambda i, j: (j, 0))],
        out_specs=[rowb, rowb],
        out_shape=[jax.ShapeDtypeStruct((s_len, D_MODEL), F32), jax.ShapeDtypeStruct((s_len, D_MODEL), BF16)],
        compiler_params=_cparams(48, dimension_semantics=("parallel", "arbitrary")),
    )(x2, g, wg, wu, wd)


def _ffn_bwd(x2, hf, dx3, dx3b, g, wg, wu, wd):
    s_len = x2.shape[0]
    tm, tf = _row_block(s_len), FF_TILE
    nf = D_FF // tf

    def act_body(h_ref, dy_ref, wg_ref, wu_ref, wd_ref, dgt_ref, dup_ref, act_ref):
        hb = h_ref[...]
        gt = _dot(hb, wg_ref[...])
        up = _dot(hb, wu_ref[...])
        sg = _sigmoid(gt)
        silu = gt * sg
        dact = _dot_nt(dy_ref[...], wd_ref[...])
        dgt_ref[...] = (dact * up * (sg * (1.0 + gt * (1.0 - sg)))).astype(BF16)
        dup_ref[...] = (dact * silu).astype(BF16)
        act_ref[...] = (silu * up).astype(BF16)

    rowb = pl.BlockSpec((tm, D_MODEL), lambda i, j: (i, 0))
    ffb = pl.BlockSpec((tm, tf), lambda i, j: (i, j))
    dgt, dup, act = pl.pallas_call(
        act_body, name="ffn_bwd_act", grid=(s_len // tm, nf),
        in_specs=[rowb, rowb,
                  pl.BlockSpec((D_MODEL, tf), lambda i, j: (0, j)),
                  pl.BlockSpec((D_MODEL, tf), lambda i, j: (0, j)),
                  pl.BlockSpec((tf, D_MODEL), lambda i, j: (j, 0))],
        out_specs=[ffb, ffb, ffb],
        out_shape=[jax.ShapeDtypeStruct((s_len, D_FF), BF16)] * 3,
        compiler_params=_cparams(56, dimension_semantics=("parallel", "arbitrary")),
    )(hf, dx3b, wg, wu, wd)
    tm = min(s_len, 256)

    def in_body(x_ref, dy_ref, g_ref, wg_ref, wu_ref, dgt_ref, dup_ref, dx_ref, dg_ref):
        dh = _dot_nt(dgt_ref[...], wg_ref[...]) + _dot_nt(dup_ref[...], wu_ref[...])
        gv = g_ref[...]
        _, xh, r = _rms(x_ref[...], gv)
        dx, dgr = _rms_bwd(dh, xh, r, gv)
        dx_ref[...] = dy_ref[...] + dx
        _acc_rows(dg_ref, dgr, pl.program_id(0) == 0)

    row1 = lambda n: pl.BlockSpec((tm, n), lambda i: (i, 0))
    full = lambda shape: pl.BlockSpec(shape, lambda i: (0, 0))
    dx2, dg = pl.pallas_call(
        in_body, name="ffn_bwd_in", grid=(s_len // tm,),
        in_specs=[row1(D_MODEL), row1(D_MODEL), full((1, D_MODEL)), full((D_MODEL, D_FF)), full((D_MODEL, D_FF)),
                  row1(D_FF), row1(D_FF)],
        out_specs=[row1(D_MODEL), full((1, D_MODEL))],
        out_shape=[jax.ShapeDtypeStruct((s_len, D_MODEL), F32), jax.ShapeDtypeStruct((1, D_MODEL), F32)],
        compiler_params=_cparams(48, dimension_semantics=("arbitrary",)),
    )(x2, dx3, g, wg, wu, dgt, dup)
    return dx2, dgt, dup, act, dg


def _loss_head(x3, g, target):
    s_len = x3.shape[0]
    tm = _row_block(s_len)

    def body(x_ref, g_ref, t_ref, sse_ref, dx_ref, dxb_ref, dg_ref):
        first = pl.program_id(0) == 0
        gv = g_ref[...]
        y, xh, r = _rms(x_ref[...], gv)
        err = y - t_ref[...]
        _acc(sse_ref, jnp.broadcast_to(jnp.sum(err * err), (8, LANES)), first)
        dx, dgr = _rms_bwd(err * (1.0 / D_MODEL), xh, r, gv)
        dx_ref[...] = dx
        dxb_ref[...] = dx.astype(BF16)
        _acc_rows(dg_ref, dgr, first)

    rowb = pl.BlockSpec((tm, D_MODEL), lambda i: (i, 0))
    return pl.pallas_call(
        body, name="loss_head", grid=(s_len // tm,),
        in_specs=[rowb, pl.BlockSpec((1, D_MODEL), lambda i: (0, 0)), rowb],
        out_specs=[pl.BlockSpec((8, LANES), lambda i: (0, 0)), rowb, rowb,
                   pl.BlockSpec((1, D_MODEL), lambda i: (0, 0))],
        out_shape=[jax.ShapeDtypeStruct((8, LANES), F32), jax.ShapeDtypeStruct((s_len, D_MODEL), F32),
                   jax.ShapeDtypeStruct((s_len, D_MODEL), BF16), jax.ShapeDtypeStruct((1, D_MODEL), F32)],
        compiler_params=_cparams(dimension_semantics=("arbitrary",)),
    )(x3, g, target)


def _mla_prep_bwd(lat, g_q, g_kv, w_uq, w_uk, w_uv, cosf, sinf, dq, dk, dv):
    s_len = lat.shape[0]
    tm = _row_block(s_len)

    def body(lat_ref, gq_ref, gkv_ref, wuq_ref, wuk_ref, wuv_ref, cos_ref, sin_ref, dq_ref, dk_ref, dv_ref,
             dlat_ref, dqb_ref, dkb_ref, dvb_ref, dgq_ref, dgkv_ref):
        first = pl.program_id(0) == 0
        lane = lax.broadcasted_iota(jnp.int32, (tm, LANES), 1)
        cosv, sinv = cos_ref[...], sin_ref[...]
        gq, gkv = gq_ref[...], gkv_ref[...]
        _, qxh, qr = _rms(lat_ref[:, 0:256], gq)
        _, kxh, kr_ = _rms(lat_ref[:, 256:384], gkv)
        dkr = jnp.zeros((tm, LANES), F32)
        for h in range(MLA_HEADS):
            sl = slice(h * HEAD_PAD, (h + 1) * HEAD_PAD)
            blk = dq_ref[:, sl]
            dqb_ref[:, sl] = (blk * cosv + _rope_rot_t(blk, lane) * sinv).astype(BF16)
            kblk = dk_ref[:, sl] * (1.0 / MLA_Q_FOLD)
            dkb_ref[:, sl] = kblk.astype(BF16)
            dkr = dkr + kblk
        dvb = dv_ref[...].astype(BF16)
        dvb_ref[...] = dvb
        dkr = jnp.where((lane >= 64) & (lane < 96), dkr, 0.0)
        dkr = dkr * cosv + _rope_rot_t(dkr, lane) * sinv
        dql = _dot_nt(dqb_ref[...], wuq_ref[...])
        dkvl = _dot_nt(dkb_ref[...], wuk_ref[...]) + _dot_nt(dvb, wuv_ref[...])
        dcq, dgqr = _rms_bwd(dql, qxh, qr, gq)
        dckv, dgkvr = _rms_bwd(dkvl, kxh, kr_, gkv)
        dlat_ref[:, 0:256] = dcq
        dlat_ref[:, 256:384] = dckv
        dlat_ref[:, K_R_OFF:K_R_OFF + LANES] = pltpu.roll(dkr, 64, 1)
        _acc_rows(dgq_ref, dgqr, first)
        _acc_rows(dgkv_ref, dgkvr, first)

    full = lambda shape: pl.BlockSpec(shape, lambda i: (0, 0))
    rowb = lambda n: pl.BlockSpec((tm, n), lambda i: (i, 0))
    sds = lambda n, dt: jax.ShapeDtypeStruct((s_len, n), dt)
    return pl.pallas_call(
        body, name="mla_prep_bwd", grid=(s_len // tm,),
        in_specs=[rowb(512), full((1, 256)), full((1, 128)), full((256, 1024)), full((128, 1024)),
                  full((128, 512)), rowb(128), rowb(128), rowb(1024), rowb(1024), rowb(512)],
        out_specs=[rowb(512), rowb(1024), rowb(1024), rowb(512), full((1, 256)), full((1, 128))],
        out_shape=[sds(512, F32), sds(1024, BF16), sds(1024, BF16), sds(512, BF16),
                   jax.ShapeDtypeStruct((1, 256), F32), jax.ShapeDtypeStruct((1, 128), F32)],
        compiler_params=_cparams(48, dimension_semantics=("arbitrary",)),
    )(lat, g_q, g_kv, w_uq, w_uk, w_uv, cosf, sinf, dq, dk, dv)


def _in_proj_bwd(x, g, w, dx1, dlat, dsbq, dsbk, dsbv, dgates, ride):
    s_len = x.shape[0]
    tm = min(s_len, 256)
    nb = s_len // tm

    def body(x_ref, g_ref, w_ref, dx1_ref, dlat_ref, dq_ref, dk_ref, dv_ref, dgate_ref, *rest):
        gx_ref, dg_ref = rest[ride.n:ride.n + 2]
        dproj = rest[-1]
        ride.bind(rest[:ride.n], rest[ride.n + 2:2 * ride.n + 2], rest[2 * ride.n + 2:-1])
        pl.when(pl.program_id(0) == 0)(ride.issue)
        dproj[:, 0:512] = dlat_ref[...].astype(BF16)
        dproj[:, 512:1024] = dq_ref[...].astype(BF16)
        dproj[:, 1024:1536] = (dk_ref[...] * LN2).astype(BF16)
        dproj[:, 1536:2048] = dv_ref[...].astype(BF16)
        dproj[:, 2048:4096] = dgate_ref[...]
        dh = _dot_nt(dproj[...], w_ref[...])
        gv = g_ref[...]
        _, xh, r = _rms(x_ref[...], gv)
        dx, dgr = _rms_bwd(dh, xh, r, gv)
        gx_ref[...] = dx1_ref[...] + dx
        _acc_rows(dg_ref, dgr, pl.program_id(0) == 0)
        pl.when(pl.program_id(0) == nb - 1)(ride.finish)

    rowb = lambda n: pl.BlockSpec((tm, n), lambda i: (i, 0))
    full = lambda shape: pl.BlockSpec(shape, lambda i: (0, 0))
    outs = pl.pallas_call(
        body, name="in_proj_bwd", grid=(nb,),
        in_specs=[rowb(D_MODEL), full((1, D_MODEL)), full((D_MODEL, D_IN_PAD)), rowb(D_MODEL),
                  rowb(512), rowb(512), rowb(512), rowb(512), rowb(2 * D_MODEL)] + ride.specs,
        out_specs=[rowb(D_MODEL), full((1, D_MODEL))] + ride.specs,
        out_shape=[jax.ShapeDtypeStruct((s_len, D_MODEL), F32), jax.ShapeDtypeStruct((1, D_MODEL), F32)]
        + ride.out_shape,
        scratch_shapes=ride.scratch + [pltpu.VMEM((tm, D_IN_PAD), BF16)],
        compiler_params=_cparams(48, dimension_semantics=("arbitrary",)),
    )(x, g, w, dx1, dlat, dsbq, dsbk, dsbv, dgates, *ride.srcs)
    return outs[0], outs[1], outs[2:]


def _adamw(landed, w, m, v, name):
    r, c = w.shape
    lanes = _round_up(c, LANES)
    tb = r
    for cand in range(r, 0, -1):
        if r % cand == 0 and (cand % 8 == 0 or cand == r) and N_DEV * cand * lanes * 4 <= ADAM_BLOCK_BYTES:
            tb = cand
            break
    c1 = 1.0 - ADAM_B1 ** ADAM_STEP
    c2 = 1.0 - ADAM_B2 ** ADAM_STEP

    def body(l_ref, w_ref, m_ref, v_ref, g_ref, d_ref, nm_ref, nv_ref):
        g = l_ref[0]
        for k in range(1, N_DEV):
            g = g + l_ref[k]
        nm = ADAM_B1 * m_ref[...] + (1.0 - ADAM_B1) * g
        nv = ADAM_B2 * v_ref[...] + (1.0 - ADAM_B2) * (g * g)
        g_ref[...] = g
        nm_ref[...] = nm
        nv_ref[...] = nv
        d_ref[...] = -ADAM_LR * ((nm / c1) / (jnp.sqrt(nv / c2) + ADAM_EPS) + ADAM_WD * w_ref[...])

    blk = pl.BlockSpec((tb, c), lambda i: (i, 0))
    return pl.pallas_call(
        body, name=name, grid=(r // tb,),
        in_specs=[pl.BlockSpec((N_DEV, tb, c), lambda i: (0, i, 0)), blk, blk, blk],
        out_specs=[blk, blk, blk, blk],
        out_shape=[jax.ShapeDtypeStruct((r, c), F32)] * 4,
        compiler_params=_cparams(dimension_semantics=("parallel",)),
    )(landed, w, m, v)


def _shard_shape(shape, axis):
    return tuple(d // N_DEV if a == axis else d for a, d in enumerate(shape))


def _split_pieces(full, axis):
    r, c = full.shape
    if axis == 0:
        return full.reshape(N_DEV, r // N_DEV, c)
    return full.reshape(r, N_DEV, c // N_DEV).transpose(1, 0, 2)


def _join_shards(gathered, axis):
    _, r, c = gathered.shape
    if axis == 0:
        return gathered.reshape(N_DEV * r, c)
    return gathered.transpose(1, 0, 2).reshape(r, N_DEV * c)


def kernel(x, mem, positions, g_mix, w_in, b_gate, g_q_lat, w_uq, g_kv_lat, w_ukv, w_a_proj, w_b_proj, w_o, g_x, g_mem, w_xq, w_xkv, w_xo, g_ffn, w_gate, w_up, w_down, g_final, loss_target, m_g_mix, m_w_in, m_b_gate, m_g_q_lat, m_w_uq, m_g_kv_lat, m_w_ukv, m_w_a_proj, m_w_b_proj, m_w_o, m_g_x, m_g_mem, m_w_xq, m_w_xkv, m_w_xo, m_g_ffn, m_w_gate, m_w_up, m_w_down, m_g_final, v_g_mix, v_w_in, v_b_gate, v_g_q_lat, v_w_uq, v_g_kv_lat, v_w_ukv, v_w_a_proj, v_w_b_proj, v_w_o, v_g_x, v_g_mem, v_w_xq, v_w_xkv, v_w_xo, v_g_ffn, v_w_gate, v_w_up, v_w_down, v_g_final):
    given = dict(locals())
    s_len = x.shape[1]
    x2d = x.reshape(s_len, D_MODEL)
    mem2d = mem.reshape(-1, D_MODEL)
    target = loss_target.reshape(s_len, D_MODEL)

    names = [name for name, _, _ in SHARDED]
    axis_of = {name: axis for name, _, axis in SHARDED}
    shard2d = lambda name, prefix="": given[prefix + name].reshape(
        _shard_shape(dict((n, s) for n, s, _ in SHARDED)[name], axis_of[name]))

    wire = lambda name: shard2d(name) if name == "b_gate" else shard2d(name).astype(BF16)
    early = [n for n in names if n in NEEDED_FIRST]
    late = [n for n in names if n not in NEEDED_FIRST]
    h1, gathered = _rms_cast(x2d, g_mix, _Exchange(True, [wire(n) for n in early]))
    wts = {n: _join_shards(g, axis_of[n]) for n, g in zip(early, gathered)}

    w_in_p = jnp.concatenate([wts["w_in"][:, :LAT_COLS], jnp.zeros((D_MODEL, D_IN_PAD - D_IN), BF16),
                              wts["w_in"][:, LAT_COLS:]], axis=1)
    w_uq_p = jnp.pad(wts["w_uq"].reshape(256, MLA_HEADS, 96), ((0, 0), (0, 0), (0, 32))).reshape(256, 1024)
    ukv = wts["w_ukv"].reshape(128, MLA_HEADS, 128)
    w_uk_p = jnp.pad(ukv[:, :, :64], ((0, 0), (0, 0), (0, 64))).reshape(128, 1024)
    w_uv = ukv[:, :, 64:].reshape(128, 512)
    w_uv1 = jnp.pad(ukv[:, :, 64:], ((0, 0), (0, 0), (0, 64))).reshape(128, 1024)
    bg = wts["b_gate"]

    inv_freq = ROPE_THETA ** (-jnp.arange(0, MLA_ROPE, 2, dtype=F32) / MLA_ROPE)
    ang = positions.reshape(s_len).astype(F32)[:, None] * inv_freq
    cos16, sin16 = jnp.cos(ang), jnp.sin(ang)
    cosf = jnp.concatenate([jnp.ones((s_len, 64), F32), cos16, cos16, jnp.ones((s_len, 32), F32)], axis=1)
    sinf = jnp.concatenate([jnp.zeros((s_len, 64), F32), sin16, sin16, jnp.zeros((s_len, 32), F32)], axis=1)

    lat, sb, gates = _in_proj_mm(h1, w_in_p)
    qa, ka, va, va1, q_lat, kv_lat = _mla_prep(lat, g_q_lat, g_kv_lat, w_uq_p, w_uk_p, w_uv, w_uv1, cosf, sinf)
    oa, lse, gathered = _mla_fwd(qa, ka, va1, _Exchange(True, [wire(n) for n in late]))
    wts.update({n: _join_shards(g, axis_of[n]) for n, g in zip(late, gathered)})
    ob, sb_r = _sb_fwd(sb)
    x1 = _merge_fwd(x2d, oa, ob, gates, bg, wts["w_a_proj"], wts["w_b_proj"], wts["w_o"])
    mn, xkv = _mem_kv(mem2d, g_mem, wts["w_xkv"])
    x2 = _xattn_fwd(x1, g_x, wts["w_xq"], xkv, wts["w_xo"])
    x3, hf = _ffn_fwd(x2, g_ffn, wts["w_gate"], wts["w_up"], wts["w_down"])
    g_final2d = g_final.reshape(1, D_MODEL)
    sse, dx3, dx3b, dg_final = _loss_head(x3, g_final2d, target)

    dx2, dgt, dup, act, dg_ffn = _ffn_bwd(x2, hf, dx3, dx3b, g_ffn, wts["w_gate"], wts["w_up"], wts["w_down"])
    dx1, dw_xq, dw_xo, dxkv, dg_x = _xattn_bwd(x1, dx2, g_x, wts["w_xq"], xkv, wts["w_xo"])
    dw_xkv, dg_mem = _mem_bwd(mem2d, g_mem, wts["w_xkv"], mn, dxkv)
    doa, dob, dgates, dpa, dpb, merged, dx1b, dbg = _merge_bwd(
        dx1, oa, ob, gates, bg, wts["w_a_proj"], wts["w_b_proj"], wts["w_o"])
    dsbq, dsbk, dsbv = _sb_bwd(sb, dob, sb_r)
    full_grads = {
        "w_a_proj": _tn_matmul(oa, dpa, "dw_a"),
        "w_b_proj": _tn_matmul(ob, dpb, "dw_b"),
        "w_o": _tn_matmul(merged, dx1b, "dw_o"),
        "w_xq": dw_xq,
        "w_xkv": dw_xkv,
        "w_xo": dw_xo,
        "w_gate": _tn_matmul(hf, dgt, "dw_gate", tn=FF_TILE),
        "w_up": _tn_matmul(hf, dup, "dw_up", tn=FF_TILE),
        "w_down": _tn_matmul(act, dx3b, "dw_down", tka=FF_TILE),
    }
    dw_sb = _tn_matmul_sb(h1, dsbq, dsbk, dsbv, "dw_in_sb")
    own0 = D_IN // N_DEV - LAT_COLS
    w_in_rest = _split_pieces(
        jnp.concatenate([dw_sb[:, :LAT_COLS], dw_sb, _tn_matmul(h1, dgates, "dw_in_gates")], axis=1), 1)
    dqa, dka, dva, got = _mla_bwd(
        qa, ka, va, oa, doa, lse,
        _Exchange(False, [_split_pieces(full_grads[n], axis_of[n]) for n in late] + [w_in_rest]))
    landed = dict(zip(late, got[:-1]))
    dlat, dqb, dkb, dvb, dg_q, dg_kv = _mla_prep_bwd(
        lat, g_q_lat, g_kv_lat, w_uq_p, w_uk_p, w_uv, cosf, sinf, dqa, dka, dva)
    w_in_first = jnp.concatenate([_tn_matmul(h1, dlat, "dw_in_lat")[:, :LAT_COLS], dw_sb[:, :own0]], axis=1)
    dw_uq_p = _tn_matmul(q_lat, dqb, "dw_uq")
    dw_uk_p = _tn_matmul(kv_lat, dkb, "dw_uk")
    dw_uv = _tn_matmul(kv_lat, dvb, "dw_uv")
    full_grads.update({
        "b_gate": dbg,
        "w_uq": dw_uq_p.reshape(256, MLA_HEADS, 128)[:, :, :96].reshape(256, 768),
        "w_ukv": jnp.concatenate([dw_uk_p.reshape(128, MLA_HEADS, 128)[:, :, :64],
                                  dw_uv.reshape(128, MLA_HEADS, 64)], axis=2).reshape(128, 1024),
    })
    small = [n for n in early if n != "w_in"]
    grad_x, dg_mix, got2 = _in_proj_bwd(
        x2d, g_mix, w_in_p, dx1, dlat, dsbq, dsbk, dsbv, dgates,
        _Exchange(False, [_split_pieces(full_grads[n], axis_of[n]) for n in small], to_first=[w_in_first]))
    landed.update(zip(small, got2[:-1]))
    me = 4 * lax.axis_index("x") + 2 * lax.axis_index("y") + lax.axis_index("c")
    landed["w_in"] = jnp.where(me == 0, got2[-1], got[-1])
    rep_grads = {"g_mix": dg_mix, "g_q_lat": dg_q, "g_kv_lat": dg_kv, "g_x": dg_x, "g_mem": dg_mem,
                 "g_ffn": dg_ffn, "g_final": dg_final}
    tail = jnp.zeros((8, LANES), F32)
    rep_cat = lambda prefix, src, last=tail: jnp.concatenate(
        [jnp.concatenate([src[prefix + n].reshape(-1) for n, _ in REPLICATED]).reshape(-1, LANES), last], axis=0)
    rep_own = rep_cat("", rep_grads, jnp.concatenate([sse[0:1], tail[1:]], axis=0))
    rep_src = jnp.broadcast_to(rep_own, (N_DEV,) + rep_own.shape)
    rep_landed = _exchange(False, [rep_src], "grads_gains")[0]

    res = {}
    for name, _, _ in SHARDED:
        outs = _adamw(landed[name], shard2d(name), shard2d(name, "m_"), shard2d(name, "v_"), "adamw_" + name)
        res[name] = [o.reshape(given[name].shape) for o in outs]
    rep_outs = _adamw(rep_landed, rep_cat("", given), rep_cat("m_", given), rep_cat("v_", given), "adamw_gains")
    off = 0
    for name, n in REPLICATED:
        res[name] = [o.reshape(-1)[off:off + n].reshape(given[name].shape) for o in rep_outs]
        off += n
    loss = rep_outs[0][off // LANES, 0] * (0.5 / D_MODEL)
    result = [loss, grad_x.reshape(x.shape)]
    for k in range(4):
        result.extend(res[name][k] for name in WEIGHT_ORDER)
    return tuple(result)
```
